```python
import jax, jax.numpy as jnp
from jax import lax
import numpy as np

D_MODEL = 1024
BATCH = 8
SEQ = 8192
DEPTH = 4

HEAD_DIM = 64
SB_HEADS = D_MODEL // 256
FOX_HEADS = D_MODEL // 128
SGU_GROUPS = D_MODEL // 256
SGU_DIM = 64
SGU_CHUNK = 128
Q_BLOCK = 128
D_FF = 4 * D_MODEL
EPS = 1e-6

SB_W = SB_HEADS * HEAD_DIM
FOX_W = FOX_HEADS * HEAD_DIM
SGU_W = SGU_GROUPS * SGU_DIM
MIX_W = SB_W + FOX_W + SGU_W
IN_SIZES = [SB_W, SB_W, SB_W, FOX_W, FOX_W, FOX_W, FOX_HEADS, SGU_W, SGU_W]
IN_W = sum(IN_SIZES)
IN_SPLITS = [int(s) for s in np.cumsum(IN_SIZES)[:-1]]

kernel_name = "hybrid_sb_fox_sgu_adaln"


def rmsnorm(x, g):
    xf = x.astype(jnp.float32)
    y = xf * lax.rsqrt(jnp.mean(xf * xf, axis=-1, keepdims=True) + EPS)
    return (y * g).astype(x.dtype)


def modulate(h, shift, scale):
    return h * (1.0 + scale[:, None, :]) + shift[:, None, :]


def split_heads(t, n_heads):
    b, s, _ = t.shape
    return t.reshape(b, s, n_heads, -1).transpose(0, 2, 1, 3)


def merge_heads(t):
    b, h, s, d = t.shape
    return t.transpose(0, 2, 1, 3).reshape(b, s, h * d)


def to_blocks(t):
    b, h, s, d = t.shape
    return jnp.moveaxis(t.reshape(b, h, s // Q_BLOCK, Q_BLOCK, d), 2, 0)


def from_blocks(o):
    n, b, h, q, d = o.shape
    return jnp.moveaxis(o, 0, 2).reshape(b, h, n * q, d)


def stick_breaking_attention(q, k, v):
    s_len = k.shape[2]
    scale = HEAD_DIM ** -0.5
    s_pos = jnp.arange(s_len)

    def block(args):
        qb, i = args
        z = jnp.einsum('bhqd,bhsd->bhqs', qb, k).astype(jnp.float32) * scale
        t_pos = i * Q_BLOCK + jnp.arange(Q_BLOCK)
        mask = s_pos[None, :] < t_pos[:, None]
        log_1mb = jnp.where(mask, jax.nn.log_sigmoid(-z), 0.0)
        between = lax.cumsum(log_1mb, axis=3, reverse=True) - log_1mb
        a = jnp.where(mask, jnp.exp(jax.nn.log_sigmoid(z) + between), 0.0)
        return jnp.einsum('bhqs,bhsd->bhqd', a.astype(v.dtype), v)

    o = lax.map(block, (to_blocks(q), jnp.arange(s_len // Q_BLOCK)))
    return from_blocks(o)


def forgetting_attention(q, k, v, log_f):
    b, h, s_len, _ = k.shape
    n_blk = s_len // Q_BLOCK
    scale = HEAD_DIM ** -0.5
    cum_f = jnp.cumsum(log_f, axis=-1)
    cum_f_blocks = jnp.moveaxis(cum_f.reshape(b, h, n_blk, Q_BLOCK), 2, 0)
    s_pos = jnp.arange(s_len)

    def block(args):
        qb, fq, i = args
        z = jnp.einsum('bhqd,bhsd->bhqs', qb, k).astype(jnp.float32) * scale
        z = z + fq[..., :, None] - cum_f[:, :, None, :]
        t_pos = i * Q_BLOCK + jnp.arange(Q_BLOCK)
        mask = s_pos[None, :] <= t_pos[:, None]
        p = jax.nn.softmax(jnp.where(mask, z, -jnp.inf), axis=-1)
        return jnp.einsum('bhqs,bhsd->bhqd', p.astype(v.dtype), v)

    o = lax.map(block, (to_blocks(q), cum_f_blocks, jnp.arange(n_blk)))
    return from_blocks(o)


def spatial_gating(u, vv, norm_g, w_s, b_s):
    b, s_len, _ = u.shape
    n_chunk = s_len // SGU_CHUNK
    vv = rmsnorm(vv.reshape(b, s_len, SGU_GROUPS, SGU_DIM), norm_g)
    vv = vv.reshape(b, n_chunk, SGU_CHUNK, SGU_GROUPS, SGU_DIM)
    w = jnp.tril(w_s)
    mixed = jnp.einsum('gts,bnsgd->bntgd', w, vv) + b_s.T[None, None, :, :, None]
    return u * mixed.reshape(b, s_len, SGU_W)


def _fwd_setup_inputs(seed: int = 0) -> dict:
    key = jax.random.key(seed)
    ks = jax.random.split(key, 16)
    nrm = jax.random.normal
    f32 = jnp.float32
    return {
        "x": nrm(ks[0], (BATCH, SEQ, D_MODEL), f32),
        "c": nrm(ks[1], (BATCH, D_MODEL), f32),
        "ada_w": nrm(ks[2], (DEPTH, D_MODEL, 6 * D_MODEL), f32) * (0.5 * D_MODEL ** -0.5),
        "ada_b": nrm(ks[3], (DEPTH, 6 * D_MODEL), f32) * 0.02,
        "norm1_g": 1.0 + 0.01 * nrm(ks[4], (DEPTH, D_MODEL), f32),
        "norm2_g": 1.0 + 0.01 * nrm(ks[5], (DEPTH, D_MODEL), f32),
        "w_in": nrm(ks[6], (DEPTH, D_MODEL, IN_W), f32) * D_MODEL ** -0.5,
        "b_forget": 2.0 + 0.5 * nrm(ks[7], (DEPTH, FOX_HEADS), f32),
        "q_norm_g": 1.0 + 0.01 * nrm(ks[8], (DEPTH, HEAD_DIM), f32),
        "k_norm_g": 1.0 + 0.01 * nrm(ks[9], (DEPTH, HEAD_DIM), f32),
        "sgu_norm_g": 1.0 + 0.01 * nrm(ks[10], (DEPTH, SGU_GROUPS, SGU_DIM), f32),
        "sgu_w": nrm(ks[11], (DEPTH, SGU_GROUPS, SGU_CHUNK, SGU_CHUNK), f32) * SGU_CHUNK ** -0.5,
        "sgu_b": 1.0 + 0.1 * nrm(ks[12], (DEPTH, SGU_GROUPS, SGU_CHUNK), f32),
        "w_out": nrm(ks[13], (DEPTH, MIX_W, D_MODEL), f32) * MIX_W ** -0.5,
        "mlp_w1": nrm(ks[14], (DEPTH, D_MODEL, D_FF), f32) * D_MODEL ** -0.5,
        "mlp_w2": nrm(ks[15], (DEPTH, D_FF, D_MODEL), f32) * D_FF ** -0.5,
    }


def _fwd_reference(x, c, ada_w, ada_b, norm1_g, norm2_g, w_in, b_forget, q_norm_g, k_norm_g,
              sgu_norm_g, sgu_w, sgu_b, w_out, mlp_w1, mlp_w2):
    cond = jax.nn.silu(c)
    for l in range(DEPTH):
        mod = cond @ ada_w[l] + ada_b[l]
        sh1, sc1, g1, sh2, sc2, g2 = jnp.split(mod, 6, axis=-1)

        h = modulate(rmsnorm(x, norm1_g[l]), sh1, sc1)
        proj = h @ w_in[l]
        qa, ka, va, qb, kb, vb, fl, uc, vc = jnp.split(proj, IN_SPLITS, axis=-1)

        o_sb = stick_breaking_attention(split_heads(qa, SB_HEADS), split_heads(ka, SB_HEADS),
                                        split_heads(va, SB_HEADS))

        q_fox = rmsnorm(split_heads(qb, FOX_HEADS), q_norm_g[l])
        k_fox = rmsnorm(split_heads(kb, FOX_HEADS), k_norm_g[l])
        log_f = jax.nn.log_sigmoid(fl.astype(jnp.float32) + b_forget[l].astype(jnp.float32))
        o_fox = forgetting_attention(q_fox, k_fox, split_heads(vb, FOX_HEADS),
                                     log_f.transpose(0, 2, 1))

        o_sgu = spatial_gating(jax.nn.gelu(uc), jax.nn.gelu(vc), sgu_norm_g[l], sgu_w[l], sgu_b[l])

        mixed = jnp.concatenate([merge_heads(o_sb), merge_heads(o_fox), o_sgu], axis=-1)
        x = x + g1[:, None, :] * (mixed @ w_out[l])

        h = modulate(rmsnorm(x, norm2_g[l]), sh2, sc2)
        x = x + g2[:, None, :] * (jnp.square(jax.nn.relu(h @ mlp_w1[l])) @ mlp_w2[l])
    return x


import jax as _jax
import jax.numpy as _jnp

TWIN_FORMAT = 'train_step'
FWD_PARAMS = ['x', 'c', 'ada_w', 'ada_b', 'norm1_g', 'norm2_g', 'w_in', 'b_forget', 'q_norm_g', 'k_norm_g', 'sgu_norm_g', 'sgu_w', 'sgu_b', 'w_out', 'mlp_w1', 'mlp_w2']
TWIN_WEIGHTS = ['ada_w', 'ada_b', 'norm1_g', 'norm2_g', 'w_in', 'b_forget', 'q_norm_g', 'k_norm_g', 'sgu_norm_g', 'sgu_w', 'sgu_b', 'w_out', 'mlp_w1', 'mlp_w2']
TWIN_DIFF_INPUT = 'x'
TWIN_INPUTS = ['x', 'c', 'ada_w', 'ada_b', 'norm1_g', 'norm2_g', 'w_in', 'b_forget', 'q_norm_g', 'k_norm_g', 'sgu_norm_g', 'sgu_w', 'sgu_b', 'w_out', 'mlp_w1', 'mlp_w2', 'loss_target', 'm_ada_w', 'm_ada_b', 'm_norm1_g', 'm_norm2_g', 'm_w_in', 'm_b_forget', 'm_q_norm_g', 'm_k_norm_g', 'm_sgu_norm_g', 'm_sgu_w', 'm_sgu_b', 'm_w_out', 'm_mlp_w1', 'm_mlp_w2', 'v_ada_w', 'v_ada_b', 'v_norm1_g', 'v_norm2_g', 'v_w_in', 'v_b_forget', 'v_q_norm_g', 'v_k_norm_g', 'v_sgu_norm_g', 'v_sgu_w', 'v_sgu_b', 'v_w_out', 'v_mlp_w1', 'v_mlp_w2']
TWIN_OUTPUTS = ['loss', 'grad_x', 'grad_ada_w', 'grad_ada_b', 'grad_norm1_g', 'grad_norm2_g', 'grad_w_in', 'grad_b_forget', 'grad_q_norm_g', 'grad_k_norm_g', 'grad_sgu_norm_g', 'grad_sgu_w', 'grad_sgu_b', 'grad_w_out', 'grad_mlp_w1', 'grad_mlp_w2', 'delta_ada_w', 'delta_ada_b', 'delta_norm1_g', 'delta_norm2_g', 'delta_w_in', 'delta_b_forget', 'delta_q_norm_g', 'delta_k_norm_g', 'delta_sgu_norm_g', 'delta_sgu_w', 'delta_sgu_b', 'delta_w_out', 'delta_mlp_w1', 'delta_mlp_w2', 'new_m_ada_w', 'new_m_ada_b', 'new_m_norm1_g', 'new_m_norm2_g', 'new_m_w_in', 'new_m_b_forget', 'new_m_q_norm_g', 'new_m_k_norm_g', 'new_m_sgu_norm_g', 'new_m_sgu_w', 'new_m_sgu_b', 'new_m_w_out', 'new_m_mlp_w1', 'new_m_mlp_w2', 'new_v_ada_w', 'new_v_ada_b', 'new_v_norm1_g', 'new_v_norm2_g', 'new_v_w_in', 'new_v_b_forget', 'new_v_q_norm_g', 'new_v_k_norm_g', 'new_v_sgu_norm_g', 'new_v_sgu_w', 'new_v_sgu_b', 'new_v_w_out', 'new_v_mlp_w1', 'new_v_mlp_w2']
TWIN_LEAF_KINDS = {'loss': 'loss', 'grad_x': 'grad_x', 'grad_ada_w': 'grad_w', 'grad_ada_b': 'grad_w', 'grad_norm1_g': 'grad_w', 'grad_norm2_g': 'grad_w', 'grad_w_in': 'grad_w', 'grad_b_forget': 'grad_w', 'grad_q_norm_g': 'grad_w', 'grad_k_norm_g': 'grad_w', 'grad_sgu_norm_g': 'grad_w', 'grad_sgu_w': 'grad_w', 'grad_sgu_b': 'grad_w', 'grad_w_out': 'grad_w', 'grad_mlp_w1': 'grad_w', 'grad_mlp_w2': 'grad_w', 'delta_ada_w': 'delta_w', 'delta_ada_b': 'delta_w', 'delta_norm1_g': 'delta_w', 'delta_norm2_g': 'delta_w', 'delta_w_in': 'delta_w', 'delta_b_forget': 'delta_w', 'delta_q_norm_g': 'delta_w', 'delta_k_norm_g': 'delta_w', 'delta_sgu_norm_g': 'delta_w', 'delta_sgu_w': 'delta_w', 'delta_sgu_b': 'delta_w', 'delta_w_out': 'delta_w', 'delta_mlp_w1': 'delta_w', 'delta_mlp_w2': 'delta_w', 'new_m_ada_w': 'new_m', 'new_m_ada_b': 'new_m', 'new_m_norm1_g': 'new_m', 'new_m_norm2_g': 'new_m', 'new_m_w_in': 'new_m', 'new_m_b_forget': 'new_m', 'new_m_q_norm_g': 'new_m', 'new_m_k_norm_g': 'new_m', 'new_m_sgu_norm_g': 'new_m', 'new_m_sgu_w': 'new_m', 'new_m_sgu_b': 'new_m', 'new_m_w_out': 'new_m', 'new_m_mlp_w1': 'new_m', 'new_m_mlp_w2': 'new_m', 'new_v_ada_w': 'new_v', 'new_v_ada_b': 'new_v', 'new_v_norm1_g': 'new_v', 'new_v_norm2_g': 'new_v', 'new_v_w_in': 'new_v', 'new_v_b_forget': 'new_v', 'new_v_q_norm_g': 'new_v', 'new_v_k_norm_g': 'new_v', 'new_v_sgu_norm_g': 'new_v', 'new_v_sgu_w': 'new_v', 'new_v_sgu_b': 'new_v', 'new_v_w_out': 'new_v', 'new_v_mlp_w1': 'new_v', 'new_v_mlp_w2': 'new_v'}


def _forward(args):
    return _fwd_reference(*[args[k] for k in FWD_PARAMS])


def _output_shape():
    def fwd():
        inp = _fwd_setup_inputs(0)
        return _fwd_reference(*[inp[k] for k in FWD_PARAMS])
    out = _jax.eval_shape(fwd)
    return out.shape, out.dtype

N_MICROBATCH = 1
ADAM_LR = 0.001
ADAM_B1 = 0.9
ADAM_B2 = 0.999
ADAM_EPS = 1e-08
ADAM_WD = 0.01
ADAM_STEP = 10
PER_EXAMPLE_BATCH_AXIS = {'x': 0, 'c': 0, 'loss_target': 0}
SHARED_INPUTS = []
_WEIGHT_DTYPES = {'ada_w': _jnp.float32, 'ada_b': _jnp.float32, 'norm1_g': _jnp.float32, 'norm2_g': _jnp.float32, 'w_in': _jnp.float32, 'b_forget': _jnp.float32, 'q_norm_g': _jnp.float32, 'k_norm_g': _jnp.float32, 'sgu_norm_g': _jnp.float32, 'sgu_w': _jnp.float32, 'sgu_b': _jnp.float32, 'w_out': _jnp.float32, 'mlp_w1': _jnp.float32, 'mlp_w2': _jnp.float32}
MOMENT_SCALE = {'ada_w': 5.757663e+00, 'ada_b': 1.389147e+01, 'norm1_g': 3.153071e+00, 'norm2_g': 2.402808e+01, 'w_in': 8.973729e-01, 'b_forget': 2.029083e+01, 'q_norm_g': 2.468305e+00, 'k_norm_g': 2.469502e+00, 'sgu_norm_g': 1.556207e+00, 'sgu_w': 4.719101e-01, 'sgu_b': 1.558490e+00, 'w_out': 1.594074e+00, 'mlp_w1': 9.787009e-01, 'mlp_w2': 3.676972e+00}


def _to_microbatches(a, axis):
    t = _jnp.moveaxis(a, axis, 0)
    t = t.reshape((N_MICROBATCH, t.shape[0] // N_MICROBATCH) + t.shape[1:])
    return _jnp.moveaxis(t, 1, axis + 1)


def setup_inputs(seed: int = 0) -> dict:
    inp = _fwd_setup_inputs(seed)
    key = _jax.random.fold_in(_jax.random.key(seed), 7919)
    shape, _ = _output_shape()
    out = dict(inp)
    out["loss_target"] = _jax.random.normal(_jax.random.fold_in(key, 0), shape, _jnp.float32)
    for i, name in enumerate(TWIN_WEIGHTS):
        w = inp[name].astype(_jnp.float32)
        if MOMENT_SCALE is None:
            s = _jnp.sqrt(_jnp.mean(_jnp.square(w)) + 1e-30)
        else:
            s = MOMENT_SCALE[name]
        km, kv = _jax.random.split(_jax.random.fold_in(key, i + 1))
        out[name] = w
        out["m_" + name] = s * _jax.random.normal(km, w.shape, _jnp.float32)
        out["v_" + name] = (s * s) * _jax.random.uniform(kv, w.shape, _jnp.float32, 0.5, 1.5)
    if N_MICROBATCH > 1:
        for name, axis in PER_EXAMPLE_BATCH_AXIS.items():
            out[name] = _to_microbatches(out[name], axis)
    return {'x': out['x'], 'c': out['c'], 'ada_w': out['ada_w'], 'ada_b': out['ada_b'], 'norm1_g': out['norm1_g'], 'norm2_g': out['norm2_g'], 'w_in': out['w_in'], 'b_forget': out['b_forget'], 'q_norm_g': out['q_norm_g'], 'k_norm_g': out['k_norm_g'], 'sgu_norm_g': out['sgu_norm_g'], 'sgu_w': out['sgu_w'], 'sgu_b': out['sgu_b'], 'w_out': out['w_out'], 'mlp_w1': out['mlp_w1'], 'mlp_w2': out['mlp_w2'], 'loss_target': out['loss_target'], 'm_ada_w': out['m_ada_w'], 'm_ada_b': out['m_ada_b'], 'm_norm1_g': out['m_norm1_g'], 'm_norm2_g': out['m_norm2_g'], 'm_w_in': out['m_w_in'], 'm_b_forget': out['m_b_forget'], 'm_q_norm_g': out['m_q_norm_g'], 'm_k_norm_g': out['m_k_norm_g'], 'm_sgu_norm_g': out['m_sgu_norm_g'], 'm_sgu_w': out['m_sgu_w'], 'm_sgu_b': out['m_sgu_b'], 'm_w_out': out['m_w_out'], 'm_mlp_w1': out['m_mlp_w1'], 'm_mlp_w2': out['m_mlp_w2'], 'v_ada_w': out['v_ada_w'], 'v_ada_b': out['v_ada_b'], 'v_norm1_g': out['v_norm1_g'], 'v_norm2_g': out['v_norm2_g'], 'v_w_in': out['v_w_in'], 'v_b_forget': out['v_b_forget'], 'v_q_norm_g': out['v_q_norm_g'], 'v_k_norm_g': out['v_k_norm_g'], 'v_sgu_norm_g': out['v_sgu_norm_g'], 'v_sgu_w': out['v_sgu_w'], 'v_sgu_b': out['v_sgu_b'], 'v_w_out': out['v_w_out'], 'v_mlp_w1': out['v_mlp_w1'], 'v_mlp_w2': out['v_mlp_w2']}


def _loss(weights, diff, rest, loss_target):
    with _jax.named_scope("forward"):
        args = {**rest, TWIN_DIFF_INPUT: diff, **{k: w.astype(_WEIGHT_DTYPES[k]) for k, w in weights.items()}}
        y = _forward(args)
    with _jax.named_scope("loss_head"):
        err = _jnp.square(y.astype(_jnp.float32) - loss_target)
        return 0.5 * _jnp.sum(_jnp.mean(err, axis=-1)) if err.ndim else 0.5 * err


def _adamw(w, g, m, v):
    m = ADAM_B1 * m + (1.0 - ADAM_B1) * g
    v = ADAM_B2 * v + (1.0 - ADAM_B2) * _jnp.square(g)
    m_hat = m / (1.0 - ADAM_B1 ** ADAM_STEP)
    v_hat = v / (1.0 - ADAM_B2 ** ADAM_STEP)
    delta = -ADAM_LR * (m_hat / (_jnp.sqrt(v_hat) + ADAM_EPS) + ADAM_WD * w)
    return delta, m, v


def reference(x, c, ada_w, ada_b, norm1_g, norm2_g, w_in, b_forget, q_norm_g, k_norm_g, sgu_norm_g, sgu_w, sgu_b, w_out, mlp_w1, mlp_w2, loss_target, m_ada_w, m_ada_b, m_norm1_g, m_norm2_g, m_w_in, m_b_forget, m_q_norm_g, m_k_norm_g, m_sgu_norm_g, m_sgu_w, m_sgu_b, m_w_out, m_mlp_w1, m_mlp_w2, v_ada_w, v_ada_b, v_norm1_g, v_norm2_g, v_w_in, v_b_forget, v_q_norm_g, v_k_norm_g, v_sgu_norm_g, v_sgu_w, v_sgu_b, v_w_out, v_mlp_w1, v_mlp_w2):
    given = dict(x=x, c=c, ada_w=ada_w, ada_b=ada_b, norm1_g=norm1_g, norm2_g=norm2_g, w_in=w_in, b_forget=b_forget, q_norm_g=q_norm_g, k_norm_g=k_norm_g, sgu_norm_g=sgu_norm_g, sgu_w=sgu_w, sgu_b=sgu_b, w_out=w_out, mlp_w1=mlp_w1, mlp_w2=mlp_w2, loss_target=loss_target, m_ada_w=m_ada_w, m_ada_b=m_ada_b, m_norm1_g=m_norm1_g, m_norm2_g=m_norm2_g, m_w_in=m_w_in, m_b_forget=m_b_forget, m_q_norm_g=m_q_norm_g, m_k_norm_g=m_k_norm_g, m_sgu_norm_g=m_sgu_norm_g, m_sgu_w=m_sgu_w, m_sgu_b=m_sgu_b, m_w_out=m_w_out, m_mlp_w1=m_mlp_w1, m_mlp_w2=m_mlp_w2, v_ada_w=v_ada_w, v_ada_b=v_ada_b, v_norm1_g=v_norm1_g, v_norm2_g=v_norm2_g, v_w_in=v_w_in, v_b_forget=v_b_forget, v_q_norm_g=v_q_norm_g, v_k_norm_g=v_k_norm_g, v_sgu_norm_g=v_sgu_norm_g, v_sgu_w=v_sgu_w, v_sgu_b=v_sgu_b, v_w_out=v_w_out, v_mlp_w1=v_mlp_w1, v_mlp_w2=v_mlp_w2)
    weights = {n: given[n] for n in TWIN_WEIGHTS}
    shared = {n: given[n] for n in SHARED_INPUTS}
    per_example = {n: given[n] for n in ['x', 'c']}
    grad_fn = _jax.value_and_grad(_loss, argnums=(0, 1))

    def one_microbatch(ex, loss_target):
        ex = dict(ex)
        diff = ex.pop(TWIN_DIFF_INPUT)
        return grad_fn(weights, diff, {**shared, **ex}, loss_target)

    if N_MICROBATCH == 1:
        loss, (grad_w, grad_x) = one_microbatch(per_example, given["loss_target"])
    else:
        def body(carry, xs):
            loss_sum, grad_sum = carry
            l_k, (gw_k, gx_k) = one_microbatch(xs[0], xs[1])
            with _jax.named_scope("update"):
                return (loss_sum + l_k, _jax.tree.map(_jnp.add, grad_sum, gw_k)), gx_k

        init = (_jnp.zeros((), _jnp.float32), _jax.tree.map(_jnp.zeros_like, weights))
        (loss, grad_w), grad_x = _jax.lax.scan(body, init, (per_example, given["loss_target"]))
    with _jax.named_scope("update"):
        delta_w, new_m, new_v = {}, {}, {}
        for n in TWIN_WEIGHTS:
            delta_w[n], new_m[n], new_v[n] = _adamw(weights[n], grad_w[n], given["m_" + n], given["v_" + n])
    return (loss, grad_x, *[grad_w[n] for n in TWIN_WEIGHTS], *[delta_w[n] for n in TWIN_WEIGHTS],
            *[new_m[n] for n in TWIN_WEIGHTS], *[new_v[n] for n in TWIN_WEIGHTS])
```

```python
import functools
import math

import jax
import jax.numpy as jnp
from jax import lax
from jax.experimental import pallas as pl
from jax.experimental.pallas import tpu as pltpu

F32 = jnp.float32
BF16 = jnp.bfloat16

D_MODEL = 1024
DEPTH = 4
HEAD_DIM = 64
LANES = 128
D_FF = 4 * D_MODEL
EPS = 1e-6
SB_W, FOX_W, SGU_W = 256, 512, 256
FOX_HEADS = 8
SGU_CHUNK = 128
IN_W = 2824
ATT_W = 3 * SB_W + 3 * FOX_W
PROJ_W = 3072
CB_QA, CB_KA, CB_VA = 0, 2, 4
CB_QB, CB_KB, CB_VB = 6, 10, 14
CB_UC, CB_VC, CB_FL = 18, 20, 22
ATT_T = 256
VMEM_LIMIT = 56 * 2 ** 20

ADAM_LR, ADAM_B1, ADAM_B2, ADAM_EPS, ADAM_WD, ADAM_STEP = 0.001, 0.9, 0.999, 1e-08, 0.01, 10

MESH = pl.DeviceIdType.MESH


def _pcall(body, *, name, out_shape, grid=(), in_specs=None, out_specs=None, scratch_shapes=(),
           semantics=None):
    params = dict(vmem_limit_bytes=VMEM_LIMIT)
    if semantics is not None:
        params["dimension_semantics"] = semantics
    kwargs = {}
    if in_specs is not None:
        kwargs["in_specs"] = in_specs
    if out_specs is not None:
        kwargs["out_specs"] = out_specs
    return pl.pallas_call(body, name=name, out_shape=out_shape, grid=grid,
                          scratch_shapes=list(scratch_shapes),
                          compiler_params=pltpu.CompilerParams(**params), **kwargs)


def _dot(a, b):
    return jnp.dot(a, b, preferred_element_type=F32)


def _dot_nt(a, b):
    return lax.dot_general(a, b, (((1,), (1,)), ((), ())), preferred_element_type=F32)


def _dot_tn(a, b):
    return lax.dot_general(a, b, (((0,), (0,)), ((), ())), preferred_element_type=F32)


def _split2(x):
    hi = x.astype(BF16)
    lo = (x - hi.astype(F32)).astype(BF16)
    return hi, lo


def _ones_dot(x, ones_bf16):
    hi, lo = _split2(x)
    return _dot(hi, ones_bf16) + _dot(lo, ones_bf16)


def _rowwise(fn, fulls, vecs, out_dtypes, n_vec_out, *, name, tr):
    s, n = fulls[0].shape
    tr = min(tr, s)
    assert s % tr == 0, (name, s, tr)
    nf, nv, nfo = len(fulls), len(vecs), len(out_dtypes)

    def body(*refs):
        fi, vi = refs[:nf], refs[nf:nf + nv]
        fo, vo = refs[nf + nv:nf + nv + nfo], refs[nf + nv + nfo:]
        outs_f, outs_v = fn([r[...] for r in fi], [r[...] for r in vi])
        for r, o in zip(fo, outs_f):
            r[...] = o.astype(r.dtype)
        if n_vec_out:
            @pl.when(pl.program_id(0) == 0)
            def _():
                for r in vo:
                    r[...] = jnp.zeros_like(r)
            for r, o in zip(vo, outs_v):
                r[...] += o

    full_spec = pl.BlockSpec((tr, n), lambda i: (i, 0))
    vec_specs = [pl.BlockSpec(v.shape, lambda i: (0, 0)) for v in vecs]
    out_vec_spec = pl.BlockSpec((1, n), lambda i: (0, 0))
    out_shape = [jax.ShapeDtypeStruct((s, n), dt) for dt in out_dtypes]
    out_shape += [jax.ShapeDtypeStruct((1, n), F32)] * n_vec_out
    outs = _pcall(body, name=name, grid=(s // tr,),
                  in_specs=[full_spec] * nf + vec_specs,
                  out_specs=[full_spec] * nfo + [out_vec_spec] * n_vec_out,
                  out_shape=out_shape,
                  semantics=("arbitrary",) if n_vec_out else ("parallel",))(*fulls, *vecs)
    return outs[:nfo], outs[nfo:]


def _colsum(t):
    return jnp.sum(t, axis=0, keepdims=True)


def _rms_mod(x, g, sc, sh):
    r = lax.rsqrt(jnp.mean(x * x, axis=-1, keepdims=True) + EPS)
    return (x * r * g) * (1.0 + sc) + sh


def _norm_mod_fwd(x, g, sc, sh, *, name):
    def fn(f, v):
        return [_rms_mod(f[0], v[0], v[1], v[2])], []
    (h,), _ = _rowwise(fn, [x], [g, sc, sh], [BF16], 0, name=name, tr=512)
    return h


def _resid_norm_mod_fwd(x, m, gate, g, sc, sh, *, name):
    def fn(f, v):
        xn = f[0] + v[0] * f[1]
        return [xn, _rms_mod(xn, v[1], v[2], v[3])], []
    (xn, h), _ = _rowwise(fn, [x, m], [gate, g, sc, sh], [F32, BF16], 0, name=name, tr=512)
    return xn, h


def _norm_mod_bwd(x, dh, dres, g, sc, *, name):
    def fn(f, v):
        xv, dhv, dr = f
        gv, scv = v
        r = lax.rsqrt(jnp.mean(xv * xv, axis=-1, keepdims=True) + EPS)
        xh = xv * r
        dn = dhv * (1.0 + scv)
        dxh = dn * gv
        dx = dr + r * (dxh - xh * jnp.mean(dxh * xh, axis=-1, keepdims=True))
        return [dx], [_colsum(dn * xh), _colsum(dhv * (xh * gv)), _colsum(dhv)]
    (dx,), (dg, dsc, dsh) = _rowwise(fn, [x, dh, dres], [g, sc], [F32], 3, name=name, tr=256)
    return dx, dg, dsc, dsh


def _gate_bwd(dx, m, gate, *, name):
    def fn(f, v):
        return [f[0] * v[0]], [_colsum(f[0] * f[1])]
    (dm,), (dgate,) = _rowwise(fn, [dx, m], [gate], [BF16], 1, name=name, tr=512)
    return dm, dgate


def _relu2_fwd(a, *, name):
    def fn(f, v):
        r = jnp.maximum(f[0], 0.0)
        return [r * r], []
    (r,), _ = _rowwise(fn, [a], [], [BF16], 0, name=name, tr=128)
    return r


def _relu2_bwd(dr, a, *, name):
    def fn(f, v):
        return [f[0] * (2.0 * jnp.maximum(f[1], 0.0))], []
    (da,), _ = _rowwise(fn, [dr, a], [], [BF16], 0, name=name, tr=128)
    return da


def _loss_fwd_bwd(x, m, gate, target, *, name):
    n = x.shape[1]

    def fn(f, v):
        err = f[0] + v[0] * f[1] - f[2]
        return [err * (1.0 / n)], [_colsum(err * err)]
    (dy,), (sq,) = _rowwise(fn, [x, m, target], [gate], [F32], 1, name=name, tr=512)
    return sq, dy


def _matmul(a, b, *, name, ta=False, tb=False, out_dtype=F32, tm=512, tn=512, tk=512):
    m = a.shape[1] if ta else a.shape[0]
    k = a.shape[0] if ta else a.shape[1]
    n = b.shape[0] if tb else b.shape[1]
    assert k == (b.shape[1] if tb else b.shape[0])
    tm, tn, tk = min(tm, m), min(tn, n), min(tk, k)
    assert m % tm == 0 and n % tn == 0 and k % tk == 0, (name, m, n, k)
    nk = k // tk
    dims = (((0 if ta else 1,), (1 if tb else 0,)), ((), ()))

    def body(a_ref, b_ref, o_ref, acc_ref):
        kk = pl.program_id(2)

        @pl.when(kk == 0)
        def _():
            acc_ref[...] = jnp.zeros_like(acc_ref)
        acc_ref[...] += lax.dot_general(a_ref[...].astype(BF16), b_ref[...].astype(BF16), dims,
                                        preferred_element_type=F32)

        @pl.when(kk == nk - 1)
        def _():
            o_ref[...] = acc_ref[...].astype(o_ref.dtype)

    a_spec = (pl.BlockSpec((tk, tm), lambda i, j, kk: (kk, i)) if ta
              else pl.BlockSpec((tm, tk), lambda i, j, kk: (i, kk)))
    b_spec = (pl.BlockSpec((tn, tk), lambda i, j, kk: (j, kk)) if tb
              else pl.BlockSpec((tk, tn), lambda i, j, kk: (kk, j)))
    return _pcall(body, name=name, grid=(m // tm, n // tn, nk),
                  in_specs=[a_spec, b_spec],
                  out_specs=pl.BlockSpec((tm, tn), lambda i, j, kk: (i, j)),
                  out_shape=jax.ShapeDtypeStruct((m, n), out_dtype),
                  scratch_shapes=[pltpu.VMEM((tm, tn), F32)],
                  semantics=("parallel", "parallel", "arbitrary"))(a, b)


def _lane_masks():
    lane = lax.broadcasted_iota(jnp.int32, (1, LANES), 1)
    return [lane < HEAD_DIM, lane >= HEAD_DIM]


def _tri_iotas(t):
    r = lax.broadcasted_iota(jnp.int32, (t, t), 0)
    c = lax.broadcasted_iota(jnp.int32, (t, t), 1)
    return r, c


def _rows(j, t):
    return pl.ds(pl.multiple_of(j * t, t), t)


def _neg_softplus(z):
    e = jnp.exp(-jnp.abs(z))
    return -(jnp.maximum(z, 0.0) + jnp.log(1.0 + e)), e


def _sb_fwd(proj, *, name):
    s = proj.shape[0]
    t = min(ATT_T, s)
    scale = HEAD_DIM ** -0.5

    def body(q_ref, k_ref, v_ref, o_ref, l_ref):
        i = pl.program_id(1)
        hm = _lane_masks()
        q = q_ref[...] * scale
        qh = [jnp.where(mk, q, 0.0).astype(BF16) for mk in hm]
        r, c = _tri_iotas(t)
        later = (r > c).astype(BF16)
        causal = c < r

        def chunk(j, carry, masked):
            kb = k_ref[_rows(j, t), :].astype(BF16)
            vb = v_ref[_rows(j, t), :].astype(BF16)
            out = []
            for h in range(2):
                e_run, acc = carry[h]
                z = _dot_nt(qh[h], kb)
                l, _ = _neg_softplus(z)
                if masked:
                    l = jnp.where(causal, l, 0.0)
                between = _ones_dot(l, later) + e_run
                a = jnp.exp(z + l + between)
                if masked:
                    a = jnp.where(causal, a, 0.0)
                acc = acc + _dot(a.astype(BF16), vb)
                out.append((e_run + jnp.sum(l, axis=1, keepdims=True), acc))
            return tuple(out)

        init = tuple((jnp.zeros((t, 1), F32), jnp.zeros((t, LANES), F32)) for _ in range(2))
        carry = chunk(i, init, True)
        carry = lax.fori_loop(0, i, lambda n, cr: chunk(i - 1 - n, cr, False), carry)
        o_ref[...] = jnp.where(hm[0], carry[0][1], carry[1][1])
        l_ref[...] = jnp.where(hm[0], carry[0][0], carry[1][0])

    blk = lambda cb: pl.BlockSpec((t, LANES), lambda p, i: (i, cb + p))
    full = lambda cb: pl.BlockSpec((s, LANES), lambda p, i: (0, cb + p))
    out_blk = pl.BlockSpec((t, LANES), lambda p, i: (i, p))
    return _pcall(body, name=name, grid=(SB_W // LANES, s // t),
                  in_specs=[blk(CB_QA), full(CB_KA), full(CB_VA)],
                  out_specs=[out_blk, out_blk],
                  out_shape=[jax.ShapeDtypeStruct((s, SB_W), F32)] * 2,
                  semantics=("parallel", "arbitrary"))(proj, proj, proj)


def _sb_bwd(proj, dmixed, ltot, *, name):
    s = proj.shape[0]
    t = min(ATT_T, s)
    scale = HEAD_DIM ** -0.5

    def body(q_ref, k_ref, v_ref, do_ref, l_ref, dq_ref, dk_ref, dv_ref):
        i = pl.program_id(1)

        @pl.when(i == 0)
        def _():
            dk_ref[...] = jnp.zeros_like(dk_ref)
            dv_ref[...] = jnp.zeros_like(dv_ref)

        hm = _lane_masks()
        q = q_ref[...] * scale
        do = do_ref[...]
        qh = [jnp.where(mk, q, 0.0).astype(BF16) for mk in hm]
        doh = [jnp.where(mk, do, 0.0).astype(BF16) for mk in hm]
        ltv = l_ref[...]
        lt = [ltv[:, 0:1], ltv[:, HEAD_DIM:HEAD_DIM + 1]]
        r, c = _tri_iotas(t)
        upto = (r <= c).astype(BF16)
        before = (r < c).astype(BF16)
        causal = c < r

        def chunk(j, carry, masked):
            kf = k_ref[_rows(j, t), :]
            kb = kf.astype(BF16)
            vb = v_ref[_rows(j, t), :].astype(BF16)
            out = []
            dk_add = jnp.zeros((t, LANES), F32)
            dv_add = jnp.zeros((t, LANES), F32)
            for h in range(2):
                l_run, g_run, dq = carry[h]
                kh = jnp.where(hm[h], kf, 0.0).astype(BF16)
                z = _dot_nt(qh[h], kb)
                l, e = _neg_softplus(z)
                beta = jnp.where(z >= 0.0, 1.0, e) / (1.0 + e)
                if masked:
                    l = jnp.where(causal, l, 0.0)
                prefix = _ones_dot(l, upto) + l_run
                a = jnp.exp(z + l + (lt[h] - prefix))
                if masked:
                    a = jnp.where(causal, a, 0.0)
                g = a * _dot_nt(doh[h], vb)
                g_before = _ones_dot(g, before) + g_run
                dz = g * (1.0 - beta) - beta * g_before
                if masked:
                    dz = jnp.where(causal, dz, 0.0)
                dzb = dz.astype(BF16)
                dq = dq + _dot(dzb, kh)
                dk_add = dk_add + _dot_tn(dzb, qh[h])
                dv_add = dv_add + _dot_tn(a.astype(BF16), doh[h])
                out.append((l_run + jnp.sum(l, axis=1, keepdims=True),
                            g_run + jnp.sum(g, axis=1, keepdims=True), dq))
            dk_ref[_rows(j, t), :] += dk_add
            dv_ref[_rows(j, t), :] += dv_add
            return tuple(out)

        init = tuple((jnp.zeros((t, 1), F32), jnp.zeros((t, 1), F32), jnp.zeros((t, LANES), F32))
                     for _ in range(2))
        carry = lax.fori_loop(0, i, lambda j, cr: chunk(j, cr, False), init)
        carry = chunk(i, carry, True)
        dq_ref[...] = (carry[0][2] + carry[1][2]) * scale

    blk = lambda cb: pl.BlockSpec((t, LANES), lambda p, i: (i, cb + p))
    full = lambda cb: pl.BlockSpec((s, LANES), lambda p, i: (0, cb + p))
    out_blk = pl.BlockSpec((t, LANES), lambda p, i: (i, p))
    out_full = pl.BlockSpec((s, LANES), lambda p, i: (0, p))
    return _pcall(body, name=name, grid=(SB_W // LANES, s // t),
                  in_specs=[blk(CB_QA), full(CB_KA), full(CB_VA), blk(0), blk(0)],
                  out_specs=[out_blk, out_full, out_full],
                  out_shape=[jax.ShapeDtypeStruct((s, SB_W), F32)] * 3,
                  semantics=("parallel", "arbitrary"))(proj, proj, proj, dmixed, ltot)


def _group_mean(v, lo):
    s0 = jnp.sum(jnp.where(lo, v, 0.0), axis=1, keepdims=True)
    s1 = jnp.sum(jnp.where(lo, 0.0, v), axis=1, keepdims=True)
    return jnp.where(lo, s0, s1) * (1.0 / HEAD_DIM)


def _fox_prep_fwd(proj, qg, kg, *, name):
    s = proj.shape[0]
    tr = min(512, s)

    def body(q_ref, k_ref, qg_ref, kg_ref, qn_ref, kn_ref):
        lo = _lane_masks()[0]
        for x_ref, g_ref, o_ref in ((q_ref, qg_ref, qn_ref), (k_ref, kg_ref, kn_ref)):
            x = x_ref[...]
            o_ref[...] = x * lax.rsqrt(_group_mean(x * x, lo) + EPS) * g_ref[...]

    blk = lambda cb: pl.BlockSpec((tr, LANES), lambda p, i: (i, cb + p))
    vec = pl.BlockSpec((1, LANES), lambda p, i: (0, 0))
    out_blk = pl.BlockSpec((tr, LANES), lambda p, i: (i, p))
    return _pcall(body, name=name, grid=(FOX_W // LANES, s // tr),
                  in_specs=[blk(CB_QB), blk(CB_KB), vec, vec],
                  out_specs=[out_blk, out_blk],
                  out_shape=[jax.ShapeDtypeStruct((s, FOX_W), F32)] * 2,
                  semantics=("parallel", "parallel"))(proj, proj, qg, kg)


def _fox_prep_bwd(proj, dqn, dkn, qg, kg, *, name):
    s = proj.shape[0]
    tr = min(512, s)

    def body(q_ref, k_ref, dqn_ref, dkn_ref, qg_ref, kg_ref, dq_ref, dk_ref, dqg_ref, dkg_ref):
        @pl.when((pl.program_id(0) == 0) & (pl.program_id(1) == 0))
        def _():
            dqg_ref[...] = jnp.zeros_like(dqg_ref)
            dkg_ref[...] = jnp.zeros_like(dkg_ref)

        lo = _lane_masks()[0]
        for x_ref, dy_ref, g_ref, dx_ref, dg_ref in ((q_ref, dqn_ref, qg_ref, dq_ref, dqg_ref),
                                                     (k_ref, dkn_ref, kg_ref, dk_ref, dkg_ref)):
            x, dy = x_ref[...], dy_ref[...]
            r = lax.rsqrt(_group_mean(x * x, lo) + EPS)
            xh = x * r
            dxh = dy * g_ref[...]
            dx_ref[...] = r * (dxh - xh * _group_mean(dxh * xh, lo))
            dg_ref[...] += _colsum(dy * xh)

    blk = lambda cb: pl.BlockSpec((tr, LANES), lambda p, i: (i, cb + p))
    vec = pl.BlockSpec((1, LANES), lambda p, i: (0, 0))
    out_blk = pl.BlockSpec((tr, LANES), lambda p, i: (i, p))
    return _pcall(body, name=name, grid=(FOX_W // LANES, s // tr),
                  in_specs=[blk(CB_QB), blk(CB_KB), out_blk, out_blk, vec, vec],
                  out_specs=[out_blk, out_blk, vec, vec],
                  out_shape=[jax.ShapeDtypeStruct((s, FOX_W), F32)] * 2
                  + [jax.ShapeDtypeStruct((1, LANES), F32)] * 2,
                  semantics=("arbitrary", "arbitrary"))(proj, proj, dqn, dkn, qg, kg)


def _split3_dot(tri_bf16, x):
    hi = x.astype(BF16)
    r1 = x - hi.astype(F32)
    mid = r1.astype(BF16)
    lo = (r1 - mid.astype(F32)).astype(BF16)
    return _dot(tri_bf16, hi) + _dot(tri_bf16, mid) + _dot(tri_bf16, lo)


def _forget_cumsum_fwd(proj, b_pad, *, name):
    s = proj.shape[0]
    tb = min(256, s)

    def body(fl_ref, b_ref, cf_ref, run_ref):
        @pl.when(pl.program_id(0) == 0)
        def _():
            run_ref[...] = jnp.zeros_like(run_ref)
        lf, _ = _neg_softplus(-(fl_ref[...] + b_ref[...]))
        r, c = _tri_iotas(tb)
        incl = _split3_dot((c <= r).astype(BF16), lf) + run_ref[...]
        cf_ref[...] = incl
        run_ref[...] = incl[tb - 1:tb, :]

    return _pcall(body, name=name, grid=(s // tb,),
                  in_specs=[pl.BlockSpec((tb, LANES), lambda i: (i, CB_FL)),
                            pl.BlockSpec((1, LANES), lambda i: (0, 0))],
                  out_specs=pl.BlockSpec((tb, LANES), lambda i: (i, 0)),
                  out_shape=jax.ShapeDtypeStruct((s, LANES), F32),
                  scratch_shapes=[pltpu.VMEM((1, LANES), F32)],
                  semantics=("arbitrary",))(proj, b_pad)


def _forget_cumsum_bwd(proj, b_pad, dcf, *, name):
    s = proj.shape[0]
    tb = min(256, s)
    nb = s // tb

    def body(fl_ref, b_ref, dcf_ref, dfl_ref, db_ref, run_ref):
        @pl.when(pl.program_id(0) == 0)
        def _():
            run_ref[...] = jnp.zeros_like(run_ref)
            db_ref[...] = jnp.zeros_like(db_ref)
        r, c = _tri_iotas(tb)
        dlf = _split3_dot((c >= r).astype(BF16), dcf_ref[...]) + run_ref[...]
        run_ref[...] = dlf[0:1, :]
        xv = fl_ref[...] + b_ref[...]
        e = jnp.exp(-jnp.abs(xv))
        sig_neg = jnp.where(xv >= 0.0, e, 1.0) / (1.0 + e)
        dfl = dlf * sig_neg
        dfl_ref[...] = dfl
        db_ref[...] += _colsum(dfl)

    return _pcall(body, name=name, grid=(nb,),
                  in_specs=[pl.BlockSpec((tb, LANES), lambda i: (nb - 1 - i, CB_FL)),
                            pl.BlockSpec((1, LANES), lambda i: (0, 0)),
                            pl.BlockSpec((tb, LANES), lambda i: (nb - 1 - i, 0))],
                  out_specs=[pl.BlockSpec((tb, LANES), lambda i: (nb - 1 - i, 0)),
                             pl.BlockSpec((1, LANES), lambda i: (0, 0))],
                  out_shape=[jax.ShapeDtypeStruct((s, LANES), F32),
                             jax.ShapeDtypeStruct((1, LANES), F32)],
                  scratch_shapes=[pltpu.VMEM((1, LANES), F32)],
                  semantics=("arbitrary",))(proj, b_pad, dcf)


def _fox_bias_q(cfc, p, h):
    lane = lax.broadcasted_iota(jnp.int32, (1, LANES), 1)
    return jnp.sum(jnp.where(lane == 2 * p + h, cfc, 0.0), axis=1, keepdims=True)


def _fox_fwd(proj, qn, kn, cf, cf_rows, *, name):
    s = proj.shape[0]
    t = min(ATT_T, s)
    scale = HEAD_DIM ** -0.5

    def body(q_ref, k_ref, v_ref, cfc_ref, cfr_ref, o_ref, lse_ref):
        p, i = pl.program_id(0), pl.program_id(1)
        hm = _lane_masks()
        q = q_ref[...] * scale
        qh = [jnp.where(mk, q, 0.0).astype(BF16) for mk in hm]
        cfc = cfc_ref[...]
        bq = [_fox_bias_q(cfc, p, h) for h in range(2)]
        r, c = _tri_iotas(t)
        causal = c <= r

        def chunk(j, carry, masked):
            kb = k_ref[_rows(j, t), :].astype(BF16)
            vb = v_ref[_rows(j, t), :].astype(BF16)
            out = []
            for h in range(2):
                m_run, l_run, acc = carry[h]
                z = _dot_nt(qh[h], kb) + (bq[h] - cfr_ref[0, pl.ds(h, 1), _rows(j, t)])
                if masked:
                    z = jnp.where(causal, z, -1e30)
                m_new = jnp.maximum(m_run, jnp.max(z, axis=1, keepdims=True))
                alpha = jnp.exp(m_run - m_new)
                pr = jnp.exp(z - m_new)
                out.append((m_new, alpha * l_run + jnp.sum(pr, axis=1, keepdims=True),
                            alpha * acc + _dot(pr.astype(BF16), vb)))
            return tuple(out)

        init = tuple((jnp.full((t, 1), -1e30, F32), jnp.zeros((t, 1), F32),
                      jnp.zeros((t, LANES), F32)) for _ in range(2))
        carry = chunk(i, init, True)
        carry = lax.fori_loop(0, i, lambda j, cr: chunk(j, cr, False), carry)
        o_ref[...] = jnp.where(hm[0], carry[0][2] / carry[0][1], carry[1][2] / carry[1][1])
        lse_ref[...] = jnp.where(hm[0], carry[0][0] + jnp.log(carry[0][1]),
                                 carry[1][0] + jnp.log(carry[1][1]))

    blk = pl.BlockSpec((t, LANES), lambda p, i: (i, p))
    full = pl.BlockSpec((s, LANES), lambda p, i: (0, p))
    return _pcall(body, name=name, grid=(FOX_W // LANES, s // t),
                  in_specs=[blk, full, pl.BlockSpec((s, LANES), lambda p, i: (0, CB_VB + p)),
                            pl.BlockSpec((t, LANES), lambda p, i: (i, 0)),
                            pl.BlockSpec((1, 2, s), lambda p, i: (p, 0, 0))],
                  out_specs=[blk, blk],
                  out_shape=[jax.ShapeDtypeStruct((s, FOX_W), F32)] * 2,
                  semantics=("parallel", "arbitrary"))(qn, kn, proj, cf, cf_rows)


def _fox_bwd(proj, qn, kn, cf, cf_rows, do, o, lse, *, name):
    s = proj.shape[0]
    t = min(ATT_T, s)
    scale = HEAD_DIM ** -0.5

    def body(q_ref, k_ref, v_ref, cfc_ref, cfr_ref, do_ref, o_ref, lse_ref,
             dq_ref, dk_ref, dv_ref, dcf_ref, dcfq_ref):
        p, i = pl.program_id(0), pl.program_id(1)

        @pl.when(i == 0)
        def _():
            dk_ref[...] = jnp.zeros_like(dk_ref)
            dv_ref[...] = jnp.zeros_like(dv_ref)
            dcf_ref[...] = jnp.zeros_like(dcf_ref)

        hm = _lane_masks()
        q = q_ref[...] * scale
        do = do_ref[...]
        dov = do * o_ref[...]
        qh = [jnp.where(mk, q, 0.0).astype(BF16) for mk in hm]
        doh = [jnp.where(mk, do, 0.0).astype(BF16) for mk in hm]
        delta = [jnp.sum(jnp.where(mk, dov, 0.0), axis=1, keepdims=True) for mk in hm]
        lsev = lse_ref[...]
        lse = [lsev[:, 0:1], lsev[:, HEAD_DIM:HEAD_DIM + 1]]
        cfc = cfc_ref[...]
        bq = [_fox_bias_q(cfc, p, h) - lse[h] for h in range(2)]
        r, c = _tri_iotas(t)
        causal = c <= r

        def chunk(j, carry, masked):
            kf = k_ref[_rows(j, t), :]
            kb = kf.astype(BF16)
            vb = v_ref[_rows(j, t), :].astype(BF16)
            out = []
            dk_add = jnp.zeros((t, LANES), F32)
            dv_add = jnp.zeros((t, LANES), F32)
            for h in range(2):
                kh = jnp.where(hm[h], kf, 0.0).astype(BF16)
                z = _dot_nt(qh[h], kb) + (bq[h] - cfr_ref[0, pl.ds(h, 1), _rows(j, t)])
                pr = jnp.exp(z)
                if masked:
                    pr = jnp.where(causal, pr, 0.0)
                ds = pr * (_dot_nt(doh[h], vb) - delta[h])
                dsb = ds.astype(BF16)
                out.append((carry[h][0] + _dot(dsb, kh),
                            carry[h][1] + jnp.sum(ds, axis=1, keepdims=True)))
                dk_add = dk_add + _dot_tn(dsb, qh[h])
                dv_add = dv_add + _dot_tn(pr.astype(BF16), doh[h])
                dcf_ref[0, pl.ds(h, 1), _rows(j, t)] -= jnp.sum(ds, axis=0, keepdims=True)
            dk_ref[_rows(j, t), :] += dk_add
            dv_ref[_rows(j, t), :] += dv_add
            return tuple(out)

        init = tuple((jnp.zeros((t, LANES), F32), jnp.zeros((t, 1), F32)) for _ in range(2))
        carry = lax.fori_loop(0, i, lambda j, cr: chunk(j, cr, False), init)
        carry = chunk(i, carry, True)
        dq_ref[...] = (carry[0][0] + carry[1][0]) * scale
        dcfq_ref[...] = jnp.where(hm[0], carry[0][1], carry[1][1])

    blk = pl.BlockSpec((t, LANES), lambda p, i: (i, p))
    full = pl.BlockSpec((s, LANES), lambda p, i: (0, p))
    rows = pl.BlockSpec((1, 2, s), lambda p, i: (p, 0, 0))
    return _pcall(body, name=name, grid=(FOX_W // LANES, s // t),
                  in_specs=[blk, full, pl.BlockSpec((s, LANES), lambda p, i: (0, CB_VB + p)),
                            pl.BlockSpec((t, LANES), lambda p, i: (i, 0)), rows,
                            pl.BlockSpec((t, LANES), lambda p, i: (i, SB_W // LANES + p)),
                            blk, blk],
                  out_specs=[blk, full, full, rows, blk],
                  out_shape=[jax.ShapeDtypeStruct((s, FOX_W), F32)] * 3
                  + [jax.ShapeDtypeStruct((FOX_W // LANES, 2, s), F32),
                     jax.ShapeDtypeStruct((s, FOX_W), F32)],
                  semantics=("parallel", "arbitrary"))(qn, kn, proj, cf, cf_rows, do, o, lse)


_GELU_C0 = math.sqrt(2.0 / math.pi)
_GELU_C1 = 0.044715


def _gelu(x):
    th = jnp.tanh(_GELU_C0 * (x + _GELU_C1 * (x * x * x)))
    return 0.5 * x * (1.0 + th), th


def _gelu_grad(x, th):
    return 0.5 * (1.0 + th) + 0.5 * x * (1.0 - th * th) * (_GELU_C0 * (1.0 + 3.0 * _GELU_C1 * x * x))


def _sgu_mix(wm, vn_c, lo, bcol):
    return jnp.where(lo, _dot(wm[0], vn_c) + bcol[0], _dot(wm[1], vn_c) + bcol[1])


def _sgu_fwd(proj, w, b_cols, gn, *, name):
    s = proj.shape[0]
    tr = min(512, s)
    ch = SGU_CHUNK

    def body(u_ref, v_ref, w_ref, b_ref, gn_ref, o_ref):
        lo = _lane_masks()[0]
        r, c = _tri_iotas(ch)
        wm = [jnp.where(c <= r, w_ref[h], 0.0).astype(BF16) for h in range(2)]
        bcol = [b_ref[0, :, h:h + 1] for h in range(2)]
        for n in range(tr // ch):
            rows = slice(n * ch, (n + 1) * ch)
            u, _ = _gelu(u_ref[rows, :])
            vg, _ = _gelu(v_ref[rows, :])
            vn = vg * lax.rsqrt(_group_mean(vg * vg, lo) + EPS) * gn_ref[0]
            o_ref[rows, :] = u * _sgu_mix(wm, vn.astype(BF16), lo, bcol)

    blk = lambda cb: pl.BlockSpec((tr, LANES), lambda p, i: (i, cb + p))
    return _pcall(body, name=name, grid=(SGU_W // LANES, s // tr),
                  in_specs=[blk(CB_UC), blk(CB_VC),
                            pl.BlockSpec((2, ch, ch), lambda p, i: (p, 0, 0)),
                            pl.BlockSpec((1, ch, 2), lambda p, i: (p, 0, 0)),
                            pl.BlockSpec((1, 1, LANES), lambda p, i: (p, 0, 0))],
                  out_specs=pl.BlockSpec((tr, LANES), lambda p, i: (i, p)),
                  out_shape=jax.ShapeDtypeStruct((s, SGU_W), F32),
                  semantics=("parallel", "parallel"))(proj, proj, w, b_cols, gn)


def _sgu_bwd(proj, dmixed, w, w_t, b_cols, gn, *, name):
    s = proj.shape[0]
    tr = min(512, s)
    ch = SGU_CHUNK
    cb_do = (SB_W + FOX_W) // LANES

    def body(u_ref, v_ref, do_ref, w_ref, wt_ref, b_ref, gn_ref,
             du_ref, dv_ref, dw_ref, db_ref, dgn_ref):
        @pl.when(pl.program_id(1) == 0)
        def _():
            dw_ref[...] = jnp.zeros_like(dw_ref)
            db_ref[...] = jnp.zeros_like(db_ref)
            dgn_ref[...] = jnp.zeros_like(dgn_ref)

        hm = _lane_masks()
        lo = hm[0]
        r, c = _tri_iotas(ch)
        wm = [jnp.where(c <= r, w_ref[h], 0.0).astype(BF16) for h in range(2)]
        wtm = [jnp.where(r <= c, wt_ref[h], 0.0).astype(BF16) for h in range(2)]
        bcol = [b_ref[0, :, h:h + 1] for h in range(2)]
        gnv = gn_ref[0]
        for n in range(tr // ch):
            rows = slice(n * ch, (n + 1) * ch)
            uc, vc, do = u_ref[rows, :], v_ref[rows, :], do_ref[rows, :]
            u, thu = _gelu(uc)
            vg, thv = _gelu(vc)
            rinv = lax.rsqrt(_group_mean(vg * vg, lo) + EPS)
            xh = vg * rinv
            vnb = (xh * gnv).astype(BF16)
            mix = _sgu_mix(wm, vnb, lo, bcol)
            du_ref[rows, :] = do * mix * _gelu_grad(uc, thu)
            dm = do * u
            dmb = dm.astype(BF16)
            dvn = jnp.where(lo, _dot(wtm[0], dmb), _dot(wtm[1], dmb))
            for h in range(2):
                dmh = jnp.where(hm[h], dm, 0.0)
                dw_ref[h] += jnp.where(c <= r, _dot_nt(dmh.astype(BF16), vnb), 0.0)
                db_ref[0, :, h:h + 1] += jnp.sum(dmh, axis=1, keepdims=True)
            dgn_ref[0] += _colsum(dvn * xh)
            dxh = dvn * gnv
            dvg = rinv * (dxh - xh * _group_mean(dxh * xh, lo))
            dv_ref[rows, :] = dvg * _gelu_grad(vc, thv)

    blk = lambda cb: pl.BlockSpec((tr, LANES), lambda p, i: (i, cb + p))
    w_spec = pl.BlockSpec((2, ch, ch), lambda p, i: (p, 0, 0))
    b_spec = pl.BlockSpec((1, ch, 2), lambda p, i: (p, 0, 0))
    g_spec = pl.BlockSpec((1, 1, LANES), lambda p, i: (p, 0, 0))
    out_blk = pl.BlockSpec((tr, LANES), lambda p, i: (i, p))
    return _pcall(body, name=name, grid=(SGU_W // LANES, s // tr),
                  in_specs=[blk(CB_UC), blk(CB_VC), blk(cb_do), w_spec, w_spec, b_spec, g_spec],
                  out_specs=[out_blk, out_blk, w_spec, b_spec, g_spec],
                  out_shape=[jax.ShapeDtypeStruct((s, SGU_W), F32)] * 2
                  + [jax.ShapeDtypeStruct(w.shape, F32), jax.ShapeDtypeStruct(b_cols.shape, F32),
                     jax.ShapeDtypeStruct(gn.shape, F32)],
                  semantics=("parallel", "arbitrary"))(proj, proj, dmixed, w, w_t, b_cols, gn)


def _pad_lanes(v):
    return jnp.zeros((1, LANES), F32).at[0, :v.shape[0]].set(v)


def _small_views(sm):
    return dict(
        n1=sm["norm1_g"][None, :], n2=sm["norm2_g"][None, :],
        b_pad=_pad_lanes(sm["b_forget"]),
        qg=jnp.tile(sm["q_norm_g"], 2)[None, :], kg=jnp.tile(sm["k_norm_g"], 2)[None, :],
        gn=sm["sgu_norm_g"].reshape(2, 1, LANES),
        w=sm["sgu_w"], w_t=jnp.swapaxes(sm["sgu_w"], 1, 2),
        b_cols=sm["sgu_b"].reshape(2, 2, SGU_CHUNK).transpose(0, 2, 1))


def _cf_rows(cf):
    return cf[:, :FOX_HEADS].T.reshape(FOX_W // LANES, 2, cf.shape[0])


def _layer_fwd(x_in, prev, mod, wts, sm, l):
    sh1, sc1, g1, sh2, sc2, g2 = mod
    v = _small_views(sm)
    if prev is None:
        x0 = x_in
        h1 = _norm_mod_fwd(x0, v["n1"], sc1, sh1, name=f"l{l}_norm1")
    else:
        x0, h1 = _resid_norm_mod_fwd(x_in, prev[0], prev[1], v["n1"], sc1, sh1, name=f"l{l}_norm1")
    proj = _matmul(h1, wts["w_in"], name=f"l{l}_proj")
    o_sb, ltot = _sb_fwd(proj, name=f"l{l}_sb_fwd")
    qn, kn = _fox_prep_fwd(proj, v["qg"], v["kg"], name=f"l{l}_fox_prep")
    cf = _forget_cumsum_fwd(proj, v["b_pad"], name=f"l{l}_cumf")
    cfr = _cf_rows(cf)
    o_fox, lse = _fox_fwd(proj, qn, kn, cf, cfr, name=f"l{l}_fox_fwd")
    o_sgu = _sgu_fwd(proj, v["w"], v["b_cols"], v["gn"], name=f"l{l}_sgu_fwd")
    mixed = jnp.concatenate([o_sb, o_fox, o_sgu], axis=1).astype(BF16)
    mo = _matmul(mixed, wts["w_out"], name=f"l{l}_wout")
    x1, h2 = _resid_norm_mod_fwd(x0, mo, g1, v["n2"], sc2, sh2, name=f"l{l}_norm2")
    a = _matmul(h2, wts["w1"], name=f"l{l}_mlp1", out_dtype=BF16)
    rr = _relu2_fwd(a, name=f"l{l}_relu2")
    m2 = _matmul(rr, wts["w2"], name=f"l{l}_mlp2")
    saved = dict(x0=x0, h1=h1, proj=proj, ltot=ltot, qn=qn, kn=kn, cf=cf, cfr=cfr, o_fox=o_fox,
                 lse=lse, mixed=mixed, mo=mo, x1=x1, h2=h2, a=a, rr=rr, m2=m2)
    return saved


def _layer_bwd(dx2, sv, mod, wts, sm, l):
    sh1, sc1, g1, sh2, sc2, g2 = mod
    v = _small_views(sm)
    dm2, dg2 = _gate_bwd(dx2, sv["m2"], g2, name=f"l{l}_gate2_bwd")
    dw2 = _matmul(sv["rr"], dm2, ta=True, name=f"l{l}_dw2")
    dr = _matmul(dm2, wts["w2"], tb=True, name=f"l{l}_dr", out_dtype=BF16)
    da = _relu2_bwd(dr, sv["a"], name=f"l{l}_relu2_bwd")
    dw1 = _matmul(sv["h2"], da, ta=True, name=f"l{l}_dw1")
    dh2 = _matmul(da, wts["w1"], tb=True, name=f"l{l}_dh2")
    dx1, dn2, dsc2, dsh2 = _norm_mod_bwd(sv["x1"], dh2, dx2, v["n2"], sc2, name=f"l{l}_norm2_bwd")
    dmo, dg1 = _gate_bwd(dx1, sv["mo"], g1, name=f"l{l}_gate1_bwd")
    dwo = _matmul(sv["mixed"], dmo, ta=True, name=f"l{l}_dwout")
    dmixed = _matmul(dmo, wts["w_out"], tb=True, name=f"l{l}_dmixed")
    proj = sv["proj"]
    dqa, dka, dva = _sb_bwd(proj, dmixed, sv["ltot"], name=f"l{l}_sb_bwd")
    dqn, dkn, dvb, dcfr, dcfq = _fox_bwd(proj, sv["qn"], sv["kn"], sv["cf"], sv["cfr"], dmixed,
                                   sv["o_fox"], sv["lse"], name=f"l{l}_fox_bwd")
    dqb, dkb, dqg, dkg = _fox_prep_bwd(proj, dqn, dkn, v["qg"], v["kg"], name=f"l{l}_fox_prep_bwd")
    s = proj.shape[0]
    dcf_heads = dcfr.reshape(FOX_HEADS, s).T + dcfq.reshape(s, FOX_HEADS, HEAD_DIM)[:, :, 0]
    dcf = jnp.zeros((s, LANES), F32).at[:, :FOX_HEADS].set(dcf_heads)
    dfl, dbf = _forget_cumsum_bwd(proj, v["b_pad"], dcf, name=f"l{l}_cumf_bwd")
    duc, dvc, dsw, dsb_cols, dgn = _sgu_bwd(proj, dmixed, v["w"], v["w_t"], v["b_cols"], v["gn"],
                                            name=f"l{l}_sgu_bwd")
    dproj = jnp.concatenate([dqa, dka, dva, dqb, dkb, dvb, duc, dvc, dfl,
                             jnp.zeros((s, LANES), F32)], axis=1).astype(BF16)
    dwin = _matmul(sv["h1"], dproj, ta=True, name=f"l{l}_dwin")
    dh1 = _matmul(dproj, wts["w_in"], tb=True, name=f"l{l}_dh1")
    dx0, dn1, dsc1, dsh1 = _norm_mod_bwd(sv["x0"], dh1, dx1, v["n1"], sc1, name=f"l{l}_norm1_bwd")
    big = dict(w_in=dwin, w_out=dwo, w1=dw1, w2=dw2)
    small = dict(norm1_g=dn1[0], norm2_g=dn2[0], b_forget=dbf[0, :FOX_HEADS],
                 q_norm_g=dqg[0, :HEAD_DIM] + dqg[0, HEAD_DIM:],
                 k_norm_g=dkg[0, :HEAD_DIM] + dkg[0, HEAD_DIM:],
                 sgu_norm_g=dgn.reshape(4, HEAD_DIM), sgu_w=dsw,
                 sgu_b=dsb_cols.transpose(0, 2, 1).reshape(4, SGU_CHUNK))
    dmod = jnp.concatenate([dsh1, dsc1, dg1, dsh2, dsc2, dg2], axis=1)
    return dx0, big, small, dmod


def _w_in_to_internal(w):
    pad = jnp.zeros((w.shape[0], PROJ_W - IN_W), w.dtype)
    return jnp.concatenate([w[:, :ATT_W], w[:, ATT_W + FOX_HEADS:], w[:, ATT_W:ATT_W + FOX_HEADS],
                            pad], axis=1)


def _w_in_from_internal(g):
    n_gate = SGU_W * 2
    return jnp.concatenate([g[:, :ATT_W], g[:, ATT_W + n_gate:ATT_W + n_gate + FOX_HEADS],
                            g[:, ATT_W:ATT_W + n_gate]], axis=1)


def _exchange(x, masks, slot_shift, slot_bits, scatter, *, name):
    n_slots = 2 ** slot_bits
    blk_shape = x.shape[1:] if scatter else x.shape
    n_peers = len(masks)

    def body(x_ref, out_ref, send_sems, recv_sems, local_sem):
        ids = (lax.axis_index("x"), lax.axis_index("y"), lax.axis_index("c"))
        me = 4 * ids[0] + 2 * ids[1] + ids[2]
        my_slot = (me >> slot_shift) & (n_slots - 1)

        def peer(mask):
            return tuple(1 - v if (mask >> b) & 1 else v for v, b in zip(ids, (2, 1, 0)))

        def src_for(slot):
            return x_ref.at[slot] if scatter else x_ref

        mine = pltpu.make_async_copy(src_for(my_slot), out_ref.at[my_slot], local_sem)
        mine.start()
        copies = []
        for kk, mask in enumerate(masks):
            peer_slot = ((me ^ mask) >> slot_shift) & (n_slots - 1)
            copies.append(pltpu.make_async_remote_copy(
                src_ref=src_for(peer_slot), dst_ref=out_ref.at[my_slot],
                send_sem=send_sems.at[kk], recv_sem=recv_sems.at[kk],
                device_id=peer(mask), device_id_type=MESH))
        for cp in copies:
            cp.start()
        for cp in copies:
            cp.wait()
        mine.wait()

    any_spec = pl.BlockSpec(memory_space=pl.ANY)
    return _pcall(body, name=name, in_specs=[any_spec], out_specs=any_spec,
                  out_shape=jax.ShapeDtypeStruct((n_slots,) + tuple(blk_shape), x.dtype),
                  scratch_shapes=[pltpu.SemaphoreType.DMA((n_peers,)),
                                  pltpu.SemaphoreType.DMA((n_peers,)),
                                  pltpu.SemaphoreType.DMA(())])(x)


def _gather_chips(x, *, name):
    return _exchange(x, (2, 4, 6), 1, 2, False, name=name)


def _gather_all(x, *, name):
    return _exchange(x, (1, 2, 3, 4, 5, 6, 7), 0, 3, False, name=name)


def _scatter_chips(x4, *, name):
    return _exchange(x4, (2, 4, 6), 1, 2, True, name=name)


def _swap_cores(x, *, name):
    return _exchange(x, (1,), 0, 1, False, name=name)


def _sum_slots(parts, *, name, tr=256):
    n, rows, cols = parts.shape
    tr = min(tr, rows)
    assert rows % tr == 0, (name, rows, tr)

    def body(p_ref, o_ref):
        acc = p_ref[0]
        for kk in range(1, n):
            acc = acc + p_ref[kk]
        o_ref[...] = acc

    return _pcall(body, name=name, grid=(rows // tr,),
                  in_specs=[pl.BlockSpec((n, tr, cols), lambda i: (0, i, 0))],
                  out_specs=pl.BlockSpec((tr, cols), lambda i: (i, 0)),
                  out_shape=jax.ShapeDtypeStruct((rows, cols), F32),
                  semantics=("parallel",))(parts)


def _adamw(w, m, v, parts, *, name, tr=256):
    n, rows, cols = parts.shape
    tr = min(tr, rows)
    assert rows % tr == 0, (name, rows, tr)
    c1 = 1.0 - ADAM_B1 ** ADAM_STEP
    c2 = 1.0 - ADAM_B2 ** ADAM_STEP

    def body(w_ref, m_ref, v_ref, p_ref, g_ref, d_ref, nm_ref, nv_ref):
        g = p_ref[0]
        for kk in range(1, n):
            g = g + p_ref[kk]
        nm = ADAM_B1 * m_ref[...] + (1.0 - ADAM_B1) * g
        nv = ADAM_B2 * v_ref[...] + (1.0 - ADAM_B2) * (g * g)
        g_ref[...] = g
        nm_ref[...] = nm
        nv_ref[...] = nv
        d_ref[...] = -ADAM_LR * ((nm / c1) / (jnp.sqrt(nv / c2) + ADAM_EPS) + ADAM_WD * w_ref[...])

    spec = pl.BlockSpec((tr, cols), lambda i: (i, 0))
    return _pcall(body, name=name, grid=(rows // tr,),
                  in_specs=[spec, spec, spec, pl.BlockSpec((n, tr, cols), lambda i: (0, i, 0))],
                  out_specs=[spec] * 4,
                  out_shape=[jax.ShapeDtypeStruct((rows, cols), F32)] * 4,
                  semantics=("parallel",))(w, m, v, parts)


def _silu(c):
    return c / (1.0 + jnp.exp(-c))


def _ada_fwd(c_all, ada_w, ada_b_sh, *, name):
    nl, d, wsh = ada_w.shape

    def body(c_ref, w_ref, b_ref, o_ref):
        cond = _silu(c_ref[...]).astype(BF16)
        o_ref[0] = _dot(cond, w_ref[0].astype(BF16)) + b_ref[0]

    return _pcall(body, name=name, grid=(nl,),
                  in_specs=[pl.BlockSpec(c_all.shape, lambda l: (0, 0)),
                            pl.BlockSpec((1, d, wsh), lambda l: (l, 0, 0)),
                            pl.BlockSpec((1, 1, wsh), lambda l: (l, 0, 0))],
                  out_specs=pl.BlockSpec((1, c_all.shape[0], wsh), lambda l: (l, 0, 0)),
                  out_shape=jax.ShapeDtypeStruct((nl, c_all.shape[0], wsh), F32),
                  semantics=("parallel",))(c_all, ada_w, ada_b_sh)


def _ada_bwd(c_all, dmod_sh, *, name):
    nl, nb, wsh = dmod_sh.shape
    d = c_all.shape[1]

    def body(c_ref, dm_ref, o_ref):
        cond = _silu(c_ref[...]).astype(BF16)
        o_ref[0] = _dot_tn(cond, dm_ref[0].astype(BF16))

    return _pcall(body, name=name, grid=(nl,),
                  in_specs=[pl.BlockSpec(c_all.shape, lambda l: (0, 0)),
                            pl.BlockSpec((1, nb, wsh), lambda l: (l, 0, 0))],
                  out_specs=pl.BlockSpec((1, d, wsh), lambda l: (l, 0, 0)),
                  out_shape=jax.ShapeDtypeStruct((nl, d, wsh), F32),
                  semantics=("parallel",))(c_all, dmod_sh)


SMALL_NAMES = ("norm1_g", "norm2_g", "b_forget", "q_norm_g", "k_norm_g", "sgu_norm_g", "sgu_w",
               "sgu_b")
WEIGHT_NAMES = ("ada_w", "ada_b", "norm1_g", "norm2_g", "w_in", "b_forget", "q_norm_g", "k_norm_g",
                "sgu_norm_g", "sgu_w", "sgu_b", "w_out", "mlp_w1", "mlp_w2")


SMALL_TILE_ROWS = 256


def _pack_small(tree):
    flat = jnp.concatenate([tree[n].reshape(-1) for n in SMALL_NAMES])
    n = flat.shape[0]
    rows = -(-n // (SMALL_TILE_ROWS * LANES)) * SMALL_TILE_ROWS
    return jnp.zeros((rows * LANES,), F32).at[:n].set(flat).reshape(rows, LANES)


def _unpack_small(packed, like):
    flat = packed.reshape(-1)
    out, off = {}, 0
    for n in SMALL_NAMES:
        size = like[n].size
        out[n] = flat[off:off + size].reshape(like[n].shape)
        off += size
    return out


def kernel(x, c, ada_w, ada_b, norm1_g, norm2_g, w_in, b_forget, q_norm_g, k_norm_g, sgu_norm_g, sgu_w, sgu_b, w_out, mlp_w1, mlp_w2, loss_target, m_ada_w, m_ada_b, m_norm1_g, m_norm2_g, m_w_in, m_b_forget, m_q_norm_g, m_k_norm_g, m_sgu_norm_g, m_sgu_w, m_sgu_b, m_w_out, m_mlp_w1, m_mlp_w2, v_ada_w, v_ada_b, v_norm1_g, v_norm2_g, v_w_in, v_b_forget, v_q_norm_g, v_k_norm_g, v_sgu_norm_g, v_sgu_w, v_sgu_b, v_w_out, v_mlp_w1, v_mlp_w2):
    w = dict(ada_w=ada_w, ada_b=ada_b, norm1_g=norm1_g, norm2_g=norm2_g, w_in=w_in,
             b_forget=b_forget, q_norm_g=q_norm_g, k_norm_g=k_norm_g, sgu_norm_g=sgu_norm_g,
             sgu_w=sgu_w, sgu_b=sgu_b, w_out=w_out, mlp_w1=mlp_w1, mlp_w2=mlp_w2)
    mom = dict(ada_w=m_ada_w, ada_b=m_ada_b, norm1_g=m_norm1_g, norm2_g=m_norm2_g, w_in=m_w_in,
               b_forget=m_b_forget, q_norm_g=m_q_norm_g, k_norm_g=m_k_norm_g,
               sgu_norm_g=m_sgu_norm_g, sgu_w=m_sgu_w, sgu_b=m_sgu_b, w_out=m_w_out,
               mlp_w1=m_mlp_w1, mlp_w2=m_mlp_w2)
    var = dict(ada_w=v_ada_w, ada_b=v_ada_b, norm1_g=v_norm1_g, norm2_g=v_norm2_g, w_in=v_w_in,
               b_forget=v_b_forget, q_norm_g=v_q_norm_g, k_norm_g=v_k_norm_g,
               sgu_norm_g=v_sgu_norm_g, sgu_w=v_sgu_w, sgu_b=v_sgu_b, w_out=v_w_out,
               mlp_w1=v_mlp_w1, mlp_w2=v_mlp_w2)
    depth, d = norm1_g.shape
    chip = 2 * lax.axis_index("x") + lax.axis_index("y")
    me = 2 * chip + lax.axis_index("c")
    n_chips = 4
    ada_sh = ada_w.shape[2]

    g_in = _gather_chips(w_in.astype(BF16), name="gather_w_in")
    g_out = _gather_chips(w_out.astype(BF16), name="gather_w_out")
    g_w1 = _gather_chips(mlp_w1.astype(BF16), name="gather_w1")
    g_w2 = _gather_chips(mlp_w2.astype(BF16), name="gather_w2")
    layer_w = []
    for l in range(depth):
        layer_w.append(dict(
            w_in=_w_in_to_internal(jnp.concatenate([g_in[k, l] for k in range(n_chips)], axis=1)),
            w_out=g_out[:, l].reshape(d, d),
            w1=jnp.concatenate([g_w1[k, l] for k in range(n_chips)], axis=1),
            w2=g_w2[:, l].reshape(D_FF, d)))

    c_all = _gather_all(jnp.zeros((8, d), F32).at[0].set(c[0]), name="gather_c")[:, 0]
    c_pad = jnp.concatenate([c_all, jnp.zeros_like(c_all)], axis=0)
    ada_b_sh = lax.dynamic_slice_in_dim(ada_b, chip * ada_sh, ada_sh, axis=1)[:, None, :]
    mod_sh = _ada_fwd(c_pad, ada_w, ada_b_sh, name="ada_fwd")
    mod_all = _gather_chips(mod_sh, name="gather_mod")
    mod_me = lax.dynamic_index_in_dim(mod_all, me, axis=2, keepdims=False)
    mod_me = mod_me.transpose(1, 0, 2).reshape(depth, 6, 1, d)

    saved = []
    xs, prev = x[0], None
    for l in range(depth):
        mod = [mod_me[l, kk] for kk in range(6)]
        sm = {n: w[n][l] for n in SMALL_NAMES}
        sv = _layer_fwd(xs, prev, mod, layer_w[l], sm, l)
        saved.append(sv)
        xs, prev = sv["x1"], (sv["m2"], mod[5])

    sq, dxs = _loss_fwd_bwd(xs, prev[0], prev[1], loss_target[0], name="loss")
    loss = lax.psum(0.5 * jnp.sum(sq) / d, ("x", "y", "c"))

    big = {n: [] for n in ("w_in", "w_out", "w1", "w2")}
    small = {n: [] for n in SMALL_NAMES}
    dmods = []
    for l in reversed(range(depth)):
        mod = [mod_me[l, kk] for kk in range(6)]
        sm = {n: w[n][l] for n in SMALL_NAMES}
        dxs, bg, smg, dmod = _layer_bwd(dxs, saved[l], mod, layer_w[l], sm, l)
        for n in big:
            big[n].insert(0, bg[n])
        for n in SMALL_NAMES:
            small[n].insert(0, smg[n])
        dmods.insert(0, dmod)
    grad_x = dxs[None]

    out_g, out_d, out_m, out_v = {}, {}, {}, {}

    def run_adamw(name, parts2d, shape):
        rows, cols = parts2d.shape[1:]
        g, dl, nm, nv = _adamw(w[name].reshape(rows, cols), mom[name].reshape(rows, cols),
                               var[name].reshape(rows, cols), parts2d, name=f"adamw_{name}")
        out_g[name], out_d[name] = g.reshape(shape), dl.reshape(shape)
        out_m[name], out_v[name] = nm.reshape(shape), nv.reshape(shape)

    def shards_of(name, l):
        if name == "w_in":
            g = _w_in_from_internal(big["w_in"][l])
            return jnp.stack(jnp.split(g, n_chips, axis=1))
        if name == "mlp_w1":
            return jnp.stack(jnp.split(big["w1"][l], n_chips, axis=1))
        if name == "w_out":
            return big["w_out"][l].reshape(n_chips, d // n_chips, d)
        return big["w2"][l].reshape(n_chips, D_FF // n_chips, d)

    for name in ("w_in", "w_out", "mlp_w1", "mlp_w2"):
        per_chip = jnp.stack([shards_of(name, l) for l in range(depth)], axis=1)
        rows, cols = depth * per_chip.shape[2], per_chip.shape[3]
        per_chip = per_chip.reshape(n_chips, rows, cols)
        got = _scatter_chips(per_chip, name=f"scatter_{name}")
        part = _sum_slots(got, name=f"sum_{name}")
        both = _swap_cores(part, name=f"swap_{name}")
        run_adamw(name, both, w[name].shape)

    small_tree = {n: jnp.stack(small[n]) for n in SMALL_NAMES}
    gathered = _gather_all(_pack_small(small_tree), name="gather_small")
    gs, ds_, ms, vs = _adamw(_pack_small({n: w[n] for n in SMALL_NAMES}),
                             _pack_small({n: mom[n] for n in SMALL_NAMES}),
                             _pack_small({n: var[n] for n in SMALL_NAMES}), gathered,
                             name="adamw_small")
    like = {n: w[n] for n in SMALL_NAMES}
    for tree, packed in ((out_g, gs), (out_d, ds_), (out_m, ms), (out_v, vs)):
        tree.update(_unpack_small(packed, like))

    dmod_mine = jnp.concatenate(dmods, axis=0)
    dmod_all = _gather_all(jnp.zeros((depth, 8, 6 * d), F32).at[:, 0].set(dmod_mine),
                           name="gather_dmod")[:, :, 0]
    dmod_lb = dmod_all.transpose(1, 0, 2)
    dmod_sh = lax.dynamic_slice_in_dim(dmod_lb, chip * ada_sh, ada_sh, axis=2)
    dmod_sh = jnp.concatenate([dmod_sh, jnp.zeros_like(dmod_sh)], axis=1)
    g_ada_w = _ada_bwd(c_pad, dmod_sh, name="ada_bwd")
    run_adamw("ada_w", g_ada_w.reshape(1, depth * d, ada_sh), ada_w.shape)
    parts_b = dmod_all.reshape(8, depth * 6 * d // LANES, LANES)
    run_adamw("ada_b", parts_b, ada_b.shape)

    outs = [loss, grad_x]
    for tree in (out_g, out_d, out_m, out_v):
        outs += [tree[n] for n in WEIGHT_NAMES]
    return tuple(outs)
```

```python
import functools
import math

import jax
import jax.numpy as jnp
from jax import lax
from jax.experimental import pallas as pl
from jax.experimental.pallas import tpu as pltpu

F32 = jnp.float32
BF16 = jnp.bfloat16

D_MODEL = 1024
DEPTH = 4
HEAD_DIM = 64
LANES = 128
D_FF = 4 * D_MODEL
EPS = 1e-6
SB_W, FOX_W, SGU_W = 256, 512, 256
FOX_HEADS = 8
SGU_CHUNK = 128
IN_W = 2824
ATT_W = 3 * SB_W + 3 * FOX_W
PROJ_W = 3072
CB_QA, CB_KA, CB_VA = 0, 2, 4
CB_QB, CB_KB, CB_VB = 6, 10, 14
CB_UC, CB_VC, CB_FL = 18, 20, 22
ATT_T = 256
VMEM_LIMIT = 56 * 2 ** 20
SKIP_LOG = 110.0

ADAM_LR, ADAM_B1, ADAM_B2, ADAM_EPS, ADAM_WD, ADAM_STEP = 0.001, 0.9, 0.999, 1e-08, 0.01, 10

MESH = pl.DeviceIdType.MESH


def _pcall(body, *, name, out_shape, grid=(), in_specs=None, out_specs=None, scratch_shapes=(),
           semantics=None):
    params = dict(vmem_limit_bytes=VMEM_LIMIT)
    if semantics is not None:
        params["dimension_semantics"] = semantics
    kwargs = {}
    if in_specs is not None:
        kwargs["in_specs"] = in_specs
    if out_specs is not None:
        kwargs["out_specs"] = out_specs
    return pl.pallas_call(body, name=name, out_shape=out_shape, grid=grid,
                          scratch_shapes=list(scratch_shapes),
                          compiler_params=pltpu.CompilerParams(**params), **kwargs)


def _dot(a, b):
    return jnp.dot(a, b, preferred_element_type=F32)


def _dot_nt(a, b):
    return lax.dot_general(a, b, (((1,), (1,)), ((), ())), preferred_element_type=F32)


def _dot_tn(a, b):
    return lax.dot_general(a, b, (((0,), (0,)), ((), ())), preferred_element_type=F32)


def _split2(x):
    hi = x.astype(BF16)
    lo = (x - hi.astype(F32)).astype(BF16)
    return hi, lo


def _ones_dot(x, ones_bf16):
    hi, lo = _split2(x)
    return _dot(hi, ones_bf16) + _dot(lo, ones_bf16)


def _rowwise(fn, fulls, vecs, out_dtypes, n_vec_out, *, name, tr):
    s, n = fulls[0].shape
    tr = min(tr, s)
    assert s % tr == 0, (name, s, tr)
    nf, nv, nfo = len(fulls), len(vecs), len(out_dtypes)

    def body(*refs):
        fi, vi = refs[:nf], refs[nf:nf + nv]
        fo, vo = refs[nf + nv:nf + nv + nfo], refs[nf + nv + nfo:]
        outs_f, outs_v = fn([r[...] for r in fi], [r[...] for r in vi])
        for r, o in zip(fo, outs_f):
            r[...] = o.astype(r.dtype)
        if n_vec_out:
            @pl.when(pl.program_id(0) == 0)
            def _():
                for r in vo:
                    r[...] = jnp.zeros_like(r)
            for r, o in zip(vo, outs_v):
                r[...] += o

    full_spec = pl.BlockSpec((tr, n), lambda i: (i, 0))
    vec_specs = [pl.BlockSpec(v.shape, lambda i: (0, 0)) for v in vecs]
    out_vec_spec = pl.BlockSpec((1, n), lambda i: (0, 0))
    out_shape = [jax.ShapeDtypeStruct((s, n), dt) for dt in out_dtypes]
    out_shape += [jax.ShapeDtypeStruct((1, n), F32)] * n_vec_out
    outs = _pcall(body, name=name, grid=(s // tr,),
                  in_specs=[full_spec] * nf + vec_specs,
                  out_specs=[full_spec] * nfo + [out_vec_spec] * n_vec_out,
                  out_shape=out_shape,
                  semantics=("arbitrary",) if n_vec_out else ("parallel",))(*fulls, *vecs)
    return outs[:nfo], outs[nfo:]


def _colsum(t):
    return jnp.sum(t, axis=0, keepdims=True)


def _rms_mod(x, g, sc, sh):
    r = lax.rsqrt(jnp.mean(x * x, axis=-1, keepdims=True) + EPS)
    return (x * r * g) * (1.0 + sc) + sh


def _norm_mod_fwd(x, g, sc, sh, *, name):
    def fn(f, v):
        return [_rms_mod(f[0], v[0], v[1], v[2])], []
    (h,), _ = _rowwise(fn, [x], [g, sc, sh], [BF16], 0, name=name, tr=512)
    return h


def _resid_norm_mod_fwd(x, m, gate, g, sc, sh, *, name):
    def fn(f, v):
        xn = f[0] + v[0] * f[1]
        return [xn, _rms_mod(xn, v[1], v[2], v[3])], []
    (xn, h), _ = _rowwise(fn, [x, m], [gate, g, sc, sh], [F32, BF16], 0, name=name, tr=512)
    return xn, h


def _norm_mod_bwd(x, dh, dres, g, sc, *, name):
    def fn(f, v):
        xv, dhv, dr = f
        gv, scv = v
        r = lax.rsqrt(jnp.mean(xv * xv, axis=-1, keepdims=True) + EPS)
        xh = xv * r
        dn = dhv * (1.0 + scv)
        dxh = dn * gv
        dx = dr + r * (dxh - xh * jnp.mean(dxh * xh, axis=-1, keepdims=True))
        return [dx], [_colsum(dn * xh), _colsum(dhv * (xh * gv)), _colsum(dhv)]
    (dx,), (dg, dsc, dsh) = _rowwise(fn, [x, dh, dres], [g, sc], [F32], 3, name=name, tr=256)
    return dx, dg, dsc, dsh


def _gate_bwd(dx, m, gate, *, name):
    def fn(f, v):
        return [f[0] * v[0]], [_colsum(f[0] * f[1])]
    (dm,), (dgate,) = _rowwise(fn, [dx, m], [gate], [BF16], 1, name=name, tr=512)
    return dm, dgate


def _relu2_fwd(a, *, name):
    def fn(f, v):
        r = jnp.maximum(f[0], 0.0)
        return [r * r], []
    (r,), _ = _rowwise(fn, [a], [], [BF16], 0, name=name, tr=128)
    return r


def _relu2_bwd(dr, a, *, name):
    def fn(f, v):
        return [f[0] * (2.0 * jnp.maximum(f[1], 0.0))], []
    (da,), _ = _rowwise(fn, [dr, a], [], [BF16], 0, name=name, tr=128)
    return da


def _loss_fwd_bwd(x, m, gate, target, *, name):
    n = x.shape[1]

    def fn(f, v):
        err = f[0] + v[0] * f[1] - f[2]
        return [err * (1.0 / n)], [_colsum(err * err)]
    (dy,), (sq,) = _rowwise(fn, [x, m, target], [gate], [F32], 1, name=name, tr=512)
    return sq, dy


def _matmul(a, b, *, name, ta=False, tb=False, out_dtype=F32, tm=1024, tn=1024, tk=1024):
    m = a.shape[1] if ta else a.shape[0]
    k = a.shape[0] if ta else a.shape[1]
    n = b.shape[0] if tb else b.shape[1]
    assert k == (b.shape[1] if tb else b.shape[0])
    tm, tn, tk = min(tm, m), min(tn, n), min(tk, k)
    assert m % tm == 0 and n % tn == 0 and k % tk == 0, (name, m, n, k)
    nk = k // tk
    dims = (((0 if ta else 1,), (1 if tb else 0,)), ((), ()))

    def body(a_ref, b_ref, o_ref, acc_ref):
        kk = pl.program_id(2)

        @pl.when(kk == 0)
        def _():
            acc_ref[...] = jnp.zeros_like(acc_ref)
        acc_ref[...] += lax.dot_general(a_ref[...].astype(BF16), b_ref[...].astype(BF16), dims,
                                        preferred_element_type=F32)

        @pl.when(kk == nk - 1)
        def _():
            o_ref[...] = acc_ref[...].astype(o_ref.dtype)

    a_spec = (pl.BlockSpec((tk, tm), lambda i, j, kk: (kk, i)) if ta
              else pl.BlockSpec((tm, tk), lambda i, j, kk: (i, kk)))
    b_spec = (pl.BlockSpec((tn, tk), lambda i, j, kk: (j, kk)) if tb
              else pl.BlockSpec((tk, tn), lambda i, j, kk: (kk, j)))
    return _pcall(body, name=name, grid=(m // tm, n // tn, nk),
                  in_specs=[a_spec, b_spec],
                  out_specs=pl.BlockSpec((tm, tn), lambda i, j, kk: (i, j)),
                  out_shape=jax.ShapeDtypeStruct((m, n), out_dtype),
                  scratch_shapes=[pltpu.VMEM((tm, tn), F32)],
                  semantics=("parallel", "parallel", "arbitrary"))(a, b)


def _lane_masks():
    lane = lax.broadcasted_iota(jnp.int32, (1, LANES), 1)
    return [lane < HEAD_DIM, lane >= HEAD_DIM]


def _tri_iotas(t):
    r = lax.broadcasted_iota(jnp.int32, (t, t), 0)
    c = lax.broadcasted_iota(jnp.int32, (t, t), 1)
    return r, c


def _rows(j, t):
    return pl.ds(pl.multiple_of(j * t, t), t)


def _neg_softplus(z):
    e = jnp.exp(-jnp.abs(z))
    return -(jnp.maximum(z, 0.0) + jnp.log(1.0 + e)), e


def _live(e_runs):
    return jnp.maximum(jnp.max(e_runs[0]), jnp.max(e_runs[1])) > -SKIP_LOG


def _sb_fwd(proj, *, name):
    s = proj.shape[0]
    t = min(ATT_T, s)
    scale = HEAD_DIM ** -0.5

    def body(q_ref, k_ref, v_ref, o_ref):
        i = pl.program_id(1)
        hm = _lane_masks()
        q = q_ref[...] * scale
        qh = [jnp.where(mk, q, 0.0).astype(BF16) for mk in hm]
        r, c = _tri_iotas(t)
        later = (r > c).astype(BF16)
        causal = c < r

        def chunk(j, carry, masked):
            kb = k_ref[_rows(j, t), :].astype(BF16)
            vb = v_ref[_rows(j, t), :].astype(BF16)
            out = []
            for h in range(2):
                e_run, acc = carry[h]
                z = _dot_nt(qh[h], kb)
                l, _ = _neg_softplus(z)
                if masked:
                    l = jnp.where(causal, l, 0.0)
                between = _ones_dot(l, later) + e_run
                a = jnp.exp(z + l + between)
                if masked:
                    a = jnp.where(causal, a, 0.0)
                acc = acc + _dot(a.astype(BF16), vb)
                out.append((e_run + jnp.sum(l, axis=1, keepdims=True), acc))
            return tuple(out)

        init = tuple((jnp.zeros((t, 1), F32), jnp.zeros((t, LANES), F32)) for _ in range(2))
        carry = chunk(i, init, True)

        def cond(st):
            return (st[0] >= 0) & _live([st[1][0][0], st[1][1][0]])

        _, carry = lax.while_loop(cond, lambda st: (st[0] - 1, chunk(st[0], st[1], False)),
                                  (i - 1, carry))
        o_ref[...] = jnp.where(hm[0], carry[0][1], carry[1][1])

    blk = lambda cb: pl.BlockSpec((t, LANES), lambda p, i: (i, cb + p))
    full = lambda cb: pl.BlockSpec((s, LANES), lambda p, i: (0, cb + p))
    out_blk = pl.BlockSpec((t, LANES), lambda p, i: (i, p))
    return _pcall(body, name=name, grid=(SB_W // LANES, s // t),
                  in_specs=[blk(CB_QA), full(CB_KA), full(CB_VA)],
                  out_specs=out_blk,
                  out_shape=jax.ShapeDtypeStruct((s, SB_W), F32),
                  semantics=("parallel", "arbitrary"))(proj, proj, proj)


def _sb_bwd(proj, dmixed, *, name):
    s = proj.shape[0]
    t = min(ATT_T, s)
    scale = HEAD_DIM ** -0.5

    def body(q_ref, k_ref, v_ref, do_ref, dq_ref, dk_ref, dv_ref):
        i = pl.program_id(1)

        @pl.when(i == 0)
        def _():
            dk_ref[...] = jnp.zeros_like(dk_ref)
            dv_ref[...] = jnp.zeros_like(dv_ref)

        hm = _lane_masks()
        q = q_ref[...] * scale
        do = do_ref[...]
        qh = [jnp.where(mk, q, 0.0).astype(BF16) for mk in hm]
        doh = [jnp.where(mk, do, 0.0).astype(BF16) for mk in hm]
        r, c = _tri_iotas(t)
        upto = (r <= c).astype(BF16)
        before = (r < c).astype(BF16)
        causal = c < r

        def totals(j, e_runs, masked):
            kb = k_ref[_rows(j, t), :].astype(BF16)
            out = []
            for h in range(2):
                l, _ = _neg_softplus(_dot_nt(qh[h], kb))
                if masked:
                    l = jnp.where(causal, l, 0.0)
                out.append(e_runs[h] + jnp.sum(l, axis=1, keepdims=True))
            return tuple(out)

        lt = totals(i, (jnp.zeros((t, 1), F32),) * 2, True)
        j_stop, lt = lax.while_loop(lambda st: (st[0] >= 0) & _live(st[1]),
                                    lambda st: (st[0] - 1, totals(st[0], st[1], False)),
                                    (i - 1, lt))

        def chunk(j, carry, masked):
            kf = k_ref[_rows(j, t), :]
            kb = kf.astype(BF16)
            vb = v_ref[_rows(j, t), :].astype(BF16)
            out = []
            dk_add = jnp.zeros((t, LANES), F32)
            dv_add = jnp.zeros((t, LANES), F32)
            for h in range(2):
                l_run, g_run, dq = carry[h]
                kh = jnp.where(hm[h], kf, 0.0).astype(BF16)
                z = _dot_nt(qh[h], kb)
                l, e = _neg_softplus(z)
                beta = jnp.where(z >= 0.0, 1.0, e) / (1.0 + e)
                if masked:
                    l = jnp.where(causal, l, 0.0)
                prefix = _ones_dot(l, upto) + l_run
                a = jnp.exp(z + l + (lt[h] - prefix))
                if masked:
                    a = jnp.where(causal, a, 0.0)
                g = a * _dot_nt(doh[h], vb)
                g_before = _ones_dot(g, before) + g_run
                dz = g * (1.0 - beta) - beta * g_before
                if masked:
                    dz = jnp.where(causal, dz, 0.0)
                dzb = dz.astype(BF16)
                dq = dq + _dot(dzb, kh)
                dk_add = dk_add + _dot_tn(dzb, qh[h])
                dv_add = dv_add + _dot_tn(a.astype(BF16), doh[h])
                out.append((l_run + jnp.sum(l, axis=1, keepdims=True),
                            g_run + jnp.sum(g, axis=1, keepdims=True), dq))
            dk_ref[_rows(j, t), :] += dk_add
            dv_ref[_rows(j, t), :] += dv_add
            return tuple(out)

        init = tuple((jnp.zeros((t, 1), F32), jnp.zeros((t, 1), F32), jnp.zeros((t, LANES), F32))
                     for _ in range(2))
        carry = lax.fori_loop(j_stop + 1, i, lambda j, cr: chunk(j, cr, False), init)
        carry = chunk(i, carry, True)
        dq_ref[...] = (carry[0][2] + carry[1][2]) * scale

    blk = lambda cb: pl.BlockSpec((t, LANES), lambda p, i: (i, cb + p))
    full = lambda cb: pl.BlockSpec((s, LANES), lambda p, i: (0, cb + p))
    out_blk = pl.BlockSpec((t, LANES), lambda p, i: (i, p))
    out_full = pl.BlockSpec((s, LANES), lambda p, i: (0, p))
    return _pcall(body, name=name, grid=(SB_W // LANES, s // t),
                  in_specs=[blk(CB_QA), full(CB_KA), full(CB_VA), blk(0)],
                  out_specs=[out_blk, out_full, out_full],
                  out_shape=[jax.ShapeDtypeStruct((s, SB_W), F32)] * 3,
                  semantics=("parallel", "arbitrary"))(proj, proj, proj, dmixed)


def _group_mean(v, lo):
    s0 = jnp.sum(jnp.where(lo, v, 0.0), axis=1, keepdims=True)
    s1 = jnp.sum(jnp.where(lo, 0.0, v), axis=1, keepdims=True)
    return jnp.where(lo, s0, s1) * (1.0 / HEAD_DIM)


def _fox_prep_fwd(proj, qg, kg, *, name):
    s = proj.shape[0]
    tr = min(512, s)

    def body(q_ref, k_ref, qg_ref, kg_ref, qn_ref, kn_ref, kmax_ref):
        lo = _lane_masks()[0]
        for x_ref, g_ref, o_ref in ((q_ref, qg_ref, qn_ref), (k_ref, kg_ref, kn_ref)):
            x = x_ref[...]
            o_ref[...] = x * lax.rsqrt(_group_mean(x * x, lo) + EPS) * g_ref[...]

        @pl.when(pl.program_id(1) == 0)
        def _():
            kmax_ref[...] = jnp.zeros_like(kmax_ref)
        kn = kn_ref[...]
        norms = jnp.sqrt(_group_mean(kn * kn, lo) * HEAD_DIM)
        kmax_ref[...] = jnp.maximum(kmax_ref[...], jnp.max(norms, axis=0, keepdims=True))

    blk = lambda cb: pl.BlockSpec((tr, LANES), lambda p, i: (i, cb + p))
    vec = pl.BlockSpec((1, LANES), lambda p, i: (0, 0))
    out_blk = pl.BlockSpec((tr, LANES), lambda p, i: (i, p))
    return _pcall(body, name=name, grid=(FOX_W // LANES, s // tr),
                  in_specs=[blk(CB_QB), blk(CB_KB), vec, vec],
                  out_specs=[out_blk, out_blk, pl.BlockSpec((1, LANES), lambda p, i: (0, p))],
                  out_shape=[jax.ShapeDtypeStruct((s, FOX_W), F32)] * 2
                  + [jax.ShapeDtypeStruct((1, FOX_W), F32)],
                  semantics=("parallel", "arbitrary"))(proj, proj, qg, kg)


def _fox_prep_bwd(proj, dqn, dkn, qg, kg, *, name):
    s = proj.shape[0]
    tr = min(512, s)

    def body(q_ref, k_ref, dqn_ref, dkn_ref, qg_ref, kg_ref, dq_ref, dk_ref, dqg_ref, dkg_ref):
        @pl.when((pl.program_id(0) == 0) & (pl.program_id(1) == 0))
        def _():
            dqg_ref[...] = jnp.zeros_like(dqg_ref)
            dkg_ref[...] = jnp.zeros_like(dkg_ref)

        lo = _lane_masks()[0]
        for x_ref, dy_ref, g_ref, dx_ref, dg_ref in ((q_ref, dqn_ref, qg_ref, dq_ref, dqg_ref),
                                                     (k_ref, dkn_ref, kg_ref, dk_ref, dkg_ref)):
            x, dy = x_ref[...], dy_ref[...]
            r = lax.rsqrt(_group_mean(x * x, lo) + EPS)
            xh = x * r
            dxh = dy * g_ref[...]
            dx_ref[...] = r * (dxh - xh * _group_mean(dxh * xh, lo))
            dg_ref[...] += _colsum(dy * xh)

    blk = lambda cb: pl.BlockSpec((tr, LANES), lambda p, i: (i, cb + p))
    vec = pl.BlockSpec((1, LANES), lambda p, i: (0, 0))
    out_blk = pl.BlockSpec((tr, LANES), lambda p, i: (i, p))
    return _pcall(body, name=name, grid=(FOX_W // LANES, s // tr),
                  in_specs=[blk(CB_QB), blk(CB_KB), out_blk, out_blk, vec, vec],
                  out_specs=[out_blk, out_blk, vec, vec],
                  out_shape=[jax.ShapeDtypeStruct((s, FOX_W), F32)] * 2
                  + [jax.ShapeDtypeStruct((1, LANES), F32)] * 2,
                  semantics=("arbitrary", "arbitrary"))(proj, proj, dqn, dkn, qg, kg)


def _split3_dot(tri_bf16, x):
    hi = x.astype(BF16)
    r1 = x - hi.astype(F32)
    mid = r1.astype(BF16)
    lo = (r1 - mid.astype(F32)).astype(BF16)
    return _dot(tri_bf16, hi) + _dot(tri_bf16, mid) + _dot(tri_bf16, lo)


def _forget_cumsum_fwd(proj, b_pad, *, name):
    s = proj.shape[0]
    tb = min(256, s)

    def body(fl_ref, b_ref, cf_ref, run_ref):
        @pl.when(pl.program_id(0) == 0)
        def _():
            run_ref[...] = jnp.zeros_like(run_ref)
        lf, _ = _neg_softplus(-(fl_ref[...] + b_ref[...]))
        r, c = _tri_iotas(tb)
        incl = _split3_dot((c <= r).astype(BF16), lf) + run_ref[...]
        cf_ref[...] = incl
        run_ref[...] = incl[tb - 1:tb, :]

    return _pcall(body, name=name, grid=(s // tb,),
                  in_specs=[pl.BlockSpec((tb, LANES), lambda i: (i, CB_FL)),
                            pl.BlockSpec((1, LANES), lambda i: (0, 0))],
                  out_specs=pl.BlockSpec((tb, LANES), lambda i: (i, 0)),
                  out_shape=jax.ShapeDtypeStruct((s, LANES), F32),
                  scratch_shapes=[pltpu.VMEM((1, LANES), F32)],
                  semantics=("arbitrary",))(proj, b_pad)


def _forget_cumsum_bwd(proj, b_pad, dcf, *, name):
    s = proj.shape[0]
    tb = min(256, s)
    nb = s // tb

    def body(fl_ref, b_ref, dcf_ref, dfl_ref, db_ref, run_ref):
        @pl.when(pl.program_id(0) == 0)
        def _():
            run_ref[...] = jnp.zeros_like(run_ref)
            db_ref[...] = jnp.zeros_like(db_ref)
        r, c = _tri_iotas(tb)
        dlf = _split3_dot((c >= r).astype(BF16), dcf_ref[...]) + run_ref[...]
        run_ref[...] = dlf[0:1, :]
        xv = fl_ref[...] + b_ref[...]
        e = jnp.exp(-jnp.abs(xv))
        sig_neg = jnp.where(xv >= 0.0, e, 1.0) / (1.0 + e)
        dfl = dlf * sig_neg
        dfl_ref[...] = dfl
        db_ref[...] += _colsum(dfl)

    return _pcall(body, name=name, grid=(nb,),
                  in_specs=[pl.BlockSpec((tb, LANES), lambda i: (nb - 1 - i, CB_FL)),
                            pl.BlockSpec((1, LANES), lambda i: (0, 0)),
                            pl.BlockSpec((tb, LANES), lambda i: (nb - 1 - i, 0))],
                  out_specs=[pl.BlockSpec((tb, LANES), lambda i: (nb - 1 - i, 0)),
                             pl.BlockSpec((1, LANES), lambda i: (0, 0))],
                  out_shape=[jax.ShapeDtypeStruct((s, LANES), F32),
                             jax.ShapeDtypeStruct((1, LANES), F32)],
                  scratch_shapes=[pltpu.VMEM((1, LANES), F32)],
                  semantics=("arbitrary",))(proj, b_pad, dcf)


def _fox_bias_q(cfc, p, h):
    lane = lax.broadcasted_iota(jnp.int32, (1, LANES), 1)
    return jnp.sum(jnp.where(lane == 2 * p + h, cfc, 0.0), axis=1, keepdims=True)


def _fox_score_bound(q, kmax_row, hm):
    out = []
    for h in range(2):
        qnorm = jnp.sqrt(jnp.sum(jnp.where(hm[h], q * q, 0.0), axis=1, keepdims=True))
        out.append(1.02 * qnorm * kmax_row[:, h * HEAD_DIM:h * HEAD_DIM + 1])
    return out


def _fox_live(cfr_ref, j, t, tops):
    jc = jnp.maximum(j, 0)
    worst = []
    for h in range(2):
        cf_min = jnp.min(cfr_ref[0, pl.ds(h, 1), _rows(jc, t)], axis=1, keepdims=True)
        worst.append(jnp.max(tops[h] - cf_min))
    return (j >= 0) & (jnp.maximum(worst[0], worst[1]) > -SKIP_LOG)


def _fox_fwd(proj, qn, kn, cf, cf_rows, kmax, *, name):
    s = proj.shape[0]
    t = min(ATT_T, s)
    scale = HEAD_DIM ** -0.5

    def body(q_ref, k_ref, v_ref, cfc_ref, cfr_ref, kmax_ref, o_ref, lse_ref):
        p, i = pl.program_id(0), pl.program_id(1)
        hm = _lane_masks()
        q = q_ref[...] * scale
        qh = [jnp.where(mk, q, 0.0).astype(BF16) for mk in hm]
        cfc = cfc_ref[...]
        bq = [_fox_bias_q(cfc, p, h) for h in range(2)]
        qk_top = _fox_score_bound(q, kmax_ref[...], hm)
        r, c = _tri_iotas(t)
        causal = c <= r

        def chunk(j, carry, masked):
            kb = k_ref[_rows(j, t), :].astype(BF16)
            vb = v_ref[_rows(j, t), :].astype(BF16)
            out = []
            for h in range(2):
                m_run, l_run, acc = carry[h]
                z = _dot_nt(qh[h], kb) + (bq[h] - cfr_ref[0, pl.ds(h, 1), _rows(j, t)])
                if masked:
                    z = jnp.where(causal, z, -1e30)
                m_new = jnp.maximum(m_run, jnp.max(z, axis=1, keepdims=True))
                alpha = jnp.exp(m_run - m_new)
                pr = jnp.exp(z - m_new)
                out.append((m_new, alpha * l_run + jnp.sum(pr, axis=1, keepdims=True),
                            alpha * acc + _dot(pr.astype(BF16), vb)))
            return tuple(out)

        init = tuple((jnp.full((t, 1), -1e30, F32), jnp.zeros((t, 1), F32),
                      jnp.zeros((t, LANES), F32)) for _ in range(2))
        carry = chunk(i, init, True)

        def live(j, cr):
            return _fox_live(cfr_ref, j, t, [qk_top[h] + bq[h] - cr[h][0] for h in range(2)])

        def step(st):
            cr = chunk(st[0], st[2], False)
            return st[0] - 1, live(st[0] - 1, cr), cr

        carry = lax.while_loop(lambda st: st[1], step, (i - 1, live(i - 1, carry), carry))[2]
        o_ref[...] = jnp.where(hm[0], carry[0][2] / carry[0][1], carry[1][2] / carry[1][1])
        lse_ref[...] = jnp.where(hm[0], carry[0][0] + jnp.log(carry[0][1]),
                                 carry[1][0] + jnp.log(carry[1][1]))

    blk = pl.BlockSpec((t, LANES), lambda p, i: (i, p))
    full = pl.BlockSpec((s, LANES), lambda p, i: (0, p))
    return _pcall(body, name=name, grid=(FOX_W // LANES, s // t),
                  in_specs=[blk, full, pl.BlockSpec((s, LANES), lambda p, i: (0, CB_VB + p)),
                            pl.BlockSpec((t, LANES), lambda p, i: (i, 0)),
                            pl.BlockSpec((1, 2, s), lambda p, i: (p, 0, 0)),
                            pl.BlockSpec((1, LANES), lambda p, i: (0, p))],
                  out_specs=[blk, blk],
                  out_shape=[jax.ShapeDtypeStruct((s, FOX_W), F32)] * 2,
                  semantics=("parallel", "arbitrary"))(qn, kn, proj, cf, cf_rows, kmax)


def _fox_bwd(proj, qn, kn, cf, cf_rows, kmax, do, o, lse, *, name):
    s = proj.shape[0]
    t = min(ATT_T, s)
    scale = HEAD_DIM ** -0.5

    def body(q_ref, k_ref, v_ref, cfc_ref, cfr_ref, kmax_ref, do_ref, o_ref, lse_ref,
             dq_ref, dk_ref, dv_ref, dcf_ref, dcfq_ref):
        p, i = pl.program_id(0), pl.program_id(1)

        @pl.when(i == 0)
        def _():
            dk_ref[...] = jnp.zeros_like(dk_ref)
            dv_ref[...] = jnp.zeros_like(dv_ref)
            dcf_ref[...] = jnp.zeros_like(dcf_ref)

        hm = _lane_masks()
        q = q_ref[...] * scale
        do = do_ref[...]
        dov = do * o_ref[...]
        qh = [jnp.where(mk, q, 0.0).astype(BF16) for mk in hm]
        doh = [jnp.where(mk, do, 0.0).astype(BF16) for mk in hm]
        delta = [jnp.sum(jnp.where(mk, dov, 0.0), axis=1, keepdims=True) for mk in hm]
        lsev = lse_ref[...]
        lse = [lsev[:, 0:1], lsev[:, HEAD_DIM:HEAD_DIM + 1]]
        cfc = cfc_ref[...]
        bq = [_fox_bias_q(cfc, p, h) - lse[h] for h in range(2)]
        qk_top = _fox_score_bound(q, kmax_ref[...], hm)
        tops = [qk_top[h] + bq[h] for h in range(2)]
        r, c = _tri_iotas(t)
        causal = c <= r
        j_stop = lax.while_loop(lambda st: st[1],
                                lambda st: (st[0] - 1, _fox_live(cfr_ref, st[0] - 1, t, tops)),
                                (i - 1, _fox_live(cfr_ref, i - 1, t, tops)))[0]

        def chunk(j, carry, masked):
            kf = k_ref[_rows(j, t), :]
            kb = kf.astype(BF16)
            vb = v_ref[_rows(j, t), :].astype(BF16)
            out = []
            dk_add = jnp.zeros((t, LANES), F32)
            dv_add = jnp.zeros((t, LANES), F32)
            for h in range(2):
                kh = jnp.where(hm[h], kf, 0.0).astype(BF16)
                z = _dot_nt(qh[h], kb) + (bq[h] - cfr_ref[0, pl.ds(h, 1), _rows(j, t)])
                pr = jnp.exp(z)
                if masked:
                    pr = jnp.where(causal, pr, 0.0)
                ds = pr * (_dot_nt(doh[h], vb) - delta[h])
                dsb = ds.astype(BF16)
                out.append((carry[h][0] + _dot(dsb, kh),
                            carry[h][1] + jnp.sum(ds, axis=1, keepdims=True)))
                dk_add = dk_add + _dot_tn(dsb, qh[h])
                dv_add = dv_add + _dot_tn(pr.astype(BF16), doh[h])
                dcf_ref[0, pl.ds(h, 1), _rows(j, t)] -= jnp.sum(ds, axis=0, keepdims=True)
            dk_ref[_rows(j, t), :] += dk_add
            dv_ref[_rows(j, t), :] += dv_add
            return tuple(out)

        init = tuple((jnp.zeros((t, LANES), F32), jnp.zeros((t, 1), F32)) for _ in range(2))
        carry = lax.fori_loop(j_stop + 1, i, lambda j, cr: chunk(j, cr, False), init)
        carry = chunk(i, carry, True)
        dq_ref[...] = (carry[0][0] + carry[1][0]) * scale
        dcfq_ref[...] = jnp.where(hm[0], carry[0][1], carry[1][1])

    blk = pl.BlockSpec((t, LANES), lambda p, i: (i, p))
    full = pl.BlockSpec((s, LANES), lambda p, i: (0, p))
    rows = pl.BlockSpec((1, 2, s), lambda p, i: (p, 0, 0))
    return _pcall(body, name=name, grid=(FOX_W // LANES, s // t),
                  in_specs=[blk, full, pl.BlockSpec((s, LANES), lambda p, i: (0, CB_VB + p)),
                            pl.BlockSpec((t, LANES), lambda p, i: (i, 0)), rows,
                            pl.BlockSpec((1, LANES), lambda p, i: (0, p)),
                            pl.BlockSpec((t, LANES), lambda p, i: (i, SB_W // LANES + p)),
                            blk, blk],
                  out_specs=[blk, full, full, rows, blk],
                  out_shape=[jax.ShapeDtypeStruct((s, FOX_W), F32)] * 3
                  + [jax.ShapeDtypeStruct((FOX_W // LANES, 2, s), F32),
                     jax.ShapeDtypeStruct((s, FOX_W), F32)],
                  semantics=("parallel", "arbitrary"))(qn, kn, proj, cf, cf_rows, kmax, do, o, lse)


_GELU_C0 = math.sqrt(2.0 / math.pi)
_GELU_C1 = 0.044715


def _gelu(x):
    th = jnp.tanh(_GELU_C0 * (x + _GELU_C1 * (x * x * x)))
    return 0.5 * x * (1.0 + th), th


def _gelu_grad(x, th):
    return 0.5 * (1.0 + th) + 0.5 * x * (1.0 - th * th) * (_GELU_C0 * (1.0 + 3.0 * _GELU_C1 * x * x))


def _sgu_mix(wm, vn_c, lo, bcol):
    return jnp.where(lo, _dot(wm[0], vn_c) + bcol[0], _dot(wm[1], vn_c) + bcol[1])


def _sgu_fwd(proj, w, b_cols, gn, *, name):
    s = proj.shape[0]
    tr = min(512, s)
    ch = SGU_CHUNK

    def body(u_ref, v_ref, w_ref, b_ref, gn_ref, o_ref):
        lo = _lane_masks()[0]
        r, c = _tri_iotas(ch)
        wm = [jnp.where(c <= r, w_ref[h], 0.0).astype(BF16) for h in range(2)]
        bcol = [b_ref[0, :, h:h + 1] for h in range(2)]
        for n in range(tr // ch):
            rows = slice(n * ch, (n + 1) * ch)
            u, _ = _gelu(u_ref[rows, :])
            vg, _ = _gelu(v_ref[rows, :])
            vn = vg * lax.rsqrt(_group_mean(vg * vg, lo) + EPS) * gn_ref[0]
            o_ref[rows, :] = u * _sgu_mix(wm, vn.astype(BF16), lo, bcol)

    blk = lambda cb: pl.BlockSpec((tr, LANES), lambda p, i: (i, cb + p))
    return _pcall(body, name=name, grid=(SGU_W // LANES, s // tr),
                  in_specs=[blk(CB_UC), blk(CB_VC),
                            pl.BlockSpec((2, ch, ch), lambda p, i: (p, 0, 0)),
                            pl.BlockSpec((1, ch, 2), lambda p, i: (p, 0, 0)),
                            pl.BlockSpec((1, 1, LANES), lambda p, i: (p, 0, 0))],
                  out_specs=pl.BlockSpec((tr, LANES), lambda p, i: (i, p)),
                  out_shape=jax.ShapeDtypeStruct((s, SGU_W), F32),
                  semantics=("parallel", "parallel"))(proj, proj, w, b_cols, gn)


def _sgu_bwd(proj, dmixed, w, w_t, b_cols, gn, *, name):
    s = proj.shape[0]
    tr = min(512, s)
    ch = SGU_CHUNK
    cb_do = (SB_W + FOX_W) // LANES

    def body(u_ref, v_ref, do_ref, w_ref, wt_ref, b_ref, gn_ref,
             du_ref, dv_ref, dw_ref, db_ref, dgn_ref):
        @pl.when(pl.program_id(1) == 0)
        def _():
            dw_ref[...] = jnp.zeros_like(dw_ref)
            db_ref[...] = jnp.zeros_like(db_ref)
            dgn_ref[...] = jnp.zeros_like(dgn_ref)

        hm = _lane_masks()
        lo = hm[0]
        r, c = _tri_iotas(ch)
        wm = [jnp.where(c <= r, w_ref[h], 0.0).astype(BF16) for h in range(2)]
        wtm = [jnp.where(r <= c, wt_ref[h], 0.0).astype(BF16) for h in range(2)]
        bcol = [b_ref[0, :, h:h + 1] for h in range(2)]
        gnv = gn_ref[0]
        for n in range(tr // ch):
            rows = slice(n * ch, (n + 1) * ch)
            uc, vc, do = u_ref[rows, :], v_ref[rows, :], do_ref[rows, :]
            u, thu = _gelu(uc)
            vg, thv = _gelu(vc)
            rinv = lax.rsqrt(_group_mean(vg * vg, lo) + EPS)
            xh = vg * rinv
            vnb = (xh * gnv).astype(BF16)
            mix = _sgu_mix(wm, vnb, lo, bcol)
            du_ref[rows, :] = do * mix * _gelu_grad(uc, thu)
            dm = do * u
            dmb = dm.astype(BF16)
            dvn = jnp.where(lo, _dot(wtm[0], dmb), _dot(wtm[1], dmb))
            for h in range(2):
                dmh = jnp.where(hm[h], dm, 0.0)
                dw_ref[h] += jnp.where(c <= r, _dot_nt(dmh.astype(BF16), vnb), 0.0)
                db_ref[0, :, h:h + 1] += jnp.sum(dmh, axis=1, keepdims=True)
            dgn_ref[0] += _colsum(dvn * xh)
            dxh = dvn * gnv
            dvg = rinv * (dxh - xh * _group_mean(dxh * xh, lo))
            dv_ref[rows, :] = dvg * _gelu_grad(vc, thv)

    blk = lambda cb: pl.BlockSpec((tr, LANES), lambda p, i: (i, cb + p))
    w_spec = pl.BlockSpec((2, ch, ch), lambda p, i: (p, 0, 0))
    b_spec = pl.BlockSpec((1, ch, 2), lambda p, i: (p, 0, 0))
    g_spec = pl.BlockSpec((1, 1, LANES), lambda p, i: (p, 0, 0))
    out_blk = pl.BlockSpec((tr, LANES), lambda p, i: (i, p))
    return _pcall(body, name=name, grid=(SGU_W // LANES, s // tr),
                  in_specs=[blk(CB_UC), blk(CB_VC), blk(cb_do), w_spec, w_spec, b_spec, g_spec],
                  out_specs=[out_blk, out_blk, w_spec, b_spec, g_spec],
                  out_shape=[jax.ShapeDtypeStruct((s, SGU_W), F32)] * 2
                  + [jax.ShapeDtypeStruct(w.shape, F32), jax.ShapeDtypeStruct(b_cols.shape, F32),
                     jax.ShapeDtypeStruct(gn.shape, F32)],
                  semantics=("parallel", "arbitrary"))(proj, proj, dmixed, w, w_t, b_cols, gn)


def _pad_lanes(v):
    return jnp.zeros((1, LANES), F32).at[0, :v.shape[0]].set(v)


def _small_views(sm):
    return dict(
        n1=sm["norm1_g"][None, :], n2=sm["norm2_g"][None, :],
        b_pad=_pad_lanes(sm["b_forget"]),
        qg=jnp.tile(sm["q_norm_g"], 2)[None, :], kg=jnp.tile(sm["k_norm_g"], 2)[None, :],
        gn=sm["sgu_norm_g"].reshape(2, 1, LANES),
        w=sm["sgu_w"], w_t=jnp.swapaxes(sm["sgu_w"], 1, 2),
        b_cols=sm["sgu_b"].reshape(2, 2, SGU_CHUNK).transpose(0, 2, 1))


def _cf_rows(cf):
    return cf[:, :FOX_HEADS].T.reshape(FOX_W // LANES, 2, cf.shape[0])


def _layer_fwd(x_in, prev, mod, wts, sm, l):
    sh1, sc1, g1, sh2, sc2, g2 = mod
    v = _small_views(sm)
    if prev is None:
        x0 = x_in
        h1 = _norm_mod_fwd(x0, v["n1"], sc1, sh1, name=f"l{l}_norm1")
    else:
        x0, h1 = _resid_norm_mod_fwd(x_in, prev[0], prev[1], v["n1"], sc1, sh1, name=f"l{l}_norm1")
    proj = _matmul(h1, wts["w_in"], name=f"l{l}_proj")
    o_sb = _sb_fwd(proj, name=f"l{l}_sb_fwd")
    qn, kn, kmax = _fox_prep_fwd(proj, v["qg"], v["kg"], name=f"l{l}_fox_prep")
    cf = _forget_cumsum_fwd(proj, v["b_pad"], name=f"l{l}_cumf")
    cfr = _cf_rows(cf)
    o_fox, lse = _fox_fwd(proj, qn, kn, cf, cfr, kmax, name=f"l{l}_fox_fwd")
    o_sgu = _sgu_fwd(proj, v["w"], v["b_cols"], v["gn"], name=f"l{l}_sgu_fwd")
    mixed = jnp.concatenate([o_sb, o_fox, o_sgu], axis=1).astype(BF16)
    mo = _matmul(mixed, wts["w_out"], name=f"l{l}_wout")
    x1, h2 = _resid_norm_mod_fwd(x0, mo, g1, v["n2"], sc2, sh2, name=f"l{l}_norm2")
    a = _matmul(h2, wts["w1"], name=f"l{l}_mlp1", out_dtype=BF16)
    rr = _relu2_fwd(a, name=f"l{l}_relu2")
    m2 = _matmul(rr, wts["w2"], name=f"l{l}_mlp2")
    saved = dict(x0=x0, h1=h1, proj=proj, qn=qn, kn=kn, kmax=kmax, cf=cf, cfr=cfr, o_fox=o_fox,
                 lse=lse, mixed=mixed, mo=mo, x1=x1, h2=h2, a=a, rr=rr, m2=m2)
    return saved


def _layer_bwd(dx2, sv, mod, wts, sm, l):
    sh1, sc1, g1, sh2, sc2, g2 = mod
    v = _small_views(sm)
    dm2, dg2 = _gate_bwd(dx2, sv["m2"], g2, name=f"l{l}_gate2_bwd")
    dw2 = _matmul(sv["rr"], dm2, ta=True, name=f"l{l}_dw2")
    dr = _matmul(dm2, wts["w2"], tb=True, name=f"l{l}_dr", out_dtype=BF16)
    da = _relu2_bwd(dr, sv["a"], name=f"l{l}_relu2_bwd")
    dw1 = _matmul(sv["h2"], da, ta=True, name=f"l{l}_dw1")
    dh2 = _matmul(da, wts["w1"], tb=True, name=f"l{l}_dh2")
    dx1, dn2, dsc2, dsh2 = _norm_mod_bwd(sv["x1"], dh2, dx2, v["n2"], sc2, name=f"l{l}_norm2_bwd")
    dmo, dg1 = _gate_bwd(dx1, sv["mo"], g1, name=f"l{l}_gate1_bwd")
    dwo = _matmul(sv["mixed"], dmo, ta=True, name=f"l{l}_dwout")
    dmixed = _matmul(dmo, wts["w_out"], tb=True, name=f"l{l}_dmixed")
    proj = sv["proj"]
    dqa, dka, dva = _sb_bwd(proj, dmixed, name=f"l{l}_sb_bwd")
    dqn, dkn, dvb, dcfr, dcfq = _fox_bwd(proj, sv["qn"], sv["kn"], sv["cf"], sv["cfr"], sv["kmax"], dmixed,
                                   sv["o_fox"], sv["lse"], name=f"l{l}_fox_bwd")
    dqb, dkb, dqg, dkg = _fox_prep_bwd(proj, dqn, dkn, v["qg"], v["kg"], name=f"l{l}_fox_prep_bwd")
    s = proj.shape[0]
    dcf_heads = dcfr.reshape(FOX_HEADS, s).T + dcfq.reshape(s, FOX_HEADS, HEAD_DIM)[:, :, 0]
    dcf = jnp.zeros((s, LANES), F32).at[:, :FOX_HEADS].set(dcf_heads)
    dfl, dbf = _forget_cumsum_bwd(proj, v["b_pad"], dcf, name=f"l{l}_cumf_bwd")
    duc, dvc, dsw, dsb_cols, dgn = _sgu_bwd(proj, dmixed, v["w"], v["w_t"], v["b_cols"], v["gn"],
                                            name=f"l{l}_sgu_bwd")
    dproj = jnp.concatenate([dqa, dka, dva, dqb, dkb, dvb, duc, dvc, dfl,
                             jnp.zeros((s, LANES), F32)], axis=1).astype(BF16)
    dwin = _matmul(sv["h1"], dproj, ta=True, name=f"l{l}_dwin")
    dh1 = _matmul(dproj, wts["w_in"], tb=True, name=f"l{l}_dh1")
    dx0, dn1, dsc1, dsh1 = _norm_mod_bwd(sv["x0"], dh1, dx1, v["n1"], sc1, name=f"l{l}_norm1_bwd")
    big = dict(w_in=dwin, w_out=dwo, w1=dw1, w2=dw2)
    small = dict(norm1_g=dn1[0], norm2_g=dn2[0], b_forget=dbf[0, :FOX_HEADS],
                 q_norm_g=dqg[0, :HEAD_DIM] + dqg[0, HEAD_DIM:],
                 k_norm_g=dkg[0, :HEAD_DIM] + dkg[0, HEAD_DIM:],
                 sgu_norm_g=dgn.reshape(4, HEAD_DIM), sgu_w=dsw,
                 sgu_b=dsb_cols.transpose(0, 2, 1).reshape(4, SGU_CHUNK))
    dmod = jnp.concatenate([dsh1, dsc1, dg1, dsh2, dsc2, dg2], axis=1)
    return dx0, big, small, dmod


def _w_in_to_internal(w):
    pad = jnp.zeros((w.shape[0], PROJ_W - IN_W), w.dtype)
    return jnp.concatenate([w[:, :ATT_W], w[:, ATT_W + FOX_HEADS:], w[:, ATT_W:ATT_W + FOX_HEADS],
                            pad], axis=1)


def _w_in_from_internal(g):
    n_gate = SGU_W * 2
    return jnp.concatenate([g[:, :ATT_W], g[:, ATT_W + n_gate:ATT_W + n_gate + FOX_HEADS],
                            g[:, ATT_W:ATT_W + n_gate]], axis=1)


def _dma_chunks(rows):
    for n in (16, 8, 4, 2):
        if rows >= 2048 and rows % (16 * n) == 0:
            return n
    return 1


def _exchange(x, masks, slot_shift, slot_bits, scatter, *, name):
    n_slots = 2 ** slot_bits
    blk_shape = x.shape[1:] if scatter else x.shape
    n_peers = len(masks)
    n_ch = _dma_chunks(blk_shape[0])
    rc = blk_shape[0] // n_ch

    def body(x_ref, out_ref, send_sems, recv_sems, local_sems):
        ids = (lax.axis_index("x"), lax.axis_index("y"), lax.axis_index("c"))
        me = 4 * ids[0] + 2 * ids[1] + ids[2]
        my_slot = (me >> slot_shift) & (n_slots - 1)

        def peer(mask):
            return tuple(1 - v if (mask >> b) & 1 else v for v, b in zip(ids, (2, 1, 0)))

        def src_for(slot, ch):
            blk = x_ref.at[slot] if scatter else x_ref
            return blk.at[pl.ds(ch * rc, rc)]

        def dst_for(ch):
            return out_ref.at[my_slot].at[pl.ds(ch * rc, rc)]

        copies = [pltpu.make_async_copy(src_for(my_slot, ch), dst_for(ch), local_sems.at[ch])
                  for ch in range(n_ch)]
        for kk, mask in enumerate(masks):
            peer_slot = ((me ^ mask) >> slot_shift) & (n_slots - 1)
            for ch in range(n_ch):
                copies.append(pltpu.make_async_remote_copy(
                    src_ref=src_for(peer_slot, ch), dst_ref=dst_for(ch),
                    send_sem=send_sems.at[kk * n_ch + ch], recv_sem=recv_sems.at[kk * n_ch + ch],
                    device_id=peer(mask), device_id_type=MESH))
        for cp in copies:
            cp.start()
        for cp in copies:
            cp.wait()

    any_spec = pl.BlockSpec(memory_space=pl.ANY)
    return _pcall(body, name=name, in_specs=[any_spec], out_specs=any_spec,
                  out_shape=jax.ShapeDtypeStruct((n_slots,) + tuple(blk_shape), x.dtype),
                  scratch_shapes=[pltpu.SemaphoreType.DMA((n_peers * n_ch,)),
                                  pltpu.SemaphoreType.DMA((n_peers * n_ch,)),
                                  pltpu.SemaphoreType.DMA((n_ch,))])(x)


def _gather_chips(x, *, name):
    return _exchange(x, (2, 4, 6), 1, 2, False, name=name)


def _gather_all(x, *, name):
    return _exchange(x, (1, 2, 3, 4, 5, 6, 7), 0, 3, False, name=name)


def _scatter_chips(x4, *, name):
    return _exchange(x4, (2, 4, 6), 1, 2, True, name=name)


def _swap_cores(x, *, name):
    return _exchange(x, (1,), 0, 1, False, name=name)


def _scatter_cores(x2, *, name):
    return _exchange(x2, (1,), 0, 1, True, name=name)


def _sum_slots(parts, *, name, out_dtype=F32, tr=256):
    n, rows, cols = parts.shape
    tr = min(tr, rows)
    assert rows % tr == 0, (name, rows, tr)

    def body(p_ref, o_ref):
        acc = p_ref[0].astype(F32)
        for kk in range(1, n):
            acc = acc + p_ref[kk].astype(F32)
        o_ref[...] = acc.astype(o_ref.dtype)

    return _pcall(body, name=name, grid=(rows // tr,),
                  in_specs=[pl.BlockSpec((n, tr, cols), lambda i: (0, i, 0))],
                  out_specs=pl.BlockSpec((tr, cols), lambda i: (i, 0)),
                  out_shape=jax.ShapeDtypeStruct((rows, cols), out_dtype),
                  semantics=("parallel",))(parts)


def _adamw(w, m, v, parts, *, name, tr=256):
    n, rows, cols = parts.shape
    tr = min(tr, rows)
    assert rows % tr == 0, (name, rows, tr)
    c1 = 1.0 - ADAM_B1 ** ADAM_STEP
    c2 = 1.0 - ADAM_B2 ** ADAM_STEP

    def body(w_ref, m_ref, v_ref, p_ref, g_ref, d_ref, nm_ref, nv_ref):
        g = p_ref[0]
        for kk in range(1, n):
            g = g + p_ref[kk]
        nm = ADAM_B1 * m_ref[...] + (1.0 - ADAM_B1) * g
        nv = ADAM_B2 * v_ref[...] + (1.0 - ADAM_B2) * (g * g)
        g_ref[...] = g
        nm_ref[...] = nm
        nv_ref[...] = nv
        d_ref[...] = -ADAM_LR * ((nm / c1) / (jnp.sqrt(nv / c2) + ADAM_EPS) + ADAM_WD * w_ref[...])

    spec = pl.BlockSpec((tr, cols), lambda i: (i, 0))
    return _pcall(body, name=name, grid=(rows // tr,),
                  in_specs=[spec, spec, spec, pl.BlockSpec((n, tr, cols), lambda i: (0, i, 0))],
                  out_specs=[spec] * 4,
                  out_shape=[jax.ShapeDtypeStruct((rows, cols), F32)] * 4,
                  semantics=("parallel",))(w, m, v, parts)


def _silu(c):
    return c / (1.0 + jnp.exp(-c))


def _ada_fwd(c_all, ada_w, ada_b_sh, *, name):
    nl, d, wsh = ada_w.shape

    def body(c_ref, w_ref, b_ref, o_ref):
        cond = _silu(c_ref[...]).astype(BF16)
        o_ref[0] = _dot(cond, w_ref[0].astype(BF16)) + b_ref[0]

    return _pcall(body, name=name, grid=(nl,),
                  in_specs=[pl.BlockSpec(c_all.shape, lambda l: (0, 0)),
                            pl.BlockSpec((1, d, wsh), lambda l: (l, 0, 0)),
                            pl.BlockSpec((1, 1, wsh), lambda l: (l, 0, 0))],
                  out_specs=pl.BlockSpec((1, c_all.shape[0], wsh), lambda l: (l, 0, 0)),
                  out_shape=jax.ShapeDtypeStruct((nl, c_all.shape[0], wsh), F32),
                  semantics=("parallel",))(c_all, ada_w, ada_b_sh)


def _ada_bwd(c_all, dmod_sh, *, name):
    nl, nb, wsh = dmod_sh.shape
    d = c_all.shape[1]

    def body(c_ref, dm_ref, o_ref):
        cond = _silu(c_ref[...]).astype(BF16)
        o_ref[0] = _dot_tn(cond, dm_ref[0].astype(BF16))

    return _pcall(body, name=name, grid=(nl,),
                  in_specs=[pl.BlockSpec(c_all.shape, lambda l: (0, 0)),
                            pl.BlockSpec((1, nb, wsh), lambda l: (l, 0, 0))],
                  out_specs=pl.BlockSpec((1, d, wsh), lambda l: (l, 0, 0)),
                  out_shape=jax.ShapeDtypeStruct((nl, d, wsh), F32),
                  semantics=("parallel",))(c_all, dmod_sh)


SMALL_NAMES = ("norm1_g", "norm2_g", "b_forget", "q_norm_g", "k_norm_g", "sgu_norm_g", "sgu_w",
               "sgu_b")
WEIGHT_NAMES = ("ada_w", "ada_b", "norm1_g", "norm2_g", "w_in", "b_forget", "q_norm_g", "k_norm_g",
                "sgu_norm_g", "sgu_w", "sgu_b", "w_out", "mlp_w1", "mlp_w2")


SMALL_TILE_ROWS = 256


def _pack_small(tree):
    flat = jnp.concatenate([tree[n].reshape(-1) for n in SMALL_NAMES])
    n = flat.shape[0]
    rows = -(-n // (SMALL_TILE_ROWS * LANES)) * SMALL_TILE_ROWS
    return jnp.zeros((rows * LANES,), F32).at[:n].set(flat).reshape(rows, LANES)


def _unpack_small(packed, like):
    flat = packed.reshape(-1)
    out, off = {}, 0
    for n in SMALL_NAMES:
        size = like[n].size
        out[n] = flat[off:off + size].reshape(like[n].shape)
        off += size
    return out


def kernel(x, c, ada_w, ada_b, norm1_g, norm2_g, w_in, b_forget, q_norm_g, k_norm_g, sgu_norm_g, sgu_w, sgu_b, w_out, mlp_w1, mlp_w2, loss_target, m_ada_w, m_ada_b, m_norm1_g, m_norm2_g, m_w_in, m_b_forget, m_q_norm_g, m_k_norm_g, m_sgu_norm_g, m_sgu_w, m_sgu_b, m_w_out, m_mlp_w1, m_mlp_w2, v_ada_w, v_ada_b, v_norm1_g, v_norm2_g, v_w_in, v_b_forget, v_q_norm_g, v_k_norm_g, v_sgu_norm_g, v_sgu_w, v_sgu_b, v_w_out, v_mlp_w1, v_mlp_w2):
    w = dict(ada_w=ada_w, ada_b=ada_b, norm1_g=norm1_g, norm2_g=norm2_g, w_in=w_in,
             b_forget=b_forget, q_norm_g=q_norm_g, k_norm_g=k_norm_g, sgu_norm_g=sgu_norm_g,
             sgu_w=sgu_w, sgu_b=sgu_b, w_out=w_out, mlp_w1=mlp_w1, mlp_w2=mlp_w2)
    mom = dict(ada_w=m_ada_w, ada_b=m_ada_b, norm1_g=m_norm1_g, norm2_g=m_norm2_g, w_in=m_w_in,
               b_forget=m_b_forget, q_norm_g=m_q_norm_g, k_norm_g=m_k_norm_g,
               sgu_norm_g=m_sgu_norm_g, sgu_w=m_sgu_w, sgu_b=m_sgu_b, w_out=m_w_out,
               mlp_w1=m_mlp_w1, mlp_w2=m_mlp_w2)
    var = dict(ada_w=v_ada_w, ada_b=v_ada_b, norm1_g=v_norm1_g, norm2_g=v_norm2_g, w_in=v_w_in,
               b_forget=v_b_forget, q_norm_g=v_q_norm_g, k_norm_g=v_k_norm_g,
               sgu_norm_g=v_sgu_norm_g, sgu_w=v_sgu_w, sgu_b=v_sgu_b, w_out=v_w_out,
               mlp_w1=v_mlp_w1, mlp_w2=v_mlp_w2)
    depth, d = norm1_g.shape
    chip = 2 * lax.axis_index("x") + lax.axis_index("y")
    me = 2 * chip + lax.axis_index("c")
    n_chips = 4
    ada_sh = ada_w.shape[2]

    core = lax.axis_index("c")
    half_l = depth // 2

    def gather_weight(w_sh, name):
        _, r, cols = w_sh.shape
        mine = lax.dynamic_slice_in_dim(w_sh, core * half_l, half_l, axis=0).astype(BF16)
        got = _gather_chips(mine.reshape(half_l * r, cols), name=f"gather_{name}")
        both = _swap_cores(got.reshape(n_chips * half_l * r, cols), name=f"share_{name}")
        both = both.reshape(2, n_chips, half_l, r, cols).transpose(1, 0, 2, 3, 4)
        return both.reshape(n_chips, depth, r, cols)

    g_in = gather_weight(w_in, "w_in")
    g_out = gather_weight(w_out, "w_out")
    g_w1 = gather_weight(mlp_w1, "w1")
    g_w2 = gather_weight(mlp_w2, "w2")
    layer_w = []
    for l in range(depth):
        layer_w.append(dict(
            w_in=_w_in_to_internal(jnp.concatenate([g_in[k, l] for k in range(n_chips)], axis=1)),
            w_out=g_out[:, l].reshape(d, d),
            w1=jnp.concatenate([g_w1[k, l] for k in range(n_chips)], axis=1),
            w2=g_w2[:, l].reshape(D_FF, d)))

    c_all = _gather_all(jnp.zeros((8, d), F32).at[0].set(c[0]), name="gather_c")[:, 0]
    c_pad = jnp.concatenate([c_all, jnp.zeros_like(c_all)], axis=0)
    ada_b_sh = lax.dynamic_slice_in_dim(ada_b, chip * ada_sh, ada_sh, axis=1)[:, None, :]
    mod_sh = _ada_fwd(c_pad, ada_w, ada_b_sh, name="ada_fwd")
    mod_all = _gather_chips(mod_sh, name="gather_mod")
    mod_me = lax.dynamic_index_in_dim(mod_all, me, axis=2, keepdims=False)
    mod_me = mod_me.transpose(1, 0, 2).reshape(depth, 6, 1, d)

    saved = []
    xs, prev = x[0], None
    for l in range(depth):
        mod = [mod_me[l, kk] for kk in range(6)]
        sm = {n: w[n][l] for n in SMALL_NAMES}
        sv = _layer_fwd(xs, prev, mod, layer_w[l], sm, l)
        saved.append(sv)
        xs, prev = sv["x1"], (sv["m2"], mod[5])

    sq, dxs = _loss_fwd_bwd(xs, prev[0], prev[1], loss_target[0], name="loss")
    loss = lax.psum(0.5 * jnp.sum(sq) / d, ("x", "y", "c"))

    big = {n: [] for n in ("w_in", "w_out", "w1", "w2")}
    small = {n: [] for n in SMALL_NAMES}
    dmods = []
    for l in reversed(range(depth)):
        mod = [mod_me[l, kk] for kk in range(6)]
        sm = {n: w[n][l] for n in SMALL_NAMES}
        dxs, bg, smg, dmod = _layer_bwd(dxs, saved[l], mod, layer_w[l], sm, l)
        for n in big:
            big[n].insert(0, bg[n])
        for n in SMALL_NAMES:
            small[n].insert(0, smg[n])
        dmods.insert(0, dmod)
    grad_x = dxs[None]

    out_g, out_d, out_m, out_v = {}, {}, {}, {}

    def run_adamw(name, parts2d, shape):
        rows, cols = parts2d.shape[1:]
        g, dl, nm, nv = _adamw(w[name].reshape(rows, cols), mom[name].reshape(rows, cols),
                               var[name].reshape(rows, cols), parts2d, name=f"adamw_{name}")
        out_g[name], out_d[name] = g.reshape(shape), dl.reshape(shape)
        out_m[name], out_v[name] = nm.reshape(shape), nv.reshape(shape)

    def shards_of(name, l):
        if name == "w_in":
            g = _w_in_from_internal(big["w_in"][l])
            return jnp.stack(jnp.split(g, n_chips, axis=1))
        if name == "mlp_w1":
            return jnp.stack(jnp.split(big["w1"][l], n_chips, axis=1))
        if name == "w_out":
            return big["w_out"][l].reshape(n_chips, d // n_chips, d)
        return big["w2"][l].reshape(n_chips, D_FF // n_chips, d)

    for name in ("w_in", "w_out", "mlp_w1", "mlp_w2"):
        per_chip = jnp.stack([shards_of(name, l) for l in range(depth)], axis=1)
        r, cols = per_chip.shape[2:]
        half_rows = half_l * r
        by_core = per_chip.reshape(n_chips, 2, half_rows, cols).transpose(1, 0, 2, 3)
        pair = _scatter_cores(by_core.reshape(2, n_chips * half_rows, cols), name=f"pair_{name}")
        chip_sum = _sum_slots(pair, out_dtype=BF16, name=f"pairsum_{name}")
        got = _scatter_chips(chip_sum.reshape(n_chips, half_rows, cols), name=f"scatter_{name}")
        half = _sum_slots(got, name=f"sum_{name}")
        both = _swap_cores(half, name=f"swap_{name}")
        run_adamw(name, both.reshape(1, depth * r, cols), w[name].shape)

    small_tree = {n: jnp.stack(small[n]) for n in SMALL_NAMES}
    gathered = _gather_all(_pack_small(small_tree), name="gather_small")
    gs, ds_, ms, vs = _adamw(_pack_small({n: w[n] for n in SMALL_NAMES}),
                             _pack_small({n: mom[n] for n in SMALL_NAMES}),
                             _pack_small({n: var[n] for n in SMALL_NAMES}), gathered,
                             name="adamw_small")
    like = {n: w[n] for n in SMALL_NAMES}
    for tree, packed in ((out_g, gs), (out_d, ds_), (out_m, ms), (out_v, vs)):
        tree.update(_unpack_small(packed, like))

    dmod_mine = jnp.concatenate(dmods, axis=0)
    dmod_all = _gather_all(jnp.zeros((depth, 8, 6 * d), F32).at[:, 0].set(dmod_mine),
                           name="gather_dmod")[:, :, 0]
    dmod_lb = dmod_all.transpose(1, 0, 2)
    dmod_sh = lax.dynamic_slice_in_dim(dmod_lb, chip * ada_sh, ada_sh, axis=2)
    dmod_sh = jnp.concatenate([dmod_sh, jnp.zeros_like(dmod_sh)], axis=1)
    g_ada_w = _ada_bwd(c_pad, dmod_sh, name="ada_bwd")
    run_adamw("ada_w", g_ada_w.reshape(1, depth * d, ada_sh), ada_w.shape)
    parts_b = dmod_all.reshape(8, depth * 6 * d // LANES, LANES)
    run_adamw("ada_b", parts_b, ada_b.shape)

    outs = [loss, grad_x]
    for tree in (out_g, out_d, out_m, out_v):
        outs += [tree[n] for n in WEIGHT_NAMES]
    return tuple(outs)
```

```python
import functools
import math

import jax
import jax.numpy as jnp
from jax import lax
from jax.experimental import pallas as pl
from jax.experimental.pallas import tpu as pltpu

F32 = jnp.float32
BF16 = jnp.bfloat16

D_MODEL = 1024
DEPTH = 4
HEAD_DIM = 64
LANES = 128
D_FF = 4 * D_MODEL
EPS = 1e-6
SB_W, FOX_W, SGU_W = 256, 512, 256
FOX_HEADS = 8
SGU_CHUNK = 128
IN_W = 2824
ATT_W = 3 * SB_W + 3 * FOX_W
PROJ_W = 3072
CB_QA, CB_KA, CB_VA = 0, 2, 4
CB_QB, CB_KB, CB_VB = 6, 10, 14
CB_UC, CB_VC, CB_FL = 18, 20, 22
ATT_T = 256
VMEM_LIMIT = 56 * 2 ** 20
SKIP_LOG = 110.0

ADAM_LR, ADAM_B1, ADAM_B2, ADAM_EPS, ADAM_WD, ADAM_STEP = 0.001, 0.9, 0.999, 1e-08, 0.01, 10

MESH = pl.DeviceIdType.MESH


def _pcall(body, *, name, out_shape, grid=(), in_specs=None, out_specs=None, scratch_shapes=(),
           semantics=None):
    params = dict(vmem_limit_bytes=VMEM_LIMIT)
    if semantics is not None:
        params["dimension_semantics"] = semantics
    kwargs = {}
    if in_specs is not None:
        kwargs["in_specs"] = in_specs
    if out_specs is not None:
        kwargs["out_specs"] = out_specs
    return pl.pallas_call(body, name=name, out_shape=out_shape, grid=grid,
                          scratch_shapes=list(scratch_shapes),
                          compiler_params=pltpu.CompilerParams(**params), **kwargs)


def _dot(a, b):
    return jnp.dot(a, b, preferred_element_type=F32)


def _dot_nt(a, b):
    return lax.dot_general(a, b, (((1,), (1,)), ((), ())), preferred_element_type=F32)


def _dot_tn(a, b):
    return lax.dot_general(a, b, (((0,), (0,)), ((), ())), preferred_element_type=F32)


def _split2(x):
    hi = x.astype(BF16)
    lo = (x - hi.astype(F32)).astype(BF16)
    return hi, lo


def _ones_dot(x, ones_bf16):
    hi, lo = _split2(x)
    return _dot(hi, ones_bf16) + _dot(lo, ones_bf16)


def _rowwise(fn, fulls, vecs, out_dtypes, n_vec_out, *, name, tr):
    s, n = fulls[0].shape
    tr = min(tr, s)
    assert s % tr == 0, (name, s, tr)
    nf, nv, nfo = len(fulls), len(vecs), len(out_dtypes)

    def body(*refs):
        fi, vi = refs[:nf], refs[nf:nf + nv]
        fo, vo = refs[nf + nv:nf + nv + nfo], refs[nf + nv + nfo:]
        outs_f, outs_v = fn([r[...] for r in fi], [r[...] for r in vi])
        for r, o in zip(fo, outs_f):
            r[...] = o.astype(r.dtype)
        if n_vec_out:
            @pl.when(pl.program_id(0) == 0)
            def _():
                for r in vo:
                    r[...] = jnp.zeros_like(r)
            for r, o in zip(vo, outs_v):
                r[...] += o

    full_spec = pl.BlockSpec((tr, n), lambda i: (i, 0))
    vec_specs = [pl.BlockSpec(v.shape, lambda i: (0, 0)) for v in vecs]
    out_vec_spec = pl.BlockSpec((1, n), lambda i: (0, 0))
    out_shape = [jax.ShapeDtypeStruct((s, n), dt) for dt in out_dtypes]
    out_shape += [jax.ShapeDtypeStruct((1, n), F32)] * n_vec_out
    outs = _pcall(body, name=name, grid=(s // tr,),
                  in_specs=[full_spec] * nf + vec_specs,
                  out_specs=[full_spec] * nfo + [out_vec_spec] * n_vec_out,
                  out_shape=out_shape,
                  semantics=("arbitrary",) if n_vec_out else ("parallel",))(*fulls, *vecs)
    return outs[:nfo], outs[nfo:]


def _colsum(t):
    return jnp.sum(t, axis=0, keepdims=True)


def _rms_mod(x, g, sc, sh):
    r = lax.rsqrt(jnp.mean(x * x, axis=-1, keepdims=True) + EPS)
    return (x * r * g) * (1.0 + sc) + sh


def _norm_mod_fwd(x, g, sc, sh, *, name):
    def fn(f, v):
        return [_rms_mod(f[0], v[0], v[1], v[2])], []
    (h,), _ = _rowwise(fn, [x], [g, sc, sh], [BF16], 0, name=name, tr=512)
    return h


def _resid_norm_mod_fwd(x, m, gate, g, sc, sh, *, name):
    def fn(f, v):
        xn = f[0] + v[0] * f[1]
        return [xn, _rms_mod(xn, v[1], v[2], v[3])], []
    (xn, h), _ = _rowwise(fn, [x, m], [gate, g, sc, sh], [F32, BF16], 0, name=name, tr=512)
    return xn, h


def _norm_mod_bwd(x, dh, dres, g, sc, *, name):
    def fn(f, v):
        xv, dhv, dr = f
        gv, scv = v
        r = lax.rsqrt(jnp.mean(xv * xv, axis=-1, keepdims=True) + EPS)
        xh = xv * r
        dn = dhv * (1.0 + scv)
        dxh = dn * gv
        dx = dr + r * (dxh - xh * jnp.mean(dxh * xh, axis=-1, keepdims=True))
        return [dx], [_colsum(dn * xh), _colsum(dhv * (xh * gv)), _colsum(dhv)]
    (dx,), (dg, dsc, dsh) = _rowwise(fn, [x, dh, dres], [g, sc], [F32], 3, name=name, tr=256)
    return dx, dg, dsc, dsh


def _gate_bwd(dx, m, gate, *, name):
    def fn(f, v):
        return [f[0] * v[0]], [_colsum(f[0] * f[1])]
    (dm,), (dgate,) = _rowwise(fn, [dx, m], [gate], [BF16], 1, name=name, tr=512)
    return dm, dgate


def _relu2_fwd(a, *, name):
    def fn(f, v):
        r = jnp.maximum(f[0], 0.0)
        return [r * r], []
    (r,), _ = _rowwise(fn, [a], [], [BF16], 0, name=name, tr=128)
    return r


def _relu2_bwd(dr, a, *, name):
    def fn(f, v):
        return [f[0] * (2.0 * jnp.maximum(f[1], 0.0))], []
    (da,), _ = _rowwise(fn, [dr, a], [], [BF16], 0, name=name, tr=128)
    return da


def _loss_fwd_bwd(x, m, gate, target, *, name):
    n = x.shape[1]

    def fn(f, v):
        err = f[0] + v[0] * f[1] - f[2]
        return [err * (1.0 / n)], [_colsum(err * err)]
    (dy,), (sq,) = _rowwise(fn, [x, m, target], [gate], [F32], 1, name=name, tr=512)
    return sq, dy


def _matmul(a, b, *, name, ta=False, tb=False, out_dtype=F32, tm=1024, tn=1024, tk=1024):
    m = a.shape[1] if ta else a.shape[0]
    k = a.shape[0] if ta else a.shape[1]
    n = b.shape[0] if tb else b.shape[1]
    assert k == (b.shape[1] if tb else b.shape[0])
    tm, tn, tk = min(tm, m), min(tn, n), min(tk, k)
    assert m % tm == 0 and n % tn == 0 and k % tk == 0, (name, m, n, k)
    nk = k // tk
    dims = (((0 if ta else 1,), (1 if tb else 0,)), ((), ()))

    def body(a_ref, b_ref, o_ref, acc_ref):
        kk = pl.program_id(2)

        @pl.when(kk == 0)
        def _():
            acc_ref[...] = jnp.zeros_like(acc_ref)
        acc_ref[...] += lax.dot_general(a_ref[...].astype(BF16), b_ref[...].astype(BF16), dims,
                                        preferred_element_type=F32)

        @pl.when(kk == nk - 1)
        def _():
            o_ref[...] = acc_ref[...].astype(o_ref.dtype)

    a_spec = (pl.BlockSpec((tk, tm), lambda i, j, kk: (kk, i)) if ta
              else pl.BlockSpec((tm, tk), lambda i, j, kk: (i, kk)))
    b_spec = (pl.BlockSpec((tn, tk), lambda i, j, kk: (j, kk)) if tb
              else pl.BlockSpec((tk, tn), lambda i, j, kk: (kk, j)))
    return _pcall(body, name=name, grid=(m // tm, n // tn, nk),
                  in_specs=[a_spec, b_spec],
                  out_specs=pl.BlockSpec((tm, tn), lambda i, j, kk: (i, j)),
                  out_shape=jax.ShapeDtypeStruct((m, n), out_dtype),
                  scratch_shapes=[pltpu.VMEM((tm, tn), F32)],
                  semantics=("parallel", "parallel", "arbitrary"))(a, b)


def _lane_masks():
    lane = lax.broadcasted_iota(jnp.int32, (1, LANES), 1)
    return [lane < HEAD_DIM, lane >= HEAD_DIM]


def _tri_iotas(t):
    r = lax.broadcasted_iota(jnp.int32, (t, t), 0)
    c = lax.broadcasted_iota(jnp.int32, (t, t), 1)
    return r, c


def _rows(j, t):
    return pl.ds(pl.multiple_of(j * t, t), t)


def _neg_softplus(z):
    e = jnp.exp(-jnp.abs(z))
    return -(jnp.maximum(z, 0.0) + jnp.log(1.0 + e)), e


def _live(e_runs):
    return jnp.maximum(jnp.max(e_runs[0]), jnp.max(e_runs[1])) > -SKIP_LOG


def _sb_fwd(proj, *, name):
    s = proj.shape[0]
    t = min(ATT_T, s)
    scale = HEAD_DIM ** -0.5

    def body(q_ref, k_ref, v_ref, o_ref):
        i = pl.program_id(1)
        hm = _lane_masks()
        q = q_ref[...] * scale
        qh = [jnp.where(mk, q, 0.0).astype(BF16) for mk in hm]
        r, c = _tri_iotas(t)
        later = (r > c).astype(BF16)
        causal = c < r

        def chunk(j, carry, masked):
            kb = k_ref[_rows(j, t), :].astype(BF16)
            vb = v_ref[_rows(j, t), :].astype(BF16)
            out = []
            for h in range(2):
                e_run, acc = carry[h]
                z = _dot_nt(qh[h], kb)
                l, _ = _neg_softplus(z)
                if masked:
                    l = jnp.where(causal, l, 0.0)
                between = _ones_dot(l, later) + e_run
                a = jnp.exp(z + l + between)
                if masked:
                    a = jnp.where(causal, a, 0.0)
                acc = acc + _dot(a.astype(BF16), vb)
                out.append((e_run + jnp.sum(l, axis=1, keepdims=True), acc))
            return tuple(out)

        init = tuple((jnp.zeros((t, 1), F32), jnp.zeros((t, LANES), F32)) for _ in range(2))
        carry = chunk(i, init, True)

        def cond(st):
            return (st[0] >= 0) & _live([st[1][0][0], st[1][1][0]])

        _, carry = lax.while_loop(cond, lambda st: (st[0] - 1, chunk(st[0], st[1], False)),
                                  (i - 1, carry))
        o_ref[...] = jnp.where(hm[0], carry[0][1], carry[1][1])

    blk = lambda cb: pl.BlockSpec((t, LANES), lambda p, i: (i, cb + p))
    full = lambda cb: pl.BlockSpec((s, LANES), lambda p, i: (0, cb + p))
    out_blk = pl.BlockSpec((t, LANES), lambda p, i: (i, p))
    return _pcall(body, name=name, grid=(SB_W // LANES, s // t),
                  in_specs=[blk(CB_QA), full(CB_KA), full(CB_VA)],
                  out_specs=out_blk,
                  out_shape=jax.ShapeDtypeStruct((s, SB_W), F32),
                  semantics=("parallel", "arbitrary"))(proj, proj, proj)


def _sb_bwd(proj, dmixed, *, name):
    s = proj.shape[0]
    t = min(ATT_T, s)
    scale = HEAD_DIM ** -0.5

    def body(q_ref, k_ref, v_ref, do_ref, dq_ref, dk_ref, dv_ref):
        i = pl.program_id(1)

        @pl.when(i == 0)
        def _():
            dk_ref[...] = jnp.zeros_like(dk_ref)
            dv_ref[...] = jnp.zeros_like(dv_ref)

        hm = _lane_masks()
        q = q_ref[...] * scale
        do = do_ref[...]
        qh = [jnp.where(mk, q, 0.0).astype(BF16) for mk in hm]
        doh = [jnp.where(mk, do, 0.0).astype(BF16) for mk in hm]
        r, c = _tri_iotas(t)
        upto = (r <= c).astype(BF16)
        before = (r < c).astype(BF16)
        causal = c < r

        def totals(j, e_runs, masked):
            kb = k_ref[_rows(j, t), :].astype(BF16)
            out = []
            for h in range(2):
                l, _ = _neg_softplus(_dot_nt(qh[h], kb))
                if masked:
                    l = jnp.where(causal, l, 0.0)
                out.append(e_runs[h] + jnp.sum(l, axis=1, keepdims=True))
            return tuple(out)

        lt = totals(i, (jnp.zeros((t, 1), F32),) * 2, True)
        j_stop, lt = lax.while_loop(lambda st: (st[0] >= 0) & _live(st[1]),
                                    lambda st: (st[0] - 1, totals(st[0], st[1], False)),
                                    (i - 1, lt))

        def chunk(j, carry, masked):
            kf = k_ref[_rows(j, t), :]
            kb = kf.astype(BF16)
            vb = v_ref[_rows(j, t), :].astype(BF16)
            out = []
            dk_add = jnp.zeros((t, LANES), F32)
            dv_add = jnp.zeros((t, LANES), F32)
            for h in range(2):
                l_run, g_run, dq = carry[h]
                kh = jnp.where(hm[h], kf, 0.0).astype(BF16)
                z = _dot_nt(qh[h], kb)
                l, e = _neg_softplus(z)
                beta = jnp.where(z >= 0.0, 1.0, e) / (1.0 + e)
                if masked:
                    l = jnp.where(causal, l, 0.0)
                prefix = _ones_dot(l, upto) + l_run
                a = jnp.exp(z + l + (lt[h] - prefix))
                if masked:
                    a = jnp.where(causal, a, 0.0)
                g = a * _dot_nt(doh[h], vb)
                g_before = _ones_dot(g, before) + g_run
                dz = g * (1.0 - beta) - beta * g_before
                if masked:
                    dz = jnp.where(causal, dz, 0.0)
                dzb = dz.astype(BF16)
                dq = dq + _dot(dzb, kh)
                dk_add = dk_add + _dot_tn(dzb, qh[h])
                dv_add = dv_add + _dot_tn(a.astype(BF16), doh[h])
                out.append((l_run + jnp.sum(l, axis=1, keepdims=True),
                            g_run + jnp.sum(g, axis=1, keepdims=True), dq))
            dk_ref[_rows(j, t), :] += dk_add
            dv_ref[_rows(j, t), :] += dv_add
            return tuple(out)

        init = tuple((jnp.zeros((t, 1), F32), jnp.zeros((t, 1), F32), jnp.zeros((t, LANES), F32))
                     for _ in range(2))
        carry = lax.fori_loop(j_stop + 1, i, lambda j, cr: chunk(j, cr, False), init)
        carry = chunk(i, carry, True)
        dq_ref[...] = (carry[0][2] + carry[1][2]) * scale

    blk = lambda cb: pl.BlockSpec((t, LANES), lambda p, i: (i, cb + p))
    full = lambda cb: pl.BlockSpec((s, LANES), lambda p, i: (0, cb + p))
    out_blk = pl.BlockSpec((t, LANES), lambda p, i: (i, p))
    out_full = pl.BlockSpec((s, LANES), lambda p, i: (0, p))
    return _pcall(body, name=name, grid=(SB_W // LANES, s // t),
                  in_specs=[blk(CB_QA), full(CB_KA), full(CB_VA), blk(0)],
                  out_specs=[out_blk, out_full, out_full],
                  out_shape=[jax.ShapeDtypeStruct((s, SB_W), F32)] * 3,
                  semantics=("parallel", "arbitrary"))(proj, proj, proj, dmixed)


def _group_mean(v, lo):
    s0 = jnp.sum(jnp.where(lo, v, 0.0), axis=1, keepdims=True)
    s1 = jnp.sum(jnp.where(lo, 0.0, v), axis=1, keepdims=True)
    return jnp.where(lo, s0, s1) * (1.0 / HEAD_DIM)


def _fox_prep_fwd(proj, qg, kg, *, name):
    s = proj.shape[0]
    tr = min(512, s)

    def body(q_ref, k_ref, qg_ref, kg_ref, qn_ref, kn_ref, kmax_ref):
        lo = _lane_masks()[0]
        for x_ref, g_ref, o_ref in ((q_ref, qg_ref, qn_ref), (k_ref, kg_ref, kn_ref)):
            x = x_ref[...]
            o_ref[...] = x * lax.rsqrt(_group_mean(x * x, lo) + EPS) * g_ref[...]

        @pl.when(pl.program_id(1) == 0)
        def _():
            kmax_ref[...] = jnp.zeros_like(kmax_ref)
        kn = kn_ref[...]
        norms = jnp.sqrt(_group_mean(kn * kn, lo) * HEAD_DIM)
        kmax_ref[...] = jnp.maximum(kmax_ref[...], jnp.max(norms, axis=0, keepdims=True))

    blk = lambda cb: pl.BlockSpec((tr, LANES), lambda p, i: (i, cb + p))
    vec = pl.BlockSpec((1, LANES), lambda p, i: (0, 0))
    out_blk = pl.BlockSpec((tr, LANES), lambda p, i: (i, p))
    return _pcall(body, name=name, grid=(FOX_W // LANES, s // tr),
                  in_specs=[blk(CB_QB), blk(CB_KB), vec, vec],
                  out_specs=[out_blk, out_blk, pl.BlockSpec((1, LANES), lambda p, i: (0, p))],
                  out_shape=[jax.ShapeDtypeStruct((s, FOX_W), F32)] * 2
                  + [jax.ShapeDtypeStruct((1, FOX_W), F32)],
                  semantics=("parallel", "arbitrary"))(proj, proj, qg, kg)


def _fox_prep_bwd(proj, dqn, dkn, qg, kg, *, name):
    s = proj.shape[0]
    tr = min(512, s)

    def body(q_ref, k_ref, dqn_ref, dkn_ref, qg_ref, kg_ref, dq_ref, dk_ref, dqg_ref, dkg_ref):
        @pl.when((pl.program_id(0) == 0) & (pl.program_id(1) == 0))
        def _():
            dqg_ref[...] = jnp.zeros_like(dqg_ref)
            dkg_ref[...] = jnp.zeros_like(dkg_ref)

        lo = _lane_masks()[0]
        for x_ref, dy_ref, g_ref, dx_ref, dg_ref in ((q_ref, dqn_ref, qg_ref, dq_ref, dqg_ref),
                                                     (k_ref, dkn_ref, kg_ref, dk_ref, dkg_ref)):
            x, dy = x_ref[...], dy_ref[...]
            r = lax.rsqrt(_group_mean(x * x, lo) + EPS)
            xh = x * r
            dxh = dy * g_ref[...]
            dx_ref[...] = r * (dxh - xh * _group_mean(dxh * xh, lo))
            dg_ref[...] += _colsum(dy * xh)

    blk = lambda cb: pl.BlockSpec((tr, LANES), lambda p, i: (i, cb + p))
    vec = pl.BlockSpec((1, LANES), lambda p, i: (0, 0))
    out_blk = pl.BlockSpec((tr, LANES), lambda p, i: (i, p))
    return _pcall(body, name=name, grid=(FOX_W // LANES, s // tr),
                  in_specs=[blk(CB_QB), blk(CB_KB), out_blk, out_blk, vec, vec],
                  out_specs=[out_blk, out_blk, vec, vec],
                  out_shape=[jax.ShapeDtypeStruct((s, FOX_W), F32)] * 2
                  + [jax.ShapeDtypeStruct((1, LANES), F32)] * 2,
                  semantics=("arbitrary", "arbitrary"))(proj, proj, dqn, dkn, qg, kg)


def _split3_dot(tri_bf16, x):
    hi = x.astype(BF16)
    r1 = x - hi.astype(F32)
    mid = r1.astype(BF16)
    lo = (r1 - mid.astype(F32)).astype(BF16)
    return _dot(tri_bf16, hi) + _dot(tri_bf16, mid) + _dot(tri_bf16, lo)


def _forget_cumsum_fwd(proj, b_pad, *, name):
    s = proj.shape[0]
    tb = min(256, s)

    def body(fl_ref, b_ref, cf_ref, run_ref):
        @pl.when(pl.program_id(0) == 0)
        def _():
            run_ref[...] = jnp.zeros_like(run_ref)
        lf, _ = _neg_softplus(-(fl_ref[...] + b_ref[...]))
        r, c = _tri_iotas(tb)
        incl = _split3_dot((c <= r).astype(BF16), lf) + run_ref[...]
        cf_ref[...] = incl
        run_ref[...] = incl[tb - 1:tb, :]

    return _pcall(body, name=name, grid=(s // tb,),
                  in_specs=[pl.BlockSpec((tb, LANES), lambda i: (i, CB_FL)),
                            pl.BlockSpec((1, LANES), lambda i: (0, 0))],
                  out_specs=pl.BlockSpec((tb, LANES), lambda i: (i, 0)),
                  out_shape=jax.ShapeDtypeStruct((s, LANES), F32),
                  scratch_shapes=[pltpu.VMEM((1, LANES), F32)],
                  semantics=("arbitrary",))(proj, b_pad)


def _forget_cumsum_bwd(proj, b_pad, dcf, *, name):
    s = proj.shape[0]
    tb = min(256, s)
    nb = s // tb

    def body(fl_ref, b_ref, dcf_ref, dfl_ref, db_ref, run_ref):
        @pl.when(pl.program_id(0) == 0)
        def _():
            run_ref[...] = jnp.zeros_like(run_ref)
            db_ref[...] = jnp.zeros_like(db_ref)
        r, c = _tri_iotas(tb)
        dlf = _split3_dot((c >= r).astype(BF16), dcf_ref[...]) + run_ref[...]
        run_ref[...] = dlf[0:1, :]
        xv = fl_ref[...] + b_ref[...]
        e = jnp.exp(-jnp.abs(xv))
        sig_neg = jnp.where(xv >= 0.0, e, 1.0) / (1.0 + e)
        dfl = dlf * sig_neg
        dfl_ref[...] = dfl
        db_ref[...] += _colsum(dfl)

    return _pcall(body, name=name, grid=(nb,),
                  in_specs=[pl.BlockSpec((tb, LANES), lambda i: (nb - 1 - i, CB_FL)),
                            pl.BlockSpec((1, LANES), lambda i: (0, 0)),
                            pl.BlockSpec((tb, LANES), lambda i: (nb - 1 - i, 0))],
                  out_specs=[pl.BlockSpec((tb, LANES), lambda i: (nb - 1 - i, 0)),
                             pl.BlockSpec((1, LANES), lambda i: (0, 0))],
                  out_shape=[jax.ShapeDtypeStruct((s, LANES), F32),
                             jax.ShapeDtypeStruct((1, LANES), F32)],
                  scratch_shapes=[pltpu.VMEM((1, LANES), F32)],
                  semantics=("arbitrary",))(proj, b_pad, dcf)


def _fox_bias_q(cfc, p, h):
    lane = lax.broadcasted_iota(jnp.int32, (1, LANES), 1)
    return jnp.sum(jnp.where(lane == 2 * p + h, cfc, 0.0), axis=1, keepdims=True)


def _fox_score_bound(q, kmax_row, hm):
    out = []
    for h in range(2):
        qnorm = jnp.sqrt(jnp.sum(jnp.where(hm[h], q * q, 0.0), axis=1, keepdims=True))
        out.append(1.02 * qnorm * kmax_row[:, h * HEAD_DIM:h * HEAD_DIM + 1])
    return out


def _fox_live(cfr_ref, j, t, tops):
    jc = jnp.maximum(j, 0)
    worst = []
    for h in range(2):
        cf_min = jnp.min(cfr_ref[0, pl.ds(h, 1), _rows(jc, t)], axis=1, keepdims=True)
        worst.append(jnp.max(tops[h] - cf_min))
    return (j >= 0) & (jnp.maximum(worst[0], worst[1]) > -SKIP_LOG)


def _fox_fwd(proj, qn, kn, cf, cf_rows, kmax, *, name):
    s = proj.shape[0]
    t = min(ATT_T, s)
    scale = HEAD_DIM ** -0.5

    def body(q_ref, k_ref, v_ref, cfc_ref, cfr_ref, kmax_ref, o_ref, lse_ref):
        p, i = pl.program_id(0), pl.program_id(1)
        hm = _lane_masks()
        q = q_ref[...] * scale
        qh = [jnp.where(mk, q, 0.0).astype(BF16) for mk in hm]
        cfc = cfc_ref[...]
        bq = [_fox_bias_q(cfc, p, h) for h in range(2)]
        qk_top = _fox_score_bound(q, kmax_ref[...], hm)
        r, c = _tri_iotas(t)
        causal = c <= r

        def chunk(j, carry, masked):
            kb = k_ref[_rows(j, t), :].astype(BF16)
            vb = v_ref[_rows(j, t), :].astype(BF16)
            out = []
            for h in range(2):
                m_run, l_run, acc = carry[h]
                z = _dot_nt(qh[h], kb) + (bq[h] - cfr_ref[0, pl.ds(h, 1), _rows(j, t)])
                if masked:
                    z = jnp.where(causal, z, -1e30)
                m_new = jnp.maximum(m_run, jnp.max(z, axis=1, keepdims=True))
                alpha = jnp.exp(m_run - m_new)
                pr = jnp.exp(z - m_new)
                out.append((m_new, alpha * l_run + jnp.sum(pr, axis=1, keepdims=True),
                            alpha * acc + _dot(pr.astype(BF16), vb)))
            return tuple(out)

        init = tuple((jnp.full((t, 1), -1e30, F32), jnp.zeros((t, 1), F32),
                      jnp.zeros((t, LANES), F32)) for _ in range(2))
        carry = chunk(i, init, True)

        def live(j, cr):
            return _fox_live(cfr_ref, j, t, [qk_top[h] + bq[h] - cr[h][0] for h in range(2)])

        def step(st):
            cr = chunk(st[0], st[2], False)
            return st[0] - 1, live(st[0] - 1, cr), cr

        carry = lax.while_loop(lambda st: st[1], step, (i - 1, live(i - 1, carry), carry))[2]
        o_ref[...] = jnp.where(hm[0], carry[0][2] / carry[0][1], carry[1][2] / carry[1][1])
        lse_ref[...] = jnp.where(hm[0], carry[0][0] + jnp.log(carry[0][1]),
                                 carry[1][0] + jnp.log(carry[1][1]))

    blk = pl.BlockSpec((t, LANES), lambda p, i: (i, p))
    full = pl.BlockSpec((s, LANES), lambda p, i: (0, p))
    return _pcall(body, name=name, grid=(FOX_W // LANES, s // t),
                  in_specs=[blk, full, pl.BlockSpec((s, LANES), lambda p, i: (0, CB_VB + p)),
                            pl.BlockSpec((t, LANES), lambda p, i: (i, 0)),
                            pl.BlockSpec((1, 2, s), lambda p, i: (p, 0, 0)),
                            pl.BlockSpec((1, LANES), lambda p, i: (0, p))],
                  out_specs=[blk, blk],
                  out_shape=[jax.ShapeDtypeStruct((s, FOX_W), F32)] * 2,
                  semantics=("parallel", "arbitrary"))(qn, kn, proj, cf, cf_rows, kmax)


def _fox_bwd(proj, qn, kn, cf, cf_rows, kmax, do, o, lse, *, name):
    s = proj.shape[0]
    t = min(ATT_T, s)
    scale = HEAD_DIM ** -0.5

    def body(q_ref, k_ref, v_ref, cfc_ref, cfr_ref, kmax_ref, do_ref, o_ref, lse_ref,
             dq_ref, dk_ref, dv_ref, dcf_ref, dcfq_ref):
        p, i = pl.program_id(0), pl.program_id(1)

        @pl.when(i == 0)
        def _():
            dk_ref[...] = jnp.zeros_like(dk_ref)
            dv_ref[...] = jnp.zeros_like(dv_ref)
            dcf_ref[...] = jnp.zeros_like(dcf_ref)

        hm = _lane_masks()
        q = q_ref[...] * scale
        do = do_ref[...]
        dov = do * o_ref[...]
        qh = [jnp.where(mk, q, 0.0).astype(BF16) for mk in hm]
        doh = [jnp.where(mk, do, 0.0).astype(BF16) for mk in hm]
        delta = [jnp.sum(jnp.where(mk, dov, 0.0), axis=1, keepdims=True) for mk in hm]
        lsev = lse_ref[...]
        lse = [lsev[:, 0:1], lsev[:, HEAD_DIM:HEAD_DIM + 1]]
        cfc = cfc_ref[...]
        bq = [_fox_bias_q(cfc, p, h) - lse[h] for h in range(2)]
        qk_top = _fox_score_bound(q, kmax_ref[...], hm)
        tops = [qk_top[h] + bq[h] for h in range(2)]
        r, c = _tri_iotas(t)
        causal = c <= r
        j_stop = lax.while_loop(lambda st: st[1],
                                lambda st: (st[0] - 1, _fox_live(cfr_ref, st[0] - 1, t, tops)),
                                (i - 1, _fox_live(cfr_ref, i - 1, t, tops)))[0]

        def chunk(j, carry, masked):
            kf = k_ref[_rows(j, t), :]
            kb = kf.astype(BF16)
            vb = v_ref[_rows(j, t), :].astype(BF16)
            out = []
            dk_add = jnp.zeros((t, LANES), F32)
            dv_add = jnp.zeros((t, LANES), F32)
            for h in range(2):
                kh = jnp.where(hm[h], kf, 0.0).astype(BF16)
                z = _dot_nt(qh[h], kb) + (bq[h] - cfr_ref[0, pl.ds(h, 1), _rows(j, t)])
                pr = jnp.exp(z)
                if masked:
                    pr = jnp.where(causal, pr, 0.0)
                ds = pr * (_dot_nt(doh[h], vb) - delta[h])
                dsb = ds.astype(BF16)
                out.append((carry[h][0] + _dot(dsb, kh),
                            carry[h][1] + jnp.sum(ds, axis=1, keepdims=True)))
                dk_add = dk_add + _dot_tn(dsb, qh[h])
                dv_add = dv_add + _dot_tn(pr.astype(BF16), doh[h])
                dcf_ref[0, pl.ds(h, 1), _rows(j, t)] -= jnp.sum(ds, axis=0, keepdims=True)
            dk_ref[_rows(j, t), :] += dk_add
            dv_ref[_rows(j, t), :] += dv_add
            return tuple(out)

        init = tuple((jnp.zeros((t, LANES), F32), jnp.zeros((t, 1), F32)) for _ in range(2))
        carry = lax.fori_loop(j_stop + 1, i, lambda j, cr: chunk(j, cr, False), init)
        carry = chunk(i, carry, True)
        dq_ref[...] = (carry[0][0] + carry[1][0]) * scale
        dcfq_ref[...] = jnp.where(hm[0], carry[0][1], carry[1][1])

    blk = pl.BlockSpec((t, LANES), lambda p, i: (i, p))
    full = pl.BlockSpec((s, LANES), lambda p, i: (0, p))
    rows = pl.BlockSpec((1, 2, s), lambda p, i: (p, 0, 0))
    return _pcall(body, name=name, grid=(FOX_W // LANES, s // t),
                  in_specs=[blk, full, pl.BlockSpec((s, LANES), lambda p, i: (0, CB_VB + p)),
                            pl.BlockSpec((t, LANES), lambda p, i: (i, 0)), rows,
                            pl.BlockSpec((1, LANES), lambda p, i: (0, p)),
                            pl.BlockSpec((t, LANES), lambda p, i: (i, SB_W // LANES + p)),
                            blk, blk],
                  out_specs=[blk, full, full, rows, blk],
                  out_shape=[jax.ShapeDtypeStruct((s, FOX_W), F32)] * 3
                  + [jax.ShapeDtypeStruct((FOX_W // LANES, 2, s), F32),
                     jax.ShapeDtypeStruct((s, FOX_W), F32)],
                  semantics=("parallel", "arbitrary"))(qn, kn, proj, cf, cf_rows, kmax, do, o, lse)


_GELU_C0 = math.sqrt(2.0 / math.pi)
_GELU_C1 = 0.044715


def _gelu(x):
    th = jnp.tanh(_GELU_C0 * (x + _GELU_C1 * (x * x * x)))
    return 0.5 * x * (1.0 + th), th


def _gelu_grad(x, th):
    return 0.5 * (1.0 + th) + 0.5 * x * (1.0 - th * th) * (_GELU_C0 * (1.0 + 3.0 * _GELU_C1 * x * x))


def _sgu_mix(wm, vn_c, lo, bcol):
    return jnp.where(lo, _dot(wm[0], vn_c) + bcol[0], _dot(wm[1], vn_c) + bcol[1])


def _sgu_fwd(proj, w, b_cols, gn, *, name):
    s = proj.shape[0]
    tr = min(512, s)
    ch = SGU_CHUNK

    def body(u_ref, v_ref, w_ref, b_ref, gn_ref, o_ref):
        lo = _lane_masks()[0]
        r, c = _tri_iotas(ch)
        wm = [jnp.where(c <= r, w_ref[h], 0.0).astype(BF16) for h in range(2)]
        bcol = [b_ref[0, :, h:h + 1] for h in range(2)]
        for n in range(tr // ch):
            rows = slice(n * ch, (n + 1) * ch)
            u, _ = _gelu(u_ref[rows, :])
            vg, _ = _gelu(v_ref[rows, :])
            vn = vg * lax.rsqrt(_group_mean(vg * vg, lo) + EPS) * gn_ref[0]
            o_ref[rows, :] = u * _sgu_mix(wm, vn.astype(BF16), lo, bcol)

    blk = lambda cb: pl.BlockSpec((tr, LANES), lambda p, i: (i, cb + p))
    return _pcall(body, name=name, grid=(SGU_W // LANES, s // tr),
                  in_specs=[blk(CB_UC), blk(CB_VC),
                            pl.BlockSpec((2, ch, ch), lambda p, i: (p, 0, 0)),
                            pl.BlockSpec((1, ch, 2), lambda p, i: (p, 0, 0)),
                            pl.BlockSpec((1, 1, LANES), lambda p, i: (p, 0, 0))],
                  out_specs=pl.BlockSpec((tr, LANES), lambda p, i: (i, p)),
                  out_shape=jax.ShapeDtypeStruct((s, SGU_W), F32),
                  semantics=("parallel", "parallel"))(proj, proj, w, b_cols, gn)


def _sgu_bwd(proj, dmixed, w, w_t, b_cols, gn, *, name):
    s = proj.shape[0]
    tr = min(512, s)
    ch = SGU_CHUNK
    cb_do = (SB_W + FOX_W) // LANES

    def body(u_ref, v_ref, do_ref, w_ref, wt_ref, b_ref, gn_ref,
             du_ref, dv_ref, dw_ref, db_ref, dgn_ref):
        @pl.when(pl.program_id(1) == 0)
        def _():
            dw_ref[...] = jnp.zeros_like(dw_ref)
            db_ref[...] = jnp.zeros_like(db_ref)
            dgn_ref[...] = jnp.zeros_like(dgn_ref)

        hm = _lane_masks()
        lo = hm[0]
        r, c = _tri_iotas(ch)
        wm = [jnp.where(c <= r, w_ref[h], 0.0).astype(BF16) for h in range(2)]
        wtm = [jnp.where(r <= c, wt_ref[h], 0.0).astype(BF16) for h in range(2)]
        bcol = [b_ref[0, :, h:h + 1] for h in range(2)]
        gnv = gn_ref[0]
        for n in range(tr // ch):
            rows = slice(n * ch, (n + 1) * ch)
            uc, vc, do = u_ref[rows, :], v_ref[rows, :], do_ref[rows, :]
            u, thu = _gelu(uc)
            vg, thv = _gelu(vc)
            rinv = lax.rsqrt(_group_mean(vg * vg, lo) + EPS)
            xh = vg * rinv
            vnb = (xh * gnv).astype(BF16)
            mix = _sgu_mix(wm, vnb, lo, bcol)
            du_ref[rows, :] = do * mix * _gelu_grad(uc, thu)
            dm = do * u
            dmb = dm.astype(BF16)
            dvn = jnp.where(lo, _dot(wtm[0], dmb), _dot(wtm[1], dmb))
            for h in range(2):
                dmh = jnp.where(hm[h], dm, 0.0)
                dw_ref[h] += jnp.where(c <= r, _dot_nt(dmh.astype(BF16), vnb), 0.0)
                db_ref[0, :, h:h + 1] += jnp.sum(dmh, axis=1, keepdims=True)
            dgn_ref[0] += _colsum(dvn * xh)
            dxh = dvn * gnv
            dvg = rinv * (dxh - xh * _group_mean(dxh * xh, lo))
            dv_ref[rows, :] = dvg * _gelu_grad(vc, thv)

    blk = lambda cb: pl.BlockSpec((tr, LANES), lambda p, i: (i, cb + p))
    w_spec = pl.BlockSpec((2, ch, ch), lambda p, i: (p, 0, 0))
    b_spec = pl.BlockSpec((1, ch, 2), lambda p, i: (p, 0, 0))
    g_spec = pl.BlockSpec((1, 1, LANES), lambda p, i: (p, 0, 0))
    out_blk = pl.BlockSpec((tr, LANES), lambda p, i: (i, p))
    return _pcall(body, name=name, grid=(SGU_W // LANES, s // tr),
                  in_specs=[blk(CB_UC), blk(CB_VC), blk(cb_do), w_spec, w_spec, b_spec, g_spec],
                  out_specs=[out_blk, out_blk, w_spec, b_spec, g_spec],
                  out_shape=[jax.ShapeDtypeStruct((s, SGU_W), F32)] * 2
                  + [jax.ShapeDtypeStruct(w.shape, F32), jax.ShapeDtypeStruct(b_cols.shape, F32),
                     jax.ShapeDtypeStruct(gn.shape, F32)],
                  semantics=("parallel", "arbitrary"))(proj, proj, dmixed, w, w_t, b_cols, gn)


def _pad_lanes(v):
    return jnp.zeros((1, LANES), F32).at[0, :v.shape[0]].set(v)


def _small_views(sm):
    return dict(
        n1=sm["norm1_g"][None, :], n2=sm["norm2_g"][None, :],
        b_pad=_pad_lanes(sm["b_forget"]),
        qg=jnp.tile(sm["q_norm_g"], 2)[None, :], kg=jnp.tile(sm["k_norm_g"], 2)[None, :],
        gn=sm["sgu_norm_g"].reshape(2, 1, LANES),
        w=sm["sgu_w"], w_t=jnp.swapaxes(sm["sgu_w"], 1, 2),
        b_cols=sm["sgu_b"].reshape(2, 2, SGU_CHUNK).transpose(0, 2, 1))


def _cf_rows(cf):
    return cf[:, :FOX_HEADS].T.reshape(FOX_W // LANES, 2, cf.shape[0])


def _layer_fwd(x_in, prev, mod, wts, sm, l):
    sh1, sc1, g1, sh2, sc2, g2 = mod
    v = _small_views(sm)
    if prev is None:
        x0 = x_in
        h1 = _norm_mod_fwd(x0, v["n1"], sc1, sh1, name=f"l{l}_norm1")
    else:
        x0, h1 = _resid_norm_mod_fwd(x_in, prev[0], prev[1], v["n1"], sc1, sh1, name=f"l{l}_norm1")
    proj = _matmul(h1, wts["w_in"], name=f"l{l}_proj")
    o_sb = _sb_fwd(proj, name=f"l{l}_sb_fwd")
    qn, kn, kmax = _fox_prep_fwd(proj, v["qg"], v["kg"], name=f"l{l}_fox_prep")
    cf = _forget_cumsum_fwd(proj, v["b_pad"], name=f"l{l}_cumf")
    cfr = _cf_rows(cf)
    o_fox, lse = _fox_fwd(proj, qn, kn, cf, cfr, kmax, name=f"l{l}_fox_fwd")
    o_sgu = _sgu_fwd(proj, v["w"], v["b_cols"], v["gn"], name=f"l{l}_sgu_fwd")
    mixed = jnp.concatenate([o_sb, o_fox, o_sgu], axis=1).astype(BF16)
    mo = _matmul(mixed, wts["w_out"], name=f"l{l}_wout")
    x1, h2 = _resid_norm_mod_fwd(x0, mo, g1, v["n2"], sc2, sh2, name=f"l{l}_norm2")
    a = _matmul(h2, wts["w1"], name=f"l{l}_mlp1", out_dtype=BF16)
    rr = _relu2_fwd(a, name=f"l{l}_relu2")
    m2 = _matmul(rr, wts["w2"], name=f"l{l}_mlp2")
    saved = dict(x0=x0, h1=h1, proj=proj, qn=qn, kn=kn, kmax=kmax, cf=cf, cfr=cfr, o_fox=o_fox,
                 lse=lse, mixed=mixed, mo=mo, x1=x1, h2=h2, a=a, rr=rr, m2=m2)
    return saved


def _layer_bwd(dx2, sv, mod, wts, sm, l):
    sh1, sc1, g1, sh2, sc2, g2 = mod
    v = _small_views(sm)
    dm2, dg2 = _gate_bwd(dx2, sv["m2"], g2, name=f"l{l}_gate2_bwd")
    dw2 = _matmul(sv["rr"], dm2, ta=True, name=f"l{l}_dw2")
    dr = _matmul(dm2, wts["w2"], tb=True, name=f"l{l}_dr", out_dtype=BF16)
    da = _relu2_bwd(dr, sv["a"], name=f"l{l}_relu2_bwd")
    dw1 = _matmul(sv["h2"], da, ta=True, name=f"l{l}_dw1")
    dh2 = _matmul(da, wts["w1"], tb=True, name=f"l{l}_dh2")
    dx1, dn2, dsc2, dsh2 = _norm_mod_bwd(sv["x1"], dh2, dx2, v["n2"], sc2, name=f"l{l}_norm2_bwd")
    dmo, dg1 = _gate_bwd(dx1, sv["mo"], g1, name=f"l{l}_gate1_bwd")
    dwo = _matmul(sv["mixed"], dmo, ta=True, name=f"l{l}_dwout")
    dmixed = _matmul(dmo, wts["w_out"], tb=True, name=f"l{l}_dmixed")
    proj = sv["proj"]
    dqa, dka, dva = _sb_bwd(proj, dmixed, name=f"l{l}_sb_bwd")
    dqn, dkn, dvb, dcfr, dcfq = _fox_bwd(proj, sv["qn"], sv["kn"], sv["cf"], sv["cfr"], sv["kmax"], dmixed,
                                   sv["o_fox"], sv["lse"], name=f"l{l}_fox_bwd")
    dqb, dkb, dqg, dkg = _fox_prep_bwd(proj, dqn, dkn, v["qg"], v["kg"], name=f"l{l}_fox_prep_bwd")
    s = proj.shape[0]
    dcf_heads = dcfr.reshape(FOX_HEADS, s).T + dcfq.reshape(s, FOX_HEADS, HEAD_DIM)[:, :, 0]
    dcf = jnp.zeros((s, LANES), F32).at[:, :FOX_HEADS].set(dcf_heads)
    dfl, dbf = _forget_cumsum_bwd(proj, v["b_pad"], dcf, name=f"l{l}_cumf_bwd")
    duc, dvc, dsw, dsb_cols, dgn = _sgu_bwd(proj, dmixed, v["w"], v["w_t"], v["b_cols"], v["gn"],
                                            name=f"l{l}_sgu_bwd")
    dproj = jnp.concatenate([dqa, dka, dva, dqb, dkb, dvb, duc, dvc, dfl,
                             jnp.zeros((s, LANES), F32)], axis=1).astype(BF16)
    dwin = _matmul(sv["h1"], dproj, ta=True, name=f"l{l}_dwin")
    dh1 = _matmul(dproj, wts["w_in"], tb=True, name=f"l{l}_dh1")
    dx0, dn1, dsc1, dsh1 = _norm_mod_bwd(sv["x0"], dh1, dx1, v["n1"], sc1, name=f"l{l}_norm1_bwd")
    big = dict(w_in=dwin, w_out=dwo, w1=dw1, w2=dw2)
    small = dict(norm1_g=dn1[0], norm2_g=dn2[0], b_forget=dbf[0, :FOX_HEADS],
                 q_norm_g=dqg[0, :HEAD_DIM] + dqg[0, HEAD_DIM:],
                 k_norm_g=dkg[0, :HEAD_DIM] + dkg[0, HEAD_DIM:],
                 sgu_norm_g=dgn.reshape(4, HEAD_DIM), sgu_w=dsw,
                 sgu_b=dsb_cols.transpose(0, 2, 1).reshape(4, SGU_CHUNK))
    dmod = jnp.concatenate([dsh1, dsc1, dg1, dsh2, dsc2, dg2], axis=1)
    return dx0, big, small, dmod


def _w_in_to_internal(w):
    pad = jnp.zeros((w.shape[0], PROJ_W - IN_W), w.dtype)
    return jnp.concatenate([w[:, :ATT_W], w[:, ATT_W + FOX_HEADS:], w[:, ATT_W:ATT_W + FOX_HEADS],
                            pad], axis=1)


def _w_in_from_internal(g):
    n_gate = SGU_W * 2
    return jnp.concatenate([g[:, :ATT_W], g[:, ATT_W + n_gate:ATT_W + n_gate + FOX_HEADS],
                            g[:, ATT_W:ATT_W + n_gate]], axis=1)


def _exchange(x, masks, slot_shift, slot_bits, scatter, *, name):
    n_slots = 2 ** slot_bits
    blk_shape = x.shape[1:] if scatter else x.shape
    n_peers = len(masks)

    def body(x_ref, out_ref, send_sems, recv_sems, local_sem):
        ids = (lax.axis_index("x"), lax.axis_index("y"), lax.axis_index("c"))
        me = 4 * ids[0] + 2 * ids[1] + ids[2]
        my_slot = (me >> slot_shift) & (n_slots - 1)

        def peer(mask):
            return tuple(1 - v if (mask >> b) & 1 else v for v, b in zip(ids, (2, 1, 0)))

        def src_for(slot):
            return x_ref.at[slot] if scatter else x_ref

        copies = [pltpu.make_async_copy(src_for(my_slot), out_ref.at[my_slot], local_sem)]
        for kk, mask in enumerate(masks):
            peer_slot = ((me ^ mask) >> slot_shift) & (n_slots - 1)
            copies.append(pltpu.make_async_remote_copy(
                src_ref=src_for(peer_slot), dst_ref=out_ref.at[my_slot],
                send_sem=send_sems.at[kk], recv_sem=recv_sems.at[kk],
                device_id=peer(mask), device_id_type=MESH))
        for cp in copies:
            cp.start()
        for cp in copies:
            cp.wait()

    any_spec = pl.BlockSpec(memory_space=pl.ANY)
    return _pcall(body, name=name, in_specs=[any_spec], out_specs=any_spec,
                  out_shape=jax.ShapeDtypeStruct((n_slots,) + tuple(blk_shape), x.dtype),
                  scratch_shapes=[pltpu.SemaphoreType.DMA((n_peers,)),
                                  pltpu.SemaphoreType.DMA((n_peers,)),
                                  pltpu.SemaphoreType.DMA(())])(x)


CORE_PIECE_BYTES = 12 * 2 ** 20
CORE_DMA_CHUNKS = 4


def _core_swap_piece(x, *, name):
    rows, cols = x.shape
    n_ch = CORE_DMA_CHUNKS if rows % (16 * CORE_DMA_CHUNKS) == 0 else 1
    rc = rows // n_ch

    def body(x_ref, out_ref, send_sems, recv_sems):
        sibling = (lax.axis_index("x"), lax.axis_index("y"), 1 - lax.axis_index("c"))
        copies = [pltpu.make_async_remote_copy(
            src_ref=x_ref.at[pl.ds(ch * rc, rc)], dst_ref=out_ref.at[pl.ds(ch * rc, rc)],
            send_sem=send_sems.at[ch], recv_sem=recv_sems.at[ch],
            device_id=sibling, device_id_type=MESH) for ch in range(n_ch)]
        for cp in copies:
            cp.start()
        for cp in copies:
            cp.wait()

    vmem = pl.BlockSpec(memory_space=pltpu.VMEM)
    return _pcall(body, name=name, in_specs=[vmem], out_specs=vmem,
                  out_shape=jax.ShapeDtypeStruct(x.shape, x.dtype),
                  scratch_shapes=[pltpu.SemaphoreType.DMA((n_ch,)),
                                  pltpu.SemaphoreType.DMA((n_ch,))])(x)


def _core_swap(x, *, name):
    rows, cols = x.shape
    n = 1
    while (rows % n or (rows // n) % 16 or
           (rows // n) * (-(-cols // LANES) * LANES) * x.dtype.itemsize > CORE_PIECE_BYTES):
        n += 1
    pr = rows // n
    pieces = [_core_swap_piece(x[kk * pr:(kk + 1) * pr], name=f"{name}_{kk}") for kk in range(n)]
    return pieces[0] if n == 1 else jnp.concatenate(pieces, axis=0)


def _by_core(core, mine, theirs, axis):
    return jnp.where(core == 0, jnp.concatenate([mine, theirs], axis=axis),
                     jnp.concatenate([theirs, mine], axis=axis))


def _gather_chips(x, *, name):
    return _exchange(x, (2, 4, 6), 1, 2, False, name=name)


def _gather_all(x, *, name):
    return _exchange(x, (1, 2, 3, 4, 5, 6, 7), 0, 3, False, name=name)


def _scatter_chips(x4, *, name):
    return _exchange(x4, (2, 4, 6), 1, 2, True, name=name)


def _sum_slots(parts, *, name, out_dtype=F32, tr=256):
    n, rows, cols = parts.shape
    tr = min(tr, rows)
    assert rows % tr == 0, (name, rows, tr)

    def body(p_ref, o_ref):
        acc = p_ref[0].astype(F32)
        for kk in range(1, n):
            acc = acc + p_ref[kk].astype(F32)
        o_ref[...] = acc.astype(o_ref.dtype)

    return _pcall(body, name=name, grid=(rows // tr,),
                  in_specs=[pl.BlockSpec((n, tr, cols), lambda i: (0, i, 0))],
                  out_specs=pl.BlockSpec((tr, cols), lambda i: (i, 0)),
                  out_shape=jax.ShapeDtypeStruct((rows, cols), out_dtype),
                  semantics=("parallel",))(parts)


def _add2(a, b, *, name, out_dtype, tr=512):
    def fn(f, v):
        return [f[0] + f[1]], []
    (out,), _ = _rowwise(fn, [a, b], [], [out_dtype], 0, name=name, tr=tr)
    return out


def _adamw(w, m, v, parts, *, name, tr=256):
    n, rows, cols = parts.shape
    tr = min(tr, rows)
    assert rows % tr == 0, (name, rows, tr)
    c1 = 1.0 - ADAM_B1 ** ADAM_STEP
    c2 = 1.0 - ADAM_B2 ** ADAM_STEP

    def body(w_ref, m_ref, v_ref, p_ref, g_ref, d_ref, nm_ref, nv_ref):
        g = p_ref[0]
        for kk in range(1, n):
            g = g + p_ref[kk]
        nm = ADAM_B1 * m_ref[...] + (1.0 - ADAM_B1) * g
        nv = ADAM_B2 * v_ref[...] + (1.0 - ADAM_B2) * (g * g)
        g_ref[...] = g
        nm_ref[...] = nm
        nv_ref[...] = nv
        d_ref[...] = -ADAM_LR * ((nm / c1) / (jnp.sqrt(nv / c2) + ADAM_EPS) + ADAM_WD * w_ref[...])

    spec = pl.BlockSpec((tr, cols), lambda i: (i, 0))
    return _pcall(body, name=name, grid=(rows // tr,),
                  in_specs=[spec, spec, spec, pl.BlockSpec((n, tr, cols), lambda i: (0, i, 0))],
                  out_specs=[spec] * 4,
                  out_shape=[jax.ShapeDtypeStruct((rows, cols), F32)] * 4,
                  semantics=("parallel",))(w, m, v, parts)


def _silu(c):
    return c / (1.0 + jnp.exp(-c))


def _ada_fwd(c_all, ada_w, ada_b_sh, *, name):
    nl, d, wsh = ada_w.shape

    def body(c_ref, w_ref, b_ref, o_ref):
        cond = _silu(c_ref[...]).astype(BF16)
        o_ref[0] = _dot(cond, w_ref[0].astype(BF16)) + b_ref[0]

    return _pcall(body, name=name, grid=(nl,),
                  in_specs=[pl.BlockSpec(c_all.shape, lambda l: (0, 0)),
                            pl.BlockSpec((1, d, wsh), lambda l: (l, 0, 0)),
                            pl.BlockSpec((1, 1, wsh), lambda l: (l, 0, 0))],
                  out_specs=pl.BlockSpec((1, c_all.shape[0], wsh), lambda l: (l, 0, 0)),
                  out_shape=jax.ShapeDtypeStruct((nl, c_all.shape[0], wsh), F32),
                  semantics=("parallel",))(c_all, ada_w, ada_b_sh)


def _ada_bwd(c_all, dmod_sh, *, name):
    nl, nb, wsh = dmod_sh.shape
    d = c_all.shape[1]

    def body(c_ref, dm_ref, o_ref):
        cond = _silu(c_ref[...]).astype(BF16)
        o_ref[0] = _dot_tn(cond, dm_ref[0].astype(BF16))

    return _pcall(body, name=name, grid=(nl,),
                  in_specs=[pl.BlockSpec(c_all.shape, lambda l: (0, 0)),
                            pl.BlockSpec((1, nb, wsh), lambda l: (l, 0, 0))],
                  out_specs=pl.BlockSpec((1, d, wsh), lambda l: (l, 0, 0)),
                  out_shape=jax.ShapeDtypeStruct((nl, d, wsh), F32),
                  semantics=("parallel",))(c_all, dmod_sh)


SMALL_NAMES = ("norm1_g", "norm2_g", "b_forget", "q_norm_g", "k_norm_g", "sgu_norm_g", "sgu_w",
               "sgu_b")
WEIGHT_NAMES = ("ada_w", "ada_b", "norm1_g", "norm2_g", "w_in", "b_forget", "q_norm_g", "k_norm_g",
                "sgu_norm_g", "sgu_w", "sgu_b", "w_out", "mlp_w1", "mlp_w2")


SMALL_TILE_ROWS = 256


def _pack_small(tree):
    flat = jnp.concatenate([tree[n].reshape(-1) for n in SMALL_NAMES])
    n = flat.shape[0]
    rows = -(-n // (SMALL_TILE_ROWS * LANES)) * SMALL_TILE_ROWS
    return jnp.zeros((rows * LANES,), F32).at[:n].set(flat).reshape(rows, LANES)


def _unpack_small(packed, like):
    flat = packed.reshape(-1)
    out, off = {}, 0
    for n in SMALL_NAMES:
        size = like[n].size
        out[n] = flat[off:off + size].reshape(like[n].shape)
        off += size
    return out


def kernel(x, c, ada_w, ada_b, norm1_g, norm2_g, w_in, b_forget, q_norm_g, k_norm_g, sgu_norm_g, sgu_w, sgu_b, w_out, mlp_w1, mlp_w2, loss_target, m_ada_w, m_ada_b, m_norm1_g, m_norm2_g, m_w_in, m_b_forget, m_q_norm_g, m_k_norm_g, m_sgu_norm_g, m_sgu_w, m_sgu_b, m_w_out, m_mlp_w1, m_mlp_w2, v_ada_w, v_ada_b, v_norm1_g, v_norm2_g, v_w_in, v_b_forget, v_q_norm_g, v_k_norm_g, v_sgu_norm_g, v_sgu_w, v_sgu_b, v_w_out, v_mlp_w1, v_mlp_w2):
    w = dict(ada_w=ada_w, ada_b=ada_b, norm1_g=norm1_g, norm2_g=norm2_g, w_in=w_in,
             b_forget=b_forget, q_norm_g=q_norm_g, k_norm_g=k_norm_g, sgu_norm_g=sgu_norm_g,
             sgu_w=sgu_w, sgu_b=sgu_b, w_out=w_out, mlp_w1=mlp_w1, mlp_w2=mlp_w2)
    mom = dict(ada_w=m_ada_w, ada_b=m_ada_b, norm1_g=m_norm1_g, norm2_g=m_norm2_g, w_in=m_w_in,
               b_forget=m_b_forget, q_norm_g=m_q_norm_g, k_norm_g=m_k_norm_g,
               sgu_norm_g=m_sgu_norm_g, sgu_w=m_sgu_w, sgu_b=m_sgu_b, w_out=m_w_out,
               mlp_w1=m_mlp_w1, mlp_w2=m_mlp_w2)
    var = dict(ada_w=v_ada_w, ada_b=v_ada_b, norm1_g=v_norm1_g, norm2_g=v_norm2_g, w_in=v_w_in,
               b_forget=v_b_forget, q_norm_g=v_q_norm_g, k_norm_g=v_k_norm_g,
               sgu_norm_g=v_sgu_norm_g, sgu_w=v_sgu_w, sgu_b=v_sgu_b, w_out=v_w_out,
               mlp_w1=v_mlp_w1, mlp_w2=v_mlp_w2)
    depth, d = norm1_g.shape
    chip = 2 * lax.axis_index("x") + lax.axis_index("y")
    me = 2 * chip + lax.axis_index("c")
    n_chips = 4
    ada_sh = ada_w.shape[2]

    core = lax.axis_index("c")
    half_l = depth // 2

    def gather_weight(w_sh, name):
        _, r, cols = w_sh.shape
        mine = lax.dynamic_slice_in_dim(w_sh, core * half_l, half_l, axis=0).astype(BF16)
        got = _gather_chips(mine.reshape(half_l * r, cols), name=f"gather_{name}")
        theirs = _core_swap(got.reshape(n_chips * half_l * r, cols), name=f"share_{name}")
        return _by_core(core, got.reshape(n_chips, half_l, r, cols),
                        theirs.reshape(n_chips, half_l, r, cols), 1)

    g_in = gather_weight(w_in, "w_in")
    g_out = gather_weight(w_out, "w_out")
    g_w1 = gather_weight(mlp_w1, "w1")
    g_w2 = gather_weight(mlp_w2, "w2")
    layer_w = []
    for l in range(depth):
        layer_w.append(dict(
            w_in=_w_in_to_internal(jnp.concatenate([g_in[k, l] for k in range(n_chips)], axis=1)),
            w_out=g_out[:, l].reshape(d, d),
            w1=jnp.concatenate([g_w1[k, l] for k in range(n_chips)], axis=1),
            w2=g_w2[:, l].reshape(D_FF, d)))

    c_all = _gather_all(jnp.zeros((8, d), F32).at[0].set(c[0]), name="gather_c")[:, 0]
    c_pad = jnp.concatenate([c_all, jnp.zeros_like(c_all)], axis=0)
    ada_b_sh = lax.dynamic_slice_in_dim(ada_b, chip * ada_sh, ada_sh, axis=1)[:, None, :]
    mod_sh = _ada_fwd(c_pad, ada_w, ada_b_sh, name="ada_fwd")
    mod_all = _gather_chips(mod_sh, name="gather_mod")
    mod_me = lax.dynamic_index_in_dim(mod_all, me, axis=2, keepdims=False)
    mod_me = mod_me.transpose(1, 0, 2).reshape(depth, 6, 1, d)

    saved = []
    xs, prev = x[0], None
    for l in range(depth):
        mod = [mod_me[l, kk] for kk in range(6)]
        sm = {n: w[n][l] for n in SMALL_NAMES}
        sv = _layer_fwd(xs, prev, mod, layer_w[l], sm, l)
        saved.append(sv)
        xs, prev = sv["x1"], (sv["m2"], mod[5])

    sq, dxs = _loss_fwd_bwd(xs, prev[0], prev[1], loss_target[0], name="loss")
    loss = lax.psum(0.5 * jnp.sum(sq) / d, ("x", "y", "c"))

    big = {n: [] for n in ("w_in", "w_out", "w1", "w2")}
    small = {n: [] for n in SMALL_NAMES}
    dmods = []
    for l in reversed(range(depth)):
        mod = [mod_me[l, kk] for kk in range(6)]
        sm = {n: w[n][l] for n in SMALL_NAMES}
        dxs, bg, smg, dmod = _layer_bwd(dxs, saved[l], mod, layer_w[l], sm, l)
        for n in big:
            big[n].insert(0, bg[n])
        for n in SMALL_NAMES:
            small[n].insert(0, smg[n])
        dmods.insert(0, dmod)
    grad_x = dxs[None]

    out_g, out_d, out_m, out_v = {}, {}, {}, {}

    def run_adamw(name, parts2d, shape):
        rows, cols = parts2d.shape[1:]
        g, dl, nm, nv = _adamw(w[name].reshape(rows, cols), mom[name].reshape(rows, cols),
                               var[name].reshape(rows, cols), parts2d, name=f"adamw_{name}")
        out_g[name], out_d[name] = g.reshape(shape), dl.reshape(shape)
        out_m[name], out_v[name] = nm.reshape(shape), nv.reshape(shape)

    def shards_of(name, l):
        if name == "w_in":
            g = _w_in_from_internal(big["w_in"][l])
            return jnp.stack(jnp.split(g, n_chips, axis=1))
        if name == "mlp_w1":
            return jnp.stack(jnp.split(big["w1"][l], n_chips, axis=1))
        if name == "w_out":
            return big["w_out"][l].reshape(n_chips, d // n_chips, d)
        return big["w2"][l].reshape(n_chips, D_FF // n_chips, d)

    for name in ("w_in", "w_out", "mlp_w1", "mlp_w2"):
        per_chip = jnp.stack([shards_of(name, l) for l in range(depth)], axis=1)
        r, cols = per_chip.shape[2:]
        half_rows = half_l * r
        keep = lax.dynamic_slice_in_dim(per_chip, core * half_l, half_l, axis=1)
        send = lax.dynamic_slice_in_dim(per_chip, (1 - core) * half_l, half_l, axis=1)
        theirs = _core_swap(send.reshape(n_chips * half_rows, cols), name=f"pair_{name}")
        chip_sum = _add2(keep.reshape(n_chips * half_rows, cols), theirs, out_dtype=BF16,
                         name=f"pairsum_{name}")
        got = _scatter_chips(chip_sum.reshape(n_chips, half_rows, cols), name=f"scatter_{name}")
        half = _sum_slots(got, name=f"sum_{name}")
        both = _by_core(core, half, _core_swap(half, name=f"swap_{name}"), 0)
        run_adamw(name, both[None], w[name].shape)

    small_tree = {n: jnp.stack(small[n]) for n in SMALL_NAMES}
    gathered = _gather_all(_pack_small(small_tree), name="gather_small")
    gs, ds_, ms, vs = _adamw(_pack_small({n: w[n] for n in SMALL_NAMES}),
                             _pack_small({n: mom[n] for n in SMALL_NAMES}),
                             _pack_small({n: var[n] for n in SMALL_NAMES}), gathered,
                             name="adamw_small")
    like = {n: w[n] for n in SMALL_NAMES}
    for tree, packed in ((out_g, gs), (out_d, ds_), (out_m, ms), (out_v, vs)):
        tree.update(_unpack_small(packed, like))

    dmod_mine = jnp.concatenate(dmods, axis=0)
    dmod_all = _gather_all(jnp.zeros((depth, 8, 6 * d), F32).at[:, 0].set(dmod_mine),
                           name="gather_dmod")[:, :, 0]
    dmod_lb = dmod_all.transpose(1, 0, 2)
    dmod_sh = lax.dynamic_slice_in_dim(dmod_lb, chip * ada_sh, ada_sh, axis=2)
    dmod_sh = jnp.concatenate([dmod_sh, jnp.zeros_like(dmod_sh)], axis=1)
    g_ada_w = _ada_bwd(c_pad, dmod_sh, name="ada_bwd")
    run_adamw("ada_w", g_ada_w.reshape(1, depth * d, ada_sh), ada_w.shape)
    parts_b = dmod_all.reshape(8, depth * 6 * d // LANES, LANES)
    run_adamw("ada_b", parts_b, ada_b.shape)

    outs = [loss, grad_x]
    for tree in (out_g, out_d, out_m, out_v):
        outs += [tree[n] for n in WEIGHT_NAMES]
    return tuple(outs)
```

```python
import functools
import math

import jax
import jax.numpy as jnp
from jax import lax
from jax.experimental import pallas as pl
from jax.experimental.pallas import tpu as pltpu

F32 = jnp.float32
BF16 = jnp.bfloat16

D_MODEL = 1024
DEPTH = 4
HEAD_DIM = 64
LANES = 128
D_FF = 4 * D_MODEL
EPS = 1e-6
SB_W, FOX_W, SGU_W = 256, 512, 256
FOX_HEADS = 8
SGU_CHUNK = 128
IN_W = 2824
ATT_W = 3 * SB_W + 3 * FOX_W
PROJ_W = 3072
CB_QA, CB_KA, CB_VA = 0, 2, 4
CB_QB, CB_KB, CB_VB = 6, 10, 14
CB_UC, CB_VC, CB_FL = 18, 20, 22
ATT_T = 256
VMEM_LIMIT = 56 * 2 ** 20
SKIP_LOG = 110.0

ADAM_LR, ADAM_B1, ADAM_B2, ADAM_EPS, ADAM_WD, ADAM_STEP = 0.001, 0.9, 0.999, 1e-08, 0.01, 10

MESH = pl.DeviceIdType.MESH


def _pcall(body, *, name, out_shape, grid=(), in_specs=None, out_specs=None, scratch_shapes=(),
           semantics=None):
    params = dict(vmem_limit_bytes=VMEM_LIMIT)
    if semantics is not None:
        params["dimension_semantics"] = semantics
    kwargs = {}
    if in_specs is not None:
        kwargs["in_specs"] = in_specs
    if out_specs is not None:
        kwargs["out_specs"] = out_specs
    return pl.pallas_call(body, name=name, out_shape=out_shape, grid=grid,
                          scratch_shapes=list(scratch_shapes),
                          compiler_params=pltpu.CompilerParams(**params), **kwargs)


def _dot(a, b):
    return jnp.dot(a, b, preferred_element_type=F32)


def _dot_nt(a, b):
    return lax.dot_general(a, b, (((1,), (1,)), ((), ())), preferred_element_type=F32)


def _dot_tn(a, b):
    return lax.dot_general(a, b, (((0,), (0,)), ((), ())), preferred_element_type=F32)


def _split2(x):
    hi = x.astype(BF16)
    lo = (x - hi.astype(F32)).astype(BF16)
    return hi, lo


def _ones_dot(x, ones_bf16):
    hi, lo = _split2(x)
    return _dot(hi, ones_bf16) + _dot(lo, ones_bf16)


def _rowwise(fn, fulls, vecs, out_dtypes, n_vec_out, *, name, tr):
    s, n = fulls[0].shape
    tr = min(tr, s)
    assert s % tr == 0, (name, s, tr)
    nf, nv, nfo = len(fulls), len(vecs), len(out_dtypes)

    def body(*refs):
        fi, vi = refs[:nf], refs[nf:nf + nv]
        fo, vo = refs[nf + nv:nf + nv + nfo], refs[nf + nv + nfo:]
        outs_f, outs_v = fn([r[...] for r in fi], [r[...] for r in vi])
        for r, o in zip(fo, outs_f):
            r[...] = o.astype(r.dtype)
        if n_vec_out:
            @pl.when(pl.program_id(0) == 0)
            def _():
                for r in vo:
                    r[...] = jnp.zeros_like(r)
            for r, o in zip(vo, outs_v):
                r[...] += o

    full_spec = pl.BlockSpec((tr, n), lambda i: (i, 0))
    vec_specs = [pl.BlockSpec(v.shape, lambda i: (0, 0)) for v in vecs]
    out_vec_spec = pl.BlockSpec((1, n), lambda i: (0, 0))
    out_shape = [jax.ShapeDtypeStruct((s, n), dt) for dt in out_dtypes]
    out_shape += [jax.ShapeDtypeStruct((1, n), F32)] * n_vec_out
    outs = _pcall(body, name=name, grid=(s // tr,),
                  in_specs=[full_spec] * nf + vec_specs,
                  out_specs=[full_spec] * nfo + [out_vec_spec] * n_vec_out,
                  out_shape=out_shape,
                  semantics=("arbitrary",) if n_vec_out else ("parallel",))(*fulls, *vecs)
    return outs[:nfo], outs[nfo:]


def _colsum(t):
    return jnp.sum(t, axis=0, keepdims=True)


def _rms_mod(x, g, sc, sh):
    r = lax.rsqrt(jnp.mean(x * x, axis=-1, keepdims=True) + EPS)
    return (x * r * g) * (1.0 + sc) + sh


def _norm_mod_fwd(x, g, sc, sh, *, name):
    def fn(f, v):
        return [_rms_mod(f[0], v[0], v[1], v[2])], []
    (h,), _ = _rowwise(fn, [x], [g, sc, sh], [BF16], 0, name=name, tr=512)
    return h


def _resid_norm_mod_fwd(x, m, gate, g, sc, sh, *, name):
    def fn(f, v):
        xn = f[0] + v[0] * f[1]
        return [xn, _rms_mod(xn, v[1], v[2], v[3])], []
    (xn, h), _ = _rowwise(fn, [x, m], [gate, g, sc, sh], [F32, BF16], 0, name=name, tr=512)
    return xn, h


def _norm_mod_bwd(x, dh, dres, g, sc, *, name):
    def fn(f, v):
        xv, dhv, dr = f
        gv, scv = v
        r = lax.rsqrt(jnp.mean(xv * xv, axis=-1, keepdims=True) + EPS)
        xh = xv * r
        dn = dhv * (1.0 + scv)
        dxh = dn * gv
        dx = dr + r * (dxh - xh * jnp.mean(dxh * xh, axis=-1, keepdims=True))
        return [dx], [_colsum(dn * xh), _colsum(dhv * (xh * gv)), _colsum(dhv)]
    (dx,), (dg, dsc, dsh) = _rowwise(fn, [x, dh, dres], [g, sc], [F32], 3, name=name, tr=256)
    return dx, dg, dsc, dsh


def _gate_bwd(dx, m, gate, *, name):
    def fn(f, v):
        return [f[0] * v[0]], [_colsum(f[0] * f[1])]
    (dm,), (dgate,) = _rowwise(fn, [dx, m], [gate], [BF16], 1, name=name, tr=512)
    return dm, dgate


def _relu2_fwd(a, *, name):
    def fn(f, v):
        r = jnp.maximum(f[0], 0.0)
        return [r * r], []
    (r,), _ = _rowwise(fn, [a], [], [BF16], 0, name=name, tr=128)
    return r


def _relu2_bwd(dr, a, *, name):
    def fn(f, v):
        return [f[0] * (2.0 * jnp.maximum(f[1], 0.0))], []
    (da,), _ = _rowwise(fn, [dr, a], [], [BF16], 0, name=name, tr=128)
    return da


def _loss_fwd_bwd(x, m, gate, target, *, name):
    n = x.shape[1]

    def fn(f, v):
        err = f[0] + v[0] * f[1] - f[2]
        return [err * (1.0 / n)], [_colsum(err * err)]
    (dy,), (sq,) = _rowwise(fn, [x, m, target], [gate], [F32], 1, name=name, tr=512)
    return sq, dy


def _matmul(a, b, *, name, ta=False, tb=False, out_dtype=F32, tm=1024, tn=1024, tk=1024):
    m = a.shape[1] if ta else a.shape[0]
    k = a.shape[0] if ta else a.shape[1]
    n = b.shape[0] if tb else b.shape[1]
    assert k == (b.shape[1] if tb else b.shape[0])
    tm, tn, tk = min(tm, m), min(tn, n), min(tk, k)
    assert m % tm == 0 and n % tn == 0 and k % tk == 0, (name, m, n, k)
    nk = k // tk
    dims = (((0 if ta else 1,), (1 if tb else 0,)), ((), ()))

    def body(a_ref, b_ref, o_ref, acc_ref):
        kk = pl.program_id(2)

        @pl.when(kk == 0)
        def _():
            acc_ref[...] = jnp.zeros_like(acc_ref)
        acc_ref[...] += lax.dot_general(a_ref[...].astype(BF16), b_ref[...].astype(BF16), dims,
                                        preferred_element_type=F32)

        @pl.when(kk == nk - 1)
        def _():
            o_ref[...] = acc_ref[...].astype(o_ref.dtype)

    a_spec = (pl.BlockSpec((tk, tm), lambda i, j, kk: (kk, i)) if ta
              else pl.BlockSpec((tm, tk), lambda i, j, kk: (i, kk)))
    b_spec = (pl.BlockSpec((tn, tk), lambda i, j, kk: (j, kk)) if tb
              else pl.BlockSpec((tk, tn), lambda i, j, kk: (kk, j)))
    return _pcall(body, name=name, grid=(m // tm, n // tn, nk),
                  in_specs=[a_spec, b_spec],
                  out_specs=pl.BlockSpec((tm, tn), lambda i, j, kk: (i, j)),
                  out_shape=jax.ShapeDtypeStruct((m, n), out_dtype),
                  scratch_shapes=[pltpu.VMEM((tm, tn), F32)],
                  semantics=("parallel", "parallel", "arbitrary"))(a, b)


def _lane_masks():
    lane = lax.broadcasted_iota(jnp.int32, (1, LANES), 1)
    return [lane < HEAD_DIM, lane >= HEAD_DIM]


def _tri_iotas(t):
    r = lax.broadcasted_iota(jnp.int32, (t, t), 0)
    c = lax.broadcasted_iota(jnp.int32, (t, t), 1)
    return r, c


def _rows(j, t):
    return pl.ds(pl.multiple_of(j * t, t), t)


def _neg_softplus(z):
    e = jnp.exp(-jnp.abs(z))
    return -(jnp.maximum(z, 0.0) + jnp.log(1.0 + e)), e


def _sb_fwd(proj, *, name):
    s = proj.shape[0]
    t = min(ATT_T, s)
    scale = HEAD_DIM ** -0.5

    def body(q_ref, k_ref, v_ref, o_ref):
        i = pl.program_id(1)
        hm = _lane_masks()
        q = q_ref[...] * scale
        qh = [jnp.where(mk, q, 0.0).astype(BF16) for mk in hm]
        r, c = _tri_iotas(t)
        later = (r > c).astype(BF16)
        q2 = jnp.concatenate(qh, axis=0)
        causal2 = jnp.concatenate([c < r, c < r], axis=0)

        def scores(j):
            return _dot_nt(q2, k_ref[_rows(j, t), :].astype(BF16))

        def chunk(j, carry, z, masked):
            e_run, acc = carry
            vb = v_ref[_rows(j, t), :].astype(BF16)
            l, _ = _neg_softplus(z)
            if masked:
                l = jnp.where(causal2, l, 0.0)
            between = _ones_dot(l, later) + e_run
            a = jnp.exp(z + l + between)
            if masked:
                a = jnp.where(causal2, a, 0.0)
            return e_run + jnp.sum(l, axis=1, keepdims=True), acc + _dot(a.astype(BF16), vb)

        init = (jnp.zeros((2 * t, 1), F32), jnp.zeros((2 * t, LANES), F32))
        carry = chunk(i, init, scores(i), True)

        def step(st):
            j, cr, z = st
            z_next = scores(jnp.maximum(j - 1, 0))
            return j - 1, chunk(j, cr, z, False), z_next

        _, (_, acc), _ = lax.while_loop(
            lambda st: (st[0] >= 0) & (jnp.max(st[1][0]) > -SKIP_LOG), step,
            (i - 1, carry, scores(jnp.maximum(i - 1, 0))))
        o_ref[...] = jnp.where(hm[0], acc[:t], acc[t:])

    blk = lambda cb: pl.BlockSpec((t, LANES), lambda p, i: (i, cb + p))
    full = lambda cb: pl.BlockSpec((s, LANES), lambda p, i: (0, cb + p))
    out_blk = pl.BlockSpec((t, LANES), lambda p, i: (i, p))
    return _pcall(body, name=name, grid=(SB_W // LANES, s // t),
                  in_specs=[blk(CB_QA), full(CB_KA), full(CB_VA)],
                  out_specs=out_blk,
                  out_shape=jax.ShapeDtypeStruct((s, SB_W), F32),
                  semantics=("parallel", "arbitrary"))(proj, proj, proj)


def _sb_bwd(proj, dmixed, *, name):
    s = proj.shape[0]
    t = min(ATT_T, s)
    scale = HEAD_DIM ** -0.5

    def body(q_ref, k_ref, v_ref, do_ref, dq_ref, dk_ref, dv_ref):
        i = pl.program_id(1)

        @pl.when(i == 0)
        def _():
            dk_ref[...] = jnp.zeros_like(dk_ref)
            dv_ref[...] = jnp.zeros_like(dv_ref)

        hm = _lane_masks()
        q = q_ref[...] * scale
        do = do_ref[...]
        qh = [jnp.where(mk, q, 0.0).astype(BF16) for mk in hm]
        doh = [jnp.where(mk, do, 0.0).astype(BF16) for mk in hm]
        r, c = _tri_iotas(t)
        upto = (r <= c).astype(BF16)
        before = (r < c).astype(BF16)
        q2 = jnp.concatenate(qh, axis=0)
        do2 = jnp.concatenate(doh, axis=0)
        causal2 = jnp.concatenate([c < r, c < r], axis=0)

        def scores(j):
            return _dot_nt(q2, k_ref[_rows(j, t), :].astype(BF16))

        def totals(j, e_run, masked):
            l, _ = _neg_softplus(scores(j))
            if masked:
                l = jnp.where(causal2, l, 0.0)
            return e_run + jnp.sum(l, axis=1, keepdims=True)

        j_stop, lt = lax.while_loop(
            lambda st: (st[0] >= 0) & (jnp.max(st[1]) > -SKIP_LOG),
            lambda st: (st[0] - 1, totals(st[0], st[1], False)),
            (i - 1, totals(i, jnp.zeros((2 * t, 1), F32), True)))

        def products(j):
            return scores(j), _dot_nt(do2, v_ref[_rows(j, t), :].astype(BF16))

        def chunk(j, carry, z, da, masked):
            l_run, g_run, dq = carry
            l, e = _neg_softplus(z)
            beta = jnp.where(z >= 0.0, 1.0, e) / (1.0 + e)
            if masked:
                l = jnp.where(causal2, l, 0.0)
            prefix = _ones_dot(l, upto) + l_run
            a = jnp.exp(z + l + (lt - prefix))
            if masked:
                a = jnp.where(causal2, a, 0.0)
            g = a * da
            g_before = _ones_dot(g, before) + g_run
            dz = g * (1.0 - beta) - beta * g_before
            if masked:
                dz = jnp.where(causal2, dz, 0.0)
            dzb = dz.astype(BF16)
            dk_ref[_rows(j, t), :] += _dot_tn(dzb, q2)
            dv_ref[_rows(j, t), :] += _dot_tn(a.astype(BF16), do2)
            return (l_run + jnp.sum(l, axis=1, keepdims=True),
                    g_run + jnp.sum(g, axis=1, keepdims=True),
                    dq + _dot(dzb, k_ref[_rows(j, t), :].astype(BF16)))

        def step(j, st):
            cr, z, da = st
            z_next, da_next = products(j + 1)
            return chunk(j, cr, z, da, False), z_next, da_next

        init = (jnp.zeros((2 * t, 1), F32), jnp.zeros((2 * t, 1), F32),
                jnp.zeros((2 * t, LANES), F32))
        carry, z, da = lax.fori_loop(j_stop + 1, i, step, (init,) + products(j_stop + 1))
        dq2 = chunk(i, carry, z, da, True)[2]
        dq_ref[...] = jnp.where(hm[0], dq2[:t], dq2[t:]) * scale

    blk = lambda cb: pl.BlockSpec((t, LANES), lambda p, i: (i, cb + p))
    full = lambda cb: pl.BlockSpec((s, LANES), lambda p, i: (0, cb + p))
    out_blk = pl.BlockSpec((t, LANES), lambda p, i: (i, p))
    out_full = pl.BlockSpec((s, LANES), lambda p, i: (0, p))
    return _pcall(body, name=name, grid=(SB_W // LANES, s // t),
                  in_specs=[blk(CB_QA), full(CB_KA), full(CB_VA), blk(0)],
                  out_specs=[out_blk, out_full, out_full],
                  out_shape=[jax.ShapeDtypeStruct((s, SB_W), F32)] * 3,
                  semantics=("parallel", "arbitrary"))(proj, proj, proj, dmixed)


def _group_mean(v, lo):
    s0 = jnp.sum(jnp.where(lo, v, 0.0), axis=1, keepdims=True)
    s1 = jnp.sum(jnp.where(lo, 0.0, v), axis=1, keepdims=True)
    return jnp.where(lo, s0, s1) * (1.0 / HEAD_DIM)


def _fox_prep_fwd(proj, qg, kg, *, name):
    s = proj.shape[0]
    tr = min(512, s)

    def body(q_ref, k_ref, qg_ref, kg_ref, qn_ref, kn_ref, kmax_ref):
        lo = _lane_masks()[0]
        for x_ref, g_ref, o_ref in ((q_ref, qg_ref, qn_ref), (k_ref, kg_ref, kn_ref)):
            x = x_ref[...]
            o_ref[...] = x * lax.rsqrt(_group_mean(x * x, lo) + EPS) * g_ref[...]

        @pl.when(pl.program_id(1) == 0)
        def _():
            kmax_ref[...] = jnp.zeros_like(kmax_ref)
        kn = kn_ref[...]
        norms = jnp.sqrt(_group_mean(kn * kn, lo) * HEAD_DIM)
        kmax_ref[...] = jnp.maximum(kmax_ref[...], jnp.max(norms, axis=0, keepdims=True))

    blk = lambda cb: pl.BlockSpec((tr, LANES), lambda p, i: (i, cb + p))
    vec = pl.BlockSpec((1, LANES), lambda p, i: (0, 0))
    out_blk = pl.BlockSpec((tr, LANES), lambda p, i: (i, p))
    return _pcall(body, name=name, grid=(FOX_W // LANES, s // tr),
                  in_specs=[blk(CB_QB), blk(CB_KB), vec, vec],
                  out_specs=[out_blk, out_blk, pl.BlockSpec((1, LANES), lambda p, i: (0, p))],
                  out_shape=[jax.ShapeDtypeStruct((s, FOX_W), F32)] * 2
                  + [jax.ShapeDtypeStruct((1, FOX_W), F32)],
                  semantics=("parallel", "arbitrary"))(proj, proj, qg, kg)


def _fox_prep_bwd(proj, dqn, dkn, qg, kg, *, name):
    s = proj.shape[0]
    tr = min(512, s)

    def body(q_ref, k_ref, dqn_ref, dkn_ref, qg_ref, kg_ref, dq_ref, dk_ref, dqg_ref, dkg_ref):
        @pl.when((pl.program_id(0) == 0) & (pl.program_id(1) == 0))
        def _():
            dqg_ref[...] = jnp.zeros_like(dqg_ref)
            dkg_ref[...] = jnp.zeros_like(dkg_ref)

        lo = _lane_masks()[0]
        for x_ref, dy_ref, g_ref, dx_ref, dg_ref in ((q_ref, dqn_ref, qg_ref, dq_ref, dqg_ref),
                                                     (k_ref, dkn_ref, kg_ref, dk_ref, dkg_ref)):
            x, dy = x_ref[...], dy_ref[...]
            r = lax.rsqrt(_group_mean(x * x, lo) + EPS)
            xh = x * r
            dxh = dy * g_ref[...]
            dx_ref[...] = r * (dxh - xh * _group_mean(dxh * xh, lo))
            dg_ref[...] += _colsum(dy * xh)

    blk = lambda cb: pl.BlockSpec((tr, LANES), lambda p, i: (i, cb + p))
    vec = pl.BlockSpec((1, LANES), lambda p, i: (0, 0))
    out_blk = pl.BlockSpec((tr, LANES), lambda p, i: (i, p))
    return _pcall(body, name=name, grid=(FOX_W // LANES, s // tr),
                  in_specs=[blk(CB_QB), blk(CB_KB), out_blk, out_blk, vec, vec],
                  out_specs=[out_blk, out_blk, vec, vec],
                  out_shape=[jax.ShapeDtypeStruct((s, FOX_W), F32)] * 2
                  + [jax.ShapeDtypeStruct((1, LANES), F32)] * 2,
                  semantics=("arbitrary", "arbitrary"))(proj, proj, dqn, dkn, qg, kg)


def _split3_dot(tri_bf16, x):
    hi = x.astype(BF16)
    r1 = x - hi.astype(F32)
    mid = r1.astype(BF16)
    lo = (r1 - mid.astype(F32)).astype(BF16)
    return _dot(tri_bf16, hi) + _dot(tri_bf16, mid) + _dot(tri_bf16, lo)


def _forget_cumsum_fwd(proj, b_pad, *, name):
    s = proj.shape[0]
    tb = min(256, s)

    def body(fl_ref, b_ref, cf_ref, run_ref):
        @pl.when(pl.program_id(0) == 0)
        def _():
            run_ref[...] = jnp.zeros_like(run_ref)
        lf, _ = _neg_softplus(-(fl_ref[...] + b_ref[...]))
        r, c = _tri_iotas(tb)
        incl = _split3_dot((c <= r).astype(BF16), lf) + run_ref[...]
        cf_ref[...] = incl
        run_ref[...] = incl[tb - 1:tb, :]

    return _pcall(body, name=name, grid=(s // tb,),
                  in_specs=[pl.BlockSpec((tb, LANES), lambda i: (i, CB_FL)),
                            pl.BlockSpec((1, LANES), lambda i: (0, 0))],
                  out_specs=pl.BlockSpec((tb, LANES), lambda i: (i, 0)),
                  out_shape=jax.ShapeDtypeStruct((s, LANES), F32),
                  scratch_shapes=[pltpu.VMEM((1, LANES), F32)],
                  semantics=("arbitrary",))(proj, b_pad)


def _forget_cumsum_bwd(proj, b_pad, dcf, *, name):
    s = proj.shape[0]
    tb = min(256, s)
    nb = s // tb

    def body(fl_ref, b_ref, dcf_ref, dfl_ref, db_ref, run_ref):
        @pl.when(pl.program_id(0) == 0)
        def _():
            run_ref[...] = jnp.zeros_like(run_ref)
            db_ref[...] = jnp.zeros_like(db_ref)
        r, c = _tri_iotas(tb)
        dlf = _split3_dot((c >= r).astype(BF16), dcf_ref[...]) + run_ref[...]
        run_ref[...] = dlf[0:1, :]
        xv = fl_ref[...] + b_ref[...]
        e = jnp.exp(-jnp.abs(xv))
        sig_neg = jnp.where(xv >= 0.0, e, 1.0) / (1.0 + e)
        dfl = dlf * sig_neg
        dfl_ref[...] = dfl
        db_ref[...] += _colsum(dfl)

    return _pcall(body, name=name, grid=(nb,),
                  in_specs=[pl.BlockSpec((tb, LANES), lambda i: (nb - 1 - i, CB_FL)),
                            pl.BlockSpec((1, LANES), lambda i: (0, 0)),
                            pl.BlockSpec((tb, LANES), lambda i: (nb - 1 - i, 0))],
                  out_specs=[pl.BlockSpec((tb, LANES), lambda i: (nb - 1 - i, 0)),
                             pl.BlockSpec((1, LANES), lambda i: (0, 0))],
                  out_shape=[jax.ShapeDtypeStruct((s, LANES), F32),
                             jax.ShapeDtypeStruct((1, LANES), F32)],
                  scratch_shapes=[pltpu.VMEM((1, LANES), F32)],
                  semantics=("arbitrary",))(proj, b_pad, dcf)


def _fox_bias_q(cfc, p, h):
    lane = lax.broadcasted_iota(jnp.int32, (1, LANES), 1)
    return jnp.sum(jnp.where(lane == 2 * p + h, cfc, 0.0), axis=1, keepdims=True)


def _fox_score_bound(q, kmax_row, hm):
    out = []
    for h in range(2):
        qnorm = jnp.sqrt(jnp.sum(jnp.where(hm[h], q * q, 0.0), axis=1, keepdims=True))
        out.append(1.02 * qnorm * kmax_row[:, h * HEAD_DIM:h * HEAD_DIM + 1])
    return out


def _fox_live(cfr_ref, j, t, tops):
    jc = jnp.maximum(j, 0)
    worst = []
    for h in range(2):
        cf_min = jnp.min(cfr_ref[0, pl.ds(h, 1), _rows(jc, t)], axis=1, keepdims=True)
        worst.append(jnp.max(tops[h] - cf_min))
    return (j >= 0) & (jnp.maximum(worst[0], worst[1]) > -SKIP_LOG)


def _fox_fwd(proj, qn, kn, cf, cf_rows, kmax, *, name):
    s = proj.shape[0]
    t = min(ATT_T, s)
    scale = HEAD_DIM ** -0.5

    def body(q_ref, k_ref, v_ref, cfc_ref, cfr_ref, kmax_ref, o_ref, lse_ref):
        p, i = pl.program_id(0), pl.program_id(1)
        hm = _lane_masks()
        q = q_ref[...] * scale
        qh = [jnp.where(mk, q, 0.0).astype(BF16) for mk in hm]
        cfc = cfc_ref[...]
        bq = [_fox_bias_q(cfc, p, h) for h in range(2)]
        qk_top = _fox_score_bound(q, kmax_ref[...], hm)
        r, c = _tri_iotas(t)
        causal = c <= r

        q2 = jnp.concatenate(qh, axis=0)
        causal2 = jnp.concatenate([causal, causal], axis=0)

        def scores(j):
            return _dot_nt(q2, k_ref[_rows(j, t), :].astype(BF16))

        def chunk(j, carry, z2, masked):
            m_run, l_run, acc = carry
            vb = v_ref[_rows(j, t), :].astype(BF16)
            z = jnp.concatenate(
                [z2[h * t:(h + 1) * t] + (bq[h] - cfr_ref[0, pl.ds(h, 1), _rows(j, t)])
                 for h in range(2)], axis=0)
            if masked:
                z = jnp.where(causal2, z, -1e30)
            m_new = jnp.maximum(m_run, jnp.max(z, axis=1, keepdims=True))
            alpha = jnp.exp(m_run - m_new)
            pr = jnp.exp(z - m_new)
            return (m_new, alpha * l_run + jnp.sum(pr, axis=1, keepdims=True),
                    alpha * acc + _dot(pr.astype(BF16), vb))

        init = (jnp.full((2 * t, 1), -1e30, F32), jnp.zeros((2 * t, 1), F32),
                jnp.zeros((2 * t, LANES), F32))
        carry = chunk(i, init, scores(i), True)

        def live(j, cr):
            return _fox_live(cfr_ref, j, t,
                             [qk_top[h] + bq[h] - cr[0][h * t:(h + 1) * t] for h in range(2)])

        def step(st):
            j, _, cr, z2 = st
            z2_next = scores(jnp.maximum(j - 1, 0))
            cr = chunk(j, cr, z2, False)
            return j - 1, live(j - 1, cr), cr, z2_next

        m_fin, l_fin, acc = lax.while_loop(
            lambda st: st[1], step,
            (i - 1, live(i - 1, carry), carry, scores(jnp.maximum(i - 1, 0))))[2]
        o2 = acc / l_fin
        lse2 = m_fin + jnp.log(l_fin)
        o_ref[...] = jnp.where(hm[0], o2[:t], o2[t:])
        lse_ref[...] = jnp.where(hm[0], lse2[:t], lse2[t:])

    blk = pl.BlockSpec((t, LANES), lambda p, i: (i, p))
    full = pl.BlockSpec((s, LANES), lambda p, i: (0, p))
    return _pcall(body, name=name, grid=(FOX_W // LANES, s // t),
                  in_specs=[blk, full, pl.BlockSpec((s, LANES), lambda p, i: (0, CB_VB + p)),
                            pl.BlockSpec((t, LANES), lambda p, i: (i, 0)),
                            pl.BlockSpec((1, 2, s), lambda p, i: (p, 0, 0)),
                            pl.BlockSpec((1, LANES), lambda p, i: (0, p))],
                  out_specs=[blk, blk],
                  out_shape=[jax.ShapeDtypeStruct((s, FOX_W), F32)] * 2,
                  semantics=("parallel", "arbitrary"))(qn, kn, proj, cf, cf_rows, kmax)


def _fox_bwd(proj, qn, kn, cf, cf_rows, kmax, do, o, lse, *, name):
    s = proj.shape[0]
    t = min(ATT_T, s)
    scale = HEAD_DIM ** -0.5

    def body(q_ref, k_ref, v_ref, cfc_ref, cfr_ref, kmax_ref, do_ref, o_ref, lse_ref,
             dq_ref, dk_ref, dv_ref, dcf_ref, dcfq_ref):
        p, i = pl.program_id(0), pl.program_id(1)

        @pl.when(i == 0)
        def _():
            dk_ref[...] = jnp.zeros_like(dk_ref)
            dv_ref[...] = jnp.zeros_like(dv_ref)
            dcf_ref[...] = jnp.zeros_like(dcf_ref)

        hm = _lane_masks()
        q = q_ref[...] * scale
        do = do_ref[...]
        dov = do * o_ref[...]
        qh = [jnp.where(mk, q, 0.0).astype(BF16) for mk in hm]
        doh = [jnp.where(mk, do, 0.0).astype(BF16) for mk in hm]
        delta = [jnp.sum(jnp.where(mk, dov, 0.0), axis=1, keepdims=True) for mk in hm]
        lsev = lse_ref[...]
        lse = [lsev[:, 0:1], lsev[:, HEAD_DIM:HEAD_DIM + 1]]
        cfc = cfc_ref[...]
        bq = [_fox_bias_q(cfc, p, h) - lse[h] for h in range(2)]
        qk_top = _fox_score_bound(q, kmax_ref[...], hm)
        tops = [qk_top[h] + bq[h] for h in range(2)]
        r, c = _tri_iotas(t)
        j_stop = lax.while_loop(lambda st: st[1],
                                lambda st: (st[0] - 1, _fox_live(cfr_ref, st[0] - 1, t, tops)),
                                (i - 1, _fox_live(cfr_ref, i - 1, t, tops)))[0]
        q2 = jnp.concatenate(qh, axis=0)
        do2 = jnp.concatenate(doh, axis=0)
        delta2 = jnp.concatenate(delta, axis=0)
        causal2 = jnp.concatenate([c <= r, c <= r], axis=0)

        def products(j):
            return (_dot_nt(q2, k_ref[_rows(j, t), :].astype(BF16)),
                    _dot_nt(do2, v_ref[_rows(j, t), :].astype(BF16)))

        def chunk(j, carry, z2, dp, masked):
            dq, row_sum = carry
            z = jnp.concatenate(
                [z2[h * t:(h + 1) * t] + (bq[h] - cfr_ref[0, pl.ds(h, 1), _rows(j, t)])
                 for h in range(2)], axis=0)
            pr = jnp.exp(z)
            if masked:
                pr = jnp.where(causal2, pr, 0.0)
            ds = pr * (dp - delta2)
            dsb = ds.astype(BF16)
            dk_ref[_rows(j, t), :] += _dot_tn(dsb, q2)
            dv_ref[_rows(j, t), :] += _dot_tn(pr.astype(BF16), do2)
            for h in range(2):
                dcf_ref[0, pl.ds(h, 1), _rows(j, t)] -= jnp.sum(ds[h * t:(h + 1) * t], axis=0,
                                                               keepdims=True)
            return (dq + _dot(dsb, k_ref[_rows(j, t), :].astype(BF16)),
                    row_sum + jnp.sum(ds, axis=1, keepdims=True))

        def step(j, st):
            cr, z2, dp = st
            z2_next, dp_next = products(j + 1)
            return chunk(j, cr, z2, dp, False), z2_next, dp_next

        init = (jnp.zeros((2 * t, LANES), F32), jnp.zeros((2 * t, 1), F32))
        carry, z2, dp = lax.fori_loop(j_stop + 1, i, step, (init,) + products(j_stop + 1))
        dq2, row_sum = chunk(i, carry, z2, dp, True)
        dq_ref[...] = jnp.where(hm[0], dq2[:t], dq2[t:]) * scale
        dcfq_ref[...] = jnp.where(hm[0], row_sum[:t], row_sum[t:])

    blk = pl.BlockSpec((t, LANES), lambda p, i: (i, p))
    full = pl.BlockSpec((s, LANES), lambda p, i: (0, p))
    rows = pl.BlockSpec((1, 2, s), lambda p, i: (p, 0, 0))
    return _pcall(body, name=name, grid=(FOX_W // LANES, s // t),
                  in_specs=[blk, full, pl.BlockSpec((s, LANES), lambda p, i: (0, CB_VB + p)),
                            pl.BlockSpec((t, LANES), lambda p, i: (i, 0)), rows,
                            pl.BlockSpec((1, LANES), lambda p, i: (0, p)),
                            pl.BlockSpec((t, LANES), lambda p, i: (i, SB_W // LANES + p)),
                            blk, blk],
                  out_specs=[blk, full, full, rows, blk],
                  out_shape=[jax.ShapeDtypeStruct((s, FOX_W), F32)] * 3
                  + [jax.ShapeDtypeStruct((FOX_W // LANES, 2, s), F32),
                     jax.ShapeDtypeStruct((s, FOX_W), F32)],
                  semantics=("parallel", "arbitrary"))(qn, kn, proj, cf, cf_rows, kmax, do, o, lse)


_GELU_C0 = math.sqrt(2.0 / math.pi)
_GELU_C1 = 0.044715


def _gelu(x):
    th = jnp.tanh(_GELU_C0 * (x + _GELU_C1 * (x * x * x)))
    return 0.5 * x * (1.0 + th), th


def _gelu_grad(x, th):
    return 0.5 * (1.0 + th) + 0.5 * x * (1.0 - th * th) * (_GELU_C0 * (1.0 + 3.0 * _GELU_C1 * x * x))


def _sgu_mix(wm, vn_c, lo, bcol):
    return jnp.where(lo, _dot(wm[0], vn_c) + bcol[0], _dot(wm[1], vn_c) + bcol[1])


def _sgu_fwd(proj, w, b_cols, gn, *, name):
    s = proj.shape[0]
    tr = min(512, s)
    ch = SGU_CHUNK

    def body(u_ref, v_ref, w_ref, b_ref, gn_ref, o_ref):
        lo = _lane_masks()[0]
        r, c = _tri_iotas(ch)
        wm = [jnp.where(c <= r, w_ref[h], 0.0).astype(BF16) for h in range(2)]
        bcol = [b_ref[0, :, h:h + 1] for h in range(2)]
        for n in range(tr // ch):
            rows = slice(n * ch, (n + 1) * ch)
            u, _ = _gelu(u_ref[rows, :])
            vg, _ = _gelu(v_ref[rows, :])
            vn = vg * lax.rsqrt(_group_mean(vg * vg, lo) + EPS) * gn_ref[0]
            o_ref[rows, :] = u * _sgu_mix(wm, vn.astype(BF16), lo, bcol)

    blk = lambda cb: pl.BlockSpec((tr, LANES), lambda p, i: (i, cb + p))
    return _pcall(body, name=name, grid=(SGU_W // LANES, s // tr),
                  in_specs=[blk(CB_UC), blk(CB_VC),
                            pl.BlockSpec((2, ch, ch), lambda p, i: (p, 0, 0)),
                            pl.BlockSpec((1, ch, 2), lambda p, i: (p, 0, 0)),
                            pl.BlockSpec((1, 1, LANES), lambda p, i: (p, 0, 0))],
                  out_specs=pl.BlockSpec((tr, LANES), lambda p, i: (i, p)),
                  out_shape=jax.ShapeDtypeStruct((s, SGU_W), F32),
                  semantics=("parallel", "parallel"))(proj, proj, w, b_cols, gn)


def _sgu_bwd(proj, dmixed, w, w_t, b_cols, gn, *, name):
    s = proj.shape[0]
    tr = min(512, s)
    ch = SGU_CHUNK
    cb_do = (SB_W + FOX_W) // LANES

    def body(u_ref, v_ref, do_ref, w_ref, wt_ref, b_ref, gn_ref,
             du_ref, dv_ref, dw_ref, db_ref, dgn_ref):
        @pl.when(pl.program_id(1) == 0)
        def _():
            dw_ref[...] = jnp.zeros_like(dw_ref)
            db_ref[...] = jnp.zeros_like(db_ref)
            dgn_ref[...] = jnp.zeros_like(dgn_ref)

        hm = _lane_masks()
        lo = hm[0]
        r, c = _tri_iotas(ch)
        wm = [jnp.where(c <= r, w_ref[h], 0.0).astype(BF16) for h in range(2)]
        wtm = [jnp.where(r <= c, wt_ref[h], 0.0).astype(BF16) for h in range(2)]
        bcol = [b_ref[0, :, h:h + 1] for h in range(2)]
        gnv = gn_ref[0]
        for n in range(tr // ch):
            rows = slice(n * ch, (n + 1) * ch)
            uc, vc, do = u_ref[rows, :], v_ref[rows, :], do_ref[rows, :]
            u, thu = _gelu(uc)
            vg, thv = _gelu(vc)
            rinv = lax.rsqrt(_group_mean(vg * vg, lo) + EPS)
            xh = vg * rinv
            vnb = (xh * gnv).astype(BF16)
            mix = _sgu_mix(wm, vnb, lo, bcol)
            du_ref[rows, :] = do * mix * _gelu_grad(uc, thu)
            dm = do * u
            dmb = dm.astype(BF16)
            dvn = jnp.where(lo, _dot(wtm[0], dmb), _dot(wtm[1], dmb))
            for h in range(2):
                dmh = jnp.where(hm[h], dm, 0.0)
                dw_ref[h] += jnp.where(c <= r, _dot_nt(dmh.astype(BF16), vnb), 0.0)
                db_ref[0, :, h:h + 1] += jnp.sum(dmh, axis=1, keepdims=True)
            dgn_ref[0] += _colsum(dvn * xh)
            dxh = dvn * gnv
            dvg = rinv * (dxh - xh * _group_mean(dxh * xh, lo))
            dv_ref[rows, :] = dvg * _gelu_grad(vc, thv)

    blk = lambda cb: pl.BlockSpec((tr, LANES), lambda p, i: (i, cb + p))
    w_spec = pl.BlockSpec((2, ch, ch), lambda p, i: (p, 0, 0))
    b_spec = pl.BlockSpec((1, ch, 2), lambda p, i: (p, 0, 0))
    g_spec = pl.BlockSpec((1, 1, LANES), lambda p, i: (p, 0, 0))
    out_blk = pl.BlockSpec((tr, LANES), lambda p, i: (i, p))
    return _pcall(body, name=name, grid=(SGU_W // LANES, s // tr),
                  in_specs=[blk(CB_UC), blk(CB_VC), blk(cb_do), w_spec, w_spec, b_spec, g_spec],
                  out_specs=[out_blk, out_blk, w_spec, b_spec, g_spec],
                  out_shape=[jax.ShapeDtypeStruct((s, SGU_W), F32)] * 2
                  + [jax.ShapeDtypeStruct(w.shape, F32), jax.ShapeDtypeStruct(b_cols.shape, F32),
                     jax.ShapeDtypeStruct(gn.shape, F32)],
                  semantics=("parallel", "arbitrary"))(proj, proj, dmixed, w, w_t, b_cols, gn)


def _pad_lanes(v):
    return jnp.zeros((1, LANES), F32).at[0, :v.shape[0]].set(v)


def _small_views(sm):
    return dict(
        n1=sm["norm1_g"][None, :], n2=sm["norm2_g"][None, :],
        b_pad=_pad_lanes(sm["b_forget"]),
        qg=jnp.tile(sm["q_norm_g"], 2)[None, :], kg=jnp.tile(sm["k_norm_g"], 2)[None, :],
        gn=sm["sgu_norm_g"].reshape(2, 1, LANES),
        w=sm["sgu_w"], w_t=jnp.swapaxes(sm["sgu_w"], 1, 2),
        b_cols=sm["sgu_b"].reshape(2, 2, SGU_CHUNK).transpose(0, 2, 1))


def _cf_rows(cf):
    return cf[:, :FOX_HEADS].T.reshape(FOX_W // LANES, 2, cf.shape[0])


def _layer_fwd(x_in, prev, mod, wts, sm, l):
    sh1, sc1, g1, sh2, sc2, g2 = mod
    v = _small_views(sm)
    if prev is None:
        x0 = x_in
        h1 = _norm_mod_fwd(x0, v["n1"], sc1, sh1, name=f"l{l}_norm1")
    else:
        x0, h1 = _resid_norm_mod_fwd(x_in, prev[0], prev[1], v["n1"], sc1, sh1, name=f"l{l}_norm1")
    proj = _matmul(h1, wts["w_in"], name=f"l{l}_proj")
    o_sb = _sb_fwd(proj, name=f"l{l}_sb_fwd")
    qn, kn, kmax = _fox_prep_fwd(proj, v["qg"], v["kg"], name=f"l{l}_fox_prep")
    cf = _forget_cumsum_fwd(proj, v["b_pad"], name=f"l{l}_cumf")
    cfr = _cf_rows(cf)
    o_fox, lse = _fox_fwd(proj, qn, kn, cf, cfr, kmax, name=f"l{l}_fox_fwd")
    o_sgu = _sgu_fwd(proj, v["w"], v["b_cols"], v["gn"], name=f"l{l}_sgu_fwd")
    mixed = jnp.concatenate([o_sb, o_fox, o_sgu], axis=1).astype(BF16)
    mo = _matmul(mixed, wts["w_out"], name=f"l{l}_wout")
    x1, h2 = _resid_norm_mod_fwd(x0, mo, g1, v["n2"], sc2, sh2, name=f"l{l}_norm2")
    a = _matmul(h2, wts["w1"], name=f"l{l}_mlp1", out_dtype=BF16)
    rr = _relu2_fwd(a, name=f"l{l}_relu2")
    m2 = _matmul(rr, wts["w2"], name=f"l{l}_mlp2")
    saved = dict(x0=x0, h1=h1, proj=proj, qn=qn, kn=kn, kmax=kmax, cf=cf, cfr=cfr, o_fox=o_fox,
                 lse=lse, mixed=mixed, mo=mo, x1=x1, h2=h2, a=a, rr=rr, m2=m2)
    return saved


def _layer_bwd(dx2, sv, mod, wts, sm, l):
    sh1, sc1, g1, sh2, sc2, g2 = mod
    v = _small_views(sm)
    dm2, dg2 = _gate_bwd(dx2, sv["m2"], g2, name=f"l{l}_gate2_bwd")
    dw2 = _matmul(sv["rr"], dm2, ta=True, name=f"l{l}_dw2")
    dr = _matmul(dm2, wts["w2"], tb=True, name=f"l{l}_dr", out_dtype=BF16)
    da = _relu2_bwd(dr, sv["a"], name=f"l{l}_relu2_bwd")
    dw1 = _matmul(sv["h2"], da, ta=True, name=f"l{l}_dw1")
    dh2 = _matmul(da, wts["w1"], tb=True, name=f"l{l}_dh2")
    dx1, dn2, dsc2, dsh2 = _norm_mod_bwd(sv["x1"], dh2, dx2, v["n2"], sc2, name=f"l{l}_norm2_bwd")
    dmo, dg1 = _gate_bwd(dx1, sv["mo"], g1, name=f"l{l}_gate1_bwd")
    dwo = _matmul(sv["mixed"], dmo, ta=True, name=f"l{l}_dwout")
    dmixed = _matmul(dmo, wts["w_out"], tb=True, name=f"l{l}_dmixed")
    proj = sv["proj"]
    dqa, dka, dva = _sb_bwd(proj, dmixed, name=f"l{l}_sb_bwd")
    dqn, dkn, dvb, dcfr, dcfq = _fox_bwd(proj, sv["qn"], sv["kn"], sv["cf"], sv["cfr"], sv["kmax"], dmixed,
                                   sv["o_fox"], sv["lse"], name=f"l{l}_fox_bwd")
    dqb, dkb, dqg, dkg = _fox_prep_bwd(proj, dqn, dkn, v["qg"], v["kg"], name=f"l{l}_fox_prep_bwd")
    s = proj.shape[0]
    dcf_heads = dcfr.reshape(FOX_HEADS, s).T + dcfq.reshape(s, FOX_HEADS, HEAD_DIM)[:, :, 0]
    dcf = jnp.zeros((s, LANES), F32).at[:, :FOX_HEADS].set(dcf_heads)
    dfl, dbf = _forget_cumsum_bwd(proj, v["b_pad"], dcf, name=f"l{l}_cumf_bwd")
    duc, dvc, dsw, dsb_cols, dgn = _sgu_bwd(proj, dmixed, v["w"], v["w_t"], v["b_cols"], v["gn"],
                                            name=f"l{l}_sgu_bwd")
    dproj = jnp.concatenate([dqa, dka, dva, dqb, dkb, dvb, duc, dvc, dfl,
                             jnp.zeros((s, LANES), F32)], axis=1).astype(BF16)
    dwin = _matmul(sv["h1"], dproj, ta=True, name=f"l{l}_dwin")
    dh1 = _matmul(dproj, wts["w_in"], tb=True, name=f"l{l}_dh1")
    dx0, dn1, dsc1, dsh1 = _norm_mod_bwd(sv["x0"], dh1, dx1, v["n1"], sc1, name=f"l{l}_norm1_bwd")
    big = dict(w_in=dwin, w_out=dwo, w1=dw1, w2=dw2)
    small = dict(norm1_g=dn1[0], norm2_g=dn2[0], b_forget=dbf[0, :FOX_HEADS],
                 q_norm_g=dqg[0, :HEAD_DIM] + dqg[0, HEAD_DIM:],
                 k_norm_g=dkg[0, :HEAD_DIM] + dkg[0, HEAD_DIM:],
                 sgu_norm_g=dgn.reshape(4, HEAD_DIM), sgu_w=dsw,
                 sgu_b=dsb_cols.transpose(0, 2, 1).reshape(4, SGU_CHUNK))
    dmod = jnp.concatenate([dsh1, dsc1, dg1, dsh2, dsc2, dg2], axis=1)
    return dx0, big, small, dmod


def _w_in_to_internal(w):
    pad = jnp.zeros((w.shape[0], PROJ_W - IN_W), w.dtype)
    return jnp.concatenate([w[:, :ATT_W], w[:, ATT_W + FOX_HEADS:], w[:, ATT_W:ATT_W + FOX_HEADS],
                            pad], axis=1)


def _w_in_from_internal(g):
    n_gate = SGU_W * 2
    return jnp.concatenate([g[:, :ATT_W], g[:, ATT_W + n_gate:ATT_W + n_gate + FOX_HEADS],
                            g[:, ATT_W:ATT_W + n_gate]], axis=1)


def _exchange(x, masks, slot_shift, slot_bits, scatter, *, name):
    n_slots = 2 ** slot_bits
    blk_shape = x.shape[1:] if scatter else x.shape
    n_peers = len(masks)

    def body(x_ref, out_ref, send_sems, recv_sems, local_sem):
        ids = (lax.axis_index("x"), lax.axis_index("y"), lax.axis_index("c"))
        me = 4 * ids[0] + 2 * ids[1] + ids[2]
        my_slot = (me >> slot_shift) & (n_slots - 1)

        def peer(mask):
            return tuple(1 - v if (mask >> b) & 1 else v for v, b in zip(ids, (2, 1, 0)))

        def src_for(slot):
            return x_ref.at[slot] if scatter else x_ref

        copies = [pltpu.make_async_copy(src_for(my_slot), out_ref.at[my_slot], local_sem)]
        for kk, mask in enumerate(masks):
            peer_slot = ((me ^ mask) >> slot_shift) & (n_slots - 1)
            copies.append(pltpu.make_async_remote_copy(
                src_ref=src_for(peer_slot), dst_ref=out_ref.at[my_slot],
                send_sem=send_sems.at[kk], recv_sem=recv_sems.at[kk],
                device_id=peer(mask), device_id_type=MESH))
        for cp in copies:
            cp.start()
        for cp in copies:
            cp.wait()

    any_spec = pl.BlockSpec(memory_space=pl.ANY)
    return _pcall(body, name=name, in_specs=[any_spec], out_specs=any_spec,
                  out_shape=jax.ShapeDtypeStruct((n_slots,) + tuple(blk_shape), x.dtype),
                  scratch_shapes=[pltpu.SemaphoreType.DMA((n_peers,)),
                                  pltpu.SemaphoreType.DMA((n_peers,)),
                                  pltpu.SemaphoreType.DMA(())])(x)


CORE_PIECE_BYTES = 12 * 2 ** 20
CORE_DMA_CHUNKS = 4


def _core_swap_piece(x, *, name):
    rows, cols = x.shape
    n_ch = CORE_DMA_CHUNKS if rows % (16 * CORE_DMA_CHUNKS) == 0 else 1
    rc = rows // n_ch

    def body(x_ref, out_ref, send_sems, recv_sems):
        sibling = (lax.axis_index("x"), lax.axis_index("y"), 1 - lax.axis_index("c"))
        copies = [pltpu.make_async_remote_copy(
            src_ref=x_ref.at[pl.ds(ch * rc, rc)], dst_ref=out_ref.at[pl.ds(ch * rc, rc)],
            send_sem=send_sems.at[ch], recv_sem=recv_sems.at[ch],
            device_id=sibling, device_id_type=MESH) for ch in range(n_ch)]
        for cp in copies:
            cp.start()
        for cp in copies:
            cp.wait()

    vmem = pl.BlockSpec(memory_space=pltpu.VMEM)
    return _pcall(body, name=name, in_specs=[vmem], out_specs=vmem,
                  out_shape=jax.ShapeDtypeStruct(x.shape, x.dtype),
                  scratch_shapes=[pltpu.SemaphoreType.DMA((n_ch,)),
                                  pltpu.SemaphoreType.DMA((n_ch,))])(x)


def _core_swap(x, *, name):
    rows, cols = x.shape
    n = 1
    while (rows % n or (rows // n) % 16 or
           (rows // n) * (-(-cols // LANES) * LANES) * x.dtype.itemsize > CORE_PIECE_BYTES):
        n += 1
    pr = rows // n
    pieces = [_core_swap_piece(x[kk * pr:(kk + 1) * pr], name=f"{name}_{kk}") for kk in range(n)]
    return pieces[0] if n == 1 else jnp.concatenate(pieces, axis=0)


def _by_core(core, mine, theirs, axis):
    return jnp.where(core == 0, jnp.concatenate([mine, theirs], axis=axis),
                     jnp.concatenate([theirs, mine], axis=axis))


def _gather_chips(x, *, name):
    return _exchange(x, (2, 4, 6), 1, 2, False, name=name)


def _gather_all(x, *, name):
    return _exchange(x, (1, 2, 3, 4, 5, 6, 7), 0, 3, False, name=name)


def _scatter_chips(x4, *, name):
    return _exchange(x4, (2, 4, 6), 1, 2, True, name=name)


def _sum_slots(parts, *, name, out_dtype=F32, tr=256):
    n, rows, cols = parts.shape
    tr = min(tr, rows)
    assert rows % tr == 0, (name, rows, tr)

    def body(p_ref, o_ref):
        acc = p_ref[0].astype(F32)
        for kk in range(1, n):
            acc = acc + p_ref[kk].astype(F32)
        o_ref[...] = acc.astype(o_ref.dtype)

    return _pcall(body, name=name, grid=(rows // tr,),
                  in_specs=[pl.BlockSpec((n, tr, cols), lambda i: (0, i, 0))],
                  out_specs=pl.BlockSpec((tr, cols), lambda i: (i, 0)),
                  out_shape=jax.ShapeDtypeStruct((rows, cols), out_dtype),
                  semantics=("parallel",))(parts)


def _add2(a, b, *, name, out_dtype, tr=512):
    def fn(f, v):
        return [f[0] + f[1]], []
    (out,), _ = _rowwise(fn, [a, b], [], [out_dtype], 0, name=name, tr=tr)
    return out


def _adamw(w, m, v, parts, *, name, tr=256):
    n, rows, cols = parts.shape
    tr = min(tr, rows)
    assert rows % tr == 0, (name, rows, tr)
    c1 = 1.0 - ADAM_B1 ** ADAM_STEP
    c2 = 1.0 - ADAM_B2 ** ADAM_STEP

    def body(w_ref, m_ref, v_ref, p_ref, g_ref, d_ref, nm_ref, nv_ref):
        g = p_ref[0]
        for kk in range(1, n):
            g = g + p_ref[kk]
        nm = ADAM_B1 * m_ref[...] + (1.0 - ADAM_B1) * g
        nv = ADAM_B2 * v_ref[...] + (1.0 - ADAM_B2) * (g * g)
        g_ref[...] = g
        nm_ref[...] = nm
        nv_ref[...] = nv
        d_ref[...] = -ADAM_LR * ((nm / c1) / (jnp.sqrt(nv / c2) + ADAM_EPS) + ADAM_WD * w_ref[...])

    spec = pl.BlockSpec((tr, cols), lambda i: (i, 0))
    return _pcall(body, name=name, grid=(rows // tr,),
                  in_specs=[spec, spec, spec, pl.BlockSpec((n, tr, cols), lambda i: (0, i, 0))],
                  out_specs=[spec] * 4,
                  out_shape=[jax.ShapeDtypeStruct((rows, cols), F32)] * 4,
                  semantics=("parallel",))(w, m, v, parts)


def _silu(c):
    return c / (1.0 + jnp.exp(-c))


def _ada_fwd(c_all, ada_w, ada_b_sh, *, name):
    nl, d, wsh = ada_w.shape

    def body(c_ref, w_ref, b_ref, o_ref):
        cond = _silu(c_ref[...]).astype(BF16)
        o_ref[0] = _dot(cond, w_ref[0].astype(BF16)) + b_ref[0]

    return _pcall(body, name=name, grid=(nl,),
                  in_specs=[pl.BlockSpec(c_all.shape, lambda l: (0, 0)),
                            pl.BlockSpec((1, d, wsh), lambda l: (l, 0, 0)),
                            pl.BlockSpec((1, 1, wsh), lambda l: (l, 0, 0))],
                  out_specs=pl.BlockSpec((1, c_all.shape[0], wsh), lambda l: (l, 0, 0)),
                  out_shape=jax.ShapeDtypeStruct((nl, c_all.shape[0], wsh), F32),
                  semantics=("parallel",))(c_all, ada_w, ada_b_sh)


def _ada_bwd(c_all, dmod_sh, *, name):
    nl, nb, wsh = dmod_sh.shape
    d = c_all.shape[1]

    def body(c_ref, dm_ref, o_ref):
        cond = _silu(c_ref[...]).astype(BF16)
        o_ref[0] = _dot_tn(cond, dm_ref[0].astype(BF16))

    return _pcall(body, name=name, grid=(nl,),
                  in_specs=[pl.BlockSpec(c_all.shape, lambda l: (0, 0)),
                            pl.BlockSpec((1, nb, wsh), lambda l: (l, 0, 0))],
                  out_specs=pl.BlockSpec((1, d, wsh), lambda l: (l, 0, 0)),
                  out_shape=jax.ShapeDtypeStruct((nl, d, wsh), F32),
                  semantics=("parallel",))(c_all, dmod_sh)


SMALL_NAMES = ("norm1_g", "norm2_g", "b_forget", "q_norm_g", "k_norm_g", "sgu_norm_g", "sgu_w",
               "sgu_b")
WEIGHT_NAMES = ("ada_w", "ada_b", "norm1_g", "norm2_g", "w_in", "b_forget", "q_norm_g", "k_norm_g",
                "sgu_norm_g", "sgu_w", "sgu_b", "w_out", "mlp_w1", "mlp_w2")


SMALL_TILE_ROWS = 256


def _pack_small(tree):
    flat = jnp.concatenate([tree[n].reshape(-1) for n in SMALL_NAMES])
    n = flat.shape[0]
    rows = -(-n // (SMALL_TILE_ROWS * LANES)) * SMALL_TILE_ROWS
    return jnp.zeros((rows * LANES,), F32).at[:n].set(flat).reshape(rows, LANES)


def _unpack_small(packed, like):
    flat = packed.reshape(-1)
    out, off = {}, 0
    for n in SMALL_NAMES:
        size = like[n].size
        out[n] = flat[off:off + size].reshape(like[n].shape)
        off += size
    return out


def kernel(x, c, ada_w, ada_b, norm1_g, norm2_g, w_in, b_forget, q_norm_g, k_norm_g, sgu_norm_g, sgu_w, sgu_b, w_out, mlp_w1, mlp_w2, loss_target, m_ada_w, m_ada_b, m_norm1_g, m_norm2_g, m_w_in, m_b_forget, m_q_norm_g, m_k_norm_g, m_sgu_norm_g, m_sgu_w, m_sgu_b, m_w_out, m_mlp_w1, m_mlp_w2, v_ada_w, v_ada_b, v_norm1_g, v_norm2_g, v_w_in, v_b_forget, v_q_norm_g, v_k_norm_g, v_sgu_norm_g, v_sgu_w, v_sgu_b, v_w_out, v_mlp_w1, v_mlp_w2):
    w = dict(ada_w=ada_w, ada_b=ada_b, norm1_g=norm1_g, norm2_g=norm2_g, w_in=w_in,
             b_forget=b_forget, q_norm_g=q_norm_g, k_norm_g=k_norm_g, sgu_norm_g=sgu_norm_g,
             sgu_w=sgu_w, sgu_b=sgu_b, w_out=w_out, mlp_w1=mlp_w1, mlp_w2=mlp_w2)
    mom = dict(ada_w=m_ada_w, ada_b=m_ada_b, norm1_g=m_norm1_g, norm2_g=m_norm2_g, w_in=m_w_in,
               b_forget=m_b_forget, q_norm_g=m_q_norm_g, k_norm_g=m_k_norm_g,
               sgu_norm_g=m_sgu_norm_g, sgu_w=m_sgu_w, sgu_b=m_sgu_b, w_out=m_w_out,
               mlp_w1=m_mlp_w1, mlp_w2=m_mlp_w2)
    var = dict(ada_w=v_ada_w, ada_b=v_ada_b, norm1_g=v_norm1_g, norm2_g=v_norm2_g, w_in=v_w_in,
               b_forget=v_b_forget, q_norm_g=v_q_norm_g, k_norm_g=v_k_norm_g,
               sgu_norm_g=v_sgu_norm_g, sgu_w=v_sgu_w, sgu_b=v_sgu_b, w_out=v_w_out,
               mlp_w1=v_mlp_w1, mlp_w2=v_mlp_w2)
    depth, d = norm1_g.shape
    chip = 2 * lax.axis_index("x") + lax.axis_index("y")
    me = 2 * chip + lax.axis_index("c")
    n_chips = 4
    ada_sh = ada_w.shape[2]

    core = lax.axis_index("c")
    half_l = depth // 2

    def gather_weight(w_sh, name):
        _, r, cols = w_sh.shape
        mine = lax.dynamic_slice_in_dim(w_sh, core * half_l, half_l, axis=0).astype(BF16)
        got = _gather_chips(mine.reshape(half_l * r, cols), name=f"gather_{name}")
        theirs = _core_swap(got.reshape(n_chips * half_l * r, cols), name=f"share_{name}")
        return _by_core(core, got.reshape(n_chips, half_l, r, cols),
                        theirs.reshape(n_chips, half_l, r, cols), 1)

    g_in = gather_weight(w_in, "w_in")
    g_out = gather_weight(w_out, "w_out")
    g_w1 = gather_weight(mlp_w1, "w1")
    g_w2 = gather_weight(mlp_w2, "w2")
    layer_w = []
    for l in range(depth):
        layer_w.append(dict(
            w_in=_w_in_to_internal(jnp.concatenate([g_in[k, l] for k in range(n_chips)], axis=1)),
            w_out=g_out[:, l].reshape(d, d),
            w1=jnp.concatenate([g_w1[k, l] for k in range(n_chips)], axis=1),
            w2=g_w2[:, l].reshape(D_FF, d)))

    c_all = _gather_all(jnp.zeros((8, d), F32).at[0].set(c[0]), name="gather_c")[:, 0]
    c_pad = jnp.concatenate([c_all, jnp.zeros_like(c_all)], axis=0)
    ada_b_sh = lax.dynamic_slice_in_dim(ada_b, chip * ada_sh, ada_sh, axis=1)[:, None, :]
    mod_sh = _ada_fwd(c_pad, ada_w, ada_b_sh, name="ada_fwd")
    mod_all = _gather_chips(mod_sh, name="gather_mod")
    mod_me = lax.dynamic_index_in_dim(mod_all, me, axis=2, keepdims=False)
    mod_me = mod_me.transpose(1, 0, 2).reshape(depth, 6, 1, d)

    saved = []
    xs, prev = x[0], None
    for l in range(depth):
        mod = [mod_me[l, kk] for kk in range(6)]
        sm = {n: w[n][l] for n in SMALL_NAMES}
        sv = _layer_fwd(xs, prev, mod, layer_w[l], sm, l)
        saved.append(sv)
        xs, prev = sv["x1"], (sv["m2"], mod[5])

    sq, dxs = _loss_fwd_bwd(xs, prev[0], prev[1], loss_target[0], name="loss")
    loss = lax.psum(0.5 * jnp.sum(sq) / d, ("x", "y", "c"))

    big = {n: [] for n in ("w_in", "w_out", "w1", "w2")}
    small = {n: [] for n in SMALL_NAMES}
    dmods = []
    for l in reversed(range(depth)):
        mod = [mod_me[l, kk] for kk in range(6)]
        sm = {n: w[n][l] for n in SMALL_NAMES}
        dxs, bg, smg, dmod = _layer_bwd(dxs, saved[l], mod, layer_w[l], sm, l)
        for n in big:
            big[n].insert(0, bg[n])
        for n in SMALL_NAMES:
            small[n].insert(0, smg[n])
        dmods.insert(0, dmod)
    grad_x = dxs[None]

    out_g, out_d, out_m, out_v = {}, {}, {}, {}

    def run_adamw(name, parts2d, shape):
        rows, cols = parts2d.shape[1:]
        g, dl, nm, nv = _adamw(w[name].reshape(rows, cols), mom[name].reshape(rows, cols),
                               var[name].reshape(rows, cols), parts2d, name=f"adamw_{name}")
        out_g[name], out_d[name] = g.reshape(shape), dl.reshape(shape)
        out_m[name], out_v[name] = nm.reshape(shape), nv.reshape(shape)

    def shards_of(name, l):
        if name == "w_in":
            g = _w_in_from_internal(big["w_in"][l])
            return jnp.stack(jnp.split(g, n_chips, axis=1))
        if name == "mlp_w1":
            return jnp.stack(jnp.split(big["w1"][l], n_chips, axis=1))
        if name == "w_out":
            return big["w_out"][l].reshape(n_chips, d // n_chips, d)
        return big["w2"][l].reshape(n_chips, D_FF // n_chips, d)

    for name in ("w_in", "w_out", "mlp_w1", "mlp_w2"):
        per_chip = jnp.stack([shards_of(name, l) for l in range(depth)], axis=1)
        r, cols = per_chip.shape[2:]
        half_rows = half_l * r
        keep = lax.dynamic_slice_in_dim(per_chip, core * half_l, half_l, axis=1)
        send = lax.dynamic_slice_in_dim(per_chip, (1 - core) * half_l, half_l, axis=1)
        theirs = _core_swap(send.reshape(n_chips * half_rows, cols), name=f"pair_{name}")
        chip_sum = _add2(keep.reshape(n_chips * half_rows, cols), theirs, out_dtype=BF16,
                         name=f"pairsum_{name}")
        got = _scatter_chips(chip_sum.reshape(n_chips, half_rows, cols), name=f"scatter_{name}")
        half = _sum_slots(got, name=f"sum_{name}")
        both = _by_core(core, half, _core_swap(half, name=f"swap_{name}"), 0)
        run_adamw(name, both[None], w[name].shape)

    small_tree = {n: jnp.stack(small[n]) for n in SMALL_NAMES}
    gathered = _gather_all(_pack_small(small_tree), name="gather_small")
    gs, ds_, ms, vs = _adamw(_pack_small({n: w[n] for n in SMALL_NAMES}),
                             _pack_small({n: mom[n] for n in SMALL_NAMES}),
                             _pack_small({n: var[n] for n in SMALL_NAMES}), gathered,
                             name="adamw_small")
    like = {n: w[n] for n in SMALL_NAMES}
    for tree, packed in ((out_g, gs), (out_d, ds_), (out_m, ms), (out_v, vs)):
        tree.update(_unpack_small(packed, like))

    dmod_mine = jnp.concatenate(dmods, axis=0)
    dmod_all = _gather_all(jnp.zeros((depth, 8, 6 * d), F32).at[:, 0].set(dmod_mine),
                           name="gather_dmod")[:, :, 0]
    dmod_lb = dmod_all.transpose(1, 0, 2)
    dmod_sh = lax.dynamic_slice_in_dim(dmod_lb, chip * ada_sh, ada_sh, axis=2)
    dmod_sh = jnp.concatenate([dmod_sh, jnp.zeros_like(dmod_sh)], axis=1)
    g_ada_w = _ada_bwd(c_pad, dmod_sh, name="ada_bwd")
    run_adamw("ada_w", g_ada_w.reshape(1, depth * d, ada_sh), ada_w.shape)
    parts_b = dmod_all.reshape(8, depth * 6 * d // LANES, LANES)
    run_adamw("ada_b", parts_b, ada_b.shape)

    outs = [loss, grad_x]
    for tree in (out_g, out_d, out_m, out_v):
        outs += [tree[n] for n in WEIGHT_NAMES]
    return tuple(outs)
```

```python
import functools
import math

import jax
import jax.numpy as jnp
from jax import lax
from jax.experimental import pallas as pl
from jax.experimental.pallas import tpu as pltpu

F32 = jnp.float32
BF16 = jnp.bfloat16

D_MODEL = 1024
DEPTH = 4
HEAD_DIM = 64
LANES = 128
D_FF = 4 * D_MODEL
EPS = 1e-6
SB_W, FOX_W, SGU_W = 256, 512, 256
FOX_HEADS = 8
SGU_CHUNK = 128
IN_W = 2824
ATT_W = 3 * SB_W + 3 * FOX_W
PROJ_W = 3072
CB_QA, CB_KA, CB_VA = 0, 2, 4
CB_QB, CB_KB, CB_VB = 6, 10, 14
CB_UC, CB_VC, CB_FL = 18, 20, 22
ATT_T = 256
VMEM_LIMIT = 56 * 2 ** 20
SKIP_LOG = 110.0

ADAM_LR, ADAM_B1, ADAM_B2, ADAM_EPS, ADAM_WD, ADAM_STEP = 0.001, 0.9, 0.999, 1e-08, 0.01, 10

MESH = pl.DeviceIdType.MESH


def _pcall(body, *, name, out_shape, grid=(), in_specs=None, out_specs=None, scratch_shapes=(),
           semantics=None):
    params = dict(vmem_limit_bytes=VMEM_LIMIT)
    if semantics is not None:
        params["dimension_semantics"] = semantics
    kwargs = {}
    if in_specs is not None:
        kwargs["in_specs"] = in_specs
    if out_specs is not None:
        kwargs["out_specs"] = out_specs
    return pl.pallas_call(body, name=name, out_shape=out_shape, grid=grid,
                          scratch_shapes=list(scratch_shapes),
                          compiler_params=pltpu.CompilerParams(**params), **kwargs)


def _dot(a, b):
    return jnp.dot(a, b, preferred_element_type=F32)


def _dot_nt(a, b):
    return lax.dot_general(a, b, (((1,), (1,)), ((), ())), preferred_element_type=F32)


def _dot_tn(a, b):
    return lax.dot_general(a, b, (((0,), (0,)), ((), ())), preferred_element_type=F32)


def _split2(x):
    hi = x.astype(BF16)
    lo = (x - hi.astype(F32)).astype(BF16)
    return hi, lo


def _ones_dot(x, ones_bf16):
    hi, lo = _split2(x)
    return _dot(hi, ones_bf16) + _dot(lo, ones_bf16)


def _rowwise(fn, fulls, vecs, out_dtypes, n_vec_out, *, name, tr):
    s, n = fulls[0].shape
    tr = min(tr, s)
    assert s % tr == 0, (name, s, tr)
    nf, nv, nfo = len(fulls), len(vecs), len(out_dtypes)

    def body(*refs):
        fi, vi = refs[:nf], refs[nf:nf + nv]
        fo, vo = refs[nf + nv:nf + nv + nfo], refs[nf + nv + nfo:]
        outs_f, outs_v = fn([r[...] for r in fi], [r[...] for r in vi])
        for r, o in zip(fo, outs_f):
            r[...] = o.astype(r.dtype)
        if n_vec_out:
            @pl.when(pl.program_id(0) == 0)
            def _():
                for r in vo:
                    r[...] = jnp.zeros_like(r)
            for r, o in zip(vo, outs_v):
                r[...] += o

    full_spec = pl.BlockSpec((tr, n), lambda i: (i, 0))
    vec_specs = [pl.BlockSpec(v.shape, lambda i: (0, 0)) for v in vecs]
    out_vec_spec = pl.BlockSpec((1, n), lambda i: (0, 0))
    out_shape = [jax.ShapeDtypeStruct((s, n), dt) for dt in out_dtypes]
    out_shape += [jax.ShapeDtypeStruct((1, n), F32)] * n_vec_out
    outs = _pcall(body, name=name, grid=(s // tr,),
                  in_specs=[full_spec] * nf + vec_specs,
                  out_specs=[full_spec] * nfo + [out_vec_spec] * n_vec_out,
                  out_shape=out_shape,
                  semantics=("arbitrary",) if n_vec_out else ("parallel",))(*fulls, *vecs)
    return outs[:nfo], outs[nfo:]


def _colsum(t):
    return jnp.sum(t, axis=0, keepdims=True)


def _rms_mod(x, g, sc, sh):
    r = lax.rsqrt(jnp.mean(x * x, axis=-1, keepdims=True) + EPS)
    return (x * r * g) * (1.0 + sc) + sh


def _norm_mod_fwd(x, g, sc, sh, *, name):
    def fn(f, v):
        return [_rms_mod(f[0], v[0], v[1], v[2])], []
    (h,), _ = _rowwise(fn, [x], [g, sc, sh], [BF16], 0, name=name, tr=512)
    return h


def _resid_norm_mod_fwd(x, m, gate, g, sc, sh, *, name):
    def fn(f, v):
        xn = f[0] + v[0] * f[1]
        return [xn, _rms_mod(xn, v[1], v[2], v[3])], []
    (xn, h), _ = _rowwise(fn, [x, m], [gate, g, sc, sh], [F32, BF16], 0, name=name, tr=512)
    return xn, h


def _norm_mod_bwd(x, dh, dres, g, sc, gated, *, name):
    def fn(f, v):
        xv, dhv, dr = f[:3]
        gv, scv = v[:2]
        r = lax.rsqrt(jnp.mean(xv * xv, axis=-1, keepdims=True) + EPS)
        xh = xv * r
        dn = dhv * (1.0 + scv)
        dxh = dn * gv
        dx = dr + r * (dxh - xh * jnp.mean(dxh * xh, axis=-1, keepdims=True))
        sums = [_colsum(dn * xh), _colsum(dhv * (xh * gv)), _colsum(dhv)]
        if gated is None:
            return [dx], sums
        return [dx, dx * v[2]], sums + [_colsum(dx * f[3])]
    if gated is None:
        (dx,), (dg, dsc, dsh) = _rowwise(fn, [x, dh, dres], [g, sc], [F32], 3, name=name, tr=256)
        return dx, dg, dsc, dsh, None, None
    (dx, dm), (dg, dsc, dsh, dgate) = _rowwise(fn, [x, dh, dres, gated[0]], [g, sc, gated[1]],
                                               [F32, BF16], 4, name=name, tr=256)
    return dx, dg, dsc, dsh, dm, dgate


def _loss_fwd_bwd(x, m, gate, target, *, name):
    n = x.shape[1]

    def fn(f, v):
        err = f[0] + v[0] * f[1] - f[2]
        dy = err * (1.0 / n)
        return [dy, dy * v[0]], [_colsum(err * err), _colsum(dy * f[1])]
    (dy, dm), (sq, dgate) = _rowwise(fn, [x, m, target], [gate], [F32, BF16], 2, name=name, tr=512)
    return sq, dy, dm, dgate


def _matmul(a, b, *, name, ta=False, tb=False, out_dtype=F32, relu2=None, pre_act=None,
            tm=1024, tn=1024, tk=1024):
    m = a.shape[1] if ta else a.shape[0]
    k = a.shape[0] if ta else a.shape[1]
    n = b.shape[0] if tb else b.shape[1]
    assert k == (b.shape[1] if tb else b.shape[0])
    tm, tn, tk = min(tm, m), min(tn, n), min(tk, k)
    assert m % tm == 0 and n % tn == 0 and k % tk == 0, (name, m, n, k)
    nk = k // tk
    dims = (((0 if ta else 1,), (1 if tb else 0,)), ((), ()))

    def body(*refs):
        a_ref, b_ref = refs[:2]
        acc_ref = refs[-1]
        kk = pl.program_id(2)

        @pl.when(kk == 0)
        def _():
            acc_ref[...] = jnp.zeros_like(acc_ref)
        acc_ref[...] += lax.dot_general(a_ref[...].astype(BF16), b_ref[...].astype(BF16), dims,
                                        preferred_element_type=F32)

        @pl.when(kk == nk - 1)
        def _():
            acc = acc_ref[...]
            if pre_act is not None:
                acc = acc * (2.0 * jnp.maximum(refs[2][...].astype(F32), 0.0))
            o_ref = refs[-2 - (relu2 is not None)]
            o_ref[...] = acc.astype(o_ref.dtype)
            if relu2 is not None:
                r = jnp.maximum(acc, 0.0)
                refs[-2][...] = (r * r).astype(relu2)

    a_spec = (pl.BlockSpec((tk, tm), lambda i, j, kk: (kk, i)) if ta
              else pl.BlockSpec((tm, tk), lambda i, j, kk: (i, kk)))
    b_spec = (pl.BlockSpec((tn, tk), lambda i, j, kk: (j, kk)) if tb
              else pl.BlockSpec((tk, tn), lambda i, j, kk: (kk, j)))
    out_spec = pl.BlockSpec((tm, tn), lambda i, j, kk: (i, j))
    in_specs, args = [a_spec, b_spec], [a, b]
    if pre_act is not None:
        in_specs.append(out_spec)
        args.append(pre_act)
    out_specs, out_shape = out_spec, jax.ShapeDtypeStruct((m, n), out_dtype)
    if relu2 is not None:
        out_specs, out_shape = [out_spec] * 2, [out_shape, jax.ShapeDtypeStruct((m, n), relu2)]
    return _pcall(body, name=name, grid=(m // tm, n // tn, nk),
                  in_specs=in_specs, out_specs=out_specs, out_shape=out_shape,
                  scratch_shapes=[pltpu.VMEM((tm, tn), F32)],
                  semantics=("parallel", "parallel", "arbitrary"))(*args)


def _lane_masks():
    lane = lax.broadcasted_iota(jnp.int32, (1, LANES), 1)
    return [lane < HEAD_DIM, lane >= HEAD_DIM]


def _tri_iotas(t):
    r = lax.broadcasted_iota(jnp.int32, (t, t), 0)
    c = lax.broadcasted_iota(jnp.int32, (t, t), 1)
    return r, c


def _rows(j, t):
    return pl.ds(pl.multiple_of(j * t, t), t)


def _neg_softplus(z):
    e = jnp.exp(-jnp.abs(z))
    return -(jnp.maximum(z, 0.0) + jnp.log(1.0 + e)), e


def _sb_fwd(proj, *, name):
    s = proj.shape[0]
    t = min(ATT_T, s)
    scale = HEAD_DIM ** -0.5

    def body(q_ref, k_ref, v_ref, o_ref):
        i = pl.program_id(1)
        hm = _lane_masks()
        q = q_ref[...] * scale
        qh = [jnp.where(mk, q, 0.0).astype(BF16) for mk in hm]
        r, c = _tri_iotas(t)
        later = (r > c).astype(BF16)
        q2 = jnp.concatenate(qh, axis=0)
        causal2 = jnp.concatenate([c < r, c < r], axis=0)

        def scores(j):
            return _dot_nt(q2, k_ref[_rows(j, t), :].astype(BF16))

        def chunk(j, carry, z, masked):
            e_run, acc = carry
            vb = v_ref[_rows(j, t), :].astype(BF16)
            l, _ = _neg_softplus(z)
            if masked:
                l = jnp.where(causal2, l, 0.0)
            between = _ones_dot(l, later) + e_run
            a = jnp.exp(z + l + between)
            if masked:
                a = jnp.where(causal2, a, 0.0)
            return e_run + jnp.sum(l, axis=1, keepdims=True), acc + _dot(a.astype(BF16), vb)

        init = (jnp.zeros((2 * t, 1), F32), jnp.zeros((2 * t, LANES), F32))
        carry = chunk(i, init, scores(i), True)

        def step(st):
            j, cr, z = st
            z_next = scores(jnp.maximum(j - 1, 0))
            return j - 1, chunk(j, cr, z, False), z_next

        _, (_, acc), _ = lax.while_loop(
            lambda st: (st[0] >= 0) & (jnp.max(st[1][0]) > -SKIP_LOG), step,
            (i - 1, carry, scores(jnp.maximum(i - 1, 0))))
        o_ref[...] = jnp.where(hm[0], acc[:t], acc[t:])

    blk = lambda cb: pl.BlockSpec((t, LANES), lambda p, i: (i, cb + p))
    full = lambda cb: pl.BlockSpec((s, LANES), lambda p, i: (0, cb + p))
    out_blk = pl.BlockSpec((t, LANES), lambda p, i: (i, p))
    return _pcall(body, name=name, grid=(SB_W // LANES, s // t),
                  in_specs=[blk(CB_QA), full(CB_KA), full(CB_VA)],
                  out_specs=out_blk,
                  out_shape=jax.ShapeDtypeStruct((s, SB_W), F32),
                  semantics=("parallel", "arbitrary"))(proj, proj, proj)


def _sb_bwd(proj, dmixed, *, name):
    s = proj.shape[0]
    t = min(ATT_T, s)
    scale = HEAD_DIM ** -0.5

    def body(q_ref, k_ref, v_ref, do_ref, dq_ref, dk_ref, dv_ref):
        i = pl.program_id(1)

        @pl.when(i == 0)
        def _():
            dk_ref[...] = jnp.zeros_like(dk_ref)
            dv_ref[...] = jnp.zeros_like(dv_ref)

        hm = _lane_masks()
        q = q_ref[...] * scale
        do = do_ref[...]
        qh = [jnp.where(mk, q, 0.0).astype(BF16) for mk in hm]
        doh = [jnp.where(mk, do, 0.0).astype(BF16) for mk in hm]
        r, c = _tri_iotas(t)
        upto = (r <= c).astype(BF16)
        before = (r < c).astype(BF16)
        q2 = jnp.concatenate(qh, axis=0)
        do2 = jnp.concatenate(doh, axis=0)
        causal2 = jnp.concatenate([c < r, c < r], axis=0)

        def scores(j):
            return _dot_nt(q2, k_ref[_rows(j, t), :].astype(BF16))

        def totals(j, e_run, masked):
            l, _ = _neg_softplus(scores(j))
            if masked:
                l = jnp.where(causal2, l, 0.0)
            return e_run + jnp.sum(l, axis=1, keepdims=True)

        j_stop, lt = lax.while_loop(
            lambda st: (st[0] >= 0) & (jnp.max(st[1]) > -SKIP_LOG),
            lambda st: (st[0] - 1, totals(st[0], st[1], False)),
            (i - 1, totals(i, jnp.zeros((2 * t, 1), F32), True)))

        def products(j):
            return scores(j), _dot_nt(do2, v_ref[_rows(j, t), :].astype(BF16))

        def chunk(j, carry, z, da, masked):
            l_run, g_run, dq = carry
            l, e = _neg_softplus(z)
            beta = jnp.where(z >= 0.0, 1.0, e) / (1.0 + e)
            if masked:
                l = jnp.where(causal2, l, 0.0)
            prefix = _ones_dot(l, upto) + l_run
            a = jnp.exp(z + l + (lt - prefix))
            if masked:
                a = jnp.where(causal2, a, 0.0)
            g = a * da
            g_before = _ones_dot(g, before) + g_run
            dz = g * (1.0 - beta) - beta * g_before
            if masked:
                dz = jnp.where(causal2, dz, 0.0)
            dzb = dz.astype(BF16)
            dk_ref[_rows(j, t), :] += _dot_tn(dzb, q2)
            dv_ref[_rows(j, t), :] += _dot_tn(a.astype(BF16), do2)
            return (l_run + jnp.sum(l, axis=1, keepdims=True),
                    g_run + jnp.sum(g, axis=1, keepdims=True),
                    dq + _dot(dzb, k_ref[_rows(j, t), :].astype(BF16)))

        init = (jnp.zeros((2 * t, 1), F32), jnp.zeros((2 * t, 1), F32),
                jnp.zeros((2 * t, LANES), F32))
        carry = lax.fori_loop(j_stop + 1, i,
                              lambda j, cr: chunk(j, cr, *products(j), False), init)
        dq2 = chunk(i, carry, *products(i), True)[2]
        dq_ref[...] = jnp.where(hm[0], dq2[:t], dq2[t:]) * scale

    blk = lambda cb: pl.BlockSpec((t, LANES), lambda p, i: (i, cb + p))
    full = lambda cb: pl.BlockSpec((s, LANES), lambda p, i: (0, cb + p))
    out_blk = pl.BlockSpec((t, LANES), lambda p, i: (i, p))
    out_full = pl.BlockSpec((s, LANES), lambda p, i: (0, p))
    return _pcall(body, name=name, grid=(SB_W // LANES, s // t),
                  in_specs=[blk(CB_QA), full(CB_KA), full(CB_VA), blk(0)],
                  out_specs=[out_blk, out_full, out_full],
                  out_shape=[jax.ShapeDtypeStruct((s, SB_W), F32)] * 3,
                  semantics=("parallel", "arbitrary"))(proj, proj, proj, dmixed)


def _group_mean(v, lo):
    s0 = jnp.sum(jnp.where(lo, v, 0.0), axis=1, keepdims=True)
    s1 = jnp.sum(jnp.where(lo, 0.0, v), axis=1, keepdims=True)
    return jnp.where(lo, s0, s1) * (1.0 / HEAD_DIM)


def _fox_prep_fwd(proj, qg, kg, *, name):
    s = proj.shape[0]
    tr = min(512, s)

    def body(q_ref, k_ref, qg_ref, kg_ref, qn_ref, kn_ref, kmax_ref):
        lo = _lane_masks()[0]
        for x_ref, g_ref, o_ref in ((q_ref, qg_ref, qn_ref), (k_ref, kg_ref, kn_ref)):
            x = x_ref[...]
            o_ref[...] = x * lax.rsqrt(_group_mean(x * x, lo) + EPS) * g_ref[...]

        @pl.when(pl.program_id(1) == 0)
        def _():
            kmax_ref[...] = jnp.zeros_like(kmax_ref)
        kn = kn_ref[...]
        norms = jnp.sqrt(_group_mean(kn * kn, lo) * HEAD_DIM)
        kmax_ref[...] = jnp.maximum(kmax_ref[...], jnp.max(norms, axis=0, keepdims=True))

    blk = lambda cb: pl.BlockSpec((tr, LANES), lambda p, i: (i, cb + p))
    vec = pl.BlockSpec((1, LANES), lambda p, i: (0, 0))
    out_blk = pl.BlockSpec((tr, LANES), lambda p, i: (i, p))
    return _pcall(body, name=name, grid=(FOX_W // LANES, s // tr),
                  in_specs=[blk(CB_QB), blk(CB_KB), vec, vec],
                  out_specs=[out_blk, out_blk, pl.BlockSpec((1, LANES), lambda p, i: (0, p))],
                  out_shape=[jax.ShapeDtypeStruct((s, FOX_W), F32)] * 2
                  + [jax.ShapeDtypeStruct((1, FOX_W), F32)],
                  semantics=("parallel", "arbitrary"))(proj, proj, qg, kg)


def _fox_prep_bwd(proj, dqn, dkn, qg, kg, *, name):
    s = proj.shape[0]
    tr = min(512, s)

    def body(q_ref, k_ref, dqn_ref, dkn_ref, qg_ref, kg_ref, dq_ref, dk_ref, dqg_ref, dkg_ref):
        @pl.when((pl.program_id(0) == 0) & (pl.program_id(1) == 0))
        def _():
            dqg_ref[...] = jnp.zeros_like(dqg_ref)
            dkg_ref[...] = jnp.zeros_like(dkg_ref)

        lo = _lane_masks()[0]
        for x_ref, dy_ref, g_ref, dx_ref, dg_ref in ((q_ref, dqn_ref, qg_ref, dq_ref, dqg_ref),
                                                     (k_ref, dkn_ref, kg_ref, dk_ref, dkg_ref)):
            x, dy = x_ref[...], dy_ref[...]
            r = lax.rsqrt(_group_mean(x * x, lo) + EPS)
            xh = x * r
            dxh = dy * g_ref[...]
            dx_ref[...] = r * (dxh - xh * _group_mean(dxh * xh, lo))
            dg_ref[...] += _colsum(dy * xh)

    blk = lambda cb: pl.BlockSpec((tr, LANES), lambda p, i: (i, cb + p))
    vec = pl.BlockSpec((1, LANES), lambda p, i: (0, 0))
    out_blk = pl.BlockSpec((tr, LANES), lambda p, i: (i, p))
    return _pcall(body, name=name, grid=(FOX_W // LANES, s // tr),
                  in_specs=[blk(CB_QB), blk(CB_KB), out_blk, out_blk, vec, vec],
                  out_specs=[out_blk, out_blk, vec, vec],
                  out_shape=[jax.ShapeDtypeStruct((s, FOX_W), F32)] * 2
                  + [jax.ShapeDtypeStruct((1, LANES), F32)] * 2,
                  semantics=("arbitrary", "arbitrary"))(proj, proj, dqn, dkn, qg, kg)


def _split3_dot(tri_bf16, x):
    hi = x.astype(BF16)
    r1 = x - hi.astype(F32)
    mid = r1.astype(BF16)
    lo = (r1 - mid.astype(F32)).astype(BF16)
    return _dot(tri_bf16, hi) + _dot(tri_bf16, mid) + _dot(tri_bf16, lo)


def _forget_cumsum_fwd(proj, b_pad, *, name):
    s = proj.shape[0]
    tb = min(256, s)

    def body(fl_ref, b_ref, cf_ref, run_ref):
        @pl.when(pl.program_id(0) == 0)
        def _():
            run_ref[...] = jnp.zeros_like(run_ref)
        lf, _ = _neg_softplus(-(fl_ref[...] + b_ref[...]))
        r, c = _tri_iotas(tb)
        incl = _split3_dot((c <= r).astype(BF16), lf) + run_ref[...]
        cf_ref[...] = incl
        run_ref[...] = incl[tb - 1:tb, :]

    return _pcall(body, name=name, grid=(s // tb,),
                  in_specs=[pl.BlockSpec((tb, LANES), lambda i: (i, CB_FL)),
                            pl.BlockSpec((1, LANES), lambda i: (0, 0))],
                  out_specs=pl.BlockSpec((tb, LANES), lambda i: (i, 0)),
                  out_shape=jax.ShapeDtypeStruct((s, LANES), F32),
                  scratch_shapes=[pltpu.VMEM((1, LANES), F32)],
                  semantics=("arbitrary",))(proj, b_pad)


def _forget_cumsum_bwd(proj, b_pad, dcf, *, name):
    s = proj.shape[0]
    tb = min(256, s)
    nb = s // tb

    def body(fl_ref, b_ref, dcf_ref, dfl_ref, db_ref, run_ref):
        @pl.when(pl.program_id(0) == 0)
        def _():
            run_ref[...] = jnp.zeros_like(run_ref)
            db_ref[...] = jnp.zeros_like(db_ref)
        r, c = _tri_iotas(tb)
        dlf = _split3_dot((c >= r).astype(BF16), dcf_ref[...]) + run_ref[...]
        run_ref[...] = dlf[0:1, :]
        xv = fl_ref[...] + b_ref[...]
        e = jnp.exp(-jnp.abs(xv))
        sig_neg = jnp.where(xv >= 0.0, e, 1.0) / (1.0 + e)
        dfl = dlf * sig_neg
        dfl_ref[...] = dfl
        db_ref[...] += _colsum(dfl)

    return _pcall(body, name=name, grid=(nb,),
                  in_specs=[pl.BlockSpec((tb, LANES), lambda i: (nb - 1 - i, CB_FL)),
                            pl.BlockSpec((1, LANES), lambda i: (0, 0)),
                            pl.BlockSpec((tb, LANES), lambda i: (nb - 1 - i, 0))],
                  out_specs=[pl.BlockSpec((tb, LANES), lambda i: (nb - 1 - i, 0)),
                             pl.BlockSpec((1, LANES), lambda i: (0, 0))],
                  out_shape=[jax.ShapeDtypeStruct((s, LANES), F32),
                             jax.ShapeDtypeStruct((1, LANES), F32)],
                  scratch_shapes=[pltpu.VMEM((1, LANES), F32)],
                  semantics=("arbitrary",))(proj, b_pad, dcf)


def _fox_bias_q(cfc, p, h):
    lane = lax.broadcasted_iota(jnp.int32, (1, LANES), 1)
    return jnp.sum(jnp.where(lane == 2 * p + h, cfc, 0.0), axis=1, keepdims=True)


def _fox_score_bound(q, kmax_row, hm):
    out = []
    for h in range(2):
        qnorm = jnp.sqrt(jnp.sum(jnp.where(hm[h], q * q, 0.0), axis=1, keepdims=True))
        out.append(1.02 * qnorm * kmax_row[:, h * HEAD_DIM:h * HEAD_DIM + 1])
    return out


def _fox_live(cfr_ref, j, t, tops):
    jc = jnp.maximum(j, 0)
    worst = []
    for h in range(2):
        cf_min = jnp.min(cfr_ref[0, pl.ds(h, 1), _rows(jc, t)], axis=1, keepdims=True)
        worst.append(jnp.max(tops[h] - cf_min))
    return (j >= 0) & (jnp.maximum(worst[0], worst[1]) > -SKIP_LOG)


def _fox_fwd(proj, qn, kn, cf, cf_rows, kmax, *, name):
    s = proj.shape[0]
    t = min(ATT_T, s)
    scale = HEAD_DIM ** -0.5

    def body(q_ref, k_ref, v_ref, cfc_ref, cfr_ref, kmax_ref, o_ref, lse_ref):
        p, i = pl.program_id(0), pl.program_id(1)
        hm = _lane_masks()
        q = q_ref[...] * scale
        qh = [jnp.where(mk, q, 0.0).astype(BF16) for mk in hm]
        cfc = cfc_ref[...]
        bq = [_fox_bias_q(cfc, p, h) for h in range(2)]
        qk_top = _fox_score_bound(q, kmax_ref[...], hm)
        r, c = _tri_iotas(t)
        causal = c <= r

        q2 = jnp.concatenate(qh, axis=0)
        causal2 = jnp.concatenate([causal, causal], axis=0)

        def scores(j):
            return _dot_nt(q2, k_ref[_rows(j, t), :].astype(BF16))

        def chunk(j, carry, z2, masked):
            m_run, l_run, acc = carry
            vb = v_ref[_rows(j, t), :].astype(BF16)
            z = jnp.concatenate(
                [z2[h * t:(h + 1) * t] + (bq[h] - cfr_ref[0, pl.ds(h, 1), _rows(j, t)])
                 for h in range(2)], axis=0)
            if masked:
                z = jnp.where(causal2, z, -1e30)
            m_new = jnp.maximum(m_run, jnp.max(z, axis=1, keepdims=True))
            alpha = jnp.exp(m_run - m_new)
            pr = jnp.exp(z - m_new)
            return (m_new, alpha * l_run + jnp.sum(pr, axis=1, keepdims=True),
                    alpha * acc + _dot(pr.astype(BF16), vb))

        init = (jnp.full((2 * t, 1), -1e30, F32), jnp.zeros((2 * t, 1), F32),
                jnp.zeros((2 * t, LANES), F32))
        carry = chunk(i, init, scores(i), True)

        def live(j, cr):
            return _fox_live(cfr_ref, j, t,
                             [qk_top[h] + bq[h] - cr[0][h * t:(h + 1) * t] for h in range(2)])

        def step(st):
            j, _, cr, z2 = st
            z2_next = scores(jnp.maximum(j - 1, 0))
            cr = chunk(j, cr, z2, False)
            return j - 1, live(j - 1, cr), cr, z2_next

        m_fin, l_fin, acc = lax.while_loop(
            lambda st: st[1], step,
            (i - 1, live(i - 1, carry), carry, scores(jnp.maximum(i - 1, 0))))[2]
        o2 = acc / l_fin
        lse2 = m_fin + jnp.log(l_fin)
        o_ref[...] = jnp.where(hm[0], o2[:t], o2[t:])
        lse_ref[...] = jnp.where(hm[0], lse2[:t], lse2[t:])

    blk = pl.BlockSpec((t, LANES), lambda p, i: (i, p))
    full = pl.BlockSpec((s, LANES), lambda p, i: (0, p))
    return _pcall(body, name=name, grid=(FOX_W // LANES, s // t),
                  in_specs=[blk, full, pl.BlockSpec((s, LANES), lambda p, i: (0, CB_VB + p)),
                            pl.BlockSpec((t, LANES), lambda p, i: (i, 0)),
                            pl.BlockSpec((1, 2, s), lambda p, i: (p, 0, 0)),
                            pl.BlockSpec((1, LANES), lambda p, i: (0, p))],
                  out_specs=[blk, blk],
                  out_shape=[jax.ShapeDtypeStruct((s, FOX_W), F32)] * 2,
                  semantics=("parallel", "arbitrary"))(qn, kn, proj, cf, cf_rows, kmax)


def _fox_bwd(proj, qn, kn, cf, cf_rows, kmax, do, o, lse, *, name):
    s = proj.shape[0]
    t = min(ATT_T, s)
    scale = HEAD_DIM ** -0.5

    def body(q_ref, k_ref, v_ref, cfc_ref, cfr_ref, kmax_ref, do_ref, o_ref, lse_ref,
             dq_ref, dk_ref, dv_ref, dcf_ref, dcfq_ref):
        p, i = pl.program_id(0), pl.program_id(1)

        @pl.when(i == 0)
        def _():
            dk_ref[...] = jnp.zeros_like(dk_ref)
            dv_ref[...] = jnp.zeros_like(dv_ref)
            dcf_ref[...] = jnp.zeros_like(dcf_ref)

        hm = _lane_masks()
        q = q_ref[...] * scale
        do = do_ref[...]
        dov = do * o_ref[...]
        qh = [jnp.where(mk, q, 0.0).astype(BF16) for mk in hm]
        doh = [jnp.where(mk, do, 0.0).astype(BF16) for mk in hm]
        delta = [jnp.sum(jnp.where(mk, dov, 0.0), axis=1, keepdims=True) for mk in hm]
        lsev = lse_ref[...]
        lse = [lsev[:, 0:1], lsev[:, HEAD_DIM:HEAD_DIM + 1]]
        cfc = cfc_ref[...]
        bq = [_fox_bias_q(cfc, p, h) - lse[h] for h in range(2)]
        qk_top = _fox_score_bound(q, kmax_ref[...], hm)
        tops = [qk_top[h] + bq[h] for h in range(2)]
        r, c = _tri_iotas(t)
        j_stop = lax.while_loop(lambda st: st[1],
                                lambda st: (st[0] - 1, _fox_live(cfr_ref, st[0] - 1, t, tops)),
                                (i - 1, _fox_live(cfr_ref, i - 1, t, tops)))[0]
        q2 = jnp.concatenate(qh, axis=0)
        do2 = jnp.concatenate(doh, axis=0)
        delta2 = jnp.concatenate(delta, axis=0)
        causal2 = jnp.concatenate([c <= r, c <= r], axis=0)

        def products(j):
            return (_dot_nt(q2, k_ref[_rows(j, t), :].astype(BF16)),
                    _dot_nt(do2, v_ref[_rows(j, t), :].astype(BF16)))

        def chunk(j, carry, z2, dp, masked):
            dq, row_sum = carry
            z = jnp.concatenate(
                [z2[h * t:(h + 1) * t] + (bq[h] - cfr_ref[0, pl.ds(h, 1), _rows(j, t)])
                 for h in range(2)], axis=0)
            pr = jnp.exp(z)
            if masked:
                pr = jnp.where(causal2, pr, 0.0)
            ds = pr * (dp - delta2)
            dsb = ds.astype(BF16)
            dk_ref[_rows(j, t), :] += _dot_tn(dsb, q2)
            dv_ref[_rows(j, t), :] += _dot_tn(pr.astype(BF16), do2)
            for h in range(2):
                dcf_ref[0, pl.ds(h, 1), _rows(j, t)] -= jnp.sum(ds[h * t:(h + 1) * t], axis=0,
                                                               keepdims=True)
            return (dq + _dot(dsb, k_ref[_rows(j, t), :].astype(BF16)),
                    row_sum + jnp.sum(ds, axis=1, keepdims=True))

        init = (jnp.zeros((2 * t, LANES), F32), jnp.zeros((2 * t, 1), F32))
        carry = lax.fori_loop(j_stop + 1, i,
                              lambda j, cr: chunk(j, cr, *products(j), False), init)
        dq2, row_sum = chunk(i, carry, *products(i), True)
        dq_ref[...] = jnp.where(hm[0], dq2[:t], dq2[t:]) * scale
        dcfq_ref[...] = jnp.where(hm[0], row_sum[:t], row_sum[t:])

    blk = pl.BlockSpec((t, LANES), lambda p, i: (i, p))
    full = pl.BlockSpec((s, LANES), lambda p, i: (0, p))
    rows = pl.BlockSpec((1, 2, s), lambda p, i: (p, 0, 0))
    return _pcall(body, name=name, grid=(FOX_W // LANES, s // t),
                  in_specs=[blk, full, pl.BlockSpec((s, LANES), lambda p, i: (0, CB_VB + p)),
                            pl.BlockSpec((t, LANES), lambda p, i: (i, 0)), rows,
                            pl.BlockSpec((1, LANES), lambda p, i: (0, p)),
                            pl.BlockSpec((t, LANES), lambda p, i: (i, SB_W // LANES + p)),
                            blk, blk],
                  out_specs=[blk, full, full, rows, blk],
                  out_shape=[jax.ShapeDtypeStruct((s, FOX_W), F32)] * 3
                  + [jax.ShapeDtypeStruct((FOX_W // LANES, 2, s), F32),
                     jax.ShapeDtypeStruct((s, FOX_W), F32)],
                  semantics=("parallel", "arbitrary"))(qn, kn, proj, cf, cf_rows, kmax, do, o, lse)


_GELU_C0 = math.sqrt(2.0 / math.pi)
_GELU_C1 = 0.044715


def _gelu(x):
    th = jnp.tanh(_GELU_C0 * (x + _GELU_C1 * (x * x * x)))
    return 0.5 * x * (1.0 + th), th


def _gelu_grad(x, th):
    return 0.5 * (1.0 + th) + 0.5 * x * (1.0 - th * th) * (_GELU_C0 * (1.0 + 3.0 * _GELU_C1 * x * x))


def _sgu_mix(wm, vn_c, lo, bcol):
    return jnp.where(lo, _dot(wm[0], vn_c) + bcol[0], _dot(wm[1], vn_c) + bcol[1])


def _sgu_fwd(proj, w, b_cols, gn, *, name):
    s = proj.shape[0]
    tr = min(512, s)
    ch = SGU_CHUNK

    def body(u_ref, v_ref, w_ref, b_ref, gn_ref, o_ref):
        lo = _lane_masks()[0]
        r, c = _tri_iotas(ch)
        wm = [jnp.where(c <= r, w_ref[h], 0.0).astype(BF16) for h in range(2)]
        bcol = [b_ref[0, :, h:h + 1] for h in range(2)]
        for n in range(tr // ch):
            rows = slice(n * ch, (n + 1) * ch)
            u, _ = _gelu(u_ref[rows, :])
            vg, _ = _gelu(v_ref[rows, :])
            vn = vg * lax.rsqrt(_group_mean(vg * vg, lo) + EPS) * gn_ref[0]
            o_ref[rows, :] = u * _sgu_mix(wm, vn.astype(BF16), lo, bcol)

    blk = lambda cb: pl.BlockSpec((tr, LANES), lambda p, i: (i, cb + p))
    return _pcall(body, name=name, grid=(SGU_W // LANES, s // tr),
                  in_specs=[blk(CB_UC), blk(CB_VC),
                            pl.BlockSpec((2, ch, ch), lambda p, i: (p, 0, 0)),
                            pl.BlockSpec((1, ch, 2), lambda p, i: (p, 0, 0)),
                            pl.BlockSpec((1, 1, LANES), lambda p, i: (p, 0, 0))],
                  out_specs=pl.BlockSpec((tr, LANES), lambda p, i: (i, p)),
                  out_shape=jax.ShapeDtypeStruct((s, SGU_W), F32),
                  semantics=("parallel", "parallel"))(proj, proj, w, b_cols, gn)


def _sgu_bwd(proj, dmixed, w, w_t, b_cols, gn, *, name):
    s = proj.shape[0]
    tr = min(512, s)
    ch = SGU_CHUNK
    cb_do = (SB_W + FOX_W) // LANES

    def body(u_ref, v_ref, do_ref, w_ref, wt_ref, b_ref, gn_ref,
             du_ref, dv_ref, dw_ref, db_ref, dgn_ref):
        @pl.when(pl.program_id(1) == 0)
        def _():
            dw_ref[...] = jnp.zeros_like(dw_ref)
            db_ref[...] = jnp.zeros_like(db_ref)
            dgn_ref[...] = jnp.zeros_like(dgn_ref)

        hm = _lane_masks()
        lo = hm[0]
        r, c = _tri_iotas(ch)
        wm = [jnp.where(c <= r, w_ref[h], 0.0).astype(BF16) for h in range(2)]
        wtm = [jnp.where(r <= c, wt_ref[h], 0.0).astype(BF16) for h in range(2)]
        bcol = [b_ref[0, :, h:h + 1] for h in range(2)]
        gnv = gn_ref[0]
        for n in range(tr // ch):
            rows = slice(n * ch, (n + 1) * ch)
            uc, vc, do = u_ref[rows, :], v_ref[rows, :], do_ref[rows, :]
            u, thu = _gelu(uc)
            vg, thv = _gelu(vc)
            rinv = lax.rsqrt(_group_mean(vg * vg, lo) + EPS)
            xh = vg * rinv
            vnb = (xh * gnv).astype(BF16)
            mix = _sgu_mix(wm, vnb, lo, bcol)
            du_ref[rows, :] = do * mix * _gelu_grad(uc, thu)
            dm = do * u
            dmb = dm.astype(BF16)
            dvn = jnp.where(lo, _dot(wtm[0], dmb), _dot(wtm[1], dmb))
            for h in range(2):
                dmh = jnp.where(hm[h], dm, 0.0)
                dw_ref[h] += jnp.where(c <= r, _dot_nt(dmh.astype(BF16), vnb), 0.0)
                db_ref[0, :, h:h + 1] += jnp.sum(dmh, axis=1, keepdims=True)
            dgn_ref[0] += _colsum(dvn * xh)
            dxh = dvn * gnv
            dvg = rinv * (dxh - xh * _group_mean(dxh * xh, lo))
            dv_ref[rows, :] = dvg * _gelu_grad(vc, thv)

    blk = lambda cb: pl.BlockSpec((tr, LANES), lambda p, i: (i, cb + p))
    w_spec = pl.BlockSpec((2, ch, ch), lambda p, i: (p, 0, 0))
    b_spec = pl.BlockSpec((1, ch, 2), lambda p, i: (p, 0, 0))
    g_spec = pl.BlockSpec((1, 1, LANES), lambda p, i: (p, 0, 0))
    out_blk = pl.BlockSpec((tr, LANES), lambda p, i: (i, p))
    return _pcall(body, name=name, grid=(SGU_W // LANES, s // tr),
                  in_specs=[blk(CB_UC), blk(CB_VC), blk(cb_do), w_spec, w_spec, b_spec, g_spec],
                  out_specs=[out_blk, out_blk, w_spec, b_spec, g_spec],
                  out_shape=[jax.ShapeDtypeStruct((s, SGU_W), F32)] * 2
                  + [jax.ShapeDtypeStruct(w.shape, F32), jax.ShapeDtypeStruct(b_cols.shape, F32),
                     jax.ShapeDtypeStruct(gn.shape, F32)],
                  semantics=("parallel", "arbitrary"))(proj, proj, dmixed, w, w_t, b_cols, gn)


def _pad_lanes(v):
    return jnp.zeros((1, LANES), F32).at[0, :v.shape[0]].set(v)


def _small_views(sm):
    return dict(
        n1=sm["norm1_g"][None, :], n2=sm["norm2_g"][None, :],
        b_pad=_pad_lanes(sm["b_forget"]),
        qg=jnp.tile(sm["q_norm_g"], 2)[None, :], kg=jnp.tile(sm["k_norm_g"], 2)[None, :],
        gn=sm["sgu_norm_g"].reshape(2, 1, LANES),
        w=sm["sgu_w"], w_t=jnp.swapaxes(sm["sgu_w"], 1, 2),
        b_cols=sm["sgu_b"].reshape(2, 2, SGU_CHUNK).transpose(0, 2, 1))


def _cf_rows(cf):
    return cf[:, :FOX_HEADS].T.reshape(FOX_W // LANES, 2, cf.shape[0])


def _layer_fwd(x_in, prev, mod, wts, sm, l):
    sh1, sc1, g1, sh2, sc2, g2 = mod
    v = _small_views(sm)
    if prev is None:
        x0 = x_in
        h1 = _norm_mod_fwd(x0, v["n1"], sc1, sh1, name=f"l{l}_norm1")
    else:
        x0, h1 = _resid_norm_mod_fwd(x_in, prev[0], prev[1], v["n1"], sc1, sh1, name=f"l{l}_norm1")
    proj = _matmul(h1, wts["w_in"], name=f"l{l}_proj")
    o_sb = _sb_fwd(proj, name=f"l{l}_sb_fwd")
    qn, kn, kmax = _fox_prep_fwd(proj, v["qg"], v["kg"], name=f"l{l}_fox_prep")
    cf = _forget_cumsum_fwd(proj, v["b_pad"], name=f"l{l}_cumf")
    cfr = _cf_rows(cf)
    o_fox, lse = _fox_fwd(proj, qn, kn, cf, cfr, kmax, name=f"l{l}_fox_fwd")
    o_sgu = _sgu_fwd(proj, v["w"], v["b_cols"], v["gn"], name=f"l{l}_sgu_fwd")
    mixed = jnp.concatenate([o_sb, o_fox, o_sgu], axis=1).astype(BF16)
    mo = _matmul(mixed, wts["w_out"], name=f"l{l}_wout")
    x1, h2 = _resid_norm_mod_fwd(x0, mo, g1, v["n2"], sc2, sh2, name=f"l{l}_norm2")
    a, rr = _matmul(h2, wts["w1"], name=f"l{l}_mlp1", out_dtype=BF16, relu2=BF16)
    m2 = _matmul(rr, wts["w2"], name=f"l{l}_mlp2")
    saved = dict(x0=x0, h1=h1, proj=proj, qn=qn, kn=kn, kmax=kmax, cf=cf, cfr=cfr, o_fox=o_fox,
                 lse=lse, mixed=mixed, mo=mo, x1=x1, h2=h2, a=a, rr=rr, m2=m2)
    return saved


def _layer_bwd(dx2, dm2, dg2, sv, mod, wts, sm, l, below):
    sh1, sc1, g1, sh2, sc2, g2 = mod
    v = _small_views(sm)
    dw2 = _matmul(sv["rr"], dm2, ta=True, name=f"l{l}_dw2")
    da = _matmul(dm2, wts["w2"], tb=True, name=f"l{l}_da", out_dtype=BF16, pre_act=sv["a"])
    dw1 = _matmul(sv["h2"], da, ta=True, name=f"l{l}_dw1")
    dh2 = _matmul(da, wts["w1"], tb=True, name=f"l{l}_dh2")
    dx1, dn2, dsc2, dsh2, dmo, dg1 = _norm_mod_bwd(sv["x1"], dh2, dx2, v["n2"], sc2,
                                                    (sv["mo"], g1), name=f"l{l}_norm2_bwd")
    dwo = _matmul(sv["mixed"], dmo, ta=True, name=f"l{l}_dwout")
    dmixed = _matmul(dmo, wts["w_out"], tb=True, name=f"l{l}_dmixed")
    proj = sv["proj"]
    dqa, dka, dva = _sb_bwd(proj, dmixed, name=f"l{l}_sb_bwd")
    dqn, dkn, dvb, dcfr, dcfq = _fox_bwd(proj, sv["qn"], sv["kn"], sv["cf"], sv["cfr"], sv["kmax"], dmixed,
                                   sv["o_fox"], sv["lse"], name=f"l{l}_fox_bwd")
    dqb, dkb, dqg, dkg = _fox_prep_bwd(proj, dqn, dkn, v["qg"], v["kg"], name=f"l{l}_fox_prep_bwd")
    s = proj.shape[0]
    dcf_heads = dcfr.reshape(FOX_HEADS, s).T + dcfq.reshape(s, FOX_HEADS, HEAD_DIM)[:, :, 0]
    dcf = jnp.zeros((s, LANES), F32).at[:, :FOX_HEADS].set(dcf_heads)
    dfl, dbf = _forget_cumsum_bwd(proj, v["b_pad"], dcf, name=f"l{l}_cumf_bwd")
    duc, dvc, dsw, dsb_cols, dgn = _sgu_bwd(proj, dmixed, v["w"], v["w_t"], v["b_cols"], v["gn"],
                                            name=f"l{l}_sgu_bwd")
    dproj = jnp.concatenate([dqa, dka, dva, dqb, dkb, dvb, duc, dvc, dfl,
                             jnp.zeros((s, LANES), F32)], axis=1).astype(BF16)
    dwin = _matmul(sv["h1"], dproj, ta=True, name=f"l{l}_dwin")
    dh1 = _matmul(dproj, wts["w_in"], tb=True, name=f"l{l}_dh1")
    dx0, dn1, dsc1, dsh1, dm_below, dg_below = _norm_mod_bwd(sv["x0"], dh1, dx1, v["n1"], sc1, below,
                                                             name=f"l{l}_norm1_bwd")
    big = dict(w_in=dwin, w_out=dwo, w1=dw1, w2=dw2)
    small = dict(norm1_g=dn1[0], norm2_g=dn2[0], b_forget=dbf[0, :FOX_HEADS],
                 q_norm_g=dqg[0, :HEAD_DIM] + dqg[0, HEAD_DIM:],
                 k_norm_g=dkg[0, :HEAD_DIM] + dkg[0, HEAD_DIM:],
                 sgu_norm_g=dgn.reshape(4, HEAD_DIM), sgu_w=dsw,
                 sgu_b=dsb_cols.transpose(0, 2, 1).reshape(4, SGU_CHUNK))
    dmod = jnp.concatenate([dsh1, dsc1, dg1, dsh2, dsc2, dg2], axis=1)
    return dx0, dm_below, dg_below, big, small, dmod


def _w_in_to_internal(w):
    pad = jnp.zeros((w.shape[0], PROJ_W - IN_W), w.dtype)
    return jnp.concatenate([w[:, :ATT_W], w[:, ATT_W + FOX_HEADS:], w[:, ATT_W:ATT_W + FOX_HEADS],
                            pad], axis=1)


def _w_in_from_internal(g):
    n_gate = SGU_W * 2
    return jnp.concatenate([g[:, :ATT_W], g[:, ATT_W + n_gate:ATT_W + n_gate + FOX_HEADS],
                            g[:, ATT_W:ATT_W + n_gate]], axis=1)


def _exchange(x, masks, slot_shift, slot_bits, scatter, *, name):
    n_slots = 2 ** slot_bits
    blk_shape = x.shape[1:] if scatter else x.shape
    n_peers = len(masks)

    def body(x_ref, out_ref, send_sems, recv_sems, local_sem):
        ids = (lax.axis_index("x"), lax.axis_index("y"), lax.axis_index("c"))
        me = 4 * ids[0] + 2 * ids[1] + ids[2]
        my_slot = (me >> slot_shift) & (n_slots - 1)

        def peer(mask):
            return tuple(1 - v if (mask >> b) & 1 else v for v, b in zip(ids, (2, 1, 0)))

        def src_for(slot):
            return x_ref.at[slot] if scatter else x_ref

        copies = [pltpu.make_async_copy(src_for(my_slot), out_ref.at[my_slot], local_sem)]
        for kk, mask in enumerate(masks):
            peer_slot = ((me ^ mask) >> slot_shift) & (n_slots - 1)
            copies.append(pltpu.make_async_remote_copy(
                src_ref=src_for(peer_slot), dst_ref=out_ref.at[my_slot],
                send_sem=send_sems.at[kk], recv_sem=recv_sems.at[kk],
                device_id=peer(mask), device_id_type=MESH))
        for cp in copies:
            cp.start()
        for cp in copies:
            cp.wait()

    any_spec = pl.BlockSpec(memory_space=pl.ANY)
    return _pcall(body, name=name, in_specs=[any_spec], out_specs=any_spec,
                  out_shape=jax.ShapeDtypeStruct((n_slots,) + tuple(blk_shape), x.dtype),
                  scratch_shapes=[pltpu.SemaphoreType.DMA((n_peers,)),
                                  pltpu.SemaphoreType.DMA((n_peers,)),
                                  pltpu.SemaphoreType.DMA(())])(x)


CORE_PIECE_BYTES = 12 * 2 ** 20
CORE_DMA_CHUNKS = 4


def _core_swap_piece(x, *, name):
    rows, cols = x.shape
    n_ch = CORE_DMA_CHUNKS if rows % (16 * CORE_DMA_CHUNKS) == 0 else 1
    rc = rows // n_ch

    def body(x_ref, out_ref, send_sems, recv_sems):
        sibling = (lax.axis_index("x"), lax.axis_index("y"), 1 - lax.axis_index("c"))
        copies = [pltpu.make_async_remote_copy(
            src_ref=x_ref.at[pl.ds(ch * rc, rc)], dst_ref=out_ref.at[pl.ds(ch * rc, rc)],
            send_sem=send_sems.at[ch], recv_sem=recv_sems.at[ch],
            device_id=sibling, device_id_type=MESH) for ch in range(n_ch)]
        for cp in copies:
            cp.start()
        for cp in copies:
            cp.wait()

    vmem = pl.BlockSpec(memory_space=pltpu.VMEM)
    return _pcall(body, name=name, in_specs=[vmem], out_specs=vmem,
                  out_shape=jax.ShapeDtypeStruct(x.shape, x.dtype),
                  scratch_shapes=[pltpu.SemaphoreType.DMA((n_ch,)),
                                  pltpu.SemaphoreType.DMA((n_ch,))])(x)


def _core_swap(x, *, name):
    rows, cols = x.shape
    n = 1
    while (rows % n or (rows // n) % 16 or
           (rows // n) * (-(-cols // LANES) * LANES) * x.dtype.itemsize > CORE_PIECE_BYTES):
        n += 1
    pr = rows // n
    pieces = [_core_swap_piece(x[kk * pr:(kk + 1) * pr], name=f"{name}_{kk}") for kk in range(n)]
    return pieces[0] if n == 1 else jnp.concatenate(pieces, axis=0)


def _by_core(core, mine, theirs, axis):
    return jnp.where(core == 0, jnp.concatenate([mine, theirs], axis=axis),
                     jnp.concatenate([theirs, mine], axis=axis))


def _gather_chips(x, *, name):
    return _exchange(x, (2, 4, 6), 1, 2, False, name=name)


def _gather_all(x, *, name):
    return _exchange(x, (1, 2, 3, 4, 5, 6, 7), 0, 3, False, name=name)


def _scatter_chips(x4, *, name):
    return _exchange(x4, (2, 4, 6), 1, 2, True, name=name)


def _sum_slots(parts, *, name, out_dtype=F32, tr=256):
    n, rows, cols = parts.shape
    tr = min(tr, rows)
    assert rows % tr == 0, (name, rows, tr)

    def body(p_ref, o_ref):
        acc = p_ref[0].astype(F32)
        for kk in range(1, n):
            acc = acc + p_ref[kk].astype(F32)
        o_ref[...] = acc.astype(o_ref.dtype)

    return _pcall(body, name=name, grid=(rows // tr,),
                  in_specs=[pl.BlockSpec((n, tr, cols), lambda i: (0, i, 0))],
                  out_specs=pl.BlockSpec((tr, cols), lambda i: (i, 0)),
                  out_shape=jax.ShapeDtypeStruct((rows, cols), out_dtype),
                  semantics=("parallel",))(parts)


def _add2(a, b, *, name, out_dtype, tr=512):
    def fn(f, v):
        return [f[0] + f[1]], []
    (out,), _ = _rowwise(fn, [a, b], [], [out_dtype], 0, name=name, tr=tr)
    return out


def _adamw(w, m, v, parts, *, name, tr=256):
    n, rows, cols = parts.shape
    tr = min(tr, rows)
    assert rows % tr == 0, (name, rows, tr)
    c1 = 1.0 - ADAM_B1 ** ADAM_STEP
    c2 = 1.0 - ADAM_B2 ** ADAM_STEP

    def body(w_ref, m_ref, v_ref, p_ref, g_ref, d_ref, nm_ref, nv_ref):
        g = p_ref[0]
        for kk in range(1, n):
            g = g + p_ref[kk]
        nm = ADAM_B1 * m_ref[...] + (1.0 - ADAM_B1) * g
        nv = ADAM_B2 * v_ref[...] + (1.0 - ADAM_B2) * (g * g)
        g_ref[...] = g
        nm_ref[...] = nm
        nv_ref[...] = nv
        d_ref[...] = -ADAM_LR * ((nm / c1) / (jnp.sqrt(nv / c2) + ADAM_EPS) + ADAM_WD * w_ref[...])

    spec = pl.BlockSpec((tr, cols), lambda i: (i, 0))
    return _pcall(body, name=name, grid=(rows // tr,),
                  in_specs=[spec, spec, spec, pl.BlockSpec((n, tr, cols), lambda i: (0, i, 0))],
                  out_specs=[spec] * 4,
                  out_shape=[jax.ShapeDtypeStruct((rows, cols), F32)] * 4,
                  semantics=("parallel",))(w, m, v, parts)


def _silu(c):
    return c / (1.0 + jnp.exp(-c))


def _ada_fwd(c_all, ada_w, ada_b_sh, *, name):
    nl, d, wsh = ada_w.shape

    def body(c_ref, w_ref, b_ref, o_ref):
        cond = _silu(c_ref[...]).astype(BF16)
        o_ref[0] = _dot(cond, w_ref[0].astype(BF16)) + b_ref[0]

    return _pcall(body, name=name, grid=(nl,),
                  in_specs=[pl.BlockSpec(c_all.shape, lambda l: (0, 0)),
                            pl.BlockSpec((1, d, wsh), lambda l: (l, 0, 0)),
                            pl.BlockSpec((1, 1, wsh), lambda l: (l, 0, 0))],
                  out_specs=pl.BlockSpec((1, c_all.shape[0], wsh), lambda l: (l, 0, 0)),
                  out_shape=jax.ShapeDtypeStruct((nl, c_all.shape[0], wsh), F32),
                  semantics=("parallel",))(c_all, ada_w, ada_b_sh)


def _ada_bwd(c_all, dmod_sh, *, name):
    nl, nb, wsh = dmod_sh.shape
    d = c_all.shape[1]

    def body(c_ref, dm_ref, o_ref):
        cond = _silu(c_ref[...]).astype(BF16)
        o_ref[0] = _dot_tn(cond, dm_ref[0].astype(BF16))

    return _pcall(body, name=name, grid=(nl,),
                  in_specs=[pl.BlockSpec(c_all.shape, lambda l: (0, 0)),
                            pl.BlockSpec((1, nb, wsh), lambda l: (l, 0, 0))],
                  out_specs=pl.BlockSpec((1, d, wsh), lambda l: (l, 0, 0)),
                  out_shape=jax.ShapeDtypeStruct((nl, d, wsh), F32),
                  semantics=("parallel",))(c_all, dmod_sh)


SMALL_NAMES = ("norm1_g", "norm2_g", "b_forget", "q_norm_g", "k_norm_g", "sgu_norm_g", "sgu_w",
               "sgu_b")
WEIGHT_NAMES = ("ada_w", "ada_b", "norm1_g", "norm2_g", "w_in", "b_forget", "q_norm_g", "k_norm_g",
                "sgu_norm_g", "sgu_w", "sgu_b", "w_out", "mlp_w1", "mlp_w2")


SMALL_TILE_ROWS = 256


def _pack_small(tree):
    flat = jnp.concatenate([tree[n].reshape(-1) for n in SMALL_NAMES])
    n = flat.shape[0]
    rows = -(-n // (SMALL_TILE_ROWS * LANES)) * SMALL_TILE_ROWS
    return jnp.zeros((rows * LANES,), F32).at[:n].set(flat).reshape(rows, LANES)


def _unpack_small(packed, like):
    flat = packed.reshape(-1)
    out, off = {}, 0
    for n in SMALL_NAMES:
        size = like[n].size
        out[n] = flat[off:off + size].reshape(like[n].shape)
        off += size
    return out


def kernel(x, c, ada_w, ada_b, norm1_g, norm2_g, w_in, b_forget, q_norm_g, k_norm_g, sgu_norm_g, sgu_w, sgu_b, w_out, mlp_w1, mlp_w2, loss_target, m_ada_w, m_ada_b, m_norm1_g, m_norm2_g, m_w_in, m_b_forget, m_q_norm_g, m_k_norm_g, m_sgu_norm_g, m_sgu_w, m_sgu_b, m_w_out, m_mlp_w1, m_mlp_w2, v_ada_w, v_ada_b, v_norm1_g, v_norm2_g, v_w_in, v_b_forget, v_q_norm_g, v_k_norm_g, v_sgu_norm_g, v_sgu_w, v_sgu_b, v_w_out, v_mlp_w1, v_mlp_w2):
    w = dict(ada_w=ada_w, ada_b=ada_b, norm1_g=norm1_g, norm2_g=norm2_g, w_in=w_in,
             b_forget=b_forget, q_norm_g=q_norm_g, k_norm_g=k_norm_g, sgu_norm_g=sgu_norm_g,
             sgu_w=sgu_w, sgu_b=sgu_b, w_out=w_out, mlp_w1=mlp_w1, mlp_w2=mlp_w2)
    mom = dict(ada_w=m_ada_w, ada_b=m_ada_b, norm1_g=m_norm1_g, norm2_g=m_norm2_g, w_in=m_w_in,
               b_forget=m_b_forget, q_norm_g=m_q_norm_g, k_norm_g=m_k_norm_g,
               sgu_norm_g=m_sgu_norm_g, sgu_w=m_sgu_w, sgu_b=m_sgu_b, w_out=m_w_out,
               mlp_w1=m_mlp_w1, mlp_w2=m_mlp_w2)
    var = dict(ada_w=v_ada_w, ada_b=v_ada_b, norm1_g=v_norm1_g, norm2_g=v_norm2_g, w_in=v_w_in,
               b_forget=v_b_forget, q_norm_g=v_q_norm_g, k_norm_g=v_k_norm_g,
               sgu_norm_g=v_sgu_norm_g, sgu_w=v_sgu_w, sgu_b=v_sgu_b, w_out=v_w_out,
               mlp_w1=v_mlp_w1, mlp_w2=v_mlp_w2)
    depth, d = norm1_g.shape
    chip = 2 * lax.axis_index("x") + lax.axis_index("y")
    me = 2 * chip + lax.axis_index("c")
    n_chips = 4
    ada_sh = ada_w.shape[2]

    core = lax.axis_index("c")
    half_l = depth // 2

    def gather_weight(w_sh, name):
        _, r, cols = w_sh.shape
        mine = lax.dynamic_slice_in_dim(w_sh, core * half_l, half_l, axis=0).astype(BF16)
        got = _gather_chips(mine.reshape(half_l * r, cols), name=f"gather_{name}")
        theirs = _core_swap(got.reshape(n_chips * half_l * r, cols), name=f"share_{name}")
        return _by_core(core, got.reshape(n_chips, half_l, r, cols),
                        theirs.reshape(n_chips, half_l, r, cols), 1)

    g_in = gather_weight(w_in, "w_in")
    g_out = gather_weight(w_out, "w_out")
    g_w1 = gather_weight(mlp_w1, "w1")
    g_w2 = gather_weight(mlp_w2, "w2")
    layer_w = []
    for l in range(depth):
        layer_w.append(dict(
            w_in=_w_in_to_internal(jnp.concatenate([g_in[k, l] for k in range(n_chips)], axis=1)),
            w_out=g_out[:, l].reshape(d, d),
            w1=jnp.concatenate([g_w1[k, l] for k in range(n_chips)], axis=1),
            w2=g_w2[:, l].reshape(D_FF, d)))

    c_all = _gather_all(jnp.zeros((8, d), F32).at[0].set(c[0]), name="gather_c")[:, 0]
    c_pad = jnp.concatenate([c_all, jnp.zeros_like(c_all)], axis=0)
    ada_b_sh = lax.dynamic_slice_in_dim(ada_b, chip * ada_sh, ada_sh, axis=1)[:, None, :]
    mod_sh = _ada_fwd(c_pad, ada_w, ada_b_sh, name="ada_fwd")
    mod_all = _gather_chips(mod_sh, name="gather_mod")
    mod_me = lax.dynamic_index_in_dim(mod_all, me, axis=2, keepdims=False)
    mod_me = mod_me.transpose(1, 0, 2).reshape(depth, 6, 1, d)

    saved = []
    xs, prev = x[0], None
    for l in range(depth):
        mod = [mod_me[l, kk] for kk in range(6)]
        sm = {n: w[n][l] for n in SMALL_NAMES}
        sv = _layer_fwd(xs, prev, mod, layer_w[l], sm, l)
        saved.append(sv)
        xs, prev = sv["x1"], (sv["m2"], mod[5])

    sq, dxs, dm2, dg2 = _loss_fwd_bwd(xs, prev[0], prev[1], loss_target[0], name="loss")
    loss = lax.psum(0.5 * jnp.sum(sq) / d, ("x", "y", "c"))

    big = {n: [] for n in ("w_in", "w_out", "w1", "w2")}
    small = {n: [] for n in SMALL_NAMES}
    dmods = []
    for l in reversed(range(depth)):
        mod = [mod_me[l, kk] for kk in range(6)]
        sm = {n: w[n][l] for n in SMALL_NAMES}
        below = (saved[l - 1]["m2"], mod_me[l - 1, 5]) if l else None
        dxs, dm2, dg2, bg, smg, dmod = _layer_bwd(dxs, dm2, dg2, saved[l], mod, layer_w[l], sm, l,
                                                  below)
        for n in big:
            big[n].insert(0, bg[n])
        for n in SMALL_NAMES:
            small[n].insert(0, smg[n])
        dmods.insert(0, dmod)
    grad_x = dxs[None]

    out_g, out_d, out_m, out_v = {}, {}, {}, {}

    def run_adamw(name, parts2d, shape):
        rows, cols = parts2d.shape[1:]
        g, dl, nm, nv = _adamw(w[name].reshape(rows, cols), mom[name].reshape(rows, cols),
                               var[name].reshape(rows, cols), parts2d, name=f"adamw_{name}")
        out_g[name], out_d[name] = g.reshape(shape), dl.reshape(shape)
        out_m[name], out_v[name] = nm.reshape(shape), nv.reshape(shape)

    def shards_of(name, l):
        if name == "w_in":
            g = _w_in_from_internal(big["w_in"][l])
            return jnp.stack(jnp.split(g, n_chips, axis=1))
        if name == "mlp_w1":
            return jnp.stack(jnp.split(big["w1"][l], n_chips, axis=1))
        if name == "w_out":
            return big["w_out"][l].reshape(n_chips, d // n_chips, d)
        return big["w2"][l].reshape(n_chips, D_FF // n_chips, d)

    for name in ("w_in", "w_out", "mlp_w1", "mlp_w2"):
        per_chip = jnp.stack([shards_of(name, l) for l in range(depth)], axis=1)
        r, cols = per_chip.shape[2:]
        half_rows = half_l * r
        keep = lax.dynamic_slice_in_dim(per_chip, core * half_l, half_l, axis=1)
        send = lax.dynamic_slice_in_dim(per_chip, (1 - core) * half_l, half_l, axis=1)
        theirs = _core_swap(send.reshape(n_chips * half_rows, cols), name=f"pair_{name}")
        chip_sum = _add2(keep.reshape(n_chips * half_rows, cols), theirs, out_dtype=BF16,
                         name=f"pairsum_{name}")
        got = _scatter_chips(chip_sum.reshape(n_chips, half_rows, cols), name=f"scatter_{name}")
        half = _sum_slots(got, name=f"sum_{name}")
        both = _by_core(core, half, _core_swap(half, name=f"swap_{name}"), 0)
        run_adamw(name, both[None], w[name].shape)

    small_tree = {n: jnp.stack(small[n]) for n in SMALL_NAMES}
    gathered = _gather_all(_pack_small(small_tree), name="gather_small")
    gs, ds_, ms, vs = _adamw(_pack_small({n: w[n] for n in SMALL_NAMES}),
                             _pack_small({n: mom[n] for n in SMALL_NAMES}),
                             _pack_small({n: var[n] for n in SMALL_NAMES}), gathered,
                             name="adamw_small")
    like = {n: w[n] for n in SMALL_NAMES}
    for tree, packed in ((out_g, gs), (out_d, ds_), (out_m, ms), (out_v, vs)):
        tree.update(_unpack_small(packed, like))

    dmod_mine = jnp.concatenate(dmods, axis=0)
    dmod_all = _gather_all(jnp.zeros((depth, 8, 6 * d), F32).at[:, 0].set(dmod_mine),
                           name="gather_dmod")[:, :, 0]
    dmod_lb = dmod_all.transpose(1, 0, 2)
    dmod_sh = lax.dynamic_slice_in_dim(dmod_lb, chip * ada_sh, ada_sh, axis=2)
    dmod_sh = jnp.concatenate([dmod_sh, jnp.zeros_like(dmod_sh)], axis=1)
    g_ada_w = _ada_bwd(c_pad, dmod_sh, name="ada_bwd")
    run_adamw("ada_w", g_ada_w.reshape(1, depth * d, ada_sh), ada_w.shape)
    parts_b = dmod_all.reshape(8, depth * 6 * d // LANES, LANES)
    run_adamw("ada_b", parts_b, ada_b.shape)

    outs = [loss, grad_x]
    for tree in (out_g, out_d, out_m, out_v):
        outs += [tree[n] for n in WEIGHT_NAMES]
    return tuple(outs)
```

```python
import functools
import math

import jax
import jax.numpy as jnp
from jax import lax
from jax.experimental import pallas as pl
from jax.experimental.pallas import tpu as pltpu

F32 = jnp.float32
BF16 = jnp.bfloat16

D_MODEL = 1024
DEPTH = 4
HEAD_DIM = 64
LANES = 128
D_FF = 4 * D_MODEL
EPS = 1e-6
SB_W, FOX_W, SGU_W = 256, 512, 256
FOX_HEADS = 8
SGU_CHUNK = 128
IN_W = 2824
ATT_W = 3 * SB_W + 3 * FOX_W
PROJ_W = 3072
CB_QA, CB_KA, CB_VA = 0, 2, 4
CB_QB, CB_KB, CB_VB = 6, 10, 14
CB_UC, CB_VC, CB_FL = 18, 20, 22
ATT_T = 256
VMEM_LIMIT = 56 * 2 ** 20
SKIP_LOG = 110.0

ADAM_LR, ADAM_B1, ADAM_B2, ADAM_EPS, ADAM_WD, ADAM_STEP = 0.001, 0.9, 0.999, 1e-08, 0.01, 10

MESH = pl.DeviceIdType.MESH


def _pcall(body, *, name, out_shape, grid=(), in_specs=None, out_specs=None, scratch_shapes=(),
           semantics=None):
    params = dict(vmem_limit_bytes=VMEM_LIMIT)
    if semantics is not None:
        params["dimension_semantics"] = semantics
    kwargs = {}
    if in_specs is not None:
        kwargs["in_specs"] = in_specs
    if out_specs is not None:
        kwargs["out_specs"] = out_specs
    return pl.pallas_call(body, name=name, out_shape=out_shape, grid=grid,
                          scratch_shapes=list(scratch_shapes),
                          compiler_params=pltpu.CompilerParams(**params), **kwargs)


def _dot(a, b):
    return jnp.dot(a, b, preferred_element_type=F32)


def _dot_nt(a, b):
    return lax.dot_general(a, b, (((1,), (1,)), ((), ())), preferred_element_type=F32)


def _dot_tn(a, b):
    return lax.dot_general(a, b, (((0,), (0,)), ((), ())), preferred_element_type=F32)


def _split2(x):
    hi = x.astype(BF16)
    lo = (x - hi.astype(F32)).astype(BF16)
    return hi, lo


def _ones_dot(x, ones_bf16):
    hi, lo = _split2(x)
    return _dot(hi, ones_bf16) + _dot(lo, ones_bf16)


def _rowwise(fn, fulls, vecs, out_dtypes, n_vec_out, *, name, tr):
    s, n = fulls[0].shape
    tr = min(tr, s)
    assert s % tr == 0, (name, s, tr)
    nf, nv, nfo = len(fulls), len(vecs), len(out_dtypes)

    def body(*refs):
        fi, vi = refs[:nf], refs[nf:nf + nv]
        fo, vo = refs[nf + nv:nf + nv + nfo], refs[nf + nv + nfo:]
        outs_f, outs_v = fn([r[...] for r in fi], [r[...] for r in vi])
        for r, o in zip(fo, outs_f):
            r[...] = o.astype(r.dtype)
        if n_vec_out:
            @pl.when(pl.program_id(0) == 0)
            def _():
                for r in vo:
                    r[...] = jnp.zeros_like(r)
            for r, o in zip(vo, outs_v):
                r[...] += o

    full_spec = pl.BlockSpec((tr, n), lambda i: (i, 0))
    vec_specs = [pl.BlockSpec(v.shape, lambda i: (0, 0)) for v in vecs]
    out_vec_spec = pl.BlockSpec((1, n), lambda i: (0, 0))
    out_shape = [jax.ShapeDtypeStruct((s, n), dt) for dt in out_dtypes]
    out_shape += [jax.ShapeDtypeStruct((1, n), F32)] * n_vec_out
    outs = _pcall(body, name=name, grid=(s // tr,),
                  in_specs=[full_spec] * nf + vec_specs,
                  out_specs=[full_spec] * nfo + [out_vec_spec] * n_vec_out,
                  out_shape=out_shape,
                  semantics=("arbitrary",) if n_vec_out else ("parallel",))(*fulls, *vecs)
    return outs[:nfo], outs[nfo:]


def _colsum(t):
    return jnp.sum(t, axis=0, keepdims=True)


def _rms_mod(x, g, sc, sh):
    r = lax.rsqrt(jnp.mean(x * x, axis=-1, keepdims=True) + EPS)
    return (x * r * g) * (1.0 + sc) + sh


def _norm_mod_fwd(x, g, sc, sh, *, name):
    def fn(f, v):
        return [_rms_mod(f[0], v[0], v[1], v[2])], []
    (h,), _ = _rowwise(fn, [x], [g, sc, sh], [BF16], 0, name=name, tr=512)
    return h


def _resid_norm_mod_fwd(x, m, gate, g, sc, sh, *, name):
    def fn(f, v):
        xn = f[0] + v[0] * f[1]
        return [xn, _rms_mod(xn, v[1], v[2], v[3])], []
    (xn, h), _ = _rowwise(fn, [x, m], [gate, g, sc, sh], [F32, BF16], 0, name=name, tr=512)
    return xn, h


def _norm_mod_bwd(x, dh, dres, g, sc, gated, *, name):
    def fn(f, v):
        xv, dhv, dr = f[:3]
        gv, scv = v[:2]
        r = lax.rsqrt(jnp.mean(xv * xv, axis=-1, keepdims=True) + EPS)
        xh = xv * r
        dn = dhv * (1.0 + scv)
        dxh = dn * gv
        dx = dr + r * (dxh - xh * jnp.mean(dxh * xh, axis=-1, keepdims=True))
        sums = [_colsum(dn * xh), _colsum(dhv * (xh * gv)), _colsum(dhv)]
        if gated is None:
            return [dx], sums
        return [dx, dx * v[2]], sums + [_colsum(dx * f[3])]
    if gated is None:
        (dx,), (dg, dsc, dsh) = _rowwise(fn, [x, dh, dres], [g, sc], [F32], 3, name=name, tr=256)
        return dx, dg, dsc, dsh, None, None
    (dx, dm), (dg, dsc, dsh, dgate) = _rowwise(fn, [x, dh, dres, gated[0]], [g, sc, gated[1]],
                                               [F32, BF16], 4, name=name, tr=256)
    return dx, dg, dsc, dsh, dm, dgate


def _loss_fwd_bwd(x, m, gate, target, *, name):
    n = x.shape[1]

    def fn(f, v):
        err = f[0] + v[0] * f[1] - f[2]
        dy = err * (1.0 / n)
        return [dy, dy * v[0]], [_colsum(err * err), _colsum(dy * f[1])]
    (dy, dm), (sq, dgate) = _rowwise(fn, [x, m, target], [gate], [F32, BF16], 2, name=name, tr=512)
    return sq, dy, dm, dgate


def _matmul(a, b, *, name, ta=False, tb=False, out_dtype=F32, relu2=None, pre_act=None,
            tm=1024, tn=1024, tk=1024):
    m = a.shape[1] if ta else a.shape[0]
    k = a.shape[0] if ta else a.shape[1]
    n = b.shape[0] if tb else b.shape[1]
    assert k == (b.shape[1] if tb else b.shape[0])
    tm, tn, tk = min(tm, m), min(tn, n), min(tk, k)
    assert m % tm == 0 and n % tn == 0 and k % tk == 0, (name, m, n, k)
    nk = k // tk
    dims = (((0 if ta else 1,), (1 if tb else 0,)), ((), ()))

    def body(*refs):
        a_ref, b_ref = refs[:2]
        acc_ref = refs[-1]
        kk = pl.program_id(2)

        @pl.when(kk == 0)
        def _():
            acc_ref[...] = jnp.zeros_like(acc_ref)
        acc_ref[...] += lax.dot_general(a_ref[...].astype(BF16), b_ref[...].astype(BF16), dims,
                                        preferred_element_type=F32)

        @pl.when(kk == nk - 1)
        def _():
            acc = acc_ref[...]
            if pre_act is not None:
                acc = acc * (2.0 * jnp.maximum(refs[2][...].astype(F32), 0.0))
            o_ref = refs[-2 - (relu2 is not None)]
            o_ref[...] = acc.astype(o_ref.dtype)
            if relu2 is not None:
                r = jnp.maximum(acc, 0.0)
                refs[-2][...] = (r * r).astype(relu2)

    a_spec = (pl.BlockSpec((tk, tm), lambda i, j, kk: (kk, i)) if ta
              else pl.BlockSpec((tm, tk), lambda i, j, kk: (i, kk)))
    b_spec = (pl.BlockSpec((tn, tk), lambda i, j, kk: (j, kk)) if tb
              else pl.BlockSpec((tk, tn), lambda i, j, kk: (kk, j)))
    out_spec = pl.BlockSpec((tm, tn), lambda i, j, kk: (i, j))
    in_specs, args = [a_spec, b_spec], [a, b]
    if pre_act is not None:
        in_specs.append(out_spec)
        args.append(pre_act)
    out_specs, out_shape = out_spec, jax.ShapeDtypeStruct((m, n), out_dtype)
    if relu2 is not None:
        out_specs, out_shape = [out_spec] * 2, [out_shape, jax.ShapeDtypeStruct((m, n), relu2)]
    return _pcall(body, name=name, grid=(m // tm, n // tn, nk),
                  in_specs=in_specs, out_specs=out_specs, out_shape=out_shape,
                  scratch_shapes=[pltpu.VMEM((tm, tn), F32)],
                  semantics=("parallel", "parallel", "arbitrary"))(*args)


def _lane_masks():
    lane = lax.broadcasted_iota(jnp.int32, (1, LANES), 1)
    return [lane < HEAD_DIM, lane >= HEAD_DIM]


def _tri_iotas(t):
    r = lax.broadcasted_iota(jnp.int32, (t, t), 0)
    c = lax.broadcasted_iota(jnp.int32, (t, t), 1)
    return r, c


def _rows(j, t):
    return pl.ds(pl.multiple_of(j * t, t), t)


def _neg_softplus(z):
    e = jnp.exp(-jnp.abs(z))
    return -(jnp.maximum(z, 0.0) + jnp.log(1.0 + e)), e


def _sb_fwd(proj, *, name):
    s = proj.shape[0]
    t = min(ATT_T, s)
    scale = HEAD_DIM ** -0.5

    def body(q_ref, k_ref, v_ref, o_ref, ltot_ref, stop_ref):
        i = pl.program_id(1)
        hm = _lane_masks()
        q = q_ref[...] * scale
        qh = [jnp.where(mk, q, 0.0).astype(BF16) for mk in hm]
        r, c = _tri_iotas(t)
        later = (r > c).astype(BF16)
        q2 = jnp.concatenate(qh, axis=0)
        causal2 = jnp.concatenate([c < r, c < r], axis=0)

        def scores(j):
            return _dot_nt(q2, k_ref[_rows(j, t), :].astype(BF16))

        def chunk(j, carry, z, masked):
            e_run, acc = carry
            vb = v_ref[_rows(j, t), :].astype(BF16)
            l, _ = _neg_softplus(z)
            if masked:
                l = jnp.where(causal2, l, 0.0)
            between = _ones_dot(l, later) + e_run
            a = jnp.exp(z + l + between)
            if masked:
                a = jnp.where(causal2, a, 0.0)
            return e_run + jnp.sum(l, axis=1, keepdims=True), acc + _dot(a.astype(BF16), vb)

        init = (jnp.zeros((2 * t, 1), F32), jnp.zeros((2 * t, LANES), F32))
        carry = chunk(i, init, scores(i), True)

        def step(st):
            j, cr, z = st
            z_next = scores(jnp.maximum(j - 1, 0))
            return j - 1, chunk(j, cr, z, False), z_next

        j_stop, (e_tot, acc), _ = lax.while_loop(
            lambda st: (st[0] >= 0) & (jnp.max(st[1][0]) > -SKIP_LOG), step,
            (i - 1, carry, scores(jnp.maximum(i - 1, 0))))
        o_ref[...] = jnp.where(hm[0], acc[:t], acc[t:])
        ltot_ref[...] = jnp.where(hm[0], e_tot[:t], e_tot[t:])
        stop_ref[...] = jnp.full(stop_ref.shape, j_stop.astype(F32), F32)

    blk = lambda cb: pl.BlockSpec((t, LANES), lambda p, i: (i, cb + p))
    full = lambda cb: pl.BlockSpec((s, LANES), lambda p, i: (0, cb + p))
    out_blk = pl.BlockSpec((t, LANES), lambda p, i: (i, p))
    n_pairs = SB_W // LANES
    return _pcall(body, name=name, grid=(n_pairs, s // t),
                  in_specs=[blk(CB_QA), full(CB_KA), full(CB_VA)],
                  out_specs=[out_blk, out_blk,
                             pl.BlockSpec((1, 1, 8, LANES), lambda p, i: (p, i, 0, 0))],
                  out_shape=[jax.ShapeDtypeStruct((s, SB_W), F32)] * 2
                  + [jax.ShapeDtypeStruct((n_pairs, s // t, 8, LANES), F32)],
                  semantics=("parallel", "arbitrary"))(proj, proj, proj)


def _sb_bwd(proj, dmixed, ltot, stop, *, name):
    s = proj.shape[0]
    t = min(ATT_T, s)
    scale = HEAD_DIM ** -0.5

    def body(q_ref, k_ref, v_ref, do_ref, ltot_ref, stop_ref, dq_ref, dk_ref, dv_ref):
        i = pl.program_id(1)

        @pl.when(i == 0)
        def _():
            dk_ref[...] = jnp.zeros_like(dk_ref)
            dv_ref[...] = jnp.zeros_like(dv_ref)

        hm = _lane_masks()
        q = q_ref[...] * scale
        do = do_ref[...]
        qh = [jnp.where(mk, q, 0.0).astype(BF16) for mk in hm]
        doh = [jnp.where(mk, do, 0.0).astype(BF16) for mk in hm]
        r, c = _tri_iotas(t)
        upto = (r <= c).astype(BF16)
        before = (r < c).astype(BF16)
        q2 = jnp.concatenate(qh, axis=0)
        do2 = jnp.concatenate(doh, axis=0)
        causal2 = jnp.concatenate([c < r, c < r], axis=0)

        j_stop = jnp.clip(jnp.max(stop_ref[...]).astype(jnp.int32), -1, i - 1)
        ltv = ltot_ref[...]
        lt = jnp.concatenate([ltv[:, 0:1], ltv[:, HEAD_DIM:HEAD_DIM + 1]], axis=0)

        def products(j):
            return (_dot_nt(q2, k_ref[_rows(j, t), :].astype(BF16)),
                    _dot_nt(do2, v_ref[_rows(j, t), :].astype(BF16)))

        def chunk(j, carry, z, da, masked):
            l_run, g_run, dq = carry
            l, e = _neg_softplus(z)
            beta = jnp.where(z >= 0.0, 1.0, e) / (1.0 + e)
            if masked:
                l = jnp.where(causal2, l, 0.0)
            prefix = _ones_dot(l, upto) + l_run
            a = jnp.exp(z + l + (lt - prefix))
            if masked:
                a = jnp.where(causal2, a, 0.0)
            g = a * da
            g_before = _ones_dot(g, before) + g_run
            dz = g * (1.0 - beta) - beta * g_before
            if masked:
                dz = jnp.where(causal2, dz, 0.0)
            dzb = dz.astype(BF16)
            dk_ref[_rows(j, t), :] += _dot_tn(dzb, q2)
            dv_ref[_rows(j, t), :] += _dot_tn(a.astype(BF16), do2)
            return (l_run + jnp.sum(l, axis=1, keepdims=True),
                    g_run + jnp.sum(g, axis=1, keepdims=True),
                    dq + _dot(dzb, k_ref[_rows(j, t), :].astype(BF16)))

        init = (jnp.zeros((2 * t, 1), F32), jnp.zeros((2 * t, 1), F32),
                jnp.zeros((2 * t, LANES), F32))
        carry = lax.fori_loop(j_stop + 1, i,
                              lambda j, cr: chunk(j, cr, *products(j), False), init)
        dq2 = chunk(i, carry, *products(i), True)[2]
        dq_ref[...] = jnp.where(hm[0], dq2[:t], dq2[t:]) * scale

    blk = lambda cb: pl.BlockSpec((t, LANES), lambda p, i: (i, cb + p))
    full = lambda cb: pl.BlockSpec((s, LANES), lambda p, i: (0, cb + p))
    out_blk = pl.BlockSpec((t, LANES), lambda p, i: (i, p))
    out_full = pl.BlockSpec((s, LANES), lambda p, i: (0, p))
    return _pcall(body, name=name, grid=(SB_W // LANES, s // t),
                  in_specs=[blk(CB_QA), full(CB_KA), full(CB_VA), blk(0), out_blk,
                            pl.BlockSpec((1, 1, 8, LANES), lambda p, i: (p, i, 0, 0))],
                  out_specs=[out_blk, out_full, out_full],
                  out_shape=[jax.ShapeDtypeStruct((s, SB_W), F32)] * 3,
                  semantics=("parallel", "arbitrary"))(proj, proj, proj, dmixed, ltot, stop)


def _group_mean(v, lo):
    s0 = jnp.sum(jnp.where(lo, v, 0.0), axis=1, keepdims=True)
    s1 = jnp.sum(jnp.where(lo, 0.0, v), axis=1, keepdims=True)
    return jnp.where(lo, s0, s1) * (1.0 / HEAD_DIM)


def _fox_prep_fwd(proj, qg, kg, *, name):
    s = proj.shape[0]
    tr = min(512, s)

    def body(q_ref, k_ref, qg_ref, kg_ref, qn_ref, kn_ref, kmax_ref):
        lo = _lane_masks()[0]
        for x_ref, g_ref, o_ref in ((q_ref, qg_ref, qn_ref), (k_ref, kg_ref, kn_ref)):
            x = x_ref[...]
            o_ref[...] = x * lax.rsqrt(_group_mean(x * x, lo) + EPS) * g_ref[...]

        @pl.when(pl.program_id(1) == 0)
        def _():
            kmax_ref[...] = jnp.zeros_like(kmax_ref)
        kn = kn_ref[...]
        norms = jnp.sqrt(_group_mean(kn * kn, lo) * HEAD_DIM)
        kmax_ref[...] = jnp.maximum(kmax_ref[...], jnp.max(norms, axis=0, keepdims=True))

    blk = lambda cb: pl.BlockSpec((tr, LANES), lambda p, i: (i, cb + p))
    vec = pl.BlockSpec((1, LANES), lambda p, i: (0, 0))
    out_blk = pl.BlockSpec((tr, LANES), lambda p, i: (i, p))
    return _pcall(body, name=name, grid=(FOX_W // LANES, s // tr),
                  in_specs=[blk(CB_QB), blk(CB_KB), vec, vec],
                  out_specs=[out_blk, out_blk, pl.BlockSpec((1, LANES), lambda p, i: (0, p))],
                  out_shape=[jax.ShapeDtypeStruct((s, FOX_W), F32)] * 2
                  + [jax.ShapeDtypeStruct((1, FOX_W), F32)],
                  semantics=("parallel", "arbitrary"))(proj, proj, qg, kg)


def _fox_prep_bwd(proj, dqn, dkn, qg, kg, *, name):
    s = proj.shape[0]
    tr = min(512, s)

    def body(q_ref, k_ref, dqn_ref, dkn_ref, qg_ref, kg_ref, dq_ref, dk_ref, dqg_ref, dkg_ref):
        @pl.when((pl.program_id(0) == 0) & (pl.program_id(1) == 0))
        def _():
            dqg_ref[...] = jnp.zeros_like(dqg_ref)
            dkg_ref[...] = jnp.zeros_like(dkg_ref)

        lo = _lane_masks()[0]
        for x_ref, dy_ref, g_ref, dx_ref, dg_ref in ((q_ref, dqn_ref, qg_ref, dq_ref, dqg_ref),
                                                     (k_ref, dkn_ref, kg_ref, dk_ref, dkg_ref)):
            x, dy = x_ref[...], dy_ref[...]
            r = lax.rsqrt(_group_mean(x * x, lo) + EPS)
            xh = x * r
            dxh = dy * g_ref[...]
            dx_ref[...] = r * (dxh - xh * _group_mean(dxh * xh, lo))
            dg_ref[...] += _colsum(dy * xh)

    blk = lambda cb: pl.BlockSpec((tr, LANES), lambda p, i: (i, cb + p))
    vec = pl.BlockSpec((1, LANES), lambda p, i: (0, 0))
    out_blk = pl.BlockSpec((tr, LANES), lambda p, i: (i, p))
    return _pcall(body, name=name, grid=(FOX_W // LANES, s // tr),
                  in_specs=[blk(CB_QB), blk(CB_KB), out_blk, out_blk, vec, vec],
                  out_specs=[out_blk, out_blk, vec, vec],
                  out_shape=[jax.ShapeDtypeStruct((s, FOX_W), F32)] * 2
                  + [jax.ShapeDtypeStruct((1, LANES), F32)] * 2,
                  semantics=("arbitrary", "arbitrary"))(proj, proj, dqn, dkn, qg, kg)


def _split3_dot(tri_bf16, x):
    hi = x.astype(BF16)
    r1 = x - hi.astype(F32)
    mid = r1.astype(BF16)
    lo = (r1 - mid.astype(F32)).astype(BF16)
    return _dot(tri_bf16, hi) + _dot(tri_bf16, mid) + _dot(tri_bf16, lo)


def _forget_cumsum_fwd(proj, b_pad, *, name):
    s = proj.shape[0]
    tb = min(256, s)

    def body(fl_ref, b_ref, cf_ref, run_ref):
        @pl.when(pl.program_id(0) == 0)
        def _():
            run_ref[...] = jnp.zeros_like(run_ref)
        lf, _ = _neg_softplus(-(fl_ref[...] + b_ref[...]))
        r, c = _tri_iotas(tb)
        incl = _split3_dot((c <= r).astype(BF16), lf) + run_ref[...]
        cf_ref[...] = incl
        run_ref[...] = incl[tb - 1:tb, :]

    return _pcall(body, name=name, grid=(s // tb,),
                  in_specs=[pl.BlockSpec((tb, LANES), lambda i: (i, CB_FL)),
                            pl.BlockSpec((1, LANES), lambda i: (0, 0))],
                  out_specs=pl.BlockSpec((tb, LANES), lambda i: (i, 0)),
                  out_shape=jax.ShapeDtypeStruct((s, LANES), F32),
                  scratch_shapes=[pltpu.VMEM((1, LANES), F32)],
                  semantics=("arbitrary",))(proj, b_pad)


def _forget_cumsum_bwd(proj, b_pad, dcf, *, name):
    s = proj.shape[0]
    tb = min(256, s)
    nb = s // tb

    def body(fl_ref, b_ref, dcf_ref, dfl_ref, db_ref, run_ref):
        @pl.when(pl.program_id(0) == 0)
        def _():
            run_ref[...] = jnp.zeros_like(run_ref)
            db_ref[...] = jnp.zeros_like(db_ref)
        r, c = _tri_iotas(tb)
        dlf = _split3_dot((c >= r).astype(BF16), dcf_ref[...]) + run_ref[...]
        run_ref[...] = dlf[0:1, :]
        xv = fl_ref[...] + b_ref[...]
        e = jnp.exp(-jnp.abs(xv))
        sig_neg = jnp.where(xv >= 0.0, e, 1.0) / (1.0 + e)
        dfl = dlf * sig_neg
        dfl_ref[...] = dfl
        db_ref[...] += _colsum(dfl)

    return _pcall(body, name=name, grid=(nb,),
                  in_specs=[pl.BlockSpec((tb, LANES), lambda i: (nb - 1 - i, CB_FL)),
                            pl.BlockSpec((1, LANES), lambda i: (0, 0)),
                            pl.BlockSpec((tb, LANES), lambda i: (nb - 1 - i, 0))],
                  out_specs=[pl.BlockSpec((tb, LANES), lambda i: (nb - 1 - i, 0)),
                             pl.BlockSpec((1, LANES), lambda i: (0, 0))],
                  out_shape=[jax.ShapeDtypeStruct((s, LANES), F32),
                             jax.ShapeDtypeStruct((1, LANES), F32)],
                  scratch_shapes=[pltpu.VMEM((1, LANES), F32)],
                  semantics=("arbitrary",))(proj, b_pad, dcf)


def _fox_bias_q(cfc, p, h):
    lane = lax.broadcasted_iota(jnp.int32, (1, LANES), 1)
    return jnp.sum(jnp.where(lane == 2 * p + h, cfc, 0.0), axis=1, keepdims=True)


def _fox_score_bound(q, kmax_row, hm):
    out = []
    for h in range(2):
        qnorm = jnp.sqrt(jnp.sum(jnp.where(hm[h], q * q, 0.0), axis=1, keepdims=True))
        out.append(1.02 * qnorm * kmax_row[:, h * HEAD_DIM:h * HEAD_DIM + 1])
    return out


def _fox_live(cfr_ref, j, t, tops):
    jc = jnp.maximum(j, 0)
    worst = []
    for h in range(2):
        cf_min = jnp.min(cfr_ref[0, pl.ds(h, 1), _rows(jc, t)], axis=1, keepdims=True)
        worst.append(jnp.max(tops[h] - cf_min))
    return (j >= 0) & (jnp.maximum(worst[0], worst[1]) > -SKIP_LOG)


def _fox_fwd(proj, qn, kn, cf, cf_rows, kmax, *, name):
    s = proj.shape[0]
    t = min(ATT_T, s)
    scale = HEAD_DIM ** -0.5

    def body(q_ref, k_ref, v_ref, cfc_ref, cfr_ref, kmax_ref, o_ref, lse_ref):
        p, i = pl.program_id(0), pl.program_id(1)
        hm = _lane_masks()
        q = q_ref[...] * scale
        qh = [jnp.where(mk, q, 0.0).astype(BF16) for mk in hm]
        cfc = cfc_ref[...]
        bq = [_fox_bias_q(cfc, p, h) for h in range(2)]
        qk_top = _fox_score_bound(q, kmax_ref[...], hm)
        r, c = _tri_iotas(t)
        causal = c <= r

        q2 = jnp.concatenate(qh, axis=0)
        causal2 = jnp.concatenate([causal, causal], axis=0)

        def scores(j):
            return _dot_nt(q2, k_ref[_rows(j, t), :].astype(BF16))

        def chunk(j, carry, z2, masked):
            m_run, l_run, acc = carry
            vb = v_ref[_rows(j, t), :].astype(BF16)
            z = jnp.concatenate(
                [z2[h * t:(h + 1) * t] + (bq[h] - cfr_ref[0, pl.ds(h, 1), _rows(j, t)])
                 for h in range(2)], axis=0)
            if masked:
                z = jnp.where(causal2, z, -1e30)
            m_new = jnp.maximum(m_run, jnp.max(z, axis=1, keepdims=True))
            alpha = jnp.exp(m_run - m_new)
            pr = jnp.exp(z - m_new)
            return (m_new, alpha * l_run + jnp.sum(pr, axis=1, keepdims=True),
                    alpha * acc + _dot(pr.astype(BF16), vb))

        init = (jnp.full((2 * t, 1), -1e30, F32), jnp.zeros((2 * t, 1), F32),
                jnp.zeros((2 * t, LANES), F32))
        carry = chunk(i, init, scores(i), True)

        def live(j, cr):
            return _fox_live(cfr_ref, j, t,
                             [qk_top[h] + bq[h] - cr[0][h * t:(h + 1) * t] for h in range(2)])

        def step(st):
            j, _, cr, z2 = st
            z2_next = scores(jnp.maximum(j - 1, 0))
            cr = chunk(j, cr, z2, False)
            return j - 1, live(j - 1, cr), cr, z2_next

        m_fin, l_fin, acc = lax.while_loop(
            lambda st: st[1], step,
            (i - 1, live(i - 1, carry), carry, scores(jnp.maximum(i - 1, 0))))[2]
        o2 = acc / l_fin
        lse2 = m_fin + jnp.log(l_fin)
        o_ref[...] = jnp.where(hm[0], o2[:t], o2[t:])
        lse_ref[...] = jnp.where(hm[0], lse2[:t], lse2[t:])

    blk = pl.BlockSpec((t, LANES), lambda p, i: (i, p))
    full = pl.BlockSpec((s, LANES), lambda p, i: (0, p))
    return _pcall(body, name=name, grid=(FOX_W // LANES, s // t),
                  in_specs=[blk, full, pl.BlockSpec((s, LANES), lambda p, i: (0, CB_VB + p)),
                            pl.BlockSpec((t, LANES), lambda p, i: (i, 0)),
                            pl.BlockSpec((1, 2, s), lambda p, i: (p, 0, 0)),
                            pl.BlockSpec((1, LANES), lambda p, i: (0, p))],
                  out_specs=[blk, blk],
                  out_shape=[jax.ShapeDtypeStruct((s, FOX_W), F32)] * 2,
                  semantics=("parallel", "arbitrary"))(qn, kn, proj, cf, cf_rows, kmax)


def _fox_bwd(proj, qn, kn, cf, cf_rows, kmax, do, o, lse, *, name):
    s = proj.shape[0]
    t = min(ATT_T, s)
    scale = HEAD_DIM ** -0.5

    def body(q_ref, k_ref, v_ref, cfc_ref, cfr_ref, kmax_ref, do_ref, o_ref, lse_ref,
             dq_ref, dk_ref, dv_ref, dcf_ref, dcfq_ref):
        p, i = pl.program_id(0), pl.program_id(1)

        @pl.when(i == 0)
        def _():
            dk_ref[...] = jnp.zeros_like(dk_ref)
            dv_ref[...] = jnp.zeros_like(dv_ref)
            dcf_ref[...] = jnp.zeros_like(dcf_ref)

        hm = _lane_masks()
        q = q_ref[...] * scale
        do = do_ref[...]
        dov = do * o_ref[...]
        qh = [jnp.where(mk, q, 0.0).astype(BF16) for mk in hm]
        doh = [jnp.where(mk, do, 0.0).astype(BF16) for mk in hm]
        delta = [jnp.sum(jnp.where(mk, dov, 0.0), axis=1, keepdims=True) for mk in hm]
        lsev = lse_ref[...]
        lse = [lsev[:, 0:1], lsev[:, HEAD_DIM:HEAD_DIM + 1]]
        cfc = cfc_ref[...]
        bq = [_fox_bias_q(cfc, p, h) - lse[h] for h in range(2)]
        qk_top = _fox_score_bound(q, kmax_ref[...], hm)
        tops = [qk_top[h] + bq[h] for h in range(2)]
        r, c = _tri_iotas(t)
        j_stop = lax.while_loop(lambda st: st[1],
                                lambda st: (st[0] - 1, _fox_live(cfr_ref, st[0] - 1, t, tops)),
                                (i - 1, _fox_live(cfr_ref, i - 1, t, tops)))[0]
        q2 = jnp.concatenate(qh, axis=0)
        do2 = jnp.concatenate(doh, axis=0)
        delta2 = jnp.concatenate(delta, axis=0)
        causal2 = jnp.concatenate([c <= r, c <= r], axis=0)

        def products(j):
            return (_dot_nt(q2, k_ref[_rows(j, t), :].astype(BF16)),
                    _dot_nt(do2, v_ref[_rows(j, t), :].astype(BF16)))

        def chunk(j, carry, z2, dp, masked):
            dq, row_sum = carry
            z = jnp.concatenate(
                [z2[h * t:(h + 1) * t] + (bq[h] - cfr_ref[0, pl.ds(h, 1), _rows(j, t)])
                 for h in range(2)], axis=0)
            pr = jnp.exp(z)
            if masked:
                pr = jnp.where(causal2, pr, 0.0)
            ds = pr * (dp - delta2)
            dsb = ds.astype(BF16)
            dk_ref[_rows(j, t), :] += _dot_tn(dsb, q2)
            dv_ref[_rows(j, t), :] += _dot_tn(pr.astype(BF16), do2)
            for h in range(2):
                dcf_ref[0, pl.ds(h, 1), _rows(j, t)] -= jnp.sum(ds[h * t:(h + 1) * t], axis=0,
                                                               keepdims=True)
            return (dq + _dot(dsb, k_ref[_rows(j, t), :].astype(BF16)),
                    row_sum + jnp.sum(ds, axis=1, keepdims=True))

        init = (jnp.zeros((2 * t, LANES), F32), jnp.zeros((2 * t, 1), F32))
        carry = lax.fori_loop(j_stop + 1, i,
                              lambda j, cr: chunk(j, cr, *products(j), False), init)
        dq2, row_sum = chunk(i, carry, *products(i), True)
        dq_ref[...] = jnp.where(hm[0], dq2[:t], dq2[t:]) * scale
        dcfq_ref[...] = jnp.where(hm[0], row_sum[:t], row_sum[t:])

    blk = pl.BlockSpec((t, LANES), lambda p, i: (i, p))
    full = pl.BlockSpec((s, LANES), lambda p, i: (0, p))
    rows = pl.BlockSpec((1, 2, s), lambda p, i: (p, 0, 0))
    return _pcall(body, name=name, grid=(FOX_W // LANES, s // t),
                  in_specs=[blk, full, pl.BlockSpec((s, LANES), lambda p, i: (0, CB_VB + p)),
                            pl.BlockSpec((t, LANES), lambda p, i: (i, 0)), rows,
                            pl.BlockSpec((1, LANES), lambda p, i: (0, p)),
                            pl.BlockSpec((t, LANES), lambda p, i: (i, SB_W // LANES + p)),
                            blk, blk],
                  out_specs=[blk, full, full, rows, blk],
                  out_shape=[jax.ShapeDtypeStruct((s, FOX_W), F32)] * 3
                  + [jax.ShapeDtypeStruct((FOX_W // LANES, 2, s), F32),
                     jax.ShapeDtypeStruct((s, FOX_W), F32)],
                  semantics=("parallel", "arbitrary"))(qn, kn, proj, cf, cf_rows, kmax, do, o, lse)


_GELU_C0 = math.sqrt(2.0 / math.pi)
_GELU_C1 = 0.044715


def _gelu(x):
    th = jnp.tanh(_GELU_C0 * (x + _GELU_C1 * (x * x * x)))
    return 0.5 * x * (1.0 + th), th


def _gelu_grad(x, th):
    return 0.5 * (1.0 + th) + 0.5 * x * (1.0 - th * th) * (_GELU_C0 * (1.0 + 3.0 * _GELU_C1 * x * x))


def _sgu_mix(wm, vn_c, lo, bcol):
    return jnp.where(lo, _dot(wm[0], vn_c) + bcol[0], _dot(wm[1], vn_c) + bcol[1])


def _sgu_fwd(proj, w, b_cols, gn, *, name):
    s = proj.shape[0]
    tr = min(512, s)
    ch = SGU_CHUNK

    def body(u_ref, v_ref, w_ref, b_ref, gn_ref, o_ref):
        lo = _lane_masks()[0]
        r, c = _tri_iotas(ch)
        wm = [jnp.where(c <= r, w_ref[h], 0.0).astype(BF16) for h in range(2)]
        bcol = [b_ref[0, :, h:h + 1] for h in range(2)]
        for n in range(tr // ch):
            rows = slice(n * ch, (n + 1) * ch)
            u, _ = _gelu(u_ref[rows, :])
            vg, _ = _gelu(v_ref[rows, :])
            vn = vg * lax.rsqrt(_group_mean(vg * vg, lo) + EPS) * gn_ref[0]
            o_ref[rows, :] = u * _sgu_mix(wm, vn.astype(BF16), lo, bcol)

    blk = lambda cb: pl.BlockSpec((tr, LANES), lambda p, i: (i, cb + p))
    return _pcall(body, name=name, grid=(SGU_W // LANES, s // tr),
                  in_specs=[blk(CB_UC), blk(CB_VC),
                            pl.BlockSpec((2, ch, ch), lambda p, i: (p, 0, 0)),
                            pl.BlockSpec((1, ch, 2), lambda p, i: (p, 0, 0)),
                            pl.BlockSpec((1, 1, LANES), lambda p, i: (p, 0, 0))],
                  out_specs=pl.BlockSpec((tr, LANES), lambda p, i: (i, p)),
                  out_shape=jax.ShapeDtypeStruct((s, SGU_W), F32),
                  semantics=("parallel", "parallel"))(proj, proj, w, b_cols, gn)


def _sgu_bwd(proj, dmixed, w, w_t, b_cols, gn, *, name):
    s = proj.shape[0]
    tr = min(512, s)
    ch = SGU_CHUNK
    cb_do = (SB_W + FOX_W) // LANES

    def body(u_ref, v_ref, do_ref, w_ref, wt_ref, b_ref, gn_ref,
             du_ref, dv_ref, dw_ref, db_ref, dgn_ref):
        @pl.when(pl.program_id(1) == 0)
        def _():
            dw_ref[...] = jnp.zeros_like(dw_ref)
            db_ref[...] = jnp.zeros_like(db_ref)
            dgn_ref[...] = jnp.zeros_like(dgn_ref)

        hm = _lane_masks()
        lo = hm[0]
        r, c = _tri_iotas(ch)
        wm = [jnp.where(c <= r, w_ref[h], 0.0).astype(BF16) for h in range(2)]
        wtm = [jnp.where(r <= c, wt_ref[h], 0.0).astype(BF16) for h in range(2)]
        bcol = [b_ref[0, :, h:h + 1] for h in range(2)]
        gnv = gn_ref[0]
        for n in range(tr // ch):
            rows = slice(n * ch, (n + 1) * ch)
            uc, vc, do = u_ref[rows, :], v_ref[rows, :], do_ref[rows, :]
            u, thu = _gelu(uc)
            vg, thv = _gelu(vc)
            rinv = lax.rsqrt(_group_mean(vg * vg, lo) + EPS)
            xh = vg * rinv
            vnb = (xh * gnv).astype(BF16)
            mix = _sgu_mix(wm, vnb, lo, bcol)
            du_ref[rows, :] = do * mix * _gelu_grad(uc, thu)
            dm = do * u
            dmb = dm.astype(BF16)
            dvn = jnp.where(lo, _dot(wtm[0], dmb), _dot(wtm[1], dmb))
            for h in range(2):
                dmh = jnp.where(hm[h], dm, 0.0)
                dw_ref[h] += jnp.where(c <= r, _dot_nt(dmh.astype(BF16), vnb), 0.0)
                db_ref[0, :, h:h + 1] += jnp.sum(dmh, axis=1, keepdims=True)
            dgn_ref[0] += _colsum(dvn * xh)
            dxh = dvn * gnv
            dvg = rinv * (dxh - xh * _group_mean(dxh * xh, lo))
            dv_ref[rows, :] = dvg * _gelu_grad(vc, thv)

    blk = lambda cb: pl.BlockSpec((tr, LANES), lambda p, i: (i, cb + p))
    w_spec = pl.BlockSpec((2, ch, ch), lambda p, i: (p, 0, 0))
    b_spec = pl.BlockSpec((1, ch, 2), lambda p, i: (p, 0, 0))
    g_spec = pl.BlockSpec((1, 1, LANES), lambda p, i: (p, 0, 0))
    out_blk = pl.BlockSpec((tr, LANES), lambda p, i: (i, p))
    return _pcall(body, name=name, grid=(SGU_W // LANES, s // tr),
                  in_specs=[blk(CB_UC), blk(CB_VC), blk(cb_do), w_spec, w_spec, b_spec, g_spec],
                  out_specs=[out_blk, out_blk, w_spec, b_spec, g_spec],
                  out_shape=[jax.ShapeDtypeStruct((s, SGU_W), F32)] * 2
                  + [jax.ShapeDtypeStruct(w.shape, F32), jax.ShapeDtypeStruct(b_cols.shape, F32),
                     jax.ShapeDtypeStruct(gn.shape, F32)],
                  semantics=("parallel", "arbitrary"))(proj, proj, dmixed, w, w_t, b_cols, gn)


def _pad_lanes(v):
    return jnp.zeros((1, LANES), F32).at[0, :v.shape[0]].set(v)


def _small_views(sm):
    return dict(
        n1=sm["norm1_g"][None, :], n2=sm["norm2_g"][None, :],
        b_pad=_pad_lanes(sm["b_forget"]),
        qg=jnp.tile(sm["q_norm_g"], 2)[None, :], kg=jnp.tile(sm["k_norm_g"], 2)[None, :],
        gn=sm["sgu_norm_g"].reshape(2, 1, LANES),
        w=sm["sgu_w"], w_t=jnp.swapaxes(sm["sgu_w"], 1, 2),
        b_cols=sm["sgu_b"].reshape(2, 2, SGU_CHUNK).transpose(0, 2, 1))


def _cf_rows(cf):
    return cf[:, :FOX_HEADS].T.reshape(FOX_W // LANES, 2, cf.shape[0])


def _layer_fwd(x_in, prev, mod, wts, sm, l):
    sh1, sc1, g1, sh2, sc2, g2 = mod
    v = _small_views(sm)
    if prev is None:
        x0 = x_in
        h1 = _norm_mod_fwd(x0, v["n1"], sc1, sh1, name=f"l{l}_norm1")
    else:
        x0, h1 = _resid_norm_mod_fwd(x_in, prev[0], prev[1], v["n1"], sc1, sh1, name=f"l{l}_norm1")
    proj = _matmul(h1, wts["w_in"], name=f"l{l}_proj")
    o_sb, sb_ltot, sb_stop = _sb_fwd(proj, name=f"l{l}_sb_fwd")
    qn, kn, kmax = _fox_prep_fwd(proj, v["qg"], v["kg"], name=f"l{l}_fox_prep")
    cf = _forget_cumsum_fwd(proj, v["b_pad"], name=f"l{l}_cumf")
    cfr = _cf_rows(cf)
    o_fox, lse = _fox_fwd(proj, qn, kn, cf, cfr, kmax, name=f"l{l}_fox_fwd")
    o_sgu = _sgu_fwd(proj, v["w"], v["b_cols"], v["gn"], name=f"l{l}_sgu_fwd")
    mixed = jnp.concatenate([o_sb, o_fox, o_sgu], axis=1).astype(BF16)
    mo = _matmul(mixed, wts["w_out"], name=f"l{l}_wout")
    x1, h2 = _resid_norm_mod_fwd(x0, mo, g1, v["n2"], sc2, sh2, name=f"l{l}_norm2")
    a, rr = _matmul(h2, wts["w1"], name=f"l{l}_mlp1", out_dtype=BF16, relu2=BF16)
    m2 = _matmul(rr, wts["w2"], name=f"l{l}_mlp2")
    saved = dict(x0=x0, h1=h1, proj=proj, sb_ltot=sb_ltot, sb_stop=sb_stop, qn=qn, kn=kn, kmax=kmax, cf=cf, cfr=cfr, o_fox=o_fox,
                 lse=lse, mixed=mixed, mo=mo, x1=x1, h2=h2, a=a, rr=rr, m2=m2)
    return saved


def _layer_bwd(dx2, dm2, dg2, sv, mod, wts, sm, l, below):
    sh1, sc1, g1, sh2, sc2, g2 = mod
    v = _small_views(sm)
    dw2 = _matmul(sv["rr"], dm2, ta=True, name=f"l{l}_dw2")
    da = _matmul(dm2, wts["w2"], tb=True, name=f"l{l}_da", out_dtype=BF16, pre_act=sv["a"])
    dw1 = _matmul(sv["h2"], da, ta=True, name=f"l{l}_dw1")
    dh2 = _matmul(da, wts["w1"], tb=True, name=f"l{l}_dh2")
    dx1, dn2, dsc2, dsh2, dmo, dg1 = _norm_mod_bwd(sv["x1"], dh2, dx2, v["n2"], sc2,
                                                    (sv["mo"], g1), name=f"l{l}_norm2_bwd")
    dwo = _matmul(sv["mixed"], dmo, ta=True, name=f"l{l}_dwout")
    dmixed = _matmul(dmo, wts["w_out"], tb=True, name=f"l{l}_dmixed")
    proj = sv["proj"]
    dqa, dka, dva = _sb_bwd(proj, dmixed, sv["sb_ltot"], sv["sb_stop"], name=f"l{l}_sb_bwd")
    dqn, dkn, dvb, dcfr, dcfq = _fox_bwd(proj, sv["qn"], sv["kn"], sv["cf"], sv["cfr"], sv["kmax"], dmixed,
                                   sv["o_fox"], sv["lse"], name=f"l{l}_fox_bwd")
    dqb, dkb, dqg, dkg = _fox_prep_bwd(proj, dqn, dkn, v["qg"], v["kg"], name=f"l{l}_fox_prep_bwd")
    s = proj.shape[0]
    dcf_heads = dcfr.reshape(FOX_HEADS, s).T + dcfq.reshape(s, FOX_HEADS, HEAD_DIM)[:, :, 0]
    dcf = jnp.zeros((s, LANES), F32).at[:, :FOX_HEADS].set(dcf_heads)
    dfl, dbf = _forget_cumsum_bwd(proj, v["b_pad"], dcf, name=f"l{l}_cumf_bwd")
    duc, dvc, dsw, dsb_cols, dgn = _sgu_bwd(proj, dmixed, v["w"], v["w_t"], v["b_cols"], v["gn"],
                                            name=f"l{l}_sgu_bwd")
    dproj = jnp.concatenate([dqa, dka, dva, dqb, dkb, dvb, duc, dvc, dfl,
                             jnp.zeros((s, LANES), F32)], axis=1).astype(BF16)
    dwin = _matmul(sv["h1"], dproj, ta=True, name=f"l{l}_dwin")
    dh1 = _matmul(dproj, wts["w_in"], tb=True, name=f"l{l}_dh1")
    dx0, dn1, dsc1, dsh1, dm_below, dg_below = _norm_mod_bwd(sv["x0"], dh1, dx1, v["n1"], sc1, below,
                                                             name=f"l{l}_norm1_bwd")
    big = dict(w_in=dwin, w_out=dwo, w1=dw1, w2=dw2)
    small = dict(norm1_g=dn1[0], norm2_g=dn2[0], b_forget=dbf[0, :FOX_HEADS],
                 q_norm_g=dqg[0, :HEAD_DIM] + dqg[0, HEAD_DIM:],
                 k_norm_g=dkg[0, :HEAD_DIM] + dkg[0, HEAD_DIM:],
                 sgu_norm_g=dgn.reshape(4, HEAD_DIM), sgu_w=dsw,
                 sgu_b=dsb_cols.transpose(0, 2, 1).reshape(4, SGU_CHUNK))
    dmod = jnp.concatenate([dsh1, dsc1, dg1, dsh2, dsc2, dg2], axis=1)
    return dx0, dm_below, dg_below, big, small, dmod


def _w_in_to_internal(w):
    pad = jnp.zeros((w.shape[0], PROJ_W - IN_W), w.dtype)
    return jnp.concatenate([w[:, :ATT_W], w[:, ATT_W + FOX_HEADS:], w[:, ATT_W:ATT_W + FOX_HEADS],
                            pad], axis=1)


def _w_in_from_internal(g):
    n_gate = SGU_W * 2
    return jnp.concatenate([g[:, :ATT_W], g[:, ATT_W + n_gate:ATT_W + n_gate + FOX_HEADS],
                            g[:, ATT_W:ATT_W + n_gate]], axis=1)


def _exchange(x, masks, slot_shift, slot_bits, scatter, *, name):
    n_slots = 2 ** slot_bits
    blk_shape = x.shape[1:] if scatter else x.shape
    n_peers = len(masks)

    def body(x_ref, out_ref, send_sems, recv_sems, local_sem):
        ids = (lax.axis_index("x"), lax.axis_index("y"), lax.axis_index("c"))
        me = 4 * ids[0] + 2 * ids[1] + ids[2]
        my_slot = (me >> slot_shift) & (n_slots - 1)

        def peer(mask):
            return tuple(1 - v if (mask >> b) & 1 else v for v, b in zip(ids, (2, 1, 0)))

        def src_for(slot):
            return x_ref.at[slot] if scatter else x_ref

        copies = [pltpu.make_async_copy(src_for(my_slot), out_ref.at[my_slot], local_sem)]
        for kk, mask in enumerate(masks):
            peer_slot = ((me ^ mask) >> slot_shift) & (n_slots - 1)
            copies.append(pltpu.make_async_remote_copy(
                src_ref=src_for(peer_slot), dst_ref=out_ref.at[my_slot],
                send_sem=send_sems.at[kk], recv_sem=recv_sems.at[kk],
                device_id=peer(mask), device_id_type=MESH))
        for cp in copies:
            cp.start()
        for cp in copies:
            cp.wait()

    any_spec = pl.BlockSpec(memory_space=pl.ANY)
    return _pcall(body, name=name, in_specs=[any_spec], out_specs=any_spec,
                  out_shape=jax.ShapeDtypeStruct((n_slots,) + tuple(blk_shape), x.dtype),
                  scratch_shapes=[pltpu.SemaphoreType.DMA((n_peers,)),
                                  pltpu.SemaphoreType.DMA((n_peers,)),
                                  pltpu.SemaphoreType.DMA(())])(x)


CORE_PIECE_BYTES = 12 * 2 ** 20
CORE_DMA_CHUNKS = 4


def _core_swap_piece(x, *, name):
    rows, cols = x.shape
    n_ch = CORE_DMA_CHUNKS if rows % (16 * CORE_DMA_CHUNKS) == 0 else 1
    rc = rows // n_ch

    def body(x_ref, out_ref, send_sems, recv_sems):
        sibling = (lax.axis_index("x"), lax.axis_index("y"), 1 - lax.axis_index("c"))
        copies = [pltpu.make_async_remote_copy(
            src_ref=x_ref.at[pl.ds(ch * rc, rc)], dst_ref=out_ref.at[pl.ds(ch * rc, rc)],
            send_sem=send_sems.at[ch], recv_sem=recv_sems.at[ch],
            device_id=sibling, device_id_type=MESH) for ch in range(n_ch)]
        for cp in copies:
            cp.start()
        for cp in copies:
            cp.wait()

    vmem = pl.BlockSpec(memory_space=pltpu.VMEM)
    return _pcall(body, name=name, in_specs=[vmem], out_specs=vmem,
                  out_shape=jax.ShapeDtypeStruct(x.shape, x.dtype),
                  scratch_shapes=[pltpu.SemaphoreType.DMA((n_ch,)),
                                  pltpu.SemaphoreType.DMA((n_ch,))])(x)


def _core_swap(x, *, name):
    rows, cols = x.shape
    n = 1
    while (rows % n or (rows // n) % 16 or
           (rows // n) * (-(-cols // LANES) * LANES) * x.dtype.itemsize > CORE_PIECE_BYTES):
        n += 1
    pr = rows // n
    pieces = [_core_swap_piece(x[kk * pr:(kk + 1) * pr], name=f"{name}_{kk}") for kk in range(n)]
    return pieces[0] if n == 1 else jnp.concatenate(pieces, axis=0)


def _by_core(core, mine, theirs, axis):
    return jnp.where(core == 0, jnp.concatenate([mine, theirs], axis=axis),
                     jnp.concatenate([theirs, mine], axis=axis))


def _gather_chips(x, *, name):
    return _exchange(x, (2, 4, 6), 1, 2, False, name=name)


def _gather_all(x, *, name):
    return _exchange(x, (1, 2, 3, 4, 5, 6, 7), 0, 3, False, name=name)


def _scatter_chips(x4, *, name):
    return _exchange(x4, (2, 4, 6), 1, 2, True, name=name)


def _sum_slots(parts, *, name, out_dtype=F32, tr=256):
    n, rows, cols = parts.shape
    tr = min(tr, rows)
    assert rows % tr == 0, (name, rows, tr)

    def body(p_ref, o_ref):
        acc = p_ref[0].astype(F32)
        for kk in range(1, n):
            acc = acc + p_ref[kk].astype(F32)
        o_ref[...] = acc.astype(o_ref.dtype)

    return _pcall(body, name=name, grid=(rows // tr,),
                  in_specs=[pl.BlockSpec((n, tr, cols), lambda i: (0, i, 0))],
                  out_specs=pl.BlockSpec((tr, cols), lambda i: (i, 0)),
                  out_shape=jax.ShapeDtypeStruct((rows, cols), out_dtype),
                  semantics=("parallel",))(parts)


def _add2(a, b, *, name, out_dtype, tr=512):
    def fn(f, v):
        return [f[0] + f[1]], []
    (out,), _ = _rowwise(fn, [a, b], [], [out_dtype], 0, name=name, tr=tr)
    return out


def _adamw(w, m, v, parts, *, name, tr=256):
    n, rows, cols = parts.shape
    tr = min(tr, rows)
    assert rows % tr == 0, (name, rows, tr)
    c1 = 1.0 - ADAM_B1 ** ADAM_STEP
    c2 = 1.0 - ADAM_B2 ** ADAM_STEP

    def body(w_ref, m_ref, v_ref, p_ref, g_ref, d_ref, nm_ref, nv_ref):
        g = p_ref[0]
        for kk in range(1, n):
            g = g + p_ref[kk]
        nm = ADAM_B1 * m_ref[...] + (1.0 - ADAM_B1) * g
        nv = ADAM_B2 * v_ref[...] + (1.0 - ADAM_B2) * (g * g)
        g_ref[...] = g
        nm_ref[...] = nm
        nv_ref[...] = nv
        d_ref[...] = -ADAM_LR * ((nm / c1) / (jnp.sqrt(nv / c2) + ADAM_EPS) + ADAM_WD * w_ref[...])

    spec = pl.BlockSpec((tr, cols), lambda i: (i, 0))
    return _pcall(body, name=name, grid=(rows // tr,),
                  in_specs=[spec, spec, spec, pl.BlockSpec((n, tr, cols), lambda i: (0, i, 0))],
                  out_specs=[spec] * 4,
                  out_shape=[jax.ShapeDtypeStruct((rows, cols), F32)] * 4,
                  semantics=("parallel",))(w, m, v, parts)


def _silu(c):
    return c / (1.0 + jnp.exp(-c))


def _ada_fwd(c_all, ada_w, ada_b_sh, *, name):
    nl, d, wsh = ada_w.shape

    def body(c_ref, w_ref, b_ref, o_ref):
        cond = _silu(c_ref[...]).astype(BF16)
        o_ref[0] = _dot(cond, w_ref[0].astype(BF16)) + b_ref[0]

    return _pcall(body, name=name, grid=(nl,),
                  in_specs=[pl.BlockSpec(c_all.shape, lambda l: (0, 0)),
                            pl.BlockSpec((1, d, wsh), lambda l: (l, 0, 0)),
                            pl.BlockSpec((1, 1, wsh), lambda l: (l, 0, 0))],
                  out_specs=pl.BlockSpec((1, c_all.shape[0], wsh), lambda l: (l, 0, 0)),
                  out_shape=jax.ShapeDtypeStruct((nl, c_all.shape[0], wsh), F32),
                  semantics=("parallel",))(c_all, ada_w, ada_b_sh)


def _ada_bwd(c_all, dmod_sh, *, name):
    nl, nb, wsh = dmod_sh.shape
    d = c_all.shape[1]

    def body(c_ref, dm_ref, o_ref):
        cond = _silu(c_ref[...]).astype(BF16)
        o_ref[0] = _dot_tn(cond, dm_ref[0].astype(BF16))

    return _pcall(body, name=name, grid=(nl,),
                  in_specs=[pl.BlockSpec(c_all.shape, lambda l: (0, 0)),
                            pl.BlockSpec((1, nb, wsh), lambda l: (l, 0, 0))],
                  out_specs=pl.BlockSpec((1, d, wsh), lambda l: (l, 0, 0)),
                  out_shape=jax.ShapeDtypeStruct((nl, d, wsh), F32),
                  semantics=("parallel",))(c_all, dmod_sh)


SMALL_NAMES = ("norm1_g", "norm2_g", "b_forget", "q_norm_g", "k_norm_g", "sgu_norm_g", "sgu_w",
               "sgu_b")
WEIGHT_NAMES = ("ada_w", "ada_b", "norm1_g", "norm2_g", "w_in", "b_forget", "q_norm_g", "k_norm_g",
                "sgu_norm_g", "sgu_w", "sgu_b", "w_out", "mlp_w1", "mlp_w2")


SMALL_TILE_ROWS = 256


def _pack_small(tree):
    flat = jnp.concatenate([tree[n].reshape(-1) for n in SMALL_NAMES])
    n = flat.shape[0]
    rows = -(-n // (SMALL_TILE_ROWS * LANES)) * SMALL_TILE_ROWS
    return jnp.zeros((rows * LANES,), F32).at[:n].set(flat).reshape(rows, LANES)


def _unpack_small(packed, like):
    flat = packed.reshape(-1)
    out, off = {}, 0
    for n in SMALL_NAMES:
        size = like[n].size
        out[n] = flat[off:off + size].reshape(like[n].shape)
        off += size
    return out


def kernel(x, c, ada_w, ada_b, norm1_g, norm2_g, w_in, b_forget, q_norm_g, k_norm_g, sgu_norm_g, sgu_w, sgu_b, w_out, mlp_w1, mlp_w2, loss_target, m_ada_w, m_ada_b, m_norm1_g, m_norm2_g, m_w_in, m_b_forget, m_q_norm_g, m_k_norm_g, m_sgu_norm_g, m_sgu_w, m_sgu_b, m_w_out, m_mlp_w1, m_mlp_w2, v_ada_w, v_ada_b, v_norm1_g, v_norm2_g, v_w_in, v_b_forget, v_q_norm_g, v_k_norm_g, v_sgu_norm_g, v_sgu_w, v_sgu_b, v_w_out, v_mlp_w1, v_mlp_w2):
    w = dict(ada_w=ada_w, ada_b=ada_b, norm1_g=norm1_g, norm2_g=norm2_g, w_in=w_in,
             b_forget=b_forget, q_norm_g=q_norm_g, k_norm_g=k_norm_g, sgu_norm_g=sgu_norm_g,
             sgu_w=sgu_w, sgu_b=sgu_b, w_out=w_out, mlp_w1=mlp_w1, mlp_w2=mlp_w2)
    mom = dict(ada_w=m_ada_w, ada_b=m_ada_b, norm1_g=m_norm1_g, norm2_g=m_norm2_g, w_in=m_w_in,
               b_forget=m_b_forget, q_norm_g=m_q_norm_g, k_norm_g=m_k_norm_g,
               sgu_norm_g=m_sgu_norm_g, sgu_w=m_sgu_w, sgu_b=m_sgu_b, w_out=m_w_out,
               mlp_w1=m_mlp_w1, mlp_w2=m_mlp_w2)
    var = dict(ada_w=v_ada_w, ada_b=v_ada_b, norm1_g=v_norm1_g, norm2_g=v_norm2_g, w_in=v_w_in,
               b_forget=v_b_forget, q_norm_g=v_q_norm_g, k_norm_g=v_k_norm_g,
               sgu_norm_g=v_sgu_norm_g, sgu_w=v_sgu_w, sgu_b=v_sgu_b, w_out=v_w_out,
               mlp_w1=v_mlp_w1, mlp_w2=v_mlp_w2)
    depth, d = norm1_g.shape
    chip = 2 * lax.axis_index("x") + lax.axis_index("y")
    me = 2 * chip + lax.axis_index("c")
    n_chips = 4
    ada_sh = ada_w.shape[2]

    core = lax.axis_index("c")
    half_l = depth // 2

    def gather_weight(w_sh, name):
        _, r, cols = w_sh.shape
        mine = lax.dynamic_slice_in_dim(w_sh, core * half_l, half_l, axis=0).astype(BF16)
        got = _gather_chips(mine.reshape(half_l * r, cols), name=f"gather_{name}")
        theirs = _core_swap(got.reshape(n_chips * half_l * r, cols), name=f"share_{name}")
        return _by_core(core, got.reshape(n_chips, half_l, r, cols),
                        theirs.reshape(n_chips, half_l, r, cols), 1)

    g_in = gather_weight(w_in, "w_in")
    g_out = gather_weight(w_out, "w_out")
    g_w1 = gather_weight(mlp_w1, "w1")
    g_w2 = gather_weight(mlp_w2, "w2")
    layer_w = []
    for l in range(depth):
        layer_w.append(dict(
            w_in=_w_in_to_internal(jnp.concatenate([g_in[k, l] for k in range(n_chips)], axis=1)),
            w_out=g_out[:, l].reshape(d, d),
            w1=jnp.concatenate([g_w1[k, l] for k in range(n_chips)], axis=1),
            w2=g_w2[:, l].reshape(D_FF, d)))

    c_all = _gather_all(jnp.zeros((8, d), F32).at[0].set(c[0]), name="gather_c")[:, 0]
    c_pad = jnp.concatenate([c_all, jnp.zeros_like(c_all)], axis=0)
    ada_b_sh = lax.dynamic_slice_in_dim(ada_b, chip * ada_sh, ada_sh, axis=1)[:, None, :]
    mod_sh = _ada_fwd(c_pad, ada_w, ada_b_sh, name="ada_fwd")
    mod_all = _gather_chips(mod_sh, name="gather_mod")
    mod_me = lax.dynamic_index_in_dim(mod_all, me, axis=2, keepdims=False)
    mod_me = mod_me.transpose(1, 0, 2).reshape(depth, 6, 1, d)

    saved = []
    xs, prev = x[0], None
    for l in range(depth):
        mod = [mod_me[l, kk] for kk in range(6)]
        sm = {n: w[n][l] for n in SMALL_NAMES}
        sv = _layer_fwd(xs, prev, mod, layer_w[l], sm, l)
        saved.append(sv)
        xs, prev = sv["x1"], (sv["m2"], mod[5])

    sq, dxs, dm2, dg2 = _loss_fwd_bwd(xs, prev[0], prev[1], loss_target[0], name="loss")
    loss = lax.psum(0.5 * jnp.sum(sq) / d, ("x", "y", "c"))

    big = {n: [] for n in ("w_in", "w_out", "w1", "w2")}
    small = {n: [] for n in SMALL_NAMES}
    dmods = []
    for l in reversed(range(depth)):
        mod = [mod_me[l, kk] for kk in range(6)]
        sm = {n: w[n][l] for n in SMALL_NAMES}
        below = (saved[l - 1]["m2"], mod_me[l - 1, 5]) if l else None
        dxs, dm2, dg2, bg, smg, dmod = _layer_bwd(dxs, dm2, dg2, saved[l], mod, layer_w[l], sm, l,
                                                  below)
        for n in big:
            big[n].insert(0, bg[n])
        for n in SMALL_NAMES:
            small[n].insert(0, smg[n])
        dmods.insert(0, dmod)
    grad_x = dxs[None]

    out_g, out_d, out_m, out_v = {}, {}, {}, {}

    def run_adamw(name, parts2d, shape):
        rows, cols = parts2d.shape[1:]
        g, dl, nm, nv = _adamw(w[name].reshape(rows, cols), mom[name].reshape(rows, cols),
                               var[name].reshape(rows, cols), parts2d, name=f"adamw_{name}")
        out_g[name], out_d[name] = g.reshape(shape), dl.reshape(shape)
        out_m[name], out_v[name] = nm.reshape(shape), nv.reshape(shape)

    def shards_of(name, l):
        if name == "w_in":
            g = _w_in_from_internal(big["w_in"][l])
            return jnp.stack(jnp.split(g, n_chips, axis=1))
        if name == "mlp_w1":
            return jnp.stack(jnp.split(big["w1"][l], n_chips, axis=1))
        if name == "w_out":
            return big["w_out"][l].reshape(n_chips, d // n_chips, d)
        return big["w2"][l].reshape(n_chips, D_FF // n_chips, d)

    for name in ("w_in", "w_out", "mlp_w1", "mlp_w2"):
        per_chip = jnp.stack([shards_of(name, l) for l in range(depth)], axis=1)
        r, cols = per_chip.shape[2:]
        half_rows = half_l * r
        keep = lax.dynamic_slice_in_dim(per_chip, core * half_l, half_l, axis=1)
        send = lax.dynamic_slice_in_dim(per_chip, (1 - core) * half_l, half_l, axis=1)
        theirs = _core_swap(send.reshape(n_chips * half_rows, cols), name=f"pair_{name}")
        chip_sum = _add2(keep.reshape(n_chips * half_rows, cols), theirs, out_dtype=BF16,
                         name=f"pairsum_{name}")
        got = _scatter_chips(chip_sum.reshape(n_chips, half_rows, cols), name=f"scatter_{name}")
        half = _sum_slots(got, name=f"sum_{name}")
        both = _by_core(core, half, _core_swap(half, name=f"swap_{name}"), 0)
        run_adamw(name, both[None], w[name].shape)

    small_tree = {n: jnp.stack(small[n]) for n in SMALL_NAMES}
    gathered = _gather_all(_pack_small(small_tree), name="gather_small")
    gs, ds_, ms, vs = _adamw(_pack_small({n: w[n] for n in SMALL_NAMES}),
                             _pack_small({n: mom[n] for n in SMALL_NAMES}),
                             _pack_small({n: var[n] for n in SMALL_NAMES}), gathered,
                             name="adamw_small")
    like = {n: w[n] for n in SMALL_NAMES}
    for tree, packed in ((out_g, gs), (out_d, ds_), (out_m, ms), (out_v, vs)):
        tree.update(_unpack_small(packed, like))

    dmod_mine = jnp.concatenate(dmods, axis=0)
    dmod_all = _gather_all(jnp.zeros((depth, 8, 6 * d), F32).at[:, 0].set(dmod_mine),
                           name="gather_dmod")[:, :, 0]
    dmod_lb = dmod_all.transpose(1, 0, 2)
    dmod_sh = lax.dynamic_slice_in_dim(dmod_lb, chip * ada_sh, ada_sh, axis=2)
    dmod_sh = jnp.concatenate([dmod_sh, jnp.zeros_like(dmod_sh)], axis=1)
    g_ada_w = _ada_bwd(c_pad, dmod_sh, name="ada_bwd")
    run_adamw("ada_w", g_ada_w.reshape(1, depth * d, ada_sh), ada_w.shape)
    parts_b = dmod_all.reshape(8, depth * 6 * d // LANES, LANES)
    run_adamw("ada_b", parts_b, ada_b.shape)

    outs = [loss, grad_x]
    for tree in (out_g, out_d, out_m, out_v):
        outs += [tree[n] for n in WEIGHT_NAMES]
    return tuple(outs)
```

```python
import functools
import math

import jax
import jax.numpy as jnp
from jax import lax
from jax.experimental import pallas as pl
from jax.experimental.pallas import tpu as pltpu

F32 = jnp.float32
BF16 = jnp.bfloat16

D_MODEL = 1024
DEPTH = 4
HEAD_DIM = 64
LANES = 128
D_FF = 4 * D_MODEL
EPS = 1e-6
SB_W, FOX_W, SGU_W = 256, 512, 256
FOX_HEADS = 8
SGU_CHUNK = 128
IN_W = 2824
ATT_W = 3 * SB_W + 3 * FOX_W
PROJ_W = 3072
CB_QA, CB_KA, CB_VA = 0, 2, 4
CB_QB, CB_KB, CB_VB = 6, 10, 14
CB_UC, CB_VC, CB_FL = 18, 20, 22
ATT_T = 256
VMEM_LIMIT = 56 * 2 ** 20
SKIP_LOG = 110.0

ADAM_LR, ADAM_B1, ADAM_B2, ADAM_EPS, ADAM_WD, ADAM_STEP = 0.001, 0.9, 0.999, 1e-08, 0.01, 10

MESH = pl.DeviceIdType.MESH


def _pcall(body, *, name, out_shape, grid=(), in_specs=None, out_specs=None, scratch_shapes=(),
           semantics=None):
    params = dict(vmem_limit_bytes=VMEM_LIMIT)
    if semantics is not None:
        params["dimension_semantics"] = semantics
    kwargs = {}
    if in_specs is not None:
        kwargs["in_specs"] = in_specs
    if out_specs is not None:
        kwargs["out_specs"] = out_specs
    return pl.pallas_call(body, name=name, out_shape=out_shape, grid=grid,
                          scratch_shapes=list(scratch_shapes),
                          compiler_params=pltpu.CompilerParams(**params), **kwargs)


def _dot(a, b):
    return jnp.dot(a, b, preferred_element_type=F32)


def _dot_nt(a, b):
    return lax.dot_general(a, b, (((1,), (1,)), ((), ())), preferred_element_type=F32)


def _dot_tn(a, b):
    return lax.dot_general(a, b, (((0,), (0,)), ((), ())), preferred_element_type=F32)


def _split2(x):
    hi = x.astype(BF16)
    lo = (x - hi.astype(F32)).astype(BF16)
    return hi, lo


def _ones_dot(x, ones_bf16):
    hi, lo = _split2(x)
    return _dot(hi, ones_bf16) + _dot(lo, ones_bf16)


def _rowwise(fn, fulls, vecs, out_dtypes, n_vec_out, *, name, tr):
    s, n = fulls[0].shape
    tr = min(tr, s)
    assert s % tr == 0, (name, s, tr)
    nf, nv, nfo = len(fulls), len(vecs), len(out_dtypes)

    def body(*refs):
        fi, vi = refs[:nf], refs[nf:nf + nv]
        fo, vo = refs[nf + nv:nf + nv + nfo], refs[nf + nv + nfo:]
        outs_f, outs_v = fn([r[...] for r in fi], [r[...] for r in vi])
        for r, o in zip(fo, outs_f):
            r[...] = o.astype(r.dtype)
        if n_vec_out:
            @pl.when(pl.program_id(0) == 0)
            def _():
                for r in vo:
                    r[...] = jnp.zeros_like(r)
            for r, o in zip(vo, outs_v):
                r[...] += o

    full_spec = pl.BlockSpec((tr, n), lambda i: (i, 0))
    vec_specs = [pl.BlockSpec(v.shape, lambda i: (0, 0)) for v in vecs]
    out_vec_spec = pl.BlockSpec((1, n), lambda i: (0, 0))
    out_shape = [jax.ShapeDtypeStruct((s, n), dt) for dt in out_dtypes]
    out_shape += [jax.ShapeDtypeStruct((1, n), F32)] * n_vec_out
    outs = _pcall(body, name=name, grid=(s // tr,),
                  in_specs=[full_spec] * nf + vec_specs,
                  out_specs=[full_spec] * nfo + [out_vec_spec] * n_vec_out,
                  out_shape=out_shape,
                  semantics=("arbitrary",) if n_vec_out else ("parallel",))(*fulls, *vecs)
    return outs[:nfo], outs[nfo:]


def _colsum(t):
    return jnp.sum(t, axis=0, keepdims=True)


def _rms_mod(x, g, sc, sh):
    r = lax.rsqrt(jnp.mean(x * x, axis=-1, keepdims=True) + EPS)
    return (x * r * g) * (1.0 + sc) + sh


def _norm_mod_fwd(x, g, sc, sh, *, name):
    def fn(f, v):
        return [_rms_mod(f[0], v[0], v[1], v[2])], []
    (h,), _ = _rowwise(fn, [x], [g, sc, sh], [BF16], 0, name=name, tr=512)
    return h


def _resid_norm_mod_fwd(x, m, gate, g, sc, sh, *, name):
    def fn(f, v):
        xn = f[0] + v[0] * f[1]
        return [xn, _rms_mod(xn, v[1], v[2], v[3])], []
    (xn, h), _ = _rowwise(fn, [x, m], [gate, g, sc, sh], [F32, BF16], 0, name=name, tr=512)
    return xn, h


def _norm_mod_bwd(x, dh, dres, g, sc, gated, *, name):
    def fn(f, v):
        xv, dhv, dr = f[:3]
        gv, scv = v[:2]
        r = lax.rsqrt(jnp.mean(xv * xv, axis=-1, keepdims=True) + EPS)
        xh = xv * r
        dn = dhv * (1.0 + scv)
        dxh = dn * gv
        dx = dr + r * (dxh - xh * jnp.mean(dxh * xh, axis=-1, keepdims=True))
        sums = [_colsum(dn * xh), _colsum(dhv * (xh * gv)), _colsum(dhv)]
        if gated is None:
            return [dx], sums
        return [dx, dx * v[2]], sums + [_colsum(dx * f[3])]
    if gated is None:
        (dx,), (dg, dsc, dsh) = _rowwise(fn, [x, dh, dres], [g, sc], [F32], 3, name=name, tr=256)
        return dx, dg, dsc, dsh, None, None
    (dx, dm), (dg, dsc, dsh, dgate) = _rowwise(fn, [x, dh, dres, gated[0]], [g, sc, gated[1]],
                                               [F32, BF16], 4, name=name, tr=256)
    return dx, dg, dsc, dsh, dm, dgate


def _loss_fwd_bwd(x, m, gate, target, *, name):
    n = x.shape[1]

    def fn(f, v):
        err = f[0] + v[0] * f[1] - f[2]
        dy = err * (1.0 / n)
        return [dy, dy * v[0]], [_colsum(err * err), _colsum(dy * f[1])]
    (dy, dm), (sq, dgate) = _rowwise(fn, [x, m, target], [gate], [F32, BF16], 2, name=name, tr=512)
    return sq, dy, dm, dgate


def _matmul(a, b, *, name, ta=False, tb=False, out_dtype=F32, relu2=None, pre_act=None,
            tm=1024, tn=1024, tk=1024):
    m = a.shape[1] if ta else a.shape[0]
    k = a.shape[0] if ta else a.shape[1]
    n = b.shape[0] if tb else b.shape[1]
    assert k == (b.shape[1] if tb else b.shape[0])
    tm, tn, tk = min(tm, m), min(tn, n), min(tk, k)
    assert m % tm == 0 and n % tn == 0 and k % tk == 0, (name, m, n, k)
    nk = k // tk
    dims = (((0 if ta else 1,), (1 if tb else 0,)), ((), ()))

    def body(*refs):
        a_ref, b_ref = refs[:2]
        acc_ref = refs[-1]
        kk = pl.program_id(2)

        @pl.when(kk == 0)
        def _():
            acc_ref[...] = jnp.zeros_like(acc_ref)
        acc_ref[...] += lax.dot_general(a_ref[...].astype(BF16), b_ref[...].astype(BF16), dims,
                                        preferred_element_type=F32)

        @pl.when(kk == nk - 1)
        def _():
            acc = acc_ref[...]
            if pre_act is not None:
                acc = acc * (2.0 * jnp.maximum(refs[2][...].astype(F32), 0.0))
            o_ref = refs[-2 - (relu2 is not None)]
            o_ref[...] = acc.astype(o_ref.dtype)
            if relu2 is not None:
                r = jnp.maximum(acc, 0.0)
                refs[-2][...] = (r * r).astype(relu2)

    a_spec = (pl.BlockSpec((tk, tm), lambda i, j, kk: (kk, i)) if ta
              else pl.BlockSpec((tm, tk), lambda i, j, kk: (i, kk)))
    b_spec = (pl.BlockSpec((tn, tk), lambda i, j, kk: (j, kk)) if tb
              else pl.BlockSpec((tk, tn), lambda i, j, kk: (kk, j)))
    out_spec = pl.BlockSpec((tm, tn), lambda i, j, kk: (i, j))
    in_specs, args = [a_spec, b_spec], [a, b]
    if pre_act is not None:
        in_specs.append(out_spec)
        args.append(pre_act)
    out_specs, out_shape = out_spec, jax.ShapeDtypeStruct((m, n), out_dtype)
    if relu2 is not None:
        out_specs, out_shape = [out_spec] * 2, [out_shape, jax.ShapeDtypeStruct((m, n), relu2)]
    return _pcall(body, name=name, grid=(m // tm, n // tn, nk),
                  in_specs=in_specs, out_specs=out_specs, out_shape=out_shape,
                  scratch_shapes=[pltpu.VMEM((tm, tn), F32)],
                  semantics=("parallel", "parallel", "arbitrary"))(*args)


def _lane_masks():
    lane = lax.broadcasted_iota(jnp.int32, (1, LANES), 1)
    return [lane < HEAD_DIM, lane >= HEAD_DIM]


def _tri_iotas(t):
    r = lax.broadcasted_iota(jnp.int32, (t, t), 0)
    c = lax.broadcasted_iota(jnp.int32, (t, t), 1)
    return r, c


def _rows(j, t):
    return pl.ds(pl.multiple_of(j * t, t), t)


def _neg_softplus(z):
    e = jnp.exp(-jnp.abs(z))
    return -(jnp.maximum(z, 0.0) + jnp.log(1.0 + e)), e


def _sb_fwd(proj, *, name):
    s = proj.shape[0]
    t = min(ATT_T, s)
    scale = HEAD_DIM ** -0.5

    def body(q_ref, k_ref, v_ref, o_ref, ltot_ref, stop_ref):
        i = pl.program_id(1)
        hm = _lane_masks()
        q = q_ref[...] * scale
        qh = [jnp.where(mk, q, 0.0).astype(BF16) for mk in hm]
        r, c = _tri_iotas(t)
        later = (r > c).astype(BF16)
        q2 = jnp.concatenate(qh, axis=0)
        causal2 = jnp.concatenate([c < r, c < r], axis=0)

        def scores(j):
            return _dot_nt(q2, k_ref[_rows(j, t), :].astype(BF16))

        def chunk(j, carry, z, masked):
            e_run, acc = carry
            vb = v_ref[_rows(j, t), :].astype(BF16)
            l, _ = _neg_softplus(z)
            if masked:
                l = jnp.where(causal2, l, 0.0)
            between = _ones_dot(l, later) + e_run
            a = jnp.exp(z + l + between)
            if masked:
                a = jnp.where(causal2, a, 0.0)
            return e_run + jnp.sum(l, axis=1, keepdims=True), acc + _dot(a.astype(BF16), vb)

        init = (jnp.zeros((2 * t, 1), F32), jnp.zeros((2 * t, LANES), F32))
        carry = chunk(i, init, scores(i), True)

        def step(st):
            j, cr, z = st
            z_next = scores(jnp.maximum(j - 1, 0))
            return j - 1, chunk(j, cr, z, False), z_next

        j_stop, (e_tot, acc), _ = lax.while_loop(
            lambda st: (st[0] >= 0) & (jnp.max(st[1][0]) > -SKIP_LOG), step,
            (i - 1, carry, scores(jnp.maximum(i - 1, 0))))
        o_ref[...] = jnp.where(hm[0], acc[:t], acc[t:])
        ltot_ref[...] = jnp.where(hm[0], e_tot[:t], e_tot[t:])
        stop_ref[...] = jnp.full(stop_ref.shape, j_stop.astype(F32), F32)

    blk = lambda cb: pl.BlockSpec((t, LANES), lambda p, i: (i, cb + p))
    full = lambda cb: pl.BlockSpec((s, LANES), lambda p, i: (0, cb + p))
    out_blk = pl.BlockSpec((t, LANES), lambda p, i: (i, p))
    n_pairs = SB_W // LANES
    return _pcall(body, name=name, grid=(n_pairs, s // t),
                  in_specs=[blk(CB_QA), full(CB_KA), full(CB_VA)],
                  out_specs=[out_blk, out_blk,
                             pl.BlockSpec((1, 1, 8, LANES), lambda p, i: (p, i, 0, 0))],
                  out_shape=[jax.ShapeDtypeStruct((s, SB_W), F32)] * 2
                  + [jax.ShapeDtypeStruct((n_pairs, s // t, 8, LANES), F32)],
                  semantics=("parallel", "arbitrary"))(proj, proj, proj)


def _sb_bwd(proj, dmixed, ltot, stop, *, name):
    s = proj.shape[0]
    t = min(ATT_T, s)
    scale = HEAD_DIM ** -0.5

    def body(q_ref, k_ref, v_ref, do_ref, ltot_ref, stop_ref, dq_ref, dk_ref, dv_ref):
        i = pl.program_id(1)

        @pl.when(i == 0)
        def _():
            dk_ref[...] = jnp.zeros_like(dk_ref)
            dv_ref[...] = jnp.zeros_like(dv_ref)

        hm = _lane_masks()
        q = q_ref[...] * scale
        do = do_ref[...]
        qh = [jnp.where(mk, q, 0.0).astype(BF16) for mk in hm]
        doh = [jnp.where(mk, do, 0.0).astype(BF16) for mk in hm]
        r, c = _tri_iotas(t)
        upto = (r <= c).astype(BF16)
        before = (r < c).astype(BF16)
        q2 = jnp.concatenate(qh, axis=0)
        do2 = jnp.concatenate(doh, axis=0)
        causal2 = jnp.concatenate([c < r, c < r], axis=0)

        j_stop = jnp.clip(jnp.max(stop_ref[...]).astype(jnp.int32), -1, i - 1)
        ltv = ltot_ref[...]
        lt = jnp.concatenate([ltv[:, 0:1], ltv[:, HEAD_DIM:HEAD_DIM + 1]], axis=0)

        def products(j):
            return (_dot_nt(q2, k_ref[_rows(j, t), :].astype(BF16)),
                    _dot_nt(do2, v_ref[_rows(j, t), :].astype(BF16)))

        def chunk(j, carry, z, da, masked):
            l_run, g_run, dq = carry
            l, e = _neg_softplus(z)
            beta = jnp.where(z >= 0.0, 1.0, e) / (1.0 + e)
            if masked:
                l = jnp.where(causal2, l, 0.0)
            prefix = _ones_dot(l, upto) + l_run
            a = jnp.exp(z + l + (lt - prefix))
            if masked:
                a = jnp.where(causal2, a, 0.0)
            g = a * da
            g_before = _ones_dot(g, before) + g_run
            dz = g * (1.0 - beta) - beta * g_before
            if masked:
                dz = jnp.where(causal2, dz, 0.0)
            dzb = dz.astype(BF16)
            dk_ref[_rows(j, t), :] += _dot_tn(dzb, q2)
            dv_ref[_rows(j, t), :] += _dot_tn(a.astype(BF16), do2)
            return (l_run + jnp.sum(l, axis=1, keepdims=True),
                    g_run + jnp.sum(g, axis=1, keepdims=True),
                    dq + _dot(dzb, k_ref[_rows(j, t), :].astype(BF16)))

        init = (jnp.zeros((2 * t, 1), F32), jnp.zeros((2 * t, 1), F32),
                jnp.zeros((2 * t, LANES), F32))
        carry = lax.fori_loop(j_stop + 1, i,
                              lambda j, cr: chunk(j, cr, *products(j), False), init)
        dq2 = chunk(i, carry, *products(i), True)[2]
        dq_ref[...] = jnp.where(hm[0], dq2[:t], dq2[t:]) * scale

    blk = lambda cb: pl.BlockSpec((t, LANES), lambda p, i: (i, cb + p))
    full = lambda cb: pl.BlockSpec((s, LANES), lambda p, i: (0, cb + p))
    out_blk = pl.BlockSpec((t, LANES), lambda p, i: (i, p))
    out_full = pl.BlockSpec((s, LANES), lambda p, i: (0, p))
    return _pcall(body, name=name, grid=(SB_W // LANES, s // t),
                  in_specs=[blk(CB_QA), full(CB_KA), full(CB_VA), blk(0), out_blk,
                            pl.BlockSpec((1, 1, 8, LANES), lambda p, i: (p, i, 0, 0))],
                  out_specs=[out_blk, out_full, out_full],
                  out_shape=[jax.ShapeDtypeStruct((s, SB_W), F32)] * 3,
                  semantics=("parallel", "arbitrary"))(proj, proj, proj, dmixed, ltot, stop)


def _group_mean(v, lo):
    s0 = jnp.sum(jnp.where(lo, v, 0.0), axis=1, keepdims=True)
    s1 = jnp.sum(jnp.where(lo, 0.0, v), axis=1, keepdims=True)
    return jnp.where(lo, s0, s1) * (1.0 / HEAD_DIM)


def _fox_prep_fwd(proj, qg, kg, *, name):
    s = proj.shape[0]
    tr = min(512, s)

    def body(q_ref, k_ref, qg_ref, kg_ref, qn_ref, kn_ref, kmax_ref):
        lo = _lane_masks()[0]
        for x_ref, g_ref, o_ref in ((q_ref, qg_ref, qn_ref), (k_ref, kg_ref, kn_ref)):
            x = x_ref[...]
            o_ref[...] = x * lax.rsqrt(_group_mean(x * x, lo) + EPS) * g_ref[...]

        @pl.when(pl.program_id(1) == 0)
        def _():
            kmax_ref[...] = jnp.zeros_like(kmax_ref)
        kn = kn_ref[...]
        norms = jnp.sqrt(_group_mean(kn * kn, lo) * HEAD_DIM)
        kmax_ref[...] = jnp.maximum(kmax_ref[...], jnp.max(norms, axis=0, keepdims=True))

    blk = lambda cb: pl.BlockSpec((tr, LANES), lambda p, i: (i, cb + p))
    vec = pl.BlockSpec((1, LANES), lambda p, i: (0, 0))
    out_blk = pl.BlockSpec((tr, LANES), lambda p, i: (i, p))
    return _pcall(body, name=name, grid=(FOX_W // LANES, s // tr),
                  in_specs=[blk(CB_QB), blk(CB_KB), vec, vec],
                  out_specs=[out_blk, out_blk, pl.BlockSpec((1, LANES), lambda p, i: (0, p))],
                  out_shape=[jax.ShapeDtypeStruct((s, FOX_W), F32)] * 2
                  + [jax.ShapeDtypeStruct((1, FOX_W), F32)],
                  semantics=("parallel", "arbitrary"))(proj, proj, qg, kg)


def _fox_prep_bwd(proj, dqn, dkn, qg, kg, *, name):
    s = proj.shape[0]
    tr = min(512, s)

    def body(q_ref, k_ref, dqn_ref, dkn_ref, qg_ref, kg_ref, dq_ref, dk_ref, dqg_ref, dkg_ref):
        @pl.when((pl.program_id(0) == 0) & (pl.program_id(1) == 0))
        def _():
            dqg_ref[...] = jnp.zeros_like(dqg_ref)
            dkg_ref[...] = jnp.zeros_like(dkg_ref)

        lo = _lane_masks()[0]
        for x_ref, dy_ref, g_ref, dx_ref, dg_ref in ((q_ref, dqn_ref, qg_ref, dq_ref, dqg_ref),
                                                     (k_ref, dkn_ref, kg_ref, dk_ref, dkg_ref)):
            x, dy = x_ref[...], dy_ref[...]
            r = lax.rsqrt(_group_mean(x * x, lo) + EPS)
            xh = x * r
            dxh = dy * g_ref[...]
            dx_ref[...] = r * (dxh - xh * _group_mean(dxh * xh, lo))
            dg_ref[...] += _colsum(dy * xh)

    blk = lambda cb: pl.BlockSpec((tr, LANES), lambda p, i: (i, cb + p))
    vec = pl.BlockSpec((1, LANES), lambda p, i: (0, 0))
    out_blk = pl.BlockSpec((tr, LANES), lambda p, i: (i, p))
    return _pcall(body, name=name, grid=(FOX_W // LANES, s // tr),
                  in_specs=[blk(CB_QB), blk(CB_KB), out_blk, out_blk, vec, vec],
                  out_specs=[out_blk, out_blk, vec, vec],
                  out_shape=[jax.ShapeDtypeStruct((s, FOX_W), F32)] * 2
                  + [jax.ShapeDtypeStruct((1, LANES), F32)] * 2,
                  semantics=("arbitrary", "arbitrary"))(proj, proj, dqn, dkn, qg, kg)


def _split3_dot(tri_bf16, x):
    hi = x.astype(BF16)
    r1 = x - hi.astype(F32)
    mid = r1.astype(BF16)
    lo = (r1 - mid.astype(F32)).astype(BF16)
    return _dot(tri_bf16, hi) + _dot(tri_bf16, mid) + _dot(tri_bf16, lo)


def _forget_cumsum_fwd(proj, b_pad, *, name):
    s = proj.shape[0]
    tb = min(256, s)

    def body(fl_ref, b_ref, cf_ref, run_ref):
        @pl.when(pl.program_id(0) == 0)
        def _():
            run_ref[...] = jnp.zeros_like(run_ref)
        lf, _ = _neg_softplus(-(fl_ref[...] + b_ref[...]))
        r, c = _tri_iotas(tb)
        incl = _split3_dot((c <= r).astype(BF16), lf) + run_ref[...]
        cf_ref[...] = incl
        run_ref[...] = incl[tb - 1:tb, :]

    return _pcall(body, name=name, grid=(s // tb,),
                  in_specs=[pl.BlockSpec((tb, LANES), lambda i: (i, CB_FL)),
                            pl.BlockSpec((1, LANES), lambda i: (0, 0))],
                  out_specs=pl.BlockSpec((tb, LANES), lambda i: (i, 0)),
                  out_shape=jax.ShapeDtypeStruct((s, LANES), F32),
                  scratch_shapes=[pltpu.VMEM((1, LANES), F32)],
                  semantics=("arbitrary",))(proj, b_pad)


def _forget_cumsum_bwd(proj, b_pad, dcf, *, name):
    s = proj.shape[0]
    tb = min(256, s)
    nb = s // tb

    def body(fl_ref, b_ref, dcf_ref, dfl_ref, db_ref, run_ref):
        @pl.when(pl.program_id(0) == 0)
        def _():
            run_ref[...] = jnp.zeros_like(run_ref)
            db_ref[...] = jnp.zeros_like(db_ref)
        r, c = _tri_iotas(tb)
        dlf = _split3_dot((c >= r).astype(BF16), dcf_ref[...]) + run_ref[...]
        run_ref[...] = dlf[0:1, :]
        xv = fl_ref[...] + b_ref[...]
        e = jnp.exp(-jnp.abs(xv))
        sig_neg = jnp.where(xv >= 0.0, e, 1.0) / (1.0 + e)
        dfl = dlf * sig_neg
        dfl_ref[...] = dfl
        db_ref[...] += _colsum(dfl)

    return _pcall(body, name=name, grid=(nb,),
                  in_specs=[pl.BlockSpec((tb, LANES), lambda i: (nb - 1 - i, CB_FL)),
                            pl.BlockSpec((1, LANES), lambda i: (0, 0)),
                            pl.BlockSpec((tb, LANES), lambda i: (nb - 1 - i, 0))],
                  out_specs=[pl.BlockSpec((tb, LANES), lambda i: (nb - 1 - i, 0)),
                             pl.BlockSpec((1, LANES), lambda i: (0, 0))],
                  out_shape=[jax.ShapeDtypeStruct((s, LANES), F32),
                             jax.ShapeDtypeStruct((1, LANES), F32)],
                  scratch_shapes=[pltpu.VMEM((1, LANES), F32)],
                  semantics=("arbitrary",))(proj, b_pad, dcf)


def _fox_bias_q(cfc, p, h):
    lane = lax.broadcasted_iota(jnp.int32, (1, LANES), 1)
    return jnp.sum(jnp.where(lane == 2 * p + h, cfc, 0.0), axis=1, keepdims=True)


def _fox_score_bound(q, kmax_row, hm):
    out = []
    for h in range(2):
        qnorm = jnp.sqrt(jnp.sum(jnp.where(hm[h], q * q, 0.0), axis=1, keepdims=True))
        out.append(1.02 * qnorm * kmax_row[:, h * HEAD_DIM:h * HEAD_DIM + 1])
    return out


def _fox_live(cfr_ref, j, t, tops):
    jc = jnp.maximum(j, 0)
    worst = []
    for h in range(2):
        cf_min = jnp.min(cfr_ref[0, pl.ds(h, 1), _rows(jc, t)], axis=1, keepdims=True)
        worst.append(jnp.max(tops[h] - cf_min))
    return (j >= 0) & (jnp.maximum(worst[0], worst[1]) > -SKIP_LOG)


def _chip_gather_copies(x_refs, out_refs, send_sems, recv_sems, local_sems):
    ids = (lax.axis_index("x"), lax.axis_index("y"), lax.axis_index("c"))
    chip = 2 * ids[0] + ids[1]
    copies = []
    for n, (x_ref, out_ref) in enumerate(zip(x_refs, out_refs)):
        copies.append(pltpu.make_async_copy(x_ref, out_ref.at[chip], local_sems.at[n]))
        for kk, (flip_x, flip_y) in enumerate(((1, 0), (0, 1), (1, 1))):
            peer = (1 - ids[0] if flip_x else ids[0], 1 - ids[1] if flip_y else ids[1], ids[2])
            copies.append(pltpu.make_async_remote_copy(
                src_ref=x_ref, dst_ref=out_ref.at[chip],
                send_sem=send_sems.at[3 * n + kk], recv_sem=recv_sems.at[3 * n + kk],
                device_id=peer, device_id_type=MESH))
    return copies


def _fox_fwd(proj, qn, kn, cf, cf_rows, kmax, *, name, gathers=()):
    s = proj.shape[0]
    t = min(ATT_T, s)
    scale = HEAD_DIM ** -0.5
    n_pairs, nq, ng = FOX_W // LANES, s // t, len(gathers)

    def body(*refs):
        q_ref, k_ref, v_ref, cfc_ref, cfr_ref, kmax_ref = refs[:6]
        o_ref, lse_ref = refs[6 + ng:8 + ng]
        p, i = pl.program_id(0), pl.program_id(1)
        if ng:
            def copies():
                return _chip_gather_copies(refs[6:6 + ng], refs[8 + ng:8 + 2 * ng], *refs[8 + 2 * ng:])

            @pl.when((p == 0) & (i == 0))
            def _():
                for cp in copies():
                    cp.start()
        hm = _lane_masks()
        q = q_ref[...] * scale
        qh = [jnp.where(mk, q, 0.0).astype(BF16) for mk in hm]
        cfc = cfc_ref[...]
        bq = [_fox_bias_q(cfc, p, h) for h in range(2)]
        qk_top = _fox_score_bound(q, kmax_ref[...], hm)
        r, c = _tri_iotas(t)
        causal = c <= r

        q2 = jnp.concatenate(qh, axis=0)
        causal2 = jnp.concatenate([causal, causal], axis=0)

        def scores(j):
            return _dot_nt(q2, k_ref[_rows(j, t), :].astype(BF16))

        def chunk(j, carry, z2, masked):
            m_run, l_run, acc = carry
            vb = v_ref[_rows(j, t), :].astype(BF16)
            z = jnp.concatenate(
                [z2[h * t:(h + 1) * t] + (bq[h] - cfr_ref[0, pl.ds(h, 1), _rows(j, t)])
                 for h in range(2)], axis=0)
            if masked:
                z = jnp.where(causal2, z, -1e30)
            m_new = jnp.maximum(m_run, jnp.max(z, axis=1, keepdims=True))
            alpha = jnp.exp(m_run - m_new)
            pr = jnp.exp(z - m_new)
            return (m_new, alpha * l_run + jnp.sum(pr, axis=1, keepdims=True),
                    alpha * acc + _dot(pr.astype(BF16), vb))

        init = (jnp.full((2 * t, 1), -1e30, F32), jnp.zeros((2 * t, 1), F32),
                jnp.zeros((2 * t, LANES), F32))
        carry = chunk(i, init, scores(i), True)

        def live(j, cr):
            return _fox_live(cfr_ref, j, t,
                             [qk_top[h] + bq[h] - cr[0][h * t:(h + 1) * t] for h in range(2)])

        def step(st):
            j, _, cr, z2 = st
            z2_next = scores(jnp.maximum(j - 1, 0))
            cr = chunk(j, cr, z2, False)
            return j - 1, live(j - 1, cr), cr, z2_next

        m_fin, l_fin, acc = lax.while_loop(
            lambda st: st[1], step,
            (i - 1, live(i - 1, carry), carry, scores(jnp.maximum(i - 1, 0))))[2]
        o2 = acc / l_fin
        lse2 = m_fin + jnp.log(l_fin)
        o_ref[...] = jnp.where(hm[0], o2[:t], o2[t:])
        lse_ref[...] = jnp.where(hm[0], lse2[:t], lse2[t:])
        if ng:
            @pl.when((p == n_pairs - 1) & (i == nq - 1))
            def _():
                for cp in copies():
                    cp.wait()

    blk = pl.BlockSpec((t, LANES), lambda p, i: (i, p))
    full = pl.BlockSpec((s, LANES), lambda p, i: (0, p))
    any_spec = pl.BlockSpec(memory_space=pl.ANY)
    dma = pltpu.SemaphoreType.DMA
    return _pcall(body, name=name, grid=(n_pairs, nq),
                  in_specs=[blk, full, pl.BlockSpec((s, LANES), lambda p, i: (0, CB_VB + p)),
                            pl.BlockSpec((t, LANES), lambda p, i: (i, 0)),
                            pl.BlockSpec((1, 2, s), lambda p, i: (p, 0, 0)),
                            pl.BlockSpec((1, LANES), lambda p, i: (0, p))] + [any_spec] * ng,
                  out_specs=[blk, blk] + [any_spec] * ng,
                  out_shape=[jax.ShapeDtypeStruct((s, FOX_W), F32)] * 2
                  + [jax.ShapeDtypeStruct((4,) + g.shape, g.dtype) for g in gathers],
                  scratch_shapes=[dma((3 * ng,)), dma((3 * ng,)), dma((ng,))] if ng else [],
                  semantics=("arbitrary", "arbitrary") if ng else ("parallel", "arbitrary"))(
                      qn, kn, proj, cf, cf_rows, kmax, *gathers)


def _fox_bwd(proj, qn, kn, cf, cf_rows, kmax, do, o, lse, *, name):
    s = proj.shape[0]
    t = min(ATT_T, s)
    scale = HEAD_DIM ** -0.5

    def body(q_ref, k_ref, v_ref, cfc_ref, cfr_ref, kmax_ref, do_ref, o_ref, lse_ref,
             dq_ref, dk_ref, dv_ref, dcf_ref, dcfq_ref):
        p, i = pl.program_id(0), pl.program_id(1)

        @pl.when(i == 0)
        def _():
            dk_ref[...] = jnp.zeros_like(dk_ref)
            dv_ref[...] = jnp.zeros_like(dv_ref)
            dcf_ref[...] = jnp.zeros_like(dcf_ref)

        hm = _lane_masks()
        q = q_ref[...] * scale
        do = do_ref[...]
        dov = do * o_ref[...]
        qh = [jnp.where(mk, q, 0.0).astype(BF16) for mk in hm]
        doh = [jnp.where(mk, do, 0.0).astype(BF16) for mk in hm]
        delta = [jnp.sum(jnp.where(mk, dov, 0.0), axis=1, keepdims=True) for mk in hm]
        lsev = lse_ref[...]
        lse = [lsev[:, 0:1], lsev[:, HEAD_DIM:HEAD_DIM + 1]]
        cfc = cfc_ref[...]
        bq = [_fox_bias_q(cfc, p, h) - lse[h] for h in range(2)]
        qk_top = _fox_score_bound(q, kmax_ref[...], hm)
        tops = [qk_top[h] + bq[h] for h in range(2)]
        r, c = _tri_iotas(t)
        j_stop = lax.while_loop(lambda st: st[1],
                                lambda st: (st[0] - 1, _fox_live(cfr_ref, st[0] - 1, t, tops)),
                                (i - 1, _fox_live(cfr_ref, i - 1, t, tops)))[0]
        q2 = jnp.concatenate(qh, axis=0)
        do2 = jnp.concatenate(doh, axis=0)
        delta2 = jnp.concatenate(delta, axis=0)
        causal2 = jnp.concatenate([c <= r, c <= r], axis=0)

        def products(j):
            return (_dot_nt(q2, k_ref[_rows(j, t), :].astype(BF16)),
                    _dot_nt(do2, v_ref[_rows(j, t), :].astype(BF16)))

        def chunk(j, carry, z2, dp, masked):
            dq, row_sum = carry
            z = jnp.concatenate(
                [z2[h * t:(h + 1) * t] + (bq[h] - cfr_ref[0, pl.ds(h, 1), _rows(j, t)])
                 for h in range(2)], axis=0)
            pr = jnp.exp(z)
            if masked:
                pr = jnp.where(causal2, pr, 0.0)
            ds = pr * (dp - delta2)
            dsb = ds.astype(BF16)
            dk_ref[_rows(j, t), :] += _dot_tn(dsb, q2)
            dv_ref[_rows(j, t), :] += _dot_tn(pr.astype(BF16), do2)
            for h in range(2):
                dcf_ref[0, pl.ds(h, 1), _rows(j, t)] -= jnp.sum(ds[h * t:(h + 1) * t], axis=0,
                                                               keepdims=True)
            return (dq + _dot(dsb, k_ref[_rows(j, t), :].astype(BF16)),
                    row_sum + jnp.sum(ds, axis=1, keepdims=True))

        init = (jnp.zeros((2 * t, LANES), F32), jnp.zeros((2 * t, 1), F32))
        carry = lax.fori_loop(j_stop + 1, i,
                              lambda j, cr: chunk(j, cr, *products(j), False), init)
        dq2, row_sum = chunk(i, carry, *products(i), True)
        dq_ref[...] = jnp.where(hm[0], dq2[:t], dq2[t:]) * scale
        dcfq_ref[...] = jnp.where(hm[0], row_sum[:t], row_sum[t:])

    blk = pl.BlockSpec((t, LANES), lambda p, i: (i, p))
    full = pl.BlockSpec((s, LANES), lambda p, i: (0, p))
    rows = pl.BlockSpec((1, 2, s), lambda p, i: (p, 0, 0))
    return _pcall(body, name=name, grid=(FOX_W // LANES, s // t),
                  in_specs=[blk, full, pl.BlockSpec((s, LANES), lambda p, i: (0, CB_VB + p)),
                            pl.BlockSpec((t, LANES), lambda p, i: (i, 0)), rows,
                            pl.BlockSpec((1, LANES), lambda p, i: (0, p)),
                            pl.BlockSpec((t, LANES), lambda p, i: (i, SB_W // LANES + p)),
                            blk, blk],
                  out_specs=[blk, full, full, rows, blk],
                  out_shape=[jax.ShapeDtypeStruct((s, FOX_W), F32)] * 3
                  + [jax.ShapeDtypeStruct((FOX_W // LANES, 2, s), F32),
                     jax.ShapeDtypeStruct((s, FOX_W), F32)],
                  semantics=("parallel", "arbitrary"))(qn, kn, proj, cf, cf_rows, kmax, do, o, lse)


_GELU_C0 = math.sqrt(2.0 / math.pi)
_GELU_C1 = 0.044715


def _gelu(x):
    th = jnp.tanh(_GELU_C0 * (x + _GELU_C1 * (x * x * x)))
    return 0.5 * x * (1.0 + th), th


def _gelu_grad(x, th):
    return 0.5 * (1.0 + th) + 0.5 * x * (1.0 - th * th) * (_GELU_C0 * (1.0 + 3.0 * _GELU_C1 * x * x))


def _sgu_mix(wm, vn_c, lo, bcol):
    return jnp.where(lo, _dot(wm[0], vn_c) + bcol[0], _dot(wm[1], vn_c) + bcol[1])


def _sgu_fwd(proj, w, b_cols, gn, *, name):
    s = proj.shape[0]
    tr = min(512, s)
    ch = SGU_CHUNK

    def body(u_ref, v_ref, w_ref, b_ref, gn_ref, o_ref):
        lo = _lane_masks()[0]
        r, c = _tri_iotas(ch)
        wm = [jnp.where(c <= r, w_ref[h], 0.0).astype(BF16) for h in range(2)]
        bcol = [b_ref[0, :, h:h + 1] for h in range(2)]
        for n in range(tr // ch):
            rows = slice(n * ch, (n + 1) * ch)
            u, _ = _gelu(u_ref[rows, :])
            vg, _ = _gelu(v_ref[rows, :])
            vn = vg * lax.rsqrt(_group_mean(vg * vg, lo) + EPS) * gn_ref[0]
            o_ref[rows, :] = u * _sgu_mix(wm, vn.astype(BF16), lo, bcol)

    blk = lambda cb: pl.BlockSpec((tr, LANES), lambda p, i: (i, cb + p))
    return _pcall(body, name=name, grid=(SGU_W // LANES, s // tr),
                  in_specs=[blk(CB_UC), blk(CB_VC),
                            pl.BlockSpec((2, ch, ch), lambda p, i: (p, 0, 0)),
                            pl.BlockSpec((1, ch, 2), lambda p, i: (p, 0, 0)),
                            pl.BlockSpec((1, 1, LANES), lambda p, i: (p, 0, 0))],
                  out_specs=pl.BlockSpec((tr, LANES), lambda p, i: (i, p)),
                  out_shape=jax.ShapeDtypeStruct((s, SGU_W), F32),
                  semantics=("parallel", "parallel"))(proj, proj, w, b_cols, gn)


def _sgu_bwd(proj, dmixed, w, w_t, b_cols, gn, *, name):
    s = proj.shape[0]
    tr = min(512, s)
    ch = SGU_CHUNK
    cb_do = (SB_W + FOX_W) // LANES

    def body(u_ref, v_ref, do_ref, w_ref, wt_ref, b_ref, gn_ref,
             du_ref, dv_ref, dw_ref, db_ref, dgn_ref):
        @pl.when(pl.program_id(1) == 0)
        def _():
            dw_ref[...] = jnp.zeros_like(dw_ref)
            db_ref[...] = jnp.zeros_like(db_ref)
            dgn_ref[...] = jnp.zeros_like(dgn_ref)

        hm = _lane_masks()
        lo = hm[0]
        r, c = _tri_iotas(ch)
        wm = [jnp.where(c <= r, w_ref[h], 0.0).astype(BF16) for h in range(2)]
        wtm = [jnp.where(r <= c, wt_ref[h], 0.0).astype(BF16) for h in range(2)]
        bcol = [b_ref[0, :, h:h + 1] for h in range(2)]
        gnv = gn_ref[0]
        for n in range(tr // ch):
            rows = slice(n * ch, (n + 1) * ch)
            uc, vc, do = u_ref[rows, :], v_ref[rows, :], do_ref[rows, :]
            u, thu = _gelu(uc)
            vg, thv = _gelu(vc)
            rinv = lax.rsqrt(_group_mean(vg * vg, lo) + EPS)
            xh = vg * rinv
            vnb = (xh * gnv).astype(BF16)
            mix = _sgu_mix(wm, vnb, lo, bcol)
            du_ref[rows, :] = do * mix * _gelu_grad(uc, thu)
            dm = do * u
            dmb = dm.astype(BF16)
            dvn = jnp.where(lo, _dot(wtm[0], dmb), _dot(wtm[1], dmb))
            for h in range(2):
                dmh = jnp.where(hm[h], dm, 0.0)
                dw_ref[h] += jnp.where(c <= r, _dot_nt(dmh.astype(BF16), vnb), 0.0)
                db_ref[0, :, h:h + 1] += jnp.sum(dmh, axis=1, keepdims=True)
            dgn_ref[0] += _colsum(dvn * xh)
            dxh = dvn * gnv
            dvg = rinv * (dxh - xh * _group_mean(dxh * xh, lo))
            dv_ref[rows, :] = dvg * _gelu_grad(vc, thv)

    blk = lambda cb: pl.BlockSpec((tr, LANES), lambda p, i: (i, cb + p))
    w_spec = pl.BlockSpec((2, ch, ch), lambda p, i: (p, 0, 0))
    b_spec = pl.BlockSpec((1, ch, 2), lambda p, i: (p, 0, 0))
    g_spec = pl.BlockSpec((1, 1, LANES), lambda p, i: (p, 0, 0))
    out_blk = pl.BlockSpec((tr, LANES), lambda p, i: (i, p))
    return _pcall(body, name=name, grid=(SGU_W // LANES, s // tr),
                  in_specs=[blk(CB_UC), blk(CB_VC), blk(cb_do), w_spec, w_spec, b_spec, g_spec],
                  out_specs=[out_blk, out_blk, w_spec, b_spec, g_spec],
                  out_shape=[jax.ShapeDtypeStruct((s, SGU_W), F32)] * 2
                  + [jax.ShapeDtypeStruct(w.shape, F32), jax.ShapeDtypeStruct(b_cols.shape, F32),
                     jax.ShapeDtypeStruct(gn.shape, F32)],
                  semantics=("parallel", "arbitrary"))(proj, proj, dmixed, w, w_t, b_cols, gn)


def _pad_lanes(v):
    return jnp.zeros((1, LANES), F32).at[0, :v.shape[0]].set(v)


def _small_views(sm):
    return dict(
        n1=sm["norm1_g"][None, :], n2=sm["norm2_g"][None, :],
        b_pad=_pad_lanes(sm["b_forget"]),
        qg=jnp.tile(sm["q_norm_g"], 2)[None, :], kg=jnp.tile(sm["k_norm_g"], 2)[None, :],
        gn=sm["sgu_norm_g"].reshape(2, 1, LANES),
        w=sm["sgu_w"], w_t=jnp.swapaxes(sm["sgu_w"], 1, 2),
        b_cols=sm["sgu_b"].reshape(2, 2, SGU_CHUNK).transpose(0, 2, 1))


def _cf_rows(cf):
    return cf[:, :FOX_HEADS].T.reshape(FOX_W // LANES, 2, cf.shape[0])


def _layer_fwd(x_in, prev, mod, wts, sm, l, gathers=(), late_weights=None):
    sh1, sc1, g1, sh2, sc2, g2 = mod
    v = _small_views(sm)
    if prev is None:
        x0 = x_in
        h1 = _norm_mod_fwd(x0, v["n1"], sc1, sh1, name=f"l{l}_norm1")
    else:
        x0, h1 = _resid_norm_mod_fwd(x_in, prev[0], prev[1], v["n1"], sc1, sh1, name=f"l{l}_norm1")
    proj = _matmul(h1, wts["w_in"], name=f"l{l}_proj")
    o_sb, sb_ltot, sb_stop = _sb_fwd(proj, name=f"l{l}_sb_fwd")
    qn, kn, kmax = _fox_prep_fwd(proj, v["qg"], v["kg"], name=f"l{l}_fox_prep")
    cf = _forget_cumsum_fwd(proj, v["b_pad"], name=f"l{l}_cumf")
    cfr = _cf_rows(cf)
    o_fox, lse, *gathered = _fox_fwd(proj, qn, kn, cf, cfr, kmax, name=f"l{l}_fox_fwd",
                                     gathers=gathers)
    if gathers:
        late_weights(gathered)
    o_sgu = _sgu_fwd(proj, v["w"], v["b_cols"], v["gn"], name=f"l{l}_sgu_fwd")
    mixed = jnp.concatenate([o_sb, o_fox, o_sgu], axis=1).astype(BF16)
    mo = _matmul(mixed, wts["w_out"], name=f"l{l}_wout")
    x1, h2 = _resid_norm_mod_fwd(x0, mo, g1, v["n2"], sc2, sh2, name=f"l{l}_norm2")
    a, rr = _matmul(h2, wts["w1"], name=f"l{l}_mlp1", out_dtype=BF16, relu2=BF16)
    m2 = _matmul(rr, wts["w2"], name=f"l{l}_mlp2")
    saved = dict(x0=x0, h1=h1, proj=proj, sb_ltot=sb_ltot, sb_stop=sb_stop, qn=qn, kn=kn, kmax=kmax, cf=cf, cfr=cfr, o_fox=o_fox,
                 lse=lse, mixed=mixed, mo=mo, x1=x1, h2=h2, a=a, rr=rr, m2=m2)
    return saved


def _layer_bwd(dx2, dm2, dg2, sv, mod, wts, sm, l, below):
    sh1, sc1, g1, sh2, sc2, g2 = mod
    v = _small_views(sm)
    dw2 = _matmul(sv["rr"], dm2, ta=True, name=f"l{l}_dw2")
    da = _matmul(dm2, wts["w2"], tb=True, name=f"l{l}_da", out_dtype=BF16, pre_act=sv["a"])
    dw1 = _matmul(sv["h2"], da, ta=True, name=f"l{l}_dw1")
    dh2 = _matmul(da, wts["w1"], tb=True, name=f"l{l}_dh2")
    dx1, dn2, dsc2, dsh2, dmo, dg1 = _norm_mod_bwd(sv["x1"], dh2, dx2, v["n2"], sc2,
                                                    (sv["mo"], g1), name=f"l{l}_norm2_bwd")
    dwo = _matmul(sv["mixed"], dmo, ta=True, name=f"l{l}_dwout")
    dmixed = _matmul(dmo, wts["w_out"], tb=True, name=f"l{l}_dmixed")
    proj = sv["proj"]
    dqa, dka, dva = _sb_bwd(proj, dmixed, sv["sb_ltot"], sv["sb_stop"], name=f"l{l}_sb_bwd")
    dqn, dkn, dvb, dcfr, dcfq = _fox_bwd(proj, sv["qn"], sv["kn"], sv["cf"], sv["cfr"], sv["kmax"], dmixed,
                                   sv["o_fox"], sv["lse"], name=f"l{l}_fox_bwd")
    dqb, dkb, dqg, dkg = _fox_prep_bwd(proj, dqn, dkn, v["qg"], v["kg"], name=f"l{l}_fox_prep_bwd")
    s = proj.shape[0]
    dcf_heads = dcfr.reshape(FOX_HEADS, s).T + dcfq.reshape(s, FOX_HEADS, HEAD_DIM)[:, :, 0]
    dcf = jnp.zeros((s, LANES), F32).at[:, :FOX_HEADS].set(dcf_heads)
    dfl, dbf = _forget_cumsum_bwd(proj, v["b_pad"], dcf, name=f"l{l}_cumf_bwd")
    duc, dvc, dsw, dsb_cols, dgn = _sgu_bwd(proj, dmixed, v["w"], v["w_t"], v["b_cols"], v["gn"],
                                            name=f"l{l}_sgu_bwd")
    dproj = jnp.concatenate([dqa, dka, dva, dqb, dkb, dvb, duc, dvc, dfl,
                             jnp.zeros((s, LANES), F32)], axis=1).astype(BF16)
    dwin = _matmul(sv["h1"], dproj, ta=True, name=f"l{l}_dwin")
    dh1 = _matmul(dproj, wts["w_in"], tb=True, name=f"l{l}_dh1")
    dx0, dn1, dsc1, dsh1, dm_below, dg_below = _norm_mod_bwd(sv["x0"], dh1, dx1, v["n1"], sc1, below,
                                                             name=f"l{l}_norm1_bwd")
    big = dict(w_in=dwin, w_out=dwo, w1=dw1, w2=dw2)
    small = dict(norm1_g=dn1[0], norm2_g=dn2[0], b_forget=dbf[0, :FOX_HEADS],
                 q_norm_g=dqg[0, :HEAD_DIM] + dqg[0, HEAD_DIM:],
                 k_norm_g=dkg[0, :HEAD_DIM] + dkg[0, HEAD_DIM:],
                 sgu_norm_g=dgn.reshape(4, HEAD_DIM), sgu_w=dsw,
                 sgu_b=dsb_cols.transpose(0, 2, 1).reshape(4, SGU_CHUNK))
    dmod = jnp.concatenate([dsh1, dsc1, dg1, dsh2, dsc2, dg2], axis=1)
    return dx0, dm_below, dg_below, big, small, dmod


def _w_in_to_internal(w):
    pad = jnp.zeros((w.shape[0], PROJ_W - IN_W), w.dtype)
    return jnp.concatenate([w[:, :ATT_W], w[:, ATT_W + FOX_HEADS:], w[:, ATT_W:ATT_W + FOX_HEADS],
                            pad], axis=1)


def _w_in_from_internal(g):
    n_gate = SGU_W * 2
    return jnp.concatenate([g[:, :ATT_W], g[:, ATT_W + n_gate:ATT_W + n_gate + FOX_HEADS],
                            g[:, ATT_W:ATT_W + n_gate]], axis=1)


def _exchange(x, masks, slot_shift, slot_bits, scatter, *, name):
    n_slots = 2 ** slot_bits
    blk_shape = x.shape[1:] if scatter else x.shape
    n_peers = len(masks)

    def body(x_ref, out_ref, send_sems, recv_sems, local_sem):
        ids = (lax.axis_index("x"), lax.axis_index("y"), lax.axis_index("c"))
        me = 4 * ids[0] + 2 * ids[1] + ids[2]
        my_slot = (me >> slot_shift) & (n_slots - 1)

        def peer(mask):
            return tuple(1 - v if (mask >> b) & 1 else v for v, b in zip(ids, (2, 1, 0)))

        def src_for(slot):
            return x_ref.at[slot] if scatter else x_ref

        copies = [pltpu.make_async_copy(src_for(my_slot), out_ref.at[my_slot], local_sem)]
        for kk, mask in enumerate(masks):
            peer_slot = ((me ^ mask) >> slot_shift) & (n_slots - 1)
            copies.append(pltpu.make_async_remote_copy(
                src_ref=src_for(peer_slot), dst_ref=out_ref.at[my_slot],
                send_sem=send_sems.at[kk], recv_sem=recv_sems.at[kk],
                device_id=peer(mask), device_id_type=MESH))
        for cp in copies:
            cp.start()
        for cp in copies:
            cp.wait()

    any_spec = pl.BlockSpec(memory_space=pl.ANY)
    return _pcall(body, name=name, in_specs=[any_spec], out_specs=any_spec,
                  out_shape=jax.ShapeDtypeStruct((n_slots,) + tuple(blk_shape), x.dtype),
                  scratch_shapes=[pltpu.SemaphoreType.DMA((n_peers,)),
                                  pltpu.SemaphoreType.DMA((n_peers,)),
                                  pltpu.SemaphoreType.DMA(())])(x)


CORE_PIECE_BYTES = 12 * 2 ** 20
CORE_DMA_CHUNKS = 4


def _core_swap_piece(x, *, name):
    rows, cols = x.shape
    n_ch = CORE_DMA_CHUNKS if rows % (16 * CORE_DMA_CHUNKS) == 0 else 1
    rc = rows // n_ch

    def body(x_ref, out_ref, send_sems, recv_sems):
        sibling = (lax.axis_index("x"), lax.axis_index("y"), 1 - lax.axis_index("c"))
        copies = [pltpu.make_async_remote_copy(
            src_ref=x_ref.at[pl.ds(ch * rc, rc)], dst_ref=out_ref.at[pl.ds(ch * rc, rc)],
            send_sem=send_sems.at[ch], recv_sem=recv_sems.at[ch],
            device_id=sibling, device_id_type=MESH) for ch in range(n_ch)]
        for cp in copies:
            cp.start()
        for cp in copies:
            cp.wait()

    vmem = pl.BlockSpec(memory_space=pltpu.VMEM)
    return _pcall(body, name=name, in_specs=[vmem], out_specs=vmem,
                  out_shape=jax.ShapeDtypeStruct(x.shape, x.dtype),
                  scratch_shapes=[pltpu.SemaphoreType.DMA((n_ch,)),
                                  pltpu.SemaphoreType.DMA((n_ch,))])(x)


def _core_swap(x, *, name):
    rows, cols = x.shape
    n = 1
    while (rows % n or (rows // n) % 16 or
           (rows // n) * (-(-cols // LANES) * LANES) * x.dtype.itemsize > CORE_PIECE_BYTES):
        n += 1
    pr = rows // n
    pieces = [_core_swap_piece(x[kk * pr:(kk + 1) * pr], name=f"{name}_{kk}") for kk in range(n)]
    return pieces[0] if n == 1 else jnp.concatenate(pieces, axis=0)


def _by_core(core, mine, theirs, axis):
    return jnp.where(core == 0, jnp.concatenate([mine, theirs], axis=axis),
                     jnp.concatenate([theirs, mine], axis=axis))


def _gather_chips(x, *, name):
    return _exchange(x, (2, 4, 6), 1, 2, False, name=name)


def _gather_all(x, *, name):
    return _exchange(x, (1, 2, 3, 4, 5, 6, 7), 0, 3, False, name=name)


def _scatter_chips(x4, *, name):
    return _exchange(x4, (2, 4, 6), 1, 2, True, name=name)


def _sum_slots(parts, *, name, out_dtype=F32, tr=256):
    n, rows, cols = parts.shape
    tr = min(tr, rows)
    assert rows % tr == 0, (name, rows, tr)

    def body(p_ref, o_ref):
        acc = p_ref[0].astype(F32)
        for kk in range(1, n):
            acc = acc + p_ref[kk].astype(F32)
        o_ref[...] = acc.astype(o_ref.dtype)

    return _pcall(body, name=name, grid=(rows // tr,),
                  in_specs=[pl.BlockSpec((n, tr, cols), lambda i: (0, i, 0))],
                  out_specs=pl.BlockSpec((tr, cols), lambda i: (i, 0)),
                  out_shape=jax.ShapeDtypeStruct((rows, cols), out_dtype),
                  semantics=("parallel",))(parts)


def _add2(a, b, *, name, out_dtype, tr=512):
    def fn(f, v):
        return [f[0] + f[1]], []
    (out,), _ = _rowwise(fn, [a, b], [], [out_dtype], 0, name=name, tr=tr)
    return out


def _adamw(w, m, v, parts, *, name, tr=256):
    n, rows, cols = parts.shape
    tr = min(tr, rows)
    assert rows % tr == 0, (name, rows, tr)
    c1 = 1.0 - ADAM_B1 ** ADAM_STEP
    c2 = 1.0 - ADAM_B2 ** ADAM_STEP

    def body(w_ref, m_ref, v_ref, p_ref, g_ref, d_ref, nm_ref, nv_ref):
        g = p_ref[0]
        for kk in range(1, n):
            g = g + p_ref[kk]
        nm = ADAM_B1 * m_ref[...] + (1.0 - ADAM_B1) * g
        nv = ADAM_B2 * v_ref[...] + (1.0 - ADAM_B2) * (g * g)
        g_ref[...] = g
        nm_ref[...] = nm
        nv_ref[...] = nv
        d_ref[...] = -ADAM_LR * ((nm / c1) / (jnp.sqrt(nv / c2) + ADAM_EPS) + ADAM_WD * w_ref[...])

    spec = pl.BlockSpec((tr, cols), lambda i: (i, 0))
    return _pcall(body, name=name, grid=(rows // tr,),
                  in_specs=[spec, spec, spec, pl.BlockSpec((n, tr, cols), lambda i: (0, i, 0))],
                  out_specs=[spec] * 4,
                  out_shape=[jax.ShapeDtypeStruct((rows, cols), F32)] * 4,
                  semantics=("parallel",))(w, m, v, parts)


def _silu(c):
    return c / (1.0 + jnp.exp(-c))


def _ada_fwd(c_all, ada_w, ada_b_sh, *, name):
    nl, d, wsh = ada_w.shape

    def body(c_ref, w_ref, b_ref, o_ref):
        cond = _silu(c_ref[...]).astype(BF16)
        o_ref[0] = _dot(cond, w_ref[0].astype(BF16)) + b_ref[0]

    return _pcall(body, name=name, grid=(nl,),
                  in_specs=[pl.BlockSpec(c_all.shape, lambda l: (0, 0)),
                            pl.BlockSpec((1, d, wsh), lambda l: (l, 0, 0)),
                            pl.BlockSpec((1, 1, wsh), lambda l: (l, 0, 0))],
                  out_specs=pl.BlockSpec((1, c_all.shape[0], wsh), lambda l: (l, 0, 0)),
                  out_shape=jax.ShapeDtypeStruct((nl, c_all.shape[0], wsh), F32),
                  semantics=("parallel",))(c_all, ada_w, ada_b_sh)


def _ada_bwd(c_all, dmod_sh, *, name):
    nl, nb, wsh = dmod_sh.shape
    d = c_all.shape[1]

    def body(c_ref, dm_ref, o_ref):
        cond = _silu(c_ref[...]).astype(BF16)
        o_ref[0] = _dot_tn(cond, dm_ref[0].astype(BF16))

    return _pcall(body, name=name, grid=(nl,),
                  in_specs=[pl.BlockSpec(c_all.shape, lambda l: (0, 0)),
                            pl.BlockSpec((1, nb, wsh), lambda l: (l, 0, 0))],
                  out_specs=pl.BlockSpec((1, d, wsh), lambda l: (l, 0, 0)),
                  out_shape=jax.ShapeDtypeStruct((nl, d, wsh), F32),
                  semantics=("parallel",))(c_all, dmod_sh)


SMALL_NAMES = ("norm1_g", "norm2_g", "b_forget", "q_norm_g", "k_norm_g", "sgu_norm_g", "sgu_w",
               "sgu_b")
WEIGHT_NAMES = ("ada_w", "ada_b", "norm1_g", "norm2_g", "w_in", "b_forget", "q_norm_g", "k_norm_g",
                "sgu_norm_g", "sgu_w", "sgu_b", "w_out", "mlp_w1", "mlp_w2")


SMALL_TILE_ROWS = 256


def _pack_small(tree):
    flat = jnp.concatenate([tree[n].reshape(-1) for n in SMALL_NAMES])
    n = flat.shape[0]
    rows = -(-n // (SMALL_TILE_ROWS * LANES)) * SMALL_TILE_ROWS
    return jnp.zeros((rows * LANES,), F32).at[:n].set(flat).reshape(rows, LANES)


def _unpack_small(packed, like):
    flat = packed.reshape(-1)
    out, off = {}, 0
    for n in SMALL_NAMES:
        size = like[n].size
        out[n] = flat[off:off + size].reshape(like[n].shape)
        off += size
    return out


def kernel(x, c, ada_w, ada_b, norm1_g, norm2_g, w_in, b_forget, q_norm_g, k_norm_g, sgu_norm_g, sgu_w, sgu_b, w_out, mlp_w1, mlp_w2, loss_target, m_ada_w, m_ada_b, m_norm1_g, m_norm2_g, m_w_in, m_b_forget, m_q_norm_g, m_k_norm_g, m_sgu_norm_g, m_sgu_w, m_sgu_b, m_w_out, m_mlp_w1, m_mlp_w2, v_ada_w, v_ada_b, v_norm1_g, v_norm2_g, v_w_in, v_b_forget, v_q_norm_g, v_k_norm_g, v_sgu_norm_g, v_sgu_w, v_sgu_b, v_w_out, v_mlp_w1, v_mlp_w2):
    w = dict(ada_w=ada_w, ada_b=ada_b, norm1_g=norm1_g, norm2_g=norm2_g, w_in=w_in,
             b_forget=b_forget, q_norm_g=q_norm_g, k_norm_g=k_norm_g, sgu_norm_g=sgu_norm_g,
             sgu_w=sgu_w, sgu_b=sgu_b, w_out=w_out, mlp_w1=mlp_w1, mlp_w2=mlp_w2)
    mom = dict(ada_w=m_ada_w, ada_b=m_ada_b, norm1_g=m_norm1_g, norm2_g=m_norm2_g, w_in=m_w_in,
               b_forget=m_b_forget, q_norm_g=m_q_norm_g, k_norm_g=m_k_norm_g,
               sgu_norm_g=m_sgu_norm_g, sgu_w=m_sgu_w, sgu_b=m_sgu_b, w_out=m_w_out,
               mlp_w1=m_mlp_w1, mlp_w2=m_mlp_w2)
    var = dict(ada_w=v_ada_w, ada_b=v_ada_b, norm1_g=v_norm1_g, norm2_g=v_norm2_g, w_in=v_w_in,
               b_forget=v_b_forget, q_norm_g=v_q_norm_g, k_norm_g=v_k_norm_g,
               sgu_norm_g=v_sgu_norm_g, sgu_w=v_sgu_w, sgu_b=v_sgu_b, w_out=v_w_out,
               mlp_w1=v_mlp_w1, mlp_w2=v_mlp_w2)
    depth, d = norm1_g.shape
    chip = 2 * lax.axis_index("x") + lax.axis_index("y")
    me = 2 * chip + lax.axis_index("c")
    n_chips = 4
    ada_sh = ada_w.shape[2]

    core = lax.axis_index("c")
    half_l = depth // 2

    def my_part(w_sh):
        _, r, cols = w_sh.shape
        mine = lax.dynamic_slice_in_dim(w_sh, core * half_l, half_l, axis=0).astype(BF16)
        return mine.reshape(half_l * r, cols)

    def share(got, w_sh, name):
        _, r, cols = w_sh.shape
        theirs = _core_swap(got.reshape(n_chips * half_l * r, cols), name=f"share_{name}")
        return _by_core(core, got.reshape(n_chips, half_l, r, cols),
                        theirs.reshape(n_chips, half_l, r, cols), 1)

    g_in = share(_gather_chips(my_part(w_in), name="gather_w_in"), w_in, "w_in")
    layer_w = [dict(w_in=_w_in_to_internal(
        jnp.concatenate([g_in[k, l] for k in range(n_chips)], axis=1))) for l in range(depth)]
    later = (("w_out", w_out), ("w1", mlp_w1), ("w2", mlp_w2))

    def late_weights(gathered):
        g_out, g_w1, g_w2 = [share(got, w_sh, name) for got, (name, w_sh) in zip(gathered, later)]
        for l in range(depth):
            layer_w[l].update(
                w_out=g_out[:, l].reshape(d, d),
                w1=jnp.concatenate([g_w1[k, l] for k in range(n_chips)], axis=1),
                w2=g_w2[:, l].reshape(D_FF, d))

    c_all = _gather_all(jnp.zeros((8, d), F32).at[0].set(c[0]), name="gather_c")[:, 0]
    c_pad = jnp.concatenate([c_all, jnp.zeros_like(c_all)], axis=0)
    ada_b_sh = lax.dynamic_slice_in_dim(ada_b, chip * ada_sh, ada_sh, axis=1)[:, None, :]
    mod_sh = _ada_fwd(c_pad, ada_w, ada_b_sh, name="ada_fwd")
    mod_all = _gather_chips(mod_sh, name="gather_mod")
    mod_me = lax.dynamic_index_in_dim(mod_all, me, axis=2, keepdims=False)
    mod_me = mod_me.transpose(1, 0, 2).reshape(depth, 6, 1, d)

    saved = []
    xs, prev = x[0], None
    for l in range(depth):
        mod = [mod_me[l, kk] for kk in range(6)]
        sm = {n: w[n][l] for n in SMALL_NAMES}
        first = dict(gathers=[my_part(w_sh) for _, w_sh in later], late_weights=late_weights)
        sv = _layer_fwd(xs, prev, mod, layer_w[l], sm, l, **(first if l == 0 else {}))
        saved.append(sv)
        xs, prev = sv["x1"], (sv["m2"], mod[5])

    sq, dxs, dm2, dg2 = _loss_fwd_bwd(xs, prev[0], prev[1], loss_target[0], name="loss")
    loss = lax.psum(0.5 * jnp.sum(sq) / d, ("x", "y", "c"))

    big = {n: [] for n in ("w_in", "w_out", "w1", "w2")}
    small = {n: [] for n in SMALL_NAMES}
    dmods = []
    for l in reversed(range(depth)):
        mod = [mod_me[l, kk] for kk in range(6)]
        sm = {n: w[n][l] for n in SMALL_NAMES}
        below = (saved[l - 1]["m2"], mod_me[l - 1, 5]) if l else None
        dxs, dm2, dg2, bg, smg, dmod = _layer_bwd(dxs, dm2, dg2, saved[l], mod, layer_w[l], sm, l,
                                                  below)
        for n in big:
            big[n].insert(0, bg[n])
        for n in SMALL_NAMES:
            small[n].insert(0, smg[n])
        dmods.insert(0, dmod)
    grad_x = dxs[None]

    out_g, out_d, out_m, out_v = {}, {}, {}, {}

    def run_adamw(name, parts2d, shape):
        rows, cols = parts2d.shape[1:]
        g, dl, nm, nv = _adamw(w[name].reshape(rows, cols), mom[name].reshape(rows, cols),
                               var[name].reshape(rows, cols), parts2d, name=f"adamw_{name}")
        out_g[name], out_d[name] = g.reshape(shape), dl.reshape(shape)
        out_m[name], out_v[name] = nm.reshape(shape), nv.reshape(shape)

    def shards_of(name, l):
        if name == "w_in":
            g = _w_in_from_internal(big["w_in"][l])
            return jnp.stack(jnp.split(g, n_chips, axis=1))
        if name == "mlp_w1":
            return jnp.stack(jnp.split(big["w1"][l], n_chips, axis=1))
        if name == "w_out":
            return big["w_out"][l].reshape(n_chips, d // n_chips, d)
        return big["w2"][l].reshape(n_chips, D_FF // n_chips, d)

    for name in ("w_in", "w_out", "mlp_w1", "mlp_w2"):
        per_chip = jnp.stack([shards_of(name, l) for l in range(depth)], axis=1)
        r, cols = per_chip.shape[2:]
        half_rows = half_l * r
        keep = lax.dynamic_slice_in_dim(per_chip, core * half_l, half_l, axis=1)
        send = lax.dynamic_slice_in_dim(per_chip, (1 - core) * half_l, half_l, axis=1)
        theirs = _core_swap(send.reshape(n_chips * half_rows, cols), name=f"pair_{name}")
        chip_sum = _add2(keep.reshape(n_chips * half_rows, cols), theirs, out_dtype=BF16,
                         name=f"pairsum_{name}")
        got = _scatter_chips(chip_sum.reshape(n_chips, half_rows, cols), name=f"scatter_{name}")
        half = _sum_slots(got, name=f"sum_{name}")
        both = _by_core(core, half, _core_swap(half, name=f"swap_{name}"), 0)
        run_adamw(name, both[None], w[name].shape)

    small_tree = {n: jnp.stack(small[n]) for n in SMALL_NAMES}
    gathered = _gather_all(_pack_small(small_tree), name="gather_small")
    gs, ds_, ms, vs = _adamw(_pack_small({n: w[n] for n in SMALL_NAMES}),
                             _pack_small({n: mom[n] for n in SMALL_NAMES}),
                             _pack_small({n: var[n] for n in SMALL_NAMES}), gathered,
                             name="adamw_small")
    like = {n: w[n] for n in SMALL_NAMES}
    for tree, packed in ((out_g, gs), (out_d, ds_), (out_m, ms), (out_v, vs)):
        tree.update(_unpack_small(packed, like))

    dmod_mine = jnp.concatenate(dmods, axis=0)
    dmod_all = _gather_all(jnp.zeros((depth, 8, 6 * d), F32).at[:, 0].set(dmod_mine),
                           name="gather_dmod")[:, :, 0]
    dmod_lb = dmod_all.transpose(1, 0, 2)
    dmod_sh = lax.dynamic_slice_in_dim(dmod_lb, chip * ada_sh, ada_sh, axis=2)
    dmod_sh = jnp.concatenate([dmod_sh, jnp.zeros_like(dmod_sh)], axis=1)
    g_ada_w = _ada_bwd(c_pad, dmod_sh, name="ada_bwd")
    run_adamw("ada_w", g_ada_w.reshape(1, depth * d, ada_sh), ada_w.shape)
    parts_b = dmod_all.reshape(8, depth * 6 * d // LANES, LANES)
    run_adamw("ada_b", parts_b, ada_b.shape)

    outs = [loss, grad_x]
    for tree in (out_g, out_d, out_m, out_v):
        outs += [tree[n] for n in WEIGHT_NAMES]
    return tuple(outs)
```

```python
import functools
import math

import jax
import jax.numpy as jnp
from jax import lax
from jax.experimental import pallas as pl
from jax.experimental.pallas import tpu as pltpu

F32 = jnp.float32
BF16 = jnp.bfloat16

D_MODEL = 1024
DEPTH = 4
HEAD_DIM = 64
LANES = 128
D_FF = 4 * D_MODEL
EPS = 1e-6
SB_W, FOX_W, SGU_W = 256, 512, 256
FOX_HEADS = 8
SGU_CHUNK = 128
IN_W = 2824
ATT_W = 3 * SB_W + 3 * FOX_W
PROJ_W = 3072
CB_QA, CB_KA, CB_VA = 0, 2, 4
CB_QB, CB_KB, CB_VB = 6, 10, 14
CB_UC, CB_VC, CB_FL = 18, 20, 22
ATT_T = 256
VMEM_LIMIT = 56 * 2 ** 20
SKIP_LOG = 110.0

ADAM_LR, ADAM_B1, ADAM_B2, ADAM_EPS, ADAM_WD, ADAM_STEP = 0.001, 0.9, 0.999, 1e-08, 0.01, 10

MESH = pl.DeviceIdType.MESH


def _pcall(body, *, name, out_shape, grid=(), in_specs=None, out_specs=None, scratch_shapes=(),
           semantics=None):
    params = dict(vmem_limit_bytes=VMEM_LIMIT)
    if semantics is not None:
        params["dimension_semantics"] = semantics
    kwargs = {}
    if in_specs is not None:
        kwargs["in_specs"] = in_specs
    if out_specs is not None:
        kwargs["out_specs"] = out_specs
    return pl.pallas_call(body, name=name, out_shape=out_shape, grid=grid,
                          scratch_shapes=list(scratch_shapes),
                          compiler_params=pltpu.CompilerParams(**params), **kwargs)


def _dot(a, b):
    return jnp.dot(a, b, preferred_element_type=F32)


def _dot_nt(a, b):
    return lax.dot_general(a, b, (((1,), (1,)), ((), ())), preferred_element_type=F32)


def _dot_tn(a, b):
    return lax.dot_general(a, b, (((0,), (0,)), ((), ())), preferred_element_type=F32)


def _split2(x):
    hi = x.astype(BF16)
    lo = (x - hi.astype(F32)).astype(BF16)
    return hi, lo


def _ones_dot(x, ones_bf16):
    hi, lo = _split2(x)
    return _dot(hi, ones_bf16) + _dot(lo, ones_bf16)


def _rowwise(fn, fulls, vecs, out_dtypes, n_vec_out, *, name, tr):
    s, n = fulls[0].shape
    tr = min(tr, s)
    assert s % tr == 0, (name, s, tr)
    nf, nv, nfo = len(fulls), len(vecs), len(out_dtypes)

    def body(*refs):
        fi, vi = refs[:nf], refs[nf:nf + nv]
        fo, vo = refs[nf + nv:nf + nv + nfo], refs[nf + nv + nfo:]
        outs_f, outs_v = fn([r[...] for r in fi], [r[...] for r in vi])
        for r, o in zip(fo, outs_f):
            r[...] = o.astype(r.dtype)
        if n_vec_out:
            @pl.when(pl.program_id(0) == 0)
            def _():
                for r in vo:
                    r[...] = jnp.zeros_like(r)
            for r, o in zip(vo, outs_v):
                r[...] += o

    full_spec = pl.BlockSpec((tr, n), lambda i: (i, 0))
    vec_specs = [pl.BlockSpec(v.shape, lambda i: (0, 0)) for v in vecs]
    out_vec_spec = pl.BlockSpec((1, n), lambda i: (0, 0))
    out_shape = [jax.ShapeDtypeStruct((s, n), dt) for dt in out_dtypes]
    out_shape += [jax.ShapeDtypeStruct((1, n), F32)] * n_vec_out
    outs = _pcall(body, name=name, grid=(s // tr,),
                  in_specs=[full_spec] * nf + vec_specs,
                  out_specs=[full_spec] * nfo + [out_vec_spec] * n_vec_out,
                  out_shape=out_shape,
                  semantics=("arbitrary",) if n_vec_out else ("parallel",))(*fulls, *vecs)
    return outs[:nfo], outs[nfo:]


def _colsum(t):
    return jnp.sum(t, axis=0, keepdims=True)


def _rms_mod(x, g, sc, sh):
    r = lax.rsqrt(jnp.mean(x * x, axis=-1, keepdims=True) + EPS)
    return (x * r * g) * (1.0 + sc) + sh


def _norm_mod_fwd(x, g, sc, sh, *, name):
    def fn(f, v):
        return [_rms_mod(f[0], v[0], v[1], v[2])], []
    (h,), _ = _rowwise(fn, [x], [g, sc, sh], [BF16], 0, name=name, tr=512)
    return h


def _resid_norm_mod_fwd(x, m, gate, g, sc, sh, *, name):
    def fn(f, v):
        xn = f[0] + v[0] * f[1]
        return [xn, _rms_mod(xn, v[1], v[2], v[3])], []
    (xn, h), _ = _rowwise(fn, [x, m], [gate, g, sc, sh], [F32, BF16], 0, name=name, tr=512)
    return xn, h


def _norm_mod_bwd(x, dh, dres, g, sc, gated, *, name):
    def fn(f, v):
        xv, dhv, dr = f[:3]
        gv, scv = v[:2]
        r = lax.rsqrt(jnp.mean(xv * xv, axis=-1, keepdims=True) + EPS)
        xh = xv * r
        dn = dhv * (1.0 + scv)
        dxh = dn * gv
        dx = dr + r * (dxh - xh * jnp.mean(dxh * xh, axis=-1, keepdims=True))
        sums = [_colsum(dn * xh), _colsum(dhv * (xh * gv)), _colsum(dhv)]
        if gated is None:
            return [dx], sums
        return [dx, dx * v[2]], sums + [_colsum(dx * f[3])]
    if gated is None:
        (dx,), (dg, dsc, dsh) = _rowwise(fn, [x, dh, dres], [g, sc], [F32], 3, name=name, tr=256)
        return dx, dg, dsc, dsh, None, None
    (dx, dm), (dg, dsc, dsh, dgate) = _rowwise(fn, [x, dh, dres, gated[0]], [g, sc, gated[1]],
                                               [F32, BF16], 4, name=name, tr=256)
    return dx, dg, dsc, dsh, dm, dgate


def _loss_fwd_bwd(x, m, gate, target, *, name):
    n = x.shape[1]

    def fn(f, v):
        err = f[0] + v[0] * f[1] - f[2]
        dy = err * (1.0 / n)
        return [dy, dy * v[0]], [_colsum(err * err), _colsum(dy * f[1])]
    (dy, dm), (sq, dgate) = _rowwise(fn, [x, m, target], [gate], [F32, BF16], 2, name=name, tr=512)
    return sq, dy, dm, dgate


def _matmul(a, b, *, name, ta=False, tb=False, out_dtype=F32, relu2=None, pre_act=None,
            tm=1024, tn=1024, tk_max=2048):
    m = a.shape[1] if ta else a.shape[0]
    k = a.shape[0] if ta else a.shape[1]
    n = b.shape[0] if tb else b.shape[1]
    assert k == (b.shape[1] if tb else b.shape[0])
    tk = max(dd for dd in range(LANES, min(tk_max, k) + 1, LANES) if k % dd == 0) if k > LANES else k
    tm, tn = min(tm, m), min(tn, n)
    assert m % tm == 0 and n % tn == 0 and k % tk == 0, (name, m, n, k)
    nk = k // tk
    dims = (((0 if ta else 1,), (1 if tb else 0,)), ((), ()))

    plain = relu2 is None and pre_act is None
    in_place = plain and out_dtype == F32
    n_in = 2 + (pre_act is not None)

    def body(*refs):
        a_ref, b_ref = refs[:2]
        o_ref = refs[n_in]
        prod = lax.dot_general(a_ref[...].astype(BF16), b_ref[...].astype(BF16), dims,
                               preferred_element_type=F32)

        def finish(acc):
            if pre_act is not None:
                acc = acc * (2.0 * jnp.maximum(refs[2][...].astype(F32), 0.0))
            o_ref[...] = acc.astype(o_ref.dtype)
            if relu2 is not None:
                r = jnp.maximum(acc, 0.0)
                refs[n_in + 1][...] = (r * r).astype(relu2)

        if nk == 1:
            finish(prod)
            return
        kk = pl.program_id(2)
        acc_ref = o_ref if in_place else refs[-1]

        @pl.when(kk == 0)
        def _():
            acc_ref[...] = prod

        @pl.when(kk > 0)
        def _():
            acc_ref[...] += prod

        if not in_place:
            @pl.when(kk == nk - 1)
            def _():
                finish(acc_ref[...])

    a_spec = (pl.BlockSpec((tk, tm), lambda i, j, kk: (kk, i)) if ta
              else pl.BlockSpec((tm, tk), lambda i, j, kk: (i, kk)))
    b_spec = (pl.BlockSpec((tn, tk), lambda i, j, kk: (j, kk)) if tb
              else pl.BlockSpec((tk, tn), lambda i, j, kk: (kk, j)))
    out_spec = pl.BlockSpec((tm, tn), lambda i, j, kk: (i, j))
    in_specs, args = [a_spec, b_spec], [a, b]
    if pre_act is not None:
        in_specs.append(out_spec)
        args.append(pre_act)
    out_specs, out_shape = out_spec, jax.ShapeDtypeStruct((m, n), out_dtype)
    if relu2 is not None:
        out_specs, out_shape = [out_spec] * 2, [out_shape, jax.ShapeDtypeStruct((m, n), relu2)]
    return _pcall(body, name=name, grid=(m // tm, n // tn, nk),
                  in_specs=in_specs, out_specs=out_specs, out_shape=out_shape,
                  scratch_shapes=[] if nk == 1 or in_place else [pltpu.VMEM((tm, tn), F32)],
                  semantics=("parallel", "parallel", "arbitrary"))(*args)


def _lane_masks():
    lane = lax.broadcasted_iota(jnp.int32, (1, LANES), 1)
    return [lane < HEAD_DIM, lane >= HEAD_DIM]


def _tri_iotas(t):
    r = lax.broadcasted_iota(jnp.int32, (t, t), 0)
    c = lax.broadcasted_iota(jnp.int32, (t, t), 1)
    return r, c


def _rows(j, t):
    return pl.ds(pl.multiple_of(j * t, t), t)


def _neg_softplus(z):
    e = jnp.exp(-jnp.abs(z))
    return -(jnp.maximum(z, 0.0) + jnp.log(1.0 + e)), e


def _sb_fwd(proj, *, name):
    s = proj.shape[0]
    t = min(ATT_T, s)
    scale = HEAD_DIM ** -0.5

    def body(q_ref, k_ref, v_ref, o_ref, ltot_ref, stop_ref):
        i = pl.program_id(1)
        hm = _lane_masks()
        q = q_ref[...] * scale
        qh = [jnp.where(mk, q, 0.0).astype(BF16) for mk in hm]
        r, c = _tri_iotas(t)
        later = (r > c).astype(BF16)
        q2 = jnp.concatenate(qh, axis=0)
        causal2 = jnp.concatenate([c < r, c < r], axis=0)

        def scores(j):
            return _dot_nt(q2, k_ref[_rows(j, t), :].astype(BF16))

        def chunk(j, carry, z, masked):
            e_run, acc = carry
            vb = v_ref[_rows(j, t), :].astype(BF16)
            l, _ = _neg_softplus(z)
            if masked:
                l = jnp.where(causal2, l, 0.0)
            between = _ones_dot(l, later) + e_run
            a = jnp.exp(z + l + between)
            if masked:
                a = jnp.where(causal2, a, 0.0)
            return e_run + jnp.sum(l, axis=1, keepdims=True), acc + _dot(a.astype(BF16), vb)

        init = (jnp.zeros((2 * t, 1), F32), jnp.zeros((2 * t, LANES), F32))
        carry = chunk(i, init, scores(i), True)

        def step(st):
            j, cr, z = st
            z_next = scores(jnp.maximum(j - 1, 0))
            return j - 1, chunk(j, cr, z, False), z_next

        j_stop, (e_tot, acc), _ = lax.while_loop(
            lambda st: (st[0] >= 0) & (jnp.max(st[1][0]) > -SKIP_LOG), step,
            (i - 1, carry, scores(jnp.maximum(i - 1, 0))))
        o_ref[...] = jnp.where(hm[0], acc[:t], acc[t:])
        ltot_ref[...] = jnp.where(hm[0], e_tot[:t], e_tot[t:])
        stop_ref[...] = jnp.full(stop_ref.shape, j_stop.astype(F32), F32)

    blk = lambda cb: pl.BlockSpec((t, LANES), lambda p, i: (i, cb + p))
    full = lambda cb: pl.BlockSpec((s, LANES), lambda p, i: (0, cb + p))
    out_blk = pl.BlockSpec((t, LANES), lambda p, i: (i, p))
    n_pairs = SB_W // LANES
    return _pcall(body, name=name, grid=(n_pairs, s // t),
                  in_specs=[blk(CB_QA), full(CB_KA), full(CB_VA)],
                  out_specs=[out_blk, out_blk,
                             pl.BlockSpec((1, 1, 8, LANES), lambda p, i: (p, i, 0, 0))],
                  out_shape=[jax.ShapeDtypeStruct((s, SB_W), F32)] * 2
                  + [jax.ShapeDtypeStruct((n_pairs, s // t, 8, LANES), F32)],
                  semantics=("parallel", "arbitrary"))(proj, proj, proj)


def _sb_bwd(proj, dmixed, ltot, stop, *, name):
    s = proj.shape[0]
    t = min(ATT_T, s)
    scale = HEAD_DIM ** -0.5

    def body(q_ref, k_ref, v_ref, do_ref, ltot_ref, stop_ref, dq_ref, dk_ref, dv_ref):
        i = pl.program_id(1)

        @pl.when(i == 0)
        def _():
            dk_ref[...] = jnp.zeros_like(dk_ref)
            dv_ref[...] = jnp.zeros_like(dv_ref)

        hm = _lane_masks()
        q = q_ref[...] * scale
        do = do_ref[...]
        qh = [jnp.where(mk, q, 0.0).astype(BF16) for mk in hm]
        doh = [jnp.where(mk, do, 0.0).astype(BF16) for mk in hm]
        r, c = _tri_iotas(t)
        upto = (r <= c).astype(BF16)
        before = (r < c).astype(BF16)
        q2 = jnp.concatenate(qh, axis=0)
        do2 = jnp.concatenate(doh, axis=0)
        causal2 = jnp.concatenate([c < r, c < r], axis=0)

        j_stop = jnp.clip(jnp.max(stop_ref[...]).astype(jnp.int32), -1, i - 1)
        ltv = ltot_ref[...]
        lt = jnp.concatenate([ltv[:, 0:1], ltv[:, HEAD_DIM:HEAD_DIM + 1]], axis=0)

        def products(j):
            return (_dot_nt(q2, k_ref[_rows(j, t), :].astype(BF16)),
                    _dot_nt(do2, v_ref[_rows(j, t), :].astype(BF16)))

        def chunk(j, carry, z, da, masked):
            l_run, g_run, dq = carry
            l, e = _neg_softplus(z)
            beta = jnp.where(z >= 0.0, 1.0, e) / (1.0 + e)
            if masked:
                l = jnp.where(causal2, l, 0.0)
            prefix = _ones_dot(l, upto) + l_run
            a = jnp.exp(z + l + (lt - prefix))
            if masked:
                a = jnp.where(causal2, a, 0.0)
            g = a * da
            g_before = _ones_dot(g, before) + g_run
            dz = g * (1.0 - beta) - beta * g_before
            if masked:
                dz = jnp.where(causal2, dz, 0.0)
            dzb = dz.astype(BF16)
            dk_ref[_rows(j, t), :] += _dot_tn(dzb, q2)
            dv_ref[_rows(j, t), :] += _dot_tn(a.astype(BF16), do2)
            return (l_run + jnp.sum(l, axis=1, keepdims=True),
                    g_run + jnp.sum(g, axis=1, keepdims=True),
                    dq + _dot(dzb, k_ref[_rows(j, t), :].astype(BF16)))

        init = (jnp.zeros((2 * t, 1), F32), jnp.zeros((2 * t, 1), F32),
                jnp.zeros((2 * t, LANES), F32))
        carry = lax.fori_loop(j_stop + 1, i,
                              lambda j, cr: chunk(j, cr, *products(j), False), init)
        dq2 = chunk(i, carry, *products(i), True)[2]
        dq_ref[...] = jnp.where(hm[0], dq2[:t], dq2[t:]) * scale

    blk = lambda cb: pl.BlockSpec((t, LANES), lambda p, i: (i, cb + p))
    full = lambda cb: pl.BlockSpec((s, LANES), lambda p, i: (0, cb + p))
    out_blk = pl.BlockSpec((t, LANES), lambda p, i: (i, p))
    out_full = pl.BlockSpec((s, LANES), lambda p, i: (0, p))
    return _pcall(body, name=name, grid=(SB_W // LANES, s // t),
                  in_specs=[blk(CB_QA), full(CB_KA), full(CB_VA), blk(0), out_blk,
                            pl.BlockSpec((1, 1, 8, LANES), lambda p, i: (p, i, 0, 0))],
                  out_specs=[out_blk, out_full, out_full],
                  out_shape=[jax.ShapeDtypeStruct((s, SB_W), F32)] * 3,
                  semantics=("parallel", "arbitrary"))(proj, proj, proj, dmixed, ltot, stop)


def _group_mean(v, lo):
    s0 = jnp.sum(jnp.where(lo, v, 0.0), axis=1, keepdims=True)
    s1 = jnp.sum(jnp.where(lo, 0.0, v), axis=1, keepdims=True)
    return jnp.where(lo, s0, s1) * (1.0 / HEAD_DIM)


def _fox_prep_fwd(proj, qg, kg, *, name):
    s = proj.shape[0]
    tr = min(512, s)

    def body(q_ref, k_ref, qg_ref, kg_ref, qn_ref, kn_ref, kmax_ref):
        lo = _lane_masks()[0]
        for x_ref, g_ref, o_ref in ((q_ref, qg_ref, qn_ref), (k_ref, kg_ref, kn_ref)):
            x = x_ref[...]
            o_ref[...] = x * lax.rsqrt(_group_mean(x * x, lo) + EPS) * g_ref[...]

        @pl.when(pl.program_id(1) == 0)
        def _():
            kmax_ref[...] = jnp.zeros_like(kmax_ref)
        kn = kn_ref[...]
        norms = jnp.sqrt(_group_mean(kn * kn, lo) * HEAD_DIM)
        kmax_ref[...] = jnp.maximum(kmax_ref[...], jnp.max(norms, axis=0, keepdims=True))

    blk = lambda cb: pl.BlockSpec((tr, LANES), lambda p, i: (i, cb + p))
    vec = pl.BlockSpec((1, LANES), lambda p, i: (0, 0))
    out_blk = pl.BlockSpec((tr, LANES), lambda p, i: (i, p))
    return _pcall(body, name=name, grid=(FOX_W // LANES, s // tr),
                  in_specs=[blk(CB_QB), blk(CB_KB), vec, vec],
                  out_specs=[out_blk, out_blk, pl.BlockSpec((1, LANES), lambda p, i: (0, p))],
                  out_shape=[jax.ShapeDtypeStruct((s, FOX_W), F32)] * 2
                  + [jax.ShapeDtypeStruct((1, FOX_W), F32)],
                  semantics=("parallel", "arbitrary"))(proj, proj, qg, kg)


def _fox_prep_bwd(proj, dqn, dkn, qg, kg, *, name):
    s = proj.shape[0]
    tr = min(512, s)

    def body(q_ref, k_ref, dqn_ref, dkn_ref, qg_ref, kg_ref, dq_ref, dk_ref, dqg_ref, dkg_ref):
        @pl.when((pl.program_id(0) == 0) & (pl.program_id(1) == 0))
        def _():
            dqg_ref[...] = jnp.zeros_like(dqg_ref)
            dkg_ref[...] = jnp.zeros_like(dkg_ref)

        lo = _lane_masks()[0]
        for x_ref, dy_ref, g_ref, dx_ref, dg_ref in ((q_ref, dqn_ref, qg_ref, dq_ref, dqg_ref),
                                                     (k_ref, dkn_ref, kg_ref, dk_ref, dkg_ref)):
            x, dy = x_ref[...], dy_ref[...]
            r = lax.rsqrt(_group_mean(x * x, lo) + EPS)
            xh = x * r
            dxh = dy * g_ref[...]
            dx_ref[...] = r * (dxh - xh * _group_mean(dxh * xh, lo))
            dg_ref[...] += _colsum(dy * xh)

    blk = lambda cb: pl.BlockSpec((tr, LANES), lambda p, i: (i, cb + p))
    vec = pl.BlockSpec((1, LANES), lambda p, i: (0, 0))
    out_blk = pl.BlockSpec((tr, LANES), lambda p, i: (i, p))
    return _pcall(body, name=name, grid=(FOX_W // LANES, s // tr),
                  in_specs=[blk(CB_QB), blk(CB_KB), out_blk, out_blk, vec, vec],
                  out_specs=[out_blk, out_blk, vec, vec],
                  out_shape=[jax.ShapeDtypeStruct((s, FOX_W), F32)] * 2
                  + [jax.ShapeDtypeStruct((1, LANES), F32)] * 2,
                  semantics=("arbitrary", "arbitrary"))(proj, proj, dqn, dkn, qg, kg)


def _split3_dot(tri_bf16, x):
    hi = x.astype(BF16)
    r1 = x - hi.astype(F32)
    mid = r1.astype(BF16)
    lo = (r1 - mid.astype(F32)).astype(BF16)
    return _dot(tri_bf16, hi) + _dot(tri_bf16, mid) + _dot(tri_bf16, lo)


def _forget_cumsum_fwd(proj, b_pad, *, name):
    s = proj.shape[0]
    tb = min(256, s)

    def body(fl_ref, b_ref, cf_ref, run_ref):
        @pl.when(pl.program_id(0) == 0)
        def _():
            run_ref[...] = jnp.zeros_like(run_ref)
        lf, _ = _neg_softplus(-(fl_ref[...] + b_ref[...]))
        r, c = _tri_iotas(tb)
        incl = _split3_dot((c <= r).astype(BF16), lf) + run_ref[...]
        cf_ref[...] = incl
        run_ref[...] = incl[tb - 1:tb, :]

    return _pcall(body, name=name, grid=(s // tb,),
                  in_specs=[pl.BlockSpec((tb, LANES), lambda i: (i, CB_FL)),
                            pl.BlockSpec((1, LANES), lambda i: (0, 0))],
                  out_specs=pl.BlockSpec((tb, LANES), lambda i: (i, 0)),
                  out_shape=jax.ShapeDtypeStruct((s, LANES), F32),
                  scratch_shapes=[pltpu.VMEM((1, LANES), F32)],
                  semantics=("arbitrary",))(proj, b_pad)


def _forget_cumsum_bwd(proj, b_pad, dcf, *, name):
    s = proj.shape[0]
    tb = min(256, s)
    nb = s // tb

    def body(fl_ref, b_ref, dcf_ref, dfl_ref, db_ref, run_ref):
        @pl.when(pl.program_id(0) == 0)
        def _():
            run_ref[...] = jnp.zeros_like(run_ref)
            db_ref[...] = jnp.zeros_like(db_ref)
        r, c = _tri_iotas(tb)
        dlf = _split3_dot((c >= r).astype(BF16), dcf_ref[...]) + run_ref[...]
        run_ref[...] = dlf[0:1, :]
        xv = fl_ref[...] + b_ref[...]
        e = jnp.exp(-jnp.abs(xv))
        sig_neg = jnp.where(xv >= 0.0, e, 1.0) / (1.0 + e)
        dfl = dlf * sig_neg
        dfl_ref[...] = dfl
        db_ref[...] += _colsum(dfl)

    return _pcall(body, name=name, grid=(nb,),
                  in_specs=[pl.BlockSpec((tb, LANES), lambda i: (nb - 1 - i, CB_FL)),
                            pl.BlockSpec((1, LANES), lambda i: (0, 0)),
                            pl.BlockSpec((tb, LANES), lambda i: (nb - 1 - i, 0))],
                  out_specs=[pl.BlockSpec((tb, LANES), lambda i: (nb - 1 - i, 0)),
                             pl.BlockSpec((1, LANES), lambda i: (0, 0))],
                  out_shape=[jax.ShapeDtypeStruct((s, LANES), F32),
                             jax.ShapeDtypeStruct((1, LANES), F32)],
                  scratch_shapes=[pltpu.VMEM((1, LANES), F32)],
                  semantics=("arbitrary",))(proj, b_pad, dcf)


def _fox_bias_q(cfc, p, h):
    lane = lax.broadcasted_iota(jnp.int32, (1, LANES), 1)
    return jnp.sum(jnp.where(lane == 2 * p + h, cfc, 0.0), axis=1, keepdims=True)


def _fox_score_bound(q, kmax_row, hm):
    out = []
    for h in range(2):
        qnorm = jnp.sqrt(jnp.sum(jnp.where(hm[h], q * q, 0.0), axis=1, keepdims=True))
        out.append(1.02 * qnorm * kmax_row[:, h * HEAD_DIM:h * HEAD_DIM + 1])
    return out


def _fox_live(cfr_ref, j, t, tops):
    jc = jnp.maximum(j, 0)
    worst = []
    for h in range(2):
        cf_min = jnp.min(cfr_ref[0, pl.ds(h, 1), _rows(jc, t)], axis=1, keepdims=True)
        worst.append(jnp.max(tops[h] - cf_min))
    return (j >= 0) & (jnp.maximum(worst[0], worst[1]) > -SKIP_LOG)


def _chip_gather_copies(x_refs, out_refs, send_sems, recv_sems, local_sems):
    ids = (lax.axis_index("x"), lax.axis_index("y"), lax.axis_index("c"))
    chip = 2 * ids[0] + ids[1]
    copies = []
    for n, (x_ref, out_ref) in enumerate(zip(x_refs, out_refs)):
        copies.append(pltpu.make_async_copy(x_ref, out_ref.at[chip], local_sems.at[n]))
        for kk, (flip_x, flip_y) in enumerate(((1, 0), (0, 1), (1, 1))):
            peer = (1 - ids[0] if flip_x else ids[0], 1 - ids[1] if flip_y else ids[1], ids[2])
            copies.append(pltpu.make_async_remote_copy(
                src_ref=x_ref, dst_ref=out_ref.at[chip],
                send_sem=send_sems.at[3 * n + kk], recv_sem=recv_sems.at[3 * n + kk],
                device_id=peer, device_id_type=MESH))
    return copies


def _fox_fwd(proj, qn, kn, cf, cf_rows, kmax, *, name, gathers=()):
    s = proj.shape[0]
    t = min(ATT_T, s)
    scale = HEAD_DIM ** -0.5
    n_pairs, nq, ng = FOX_W // LANES, s // t, len(gathers)

    def body(*refs):
        q_ref, k_ref, v_ref, cfc_ref, cfr_ref, kmax_ref = refs[:6]
        o_ref, lse_ref = refs[6 + ng:8 + ng]
        p, i = pl.program_id(0), pl.program_id(1)
        if ng:
            def copies():
                return _chip_gather_copies(refs[6:6 + ng], refs[8 + ng:8 + 2 * ng], *refs[8 + 2 * ng:])

            @pl.when((p == 0) & (i == 0))
            def _():
                for cp in copies():
                    cp.start()
        hm = _lane_masks()
        q = q_ref[...] * scale
        qh = [jnp.where(mk, q, 0.0).astype(BF16) for mk in hm]
        cfc = cfc_ref[...]
        bq = [_fox_bias_q(cfc, p, h) for h in range(2)]
        qk_top = _fox_score_bound(q, kmax_ref[...], hm)
        r, c = _tri_iotas(t)
        causal = c <= r

        q2 = jnp.concatenate(qh, axis=0)
        causal2 = jnp.concatenate([causal, causal], axis=0)

        def scores(j):
            return _dot_nt(q2, k_ref[_rows(j, t), :].astype(BF16))

        def chunk(j, carry, z2, masked):
            m_run, l_run, acc = carry
            vb = v_ref[_rows(j, t), :].astype(BF16)
            z = jnp.concatenate(
                [z2[h * t:(h + 1) * t] + (bq[h] - cfr_ref[0, pl.ds(h, 1), _rows(j, t)])
                 for h in range(2)], axis=0)
            if masked:
                z = jnp.where(causal2, z, -1e30)
            m_new = jnp.maximum(m_run, jnp.max(z, axis=1, keepdims=True))
            alpha = jnp.exp(m_run - m_new)
            pr = jnp.exp(z - m_new)
            return (m_new, alpha * l_run + jnp.sum(pr, axis=1, keepdims=True),
                    alpha * acc + _dot(pr.astype(BF16), vb))

        init = (jnp.full((2 * t, 1), -1e30, F32), jnp.zeros((2 * t, 1), F32),
                jnp.zeros((2 * t, LANES), F32))
        carry = chunk(i, init, scores(i), True)

        def live(j, cr):
            return _fox_live(cfr_ref, j, t,
                             [qk_top[h] + bq[h] - cr[0][h * t:(h + 1) * t] for h in range(2)])

        def step(st):
            j, _, cr, z2 = st
            z2_next = scores(jnp.maximum(j - 1, 0))
            cr = chunk(j, cr, z2, False)
            return j - 1, live(j - 1, cr), cr, z2_next

        m_fin, l_fin, acc = lax.while_loop(
            lambda st: st[1], step,
            (i - 1, live(i - 1, carry), carry, scores(jnp.maximum(i - 1, 0))))[2]
        o2 = acc / l_fin
        lse2 = m_fin + jnp.log(l_fin)
        o_ref[...] = jnp.where(hm[0], o2[:t], o2[t:])
        lse_ref[...] = jnp.where(hm[0], lse2[:t], lse2[t:])
        if ng:
            @pl.when((p == n_pairs - 1) & (i == nq - 1))
            def _():
                for cp in copies():
                    cp.wait()

    blk = pl.BlockSpec((t, LANES), lambda p, i: (i, p))
    full = pl.BlockSpec((s, LANES), lambda p, i: (0, p))
    any_spec = pl.BlockSpec(memory_space=pl.ANY)
    dma = pltpu.SemaphoreType.DMA
    return _pcall(body, name=name, grid=(n_pairs, nq),
                  in_specs=[blk, full, pl.BlockSpec((s, LANES), lambda p, i: (0, CB_VB + p)),
                            pl.BlockSpec((t, LANES), lambda p, i: (i, 0)),
                            pl.BlockSpec((1, 2, s), lambda p, i: (p, 0, 0)),
                            pl.BlockSpec((1, LANES), lambda p, i: (0, p))] + [any_spec] * ng,
                  out_specs=[blk, blk] + [any_spec] * ng,
                  out_shape=[jax.ShapeDtypeStruct((s, FOX_W), F32)] * 2
                  + [jax.ShapeDtypeStruct((4,) + g.shape, g.dtype) for g in gathers],
                  scratch_shapes=[dma((3 * ng,)), dma((3 * ng,)), dma((ng,))] if ng else [],
                  semantics=("arbitrary", "arbitrary") if ng else ("parallel", "arbitrary"))(
                      qn, kn, proj, cf, cf_rows, kmax, *gathers)


def _fox_bwd(proj, qn, kn, cf, cf_rows, kmax, do, o, lse, *, name):
    s = proj.shape[0]
    t = min(ATT_T, s)
    scale = HEAD_DIM ** -0.5

    def body(q_ref, k_ref, v_ref, cfc_ref, cfr_ref, kmax_ref, do_ref, o_ref, lse_ref,
             dq_ref, dk_ref, dv_ref, dcf_ref, dcfq_ref):
        p, i = pl.program_id(0), pl.program_id(1)

        @pl.when(i == 0)
        def _():
            dk_ref[...] = jnp.zeros_like(dk_ref)
            dv_ref[...] = jnp.zeros_like(dv_ref)
            dcf_ref[...] = jnp.zeros_like(dcf_ref)

        hm = _lane_masks()
        q = q_ref[...] * scale
        do = do_ref[...]
        dov = do * o_ref[...]
        qh = [jnp.where(mk, q, 0.0).astype(BF16) for mk in hm]
        doh = [jnp.where(mk, do, 0.0).astype(BF16) for mk in hm]
        delta = [jnp.sum(jnp.where(mk, dov, 0.0), axis=1, keepdims=True) for mk in hm]
        lsev = lse_ref[...]
        lse = [lsev[:, 0:1], lsev[:, HEAD_DIM:HEAD_DIM + 1]]
        cfc = cfc_ref[...]
        bq = [_fox_bias_q(cfc, p, h) - lse[h] for h in range(2)]
        qk_top = _fox_score_bound(q, kmax_ref[...], hm)
        tops = [qk_top[h] + bq[h] for h in range(2)]
        r, c = _tri_iotas(t)
        j_stop = lax.while_loop(lambda st: st[1],
                                lambda st: (st[0] - 1, _fox_live(cfr_ref, st[0] - 1, t, tops)),
                                (i - 1, _fox_live(cfr_ref, i - 1, t, tops)))[0]
        q2 = jnp.concatenate(qh, axis=0)
        do2 = jnp.concatenate(doh, axis=0)
        delta2 = jnp.concatenate(delta, axis=0)
        causal2 = jnp.concatenate([c <= r, c <= r], axis=0)

        def products(j):
            return (_dot_nt(q2, k_ref[_rows(j, t), :].astype(BF16)),
                    _dot_nt(do2, v_ref[_rows(j, t), :].astype(BF16)))

        def chunk(j, carry, z2, dp, masked):
            dq, row_sum = carry
            z = jnp.concatenate(
                [z2[h * t:(h + 1) * t] + (bq[h] - cfr_ref[0, pl.ds(h, 1), _rows(j, t)])
                 for h in range(2)], axis=0)
            pr = jnp.exp(z)
            if masked:
                pr = jnp.where(causal2, pr, 0.0)
            ds = pr * (dp - delta2)
            dsb = ds.astype(BF16)
            dk_ref[_rows(j, t), :] += _dot_tn(dsb, q2)
            dv_ref[_rows(j, t), :] += _dot_tn(pr.astype(BF16), do2)
            for h in range(2):
                dcf_ref[0, pl.ds(h, 1), _rows(j, t)] -= jnp.sum(ds[h * t:(h + 1) * t], axis=0,
                                                               keepdims=True)
            return (dq + _dot(dsb, k_ref[_rows(j, t), :].astype(BF16)),
                    row_sum + jnp.sum(ds, axis=1, keepdims=True))

        init = (jnp.zeros((2 * t, LANES), F32), jnp.zeros((2 * t, 1), F32))
        carry = lax.fori_loop(j_stop + 1, i,
                              lambda j, cr: chunk(j, cr, *products(j), False), init)
        dq2, row_sum = chunk(i, carry, *products(i), True)
        dq_ref[...] = jnp.where(hm[0], dq2[:t], dq2[t:]) * scale
        dcfq_ref[...] = jnp.where(hm[0], row_sum[:t], row_sum[t:])

    blk = pl.BlockSpec((t, LANES), lambda p, i: (i, p))
    full = pl.BlockSpec((s, LANES), lambda p, i: (0, p))
    rows = pl.BlockSpec((1, 2, s), lambda p, i: (p, 0, 0))
    return _pcall(body, name=name, grid=(FOX_W // LANES, s // t),
                  in_specs=[blk, full, pl.BlockSpec((s, LANES), lambda p, i: (0, CB_VB + p)),
                            pl.BlockSpec((t, LANES), lambda p, i: (i, 0)), rows,
                            pl.BlockSpec((1, LANES), lambda p, i: (0, p)),
                            pl.BlockSpec((t, LANES), lambda p, i: (i, SB_W // LANES + p)),
                            blk, blk],
                  out_specs=[blk, full, full, rows, blk],
                  out_shape=[jax.ShapeDtypeStruct((s, FOX_W), F32)] * 3
                  + [jax.ShapeDtypeStruct((FOX_W // LANES, 2, s), F32),
                     jax.ShapeDtypeStruct((s, FOX_W), F32)],
                  semantics=("parallel", "arbitrary"))(qn, kn, proj, cf, cf_rows, kmax, do, o, lse)


_GELU_C0 = math.sqrt(2.0 / math.pi)
_GELU_C1 = 0.044715


def _gelu(x):
    th = jnp.tanh(_GELU_C0 * (x + _GELU_C1 * (x * x * x)))
    return 0.5 * x * (1.0 + th), th


def _gelu_grad(x, th):
    return 0.5 * (1.0 + th) + 0.5 * x * (1.0 - th * th) * (_GELU_C0 * (1.0 + 3.0 * _GELU_C1 * x * x))


def _sgu_mix(wm, vn_c, lo, bcol):
    return jnp.where(lo, _dot(wm[0], vn_c) + bcol[0], _dot(wm[1], vn_c) + bcol[1])


def _sgu_fwd(proj, w, b_cols, gn, *, name):
    s = proj.shape[0]
    tr = min(512, s)
    ch = SGU_CHUNK

    def body(u_ref, v_ref, w_ref, b_ref, gn_ref, o_ref):
        lo = _lane_masks()[0]
        r, c = _tri_iotas(ch)
        wm = [jnp.where(c <= r, w_ref[h], 0.0).astype(BF16) for h in range(2)]
        bcol = [b_ref[0, :, h:h + 1] for h in range(2)]
        for n in range(tr // ch):
            rows = slice(n * ch, (n + 1) * ch)
            u, _ = _gelu(u_ref[rows, :])
            vg, _ = _gelu(v_ref[rows, :])
            vn = vg * lax.rsqrt(_group_mean(vg * vg, lo) + EPS) * gn_ref[0]
            o_ref[rows, :] = u * _sgu_mix(wm, vn.astype(BF16), lo, bcol)

    blk = lambda cb: pl.BlockSpec((tr, LANES), lambda p, i: (i, cb + p))
    return _pcall(body, name=name, grid=(SGU_W // LANES, s // tr),
                  in_specs=[blk(CB_UC), blk(CB_VC),
                            pl.BlockSpec((2, ch, ch), lambda p, i: (p, 0, 0)),
                            pl.BlockSpec((1, ch, 2), lambda p, i: (p, 0, 0)),
                            pl.BlockSpec((1, 1, LANES), lambda p, i: (p, 0, 0))],
                  out_specs=pl.BlockSpec((tr, LANES), lambda p, i: (i, p)),
                  out_shape=jax.ShapeDtypeStruct((s, SGU_W), F32),
                  semantics=("parallel", "parallel"))(proj, proj, w, b_cols, gn)


def _sgu_bwd(proj, dmixed, w, w_t, b_cols, gn, *, name):
    s = proj.shape[0]
    tr = min(512, s)
    ch = SGU_CHUNK
    cb_do = (SB_W + FOX_W) // LANES

    def body(u_ref, v_ref, do_ref, w_ref, wt_ref, b_ref, gn_ref,
             du_ref, dv_ref, dw_ref, db_ref, dgn_ref):
        @pl.when(pl.program_id(1) == 0)
        def _():
            dw_ref[...] = jnp.zeros_like(dw_ref)
            db_ref[...] = jnp.zeros_like(db_ref)
            dgn_ref[...] = jnp.zeros_like(dgn_ref)

        hm = _lane_masks()
        lo = hm[0]
        r, c = _tri_iotas(ch)
        wm = [jnp.where(c <= r, w_ref[h], 0.0).astype(BF16) for h in range(2)]
        wtm = [jnp.where(r <= c, wt_ref[h], 0.0).astype(BF16) for h in range(2)]
        bcol = [b_ref[0, :, h:h + 1] for h in range(2)]
        gnv = gn_ref[0]
        for n in range(tr // ch):
            rows = slice(n * ch, (n + 1) * ch)
            uc, vc, do = u_ref[rows, :], v_ref[rows, :], do_ref[rows, :]
            u, thu = _gelu(uc)
            vg, thv = _gelu(vc)
            rinv = lax.rsqrt(_group_mean(vg * vg, lo) + EPS)
            xh = vg * rinv
            vnb = (xh * gnv).astype(BF16)
            mix = _sgu_mix(wm, vnb, lo, bcol)
            du_ref[rows, :] = do * mix * _gelu_grad(uc, thu)
            dm = do * u
            dmb = dm.astype(BF16)
            dvn = jnp.where(lo, _dot(wtm[0], dmb), _dot(wtm[1], dmb))
            for h in range(2):
                dmh = jnp.where(hm[h], dm, 0.0)
                dw_ref[h] += jnp.where(c <= r, _dot_nt(dmh.astype(BF16), vnb), 0.0)
                db_ref[0, :, h:h + 1] += jnp.sum(dmh, axis=1, keepdims=True)
            dgn_ref[0] += _colsum(dvn * xh)
            dxh = dvn * gnv
            dvg = rinv * (dxh - xh * _group_mean(dxh * xh, lo))
            dv_ref[rows, :] = dvg * _gelu_grad(vc, thv)

    blk = lambda cb: pl.BlockSpec((tr, LANES), lambda p, i: (i, cb + p))
    w_spec = pl.BlockSpec((2, ch, ch), lambda p, i: (p, 0, 0))
    b_spec = pl.BlockSpec((1, ch, 2), lambda p, i: (p, 0, 0))
    g_spec = pl.BlockSpec((1, 1, LANES), lambda p, i: (p, 0, 0))
    out_blk = pl.BlockSpec((tr, LANES), lambda p, i: (i, p))
    return _pcall(body, name=name, grid=(SGU_W // LANES, s // tr),
                  in_specs=[blk(CB_UC), blk(CB_VC), blk(cb_do), w_spec, w_spec, b_spec, g_spec],
                  out_specs=[out_blk, out_blk, w_spec, b_spec, g_spec],
                  out_shape=[jax.ShapeDtypeStruct((s, SGU_W), F32)] * 2
                  + [jax.ShapeDtypeStruct(w.shape, F32), jax.ShapeDtypeStruct(b_cols.shape, F32),
                     jax.ShapeDtypeStruct(gn.shape, F32)],
                  semantics=("parallel", "arbitrary"))(proj, proj, dmixed, w, w_t, b_cols, gn)


def _pad_lanes(v):
    return jnp.zeros((1, LANES), F32).at[0, :v.shape[0]].set(v)


def _small_views(sm):
    return dict(
        n1=sm["norm1_g"][None, :], n2=sm["norm2_g"][None, :],
        b_pad=_pad_lanes(sm["b_forget"]),
        qg=jnp.tile(sm["q_norm_g"], 2)[None, :], kg=jnp.tile(sm["k_norm_g"], 2)[None, :],
        gn=sm["sgu_norm_g"].reshape(2, 1, LANES),
        w=sm["sgu_w"], w_t=jnp.swapaxes(sm["sgu_w"], 1, 2),
        b_cols=sm["sgu_b"].reshape(2, 2, SGU_CHUNK).transpose(0, 2, 1))


def _cf_rows(cf):
    return cf[:, :FOX_HEADS].T.reshape(FOX_W // LANES, 2, cf.shape[0])


def _layer_fwd(x_in, prev, mod, wts, sm, l, gathers=(), late_weights=None):
    sh1, sc1, g1, sh2, sc2, g2 = mod
    v = _small_views(sm)
    if prev is None:
        x0 = x_in
        h1 = _norm_mod_fwd(x0, v["n1"], sc1, sh1, name=f"l{l}_norm1")
    else:
        x0, h1 = _resid_norm_mod_fwd(x_in, prev[0], prev[1], v["n1"], sc1, sh1, name=f"l{l}_norm1")
    proj = _matmul(h1, wts["w_in"], name=f"l{l}_proj")
    o_sb, sb_ltot, sb_stop = _sb_fwd(proj, name=f"l{l}_sb_fwd")
    qn, kn, kmax = _fox_prep_fwd(proj, v["qg"], v["kg"], name=f"l{l}_fox_prep")
    cf = _forget_cumsum_fwd(proj, v["b_pad"], name=f"l{l}_cumf")
    cfr = _cf_rows(cf)
    o_fox, lse, *gathered = _fox_fwd(proj, qn, kn, cf, cfr, kmax, name=f"l{l}_fox_fwd",
                                     gathers=gathers)
    if gathers:
        late_weights(gathered)
    o_sgu = _sgu_fwd(proj, v["w"], v["b_cols"], v["gn"], name=f"l{l}_sgu_fwd")
    mixed = jnp.concatenate([o_sb, o_fox, o_sgu], axis=1).astype(BF16)
    mo = _matmul(mixed, wts["w_out"], name=f"l{l}_wout")
    x1, h2 = _resid_norm_mod_fwd(x0, mo, g1, v["n2"], sc2, sh2, name=f"l{l}_norm2")
    a, rr = _matmul(h2, wts["w1"], name=f"l{l}_mlp1", out_dtype=BF16, relu2=BF16)
    m2 = _matmul(rr, wts["w2"], name=f"l{l}_mlp2")
    saved = dict(x0=x0, h1=h1, proj=proj, sb_ltot=sb_ltot, sb_stop=sb_stop, qn=qn, kn=kn, kmax=kmax, cf=cf, cfr=cfr, o_fox=o_fox,
                 lse=lse, mixed=mixed, mo=mo, x1=x1, h2=h2, a=a, rr=rr, m2=m2)
    return saved


def _layer_bwd(dx2, dm2, dg2, sv, mod, wts, sm, l, below):
    sh1, sc1, g1, sh2, sc2, g2 = mod
    v = _small_views(sm)
    dw2 = _matmul(sv["rr"], dm2, ta=True, name=f"l{l}_dw2")
    da = _matmul(dm2, wts["w2"], tb=True, name=f"l{l}_da", out_dtype=BF16, pre_act=sv["a"])
    dw1 = _matmul(sv["h2"], da, ta=True, name=f"l{l}_dw1")
    dh2 = _matmul(da, wts["w1"], tb=True, name=f"l{l}_dh2")
    dx1, dn2, dsc2, dsh2, dmo, dg1 = _norm_mod_bwd(sv["x1"], dh2, dx2, v["n2"], sc2,
                                                    (sv["mo"], g1), name=f"l{l}_norm2_bwd")
    dwo = _matmul(sv["mixed"], dmo, ta=True, name=f"l{l}_dwout")
    dmixed = _matmul(dmo, wts["w_out"], tb=True, name=f"l{l}_dmixed")
    proj = sv["proj"]
    dqa, dka, dva = _sb_bwd(proj, dmixed, sv["sb_ltot"], sv["sb_stop"], name=f"l{l}_sb_bwd")
    dqn, dkn, dvb, dcfr, dcfq = _fox_bwd(proj, sv["qn"], sv["kn"], sv["cf"], sv["cfr"], sv["kmax"], dmixed,
                                   sv["o_fox"], sv["lse"], name=f"l{l}_fox_bwd")
    dqb, dkb, dqg, dkg = _fox_prep_bwd(proj, dqn, dkn, v["qg"], v["kg"], name=f"l{l}_fox_prep_bwd")
    s = proj.shape[0]
    dcf_heads = dcfr.reshape(FOX_HEADS, s).T + dcfq.reshape(s, FOX_HEADS, HEAD_DIM)[:, :, 0]
    dcf = jnp.zeros((s, LANES), F32).at[:, :FOX_HEADS].set(dcf_heads)
    dfl, dbf = _forget_cumsum_bwd(proj, v["b_pad"], dcf, name=f"l{l}_cumf_bwd")
    duc, dvc, dsw, dsb_cols, dgn = _sgu_bwd(proj, dmixed, v["w"], v["w_t"], v["b_cols"], v["gn"],
                                            name=f"l{l}_sgu_bwd")
    dproj = jnp.concatenate([dqa, dka, dva, dqb, dkb, dvb, duc, dvc, dfl,
                             jnp.zeros((s, LANES), F32)], axis=1).astype(BF16)
    dwin = _matmul(sv["h1"], dproj, ta=True, name=f"l{l}_dwin")
    dh1 = _matmul(dproj, wts["w_in"], tb=True, name=f"l{l}_dh1")
    dx0, dn1, dsc1, dsh1, dm_below, dg_below = _norm_mod_bwd(sv["x0"], dh1, dx1, v["n1"], sc1, below,
                                                             name=f"l{l}_norm1_bwd")
    big = dict(w_in=dwin, w_out=dwo, w1=dw1, w2=dw2)
    small = dict(norm1_g=dn1[0], norm2_g=dn2[0], b_forget=dbf[0, :FOX_HEADS],
                 q_norm_g=dqg[0, :HEAD_DIM] + dqg[0, HEAD_DIM:],
                 k_norm_g=dkg[0, :HEAD_DIM] + dkg[0, HEAD_DIM:],
                 sgu_norm_g=dgn.reshape(4, HEAD_DIM), sgu_w=dsw,
                 sgu_b=dsb_cols.transpose(0, 2, 1).reshape(4, SGU_CHUNK))
    dmod = jnp.concatenate([dsh1, dsc1, dg1, dsh2, dsc2, dg2], axis=1)
    return dx0, dm_below, dg_below, big, small, dmod


def _w_in_to_internal(w):
    pad = jnp.zeros((w.shape[0], PROJ_W - IN_W), w.dtype)
    return jnp.concatenate([w[:, :ATT_W], w[:, ATT_W + FOX_HEADS:], w[:, ATT_W:ATT_W + FOX_HEADS],
                            pad], axis=1)


def _w_in_from_internal(g):
    n_gate = SGU_W * 2
    return jnp.concatenate([g[:, :ATT_W], g[:, ATT_W + n_gate:ATT_W + n_gate + FOX_HEADS],
                            g[:, ATT_W:ATT_W + n_gate]], axis=1)


def _exchange(x, masks, slot_shift, slot_bits, scatter, *, name):
    n_slots = 2 ** slot_bits
    blk_shape = x.shape[1:] if scatter else x.shape
    n_peers = len(masks)

    def body(x_ref, out_ref, send_sems, recv_sems, local_sem):
        ids = (lax.axis_index("x"), lax.axis_index("y"), lax.axis_index("c"))
        me = 4 * ids[0] + 2 * ids[1] + ids[2]
        my_slot = (me >> slot_shift) & (n_slots - 1)

        def peer(mask):
            return tuple(1 - v if (mask >> b) & 1 else v for v, b in zip(ids, (2, 1, 0)))

        def src_for(slot):
            return x_ref.at[slot] if scatter else x_ref

        copies = [pltpu.make_async_copy(src_for(my_slot), out_ref.at[my_slot], local_sem)]
        for kk, mask in enumerate(masks):
            peer_slot = ((me ^ mask) >> slot_shift) & (n_slots - 1)
            copies.append(pltpu.make_async_remote_copy(
                src_ref=src_for(peer_slot), dst_ref=out_ref.at[my_slot],
                send_sem=send_sems.at[kk], recv_sem=recv_sems.at[kk],
                device_id=peer(mask), device_id_type=MESH))
        for cp in copies:
            cp.start()
        for cp in copies:
            cp.wait()

    any_spec = pl.BlockSpec(memory_space=pl.ANY)
    return _pcall(body, name=name, in_specs=[any_spec], out_specs=any_spec,
                  out_shape=jax.ShapeDtypeStruct((n_slots,) + tuple(blk_shape), x.dtype),
                  scratch_shapes=[pltpu.SemaphoreType.DMA((n_peers,)),
                                  pltpu.SemaphoreType.DMA((n_peers,)),
                                  pltpu.SemaphoreType.DMA(())])(x)


CORE_PIECE_BYTES = 12 * 2 ** 20
CORE_DMA_CHUNKS = 4


def _core_swap_piece(x, *, name):
    rows, cols = x.shape
    n_ch = CORE_DMA_CHUNKS if rows % (16 * CORE_DMA_CHUNKS) == 0 else 1
    rc = rows // n_ch

    def body(x_ref, out_ref, send_sems, recv_sems):
        sibling = (lax.axis_index("x"), lax.axis_index("y"), 1 - lax.axis_index("c"))
        copies = [pltpu.make_async_remote_copy(
            src_ref=x_ref.at[pl.ds(ch * rc, rc)], dst_ref=out_ref.at[pl.ds(ch * rc, rc)],
            send_sem=send_sems.at[ch], recv_sem=recv_sems.at[ch],
            device_id=sibling, device_id_type=MESH) for ch in range(n_ch)]
        for cp in copies:
            cp.start()
        for cp in copies:
            cp.wait()

    vmem = pl.BlockSpec(memory_space=pltpu.VMEM)
    return _pcall(body, name=name, in_specs=[vmem], out_specs=vmem,
                  out_shape=jax.ShapeDtypeStruct(x.shape, x.dtype),
                  scratch_shapes=[pltpu.SemaphoreType.DMA((n_ch,)),
                                  pltpu.SemaphoreType.DMA((n_ch,))])(x)


def _core_swap(x, *, name):
    rows, cols = x.shape
    n = 1
    while (rows % n or (rows // n) % 16 or
           (rows // n) * (-(-cols // LANES) * LANES) * x.dtype.itemsize > CORE_PIECE_BYTES):
        n += 1
    pr = rows // n
    pieces = [_core_swap_piece(x[kk * pr:(kk + 1) * pr], name=f"{name}_{kk}") for kk in range(n)]
    return pieces[0] if n == 1 else jnp.concatenate(pieces, axis=0)


def _by_core(core, mine, theirs, axis):
    return jnp.where(core == 0, jnp.concatenate([mine, theirs], axis=axis),
                     jnp.concatenate([theirs, mine], axis=axis))


def _gather_chips(x, *, name):
    return _exchange(x, (2, 4, 6), 1, 2, False, name=name)


def _gather_all(x, *, name):
    return _exchange(x, (1, 2, 3, 4, 5, 6, 7), 0, 3, False, name=name)


def _scatter_chips(x4, *, name):
    return _exchange(x4, (2, 4, 6), 1, 2, True, name=name)


def _sum_slots(parts, *, name, out_dtype=F32, tr=256):
    n, rows, cols = parts.shape
    tr = min(tr, rows)
    assert rows % tr == 0, (name, rows, tr)

    def body(p_ref, o_ref):
        acc = p_ref[0].astype(F32)
        for kk in range(1, n):
            acc = acc + p_ref[kk].astype(F32)
        o_ref[...] = acc.astype(o_ref.dtype)

    return _pcall(body, name=name, grid=(rows // tr,),
                  in_specs=[pl.BlockSpec((n, tr, cols), lambda i: (0, i, 0))],
                  out_specs=pl.BlockSpec((tr, cols), lambda i: (i, 0)),
                  out_shape=jax.ShapeDtypeStruct((rows, cols), out_dtype),
                  semantics=("parallel",))(parts)


def _add2(a, b, *, name, out_dtype, tr=512):
    def fn(f, v):
        return [f[0] + f[1]], []
    (out,), _ = _rowwise(fn, [a, b], [], [out_dtype], 0, name=name, tr=tr)
    return out


def _adamw(w, m, v, parts, *, name, tr=256):
    n, rows, cols = parts.shape
    tr = min(tr, rows)
    assert rows % tr == 0, (name, rows, tr)
    c1 = 1.0 - ADAM_B1 ** ADAM_STEP
    c2 = 1.0 - ADAM_B2 ** ADAM_STEP

    def body(w_ref, m_ref, v_ref, p_ref, g_ref, d_ref, nm_ref, nv_ref):
        g = p_ref[0]
        for kk in range(1, n):
            g = g + p_ref[kk]
        nm = ADAM_B1 * m_ref[...] + (1.0 - ADAM_B1) * g
        nv = ADAM_B2 * v_ref[...] + (1.0 - ADAM_B2) * (g * g)
        g_ref[...] = g
        nm_ref[...] = nm
        nv_ref[...] = nv
        d_ref[...] = -ADAM_LR * ((nm / c1) / (jnp.sqrt(nv / c2) + ADAM_EPS) + ADAM_WD * w_ref[...])

    spec = pl.BlockSpec((tr, cols), lambda i: (i, 0))
    return _pcall(body, name=name, grid=(rows // tr,),
                  in_specs=[spec, spec, spec, pl.BlockSpec((n, tr, cols), lambda i: (0, i, 0))],
                  out_specs=[spec] * 4,
                  out_shape=[jax.ShapeDtypeStruct((rows, cols), F32)] * 4,
                  semantics=("parallel",))(w, m, v, parts)


def _silu(c):
    return c / (1.0 + jnp.exp(-c))


def _ada_fwd(c_all, ada_w, ada_b_sh, *, name):
    nl, d, wsh = ada_w.shape

    def body(c_ref, w_ref, b_ref, o_ref):
        cond = _silu(c_ref[...]).astype(BF16)
        o_ref[0] = _dot(cond, w_ref[0].astype(BF16)) + b_ref[0]

    return _pcall(body, name=name, grid=(nl,),
                  in_specs=[pl.BlockSpec(c_all.shape, lambda l: (0, 0)),
                            pl.BlockSpec((1, d, wsh), lambda l: (l, 0, 0)),
                            pl.BlockSpec((1, 1, wsh), lambda l: (l, 0, 0))],
                  out_specs=pl.BlockSpec((1, c_all.shape[0], wsh), lambda l: (l, 0, 0)),
                  out_shape=jax.ShapeDtypeStruct((nl, c_all.shape[0], wsh), F32),
                  semantics=("parallel",))(c_all, ada_w, ada_b_sh)


def _ada_bwd(c_all, dmod_sh, *, name):
    nl, nb, wsh = dmod_sh.shape
    d = c_all.shape[1]

    def body(c_ref, dm_ref, o_ref):
        cond = _silu(c_ref[...]).astype(BF16)
        o_ref[0] = _dot_tn(cond, dm_ref[0].astype(BF16))

    return _pcall(body, name=name, grid=(nl,),
                  in_specs=[pl.BlockSpec(c_all.shape, lambda l: (0, 0)),
                            pl.BlockSpec((1, nb, wsh), lambda l: (l, 0, 0))],
                  out_specs=pl.BlockSpec((1, d, wsh), lambda l: (l, 0, 0)),
                  out_shape=jax.ShapeDtypeStruct((nl, d, wsh), F32),
                  semantics=("parallel",))(c_all, dmod_sh)


SMALL_NAMES = ("norm1_g", "norm2_g", "b_forget", "q_norm_g", "k_norm_g", "sgu_norm_g", "sgu_w",
               "sgu_b")
WEIGHT_NAMES = ("ada_w", "ada_b", "norm1_g", "norm2_g", "w_in", "b_forget", "q_norm_g", "k_norm_g",
                "sgu_norm_g", "sgu_w", "sgu_b", "w_out", "mlp_w1", "mlp_w2")


SMALL_TILE_ROWS = 256


def _pack_small(tree):
    flat = jnp.concatenate([tree[n].reshape(-1) for n in SMALL_NAMES])
    n = flat.shape[0]
    rows = -(-n // (SMALL_TILE_ROWS * LANES)) * SMALL_TILE_ROWS
    return jnp.zeros((rows * LANES,), F32).at[:n].set(flat).reshape(rows, LANES)


def _unpack_small(packed, like):
    flat = packed.reshape(-1)
    out, off = {}, 0
    for n in SMALL_NAMES:
        size = like[n].size
        out[n] = flat[off:off + size].reshape(like[n].shape)
        off += size
    return out


def kernel(x, c, ada_w, ada_b, norm1_g, norm2_g, w_in, b_forget, q_norm_g, k_norm_g, sgu_norm_g, sgu_w, sgu_b, w_out, mlp_w1, mlp_w2, loss_target, m_ada_w, m_ada_b, m_norm1_g, m_norm2_g, m_w_in, m_b_forget, m_q_norm_g, m_k_norm_g, m_sgu_norm_g, m_sgu_w, m_sgu_b, m_w_out, m_mlp_w1, m_mlp_w2, v_ada_w, v_ada_b, v_norm1_g, v_norm2_g, v_w_in, v_b_forget, v_q_norm_g, v_k_norm_g, v_sgu_norm_g, v_sgu_w, v_sgu_b, v_w_out, v_mlp_w1, v_mlp_w2):
    w = dict(ada_w=ada_w, ada_b=ada_b, norm1_g=norm1_g, norm2_g=norm2_g, w_in=w_in,
             b_forget=b_forget, q_norm_g=q_norm_g, k_norm_g=k_norm_g, sgu_norm_g=sgu_norm_g,
             sgu_w=sgu_w, sgu_b=sgu_b, w_out=w_out, mlp_w1=mlp_w1, mlp_w2=mlp_w2)
    mom = dict(ada_w=m_ada_w, ada_b=m_ada_b, norm1_g=m_norm1_g, norm2_g=m_norm2_g, w_in=m_w_in,
               b_forget=m_b_forget, q_norm_g=m_q_norm_g, k_norm_g=m_k_norm_g,
               sgu_norm_g=m_sgu_norm_g, sgu_w=m_sgu_w, sgu_b=m_sgu_b, w_out=m_w_out,
               mlp_w1=m_mlp_w1, mlp_w2=m_mlp_w2)
    var = dict(ada_w=v_ada_w, ada_b=v_ada_b, norm1_g=v_norm1_g, norm2_g=v_norm2_g, w_in=v_w_in,
               b_forget=v_b_forget, q_norm_g=v_q_norm_g, k_norm_g=v_k_norm_g,
               sgu_norm_g=v_sgu_norm_g, sgu_w=v_sgu_w, sgu_b=v_sgu_b, w_out=v_w_out,
               mlp_w1=v_mlp_w1, mlp_w2=v_mlp_w2)
    depth, d = norm1_g.shape
    chip = 2 * lax.axis_index("x") + lax.axis_index("y")
    me = 2 * chip + lax.axis_index("c")
    n_chips = 4
    ada_sh = ada_w.shape[2]

    core = lax.axis_index("c")
    half_l = depth // 2

    def my_part(w_sh):
        _, r, cols = w_sh.shape
        mine = lax.dynamic_slice_in_dim(w_sh, core * half_l, half_l, axis=0).astype(BF16)
        return mine.reshape(half_l * r, cols)

    def share(got, w_sh, name):
        _, r, cols = w_sh.shape
        theirs = _core_swap(got.reshape(n_chips * half_l * r, cols), name=f"share_{name}")
        return _by_core(core, got.reshape(n_chips, half_l, r, cols),
                        theirs.reshape(n_chips, half_l, r, cols), 1)

    g_in = share(_gather_chips(my_part(w_in), name="gather_w_in"), w_in, "w_in")
    layer_w = [dict(w_in=_w_in_to_internal(
        jnp.concatenate([g_in[k, l] for k in range(n_chips)], axis=1))) for l in range(depth)]
    later = (("w_out", w_out), ("w1", mlp_w1), ("w2", mlp_w2))

    def late_weights(gathered):
        g_out, g_w1, g_w2 = [share(got, w_sh, name) for got, (name, w_sh) in zip(gathered, later)]
        for l in range(depth):
            layer_w[l].update(
                w_out=g_out[:, l].reshape(d, d),
                w1=jnp.concatenate([g_w1[k, l] for k in range(n_chips)], axis=1),
                w2=g_w2[:, l].reshape(D_FF, d))

    c_all = _gather_all(jnp.zeros((8, d), F32).at[0].set(c[0]), name="gather_c")[:, 0]
    c_pad = jnp.concatenate([c_all, jnp.zeros_like(c_all)], axis=0)
    ada_b_sh = lax.dynamic_slice_in_dim(ada_b, chip * ada_sh, ada_sh, axis=1)[:, None, :]
    mod_sh = _ada_fwd(c_pad, ada_w, ada_b_sh, name="ada_fwd")
    mod_all = _gather_chips(mod_sh, name="gather_mod")
    mod_me = lax.dynamic_index_in_dim(mod_all, me, axis=2, keepdims=False)
    mod_me = mod_me.transpose(1, 0, 2).reshape(depth, 6, 1, d)

    saved = []
    xs, prev = x[0], None
    for l in range(depth):
        mod = [mod_me[l, kk] for kk in range(6)]
        sm = {n: w[n][l] for n in SMALL_NAMES}
        first = dict(gathers=[my_part(w_sh) for _, w_sh in later], late_weights=late_weights)
        sv = _layer_fwd(xs, prev, mod, layer_w[l], sm, l, **(first if l == 0 else {}))
        saved.append(sv)
        xs, prev = sv["x1"], (sv["m2"], mod[5])

    sq, dxs, dm2, dg2 = _loss_fwd_bwd(xs, prev[0], prev[1], loss_target[0], name="loss")
    loss = lax.psum(0.5 * jnp.sum(sq) / d, ("x", "y", "c"))

    big = {n: [] for n in ("w_in", "w_out", "w1", "w2")}
    small = {n: [] for n in SMALL_NAMES}
    dmods = []
    for l in reversed(range(depth)):
        mod = [mod_me[l, kk] for kk in range(6)]
        sm = {n: w[n][l] for n in SMALL_NAMES}
        below = (saved[l - 1]["m2"], mod_me[l - 1, 5]) if l else None
        dxs, dm2, dg2, bg, smg, dmod = _layer_bwd(dxs, dm2, dg2, saved[l], mod, layer_w[l], sm, l,
                                                  below)
        for n in big:
            big[n].insert(0, bg[n])
        for n in SMALL_NAMES:
            small[n].insert(0, smg[n])
        dmods.insert(0, dmod)
    grad_x = dxs[None]

    out_g, out_d, out_m, out_v = {}, {}, {}, {}

    def run_adamw(name, parts2d, shape):
        rows, cols = parts2d.shape[1:]
        g, dl, nm, nv = _adamw(w[name].reshape(rows, cols), mom[name].reshape(rows, cols),
                               var[name].reshape(rows, cols), parts2d, name=f"adamw_{name}")
        out_g[name], out_d[name] = g.reshape(shape), dl.reshape(shape)
        out_m[name], out_v[name] = nm.reshape(shape), nv.reshape(shape)

    def shards_of(name, l):
        if name == "w_in":
            g = _w_in_from_internal(big["w_in"][l])
            return jnp.stack(jnp.split(g, n_chips, axis=1))
        if name == "mlp_w1":
            return jnp.stack(jnp.split(big["w1"][l], n_chips, axis=1))
        if name == "w_out":
            return big["w_out"][l].reshape(n_chips, d // n_chips, d)
        return big["w2"][l].reshape(n_chips, D_FF // n_chips, d)

    for name in ("w_in", "w_out", "mlp_w1", "mlp_w2"):
        per_chip = jnp.stack([shards_of(name, l) for l in range(depth)], axis=1)
        r, cols = per_chip.shape[2:]
        half_rows = half_l * r
        keep = lax.dynamic_slice_in_dim(per_chip, core * half_l, half_l, axis=1)
        send = lax.dynamic_slice_in_dim(per_chip, (1 - core) * half_l, half_l, axis=1)
        theirs = _core_swap(send.reshape(n_chips * half_rows, cols), name=f"pair_{name}")
        chip_sum = _add2(keep.reshape(n_chips * half_rows, cols), theirs, out_dtype=BF16,
                         name=f"pairsum_{name}")
        got = _scatter_chips(chip_sum.reshape(n_chips, half_rows, cols), name=f"scatter_{name}")
        half = _sum_slots(got, name=f"sum_{name}")
        both = _by_core(core, half, _core_swap(half, name=f"swap_{name}"), 0)
        run_adamw(name, both[None], w[name].shape)

    small_tree = {n: jnp.stack(small[n]) for n in SMALL_NAMES}
    gathered = _gather_all(_pack_small(small_tree), name="gather_small")
    gs, ds_, ms, vs = _adamw(_pack_small({n: w[n] for n in SMALL_NAMES}),
                             _pack_small({n: mom[n] for n in SMALL_NAMES}),
                             _pack_small({n: var[n] for n in SMALL_NAMES}), gathered,
                             name="adamw_small")
    like = {n: w[n] for n in SMALL_NAMES}
    for tree, packed in ((out_g, gs), (out_d, ds_), (out_m, ms), (out_v, vs)):
        tree.update(_unpack_small(packed, like))

    dmod_mine = jnp.concatenate(dmods, axis=0)
    dmod_all = _gather_all(jnp.zeros((depth, 8, 6 * d), F32).at[:, 0].set(dmod_mine),
                           name="gather_dmod")[:, :, 0]
    dmod_lb = dmod_all.transpose(1, 0, 2)
    dmod_sh = lax.dynamic_slice_in_dim(dmod_lb, chip * ada_sh, ada_sh, axis=2)
    dmod_sh = jnp.concatenate([dmod_sh, jnp.zeros_like(dmod_sh)], axis=1)
    g_ada_w = _ada_bwd(c_pad, dmod_sh, name="ada_bwd")
    run_adamw("ada_w", g_ada_w.reshape(1, depth * d, ada_sh), ada_w.shape)
    parts_b = dmod_all.reshape(8, depth * 6 * d // LANES, LANES)
    run_adamw("ada_b", parts_b, ada_b.shape)

    outs = [loss, grad_x]
    for tree in (out_g, out_d, out_m, out_v):
        outs += [tree[n] for n in WEIGHT_NAMES]
    return tuple(outs)
```

```python
import functools
import math

import jax
import jax.numpy as jnp
from jax import lax
from jax.experimental import pallas as pl
from jax.experimental.pallas import tpu as pltpu

F32 = jnp.float32
BF16 = jnp.bfloat16

D_MODEL = 1024
DEPTH = 4
HEAD_DIM = 64
LANES = 128
D_FF = 4 * D_MODEL
EPS = 1e-6
SB_W, FOX_W, SGU_W = 256, 512, 256
FOX_HEADS = 8
SGU_CHUNK = 128
IN_W = 2824
ATT_W = 3 * SB_W + 3 * FOX_W
PROJ_W = 3072
CB_QA, CB_KA, CB_VA = 0, 2, 4
CB_QB, CB_KB, CB_VB = 6, 10, 14
CB_UC, CB_VC, CB_FL = 18, 20, 22
ATT_T = 256
VMEM_LIMIT = 56 * 2 ** 20
SKIP_LOG = 110.0

ADAM_LR, ADAM_B1, ADAM_B2, ADAM_EPS, ADAM_WD, ADAM_STEP = 0.001, 0.9, 0.999, 1e-08, 0.01, 10

MESH = pl.DeviceIdType.MESH


def _pcall(body, *, name, out_shape, grid=(), in_specs=None, out_specs=None, scratch_shapes=(),
           semantics=None):
    params = dict(vmem_limit_bytes=VMEM_LIMIT)
    if semantics is not None:
        params["dimension_semantics"] = semantics
    kwargs = {}
    if in_specs is not None:
        kwargs["in_specs"] = in_specs
    if out_specs is not None:
        kwargs["out_specs"] = out_specs
    return pl.pallas_call(body, name=name, out_shape=out_shape, grid=grid,
                          scratch_shapes=list(scratch_shapes),
                          compiler_params=pltpu.CompilerParams(**params), **kwargs)


def _dot(a, b):
    return jnp.dot(a, b, preferred_element_type=F32)


def _dot_nt(a, b):
    return lax.dot_general(a, b, (((1,), (1,)), ((), ())), preferred_element_type=F32)


def _dot_tn(a, b):
    return lax.dot_general(a, b, (((0,), (0,)), ((), ())), preferred_element_type=F32)


def _split2(x):
    hi = x.astype(BF16)
    lo = (x - hi.astype(F32)).astype(BF16)
    return hi, lo


def _ones_dot(x, ones_bf16):
    hi, lo = _split2(x)
    return _dot(hi, ones_bf16) + _dot(lo, ones_bf16)


def _rowwise(fn, fulls, vecs, out_dtypes, n_vec_out, *, name, tr):
    s, n = fulls[0].shape
    tr = min(tr, s)
    assert s % tr == 0, (name, s, tr)
    nf, nv, nfo = len(fulls), len(vecs), len(out_dtypes)

    def body(*refs):
        fi, vi = refs[:nf], refs[nf:nf + nv]
        fo, vo = refs[nf + nv:nf + nv + nfo], refs[nf + nv + nfo:]
        outs_f, outs_v = fn([r[...] for r in fi], [r[...] for r in vi])
        for r, o in zip(fo, outs_f):
            r[...] = o.astype(r.dtype)
        if n_vec_out:
            @pl.when(pl.program_id(0) == 0)
            def _():
                for r in vo:
                    r[...] = jnp.zeros_like(r)
            for r, o in zip(vo, outs_v):
                r[...] += o

    full_spec = pl.BlockSpec((tr, n), lambda i: (i, 0))
    vec_specs = [pl.BlockSpec(v.shape, lambda i: (0, 0)) for v in vecs]
    out_vec_spec = pl.BlockSpec((1, n), lambda i: (0, 0))
    out_shape = [jax.ShapeDtypeStruct((s, n), dt) for dt in out_dtypes]
    out_shape += [jax.ShapeDtypeStruct((1, n), F32)] * n_vec_out
    outs = _pcall(body, name=name, grid=(s // tr,),
                  in_specs=[full_spec] * nf + vec_specs,
                  out_specs=[full_spec] * nfo + [out_vec_spec] * n_vec_out,
                  out_shape=out_shape,
                  semantics=("arbitrary",) if n_vec_out else ("parallel",))(*fulls, *vecs)
    return outs[:nfo], outs[nfo:]


def _colsum(t):
    return jnp.sum(t, axis=0, keepdims=True)


def _rms_mod(x, g, sc, sh):
    r = lax.rsqrt(jnp.mean(x * x, axis=-1, keepdims=True) + EPS)
    return (x * r * g) * (1.0 + sc) + sh


def _norm_mod_fwd(x, g, sc, sh, *, name):
    def fn(f, v):
        return [_rms_mod(f[0], v[0], v[1], v[2])], []
    (h,), _ = _rowwise(fn, [x], [g, sc, sh], [BF16], 0, name=name, tr=512)
    return h


def _resid_norm_mod_fwd(x, m, gate, g, sc, sh, *, name):
    def fn(f, v):
        xn = f[0] + v[0] * f[1]
        return [xn, _rms_mod(xn, v[1], v[2], v[3])], []
    (xn, h), _ = _rowwise(fn, [x, m], [gate, g, sc, sh], [F32, BF16], 0, name=name, tr=512)
    return xn, h


def _norm_mod_bwd(x, dh, dres, g, sc, gated, *, name):
    def fn(f, v):
        xv, dhv, dr = f[:3]
        gv, scv = v[:2]
        r = lax.rsqrt(jnp.mean(xv * xv, axis=-1, keepdims=True) + EPS)
        xh = xv * r
        dn = dhv * (1.0 + scv)
        dxh = dn * gv
        dx = dr + r * (dxh - xh * jnp.mean(dxh * xh, axis=-1, keepdims=True))
        sums = [_colsum(dn * xh), _colsum(dhv * (xh * gv)), _colsum(dhv)]
        if gated is None:
            return [dx], sums
        return [dx, dx * v[2]], sums + [_colsum(dx * f[3])]
    if gated is None:
        (dx,), (dg, dsc, dsh) = _rowwise(fn, [x, dh, dres], [g, sc], [F32], 3, name=name, tr=256)
        return dx, dg, dsc, dsh, None, None
    (dx, dm), (dg, dsc, dsh, dgate) = _rowwise(fn, [x, dh, dres, gated[0]], [g, sc, gated[1]],
                                               [F32, BF16], 4, name=name, tr=256)
    return dx, dg, dsc, dsh, dm, dgate


def _loss_fwd_bwd(x, m, gate, target, *, name):
    n = x.shape[1]

    def fn(f, v):
        err = f[0] + v[0] * f[1] - f[2]
        dy = err * (1.0 / n)
        return [dy, dy * v[0]], [_colsum(err * err), _colsum(dy * f[1])]
    (dy, dm), (sq, dgate) = _rowwise(fn, [x, m, target], [gate], [F32, BF16], 2, name=name, tr=512)
    return sq, dy, dm, dgate


def _matmul(a, b, *, name, ta=False, tb=False, out_dtype=F32, relu2=None, pre_act=None,
            tm=1024, tn=1024, tk_max=2048):
    m = a.shape[1] if ta else a.shape[0]
    k = a.shape[0] if ta else a.shape[1]
    n = b.shape[0] if tb else b.shape[1]
    assert k == (b.shape[1] if tb else b.shape[0])
    tk = max(dd for dd in range(LANES, min(tk_max, k) + 1, LANES) if k % dd == 0) if k > LANES else k
    tm, tn = min(tm, m), min(tn, n)
    assert m % tm == 0 and n % tn == 0 and k % tk == 0, (name, m, n, k)
    nk = k // tk
    dims = (((0 if ta else 1,), (1 if tb else 0,)), ((), ()))

    plain = relu2 is None and pre_act is None
    in_place = plain and out_dtype == F32
    n_in = 2 + (pre_act is not None)

    def body(*refs):
        a_ref, b_ref = refs[:2]
        o_ref = refs[n_in]
        prod = lax.dot_general(a_ref[...].astype(BF16), b_ref[...].astype(BF16), dims,
                               preferred_element_type=F32)

        def finish(acc):
            if pre_act is not None:
                acc = acc * (2.0 * jnp.maximum(refs[2][...].astype(F32), 0.0))
            o_ref[...] = acc.astype(o_ref.dtype)
            if relu2 is not None:
                r = jnp.maximum(acc, 0.0)
                refs[n_in + 1][...] = (r * r).astype(relu2)

        if nk == 1:
            finish(prod)
            return
        kk = pl.program_id(2)
        acc_ref = o_ref if in_place else refs[-1]

        @pl.when(kk == 0)
        def _():
            acc_ref[...] = prod

        @pl.when(kk > 0)
        def _():
            acc_ref[...] += prod

        if not in_place:
            @pl.when(kk == nk - 1)
            def _():
                finish(acc_ref[...])

    a_spec = (pl.BlockSpec((tk, tm), lambda i, j, kk: (kk, i)) if ta
              else pl.BlockSpec((tm, tk), lambda i, j, kk: (i, kk)))
    b_spec = (pl.BlockSpec((tn, tk), lambda i, j, kk: (j, kk)) if tb
              else pl.BlockSpec((tk, tn), lambda i, j, kk: (kk, j)))
    out_spec = pl.BlockSpec((tm, tn), lambda i, j, kk: (i, j))
    in_specs, args = [a_spec, b_spec], [a, b]
    if pre_act is not None:
        in_specs.append(out_spec)
        args.append(pre_act)
    out_specs, out_shape = out_spec, jax.ShapeDtypeStruct((m, n), out_dtype)
    if relu2 is not None:
        out_specs, out_shape = [out_spec] * 2, [out_shape, jax.ShapeDtypeStruct((m, n), relu2)]
    return _pcall(body, name=name, grid=(m // tm, n // tn, nk),
                  in_specs=in_specs, out_specs=out_specs, out_shape=out_shape,
                  scratch_shapes=[] if nk == 1 or in_place else [pltpu.VMEM((tm, tn), F32)],
                  semantics=("parallel", "parallel", "arbitrary"))(*args)


def _lane_masks():
    lane = lax.broadcasted_iota(jnp.int32, (1, LANES), 1)
    return [lane < HEAD_DIM, lane >= HEAD_DIM]


def _tri_iotas(t):
    r = lax.broadcasted_iota(jnp.int32, (t, t), 0)
    c = lax.broadcasted_iota(jnp.int32, (t, t), 1)
    return r, c


def _rows(j, t):
    return pl.ds(pl.multiple_of(j * t, t), t)


def _neg_softplus(z):
    e = jnp.exp(-jnp.abs(z))
    return -(jnp.maximum(z, 0.0) + jnp.log(1.0 + e)), e


def _sb_fwd(proj, *, name):
    s = proj.shape[0]
    t = min(ATT_T, s)
    scale = HEAD_DIM ** -0.5

    def body(q_ref, k_ref, v_ref, o_ref, ltot_ref, stop_ref):
        i = pl.program_id(1)
        hm = _lane_masks()
        q = q_ref[...] * scale
        qh = [jnp.where(mk, q, 0.0).astype(BF16) for mk in hm]
        r, c = _tri_iotas(t)
        later = (r > c).astype(BF16)
        q2 = jnp.concatenate(qh, axis=0)
        causal2 = jnp.concatenate([c < r, c < r], axis=0)

        def scores(j):
            return _dot_nt(q2, k_ref[_rows(j, t), :].astype(BF16))

        def chunk(j, carry, z, masked):
            e_run, acc = carry
            vb = v_ref[_rows(j, t), :].astype(BF16)
            l, _ = _neg_softplus(z)
            if masked:
                l = jnp.where(causal2, l, 0.0)
            between = _ones_dot(l, later) + e_run
            a = jnp.exp(z + l + between)
            if masked:
                a = jnp.where(causal2, a, 0.0)
            return e_run + jnp.sum(l, axis=1, keepdims=True), acc + _dot(a.astype(BF16), vb)

        init = (jnp.zeros((2 * t, 1), F32), jnp.zeros((2 * t, LANES), F32))
        carry = chunk(i, init, scores(i), True)

        def step(st):
            j, cr, z = st
            z_next = scores(jnp.maximum(j - 1, 0))
            return j - 1, chunk(j, cr, z, False), z_next

        j_stop, (e_tot, acc), _ = lax.while_loop(
            lambda st: (st[0] >= 0) & (jnp.max(st[1][0]) > -SKIP_LOG), step,
            (i - 1, carry, scores(jnp.maximum(i - 1, 0))))
        o_ref[...] = jnp.where(hm[0], acc[:t], acc[t:])
        ltot_ref[...] = jnp.where(hm[0], e_tot[:t], e_tot[t:])
        stop_ref[...] = jnp.full(stop_ref.shape, j_stop.astype(F32), F32)

    blk = lambda cb: pl.BlockSpec((t, LANES), lambda p, i: (i, cb + p))
    full = lambda cb: pl.BlockSpec((s, LANES), lambda p, i: (0, cb + p))
    out_blk = pl.BlockSpec((t, LANES), lambda p, i: (i, p))
    n_pairs = SB_W // LANES
    return _pcall(body, name=name, grid=(n_pairs, s // t),
                  in_specs=[blk(CB_QA), full(CB_KA), full(CB_VA)],
                  out_specs=[out_blk, out_blk,
                             pl.BlockSpec((1, 1, 8, LANES), lambda p, i: (p, i, 0, 0))],
                  out_shape=[jax.ShapeDtypeStruct((s, SB_W), F32)] * 2
                  + [jax.ShapeDtypeStruct((n_pairs, s // t, 8, LANES), F32)],
                  semantics=("parallel", "arbitrary"))(proj, proj, proj)


def _sb_bwd(proj, dmixed, ltot, stop, *, name):
    s = proj.shape[0]
    t = min(ATT_T, s)
    scale = HEAD_DIM ** -0.5

    def body(q_ref, k_ref, v_ref, do_ref, ltot_ref, stop_ref, dq_ref, dk_ref, dv_ref):
        i = pl.program_id(1)

        @pl.when(i == 0)
        def _():
            dk_ref[...] = jnp.zeros_like(dk_ref)
            dv_ref[...] = jnp.zeros_like(dv_ref)

        hm = _lane_masks()
        q = q_ref[...] * scale
        do = do_ref[...]
        qh = [jnp.where(mk, q, 0.0).astype(BF16) for mk in hm]
        doh = [jnp.where(mk, do, 0.0).astype(BF16) for mk in hm]
        r, c = _tri_iotas(t)
        upto = (r <= c).astype(BF16)
        before = (r < c).astype(BF16)
        q2 = jnp.concatenate(qh, axis=0)
        do2 = jnp.concatenate(doh, axis=0)
        causal2 = jnp.concatenate([c < r, c < r], axis=0)

        j_stop = jnp.clip(jnp.max(stop_ref[...]).astype(jnp.int32), -1, i - 1)
        ltv = ltot_ref[...]
        lt = jnp.concatenate([ltv[:, 0:1], ltv[:, HEAD_DIM:HEAD_DIM + 1]], axis=0)

        def products(j):
            return (_dot_nt(q2, k_ref[_rows(j, t), :].astype(BF16)),
                    _dot_nt(do2, v_ref[_rows(j, t), :].astype(BF16)))

        def chunk(j, carry, z, da, masked):
            l_run, g_run, dq = carry
            l, e = _neg_softplus(z)
            beta = jnp.where(z >= 0.0, 1.0, e) / (1.0 + e)
            if masked:
                l = jnp.where(causal2, l, 0.0)
            prefix = _ones_dot(l, upto) + l_run
            a = jnp.exp(z + l + (lt - prefix))
            if masked:
                a = jnp.where(causal2, a, 0.0)
            g = a * da
            g_before = _ones_dot(g, before) + g_run
            dz = g * (1.0 - beta) - beta * g_before
            if masked:
                dz = jnp.where(causal2, dz, 0.0)
            dzb = dz.astype(BF16)
            dk_ref[_rows(j, t), :] += _dot_tn(dzb, q2)
            dv_ref[_rows(j, t), :] += _dot_tn(a.astype(BF16), do2)
            return (l_run + jnp.sum(l, axis=1, keepdims=True),
                    g_run + jnp.sum(g, axis=1, keepdims=True),
                    dq + _dot(dzb, k_ref[_rows(j, t), :].astype(BF16)))

        init = (jnp.zeros((2 * t, 1), F32), jnp.zeros((2 * t, 1), F32),
                jnp.zeros((2 * t, LANES), F32))
        carry = lax.fori_loop(j_stop + 1, i,
                              lambda j, cr: chunk(j, cr, *products(j), False), init)
        dq2 = chunk(i, carry, *products(i), True)[2]
        dq_ref[...] = jnp.where(hm[0], dq2[:t], dq2[t:]) * scale

    blk = lambda cb: pl.BlockSpec((t, LANES), lambda p, i: (i, cb + p))
    full = lambda cb: pl.BlockSpec((s, LANES), lambda p, i: (0, cb + p))
    out_blk = pl.BlockSpec((t, LANES), lambda p, i: (i, p))
    out_full = pl.BlockSpec((s, LANES), lambda p, i: (0, p))
    return _pcall(body, name=name, grid=(SB_W // LANES, s // t),
                  in_specs=[blk(CB_QA), full(CB_KA), full(CB_VA), blk(0), out_blk,
                            pl.BlockSpec((1, 1, 8, LANES), lambda p, i: (p, i, 0, 0))],
                  out_specs=[out_blk, out_full, out_full],
                  out_shape=[jax.ShapeDtypeStruct((s, SB_W), F32)] * 3,
                  semantics=("parallel", "arbitrary"))(proj, proj, proj, dmixed, ltot, stop)


def _group_mean(v, lo):
    s0 = jnp.sum(jnp.where(lo, v, 0.0), axis=1, keepdims=True)
    s1 = jnp.sum(jnp.where(lo, 0.0, v), axis=1, keepdims=True)
    return jnp.where(lo, s0, s1) * (1.0 / HEAD_DIM)


def _fox_prep_fwd(proj, qg, kg, *, name):
    s = proj.shape[0]
    tr = min(512, s)

    def body(q_ref, k_ref, qg_ref, kg_ref, qn_ref, kn_ref, kmax_ref):
        lo = _lane_masks()[0]
        for x_ref, g_ref, o_ref in ((q_ref, qg_ref, qn_ref), (k_ref, kg_ref, kn_ref)):
            x = x_ref[...]
            o_ref[...] = x * lax.rsqrt(_group_mean(x * x, lo) + EPS) * g_ref[...]

        @pl.when(pl.program_id(1) == 0)
        def _():
            kmax_ref[...] = jnp.zeros_like(kmax_ref)
        kn = kn_ref[...]
        norms = jnp.sqrt(_group_mean(kn * kn, lo) * HEAD_DIM)
        kmax_ref[...] = jnp.maximum(kmax_ref[...], jnp.max(norms, axis=0, keepdims=True))

    blk = lambda cb: pl.BlockSpec((tr, LANES), lambda p, i: (i, cb + p))
    vec = pl.BlockSpec((1, LANES), lambda p, i: (0, 0))
    out_blk = pl.BlockSpec((tr, LANES), lambda p, i: (i, p))
    return _pcall(body, name=name, grid=(FOX_W // LANES, s // tr),
                  in_specs=[blk(CB_QB), blk(CB_KB), vec, vec],
                  out_specs=[out_blk, out_blk, pl.BlockSpec((1, LANES), lambda p, i: (0, p))],
                  out_shape=[jax.ShapeDtypeStruct((s, FOX_W), F32)] * 2
                  + [jax.ShapeDtypeStruct((1, FOX_W), F32)],
                  semantics=("parallel", "arbitrary"))(proj, proj, qg, kg)


def _fox_prep_bwd(proj, dqn, dkn, qg, kg, *, name):
    s = proj.shape[0]
    tr = min(512, s)

    def body(q_ref, k_ref, dqn_ref, dkn_ref, qg_ref, kg_ref, dq_ref, dk_ref, dqg_ref, dkg_ref):
        @pl.when((pl.program_id(0) == 0) & (pl.program_id(1) == 0))
        def _():
            dqg_ref[...] = jnp.zeros_like(dqg_ref)
            dkg_ref[...] = jnp.zeros_like(dkg_ref)

        lo = _lane_masks()[0]
        for x_ref, dy_ref, g_ref, dx_ref, dg_ref in ((q_ref, dqn_ref, qg_ref, dq_ref, dqg_ref),
                                                     (k_ref, dkn_ref, kg_ref, dk_ref, dkg_ref)):
            x, dy = x_ref[...], dy_ref[...]
            r = lax.rsqrt(_group_mean(x * x, lo) + EPS)
            xh = x * r
            dxh = dy * g_ref[...]
            dx_ref[...] = r * (dxh - xh * _group_mean(dxh * xh, lo))
            dg_ref[...] += _colsum(dy * xh)

    blk = lambda cb: pl.BlockSpec((tr, LANES), lambda p, i: (i, cb + p))
    vec = pl.BlockSpec((1, LANES), lambda p, i: (0, 0))
    out_blk = pl.BlockSpec((tr, LANES), lambda p, i: (i, p))
    return _pcall(body, name=name, grid=(FOX_W // LANES, s // tr),
                  in_specs=[blk(CB_QB), blk(CB_KB), out_blk, out_blk, vec, vec],
                  out_specs=[out_blk, out_blk, vec, vec],
                  out_shape=[jax.ShapeDtypeStruct((s, FOX_W), F32)] * 2
                  + [jax.ShapeDtypeStruct((1, LANES), F32)] * 2,
                  semantics=("arbitrary", "arbitrary"))(proj, proj, dqn, dkn, qg, kg)


def _split3_dot(tri_bf16, x):
    hi = x.astype(BF16)
    r1 = x - hi.astype(F32)
    mid = r1.astype(BF16)
    lo = (r1 - mid.astype(F32)).astype(BF16)
    return _dot(tri_bf16, hi) + _dot(tri_bf16, mid) + _dot(tri_bf16, lo)


def _forget_cumsum_fwd(proj, b_pad, *, name):
    s = proj.shape[0]
    tb = min(256, s)

    def body(fl_ref, b_ref, cf_ref, run_ref):
        @pl.when(pl.program_id(0) == 0)
        def _():
            run_ref[...] = jnp.zeros_like(run_ref)
        lf, _ = _neg_softplus(-(fl_ref[...] + b_ref[...]))
        r, c = _tri_iotas(tb)
        incl = _split3_dot((c <= r).astype(BF16), lf) + run_ref[...]
        cf_ref[...] = incl
        run_ref[...] = incl[tb - 1:tb, :]

    return _pcall(body, name=name, grid=(s // tb,),
                  in_specs=[pl.BlockSpec((tb, LANES), lambda i: (i, CB_FL)),
                            pl.BlockSpec((1, LANES), lambda i: (0, 0))],
                  out_specs=pl.BlockSpec((tb, LANES), lambda i: (i, 0)),
                  out_shape=jax.ShapeDtypeStruct((s, LANES), F32),
                  scratch_shapes=[pltpu.VMEM((1, LANES), F32)],
                  semantics=("arbitrary",))(proj, b_pad)


def _forget_cumsum_bwd(proj, b_pad, dcf, *, name):
    s = proj.shape[0]
    tb = min(256, s)
    nb = s // tb

    def body(fl_ref, b_ref, dcf_ref, dfl_ref, db_ref, run_ref):
        @pl.when(pl.program_id(0) == 0)
        def _():
            run_ref[...] = jnp.zeros_like(run_ref)
            db_ref[...] = jnp.zeros_like(db_ref)
        r, c = _tri_iotas(tb)
        dlf = _split3_dot((c >= r).astype(BF16), dcf_ref[...]) + run_ref[...]
        run_ref[...] = dlf[0:1, :]
        xv = fl_ref[...] + b_ref[...]
        e = jnp.exp(-jnp.abs(xv))
        sig_neg = jnp.where(xv >= 0.0, e, 1.0) / (1.0 + e)
        dfl = dlf * sig_neg
        dfl_ref[...] = dfl
        db_ref[...] += _colsum(dfl)

    return _pcall(body, name=name, grid=(nb,),
                  in_specs=[pl.BlockSpec((tb, LANES), lambda i: (nb - 1 - i, CB_FL)),
                            pl.BlockSpec((1, LANES), lambda i: (0, 0)),
                            pl.BlockSpec((tb, LANES), lambda i: (nb - 1 - i, 0))],
                  out_specs=[pl.BlockSpec((tb, LANES), lambda i: (nb - 1 - i, 0)),
                             pl.BlockSpec((1, LANES), lambda i: (0, 0))],
                  out_shape=[jax.ShapeDtypeStruct((s, LANES), F32),
                             jax.ShapeDtypeStruct((1, LANES), F32)],
                  scratch_shapes=[pltpu.VMEM((1, LANES), F32)],
                  semantics=("arbitrary",))(proj, b_pad, dcf)


def _fox_bias_q(cfc, p, h):
    lane = lax.broadcasted_iota(jnp.int32, (1, LANES), 1)
    return jnp.sum(jnp.where(lane == 2 * p + h, cfc, 0.0), axis=1, keepdims=True)


def _fox_score_bound(q, kmax_row, hm):
    out = []
    for h in range(2):
        qnorm = jnp.sqrt(jnp.sum(jnp.where(hm[h], q * q, 0.0), axis=1, keepdims=True))
        out.append(1.02 * qnorm * kmax_row[:, h * HEAD_DIM:h * HEAD_DIM + 1])
    return out


def _fox_live(cfr_ref, j, t, tops):
    jc = jnp.maximum(j, 0)
    worst = []
    for h in range(2):
        cf_min = jnp.min(cfr_ref[0, pl.ds(h, 1), _rows(jc, t)], axis=1, keepdims=True)
        worst.append(jnp.max(tops[h] - cf_min))
    return (j >= 0) & (jnp.maximum(worst[0], worst[1]) > -SKIP_LOG)


def _chip_gather_copies(x_refs, out_refs, send_sems, recv_sems, local_sems):
    ids = (lax.axis_index("x"), lax.axis_index("y"), lax.axis_index("c"))
    chip = 2 * ids[0] + ids[1]
    copies = []
    for n, (x_ref, out_ref) in enumerate(zip(x_refs, out_refs)):
        copies.append(pltpu.make_async_copy(x_ref, out_ref.at[chip], local_sems.at[n]))
        for kk, (flip_x, flip_y) in enumerate(((1, 0), (0, 1), (1, 1))):
            peer = (1 - ids[0] if flip_x else ids[0], 1 - ids[1] if flip_y else ids[1], ids[2])
            copies.append(pltpu.make_async_remote_copy(
                src_ref=x_ref, dst_ref=out_ref.at[chip],
                send_sem=send_sems.at[3 * n + kk], recv_sem=recv_sems.at[3 * n + kk],
                device_id=peer, device_id_type=MESH))
    return copies


def _fox_fwd(proj, qn, kn, cf, cf_rows, kmax, *, name, gathers=()):
    s = proj.shape[0]
    t = min(ATT_T, s)
    scale = HEAD_DIM ** -0.5
    n_pairs, nq, ng = FOX_W // LANES, s // t, len(gathers)

    def body(*refs):
        q_ref, k_ref, v_ref, cfc_ref, cfr_ref, kmax_ref = refs[:6]
        o_ref, lse_ref = refs[6 + ng:8 + ng]
        p, i = pl.program_id(0), pl.program_id(1)
        if ng:
            def copies():
                return _chip_gather_copies(refs[6:6 + ng], refs[8 + ng:8 + 2 * ng], *refs[8 + 2 * ng:])

            @pl.when((p == 0) & (i == 0))
            def _():
                for cp in copies():
                    cp.start()
        hm = _lane_masks()
        q = q_ref[...] * scale
        qh = [jnp.where(mk, q, 0.0).astype(BF16) for mk in hm]
        cfc = cfc_ref[...]
        bq = [_fox_bias_q(cfc, p, h) for h in range(2)]
        qk_top = _fox_score_bound(q, kmax_ref[...], hm)
        r, c = _tri_iotas(t)
        causal = c <= r

        q2 = jnp.concatenate(qh, axis=0)
        causal2 = jnp.concatenate([causal, causal], axis=0)

        def scores(j):
            return _dot_nt(q2, k_ref[_rows(j, t), :].astype(BF16))

        def chunk(j, carry, z2, masked):
            m_run, l_run, acc = carry
            vb = v_ref[_rows(j, t), :].astype(BF16)
            z = jnp.concatenate(
                [z2[h * t:(h + 1) * t] + (bq[h] - cfr_ref[0, pl.ds(h, 1), _rows(j, t)])
                 for h in range(2)], axis=0)
            if masked:
                z = jnp.where(causal2, z, -1e30)
            m_new = jnp.maximum(m_run, jnp.max(z, axis=1, keepdims=True))
            alpha = jnp.exp(m_run - m_new)
            pr = jnp.exp(z - m_new)
            return (m_new, alpha * l_run + jnp.sum(pr, axis=1, keepdims=True),
                    alpha * acc + _dot(pr.astype(BF16), vb))

        init = (jnp.full((2 * t, 1), -1e30, F32), jnp.zeros((2 * t, 1), F32),
                jnp.zeros((2 * t, LANES), F32))
        carry = chunk(i, init, scores(i), True)

        def live(j, cr):
            return _fox_live(cfr_ref, j, t,
                             [qk_top[h] + bq[h] - cr[0][h * t:(h + 1) * t] for h in range(2)])

        def step(st):
            j, _, cr, z2 = st
            z2_next = scores(jnp.maximum(j - 1, 0))
            cr = chunk(j, cr, z2, False)
            return j - 1, live(j - 1, cr), cr, z2_next

        m_fin, l_fin, acc = lax.while_loop(
            lambda st: st[1], step,
            (i - 1, live(i - 1, carry), carry, scores(jnp.maximum(i - 1, 0))))[2]
        o2 = acc / l_fin
        lse2 = m_fin + jnp.log(l_fin)
        o_ref[...] = jnp.where(hm[0], o2[:t], o2[t:])
        lse_ref[...] = jnp.where(hm[0], lse2[:t], lse2[t:])
        if ng:
            @pl.when((p == n_pairs - 1) & (i == nq - 1))
            def _():
                for cp in copies():
                    cp.wait()

    blk = pl.BlockSpec((t, LANES), lambda p, i: (i, p))
    full = pl.BlockSpec((s, LANES), lambda p, i: (0, p))
    any_spec = pl.BlockSpec(memory_space=pl.ANY)
    dma = pltpu.SemaphoreType.DMA
    return _pcall(body, name=name, grid=(n_pairs, nq),
                  in_specs=[blk, full, pl.BlockSpec((s, LANES), lambda p, i: (0, CB_VB + p)),
                            pl.BlockSpec((t, LANES), lambda p, i: (i, 0)),
                            pl.BlockSpec((1, 2, s), lambda p, i: (p, 0, 0)),
                            pl.BlockSpec((1, LANES), lambda p, i: (0, p))] + [any_spec] * ng,
                  out_specs=[blk, blk] + [any_spec] * ng,
                  out_shape=[jax.ShapeDtypeStruct((s, FOX_W), F32)] * 2
                  + [jax.ShapeDtypeStruct((4,) + g.shape, g.dtype) for g in gathers],
                  scratch_shapes=[dma((3 * ng,)), dma((3 * ng,)), dma((ng,))] if ng else [],
                  semantics=("arbitrary", "arbitrary") if ng else ("parallel", "arbitrary"))(
                      qn, kn, proj, cf, cf_rows, kmax, *gathers)


def _fox_bwd(proj, qn, kn, cf, cf_rows, kmax, do, o, lse, *, name):
    s = proj.shape[0]
    t = min(ATT_T, s)
    scale = HEAD_DIM ** -0.5

    def body(q_ref, k_ref, v_ref, cfc_ref, cfr_ref, kmax_ref, do_ref, o_ref, lse_ref,
             dq_ref, dk_ref, dv_ref, dcf_ref, dcfq_ref):
        p, i = pl.program_id(0), pl.program_id(1)

        @pl.when(i == 0)
        def _():
            dk_ref[...] = jnp.zeros_like(dk_ref)
            dv_ref[...] = jnp.zeros_like(dv_ref)
            dcf_ref[...] = jnp.zeros_like(dcf_ref)

        hm = _lane_masks()
        q = q_ref[...] * scale
        do = do_ref[...]
        dov = do * o_ref[...]
        qh = [jnp.where(mk, q, 0.0).astype(BF16) for mk in hm]
        doh = [jnp.where(mk, do, 0.0).astype(BF16) for mk in hm]
        delta = [jnp.sum(jnp.where(mk, dov, 0.0), axis=1, keepdims=True) for mk in hm]
        lsev = lse_ref[...]
        lse = [lsev[:, 0:1], lsev[:, HEAD_DIM:HEAD_DIM + 1]]
        cfc = cfc_ref[...]
        bq = [_fox_bias_q(cfc, p, h) - lse[h] for h in range(2)]
        qk_top = _fox_score_bound(q, kmax_ref[...], hm)
        tops = [qk_top[h] + bq[h] for h in range(2)]
        r, c = _tri_iotas(t)
        j_stop = lax.while_loop(lambda st: st[1],
                                lambda st: (st[0] - 1, _fox_live(cfr_ref, st[0] - 1, t, tops)),
                                (i - 1, _fox_live(cfr_ref, i - 1, t, tops)))[0]
        q2 = jnp.concatenate(qh, axis=0)
        do2 = jnp.concatenate(doh, axis=0)
        delta2 = jnp.concatenate(delta, axis=0)
        causal2 = jnp.concatenate([c <= r, c <= r], axis=0)

        def products(j):
            return (_dot_nt(q2, k_ref[_rows(j, t), :].astype(BF16)),
                    _dot_nt(do2, v_ref[_rows(j, t), :].astype(BF16)))

        def chunk(j, carry, z2, dp, masked):
            dq, row_sum = carry
            z = jnp.concatenate(
                [z2[h * t:(h + 1) * t] + (bq[h] - cfr_ref[0, pl.ds(h, 1), _rows(j, t)])
                 for h in range(2)], axis=0)
            pr = jnp.exp(z)
            if masked:
                pr = jnp.where(causal2, pr, 0.0)
            ds = pr * (dp - delta2)
            dsb = ds.astype(BF16)
            dk_ref[_rows(j, t), :] += _dot_tn(dsb, q2)
            dv_ref[_rows(j, t), :] += _dot_tn(pr.astype(BF16), do2)
            for h in range(2):
                dcf_ref[0, pl.ds(h, 1), _rows(j, t)] -= jnp.sum(ds[h * t:(h + 1) * t], axis=0,
                                                               keepdims=True)
            return (dq + _dot(dsb, k_ref[_rows(j, t), :].astype(BF16)),
                    row_sum + jnp.sum(ds, axis=1, keepdims=True))

        def one(j, cr):
            return chunk(j, cr, *products(j), False)

        init = (jnp.zeros((2 * t, LANES), F32), jnp.zeros((2 * t, 1), F32))
        first, odd = j_stop + 1, (i - j_stop - 1) % 2
        carry = lax.cond(odd == 1, lambda cr: one(first, cr), lambda cr: cr, init)
        carry = lax.fori_loop(0, (i - first) // 2,
                              lambda n, cr: one(first + odd + 2 * n + 1,
                                                one(first + odd + 2 * n, cr)), carry)
        dq2, row_sum = chunk(i, carry, *products(i), True)
        dq_ref[...] = jnp.where(hm[0], dq2[:t], dq2[t:]) * scale
        dcfq_ref[...] = jnp.where(hm[0], row_sum[:t], row_sum[t:])

    blk = pl.BlockSpec((t, LANES), lambda p, i: (i, p))
    full = pl.BlockSpec((s, LANES), lambda p, i: (0, p))
    rows = pl.BlockSpec((1, 2, s), lambda p, i: (p, 0, 0))
    return _pcall(body, name=name, grid=(FOX_W // LANES, s // t),
                  in_specs=[blk, full, pl.BlockSpec((s, LANES), lambda p, i: (0, CB_VB + p)),
                            pl.BlockSpec((t, LANES), lambda p, i: (i, 0)), rows,
                            pl.BlockSpec((1, LANES), lambda p, i: (0, p)),
                            pl.BlockSpec((t, LANES), lambda p, i: (i, SB_W // LANES + p)),
                            blk, blk],
                  out_specs=[blk, full, full, rows, blk],
                  out_shape=[jax.ShapeDtypeStruct((s, FOX_W), F32)] * 3
                  + [jax.ShapeDtypeStruct((FOX_W // LANES, 2, s), F32),
                     jax.ShapeDtypeStruct((s, FOX_W), F32)],
                  semantics=("parallel", "arbitrary"))(qn, kn, proj, cf, cf_rows, kmax, do, o, lse)


_GELU_C0 = math.sqrt(2.0 / math.pi)
_GELU_C1 = 0.044715


def _gelu(x):
    th = jnp.tanh(_GELU_C0 * (x + _GELU_C1 * (x * x * x)))
    return 0.5 * x * (1.0 + th), th


def _gelu_grad(x, th):
    return 0.5 * (1.0 + th) + 0.5 * x * (1.0 - th * th) * (_GELU_C0 * (1.0 + 3.0 * _GELU_C1 * x * x))


def _sgu_mix(wm, vn_c, lo, bcol):
    return jnp.where(lo, _dot(wm[0], vn_c) + bcol[0], _dot(wm[1], vn_c) + bcol[1])


def _sgu_fwd(proj, w, b_cols, gn, *, name):
    s = proj.shape[0]
    tr = min(512, s)
    ch = SGU_CHUNK

    def body(u_ref, v_ref, w_ref, b_ref, gn_ref, o_ref):
        lo = _lane_masks()[0]
        r, c = _tri_iotas(ch)
        wm = [jnp.where(c <= r, w_ref[h], 0.0).astype(BF16) for h in range(2)]
        bcol = [b_ref[0, :, h:h + 1] for h in range(2)]
        for n in range(tr // ch):
            rows = slice(n * ch, (n + 1) * ch)
            u, _ = _gelu(u_ref[rows, :])
            vg, _ = _gelu(v_ref[rows, :])
            vn = vg * lax.rsqrt(_group_mean(vg * vg, lo) + EPS) * gn_ref[0]
            o_ref[rows, :] = u * _sgu_mix(wm, vn.astype(BF16), lo, bcol)

    blk = lambda cb: pl.BlockSpec((tr, LANES), lambda p, i: (i, cb + p))
    return _pcall(body, name=name, grid=(SGU_W // LANES, s // tr),
                  in_specs=[blk(CB_UC), blk(CB_VC),
                            pl.BlockSpec((2, ch, ch), lambda p, i: (p, 0, 0)),
                            pl.BlockSpec((1, ch, 2), lambda p, i: (p, 0, 0)),
                            pl.BlockSpec((1, 1, LANES), lambda p, i: (p, 0, 0))],
                  out_specs=pl.BlockSpec((tr, LANES), lambda p, i: (i, p)),
                  out_shape=jax.ShapeDtypeStruct((s, SGU_W), F32),
                  semantics=("parallel", "parallel"))(proj, proj, w, b_cols, gn)


def _sgu_bwd(proj, dmixed, w, w_t, b_cols, gn, *, name):
    s = proj.shape[0]
    tr = min(512, s)
    ch = SGU_CHUNK
    cb_do = (SB_W + FOX_W) // LANES

    def body(u_ref, v_ref, do_ref, w_ref, wt_ref, b_ref, gn_ref,
             du_ref, dv_ref, dw_ref, db_ref, dgn_ref):
        @pl.when(pl.program_id(1) == 0)
        def _():
            dw_ref[...] = jnp.zeros_like(dw_ref)
            db_ref[...] = jnp.zeros_like(db_ref)
            dgn_ref[...] = jnp.zeros_like(dgn_ref)

        hm = _lane_masks()
        lo = hm[0]
        r, c = _tri_iotas(ch)
        wm = [jnp.where(c <= r, w_ref[h], 0.0).astype(BF16) for h in range(2)]
        wtm = [jnp.where(r <= c, wt_ref[h], 0.0).astype(BF16) for h in range(2)]
        bcol = [b_ref[0, :, h:h + 1] for h in range(2)]
        gnv = gn_ref[0]
        for n in range(tr // ch):
            rows = slice(n * ch, (n + 1) * ch)
            uc, vc, do = u_ref[rows, :], v_ref[rows, :], do_ref[rows, :]
            u, thu = _gelu(uc)
            vg, thv = _gelu(vc)
            rinv = lax.rsqrt(_group_mean(vg * vg, lo) + EPS)
            xh = vg * rinv
            vnb = (xh * gnv).astype(BF16)
            mix = _sgu_mix(wm, vnb, lo, bcol)
            du_ref[rows, :] = do * mix * _gelu_grad(uc, thu)
            dm = do * u
            dmb = dm.astype(BF16)
            dvn = jnp.where(lo, _dot(wtm[0], dmb), _dot(wtm[1], dmb))
            for h in range(2):
                dmh = jnp.where(hm[h], dm, 0.0)
                dw_ref[h] += jnp.where(c <= r, _dot_nt(dmh.astype(BF16), vnb), 0.0)
                db_ref[0, :, h:h + 1] += jnp.sum(dmh, axis=1, keepdims=True)
            dgn_ref[0] += _colsum(dvn * xh)
            dxh = dvn * gnv
            dvg = rinv * (dxh - xh * _group_mean(dxh * xh, lo))
            dv_ref[rows, :] = dvg * _gelu_grad(vc, thv)

    blk = lambda cb: pl.BlockSpec((tr, LANES), lambda p, i: (i, cb + p))
    w_spec = pl.BlockSpec((2, ch, ch), lambda p, i: (p, 0, 0))
    b_spec = pl.BlockSpec((1, ch, 2), lambda p, i: (p, 0, 0))
    g_spec = pl.BlockSpec((1, 1, LANES), lambda p, i: (p, 0, 0))
    out_blk = pl.BlockSpec((tr, LANES), lambda p, i: (i, p))
    return _pcall(body, name=name, grid=(SGU_W // LANES, s // tr),
                  in_specs=[blk(CB_UC), blk(CB_VC), blk(cb_do), w_spec, w_spec, b_spec, g_spec],
                  out_specs=[out_blk, out_blk, w_spec, b_spec, g_spec],
                  out_shape=[jax.ShapeDtypeStruct((s, SGU_W), F32)] * 2
                  + [jax.ShapeDtypeStruct(w.shape, F32), jax.ShapeDtypeStruct(b_cols.shape, F32),
                     jax.ShapeDtypeStruct(gn.shape, F32)],
                  semantics=("parallel", "arbitrary"))(proj, proj, dmixed, w, w_t, b_cols, gn)


def _pad_lanes(v):
    return jnp.zeros((1, LANES), F32).at[0, :v.shape[0]].set(v)


def _small_views(sm):
    return dict(
        n1=sm["norm1_g"][None, :], n2=sm["norm2_g"][None, :],
        b_pad=_pad_lanes(sm["b_forget"]),
        qg=jnp.tile(sm["q_norm_g"], 2)[None, :], kg=jnp.tile(sm["k_norm_g"], 2)[None, :],
        gn=sm["sgu_norm_g"].reshape(2, 1, LANES),
        w=sm["sgu_w"], w_t=jnp.swapaxes(sm["sgu_w"], 1, 2),
        b_cols=sm["sgu_b"].reshape(2, 2, SGU_CHUNK).transpose(0, 2, 1))


def _cf_rows(cf):
    return cf[:, :FOX_HEADS].T.reshape(FOX_W // LANES, 2, cf.shape[0])


def _layer_fwd(x_in, prev, mod, wts, sm, l, gathers=(), late_weights=None):
    sh1, sc1, g1, sh2, sc2, g2 = mod
    v = _small_views(sm)
    if prev is None:
        x0 = x_in
        h1 = _norm_mod_fwd(x0, v["n1"], sc1, sh1, name=f"l{l}_norm1")
    else:
        x0, h1 = _resid_norm_mod_fwd(x_in, prev[0], prev[1], v["n1"], sc1, sh1, name=f"l{l}_norm1")
    proj = _matmul(h1, wts["w_in"], name=f"l{l}_proj")
    o_sb, sb_ltot, sb_stop = _sb_fwd(proj, name=f"l{l}_sb_fwd")
    qn, kn, kmax = _fox_prep_fwd(proj, v["qg"], v["kg"], name=f"l{l}_fox_prep")
    cf = _forget_cumsum_fwd(proj, v["b_pad"], name=f"l{l}_cumf")
    cfr = _cf_rows(cf)
    o_fox, lse, *gathered = _fox_fwd(proj, qn, kn, cf, cfr, kmax, name=f"l{l}_fox_fwd",
                                     gathers=gathers)
    if gathers:
        late_weights(gathered)
    o_sgu = _sgu_fwd(proj, v["w"], v["b_cols"], v["gn"], name=f"l{l}_sgu_fwd")
    mixed = jnp.concatenate([o_sb, o_fox, o_sgu], axis=1).astype(BF16)
    mo = _matmul(mixed, wts["w_out"], name=f"l{l}_wout")
    x1, h2 = _resid_norm_mod_fwd(x0, mo, g1, v["n2"], sc2, sh2, name=f"l{l}_norm2")
    a, rr = _matmul(h2, wts["w1"], name=f"l{l}_mlp1", out_dtype=BF16, relu2=BF16)
    m2 = _matmul(rr, wts["w2"], name=f"l{l}_mlp2")
    saved = dict(x0=x0, h1=h1, proj=proj, sb_ltot=sb_ltot, sb_stop=sb_stop, qn=qn, kn=kn, kmax=kmax, cf=cf, cfr=cfr, o_fox=o_fox,
                 lse=lse, mixed=mixed, mo=mo, x1=x1, h2=h2, a=a, rr=rr, m2=m2)
    return saved


def _layer_bwd(dx2, dm2, dg2, sv, mod, wts, sm, l, below):
    sh1, sc1, g1, sh2, sc2, g2 = mod
    v = _small_views(sm)
    dw2 = _matmul(sv["rr"], dm2, ta=True, name=f"l{l}_dw2")
    da = _matmul(dm2, wts["w2"], tb=True, name=f"l{l}_da", out_dtype=BF16, pre_act=sv["a"])
    dw1 = _matmul(sv["h2"], da, ta=True, name=f"l{l}_dw1")
    dh2 = _matmul(da, wts["w1"], tb=True, name=f"l{l}_dh2")
    dx1, dn2, dsc2, dsh2, dmo, dg1 = _norm_mod_bwd(sv["x1"], dh2, dx2, v["n2"], sc2,
                                                    (sv["mo"], g1), name=f"l{l}_norm2_bwd")
    dwo = _matmul(sv["mixed"], dmo, ta=True, name=f"l{l}_dwout")
    dmixed = _matmul(dmo, wts["w_out"], tb=True, name=f"l{l}_dmixed")
    proj = sv["proj"]
    dqa, dka, dva = _sb_bwd(proj, dmixed, sv["sb_ltot"], sv["sb_stop"], name=f"l{l}_sb_bwd")
    dqn, dkn, dvb, dcfr, dcfq = _fox_bwd(proj, sv["qn"], sv["kn"], sv["cf"], sv["cfr"], sv["kmax"], dmixed,
                                   sv["o_fox"], sv["lse"], name=f"l{l}_fox_bwd")
    dqb, dkb, dqg, dkg = _fox_prep_bwd(proj, dqn, dkn, v["qg"], v["kg"], name=f"l{l}_fox_prep_bwd")
    s = proj.shape[0]
    dcf_heads = dcfr.reshape(FOX_HEADS, s).T + dcfq.reshape(s, FOX_HEADS, HEAD_DIM)[:, :, 0]
    dcf = jnp.zeros((s, LANES), F32).at[:, :FOX_HEADS].set(dcf_heads)
    dfl, dbf = _forget_cumsum_bwd(proj, v["b_pad"], dcf, name=f"l{l}_cumf_bwd")
    duc, dvc, dsw, dsb_cols, dgn = _sgu_bwd(proj, dmixed, v["w"], v["w_t"], v["b_cols"], v["gn"],
                                            name=f"l{l}_sgu_bwd")
    dproj = jnp.concatenate([dqa, dka, dva, dqb, dkb, dvb, duc, dvc, dfl,
                             jnp.zeros((s, LANES), F32)], axis=1).astype(BF16)
    dwin = _matmul(sv["h1"], dproj, ta=True, name=f"l{l}_dwin")
    dh1 = _matmul(dproj, wts["w_in"], tb=True, name=f"l{l}_dh1")
    dx0, dn1, dsc1, dsh1, dm_below, dg_below = _norm_mod_bwd(sv["x0"], dh1, dx1, v["n1"], sc1, below,
                                                             name=f"l{l}_norm1_bwd")
    big = dict(w_in=dwin, w_out=dwo, w1=dw1, w2=dw2)
    small = dict(norm1_g=dn1[0], norm2_g=dn2[0], b_forget=dbf[0, :FOX_HEADS],
                 q_norm_g=dqg[0, :HEAD_DIM] + dqg[0, HEAD_DIM:],
                 k_norm_g=dkg[0, :HEAD_DIM] + dkg[0, HEAD_DIM:],
                 sgu_norm_g=dgn.reshape(4, HEAD_DIM), sgu_w=dsw,
                 sgu_b=dsb_cols.transpose(0, 2, 1).reshape(4, SGU_CHUNK))
    dmod = jnp.concatenate([dsh1, dsc1, dg1, dsh2, dsc2, dg2], axis=1)
    return dx0, dm_below, dg_below, big, small, dmod


def _w_in_to_internal(w):
    pad = jnp.zeros((w.shape[0], PROJ_W - IN_W), w.dtype)
    return jnp.concatenate([w[:, :ATT_W], w[:, ATT_W + FOX_HEADS:], w[:, ATT_W:ATT_W + FOX_HEADS],
                            pad], axis=1)


def _w_in_from_internal(g):
    n_gate = SGU_W * 2
    return jnp.concatenate([g[:, :ATT_W], g[:, ATT_W + n_gate:ATT_W + n_gate + FOX_HEADS],
                            g[:, ATT_W:ATT_W + n_gate]], axis=1)


def _exchange(x, masks, slot_shift, slot_bits, scatter, *, name):
    n_slots = 2 ** slot_bits
    blk_shape = x.shape[1:] if scatter else x.shape
    n_peers = len(masks)

    def body(x_ref, out_ref, send_sems, recv_sems, local_sem):
        ids = (lax.axis_index("x"), lax.axis_index("y"), lax.axis_index("c"))
        me = 4 * ids[0] + 2 * ids[1] + ids[2]
        my_slot = (me >> slot_shift) & (n_slots - 1)

        def peer(mask):
            return tuple(1 - v if (mask >> b) & 1 else v for v, b in zip(ids, (2, 1, 0)))

        def src_for(slot):
            return x_ref.at[slot] if scatter else x_ref

        copies = [pltpu.make_async_copy(src_for(my_slot), out_ref.at[my_slot], local_sem)]
        for kk, mask in enumerate(masks):
            peer_slot = ((me ^ mask) >> slot_shift) & (n_slots - 1)
            copies.append(pltpu.make_async_remote_copy(
                src_ref=src_for(peer_slot), dst_ref=out_ref.at[my_slot],
                send_sem=send_sems.at[kk], recv_sem=recv_sems.at[kk],
                device_id=peer(mask), device_id_type=MESH))
        for cp in copies:
            cp.start()
        for cp in copies:
            cp.wait()

    any_spec = pl.BlockSpec(memory_space=pl.ANY)
    return _pcall(body, name=name, in_specs=[any_spec], out_specs=any_spec,
                  out_shape=jax.ShapeDtypeStruct((n_slots,) + tuple(blk_shape), x.dtype),
                  scratch_shapes=[pltpu.SemaphoreType.DMA((n_peers,)),
                                  pltpu.SemaphoreType.DMA((n_peers,)),
                                  pltpu.SemaphoreType.DMA(())])(x)


CORE_PIECE_BYTES = 12 * 2 ** 20
CORE_DMA_CHUNKS = 4


def _core_swap_piece(x, *, name):
    rows, cols = x.shape
    n_ch = CORE_DMA_CHUNKS if rows % (16 * CORE_DMA_CHUNKS) == 0 else 1
    rc = rows // n_ch

    def body(x_ref, out_ref, send_sems, recv_sems):
        sibling = (lax.axis_index("x"), lax.axis_index("y"), 1 - lax.axis_index("c"))
        copies = [pltpu.make_async_remote_copy(
            src_ref=x_ref.at[pl.ds(ch * rc, rc)], dst_ref=out_ref.at[pl.ds(ch * rc, rc)],
            send_sem=send_sems.at[ch], recv_sem=recv_sems.at[ch],
            device_id=sibling, device_id_type=MESH) for ch in range(n_ch)]
        for cp in copies:
            cp.start()
        for cp in copies:
            cp.wait()

    vmem = pl.BlockSpec(memory_space=pltpu.VMEM)
    return _pcall(body, name=name, in_specs=[vmem], out_specs=vmem,
                  out_shape=jax.ShapeDtypeStruct(x.shape, x.dtype),
                  scratch_shapes=[pltpu.SemaphoreType.DMA((n_ch,)),
                                  pltpu.SemaphoreType.DMA((n_ch,))])(x)


def _core_swap(x, *, name):
    rows, cols = x.shape
    n = 1
    while (rows % n or (rows // n) % 16 or
           (rows // n) * (-(-cols // LANES) * LANES) * x.dtype.itemsize > CORE_PIECE_BYTES):
        n += 1
    pr = rows // n
    pieces = [_core_swap_piece(x[kk * pr:(kk + 1) * pr], name=f"{name}_{kk}") for kk in range(n)]
    return pieces[0] if n == 1 else jnp.concatenate(pieces, axis=0)


def _by_core(core, mine, theirs, axis):
    return jnp.where(core == 0, jnp.concatenate([mine, theirs], axis=axis),
                     jnp.concatenate([theirs, mine], axis=axis))


def _gather_chips(x, *, name):
    return _exchange(x, (2, 4, 6), 1, 2, False, name=name)


def _gather_all(x, *, name):
    return _exchange(x, (1, 2, 3, 4, 5, 6, 7), 0, 3, False, name=name)


def _scatter_chips(x4, *, name):
    return _exchange(x4, (2, 4, 6), 1, 2, True, name=name)


def _sum_slots(parts, *, name, out_dtype=F32, tr=256):
    n, rows, cols = parts.shape
    tr = min(tr, rows)
    assert rows % tr == 0, (name, rows, tr)

    def body(p_ref, o_ref):
        acc = p_ref[0].astype(F32)
        for kk in range(1, n):
            acc = acc + p_ref[kk].astype(F32)
        o_ref[...] = acc.astype(o_ref.dtype)

    return _pcall(body, name=name, grid=(rows // tr,),
                  in_specs=[pl.BlockSpec((n, tr, cols), lambda i: (0, i, 0))],
                  out_specs=pl.BlockSpec((tr, cols), lambda i: (i, 0)),
                  out_shape=jax.ShapeDtypeStruct((rows, cols), out_dtype),
                  semantics=("parallel",))(parts)


def _add2(a, b, *, name, out_dtype, tr=512):
    def fn(f, v):
        return [f[0] + f[1]], []
    (out,), _ = _rowwise(fn, [a, b], [], [out_dtype], 0, name=name, tr=tr)
    return out


def _adamw(w, m, v, parts, *, name, tr=256):
    n, rows, cols = parts.shape
    tr = min(tr, rows)
    assert rows % tr == 0, (name, rows, tr)
    c1 = 1.0 - ADAM_B1 ** ADAM_STEP
    c2 = 1.0 - ADAM_B2 ** ADAM_STEP

    def body(w_ref, m_ref, v_ref, p_ref, g_ref, d_ref, nm_ref, nv_ref):
        g = p_ref[0]
        for kk in range(1, n):
            g = g + p_ref[kk]
        nm = ADAM_B1 * m_ref[...] + (1.0 - ADAM_B1) * g
        nv = ADAM_B2 * v_ref[...] + (1.0 - ADAM_B2) * (g * g)
        g_ref[...] = g
        nm_ref[...] = nm
        nv_ref[...] = nv
        d_ref[...] = -ADAM_LR * ((nm / c1) / (jnp.sqrt(nv / c2) + ADAM_EPS) + ADAM_WD * w_ref[...])

    spec = pl.BlockSpec((tr, cols), lambda i: (i, 0))
    return _pcall(body, name=name, grid=(rows // tr,),
                  in_specs=[spec, spec, spec, pl.BlockSpec((n, tr, cols), lambda i: (0, i, 0))],
                  out_specs=[spec] * 4,
                  out_shape=[jax.ShapeDtypeStruct((rows, cols), F32)] * 4,
                  semantics=("parallel",))(w, m, v, parts)


def _silu(c):
    return c / (1.0 + jnp.exp(-c))


def _ada_fwd(c_all, ada_w, ada_b_sh, *, name):
    nl, d, wsh = ada_w.shape

    def body(c_ref, w_ref, b_ref, o_ref):
        cond = _silu(c_ref[...]).astype(BF16)
        o_ref[0] = _dot(cond, w_ref[0].astype(BF16)) + b_ref[0]

    return _pcall(body, name=name, grid=(nl,),
                  in_specs=[pl.BlockSpec(c_all.shape, lambda l: (0, 0)),
                            pl.BlockSpec((1, d, wsh), lambda l: (l, 0, 0)),
                            pl.BlockSpec((1, 1, wsh), lambda l: (l, 0, 0))],
                  out_specs=pl.BlockSpec((1, c_all.shape[0], wsh), lambda l: (l, 0, 0)),
                  out_shape=jax.ShapeDtypeStruct((nl, c_all.shape[0], wsh), F32),
                  semantics=("parallel",))(c_all, ada_w, ada_b_sh)


def _ada_bwd(c_all, dmod_sh, *, name):
    nl, nb, wsh = dmod_sh.shape
    d = c_all.shape[1]

    def body(c_ref, dm_ref, o_ref):
        cond = _silu(c_ref[...]).astype(BF16)
        o_ref[0] = _dot_tn(cond, dm_ref[0].astype(BF16))

    return _pcall(body, name=name, grid=(nl,),
                  in_specs=[pl.BlockSpec(c_all.shape, lambda l: (0, 0)),
                            pl.BlockSpec((1, nb, wsh), lambda l: (l, 0, 0))],
                  out_specs=pl.BlockSpec((1, d, wsh), lambda l: (l, 0, 0)),
                  out_shape=jax.ShapeDtypeStruct((nl, d, wsh), F32),
                  semantics=("parallel",))(c_all, dmod_sh)


SMALL_NAMES = ("norm1_g", "norm2_g", "b_forget", "q_norm_g", "k_norm_g", "sgu_norm_g", "sgu_w",
               "sgu_b")
WEIGHT_NAMES = ("ada_w", "ada_b", "norm1_g", "norm2_g", "w_in", "b_forget", "q_norm_g", "k_norm_g",
                "sgu_norm_g", "sgu_w", "sgu_b", "w_out", "mlp_w1", "mlp_w2")


SMALL_TILE_ROWS = 256


def _pack_small(tree):
    flat = jnp.concatenate([tree[n].reshape(-1) for n in SMALL_NAMES])
    n = flat.shape[0]
    rows = -(-n // (SMALL_TILE_ROWS * LANES)) * SMALL_TILE_ROWS
    return jnp.zeros((rows * LANES,), F32).at[:n].set(flat).reshape(rows, LANES)


def _unpack_small(packed, like):
    flat = packed.reshape(-1)
    out, off = {}, 0
    for n in SMALL_NAMES:
        size = like[n].size
        out[n] = flat[off:off + size].reshape(like[n].shape)
        off += size
    return out


def kernel(x, c, ada_w, ada_b, norm1_g, norm2_g, w_in, b_forget, q_norm_g, k_norm_g, sgu_norm_g, sgu_w, sgu_b, w_out, mlp_w1, mlp_w2, loss_target, m_ada_w, m_ada_b, m_norm1_g, m_norm2_g, m_w_in, m_b_forget, m_q_norm_g, m_k_norm_g, m_sgu_norm_g, m_sgu_w, m_sgu_b, m_w_out, m_mlp_w1, m_mlp_w2, v_ada_w, v_ada_b, v_norm1_g, v_norm2_g, v_w_in, v_b_forget, v_q_norm_g, v_k_norm_g, v_sgu_norm_g, v_sgu_w, v_sgu_b, v_w_out, v_mlp_w1, v_mlp_w2):
    w = dict(ada_w=ada_w, ada_b=ada_b, norm1_g=norm1_g, norm2_g=norm2_g, w_in=w_in,
             b_forget=b_forget, q_norm_g=q_norm_g, k_norm_g=k_norm_g, sgu_norm_g=sgu_norm_g,
             sgu_w=sgu_w, sgu_b=sgu_b, w_out=w_out, mlp_w1=mlp_w1, mlp_w2=mlp_w2)
    mom = dict(ada_w=m_ada_w, ada_b=m_ada_b, norm1_g=m_norm1_g, norm2_g=m_norm2_g, w_in=m_w_in,
               b_forget=m_b_forget, q_norm_g=m_q_norm_g, k_norm_g=m_k_norm_g,
               sgu_norm_g=m_sgu_norm_g, sgu_w=m_sgu_w, sgu_b=m_sgu_b, w_out=m_w_out,
               mlp_w1=m_mlp_w1, mlp_w2=m_mlp_w2)
    var = dict(ada_w=v_ada_w, ada_b=v_ada_b, norm1_g=v_norm1_g, norm2_g=v_norm2_g, w_in=v_w_in,
               b_forget=v_b_forget, q_norm_g=v_q_norm_g, k_norm_g=v_k_norm_g,
               sgu_norm_g=v_sgu_norm_g, sgu_w=v_sgu_w, sgu_b=v_sgu_b, w_out=v_w_out,
               mlp_w1=v_mlp_w1, mlp_w2=v_mlp_w2)
    depth, d = norm1_g.shape
    chip = 2 * lax.axis_index("x") + lax.axis_index("y")
    me = 2 * chip + lax.axis_index("c")
    n_chips = 4
    ada_sh = ada_w.shape[2]

    core = lax.axis_index("c")
    half_l = depth // 2

    def my_part(w_sh):
        _, r, cols = w_sh.shape
        mine = lax.dynamic_slice_in_dim(w_sh, core * half_l, half_l, axis=0).astype(BF16)
        return mine.reshape(half_l * r, cols)

    def share(got, w_sh, name):
        _, r, cols = w_sh.shape
        theirs = _core_swap(got.reshape(n_chips * half_l * r, cols), name=f"share_{name}")
        return _by_core(core, got.reshape(n_chips, half_l, r, cols),
                        theirs.reshape(n_chips, half_l, r, cols), 1)

    g_in = share(_gather_chips(my_part(w_in), name="gather_w_in"), w_in, "w_in")
    layer_w = [dict(w_in=_w_in_to_internal(
        jnp.concatenate([g_in[k, l] for k in range(n_chips)], axis=1))) for l in range(depth)]
    later = (("w_out", w_out), ("w1", mlp_w1), ("w2", mlp_w2))

    def late_weights(gathered):
        g_out, g_w1, g_w2 = [share(got, w_sh, name) for got, (name, w_sh) in zip(gathered, later)]
        for l in range(depth):
            layer_w[l].update(
                w_out=g_out[:, l].reshape(d, d),
                w1=jnp.concatenate([g_w1[k, l] for k in range(n_chips)], axis=1),
                w2=g_w2[:, l].reshape(D_FF, d))

    c_all = _gather_all(jnp.zeros((8, d), F32).at[0].set(c[0]), name="gather_c")[:, 0]
    c_pad = jnp.concatenate([c_all, jnp.zeros_like(c_all)], axis=0)
    ada_b_sh = lax.dynamic_slice_in_dim(ada_b, chip * ada_sh, ada_sh, axis=1)[:, None, :]
    mod_sh = _ada_fwd(c_pad, ada_w, ada_b_sh, name="ada_fwd")
    mod_all = _gather_chips(mod_sh, name="gather_mod")
    mod_me = lax.dynamic_index_in_dim(mod_all, me, axis=2, keepdims=False)
    mod_me = mod_me.transpose(1, 0, 2).reshape(depth, 6, 1, d)

    saved = []
    xs, prev = x[0], None
    for l in range(depth):
        mod = [mod_me[l, kk] for kk in range(6)]
        sm = {n: w[n][l] for n in SMALL_NAMES}
        first = dict(gathers=[my_part(w_sh) for _, w_sh in later], late_weights=late_weights)
        sv = _layer_fwd(xs, prev, mod, layer_w[l], sm, l, **(first if l == 0 else {}))
        saved.append(sv)
        xs, prev = sv["x1"], (sv["m2"], mod[5])

    sq, dxs, dm2, dg2 = _loss_fwd_bwd(xs, prev[0], prev[1], loss_target[0], name="loss")
    loss = lax.psum(0.5 * jnp.sum(sq) / d, ("x", "y", "c"))

    big = {n: [] for n in ("w_in", "w_out", "w1", "w2")}
    small = {n: [] for n in SMALL_NAMES}
    dmods = []
    for l in reversed(range(depth)):
        mod = [mod_me[l, kk] for kk in range(6)]
        sm = {n: w[n][l] for n in SMALL_NAMES}
        below = (saved[l - 1]["m2"], mod_me[l - 1, 5]) if l else None
        dxs, dm2, dg2, bg, smg, dmod = _layer_bwd(dxs, dm2, dg2, saved[l], mod, layer_w[l], sm, l,
                                                  below)
        for n in big:
            big[n].insert(0, bg[n])
        for n in SMALL_NAMES:
            small[n].insert(0, smg[n])
        dmods.insert(0, dmod)
    grad_x = dxs[None]

    out_g, out_d, out_m, out_v = {}, {}, {}, {}

    def run_adamw(name, parts2d, shape):
        rows, cols = parts2d.shape[1:]
        g, dl, nm, nv = _adamw(w[name].reshape(rows, cols), mom[name].reshape(rows, cols),
                               var[name].reshape(rows, cols), parts2d, name=f"adamw_{name}")
        out_g[name], out_d[name] = g.reshape(shape), dl.reshape(shape)
        out_m[name], out_v[name] = nm.reshape(shape), nv.reshape(shape)

    def shards_of(name, l):
        if name == "w_in":
            g = _w_in_from_internal(big["w_in"][l])
            return jnp.stack(jnp.split(g, n_chips, axis=1))
        if name == "mlp_w1":
            return jnp.stack(jnp.split(big["w1"][l], n_chips, axis=1))
        if name == "w_out":
            return big["w_out"][l].reshape(n_chips, d // n_chips, d)
        return big["w2"][l].reshape(n_chips, D_FF // n_chips, d)

    for name in ("w_in", "w_out", "mlp_w1", "mlp_w2"):
        per_chip = jnp.stack([shards_of(name, l) for l in range(depth)], axis=1)
        r, cols = per_chip.shape[2:]
        half_rows = half_l * r
        keep = lax.dynamic_slice_in_dim(per_chip, core * half_l, half_l, axis=1)
        send = lax.dynamic_slice_in_dim(per_chip, (1 - core) * half_l, half_l, axis=1)
        theirs = _core_swap(send.reshape(n_chips * half_rows, cols), name=f"pair_{name}")
        chip_sum = _add2(keep.reshape(n_chips * half_rows, cols), theirs, out_dtype=BF16,
                         name=f"pairsum_{name}")
        got = _scatter_chips(chip_sum.reshape(n_chips, half_rows, cols), name=f"scatter_{name}")
        half = _sum_slots(got, name=f"sum_{name}")
        both = _by_core(core, half, _core_swap(half, name=f"swap_{name}"), 0)
        run_adamw(name, both[None], w[name].shape)

    small_tree = {n: jnp.stack(small[n]) for n in SMALL_NAMES}
    gathered = _gather_all(_pack_small(small_tree), name="gather_small")
    gs, ds_, ms, vs = _adamw(_pack_small({n: w[n] for n in SMALL_NAMES}),
                             _pack_small({n: mom[n] for n in SMALL_NAMES}),
                             _pack_small({n: var[n] for n in SMALL_NAMES}), gathered,
                             name="adamw_small")
    like = {n: w[n] for n in SMALL_NAMES}
    for tree, packed in ((out_g, gs), (out_d, ds_), (out_m, ms), (out_v, vs)):
        tree.update(_unpack_small(packed, like))

    dmod_mine = jnp.concatenate(dmods, axis=0)
    dmod_all = _gather_all(jnp.zeros((depth, 8, 6 * d), F32).at[:, 0].set(dmod_mine),
                           name="gather_dmod")[:, :, 0]
    dmod_lb = dmod_all.transpose(1, 0, 2)
    dmod_sh = lax.dynamic_slice_in_dim(dmod_lb, chip * ada_sh, ada_sh, axis=2)
    dmod_sh = jnp.concatenate([dmod_sh, jnp.zeros_like(dmod_sh)], axis=1)
    g_ada_w = _ada_bwd(c_pad, dmod_sh, name="ada_bwd")
    run_adamw("ada_w", g_ada_w.reshape(1, depth * d, ada_sh), ada_w.shape)
    parts_b = dmod_all.reshape(8, depth * 6 * d // LANES, LANES)
    run_adamw("ada_b", parts_b, ada_b.shape)

    outs = [loss, grad_x]
    for tree in (out_g, out_d, out_m, out_v):
        outs += [tree[n] for n in WEIGHT_NAMES]
    return tuple(outs)
```

```python
import functools
import math

import jax
import jax.numpy as jnp
from jax import lax
from jax.experimental import pallas as pl
from jax.experimental.pallas import tpu as pltpu

F32 = jnp.float32
BF16 = jnp.bfloat16

D_MODEL = 1024
DEPTH = 4
HEAD_DIM = 64
LANES = 128
D_FF = 4 * D_MODEL
EPS = 1e-6
SB_W, FOX_W, SGU_W = 256, 512, 256
FOX_HEADS = 8
SGU_CHUNK = 128
IN_W = 2824
ATT_W = 3 * SB_W + 3 * FOX_W
PROJ_W = 3072
CB_QA, CB_KA, CB_VA = 0, 2, 4
CB_QB, CB_KB, CB_VB = 6, 10, 14
CB_UC, CB_VC, CB_FL = 18, 20, 22
ATT_T = 256
PREP_ROWS = 2048
VMEM_LIMIT = 56 * 2 ** 20
SKIP_LOG = 110.0

ADAM_LR, ADAM_B1, ADAM_B2, ADAM_EPS, ADAM_WD, ADAM_STEP = 0.001, 0.9, 0.999, 1e-08, 0.01, 10

MESH = pl.DeviceIdType.MESH


def _pcall(body, *, name, out_shape, grid=(), in_specs=None, out_specs=None, scratch_shapes=(),
           semantics=None):
    params = dict(vmem_limit_bytes=VMEM_LIMIT)
    if semantics is not None:
        params["dimension_semantics"] = semantics
    kwargs = {}
    if in_specs is not None:
        kwargs["in_specs"] = in_specs
    if out_specs is not None:
        kwargs["out_specs"] = out_specs
    return pl.pallas_call(body, name=name, out_shape=out_shape, grid=grid,
                          scratch_shapes=list(scratch_shapes),
                          compiler_params=pltpu.CompilerParams(**params), **kwargs)


def _dot(a, b):
    return jnp.dot(a, b, preferred_element_type=F32)


def _dot_nt(a, b):
    return lax.dot_general(a, b, (((1,), (1,)), ((), ())), preferred_element_type=F32)


def _dot_tn(a, b):
    return lax.dot_general(a, b, (((0,), (0,)), ((), ())), preferred_element_type=F32)


def _split2(x):
    hi = x.astype(BF16)
    lo = (x - hi.astype(F32)).astype(BF16)
    return hi, lo


def _ones_dot(x, ones_bf16):
    hi, lo = _split2(x)
    return _dot(hi, ones_bf16) + _dot(lo, ones_bf16)


def _rowwise(fn, fulls, vecs, out_dtypes, n_vec_out, *, name, tr):
    s, n = fulls[0].shape
    tr = min(tr, s)
    assert s % tr == 0, (name, s, tr)
    nf, nv, nfo = len(fulls), len(vecs), len(out_dtypes)

    def body(*refs):
        fi, vi = refs[:nf], refs[nf:nf + nv]
        fo, vo = refs[nf + nv:nf + nv + nfo], refs[nf + nv + nfo:]
        outs_f, outs_v = fn([r[...] for r in fi], [r[...] for r in vi])
        for r, o in zip(fo, outs_f):
            r[...] = o.astype(r.dtype)
        if n_vec_out:
            @pl.when(pl.program_id(0) == 0)
            def _():
                for r in vo:
                    r[...] = jnp.zeros_like(r)
            for r, o in zip(vo, outs_v):
                r[...] += o

    full_spec = pl.BlockSpec((tr, n), lambda i: (i, 0))
    vec_specs = [pl.BlockSpec(v.shape, lambda i: (0, 0)) for v in vecs]
    out_vec_spec = pl.BlockSpec((1, n), lambda i: (0, 0))
    out_shape = [jax.ShapeDtypeStruct((s, n), dt) for dt in out_dtypes]
    out_shape += [jax.ShapeDtypeStruct((1, n), F32)] * n_vec_out
    outs = _pcall(body, name=name, grid=(s // tr,),
                  in_specs=[full_spec] * nf + vec_specs,
                  out_specs=[full_spec] * nfo + [out_vec_spec] * n_vec_out,
                  out_shape=out_shape,
                  semantics=("arbitrary",) if n_vec_out else ("parallel",))(*fulls, *vecs)
    return outs[:nfo], outs[nfo:]


def _colsum(t):
    return jnp.sum(t, axis=0, keepdims=True)


def _rms_mod(x, g, sc, sh):
    r = lax.rsqrt(jnp.mean(x * x, axis=-1, keepdims=True) + EPS)
    return (x * r * g) * (1.0 + sc) + sh


def _norm_mod_fwd(x, g, sc, sh, *, name):
    def fn(f, v):
        return [_rms_mod(f[0], v[0], v[1], v[2])], []
    (h,), _ = _rowwise(fn, [x], [g, sc, sh], [BF16], 0, name=name, tr=512)
    return h


def _resid_norm_mod_fwd(x, m, gate, g, sc, sh, *, name):
    def fn(f, v):
        xn = f[0] + v[0] * f[1]
        return [xn, _rms_mod(xn, v[1], v[2], v[3])], []
    (xn, h), _ = _rowwise(fn, [x, m], [gate, g, sc, sh], [F32, BF16], 0, name=name, tr=512)
    return xn, h


def _norm_mod_bwd(x, dh, dres, g, sc, gated, *, name):
    def fn(f, v):
        xv, dhv, dr = f[:3]
        gv, scv = v[:2]
        r = lax.rsqrt(jnp.mean(xv * xv, axis=-1, keepdims=True) + EPS)
        xh = xv * r
        dn = dhv * (1.0 + scv)
        dxh = dn * gv
        dx = dr + r * (dxh - xh * jnp.mean(dxh * xh, axis=-1, keepdims=True))
        sums = [_colsum(dn * xh), _colsum(dhv * (xh * gv)), _colsum(dhv)]
        if gated is None:
            return [dx], sums
        return [dx, dx * v[2]], sums + [_colsum(dx * f[3])]
    if gated is None:
        (dx,), (dg, dsc, dsh) = _rowwise(fn, [x, dh, dres], [g, sc], [F32], 3, name=name, tr=256)
        return dx, dg, dsc, dsh, None, None
    (dx, dm), (dg, dsc, dsh, dgate) = _rowwise(fn, [x, dh, dres, gated[0]], [g, sc, gated[1]],
                                               [F32, BF16], 4, name=name, tr=256)
    return dx, dg, dsc, dsh, dm, dgate


def _loss_fwd_bwd(x, m, gate, target, *, name):
    n = x.shape[1]

    def fn(f, v):
        err = f[0] + v[0] * f[1] - f[2]
        dy = err * (1.0 / n)
        return [dy, dy * v[0]], [_colsum(err * err), _colsum(dy * f[1])]
    (dy, dm), (sq, dgate) = _rowwise(fn, [x, m, target], [gate], [F32, BF16], 2, name=name, tr=512)
    return sq, dy, dm, dgate


def _matmul(a, b, *, name, ta=False, tb=False, out_dtype=F32, relu2=None, pre_act=None,
            tm=1024, tn=1024, tk_max=2048):
    m = a.shape[1] if ta else a.shape[0]
    k = a.shape[0] if ta else a.shape[1]
    n = b.shape[0] if tb else b.shape[1]
    assert k == (b.shape[1] if tb else b.shape[0])
    tk = max(dd for dd in range(LANES, min(tk_max, k) + 1, LANES) if k % dd == 0) if k > LANES else k
    tm, tn = min(tm, m), min(tn, n)
    assert m % tm == 0 and n % tn == 0 and k % tk == 0, (name, m, n, k)
    nk = k // tk
    dims = (((0 if ta else 1,), (1 if tb else 0,)), ((), ()))

    plain = relu2 is None and pre_act is None
    in_place = plain and out_dtype == F32
    n_in = 2 + (pre_act is not None)

    def body(*refs):
        a_ref, b_ref = refs[:2]
        o_ref = refs[n_in]
        prod = lax.dot_general(a_ref[...].astype(BF16), b_ref[...].astype(BF16), dims,
                               preferred_element_type=F32)

        def finish(acc):
            if pre_act is not None:
                acc = acc * (2.0 * jnp.maximum(refs[2][...].astype(F32), 0.0))
            o_ref[...] = acc.astype(o_ref.dtype)
            if relu2 is not None:
                r = jnp.maximum(acc, 0.0)
                refs[n_in + 1][...] = (r * r).astype(relu2)

        if nk == 1:
            finish(prod)
            return
        kk = pl.program_id(2)
        acc_ref = o_ref if in_place else refs[-1]

        @pl.when(kk == 0)
        def _():
            acc_ref[...] = prod

        @pl.when(kk > 0)
        def _():
            acc_ref[...] += prod

        if not in_place:
            @pl.when(kk == nk - 1)
            def _():
                finish(acc_ref[...])

    a_spec = (pl.BlockSpec((tk, tm), lambda i, j, kk: (kk, i)) if ta
              else pl.BlockSpec((tm, tk), lambda i, j, kk: (i, kk)))
    b_spec = (pl.BlockSpec((tn, tk), lambda i, j, kk: (j, kk)) if tb
              else pl.BlockSpec((tk, tn), lambda i, j, kk: (kk, j)))
    out_spec = pl.BlockSpec((tm, tn), lambda i, j, kk: (i, j))
    in_specs, args = [a_spec, b_spec], [a, b]
    if pre_act is not None:
        in_specs.append(out_spec)
        args.append(pre_act)
    out_specs, out_shape = out_spec, jax.ShapeDtypeStruct((m, n), out_dtype)
    if relu2 is not None:
        out_specs, out_shape = [out_spec] * 2, [out_shape, jax.ShapeDtypeStruct((m, n), relu2)]
    return _pcall(body, name=name, grid=(m // tm, n // tn, nk),
                  in_specs=in_specs, out_specs=out_specs, out_shape=out_shape,
                  scratch_shapes=[] if nk == 1 or in_place else [pltpu.VMEM((tm, tn), F32)],
                  semantics=("parallel", "parallel", "arbitrary"))(*args)


def _lane_masks():
    lane = lax.broadcasted_iota(jnp.int32, (1, LANES), 1)
    return [lane < HEAD_DIM, lane >= HEAD_DIM]


def _tri_iotas(t):
    r = lax.broadcasted_iota(jnp.int32, (t, t), 0)
    c = lax.broadcasted_iota(jnp.int32, (t, t), 1)
    return r, c


def _rows(j, t):
    return pl.ds(pl.multiple_of(j * t, t), t)


def _neg_softplus(z):
    e = jnp.exp(-jnp.abs(z))
    return -(jnp.maximum(z, 0.0) + jnp.log(1.0 + e)), e


def _sb_fwd(proj, *, name):
    s = proj.shape[0]
    t = min(ATT_T, s)
    scale = HEAD_DIM ** -0.5

    def body(q_ref, k_ref, v_ref, o_ref, ltot_ref, stop_ref):
        i = pl.program_id(1)
        hm = _lane_masks()
        q = q_ref[...] * scale
        qh = [jnp.where(mk, q, 0.0).astype(BF16) for mk in hm]
        r, c = _tri_iotas(t)
        later = (r > c).astype(BF16)
        q2 = jnp.concatenate(qh, axis=0)
        causal2 = jnp.concatenate([c < r, c < r], axis=0)

        def scores(j):
            return _dot_nt(q2, k_ref[_rows(j, t), :].astype(BF16))

        def chunk(j, carry, z, masked):
            e_run, acc = carry
            vb = v_ref[_rows(j, t), :].astype(BF16)
            l, _ = _neg_softplus(z)
            if masked:
                l = jnp.where(causal2, l, 0.0)
            between = _ones_dot(l, later) + e_run
            a = jnp.exp(z + l + between)
            if masked:
                a = jnp.where(causal2, a, 0.0)
            return e_run + jnp.sum(l, axis=1, keepdims=True), acc + _dot(a.astype(BF16), vb)

        init = (jnp.zeros((2 * t, 1), F32), jnp.zeros((2 * t, LANES), F32))
        carry = chunk(i, init, scores(i), True)

        def step(st):
            j, cr, z = st
            z_next = scores(jnp.maximum(j - 1, 0))
            return j - 1, chunk(j, cr, z, False), z_next

        j_stop, (e_tot, acc), _ = lax.while_loop(
            lambda st: (st[0] >= 0) & (jnp.max(st[1][0]) > -SKIP_LOG), step,
            (i - 1, carry, scores(jnp.maximum(i - 1, 0))))
        o_ref[...] = jnp.where(hm[0], acc[:t], acc[t:])
        ltot_ref[...] = jnp.where(hm[0], e_tot[:t], e_tot[t:])
        stop_ref[...] = jnp.full(stop_ref.shape, j_stop.astype(F32), F32)

    blk = lambda cb: pl.BlockSpec((t, LANES), lambda p, i: (i, cb + p))
    full = lambda cb: pl.BlockSpec((s, LANES), lambda p, i: (0, cb + p))
    out_blk = pl.BlockSpec((t, LANES), lambda p, i: (i, p))
    n_pairs = SB_W // LANES
    return _pcall(body, name=name, grid=(n_pairs, s // t),
                  in_specs=[blk(CB_QA), full(CB_KA), full(CB_VA)],
                  out_specs=[out_blk, out_blk,
                             pl.BlockSpec((1, 1, 8, LANES), lambda p, i: (p, i, 0, 0))],
                  out_shape=[jax.ShapeDtypeStruct((s, SB_W), F32)] * 2
                  + [jax.ShapeDtypeStruct((n_pairs, s // t, 8, LANES), F32)],
                  semantics=("parallel", "arbitrary"))(proj, proj, proj)


def _sb_bwd(proj, dmixed, ltot, stop, *, name):
    s = proj.shape[0]
    t = min(ATT_T, s)
    scale = HEAD_DIM ** -0.5

    def body(q_ref, k_ref, v_ref, do_ref, ltot_ref, stop_ref, dq_ref, dk_ref, dv_ref):
        i = pl.program_id(1)

        @pl.when(i == 0)
        def _():
            dk_ref[...] = jnp.zeros_like(dk_ref)
            dv_ref[...] = jnp.zeros_like(dv_ref)

        hm = _lane_masks()
        q = q_ref[...] * scale
        do = do_ref[...]
        qh = [jnp.where(mk, q, 0.0).astype(BF16) for mk in hm]
        doh = [jnp.where(mk, do, 0.0).astype(BF16) for mk in hm]
        r, c = _tri_iotas(t)
        upto = (r <= c).astype(BF16)
        before = (r < c).astype(BF16)
        q2 = jnp.concatenate(qh, axis=0)
        do2 = jnp.concatenate(doh, axis=0)
        causal2 = jnp.concatenate([c < r, c < r], axis=0)

        j_stop = jnp.clip(jnp.max(stop_ref[...]).astype(jnp.int32), -1, i - 1)
        ltv = ltot_ref[...]
        lt = jnp.concatenate([ltv[:, 0:1], ltv[:, HEAD_DIM:HEAD_DIM + 1]], axis=0)

        def products(j):
            return (_dot_nt(q2, k_ref[_rows(j, t), :].astype(BF16)),
                    _dot_nt(do2, v_ref[_rows(j, t), :].astype(BF16)))

        def chunk(j, carry, z, da, masked):
            l_run, g_run, dq = carry
            l, e = _neg_softplus(z)
            beta = jnp.where(z >= 0.0, 1.0, e) / (1.0 + e)
            if masked:
                l = jnp.where(causal2, l, 0.0)
            prefix = _ones_dot(l, upto) + l_run
            a = jnp.exp(z + l + (lt - prefix))
            if masked:
                a = jnp.where(causal2, a, 0.0)
            g = a * da
            g_before = _ones_dot(g, before) + g_run
            dz = g * (1.0 - beta) - beta * g_before
            if masked:
                dz = jnp.where(causal2, dz, 0.0)
            dzb = dz.astype(BF16)
            dk_ref[_rows(j, t), :] += _dot_tn(dzb, q2)
            dv_ref[_rows(j, t), :] += _dot_tn(a.astype(BF16), do2)
            return (l_run + jnp.sum(l, axis=1, keepdims=True),
                    g_run + jnp.sum(g, axis=1, keepdims=True),
                    dq + _dot(dzb, k_ref[_rows(j, t), :].astype(BF16)))

        init = (jnp.zeros((2 * t, 1), F32), jnp.zeros((2 * t, 1), F32),
                jnp.zeros((2 * t, LANES), F32))
        carry = lax.fori_loop(j_stop + 1, i,
                              lambda j, cr: chunk(j, cr, *products(j), False), init)
        dq2 = chunk(i, carry, *products(i), True)[2]
        dq_ref[...] = jnp.where(hm[0], dq2[:t], dq2[t:]) * scale

    blk = lambda cb: pl.BlockSpec((t, LANES), lambda p, i: (i, cb + p))
    full = lambda cb: pl.BlockSpec((s, LANES), lambda p, i: (0, cb + p))
    out_blk = pl.BlockSpec((t, LANES), lambda p, i: (i, p))
    out_full = pl.BlockSpec((s, LANES), lambda p, i: (0, p))
    return _pcall(body, name=name, grid=(SB_W // LANES, s // t),
                  in_specs=[blk(CB_QA), full(CB_KA), full(CB_VA), blk(0), out_blk,
                            pl.BlockSpec((1, 1, 8, LANES), lambda p, i: (p, i, 0, 0))],
                  out_specs=[out_blk, out_full, out_full],
                  out_shape=[jax.ShapeDtypeStruct((s, SB_W), F32)] * 3,
                  semantics=("parallel", "arbitrary"))(proj, proj, proj, dmixed, ltot, stop)


def _group_mean(v, lo):
    s0 = jnp.sum(jnp.where(lo, v, 0.0), axis=1, keepdims=True)
    s1 = jnp.sum(jnp.where(lo, 0.0, v), axis=1, keepdims=True)
    return jnp.where(lo, s0, s1) * (1.0 / HEAD_DIM)


def _fox_prep_fwd(proj, qg, kg, *, name):
    s = proj.shape[0]
    tr = min(PREP_ROWS, s)

    def body(q_ref, k_ref, qg_ref, kg_ref, qn_ref, kn_ref, kmax_ref):
        lo = _lane_masks()[0]
        for x_ref, g_ref, o_ref in ((q_ref, qg_ref, qn_ref), (k_ref, kg_ref, kn_ref)):
            x = x_ref[...]
            o_ref[...] = x * lax.rsqrt(_group_mean(x * x, lo) + EPS) * g_ref[...]

        @pl.when(pl.program_id(1) == 0)
        def _():
            kmax_ref[...] = jnp.zeros_like(kmax_ref)
        kn = kn_ref[...]
        norms = jnp.sqrt(_group_mean(kn * kn, lo) * HEAD_DIM)
        kmax_ref[...] = jnp.maximum(kmax_ref[...], jnp.max(norms, axis=0, keepdims=True))

    blk = lambda cb: pl.BlockSpec((tr, LANES), lambda p, i: (i, cb + p))
    vec = pl.BlockSpec((1, LANES), lambda p, i: (0, 0))
    out_blk = pl.BlockSpec((tr, LANES), lambda p, i: (i, p))
    return _pcall(body, name=name, grid=(FOX_W // LANES, s // tr),
                  in_specs=[blk(CB_QB), blk(CB_KB), vec, vec],
                  out_specs=[out_blk, out_blk, pl.BlockSpec((1, LANES), lambda p, i: (0, p))],
                  out_shape=[jax.ShapeDtypeStruct((s, FOX_W), F32)] * 2
                  + [jax.ShapeDtypeStruct((1, FOX_W), F32)],
                  semantics=("parallel", "arbitrary"))(proj, proj, qg, kg)


def _fox_prep_bwd(proj, dqn, dkn, qg, kg, *, name):
    s = proj.shape[0]
    tr = min(PREP_ROWS, s)

    def body(q_ref, k_ref, dqn_ref, dkn_ref, qg_ref, kg_ref, dq_ref, dk_ref, dqg_ref, dkg_ref):
        @pl.when((pl.program_id(0) == 0) & (pl.program_id(1) == 0))
        def _():
            dqg_ref[...] = jnp.zeros_like(dqg_ref)
            dkg_ref[...] = jnp.zeros_like(dkg_ref)

        lo = _lane_masks()[0]
        for x_ref, dy_ref, g_ref, dx_ref, dg_ref in ((q_ref, dqn_ref, qg_ref, dq_ref, dqg_ref),
                                                     (k_ref, dkn_ref, kg_ref, dk_ref, dkg_ref)):
            x, dy = x_ref[...], dy_ref[...]
            r = lax.rsqrt(_group_mean(x * x, lo) + EPS)
            xh = x * r
            dxh = dy * g_ref[...]
            dx_ref[...] = r * (dxh - xh * _group_mean(dxh * xh, lo))
            dg_ref[...] += _colsum(dy * xh)

    blk = lambda cb: pl.BlockSpec((tr, LANES), lambda p, i: (i, cb + p))
    vec = pl.BlockSpec((1, LANES), lambda p, i: (0, 0))
    out_blk = pl.BlockSpec((tr, LANES), lambda p, i: (i, p))
    return _pcall(body, name=name, grid=(FOX_W // LANES, s // tr),
                  in_specs=[blk(CB_QB), blk(CB_KB), out_blk, out_blk, vec, vec],
                  out_specs=[out_blk, out_blk, vec, vec],
                  out_shape=[jax.ShapeDtypeStruct((s, FOX_W), F32)] * 2
                  + [jax.ShapeDtypeStruct((1, LANES), F32)] * 2,
                  semantics=("arbitrary", "arbitrary"))(proj, proj, dqn, dkn, qg, kg)


def _split3_dot(tri_bf16, x):
    hi = x.astype(BF16)
    r1 = x - hi.astype(F32)
    mid = r1.astype(BF16)
    lo = (r1 - mid.astype(F32)).astype(BF16)
    return _dot(tri_bf16, hi) + _dot(tri_bf16, mid) + _dot(tri_bf16, lo)


def _forget_cumsum_fwd(proj, b_pad, *, name):
    s = proj.shape[0]
    tb = min(256, s)

    def body(fl_ref, b_ref, cf_ref, run_ref):
        @pl.when(pl.program_id(0) == 0)
        def _():
            run_ref[...] = jnp.zeros_like(run_ref)
        lf, _ = _neg_softplus(-(fl_ref[...] + b_ref[...]))
        r, c = _tri_iotas(tb)
        incl = _split3_dot((c <= r).astype(BF16), lf) + run_ref[...]
        cf_ref[...] = incl
        run_ref[...] = incl[tb - 1:tb, :]

    return _pcall(body, name=name, grid=(s // tb,),
                  in_specs=[pl.BlockSpec((tb, LANES), lambda i: (i, CB_FL)),
                            pl.BlockSpec((1, LANES), lambda i: (0, 0))],
                  out_specs=pl.BlockSpec((tb, LANES), lambda i: (i, 0)),
                  out_shape=jax.ShapeDtypeStruct((s, LANES), F32),
                  scratch_shapes=[pltpu.VMEM((1, LANES), F32)],
                  semantics=("arbitrary",))(proj, b_pad)


def _forget_cumsum_bwd(proj, b_pad, dcf, *, name):
    s = proj.shape[0]
    tb = min(256, s)
    nb = s // tb

    def body(fl_ref, b_ref, dcf_ref, dfl_ref, db_ref, run_ref):
        @pl.when(pl.program_id(0) == 0)
        def _():
            run_ref[...] = jnp.zeros_like(run_ref)
            db_ref[...] = jnp.zeros_like(db_ref)
        r, c = _tri_iotas(tb)
        dlf = _split3_dot((c >= r).astype(BF16), dcf_ref[...]) + run_ref[...]
        run_ref[...] = dlf[0:1, :]
        xv = fl_ref[...] + b_ref[...]
        e = jnp.exp(-jnp.abs(xv))
        sig_neg = jnp.where(xv >= 0.0, e, 1.0) / (1.0 + e)
        dfl = dlf * sig_neg
        dfl_ref[...] = dfl
        db_ref[...] += _colsum(dfl)

    return _pcall(body, name=name, grid=(nb,),
                  in_specs=[pl.BlockSpec((tb, LANES), lambda i: (nb - 1 - i, CB_FL)),
                            pl.BlockSpec((1, LANES), lambda i: (0, 0)),
                            pl.BlockSpec((tb, LANES), lambda i: (nb - 1 - i, 0))],
                  out_specs=[pl.BlockSpec((tb, LANES), lambda i: (nb - 1 - i, 0)),
                             pl.BlockSpec((1, LANES), lambda i: (0, 0))],
                  out_shape=[jax.ShapeDtypeStruct((s, LANES), F32),
                             jax.ShapeDtypeStruct((1, LANES), F32)],
                  scratch_shapes=[pltpu.VMEM((1, LANES), F32)],
                  semantics=("arbitrary",))(proj, b_pad, dcf)


def _fox_bias_q(cfc, p, h):
    lane = lax.broadcasted_iota(jnp.int32, (1, LANES), 1)
    return jnp.sum(jnp.where(lane == 2 * p + h, cfc, 0.0), axis=1, keepdims=True)


def _fox_score_bound(q, kmax_row, hm):
    out = []
    for h in range(2):
        qnorm = jnp.sqrt(jnp.sum(jnp.where(hm[h], q * q, 0.0), axis=1, keepdims=True))
        out.append(1.02 * qnorm * kmax_row[:, h * HEAD_DIM:h * HEAD_DIM + 1])
    return out


def _fox_live(cfr_ref, j, t, tops):
    jc = jnp.maximum(j, 0)
    worst = []
    for h in range(2):
        cf_min = jnp.min(cfr_ref[0, pl.ds(h, 1), _rows(jc, t)], axis=1, keepdims=True)
        worst.append(jnp.max(tops[h] - cf_min))
    return (j >= 0) & (jnp.maximum(worst[0], worst[1]) > -SKIP_LOG)


def _chip_gather_copies(x_refs, out_refs, send_sems, recv_sems, local_sems):
    ids = (lax.axis_index("x"), lax.axis_index("y"), lax.axis_index("c"))
    chip = 2 * ids[0] + ids[1]
    copies = []
    for n, (x_ref, out_ref) in enumerate(zip(x_refs, out_refs)):
        copies.append(pltpu.make_async_copy(x_ref, out_ref.at[chip], local_sems.at[n]))
        for kk, (flip_x, flip_y) in enumerate(((1, 0), (0, 1), (1, 1))):
            peer = (1 - ids[0] if flip_x else ids[0], 1 - ids[1] if flip_y else ids[1], ids[2])
            copies.append(pltpu.make_async_remote_copy(
                src_ref=x_ref, dst_ref=out_ref.at[chip],
                send_sem=send_sems.at[3 * n + kk], recv_sem=recv_sems.at[3 * n + kk],
                device_id=peer, device_id_type=MESH))
    return copies


def _fox_fwd(proj, qn, kn, cf, cf_rows, kmax, *, name, gathers=()):
    s = proj.shape[0]
    t = min(ATT_T, s)
    scale = HEAD_DIM ** -0.5
    n_pairs, nq, ng = FOX_W // LANES, s // t, len(gathers)

    def body(*refs):
        q_ref, k_ref, v_ref, cfc_ref, cfr_ref, kmax_ref = refs[:6]
        o_ref, lse_ref = refs[6 + ng:8 + ng]
        p, i = pl.program_id(0), pl.program_id(1)
        if ng:
            def copies():
                return _chip_gather_copies(refs[6:6 + ng], refs[8 + ng:8 + 2 * ng], *refs[8 + 2 * ng:])

            @pl.when((p == 0) & (i == 0))
            def _():
                for cp in copies():
                    cp.start()
        hm = _lane_masks()
        q = q_ref[...] * scale
        qh = [jnp.where(mk, q, 0.0).astype(BF16) for mk in hm]
        cfc = cfc_ref[...]
        bq = [_fox_bias_q(cfc, p, h) for h in range(2)]
        qk_top = _fox_score_bound(q, kmax_ref[...], hm)
        r, c = _tri_iotas(t)
        causal = c <= r

        q2 = jnp.concatenate(qh, axis=0)
        causal2 = jnp.concatenate([causal, causal], axis=0)

        def scores(j):
            return _dot_nt(q2, k_ref[_rows(j, t), :].astype(BF16))

        def chunk(j, carry, z2, masked):
            m_run, l_run, acc = carry
            vb = v_ref[_rows(j, t), :].astype(BF16)
            z = jnp.concatenate(
                [z2[h * t:(h + 1) * t] + (bq[h] - cfr_ref[0, pl.ds(h, 1), _rows(j, t)])
                 for h in range(2)], axis=0)
            if masked:
                z = jnp.where(causal2, z, -1e30)
            m_new = jnp.maximum(m_run, jnp.max(z, axis=1, keepdims=True))
            alpha = jnp.exp(m_run - m_new)
            pr = jnp.exp(z - m_new)
            return (m_new, alpha * l_run + jnp.sum(pr, axis=1, keepdims=True),
                    alpha * acc + _dot(pr.astype(BF16), vb))

        init = (jnp.full((2 * t, 1), -1e30, F32), jnp.zeros((2 * t, 1), F32),
                jnp.zeros((2 * t, LANES), F32))
        carry = chunk(i, init, scores(i), True)

        def live(j, cr):
            return _fox_live(cfr_ref, j, t,
                             [qk_top[h] + bq[h] - cr[0][h * t:(h + 1) * t] for h in range(2)])

        def step(st):
            j, _, cr, z2 = st
            z2_next = scores(jnp.maximum(j - 1, 0))
            cr = chunk(j, cr, z2, False)
            return j - 1, live(j - 1, cr), cr, z2_next

        m_fin, l_fin, acc = lax.while_loop(
            lambda st: st[1], step,
            (i - 1, live(i - 1, carry), carry, scores(jnp.maximum(i - 1, 0))))[2]
        o2 = acc / l_fin
        lse2 = m_fin + jnp.log(l_fin)
        o_ref[...] = jnp.where(hm[0], o2[:t], o2[t:])
        lse_ref[...] = jnp.where(hm[0], lse2[:t], lse2[t:])
        if ng:
            @pl.when((p == n_pairs - 1) & (i == nq - 1))
            def _():
                for cp in copies():
                    cp.wait()

    blk = pl.BlockSpec((t, LANES), lambda p, i: (i, p))
    full = pl.BlockSpec((s, LANES), lambda p, i: (0, p))
    any_spec = pl.BlockSpec(memory_space=pl.ANY)
    dma = pltpu.SemaphoreType.DMA
    return _pcall(body, name=name, grid=(n_pairs, nq),
                  in_specs=[blk, full, pl.BlockSpec((s, LANES), lambda p, i: (0, CB_VB + p)),
                            pl.BlockSpec((t, LANES), lambda p, i: (i, 0)),
                            pl.BlockSpec((1, 2, s), lambda p, i: (p, 0, 0)),
                            pl.BlockSpec((1, LANES), lambda p, i: (0, p))] + [any_spec] * ng,
                  out_specs=[blk, blk] + [any_spec] * ng,
                  out_shape=[jax.ShapeDtypeStruct((s, FOX_W), F32)] * 2
                  + [jax.ShapeDtypeStruct((4,) + g.shape, g.dtype) for g in gathers],
                  scratch_shapes=[dma((3 * ng,)), dma((3 * ng,)), dma((ng,))] if ng else [],
                  semantics=("arbitrary", "arbitrary") if ng else ("parallel", "arbitrary"))(
                      qn, kn, proj, cf, cf_rows, kmax, *gathers)


def _fox_bwd(proj, qn, kn, cf, cf_rows, kmax, do, o, lse, *, name):
    s = proj.shape[0]
    t = min(ATT_T, s)
    scale = HEAD_DIM ** -0.5

    def body(q_ref, k_ref, v_ref, cfc_ref, cfr_ref, kmax_ref, do_ref, o_ref, lse_ref,
             dq_ref, dk_ref, dv_ref, dcf_ref, dcfq_ref):
        p, i = pl.program_id(0), pl.program_id(1)

        @pl.when(i == 0)
        def _():
            dk_ref[...] = jnp.zeros_like(dk_ref)
            dv_ref[...] = jnp.zeros_like(dv_ref)
            dcf_ref[...] = jnp.zeros_like(dcf_ref)

        hm = _lane_masks()
        q = q_ref[...] * scale
        do = do_ref[...]
        dov = do * o_ref[...]
        qh = [jnp.where(mk, q, 0.0).astype(BF16) for mk in hm]
        doh = [jnp.where(mk, do, 0.0).astype(BF16) for mk in hm]
        delta = [jnp.sum(jnp.where(mk, dov, 0.0), axis=1, keepdims=True) for mk in hm]
        lsev = lse_ref[...]
        lse = [lsev[:, 0:1], lsev[:, HEAD_DIM:HEAD_DIM + 1]]
        cfc = cfc_ref[...]
        bq = [_fox_bias_q(cfc, p, h) - lse[h] for h in range(2)]
        qk_top = _fox_score_bound(q, kmax_ref[...], hm)
        tops = [qk_top[h] + bq[h] for h in range(2)]
        r, c = _tri_iotas(t)
        j_stop = lax.while_loop(lambda st: st[1],
                                lambda st: (st[0] - 1, _fox_live(cfr_ref, st[0] - 1, t, tops)),
                                (i - 1, _fox_live(cfr_ref, i - 1, t, tops)))[0]
        q2 = jnp.concatenate(qh, axis=0)
        do2 = jnp.concatenate(doh, axis=0)
        delta2 = jnp.concatenate(delta, axis=0)
        causal2 = jnp.concatenate([c <= r, c <= r], axis=0)

        def products(j):
            return (_dot_nt(q2, k_ref[_rows(j, t), :].astype(BF16)),
                    _dot_nt(do2, v_ref[_rows(j, t), :].astype(BF16)))

        def chunk(j, carry, z2, dp, masked):
            dq, row_sum = carry
            z = jnp.concatenate(
                [z2[h * t:(h + 1) * t] + (bq[h] - cfr_ref[0, pl.ds(h, 1), _rows(j, t)])
                 for h in range(2)], axis=0)
            pr = jnp.exp(z)
            if masked:
                pr = jnp.where(causal2, pr, 0.0)
            ds = pr * (dp - delta2)
            dsb = ds.astype(BF16)
            dk_ref[_rows(j, t), :] += _dot_tn(dsb, q2)
            dv_ref[_rows(j, t), :] += _dot_tn(pr.astype(BF16), do2)
            for h in range(2):
                dcf_ref[0, pl.ds(h, 1), _rows(j, t)] -= jnp.sum(ds[h * t:(h + 1) * t], axis=0,
                                                               keepdims=True)
            return (dq + _dot(dsb, k_ref[_rows(j, t), :].astype(BF16)),
                    row_sum + jnp.sum(ds, axis=1, keepdims=True))

        def one(j, cr):
            return chunk(j, cr, *products(j), False)

        init = (jnp.zeros((2 * t, LANES), F32), jnp.zeros((2 * t, 1), F32))
        first, odd = j_stop + 1, (i - j_stop - 1) % 2
        carry = lax.cond(odd == 1, lambda cr: one(first, cr), lambda cr: cr, init)
        carry = lax.fori_loop(0, (i - first) // 2,
                              lambda n, cr: one(first + odd + 2 * n + 1,
                                                one(first + odd + 2 * n, cr)), carry)
        dq2, row_sum = chunk(i, carry, *products(i), True)
        dq_ref[...] = jnp.where(hm[0], dq2[:t], dq2[t:]) * scale
        dcfq_ref[...] = jnp.where(hm[0], row_sum[:t], row_sum[t:])

    blk = pl.BlockSpec((t, LANES), lambda p, i: (i, p))
    full = pl.BlockSpec((s, LANES), lambda p, i: (0, p))
    rows = pl.BlockSpec((1, 2, s), lambda p, i: (p, 0, 0))
    return _pcall(body, name=name, grid=(FOX_W // LANES, s // t),
                  in_specs=[blk, full, pl.BlockSpec((s, LANES), lambda p, i: (0, CB_VB + p)),
                            pl.BlockSpec((t, LANES), lambda p, i: (i, 0)), rows,
                            pl.BlockSpec((1, LANES), lambda p, i: (0, p)),
                            pl.BlockSpec((t, LANES), lambda p, i: (i, SB_W // LANES + p)),
                            blk, blk],
                  out_specs=[blk, full, full, rows, blk],
                  out_shape=[jax.ShapeDtypeStruct((s, FOX_W), F32)] * 3
                  + [jax.ShapeDtypeStruct((FOX_W // LANES, 2, s), F32),
                     jax.ShapeDtypeStruct((s, FOX_W), F32)],
                  semantics=("parallel", "arbitrary"))(qn, kn, proj, cf, cf_rows, kmax, do, o, lse)


_GELU_C0 = math.sqrt(2.0 / math.pi)
_GELU_C1 = 0.044715


def _gelu(x):
    th = jnp.tanh(_GELU_C0 * (x + _GELU_C1 * (x * x * x)))
    return 0.5 * x * (1.0 + th), th


def _gelu_grad(x, th):
    return 0.5 * (1.0 + th) + 0.5 * x * (1.0 - th * th) * (_GELU_C0 * (1.0 + 3.0 * _GELU_C1 * x * x))


def _sgu_mix(wm, vn_c, lo, bcol):
    return jnp.where(lo, _dot(wm[0], vn_c) + bcol[0], _dot(wm[1], vn_c) + bcol[1])


def _sgu_fwd(proj, w, b_cols, gn, *, name):
    s = proj.shape[0]
    tr = min(512, s)
    ch = SGU_CHUNK

    def body(u_ref, v_ref, w_ref, b_ref, gn_ref, o_ref):
        lo = _lane_masks()[0]
        r, c = _tri_iotas(ch)
        wm = [jnp.where(c <= r, w_ref[h], 0.0).astype(BF16) for h in range(2)]
        bcol = [b_ref[0, :, h:h + 1] for h in range(2)]
        for n in range(tr // ch):
            rows = slice(n * ch, (n + 1) * ch)
            u, _ = _gelu(u_ref[rows, :])
            vg, _ = _gelu(v_ref[rows, :])
            vn = vg * lax.rsqrt(_group_mean(vg * vg, lo) + EPS) * gn_ref[0]
            o_ref[rows, :] = u * _sgu_mix(wm, vn.astype(BF16), lo, bcol)

    blk = lambda cb: pl.BlockSpec((tr, LANES), lambda p, i: (i, cb + p))
    return _pcall(body, name=name, grid=(SGU_W // LANES, s // tr),
                  in_specs=[blk(CB_UC), blk(CB_VC),
                            pl.BlockSpec((2, ch, ch), lambda p, i: (p, 0, 0)),
                            pl.BlockSpec((1, ch, 2), lambda p, i: (p, 0, 0)),
                            pl.BlockSpec((1, 1, LANES), lambda p, i: (p, 0, 0))],
                  out_specs=pl.BlockSpec((tr, LANES), lambda p, i: (i, p)),
                  out_shape=jax.ShapeDtypeStruct((s, SGU_W), F32),
                  semantics=("parallel", "parallel"))(proj, proj, w, b_cols, gn)


def _sgu_bwd(proj, dmixed, w, w_t, b_cols, gn, *, name):
    s = proj.shape[0]
    tr = min(512, s)
    ch = SGU_CHUNK
    cb_do = (SB_W + FOX_W) // LANES

    def body(u_ref, v_ref, do_ref, w_ref, wt_ref, b_ref, gn_ref,
             du_ref, dv_ref, dw_ref, db_ref, dgn_ref):
        @pl.when(pl.program_id(1) == 0)
        def _():
            dw_ref[...] = jnp.zeros_like(dw_ref)
            db_ref[...] = jnp.zeros_like(db_ref)
            dgn_ref[...] = jnp.zeros_like(dgn_ref)

        hm = _lane_masks()
        lo = hm[0]
        r, c = _tri_iotas(ch)
        wm = [jnp.where(c <= r, w_ref[h], 0.0).astype(BF16) for h in range(2)]
        wtm = [jnp.where(r <= c, wt_ref[h], 0.0).astype(BF16) for h in range(2)]
        bcol = [b_ref[0, :, h:h + 1] for h in range(2)]
        gnv = gn_ref[0]
        for n in range(tr // ch):
            rows = slice(n * ch, (n + 1) * ch)
            uc, vc, do = u_ref[rows, :], v_ref[rows, :], do_ref[rows, :]
            u, thu = _gelu(uc)
            vg, thv = _gelu(vc)
            rinv = lax.rsqrt(_group_mean(vg * vg, lo) + EPS)
            xh = vg * rinv
            vnb = (xh * gnv).astype(BF16)
            mix = _sgu_mix(wm, vnb, lo, bcol)
            du_ref[rows, :] = do * mix * _gelu_grad(uc, thu)
            dm = do * u
            dmb = dm.astype(BF16)
            dvn = jnp.where(lo, _dot(wtm[0], dmb), _dot(wtm[1], dmb))
            for h in range(2):
                dmh = jnp.where(hm[h], dm, 0.0)
                dw_ref[h] += jnp.where(c <= r, _dot_nt(dmh.astype(BF16), vnb), 0.0)
                db_ref[0, :, h:h + 1] += jnp.sum(dmh, axis=1, keepdims=True)
            dgn_ref[0] += _colsum(dvn * xh)
            dxh = dvn * gnv
            dvg = rinv * (dxh - xh * _group_mean(dxh * xh, lo))
            dv_ref[rows, :] = dvg * _gelu_grad(vc, thv)

    blk = lambda cb: pl.BlockSpec((tr, LANES), lambda p, i: (i, cb + p))
    w_spec = pl.BlockSpec((2, ch, ch), lambda p, i: (p, 0, 0))
    b_spec = pl.BlockSpec((1, ch, 2), lambda p, i: (p, 0, 0))
    g_spec = pl.BlockSpec((1, 1, LANES), lambda p, i: (p, 0, 0))
    out_blk = pl.BlockSpec((tr, LANES), lambda p, i: (i, p))
    return _pcall(body, name=name, grid=(SGU_W // LANES, s // tr),
                  in_specs=[blk(CB_UC), blk(CB_VC), blk(cb_do), w_spec, w_spec, b_spec, g_spec],
                  out_specs=[out_blk, out_blk, w_spec, b_spec, g_spec],
                  out_shape=[jax.ShapeDtypeStruct((s, SGU_W), F32)] * 2
                  + [jax.ShapeDtypeStruct(w.shape, F32), jax.ShapeDtypeStruct(b_cols.shape, F32),
                     jax.ShapeDtypeStruct(gn.shape, F32)],
                  semantics=("parallel", "arbitrary"))(proj, proj, dmixed, w, w_t, b_cols, gn)


def _pad_lanes(v):
    return jnp.zeros((1, LANES), F32).at[0, :v.shape[0]].set(v)


def _small_views(sm):
    return dict(
        n1=sm["norm1_g"][None, :], n2=sm["norm2_g"][None, :],
        b_pad=_pad_lanes(sm["b_forget"]),
        qg=jnp.tile(sm["q_norm_g"], 2)[None, :], kg=jnp.tile(sm["k_norm_g"], 2)[None, :],
        gn=sm["sgu_norm_g"].reshape(2, 1, LANES),
        w=sm["sgu_w"], w_t=jnp.swapaxes(sm["sgu_w"], 1, 2),
        b_cols=sm["sgu_b"].reshape(2, 2, SGU_CHUNK).transpose(0, 2, 1))


def _cf_rows(cf):
    return cf[:, :FOX_HEADS].T.reshape(FOX_W // LANES, 2, cf.shape[0])


def _layer_fwd(x_in, prev, mod, wts, sm, l, gathers=(), late_weights=None):
    sh1, sc1, g1, sh2, sc2, g2 = mod
    v = _small_views(sm)
    if prev is None:
        x0 = x_in
        h1 = _norm_mod_fwd(x0, v["n1"], sc1, sh1, name=f"l{l}_norm1")
    else:
        x0, h1 = _resid_norm_mod_fwd(x_in, prev[0], prev[1], v["n1"], sc1, sh1, name=f"l{l}_norm1")
    proj = _matmul(h1, wts["w_in"], name=f"l{l}_proj")
    o_sb, sb_ltot, sb_stop = _sb_fwd(proj, name=f"l{l}_sb_fwd")
    qn, kn, kmax = _fox_prep_fwd(proj, v["qg"], v["kg"], name=f"l{l}_fox_prep")
    cf = _forget_cumsum_fwd(proj, v["b_pad"], name=f"l{l}_cumf")
    cfr = _cf_rows(cf)
    o_fox, lse, *gathered = _fox_fwd(proj, qn, kn, cf, cfr, kmax, name=f"l{l}_fox_fwd",
                                     gathers=gathers)
    if gathers:
        late_weights(gathered)
    o_sgu = _sgu_fwd(proj, v["w"], v["b_cols"], v["gn"], name=f"l{l}_sgu_fwd")
    mixed = jnp.concatenate([o_sb, o_fox, o_sgu], axis=1).astype(BF16)
    mo = _matmul(mixed, wts["w_out"], name=f"l{l}_wout")
    x1, h2 = _resid_norm_mod_fwd(x0, mo, g1, v["n2"], sc2, sh2, name=f"l{l}_norm2")
    a, rr = _matmul(h2, wts["w1"], name=f"l{l}_mlp1", out_dtype=BF16, relu2=BF16)
    m2 = _matmul(rr, wts["w2"], name=f"l{l}_mlp2")
    saved = dict(x0=x0, h1=h1, proj=proj, sb_ltot=sb_ltot, sb_stop=sb_stop, qn=qn, kn=kn, kmax=kmax, cf=cf, cfr=cfr, o_fox=o_fox,
                 lse=lse, mixed=mixed, mo=mo, x1=x1, h2=h2, a=a, rr=rr, m2=m2)
    return saved


def _layer_bwd(dx2, dm2, dg2, sv, mod, wts, sm, l, below):
    sh1, sc1, g1, sh2, sc2, g2 = mod
    v = _small_views(sm)
    dw2 = _matmul(sv["rr"], dm2, ta=True, name=f"l{l}_dw2")
    da = _matmul(dm2, wts["w2"], tb=True, name=f"l{l}_da", out_dtype=BF16, pre_act=sv["a"])
    dw1 = _matmul(sv["h2"], da, ta=True, name=f"l{l}_dw1")
    dh2 = _matmul(da, wts["w1"], tb=True, name=f"l{l}_dh2")
    dx1, dn2, dsc2, dsh2, dmo, dg1 = _norm_mod_bwd(sv["x1"], dh2, dx2, v["n2"], sc2,
                                                    (sv["mo"], g1), name=f"l{l}_norm2_bwd")
    dwo = _matmul(sv["mixed"], dmo, ta=True, name=f"l{l}_dwout")
    dmixed = _matmul(dmo, wts["w_out"], tb=True, name=f"l{l}_dmixed")
    proj = sv["proj"]
    dqa, dka, dva = _sb_bwd(proj, dmixed, sv["sb_ltot"], sv["sb_stop"], name=f"l{l}_sb_bwd")
    dqn, dkn, dvb, dcfr, dcfq = _fox_bwd(proj, sv["qn"], sv["kn"], sv["cf"], sv["cfr"], sv["kmax"], dmixed,
                                   sv["o_fox"], sv["lse"], name=f"l{l}_fox_bwd")
    dqb, dkb, dqg, dkg = _fox_prep_bwd(proj, dqn, dkn, v["qg"], v["kg"], name=f"l{l}_fox_prep_bwd")
    s = proj.shape[0]
    dcf_heads = dcfr.reshape(FOX_HEADS, s).T + dcfq.reshape(s, FOX_HEADS, HEAD_DIM)[:, :, 0]
    dcf = jnp.zeros((s, LANES), F32).at[:, :FOX_HEADS].set(dcf_heads)
    dfl, dbf = _forget_cumsum_bwd(proj, v["b_pad"], dcf, name=f"l{l}_cumf_bwd")
    duc, dvc, dsw, dsb_cols, dgn = _sgu_bwd(proj, dmixed, v["w"], v["w_t"], v["b_cols"], v["gn"],
                                            name=f"l{l}_sgu_bwd")
    dproj = jnp.concatenate([dqa, dka, dva, dqb, dkb, dvb, duc, dvc, dfl,
                             jnp.zeros((s, LANES), F32)], axis=1).astype(BF16)
    dwin = _matmul(sv["h1"], dproj, ta=True, name=f"l{l}_dwin")
    dh1 = _matmul(dproj, wts["w_in"], tb=True, name=f"l{l}_dh1")
    dx0, dn1, dsc1, dsh1, dm_below, dg_below = _norm_mod_bwd(sv["x0"], dh1, dx1, v["n1"], sc1, below,
                                                             name=f"l{l}_norm1_bwd")
    big = dict(w_in=dwin, w_out=dwo, w1=dw1, w2=dw2)
    small = dict(norm1_g=dn1[0], norm2_g=dn2[0], b_forget=dbf[0, :FOX_HEADS],
                 q_norm_g=dqg[0, :HEAD_DIM] + dqg[0, HEAD_DIM:],
                 k_norm_g=dkg[0, :HEAD_DIM] + dkg[0, HEAD_DIM:],
                 sgu_norm_g=dgn.reshape(4, HEAD_DIM), sgu_w=dsw,
                 sgu_b=dsb_cols.transpose(0, 2, 1).reshape(4, SGU_CHUNK))
    dmod = jnp.concatenate([dsh1, dsc1, dg1, dsh2, dsc2, dg2], axis=1)
    return dx0, dm_below, dg_below, big, small, dmod


def _w_in_to_internal(w):
    pad = jnp.zeros((w.shape[0], PROJ_W - IN_W), w.dtype)
    return jnp.concatenate([w[:, :ATT_W], w[:, ATT_W + FOX_HEADS:], w[:, ATT_W:ATT_W + FOX_HEADS],
                            pad], axis=1)


def _w_in_from_internal(g):
    n_gate = SGU_W * 2
    return jnp.concatenate([g[:, :ATT_W], g[:, ATT_W + n_gate:ATT_W + n_gate + FOX_HEADS],
                            g[:, ATT_W:ATT_W + n_gate]], axis=1)


def _exchange(x, masks, slot_shift, slot_bits, scatter, *, name):
    n_slots = 2 ** slot_bits
    blk_shape = x.shape[1:] if scatter else x.shape
    n_peers = len(masks)

    def body(x_ref, out_ref, send_sems, recv_sems, local_sem):
        ids = (lax.axis_index("x"), lax.axis_index("y"), lax.axis_index("c"))
        me = 4 * ids[0] + 2 * ids[1] + ids[2]
        my_slot = (me >> slot_shift) & (n_slots - 1)

        def peer(mask):
            return tuple(1 - v if (mask >> b) & 1 else v for v, b in zip(ids, (2, 1, 0)))

        def src_for(slot):
            return x_ref.at[slot] if scatter else x_ref

        copies = [pltpu.make_async_copy(src_for(my_slot), out_ref.at[my_slot], local_sem)]
        for kk, mask in enumerate(masks):
            peer_slot = ((me ^ mask) >> slot_shift) & (n_slots - 1)
            copies.append(pltpu.make_async_remote_copy(
                src_ref=src_for(peer_slot), dst_ref=out_ref.at[my_slot],
                send_sem=send_sems.at[kk], recv_sem=recv_sems.at[kk],
                device_id=peer(mask), device_id_type=MESH))
        for cp in copies:
            cp.start()
        for cp in copies:
            cp.wait()

    any_spec = pl.BlockSpec(memory_space=pl.ANY)
    return _pcall(body, name=name, in_specs=[any_spec], out_specs=any_spec,
                  out_shape=jax.ShapeDtypeStruct((n_slots,) + tuple(blk_shape), x.dtype),
                  scratch_shapes=[pltpu.SemaphoreType.DMA((n_peers,)),
                                  pltpu.SemaphoreType.DMA((n_peers,)),
                                  pltpu.SemaphoreType.DMA(())])(x)


CORE_PIECE_BYTES = 12 * 2 ** 20
CORE_DMA_CHUNKS = 4


def _core_swap_piece(x, *, name):
    rows, cols = x.shape
    n_ch = CORE_DMA_CHUNKS if rows % (16 * CORE_DMA_CHUNKS) == 0 else 1
    rc = rows // n_ch

    def body(x_ref, out_ref, send_sems, recv_sems):
        sibling = (lax.axis_index("x"), lax.axis_index("y"), 1 - lax.axis_index("c"))
        copies = [pltpu.make_async_remote_copy(
            src_ref=x_ref.at[pl.ds(ch * rc, rc)], dst_ref=out_ref.at[pl.ds(ch * rc, rc)],
            send_sem=send_sems.at[ch], recv_sem=recv_sems.at[ch],
            device_id=sibling, device_id_type=MESH) for ch in range(n_ch)]
        for cp in copies:
            cp.start()
        for cp in copies:
            cp.wait()

    vmem = pl.BlockSpec(memory_space=pltpu.VMEM)
    return _pcall(body, name=name, in_specs=[vmem], out_specs=vmem,
                  out_shape=jax.ShapeDtypeStruct(x.shape, x.dtype),
                  scratch_shapes=[pltpu.SemaphoreType.DMA((n_ch,)),
                                  pltpu.SemaphoreType.DMA((n_ch,))])(x)


def _core_swap(x, *, name):
    rows, cols = x.shape
    n = 1
    while (rows % n or (rows // n) % 16 or
           (rows // n) * (-(-cols // LANES) * LANES) * x.dtype.itemsize > CORE_PIECE_BYTES):
        n += 1
    pr = rows // n
    pieces = [_core_swap_piece(x[kk * pr:(kk + 1) * pr], name=f"{name}_{kk}") for kk in range(n)]
    return pieces[0] if n == 1 else jnp.concatenate(pieces, axis=0)


def _by_core(core, mine, theirs, axis):
    return jnp.where(core == 0, jnp.concatenate([mine, theirs], axis=axis),
                     jnp.concatenate([theirs, mine], axis=axis))


def _gather_chips(x, *, name):
    return _exchange(x, (2, 4, 6), 1, 2, False, name=name)


def _gather_all(x, *, name):
    return _exchange(x, (1, 2, 3, 4, 5, 6, 7), 0, 3, False, name=name)


def _scatter_chips(x4, *, name):
    return _exchange(x4, (2, 4, 6), 1, 2, True, name=name)


def _sum_slots(parts, *, name, out_dtype=F32, tr=256):
    n, rows, cols = parts.shape
    tr = min(tr, rows)
    assert rows % tr == 0, (name, rows, tr)

    def body(p_ref, o_ref):
        acc = p_ref[0].astype(F32)
        for kk in range(1, n):
            acc = acc + p_ref[kk].astype(F32)
        o_ref[...] = acc.astype(o_ref.dtype)

    return _pcall(body, name=name, grid=(rows // tr,),
                  in_specs=[pl.BlockSpec((n, tr, cols), lambda i: (0, i, 0))],
                  out_specs=pl.BlockSpec((tr, cols), lambda i: (i, 0)),
                  out_shape=jax.ShapeDtypeStruct((rows, cols), out_dtype),
                  semantics=("parallel",))(parts)


def _add2(a, b, *, name, out_dtype, tr=512):
    def fn(f, v):
        return [f[0] + f[1]], []
    (out,), _ = _rowwise(fn, [a, b], [], [out_dtype], 0, name=name, tr=tr)
    return out


def _adamw(w, m, v, parts, *, name, tr=256):
    n, rows, cols = parts.shape
    tr = min(tr, rows)
    assert rows % tr == 0, (name, rows, tr)
    c1 = 1.0 - ADAM_B1 ** ADAM_STEP
    c2 = 1.0 - ADAM_B2 ** ADAM_STEP

    def body(w_ref, m_ref, v_ref, p_ref, g_ref, d_ref, nm_ref, nv_ref):
        g = p_ref[0]
        for kk in range(1, n):
            g = g + p_ref[kk]
        nm = ADAM_B1 * m_ref[...] + (1.0 - ADAM_B1) * g
        nv = ADAM_B2 * v_ref[...] + (1.0 - ADAM_B2) * (g * g)
        g_ref[...] = g
        nm_ref[...] = nm
        nv_ref[...] = nv
        d_ref[...] = -ADAM_LR * ((nm / c1) / (jnp.sqrt(nv / c2) + ADAM_EPS) + ADAM_WD * w_ref[...])

    spec = pl.BlockSpec((tr, cols), lambda i: (i, 0))
    return _pcall(body, name=name, grid=(rows // tr,),
                  in_specs=[spec, spec, spec, pl.BlockSpec((n, tr, cols), lambda i: (0, i, 0))],
                  out_specs=[spec] * 4,
                  out_shape=[jax.ShapeDtypeStruct((rows, cols), F32)] * 4,
                  semantics=("parallel",))(w, m, v, parts)


def _silu(c):
    return c / (1.0 + jnp.exp(-c))


def _ada_fwd(c_all, ada_w, ada_b_sh, *, name):
    nl, d, wsh = ada_w.shape

    def body(c_ref, w_ref, b_ref, o_ref):
        cond = _silu(c_ref[...]).astype(BF16)
        o_ref[0] = _dot(cond, w_ref[0].astype(BF16)) + b_ref[0]

    return _pcall(body, name=name, grid=(nl,),
                  in_specs=[pl.BlockSpec(c_all.shape, lambda l: (0, 0)),
                            pl.BlockSpec((1, d, wsh), lambda l: (l, 0, 0)),
                            pl.BlockSpec((1, 1, wsh), lambda l: (l, 0, 0))],
                  out_specs=pl.BlockSpec((1, c_all.shape[0], wsh), lambda l: (l, 0, 0)),
                  out_shape=jax.ShapeDtypeStruct((nl, c_all.shape[0], wsh), F32),
                  semantics=("parallel",))(c_all, ada_w, ada_b_sh)


def _ada_bwd(c_all, dmod_sh, *, name):
    nl, nb, wsh = dmod_sh.shape
    d = c_all.shape[1]

    def body(c_ref, dm_ref, o_ref):
        cond = _silu(c_ref[...]).astype(BF16)
        o_ref[0] = _dot_tn(cond, dm_ref[0].astype(BF16))

    return _pcall(body, name=name, grid=(nl,),
                  in_specs=[pl.BlockSpec(c_all.shape, lambda l: (0, 0)),
                            pl.BlockSpec((1, nb, wsh), lambda l: (l, 0, 0))],
                  out_specs=pl.BlockSpec((1, d, wsh), lambda l: (l, 0, 0)),
                  out_shape=jax.ShapeDtypeStruct((nl, d, wsh), F32),
                  semantics=("parallel",))(c_all, dmod_sh)


SMALL_NAMES = ("norm1_g", "norm2_g", "b_forget", "q_norm_g", "k_norm_g", "sgu_norm_g", "sgu_w",
               "sgu_b")
WEIGHT_NAMES = ("ada_w", "ada_b", "norm1_g", "norm2_g", "w_in", "b_forget", "q_norm_g", "k_norm_g",
                "sgu_norm_g", "sgu_w", "sgu_b", "w_out", "mlp_w1", "mlp_w2")


SMALL_TILE_ROWS = 256


def _pack_small(tree):
    flat = jnp.concatenate([tree[n].reshape(-1) for n in SMALL_NAMES])
    n = flat.shape[0]
    rows = -(-n // (SMALL_TILE_ROWS * LANES)) * SMALL_TILE_ROWS
    return jnp.zeros((rows * LANES,), F32).at[:n].set(flat).reshape(rows, LANES)


def _unpack_small(packed, like):
    flat = packed.reshape(-1)
    out, off = {}, 0
    for n in SMALL_NAMES:
        size = like[n].size
        out[n] = flat[off:off + size].reshape(like[n].shape)
        off += size
    return out


def kernel(x, c, ada_w, ada_b, norm1_g, norm2_g, w_in, b_forget, q_norm_g, k_norm_g, sgu_norm_g, sgu_w, sgu_b, w_out, mlp_w1, mlp_w2, loss_target, m_ada_w, m_ada_b, m_norm1_g, m_norm2_g, m_w_in, m_b_forget, m_q_norm_g, m_k_norm_g, m_sgu_norm_g, m_sgu_w, m_sgu_b, m_w_out, m_mlp_w1, m_mlp_w2, v_ada_w, v_ada_b, v_norm1_g, v_norm2_g, v_w_in, v_b_forget, v_q_norm_g, v_k_norm_g, v_sgu_norm_g, v_sgu_w, v_sgu_b, v_w_out, v_mlp_w1, v_mlp_w2):
    w = dict(ada_w=ada_w, ada_b=ada_b, norm1_g=norm1_g, norm2_g=norm2_g, w_in=w_in,
             b_forget=b_forget, q_norm_g=q_norm_g, k_norm_g=k_norm_g, sgu_norm_g=sgu_norm_g,
             sgu_w=sgu_w, sgu_b=sgu_b, w_out=w_out, mlp_w1=mlp_w1, mlp_w2=mlp_w2)
    mom = dict(ada_w=m_ada_w, ada_b=m_ada_b, norm1_g=m_norm1_g, norm2_g=m_norm2_g, w_in=m_w_in,
               b_forget=m_b_forget, q_norm_g=m_q_norm_g, k_norm_g=m_k_norm_g,
               sgu_norm_g=m_sgu_norm_g, sgu_w=m_sgu_w, sgu_b=m_sgu_b, w_out=m_w_out,
               mlp_w1=m_mlp_w1, mlp_w2=m_mlp_w2)
    var = dict(ada_w=v_ada_w, ada_b=v_ada_b, norm1_g=v_norm1_g, norm2_g=v_norm2_g, w_in=v_w_in,
               b_forget=v_b_forget, q_norm_g=v_q_norm_g, k_norm_g=v_k_norm_g,
               sgu_norm_g=v_sgu_norm_g, sgu_w=v_sgu_w, sgu_b=v_sgu_b, w_out=v_w_out,
               mlp_w1=v_mlp_w1, mlp_w2=v_mlp_w2)
    depth, d = norm1_g.shape
    chip = 2 * lax.axis_index("x") + lax.axis_index("y")
    me = 2 * chip + lax.axis_index("c")
    n_chips = 4
    ada_sh = ada_w.shape[2]

    core = lax.axis_index("c")
    half_l = depth // 2

    def my_part(w_sh):
        _, r, cols = w_sh.shape
        mine = lax.dynamic_slice_in_dim(w_sh, core * half_l, half_l, axis=0).astype(BF16)
        return mine.reshape(half_l * r, cols)

    def share(got, w_sh, name):
        _, r, cols = w_sh.shape
        theirs = _core_swap(got.reshape(n_chips * half_l * r, cols), name=f"share_{name}")
        return _by_core(core, got.reshape(n_chips, half_l, r, cols),
                        theirs.reshape(n_chips, half_l, r, cols), 1)

    g_in = share(_gather_chips(my_part(w_in), name="gather_w_in"), w_in, "w_in")
    layer_w = [dict(w_in=_w_in_to_internal(
        jnp.concatenate([g_in[k, l] for k in range(n_chips)], axis=1))) for l in range(depth)]
    later = (("w_out", w_out), ("w1", mlp_w1), ("w2", mlp_w2))

    def late_weights(gathered):
        g_out, g_w1, g_w2 = [share(got, w_sh, name) for got, (name, w_sh) in zip(gathered, later)]
        for l in range(depth):
            layer_w[l].update(
                w_out=g_out[:, l].reshape(d, d),
                w1=jnp.concatenate([g_w1[k, l] for k in range(n_chips)], axis=1),
                w2=g_w2[:, l].reshape(D_FF, d))

    c_all = _gather_all(jnp.zeros((8, d), F32).at[0].set(c[0]), name="gather_c")[:, 0]
    c_pad = jnp.concatenate([c_all, jnp.zeros_like(c_all)], axis=0)
    ada_b_sh = lax.dynamic_slice_in_dim(ada_b, chip * ada_sh, ada_sh, axis=1)[:, None, :]
    mod_sh = _ada_fwd(c_pad, ada_w, ada_b_sh, name="ada_fwd")
    mod_all = _gather_chips(mod_sh, name="gather_mod")
    mod_me = lax.dynamic_index_in_dim(mod_all, me, axis=2, keepdims=False)
    mod_me = mod_me.transpose(1, 0, 2).reshape(depth, 6, 1, d)

    saved = []
    xs, prev = x[0], None
    for l in range(depth):
        mod = [mod_me[l, kk] for kk in range(6)]
        sm = {n: w[n][l] for n in SMALL_NAMES}
        first = dict(gathers=[my_part(w_sh) for _, w_sh in later], late_weights=late_weights)
        sv = _layer_fwd(xs, prev, mod, layer_w[l], sm, l, **(first if l == 0 else {}))
        saved.append(sv)
        xs, prev = sv["x1"], (sv["m2"], mod[5])

    sq, dxs, dm2, dg2 = _loss_fwd_bwd(xs, prev[0], prev[1], loss_target[0], name="loss")
    loss = lax.psum(0.5 * jnp.sum(sq) / d, ("x", "y", "c"))

    big = {n: [] for n in ("w_in", "w_out", "w1", "w2")}
    small = {n: [] for n in SMALL_NAMES}
    dmods = []
    for l in reversed(range(depth)):
        mod = [mod_me[l, kk] for kk in range(6)]
        sm = {n: w[n][l] for n in SMALL_NAMES}
        below = (saved[l - 1]["m2"], mod_me[l - 1, 5]) if l else None
        dxs, dm2, dg2, bg, smg, dmod = _layer_bwd(dxs, dm2, dg2, saved[l], mod, layer_w[l], sm, l,
                                                  below)
        for n in big:
            big[n].insert(0, bg[n])
        for n in SMALL_NAMES:
            small[n].insert(0, smg[n])
        dmods.insert(0, dmod)
    grad_x = dxs[None]

    out_g, out_d, out_m, out_v = {}, {}, {}, {}

    def run_adamw(name, parts2d, shape):
        rows, cols = parts2d.shape[1:]
        g, dl, nm, nv = _adamw(w[name].reshape(rows, cols), mom[name].reshape(rows, cols),
                               var[name].reshape(rows, cols), parts2d, name=f"adamw_{name}")
        out_g[name], out_d[name] = g.reshape(shape), dl.reshape(shape)
        out_m[name], out_v[name] = nm.reshape(shape), nv.reshape(shape)

    def shards_of(name, l):
        if name == "w_in":
            g = _w_in_from_internal(big["w_in"][l])
            return jnp.stack(jnp.split(g, n_chips, axis=1))
        if name == "mlp_w1":
            return jnp.stack(jnp.split(big["w1"][l], n_chips, axis=1))
        if name == "w_out":
            return big["w_out"][l].reshape(n_chips, d // n_chips, d)
        return big["w2"][l].reshape(n_chips, D_FF // n_chips, d)

    for name in ("w_in", "w_out", "mlp_w1", "mlp_w2"):
        per_chip = jnp.stack([shards_of(name, l) for l in range(depth)], axis=1)
        r, cols = per_chip.shape[2:]
        half_rows = half_l * r
        keep = lax.dynamic_slice_in_dim(per_chip, core * half_l, half_l, axis=1)
        send = lax.dynamic_slice_in_dim(per_chip, (1 - core) * half_l, half_l, axis=1)
        theirs = _core_swap(send.reshape(n_chips * half_rows, cols), name=f"pair_{name}")
        chip_sum = _add2(keep.reshape(n_chips * half_rows, cols), theirs, out_dtype=BF16,
                         name=f"pairsum_{name}")
        got = _scatter_chips(chip_sum.reshape(n_chips, half_rows, cols), name=f"scatter_{name}")
        half = _sum_slots(got, name=f"sum_{name}")
        both = _by_core(core, half, _core_swap(half, name=f"swap_{name}"), 0)
        run_adamw(name, both[None], w[name].shape)

    small_tree = {n: jnp.stack(small[n]) for n in SMALL_NAMES}
    gathered = _gather_all(_pack_small(small_tree), name="gather_small")
    gs, ds_, ms, vs = _adamw(_pack_small({n: w[n] for n in SMALL_NAMES}),
                             _pack_small({n: mom[n] for n in SMALL_NAMES}),
                             _pack_small({n: var[n] for n in SMALL_NAMES}), gathered,
                             name="adamw_small")
    like = {n: w[n] for n in SMALL_NAMES}
    for tree, packed in ((out_g, gs), (out_d, ds_), (out_m, ms), (out_v, vs)):
        tree.update(_unpack_small(packed, like))

    dmod_mine = jnp.concatenate(dmods, axis=0)
    dmod_all = _gather_all(jnp.zeros((depth, 8, 6 * d), F32).at[:, 0].set(dmod_mine),
                           name="gather_dmod")[:, :, 0]
    dmod_lb = dmod_all.transpose(1, 0, 2)
    dmod_sh = lax.dynamic_slice_in_dim(dmod_lb, chip * ada_sh, ada_sh, axis=2)
    dmod_sh = jnp.concatenate([dmod_sh, jnp.zeros_like(dmod_sh)], axis=1)
    g_ada_w = _ada_bwd(c_pad, dmod_sh, name="ada_bwd")
    run_adamw("ada_w", g_ada_w.reshape(1, depth * d, ada_sh), ada_w.shape)
    parts_b = dmod_all.reshape(8, depth * 6 * d // LANES, LANES)
    run_adamw("ada_b", parts_b, ada_b.shape)

    outs = [loss, grad_x]
    for tree in (out_g, out_d, out_m, out_v):
        outs += [tree[n] for n in WEIGHT_NAMES]
    return tuple(outs)
```

```python
import functools
import math

import jax
import jax.numpy as jnp
from jax import lax
from jax.experimental import pallas as pl
from jax.experimental.pallas import tpu as pltpu

F32 = jnp.float32
BF16 = jnp.bfloat16

D_MODEL = 1024
DEPTH = 4
HEAD_DIM = 64
LANES = 128
D_FF = 4 * D_MODEL
EPS = 1e-6
SB_W, FOX_W, SGU_W = 256, 512, 256
FOX_HEADS = 8
SGU_CHUNK = 128
IN_W = 2824
ATT_W = 3 * SB_W + 3 * FOX_W
PROJ_W = 3072
CB_QA, CB_KA, CB_VA = 0, 2, 4
CB_QB, CB_KB, CB_VB = 6, 10, 14
CB_UC, CB_VC, CB_FL = 18, 20, 22
ATT_T = 256
PREP_ROWS = 2048
SGU_ROWS = 2048
NORM_ROWS = 512
NORM_BWD_ROWS = 512
CUMSUM_ROWS = 512
VMEM_LIMIT = 56 * 2 ** 20
SKIP_LOG = 110.0

ADAM_LR, ADAM_B1, ADAM_B2, ADAM_EPS, ADAM_WD, ADAM_STEP = 0.001, 0.9, 0.999, 1e-08, 0.01, 10

MESH = pl.DeviceIdType.MESH


def _pcall(body, *, name, out_shape, grid=(), in_specs=None, out_specs=None, scratch_shapes=(),
           semantics=None):
    params = dict(vmem_limit_bytes=VMEM_LIMIT)
    if semantics is not None:
        params["dimension_semantics"] = semantics
    kwargs = {}
    if in_specs is not None:
        kwargs["in_specs"] = in_specs
    if out_specs is not None:
        kwargs["out_specs"] = out_specs
    return pl.pallas_call(body, name=name, out_shape=out_shape, grid=grid,
                          scratch_shapes=list(scratch_shapes),
                          compiler_params=pltpu.CompilerParams(**params), **kwargs)


def _dot(a, b):
    return jnp.dot(a, b, preferred_element_type=F32)


def _dot_nt(a, b):
    return lax.dot_general(a, b, (((1,), (1,)), ((), ())), preferred_element_type=F32)


def _dot_tn(a, b):
    return lax.dot_general(a, b, (((0,), (0,)), ((), ())), preferred_element_type=F32)


def _split2(x):
    hi = x.astype(BF16)
    lo = (x - hi.astype(F32)).astype(BF16)
    return hi, lo


def _ones_dot(x, ones_bf16):
    hi, lo = _split2(x)
    return _dot(hi, ones_bf16) + _dot(lo, ones_bf16)


def _rowwise(fn, fulls, vecs, out_dtypes, n_vec_out, *, name, tr):
    s, n = fulls[0].shape
    tr = min(tr, s)
    assert s % tr == 0, (name, s, tr)
    nf, nv, nfo = len(fulls), len(vecs), len(out_dtypes)

    def body(*refs):
        fi, vi = refs[:nf], refs[nf:nf + nv]
        fo, vo = refs[nf + nv:nf + nv + nfo], refs[nf + nv + nfo:]
        outs_f, outs_v = fn([r[...] for r in fi], [r[...] for r in vi])
        for r, o in zip(fo, outs_f):
            r[...] = o.astype(r.dtype)
        if n_vec_out:
            @pl.when(pl.program_id(0) == 0)
            def _():
                for r in vo:
                    r[...] = jnp.zeros_like(r)
            for r, o in zip(vo, outs_v):
                r[...] += o

    full_spec = pl.BlockSpec((tr, n), lambda i: (i, 0))
    vec_specs = [pl.BlockSpec(v.shape, lambda i: (0, 0)) for v in vecs]
    out_vec_spec = pl.BlockSpec((1, n), lambda i: (0, 0))
    out_shape = [jax.ShapeDtypeStruct((s, n), dt) for dt in out_dtypes]
    out_shape += [jax.ShapeDtypeStruct((1, n), F32)] * n_vec_out
    outs = _pcall(body, name=name, grid=(s // tr,),
                  in_specs=[full_spec] * nf + vec_specs,
                  out_specs=[full_spec] * nfo + [out_vec_spec] * n_vec_out,
                  out_shape=out_shape,
                  semantics=("arbitrary",) if n_vec_out else ("parallel",))(*fulls, *vecs)
    return outs[:nfo], outs[nfo:]


def _colsum(t):
    return jnp.sum(t, axis=0, keepdims=True)


def _rms_mod(x, g, sc, sh):
    r = lax.rsqrt(jnp.mean(x * x, axis=-1, keepdims=True) + EPS)
    return (x * r * g) * (1.0 + sc) + sh


def _norm_mod_fwd(x, g, sc, sh, *, name):
    def fn(f, v):
        return [_rms_mod(f[0], v[0], v[1], v[2])], []
    (h,), _ = _rowwise(fn, [x], [g, sc, sh], [BF16], 0, name=name, tr=NORM_ROWS)
    return h


def _resid_norm_mod_fwd(x, m, gate, g, sc, sh, *, name):
    def fn(f, v):
        xn = f[0] + v[0] * f[1]
        return [xn, _rms_mod(xn, v[1], v[2], v[3])], []
    (xn, h), _ = _rowwise(fn, [x, m], [gate, g, sc, sh], [F32, BF16], 0, name=name, tr=NORM_ROWS)
    return xn, h


def _norm_mod_bwd(x, dh, dres, g, sc, gated, *, name):
    def fn(f, v):
        xv, dhv, dr = f[:3]
        gv, scv = v[:2]
        r = lax.rsqrt(jnp.mean(xv * xv, axis=-1, keepdims=True) + EPS)
        xh = xv * r
        dn = dhv * (1.0 + scv)
        dxh = dn * gv
        dx = dr + r * (dxh - xh * jnp.mean(dxh * xh, axis=-1, keepdims=True))
        sums = [_colsum(dn * xh), _colsum(dhv * (xh * gv)), _colsum(dhv)]
        if gated is None:
            return [dx], sums
        return [dx, dx * v[2]], sums + [_colsum(dx * f[3])]
    if gated is None:
        (dx,), (dg, dsc, dsh) = _rowwise(fn, [x, dh, dres], [g, sc], [F32], 3, name=name,
                                         tr=NORM_BWD_ROWS)
        return dx, dg, dsc, dsh, None, None
    (dx, dm), (dg, dsc, dsh, dgate) = _rowwise(fn, [x, dh, dres, gated[0]], [g, sc, gated[1]],
                                               [F32, BF16], 4, name=name, tr=NORM_BWD_ROWS)
    return dx, dg, dsc, dsh, dm, dgate


def _loss_fwd_bwd(x, m, gate, target, *, name):
    n = x.shape[1]

    def fn(f, v):
        err = f[0] + v[0] * f[1] - f[2]
        dy = err * (1.0 / n)
        return [dy, dy * v[0]], [_colsum(err * err), _colsum(dy * f[1])]
    (dy, dm), (sq, dgate) = _rowwise(fn, [x, m, target], [gate], [F32, BF16], 2, name=name,
                                     tr=NORM_BWD_ROWS)
    return sq, dy, dm, dgate


def _matmul(a, b, *, name, ta=False, tb=False, out_dtype=F32, relu2=None, pre_act=None,
            tm=1024, tn=1024, tk_max=2048):
    m = a.shape[1] if ta else a.shape[0]
    k = a.shape[0] if ta else a.shape[1]
    n = b.shape[0] if tb else b.shape[1]
    assert k == (b.shape[1] if tb else b.shape[0])
    tk = max(dd for dd in range(LANES, min(tk_max, k) + 1, LANES) if k % dd == 0) if k > LANES else k
    tm, tn = min(tm, m), min(tn, n)
    assert m % tm == 0 and n % tn == 0 and k % tk == 0, (name, m, n, k)
    nk = k // tk
    dims = (((0 if ta else 1,), (1 if tb else 0,)), ((), ()))

    plain = relu2 is None and pre_act is None
    in_place = plain and out_dtype == F32
    n_in = 2 + (pre_act is not None)

    def body(*refs):
        a_ref, b_ref = refs[:2]
        o_ref = refs[n_in]
        prod = lax.dot_general(a_ref[...].astype(BF16), b_ref[...].astype(BF16), dims,
                               preferred_element_type=F32)

        def finish(acc):
            if pre_act is not None:
                acc = acc * (2.0 * jnp.maximum(refs[2][...].astype(F32), 0.0))
            o_ref[...] = acc.astype(o_ref.dtype)
            if relu2 is not None:
                r = jnp.maximum(acc, 0.0)
                refs[n_in + 1][...] = (r * r).astype(relu2)

        if nk == 1:
            finish(prod)
            return
        kk = pl.program_id(2)
        acc_ref = o_ref if in_place else refs[-1]

        @pl.when(kk == 0)
        def _():
            acc_ref[...] = prod

        @pl.when(kk > 0)
        def _():
            acc_ref[...] += prod

        if not in_place:
            @pl.when(kk == nk - 1)
            def _():
                finish(acc_ref[...])

    a_spec = (pl.BlockSpec((tk, tm), lambda i, j, kk: (kk, i)) if ta
              else pl.BlockSpec((tm, tk), lambda i, j, kk: (i, kk)))
    b_spec = (pl.BlockSpec((tn, tk), lambda i, j, kk: (j, kk)) if tb
              else pl.BlockSpec((tk, tn), lambda i, j, kk: (kk, j)))
    out_spec = pl.BlockSpec((tm, tn), lambda i, j, kk: (i, j))
    in_specs, args = [a_spec, b_spec], [a, b]
    if pre_act is not None:
        in_specs.append(out_spec)
        args.append(pre_act)
    out_specs, out_shape = out_spec, jax.ShapeDtypeStruct((m, n), out_dtype)
    if relu2 is not None:
        out_specs, out_shape = [out_spec] * 2, [out_shape, jax.ShapeDtypeStruct((m, n), relu2)]
    return _pcall(body, name=name, grid=(m // tm, n // tn, nk),
                  in_specs=in_specs, out_specs=out_specs, out_shape=out_shape,
                  scratch_shapes=[] if nk == 1 or in_place else [pltpu.VMEM((tm, tn), F32)],
                  semantics=("parallel", "parallel", "arbitrary"))(*args)


def _lane_masks():
    lane = lax.broadcasted_iota(jnp.int32, (1, LANES), 1)
    return [lane < HEAD_DIM, lane >= HEAD_DIM]


def _tri_iotas(t):
    r = lax.broadcasted_iota(jnp.int32, (t, t), 0)
    c = lax.broadcasted_iota(jnp.int32, (t, t), 1)
    return r, c


def _rows(j, t):
    return pl.ds(pl.multiple_of(j * t, t), t)


def _neg_softplus(z):
    e = jnp.exp(-jnp.abs(z))
    return -(jnp.maximum(z, 0.0) + jnp.log(1.0 + e)), e


def _sb_fwd(proj, *, name):
    s = proj.shape[0]
    t = min(ATT_T, s)
    scale = HEAD_DIM ** -0.5

    def body(q_ref, k_ref, v_ref, o_ref, ltot_ref, stop_ref):
        i = pl.program_id(1)
        hm = _lane_masks()
        q = q_ref[...] * scale
        qh = [jnp.where(mk, q, 0.0).astype(BF16) for mk in hm]
        r, c = _tri_iotas(t)
        later = (r > c).astype(BF16)
        q2 = jnp.concatenate(qh, axis=0)
        causal2 = jnp.concatenate([c < r, c < r], axis=0)

        def scores(j):
            return _dot_nt(q2, k_ref[_rows(j, t), :].astype(BF16))

        def chunk(j, carry, z, masked):
            e_run, acc = carry
            vb = v_ref[_rows(j, t), :].astype(BF16)
            l, _ = _neg_softplus(z)
            if masked:
                l = jnp.where(causal2, l, 0.0)
            between = _ones_dot(l, later) + e_run
            a = jnp.exp(z + l + between)
            if masked:
                a = jnp.where(causal2, a, 0.0)
            return e_run + jnp.sum(l, axis=1, keepdims=True), acc + _dot(a.astype(BF16), vb)

        init = (jnp.zeros((2 * t, 1), F32), jnp.zeros((2 * t, LANES), F32))
        carry = chunk(i, init, scores(i), True)

        def step(st):
            j, cr, z = st
            z_next = scores(jnp.maximum(j - 1, 0))
            return j - 1, chunk(j, cr, z, False), z_next

        j_stop, (e_tot, acc), _ = lax.while_loop(
            lambda st: (st[0] >= 0) & (jnp.max(st[1][0]) > -SKIP_LOG), step,
            (i - 1, carry, scores(jnp.maximum(i - 1, 0))))
        o_ref[...] = jnp.where(hm[0], acc[:t], acc[t:])
        ltot_ref[...] = jnp.where(hm[0], e_tot[:t], e_tot[t:])
        stop_ref[...] = jnp.full(stop_ref.shape, j_stop.astype(F32), F32)

    blk = lambda cb: pl.BlockSpec((t, LANES), lambda p, i: (i, cb + p))
    full = lambda cb: pl.BlockSpec((s, LANES), lambda p, i: (0, cb + p))
    out_blk = pl.BlockSpec((t, LANES), lambda p, i: (i, p))
    n_pairs = SB_W // LANES
    return _pcall(body, name=name, grid=(n_pairs, s // t),
                  in_specs=[blk(CB_QA), full(CB_KA), full(CB_VA)],
                  out_specs=[out_blk, out_blk,
                             pl.BlockSpec((1, 1, 8, LANES), lambda p, i: (p, i, 0, 0))],
                  out_shape=[jax.ShapeDtypeStruct((s, SB_W), F32)] * 2
                  + [jax.ShapeDtypeStruct((n_pairs, s // t, 8, LANES), F32)],
                  semantics=("parallel", "arbitrary"))(proj, proj, proj)


def _sb_bwd(proj, dmixed, ltot, stop, *, name):
    s = proj.shape[0]
    t = min(ATT_T, s)
    scale = HEAD_DIM ** -0.5

    def body(q_ref, k_ref, v_ref, do_ref, ltot_ref, stop_ref, dq_ref, dk_ref, dv_ref):
        i = pl.program_id(1)

        @pl.when(i == 0)
        def _():
            dk_ref[...] = jnp.zeros_like(dk_ref)
            dv_ref[...] = jnp.zeros_like(dv_ref)

        hm = _lane_masks()
        q = q_ref[...] * scale
        do = do_ref[...]
        qh = [jnp.where(mk, q, 0.0).astype(BF16) for mk in hm]
        doh = [jnp.where(mk, do, 0.0).astype(BF16) for mk in hm]
        r, c = _tri_iotas(t)
        upto = (r <= c).astype(BF16)
        before = (r < c).astype(BF16)
        q2 = jnp.concatenate(qh, axis=0)
        do2 = jnp.concatenate(doh, axis=0)
        causal2 = jnp.concatenate([c < r, c < r], axis=0)

        j_stop = jnp.clip(jnp.max(stop_ref[...]).astype(jnp.int32), -1, i - 1)
        ltv = ltot_ref[...]
        lt = jnp.concatenate([ltv[:, 0:1], ltv[:, HEAD_DIM:HEAD_DIM + 1]], axis=0)

        def products(j):
            return (_dot_nt(q2, k_ref[_rows(j, t), :].astype(BF16)),
                    _dot_nt(do2, v_ref[_rows(j, t), :].astype(BF16)))

        def chunk(j, carry, z, da, masked):
            l_run, g_run, dq = carry
            l, e = _neg_softplus(z)
            beta = jnp.where(z >= 0.0, 1.0, e) / (1.0 + e)
            if masked:
                l = jnp.where(causal2, l, 0.0)
            prefix = _ones_dot(l, upto) + l_run
            a = jnp.exp(z + l + (lt - prefix))
            if masked:
                a = jnp.where(causal2, a, 0.0)
            g = a * da
            g_before = _ones_dot(g, before) + g_run
            dz = g * (1.0 - beta) - beta * g_before
            if masked:
                dz = jnp.where(causal2, dz, 0.0)
            dzb = dz.astype(BF16)
            dk_ref[_rows(j, t), :] += _dot_tn(dzb, q2)
            dv_ref[_rows(j, t), :] += _dot_tn(a.astype(BF16), do2)
            return (l_run + jnp.sum(l, axis=1, keepdims=True),
                    g_run + jnp.sum(g, axis=1, keepdims=True),
                    dq + _dot(dzb, k_ref[_rows(j, t), :].astype(BF16)))

        init = (jnp.zeros((2 * t, 1), F32), jnp.zeros((2 * t, 1), F32),
                jnp.zeros((2 * t, LANES), F32))
        carry = lax.fori_loop(j_stop + 1, i,
                              lambda j, cr: chunk(j, cr, *products(j), False), init)
        dq2 = chunk(i, carry, *products(i), True)[2]
        dq_ref[...] = jnp.where(hm[0], dq2[:t], dq2[t:]) * scale

    blk = lambda cb: pl.BlockSpec((t, LANES), lambda p, i: (i, cb + p))
    full = lambda cb: pl.BlockSpec((s, LANES), lambda p, i: (0, cb + p))
    out_blk = pl.BlockSpec((t, LANES), lambda p, i: (i, p))
    out_full = pl.BlockSpec((s, LANES), lambda p, i: (0, p))
    return _pcall(body, name=name, grid=(SB_W // LANES, s // t),
                  in_specs=[blk(CB_QA), full(CB_KA), full(CB_VA), blk(0), out_blk,
                            pl.BlockSpec((1, 1, 8, LANES), lambda p, i: (p, i, 0, 0))],
                  out_specs=[out_blk, out_full, out_full],
                  out_shape=[jax.ShapeDtypeStruct((s, SB_W), F32)] * 3,
                  semantics=("parallel", "arbitrary"))(proj, proj, proj, dmixed, ltot, stop)


def _group_mean(v, lo):
    s0 = jnp.sum(jnp.where(lo, v, 0.0), axis=1, keepdims=True)
    s1 = jnp.sum(jnp.where(lo, 0.0, v), axis=1, keepdims=True)
    return jnp.where(lo, s0, s1) * (1.0 / HEAD_DIM)


def _fox_prep_fwd(proj, qg, kg, *, name):
    s = proj.shape[0]
    tr = min(PREP_ROWS, s)

    def body(q_ref, k_ref, qg_ref, kg_ref, qn_ref, kn_ref, kmax_ref):
        lo = _lane_masks()[0]
        for x_ref, g_ref, o_ref in ((q_ref, qg_ref, qn_ref), (k_ref, kg_ref, kn_ref)):
            x = x_ref[...]
            o_ref[...] = x * lax.rsqrt(_group_mean(x * x, lo) + EPS) * g_ref[...]

        @pl.when(pl.program_id(1) == 0)
        def _():
            kmax_ref[...] = jnp.zeros_like(kmax_ref)
        kn = kn_ref[...]
        norms = jnp.sqrt(_group_mean(kn * kn, lo) * HEAD_DIM)
        kmax_ref[...] = jnp.maximum(kmax_ref[...], jnp.max(norms, axis=0, keepdims=True))

    blk = lambda cb: pl.BlockSpec((tr, LANES), lambda p, i: (i, cb + p))
    vec = pl.BlockSpec((1, LANES), lambda p, i: (0, 0))
    out_blk = pl.BlockSpec((tr, LANES), lambda p, i: (i, p))
    return _pcall(body, name=name, grid=(FOX_W // LANES, s // tr),
                  in_specs=[blk(CB_QB), blk(CB_KB), vec, vec],
                  out_specs=[out_blk, out_blk, pl.BlockSpec((1, LANES), lambda p, i: (0, p))],
                  out_shape=[jax.ShapeDtypeStruct((s, FOX_W), F32)] * 2
                  + [jax.ShapeDtypeStruct((1, FOX_W), F32)],
                  semantics=("parallel", "arbitrary"))(proj, proj, qg, kg)


def _fox_prep_bwd(proj, dqn, dkn, qg, kg, *, name):
    s = proj.shape[0]
    tr = min(PREP_ROWS, s)

    def body(q_ref, k_ref, dqn_ref, dkn_ref, qg_ref, kg_ref, dq_ref, dk_ref, dqg_ref, dkg_ref):
        @pl.when((pl.program_id(0) == 0) & (pl.program_id(1) == 0))
        def _():
            dqg_ref[...] = jnp.zeros_like(dqg_ref)
            dkg_ref[...] = jnp.zeros_like(dkg_ref)

        lo = _lane_masks()[0]
        for x_ref, dy_ref, g_ref, dx_ref, dg_ref in ((q_ref, dqn_ref, qg_ref, dq_ref, dqg_ref),
                                                     (k_ref, dkn_ref, kg_ref, dk_ref, dkg_ref)):
            x, dy = x_ref[...], dy_ref[...]
            r = lax.rsqrt(_group_mean(x * x, lo) + EPS)
            xh = x * r
            dxh = dy * g_ref[...]
            dx_ref[...] = r * (dxh - xh * _group_mean(dxh * xh, lo))
            dg_ref[...] += _colsum(dy * xh)

    blk = lambda cb: pl.BlockSpec((tr, LANES), lambda p, i: (i, cb + p))
    vec = pl.BlockSpec((1, LANES), lambda p, i: (0, 0))
    out_blk = pl.BlockSpec((tr, LANES), lambda p, i: (i, p))
    return _pcall(body, name=name, grid=(FOX_W // LANES, s // tr),
                  in_specs=[blk(CB_QB), blk(CB_KB), out_blk, out_blk, vec, vec],
                  out_specs=[out_blk, out_blk, vec, vec],
                  out_shape=[jax.ShapeDtypeStruct((s, FOX_W), F32)] * 2
                  + [jax.ShapeDtypeStruct((1, LANES), F32)] * 2,
                  semantics=("arbitrary", "arbitrary"))(proj, proj, dqn, dkn, qg, kg)


def _split3_dot(tri_bf16, x):
    hi = x.astype(BF16)
    r1 = x - hi.astype(F32)
    mid = r1.astype(BF16)
    lo = (r1 - mid.astype(F32)).astype(BF16)
    return _dot(tri_bf16, hi) + _dot(tri_bf16, mid) + _dot(tri_bf16, lo)


def _forget_cumsum_fwd(proj, b_pad, *, name):
    s = proj.shape[0]
    tb = min(CUMSUM_ROWS, s)

    def body(fl_ref, b_ref, cf_ref, run_ref):
        @pl.when(pl.program_id(0) == 0)
        def _():
            run_ref[...] = jnp.zeros_like(run_ref)
        lf, _ = _neg_softplus(-(fl_ref[...] + b_ref[...]))
        r, c = _tri_iotas(tb)
        incl = _split3_dot((c <= r).astype(BF16), lf) + run_ref[...]
        cf_ref[...] = incl
        run_ref[...] = incl[tb - 1:tb, :]

    return _pcall(body, name=name, grid=(s // tb,),
                  in_specs=[pl.BlockSpec((tb, LANES), lambda i: (i, CB_FL)),
                            pl.BlockSpec((1, LANES), lambda i: (0, 0))],
                  out_specs=pl.BlockSpec((tb, LANES), lambda i: (i, 0)),
                  out_shape=jax.ShapeDtypeStruct((s, LANES), F32),
                  scratch_shapes=[pltpu.VMEM((1, LANES), F32)],
                  semantics=("arbitrary",))(proj, b_pad)


def _forget_cumsum_bwd(proj, b_pad, dcf, *, name):
    s = proj.shape[0]
    tb = min(CUMSUM_ROWS, s)
    nb = s // tb

    def body(fl_ref, b_ref, dcf_ref, dfl_ref, db_ref, run_ref):
        @pl.when(pl.program_id(0) == 0)
        def _():
            run_ref[...] = jnp.zeros_like(run_ref)
            db_ref[...] = jnp.zeros_like(db_ref)
        r, c = _tri_iotas(tb)
        dlf = _split3_dot((c >= r).astype(BF16), dcf_ref[...]) + run_ref[...]
        run_ref[...] = dlf[0:1, :]
        xv = fl_ref[...] + b_ref[...]
        e = jnp.exp(-jnp.abs(xv))
        sig_neg = jnp.where(xv >= 0.0, e, 1.0) / (1.0 + e)
        dfl = dlf * sig_neg
        dfl_ref[...] = dfl
        db_ref[...] += _colsum(dfl)

    return _pcall(body, name=name, grid=(nb,),
                  in_specs=[pl.BlockSpec((tb, LANES), lambda i: (nb - 1 - i, CB_FL)),
                            pl.BlockSpec((1, LANES), lambda i: (0, 0)),
                            pl.BlockSpec((tb, LANES), lambda i: (nb - 1 - i, 0))],
                  out_specs=[pl.BlockSpec((tb, LANES), lambda i: (nb - 1 - i, 0)),
                             pl.BlockSpec((1, LANES), lambda i: (0, 0))],
                  out_shape=[jax.ShapeDtypeStruct((s, LANES), F32),
                             jax.ShapeDtypeStruct((1, LANES), F32)],
                  scratch_shapes=[pltpu.VMEM((1, LANES), F32)],
                  semantics=("arbitrary",))(proj, b_pad, dcf)


def _fox_bias_q(cfc, p, h):
    lane = lax.broadcasted_iota(jnp.int32, (1, LANES), 1)
    return jnp.sum(jnp.where(lane == 2 * p + h, cfc, 0.0), axis=1, keepdims=True)


def _fox_score_bound(q, kmax_row, hm):
    out = []
    for h in range(2):
        qnorm = jnp.sqrt(jnp.sum(jnp.where(hm[h], q * q, 0.0), axis=1, keepdims=True))
        out.append(1.02 * qnorm * kmax_row[:, h * HEAD_DIM:h * HEAD_DIM + 1])
    return out


def _fox_live(cfr_ref, j, t, tops):
    jc = jnp.maximum(j, 0)
    worst = []
    for h in range(2):
        cf_min = jnp.min(cfr_ref[0, pl.ds(h, 1), _rows(jc, t)], axis=1, keepdims=True)
        worst.append(jnp.max(tops[h] - cf_min))
    return (j >= 0) & (jnp.maximum(worst[0], worst[1]) > -SKIP_LOG)


def _chip_gather_copies(x_refs, out_refs, send_sems, recv_sems, local_sems):
    ids = (lax.axis_index("x"), lax.axis_index("y"), lax.axis_index("c"))
    chip = 2 * ids[0] + ids[1]
    copies = []
    for n, (x_ref, out_ref) in enumerate(zip(x_refs, out_refs)):
        copies.append(pltpu.make_async_copy(x_ref, out_ref.at[chip], local_sems.at[n]))
        for kk, (flip_x, flip_y) in enumerate(((1, 0), (0, 1), (1, 1))):
            peer = (1 - ids[0] if flip_x else ids[0], 1 - ids[1] if flip_y else ids[1], ids[2])
            copies.append(pltpu.make_async_remote_copy(
                src_ref=x_ref, dst_ref=out_ref.at[chip],
                send_sem=send_sems.at[3 * n + kk], recv_sem=recv_sems.at[3 * n + kk],
                device_id=peer, device_id_type=MESH))
    return copies


def _fox_fwd(proj, qn, kn, cf, cf_rows, kmax, *, name, gathers=()):
    s = proj.shape[0]
    t = min(ATT_T, s)
    scale = HEAD_DIM ** -0.5
    n_pairs, nq, ng = FOX_W // LANES, s // t, len(gathers)

    def body(*refs):
        q_ref, k_ref, v_ref, cfc_ref, cfr_ref, kmax_ref = refs[:6]
        o_ref, lse_ref = refs[6 + ng:8 + ng]
        p, i = pl.program_id(0), pl.program_id(1)
        if ng:
            def copies():
                return _chip_gather_copies(refs[6:6 + ng], refs[8 + ng:8 + 2 * ng], *refs[8 + 2 * ng:])

            @pl.when((p == 0) & (i == 0))
            def _():
                for cp in copies():
                    cp.start()
        hm = _lane_masks()
        q = q_ref[...] * scale
        qh = [jnp.where(mk, q, 0.0).astype(BF16) for mk in hm]
        cfc = cfc_ref[...]
        bq = [_fox_bias_q(cfc, p, h) for h in range(2)]
        qk_top = _fox_score_bound(q, kmax_ref[...], hm)
        r, c = _tri_iotas(t)
        causal = c <= r

        q2 = jnp.concatenate(qh, axis=0)
        causal2 = jnp.concatenate([causal, causal], axis=0)

        def scores(j):
            return _dot_nt(q2, k_ref[_rows(j, t), :].astype(BF16))

        def chunk(j, carry, z2, masked):
            m_run, l_run, acc = carry
            vb = v_ref[_rows(j, t), :].astype(BF16)
            z = jnp.concatenate(
                [z2[h * t:(h + 1) * t] + (bq[h] - cfr_ref[0, pl.ds(h, 1), _rows(j, t)])
                 for h in range(2)], axis=0)
            if masked:
                z = jnp.where(causal2, z, -1e30)
            m_new = jnp.maximum(m_run, jnp.max(z, axis=1, keepdims=True))
            alpha = jnp.exp(m_run - m_new)
            pr = jnp.exp(z - m_new)
            return (m_new, alpha * l_run + jnp.sum(pr, axis=1, keepdims=True),
                    alpha * acc + _dot(pr.astype(BF16), vb))

        init = (jnp.full((2 * t, 1), -1e30, F32), jnp.zeros((2 * t, 1), F32),
                jnp.zeros((2 * t, LANES), F32))
        carry = chunk(i, init, scores(i), True)

        def live(j, cr):
            return _fox_live(cfr_ref, j, t,
                             [qk_top[h] + bq[h] - cr[0][h * t:(h + 1) * t] for h in range(2)])

        def step(st):
            j, _, cr, z2 = st
            z2_next = scores(jnp.maximum(j - 1, 0))
            cr = chunk(j, cr, z2, False)
            return j - 1, live(j - 1, cr), cr, z2_next

        m_fin, l_fin, acc = lax.while_loop(
            lambda st: st[1], step,
            (i - 1, live(i - 1, carry), carry, scores(jnp.maximum(i - 1, 0))))[2]
        o2 = acc / l_fin
        lse2 = m_fin + jnp.log(l_fin)
        o_ref[...] = jnp.where(hm[0], o2[:t], o2[t:])
        lse_ref[...] = jnp.where(hm[0], lse2[:t], lse2[t:])
        if ng:
            @pl.when((p == n_pairs - 1) & (i == nq - 1))
            def _():
                for cp in copies():
                    cp.wait()

    blk = pl.BlockSpec((t, LANES), lambda p, i: (i, p))
    full = pl.BlockSpec((s, LANES), lambda p, i: (0, p))
    any_spec = pl.BlockSpec(memory_space=pl.ANY)
    dma = pltpu.SemaphoreType.DMA
    return _pcall(body, name=name, grid=(n_pairs, nq),
                  in_specs=[blk, full, pl.BlockSpec((s, LANES), lambda p, i: (0, CB_VB + p)),
                            pl.BlockSpec((t, LANES), lambda p, i: (i, 0)),
                            pl.BlockSpec((1, 2, s), lambda p, i: (p, 0, 0)),
                            pl.BlockSpec((1, LANES), lambda p, i: (0, p))] + [any_spec] * ng,
                  out_specs=[blk, blk] + [any_spec] * ng,
                  out_shape=[jax.ShapeDtypeStruct((s, FOX_W), F32)] * 2
                  + [jax.ShapeDtypeStruct((4,) + g.shape, g.dtype) for g in gathers],
                  scratch_shapes=[dma((3 * ng,)), dma((3 * ng,)), dma((ng,))] if ng else [],
                  semantics=("arbitrary", "arbitrary") if ng else ("parallel", "arbitrary"))(
                      qn, kn, proj, cf, cf_rows, kmax, *gathers)


def _fox_bwd(proj, qn, kn, cf, cf_rows, kmax, do, o, lse, *, name):
    s = proj.shape[0]
    t = min(ATT_T, s)
    scale = HEAD_DIM ** -0.5

    def body(q_ref, k_ref, v_ref, cfc_ref, cfr_ref, kmax_ref, do_ref, o_ref, lse_ref,
             dq_ref, dk_ref, dv_ref, dcf_ref, dcfq_ref):
        p, i = pl.program_id(0), pl.program_id(1)

        @pl.when(i == 0)
        def _():
            dk_ref[...] = jnp.zeros_like(dk_ref)
            dv_ref[...] = jnp.zeros_like(dv_ref)
            dcf_ref[...] = jnp.zeros_like(dcf_ref)

        hm = _lane_masks()
        q = q_ref[...] * scale
        do = do_ref[...]
        dov = do * o_ref[...]
        qh = [jnp.where(mk, q, 0.0).astype(BF16) for mk in hm]
        doh = [jnp.where(mk, do, 0.0).astype(BF16) for mk in hm]
        delta = [jnp.sum(jnp.where(mk, dov, 0.0), axis=1, keepdims=True) for mk in hm]
        lsev = lse_ref[...]
        lse = [lsev[:, 0:1], lsev[:, HEAD_DIM:HEAD_DIM + 1]]
        cfc = cfc_ref[...]
        bq = [_fox_bias_q(cfc, p, h) - lse[h] for h in range(2)]
        qk_top = _fox_score_bound(q, kmax_ref[...], hm)
        tops = [qk_top[h] + bq[h] for h in range(2)]
        r, c = _tri_iotas(t)
        j_stop = lax.while_loop(lambda st: st[1],
                                lambda st: (st[0] - 1, _fox_live(cfr_ref, st[0] - 1, t, tops)),
                                (i - 1, _fox_live(cfr_ref, i - 1, t, tops)))[0]
        q2 = jnp.concatenate(qh, axis=0)
        do2 = jnp.concatenate(doh, axis=0)
        delta2 = jnp.concatenate(delta, axis=0)
        causal2 = jnp.concatenate([c <= r, c <= r], axis=0)

        def products(j):
            return (_dot_nt(q2, k_ref[_rows(j, t), :].astype(BF16)),
                    _dot_nt(do2, v_ref[_rows(j, t), :].astype(BF16)))

        def chunk(j, carry, z2, dp, masked):
            dq, row_sum = carry
            z = jnp.concatenate(
                [z2[h * t:(h + 1) * t] + (bq[h] - cfr_ref[0, pl.ds(h, 1), _rows(j, t)])
                 for h in range(2)], axis=0)
            pr = jnp.exp(z)
            if masked:
                pr = jnp.where(causal2, pr, 0.0)
            ds = pr * (dp - delta2)
            dsb = ds.astype(BF16)
            dk_ref[_rows(j, t), :] += _dot_tn(dsb, q2)
            dv_ref[_rows(j, t), :] += _dot_tn(pr.astype(BF16), do2)
            for h in range(2):
                dcf_ref[0, pl.ds(h, 1), _rows(j, t)] -= jnp.sum(ds[h * t:(h + 1) * t], axis=0,
                                                               keepdims=True)
            return (dq + _dot(dsb, k_ref[_rows(j, t), :].astype(BF16)),
                    row_sum + jnp.sum(ds, axis=1, keepdims=True))

        def one(j, cr):
            return chunk(j, cr, *products(j), False)

        init = (jnp.zeros((2 * t, LANES), F32), jnp.zeros((2 * t, 1), F32))
        first, odd = j_stop + 1, (i - j_stop - 1) % 2
        carry = lax.cond(odd == 1, lambda cr: one(first, cr), lambda cr: cr, init)
        carry = lax.fori_loop(0, (i - first) // 2,
                              lambda n, cr: one(first + odd + 2 * n + 1,
                                                one(first + odd + 2 * n, cr)), carry)
        dq2, row_sum = chunk(i, carry, *products(i), True)
        dq_ref[...] = jnp.where(hm[0], dq2[:t], dq2[t:]) * scale
        dcfq_ref[...] = jnp.where(hm[0], row_sum[:t], row_sum[t:])

    blk = pl.BlockSpec((t, LANES), lambda p, i: (i, p))
    full = pl.BlockSpec((s, LANES), lambda p, i: (0, p))
    rows = pl.BlockSpec((1, 2, s), lambda p, i: (p, 0, 0))
    return _pcall(body, name=name, grid=(FOX_W // LANES, s // t),
                  in_specs=[blk, full, pl.BlockSpec((s, LANES), lambda p, i: (0, CB_VB + p)),
                            pl.BlockSpec((t, LANES), lambda p, i: (i, 0)), rows,
                            pl.BlockSpec((1, LANES), lambda p, i: (0, p)),
                            pl.BlockSpec((t, LANES), lambda p, i: (i, SB_W // LANES + p)),
                            blk, blk],
                  out_specs=[blk, full, full, rows, blk],
                  out_shape=[jax.ShapeDtypeStruct((s, FOX_W), F32)] * 3
                  + [jax.ShapeDtypeStruct((FOX_W // LANES, 2, s), F32),
                     jax.ShapeDtypeStruct((s, FOX_W), F32)],
                  semantics=("parallel", "arbitrary"))(qn, kn, proj, cf, cf_rows, kmax, do, o, lse)


_GELU_C0 = math.sqrt(2.0 / math.pi)
_GELU_C1 = 0.044715


def _gelu(x):
    th = jnp.tanh(_GELU_C0 * (x + _GELU_C1 * (x * x * x)))
    return 0.5 * x * (1.0 + th), th


def _gelu_grad(x, th):
    return 0.5 * (1.0 + th) + 0.5 * x * (1.0 - th * th) * (_GELU_C0 * (1.0 + 3.0 * _GELU_C1 * x * x))


def _sgu_mix(wm, vn_c, lo, bcol):
    return jnp.where(lo, _dot(wm[0], vn_c) + bcol[0], _dot(wm[1], vn_c) + bcol[1])


def _sgu_fwd(proj, w, b_cols, gn, *, name):
    s = proj.shape[0]
    tr = min(SGU_ROWS, s)
    ch = SGU_CHUNK

    def body(u_ref, v_ref, w_ref, b_ref, gn_ref, o_ref):
        lo = _lane_masks()[0]
        r, c = _tri_iotas(ch)
        wm = [jnp.where(c <= r, w_ref[h], 0.0).astype(BF16) for h in range(2)]
        bcol = [b_ref[0, :, h:h + 1] for h in range(2)]
        for n in range(tr // ch):
            rows = slice(n * ch, (n + 1) * ch)
            u, _ = _gelu(u_ref[rows, :])
            vg, _ = _gelu(v_ref[rows, :])
            vn = vg * lax.rsqrt(_group_mean(vg * vg, lo) + EPS) * gn_ref[0]
            o_ref[rows, :] = u * _sgu_mix(wm, vn.astype(BF16), lo, bcol)

    blk = lambda cb: pl.BlockSpec((tr, LANES), lambda p, i: (i, cb + p))
    return _pcall(body, name=name, grid=(SGU_W // LANES, s // tr),
                  in_specs=[blk(CB_UC), blk(CB_VC),
                            pl.BlockSpec((2, ch, ch), lambda p, i: (p, 0, 0)),
                            pl.BlockSpec((1, ch, 2), lambda p, i: (p, 0, 0)),
                            pl.BlockSpec((1, 1, LANES), lambda p, i: (p, 0, 0))],
                  out_specs=pl.BlockSpec((tr, LANES), lambda p, i: (i, p)),
                  out_shape=jax.ShapeDtypeStruct((s, SGU_W), F32),
                  semantics=("parallel", "parallel"))(proj, proj, w, b_cols, gn)


def _sgu_bwd(proj, dmixed, w, w_t, b_cols, gn, *, name):
    s = proj.shape[0]
    tr = min(SGU_ROWS, s)
    ch = SGU_CHUNK
    cb_do = (SB_W + FOX_W) // LANES

    def body(u_ref, v_ref, do_ref, w_ref, wt_ref, b_ref, gn_ref,
             du_ref, dv_ref, dw_ref, db_ref, dgn_ref):
        @pl.when(pl.program_id(1) == 0)
        def _():
            dw_ref[...] = jnp.zeros_like(dw_ref)
            db_ref[...] = jnp.zeros_like(db_ref)
            dgn_ref[...] = jnp.zeros_like(dgn_ref)

        hm = _lane_masks()
        lo = hm[0]
        r, c = _tri_iotas(ch)
        wm = [jnp.where(c <= r, w_ref[h], 0.0).astype(BF16) for h in range(2)]
        wtm = [jnp.where(r <= c, wt_ref[h], 0.0).astype(BF16) for h in range(2)]
        bcol = [b_ref[0, :, h:h + 1] for h in range(2)]
        gnv = gn_ref[0]
        for n in range(tr // ch):
            rows = slice(n * ch, (n + 1) * ch)
            uc, vc, do = u_ref[rows, :], v_ref[rows, :], do_ref[rows, :]
            u, thu = _gelu(uc)
            vg, thv = _gelu(vc)
            rinv = lax.rsqrt(_group_mean(vg * vg, lo) + EPS)
            xh = vg * rinv
            vnb = (xh * gnv).astype(BF16)
            mix = _sgu_mix(wm, vnb, lo, bcol)
            du_ref[rows, :] = do * mix * _gelu_grad(uc, thu)
            dm = do * u
            dmb = dm.astype(BF16)
            dvn = jnp.where(lo, _dot(wtm[0], dmb), _dot(wtm[1], dmb))
            for h in range(2):
                dmh = jnp.where(hm[h], dm, 0.0)
                dw_ref[h] += jnp.where(c <= r, _dot_nt(dmh.astype(BF16), vnb), 0.0)
                db_ref[0, :, h:h + 1] += jnp.sum(dmh, axis=1, keepdims=True)
            dgn_ref[0] += _colsum(dvn * xh)
            dxh = dvn * gnv
            dvg = rinv * (dxh - xh * _group_mean(dxh * xh, lo))
            dv_ref[rows, :] = dvg * _gelu_grad(vc, thv)

    blk = lambda cb: pl.BlockSpec((tr, LANES), lambda p, i: (i, cb + p))
    w_spec = pl.BlockSpec((2, ch, ch), lambda p, i: (p, 0, 0))
    b_spec = pl.BlockSpec((1, ch, 2), lambda p, i: (p, 0, 0))
    g_spec = pl.BlockSpec((1, 1, LANES), lambda p, i: (p, 0, 0))
    out_blk = pl.BlockSpec((tr, LANES), lambda p, i: (i, p))
    return _pcall(body, name=name, grid=(SGU_W // LANES, s // tr),
                  in_specs=[blk(CB_UC), blk(CB_VC), blk(cb_do), w_spec, w_spec, b_spec, g_spec],
                  out_specs=[out_blk, out_blk, w_spec, b_spec, g_spec],
                  out_shape=[jax.ShapeDtypeStruct((s, SGU_W), F32)] * 2
                  + [jax.ShapeDtypeStruct(w.shape, F32), jax.ShapeDtypeStruct(b_cols.shape, F32),
                     jax.ShapeDtypeStruct(gn.shape, F32)],
                  semantics=("parallel", "arbitrary"))(proj, proj, dmixed, w, w_t, b_cols, gn)


def _pad_lanes(v):
    return jnp.zeros((1, LANES), F32).at[0, :v.shape[0]].set(v)


def _small_views(sm):
    return dict(
        n1=sm["norm1_g"][None, :], n2=sm["norm2_g"][None, :],
        b_pad=_pad_lanes(sm["b_forget"]),
        qg=jnp.tile(sm["q_norm_g"], 2)[None, :], kg=jnp.tile(sm["k_norm_g"], 2)[None, :],
        gn=sm["sgu_norm_g"].reshape(2, 1, LANES),
        w=sm["sgu_w"], w_t=jnp.swapaxes(sm["sgu_w"], 1, 2),
        b_cols=sm["sgu_b"].reshape(2, 2, SGU_CHUNK).transpose(0, 2, 1))


def _cf_rows(cf):
    return cf[:, :FOX_HEADS].T.reshape(FOX_W // LANES, 2, cf.shape[0])


def _layer_fwd(x_in, prev, mod, wts, sm, l, gathers=(), late_weights=None):
    sh1, sc1, g1, sh2, sc2, g2 = mod
    v = _small_views(sm)
    if prev is None:
        x0 = x_in
        h1 = _norm_mod_fwd(x0, v["n1"], sc1, sh1, name=f"l{l}_norm1")
    else:
        x0, h1 = _resid_norm_mod_fwd(x_in, prev[0], prev[1], v["n1"], sc1, sh1, name=f"l{l}_norm1")
    proj = _matmul(h1, wts["w_in"], name=f"l{l}_proj")
    o_sb, sb_ltot, sb_stop = _sb_fwd(proj, name=f"l{l}_sb_fwd")
    qn, kn, kmax = _fox_prep_fwd(proj, v["qg"], v["kg"], name=f"l{l}_fox_prep")
    cf = _forget_cumsum_fwd(proj, v["b_pad"], name=f"l{l}_cumf")
    cfr = _cf_rows(cf)
    o_fox, lse, *gathered = _fox_fwd(proj, qn, kn, cf, cfr, kmax, name=f"l{l}_fox_fwd",
                                     gathers=gathers)
    if gathers:
        late_weights(gathered)
    o_sgu = _sgu_fwd(proj, v["w"], v["b_cols"], v["gn"], name=f"l{l}_sgu_fwd")
    mixed = jnp.concatenate([o_sb, o_fox, o_sgu], axis=1).astype(BF16)
    mo = _matmul(mixed, wts["w_out"], name=f"l{l}_wout")
    x1, h2 = _resid_norm_mod_fwd(x0, mo, g1, v["n2"], sc2, sh2, name=f"l{l}_norm2")
    a, rr = _matmul(h2, wts["w1"], name=f"l{l}_mlp1", out_dtype=BF16, relu2=BF16)
    m2 = _matmul(rr, wts["w2"], name=f"l{l}_mlp2")
    saved = dict(x0=x0, h1=h1, proj=proj, sb_ltot=sb_ltot, sb_stop=sb_stop, qn=qn, kn=kn, kmax=kmax, cf=cf, cfr=cfr, o_fox=o_fox,
                 lse=lse, mixed=mixed, mo=mo, x1=x1, h2=h2, a=a, rr=rr, m2=m2)
    return saved


def _layer_bwd(dx2, dm2, dg2, sv, mod, wts, sm, l, below):
    sh1, sc1, g1, sh2, sc2, g2 = mod
    v = _small_views(sm)
    dw2 = _matmul(sv["rr"], dm2, ta=True, name=f"l{l}_dw2")
    da = _matmul(dm2, wts["w2"], tb=True, name=f"l{l}_da", out_dtype=BF16, pre_act=sv["a"])
    dw1 = _matmul(sv["h2"], da, ta=True, name=f"l{l}_dw1")
    dh2 = _matmul(da, wts["w1"], tb=True, name=f"l{l}_dh2")
    dx1, dn2, dsc2, dsh2, dmo, dg1 = _norm_mod_bwd(sv["x1"], dh2, dx2, v["n2"], sc2,
                                                    (sv["mo"], g1), name=f"l{l}_norm2_bwd")
    dwo = _matmul(sv["mixed"], dmo, ta=True, name=f"l{l}_dwout")
    dmixed = _matmul(dmo, wts["w_out"], tb=True, name=f"l{l}_dmixed")
    proj = sv["proj"]
    dqa, dka, dva = _sb_bwd(proj, dmixed, sv["sb_ltot"], sv["sb_stop"], name=f"l{l}_sb_bwd")
    dqn, dkn, dvb, dcfr, dcfq = _fox_bwd(proj, sv["qn"], sv["kn"], sv["cf"], sv["cfr"], sv["kmax"], dmixed,
                                   sv["o_fox"], sv["lse"], name=f"l{l}_fox_bwd")
    dqb, dkb, dqg, dkg = _fox_prep_bwd(proj, dqn, dkn, v["qg"], v["kg"], name=f"l{l}_fox_prep_bwd")
    s = proj.shape[0]
    dcf_heads = dcfr.reshape(FOX_HEADS, s).T + dcfq.reshape(s, FOX_HEADS, HEAD_DIM)[:, :, 0]
    dcf = jnp.zeros((s, LANES), F32).at[:, :FOX_HEADS].set(dcf_heads)
    dfl, dbf = _forget_cumsum_bwd(proj, v["b_pad"], dcf, name=f"l{l}_cumf_bwd")
    duc, dvc, dsw, dsb_cols, dgn = _sgu_bwd(proj, dmixed, v["w"], v["w_t"], v["b_cols"], v["gn"],
                                            name=f"l{l}_sgu_bwd")
    dproj = jnp.concatenate([dqa, dka, dva, dqb, dkb, dvb, duc, dvc, dfl,
                             jnp.zeros((s, LANES), F32)], axis=1).astype(BF16)
    dwin = _matmul(sv["h1"], dproj, ta=True, name=f"l{l}_dwin")
    dh1 = _matmul(dproj, wts["w_in"], tb=True, name=f"l{l}_dh1")
    dx0, dn1, dsc1, dsh1, dm_below, dg_below = _norm_mod_bwd(sv["x0"], dh1, dx1, v["n1"], sc1, below,
                                                             name=f"l{l}_norm1_bwd")
    big = dict(w_in=dwin, w_out=dwo, w1=dw1, w2=dw2)
    small = dict(norm1_g=dn1[0], norm2_g=dn2[0], b_forget=dbf[0, :FOX_HEADS],
                 q_norm_g=dqg[0, :HEAD_DIM] + dqg[0, HEAD_DIM:],
                 k_norm_g=dkg[0, :HEAD_DIM] + dkg[0, HEAD_DIM:],
                 sgu_norm_g=dgn.reshape(4, HEAD_DIM), sgu_w=dsw,
                 sgu_b=dsb_cols.transpose(0, 2, 1).reshape(4, SGU_CHUNK))
    dmod = jnp.concatenate([dsh1, dsc1, dg1, dsh2, dsc2, dg2], axis=1)
    return dx0, dm_below, dg_below, big, small, dmod


def _w_in_to_internal(w):
    pad = jnp.zeros((w.shape[0], PROJ_W - IN_W), w.dtype)
    return jnp.concatenate([w[:, :ATT_W], w[:, ATT_W + FOX_HEADS:], w[:, ATT_W:ATT_W + FOX_HEADS],
                            pad], axis=1)


def _w_in_from_internal(g):
    n_gate = SGU_W * 2
    return jnp.concatenate([g[:, :ATT_W], g[:, ATT_W + n_gate:ATT_W + n_gate + FOX_HEADS],
                            g[:, ATT_W:ATT_W + n_gate]], axis=1)


def _exchange(x, masks, slot_shift, slot_bits, scatter, *, name):
    n_slots = 2 ** slot_bits
    blk_shape = x.shape[1:] if scatter else x.shape
    n_peers = len(masks)

    def body(x_ref, out_ref, send_sems, recv_sems, local_sem):
        ids = (lax.axis_index("x"), lax.axis_index("y"), lax.axis_index("c"))
        me = 4 * ids[0] + 2 * ids[1] + ids[2]
        my_slot = (me >> slot_shift) & (n_slots - 1)

        def peer(mask):
            return tuple(1 - v if (mask >> b) & 1 else v for v, b in zip(ids, (2, 1, 0)))

        def src_for(slot):
            return x_ref.at[slot] if scatter else x_ref

        copies = [pltpu.make_async_copy(src_for(my_slot), out_ref.at[my_slot], local_sem)]
        for kk, mask in enumerate(masks):
            peer_slot = ((me ^ mask) >> slot_shift) & (n_slots - 1)
            copies.append(pltpu.make_async_remote_copy(
                src_ref=src_for(peer_slot), dst_ref=out_ref.at[my_slot],
                send_sem=send_sems.at[kk], recv_sem=recv_sems.at[kk],
                device_id=peer(mask), device_id_type=MESH))
        for cp in copies:
            cp.start()
        for cp in copies:
            cp.wait()

    any_spec = pl.BlockSpec(memory_space=pl.ANY)
    return _pcall(body, name=name, in_specs=[any_spec], out_specs=any_spec,
                  out_shape=jax.ShapeDtypeStruct((n_slots,) + tuple(blk_shape), x.dtype),
                  scratch_shapes=[pltpu.SemaphoreType.DMA((n_peers,)),
                                  pltpu.SemaphoreType.DMA((n_peers,)),
                                  pltpu.SemaphoreType.DMA(())])(x)


CORE_PIECE_BYTES = 12 * 2 ** 20
CORE_DMA_CHUNKS = 4


def _core_swap_piece(x, *, name):
    rows, cols = x.shape
    n_ch = CORE_DMA_CHUNKS if rows % (16 * CORE_DMA_CHUNKS) == 0 else 1
    rc = rows // n_ch

    def body(x_ref, out_ref, send_sems, recv_sems):
        sibling = (lax.axis_index("x"), lax.axis_index("y"), 1 - lax.axis_index("c"))
        copies = [pltpu.make_async_remote_copy(
            src_ref=x_ref.at[pl.ds(ch * rc, rc)], dst_ref=out_ref.at[pl.ds(ch * rc, rc)],
            send_sem=send_sems.at[ch], recv_sem=recv_sems.at[ch],
            device_id=sibling, device_id_type=MESH) for ch in range(n_ch)]
        for cp in copies:
            cp.start()
        for cp in copies:
            cp.wait()

    vmem = pl.BlockSpec(memory_space=pltpu.VMEM)
    return _pcall(body, name=name, in_specs=[vmem], out_specs=vmem,
                  out_shape=jax.ShapeDtypeStruct(x.shape, x.dtype),
                  scratch_shapes=[pltpu.SemaphoreType.DMA((n_ch,)),
                                  pltpu.SemaphoreType.DMA((n_ch,))])(x)


def _core_swap(x, *, name):
    rows, cols = x.shape
    n = 1
    while (rows % n or (rows // n) % 16 or
           (rows // n) * (-(-cols // LANES) * LANES) * x.dtype.itemsize > CORE_PIECE_BYTES):
        n += 1
    pr = rows // n
    pieces = [_core_swap_piece(x[kk * pr:(kk + 1) * pr], name=f"{name}_{kk}") for kk in range(n)]
    return pieces[0] if n == 1 else jnp.concatenate(pieces, axis=0)


def _by_core(core, mine, theirs, axis):
    return jnp.where(core == 0, jnp.concatenate([mine, theirs], axis=axis),
                     jnp.concatenate([theirs, mine], axis=axis))


def _gather_chips(x, *, name):
    return _exchange(x, (2, 4, 6), 1, 2, False, name=name)


def _gather_all(x, *, name):
    return _exchange(x, (1, 2, 3, 4, 5, 6, 7), 0, 3, False, name=name)


def _scatter_chips(x4, *, name):
    return _exchange(x4, (2, 4, 6), 1, 2, True, name=name)


def _sum_slots(parts, *, name, out_dtype=F32, tr=256):
    n, rows, cols = parts.shape
    tr = min(tr, rows)
    assert rows % tr == 0, (name, rows, tr)

    def body(p_ref, o_ref):
        acc = p_ref[0].astype(F32)
        for kk in range(1, n):
            acc = acc + p_ref[kk].astype(F32)
        o_ref[...] = acc.astype(o_ref.dtype)

    return _pcall(body, name=name, grid=(rows // tr,),
                  in_specs=[pl.BlockSpec((n, tr, cols), lambda i: (0, i, 0))],
                  out_specs=pl.BlockSpec((tr, cols), lambda i: (i, 0)),
                  out_shape=jax.ShapeDtypeStruct((rows, cols), out_dtype),
                  semantics=("parallel",))(parts)


def _add2(a, b, *, name, out_dtype, tr=512):
    def fn(f, v):
        return [f[0] + f[1]], []
    (out,), _ = _rowwise(fn, [a, b], [], [out_dtype], 0, name=name, tr=tr)
    return out


def _adamw(w, m, v, parts, *, name, tr=256):
    n, rows, cols = parts.shape
    tr = min(tr, rows)
    assert rows % tr == 0, (name, rows, tr)
    c1 = 1.0 - ADAM_B1 ** ADAM_STEP
    c2 = 1.0 - ADAM_B2 ** ADAM_STEP

    def body(w_ref, m_ref, v_ref, p_ref, g_ref, d_ref, nm_ref, nv_ref):
        g = p_ref[0]
        for kk in range(1, n):
            g = g + p_ref[kk]
        nm = ADAM_B1 * m_ref[...] + (1.0 - ADAM_B1) * g
        nv = ADAM_B2 * v_ref[...] + (1.0 - ADAM_B2) * (g * g)
        g_ref[...] = g
        nm_ref[...] = nm
        nv_ref[...] = nv
        d_ref[...] = -ADAM_LR * ((nm / c1) / (jnp.sqrt(nv / c2) + ADAM_EPS) + ADAM_WD * w_ref[...])

    spec = pl.BlockSpec((tr, cols), lambda i: (i, 0))
    return _pcall(body, name=name, grid=(rows // tr,),
                  in_specs=[spec, spec, spec, pl.BlockSpec((n, tr, cols), lambda i: (0, i, 0))],
                  out_specs=[spec] * 4,
                  out_shape=[jax.ShapeDtypeStruct((rows, cols), F32)] * 4,
                  semantics=("parallel",))(w, m, v, parts)


def _silu(c):
    return c / (1.0 + jnp.exp(-c))


def _ada_fwd(c_all, ada_w, ada_b_sh, *, name):
    nl, d, wsh = ada_w.shape

    def body(c_ref, w_ref, b_ref, o_ref):
        cond = _silu(c_ref[...]).astype(BF16)
        o_ref[0] = _dot(cond, w_ref[0].astype(BF16)) + b_ref[0]

    return _pcall(body, name=name, grid=(nl,),
                  in_specs=[pl.BlockSpec(c_all.shape, lambda l: (0, 0)),
                            pl.BlockSpec((1, d, wsh), lambda l: (l, 0, 0)),
                            pl.BlockSpec((1, 1, wsh), lambda l: (l, 0, 0))],
                  out_specs=pl.BlockSpec((1, c_all.shape[0], wsh), lambda l: (l, 0, 0)),
                  out_shape=jax.ShapeDtypeStruct((nl, c_all.shape[0], wsh), F32),
                  semantics=("parallel",))(c_all, ada_w, ada_b_sh)


def _ada_bwd(c_all, dmod_sh, *, name):
    nl, nb, wsh = dmod_sh.shape
    d = c_all.shape[1]

    def body(c_ref, dm_ref, o_ref):
        cond = _silu(c_ref[...]).astype(BF16)
        o_ref[0] = _dot_tn(cond, dm_ref[0].astype(BF16))

    return _pcall(body, name=name, grid=(nl,),
                  in_specs=[pl.BlockSpec(c_all.shape, lambda l: (0, 0)),
                            pl.BlockSpec((1, nb, wsh), lambda l: (l, 0, 0))],
                  out_specs=pl.BlockSpec((1, d, wsh), lambda l: (l, 0, 0)),
                  out_shape=jax.ShapeDtypeStruct((nl, d, wsh), F32),
                  semantics=("parallel",))(c_all, dmod_sh)


SMALL_NAMES = ("norm1_g", "norm2_g", "b_forget", "q_norm_g", "k_norm_g", "sgu_norm_g", "sgu_w",
               "sgu_b")
WEIGHT_NAMES = ("ada_w", "ada_b", "norm1_g", "norm2_g", "w_in", "b_forget", "q_norm_g", "k_norm_g",
                "sgu_norm_g", "sgu_w", "sgu_b", "w_out", "mlp_w1", "mlp_w2")


SMALL_TILE_ROWS = 256


def _pack_small(tree):
    flat = jnp.concatenate([tree[n].reshape(-1) for n in SMALL_NAMES])
    n = flat.shape[0]
    rows = -(-n // (SMALL_TILE_ROWS * LANES)) * SMALL_TILE_ROWS
    return jnp.zeros((rows * LANES,), F32).at[:n].set(flat).reshape(rows, LANES)


def _unpack_small(packed, like):
    flat = packed.reshape(-1)
    out, off = {}, 0
    for n in SMALL_NAMES:
        size = like[n].size
        out[n] = flat[off:off + size].reshape(like[n].shape)
        off += size
    return out


def kernel(x, c, ada_w, ada_b, norm1_g, norm2_g, w_in, b_forget, q_norm_g, k_norm_g, sgu_norm_g, sgu_w, sgu_b, w_out, mlp_w1, mlp_w2, loss_target, m_ada_w, m_ada_b, m_norm1_g, m_norm2_g, m_w_in, m_b_forget, m_q_norm_g, m_k_norm_g, m_sgu_norm_g, m_sgu_w, m_sgu_b, m_w_out, m_mlp_w1, m_mlp_w2, v_ada_w, v_ada_b, v_norm1_g, v_norm2_g, v_w_in, v_b_forget, v_q_norm_g, v_k_norm_g, v_sgu_norm_g, v_sgu_w, v_sgu_b, v_w_out, v_mlp_w1, v_mlp_w2):
    w = dict(ada_w=ada_w, ada_b=ada_b, norm1_g=norm1_g, norm2_g=norm2_g, w_in=w_in,
             b_forget=b_forget, q_norm_g=q_norm_g, k_norm_g=k_norm_g, sgu_norm_g=sgu_norm_g,
             sgu_w=sgu_w, sgu_b=sgu_b, w_out=w_out, mlp_w1=mlp_w1, mlp_w2=mlp_w2)
    mom = dict(ada_w=m_ada_w, ada_b=m_ada_b, norm1_g=m_norm1_g, norm2_g=m_norm2_g, w_in=m_w_in,
               b_forget=m_b_forget, q_norm_g=m_q_norm_g, k_norm_g=m_k_norm_g,
               sgu_norm_g=m_sgu_norm_g, sgu_w=m_sgu_w, sgu_b=m_sgu_b, w_out=m_w_out,
               mlp_w1=m_mlp_w1, mlp_w2=m_mlp_w2)
    var = dict(ada_w=v_ada_w, ada_b=v_ada_b, norm1_g=v_norm1_g, norm2_g=v_norm2_g, w_in=v_w_in,
               b_forget=v_b_forget, q_norm_g=v_q_norm_g, k_norm_g=v_k_norm_g,
               sgu_norm_g=v_sgu_norm_g, sgu_w=v_sgu_w, sgu_b=v_sgu_b, w_out=v_w_out,
               mlp_w1=v_mlp_w1, mlp_w2=v_mlp_w2)
    depth, d = norm1_g.shape
    chip = 2 * lax.axis_index("x") + lax.axis_index("y")
    me = 2 * chip + lax.axis_index("c")
    n_chips = 4
    ada_sh = ada_w.shape[2]

    core = lax.axis_index("c")
    half_l = depth // 2

    def my_part(w_sh):
        _, r, cols = w_sh.shape
        mine = lax.dynamic_slice_in_dim(w_sh, core * half_l, half_l, axis=0).astype(BF16)
        return mine.reshape(half_l * r, cols)

    def share(got, w_sh, name):
        _, r, cols = w_sh.shape
        theirs = _core_swap(got.reshape(n_chips * half_l * r, cols), name=f"share_{name}")
        return _by_core(core, got.reshape(n_chips, half_l, r, cols),
                        theirs.reshape(n_chips, half_l, r, cols), 1)

    g_in = share(_gather_chips(my_part(w_in), name="gather_w_in"), w_in, "w_in")
    layer_w = [dict(w_in=_w_in_to_internal(
        jnp.concatenate([g_in[k, l] for k in range(n_chips)], axis=1))) for l in range(depth)]
    later = (("w_out", w_out), ("w1", mlp_w1), ("w2", mlp_w2))

    def late_weights(gathered):
        g_out, g_w1, g_w2 = [share(got, w_sh, name) for got, (name, w_sh) in zip(gathered, later)]
        for l in range(depth):
            layer_w[l].update(
                w_out=g_out[:, l].reshape(d, d),
                w1=jnp.concatenate([g_w1[k, l] for k in range(n_chips)], axis=1),
                w2=g_w2[:, l].reshape(D_FF, d))

    c_all = _gather_all(jnp.zeros((8, d), F32).at[0].set(c[0]), name="gather_c")[:, 0]
    c_pad = jnp.concatenate([c_all, jnp.zeros_like(c_all)], axis=0)
    ada_b_sh = lax.dynamic_slice_in_dim(ada_b, chip * ada_sh, ada_sh, axis=1)[:, None, :]
    mod_sh = _ada_fwd(c_pad, ada_w, ada_b_sh, name="ada_fwd")
    mod_all = _gather_chips(mod_sh, name="gather_mod")
    mod_me = lax.dynamic_index_in_dim(mod_all, me, axis=2, keepdims=False)
    mod_me = mod_me.transpose(1, 0, 2).reshape(depth, 6, 1, d)

    saved = []
    xs, prev = x[0], None
    for l in range(depth):
        mod = [mod_me[l, kk] for kk in range(6)]
        sm = {n: w[n][l] for n in SMALL_NAMES}
        first = dict(gathers=[my_part(w_sh) for _, w_sh in later], late_weights=late_weights)
        sv = _layer_fwd(xs, prev, mod, layer_w[l], sm, l, **(first if l == 0 else {}))
        saved.append(sv)
        xs, prev = sv["x1"], (sv["m2"], mod[5])

    sq, dxs, dm2, dg2 = _loss_fwd_bwd(xs, prev[0], prev[1], loss_target[0], name="loss")
    loss = lax.psum(0.5 * jnp.sum(sq) / d, ("x", "y", "c"))

    big = {n: [] for n in ("w_in", "w_out", "w1", "w2")}
    small = {n: [] for n in SMALL_NAMES}
    dmods = []
    for l in reversed(range(depth)):
        mod = [mod_me[l, kk] for kk in range(6)]
        sm = {n: w[n][l] for n in SMALL_NAMES}
        below = (saved[l - 1]["m2"], mod_me[l - 1, 5]) if l else None
        dxs, dm2, dg2, bg, smg, dmod = _layer_bwd(dxs, dm2, dg2, saved[l], mod, layer_w[l], sm, l,
                                                  below)
        for n in big:
            big[n].insert(0, bg[n])
        for n in SMALL_NAMES:
            small[n].insert(0, smg[n])
        dmods.insert(0, dmod)
    grad_x = dxs[None]

    out_g, out_d, out_m, out_v = {}, {}, {}, {}

    def run_adamw(name, parts2d, shape):
        rows, cols = parts2d.shape[1:]
        g, dl, nm, nv = _adamw(w[name].reshape(rows, cols), mom[name].reshape(rows, cols),
                               var[name].reshape(rows, cols), parts2d, name=f"adamw_{name}")
        out_g[name], out_d[name] = g.reshape(shape), dl.reshape(shape)
        out_m[name], out_v[name] = nm.reshape(shape), nv.reshape(shape)

    def shards_of(name, l):
        if name == "w_in":
            g = _w_in_from_internal(big["w_in"][l])
            return jnp.stack(jnp.split(g, n_chips, axis=1))
        if name == "mlp_w1":
            return jnp.stack(jnp.split(big["w1"][l], n_chips, axis=1))
        if name == "w_out":
            return big["w_out"][l].reshape(n_chips, d // n_chips, d)
        return big["w2"][l].reshape(n_chips, D_FF // n_chips, d)

    for name in ("w_in", "w_out", "mlp_w1", "mlp_w2"):
        per_chip = jnp.stack([shards_of(name, l) for l in range(depth)], axis=1)
        r, cols = per_chip.shape[2:]
        half_rows = half_l * r
        keep = lax.dynamic_slice_in_dim(per_chip, core * half_l, half_l, axis=1)
        send = lax.dynamic_slice_in_dim(per_chip, (1 - core) * half_l, half_l, axis=1)
        theirs = _core_swap(send.reshape(n_chips * half_rows, cols), name=f"pair_{name}")
        chip_sum = _add2(keep.reshape(n_chips * half_rows, cols), theirs, out_dtype=BF16,
                         name=f"pairsum_{name}")
        got = _scatter_chips(chip_sum.reshape(n_chips, half_rows, cols), name=f"scatter_{name}")
        half = _sum_slots(got, name=f"sum_{name}")
        both = _by_core(core, half, _core_swap(half, name=f"swap_{name}"), 0)
        run_adamw(name, both[None], w[name].shape)

    small_tree = {n: jnp.stack(small[n]) for n in SMALL_NAMES}
    gathered = _gather_all(_pack_small(small_tree), name="gather_small")
    gs, ds_, ms, vs = _adamw(_pack_small({n: w[n] for n in SMALL_NAMES}),
                             _pack_small({n: mom[n] for n in SMALL_NAMES}),
                             _pack_small({n: var[n] for n in SMALL_NAMES}), gathered,
                             name="adamw_small")
    like = {n: w[n] for n in SMALL_NAMES}
    for tree, packed in ((out_g, gs), (out_d, ds_), (out_m, ms), (out_v, vs)):
        tree.update(_unpack_small(packed, like))

    dmod_mine = jnp.concatenate(dmods, axis=0)
    dmod_all = _gather_all(jnp.zeros((depth, 8, 6 * d), F32).at[:, 0].set(dmod_mine),
                           name="gather_dmod")[:, :, 0]
    dmod_lb = dmod_all.transpose(1, 0, 2)
    dmod_sh = lax.dynamic_slice_in_dim(dmod_lb, chip * ada_sh, ada_sh, axis=2)
    dmod_sh = jnp.concatenate([dmod_sh, jnp.zeros_like(dmod_sh)], axis=1)
    g_ada_w = _ada_bwd(c_pad, dmod_sh, name="ada_bwd")
    run_adamw("ada_w", g_ada_w.reshape(1, depth * d, ada_sh), ada_w.shape)
    parts_b = dmod_all.reshape(8, depth * 6 * d // LANES, LANES)
    run_adamw("ada_b", parts_b, ada_b.shape)

    outs = [loss, grad_x]
    for tree in (out_g, out_d, out_m, out_v):
        outs += [tree[n] for n in WEIGHT_NAMES]
    return tuple(outs)
```

```python
import functools
import math

import jax
import jax.numpy as jnp
from jax import lax
from jax.experimental import pallas as pl
from jax.experimental.pallas import tpu as pltpu

F32 = jnp.float32
BF16 = jnp.bfloat16

D_MODEL = 1024
DEPTH = 4
HEAD_DIM = 64
LANES = 128
D_FF = 4 * D_MODEL
EPS = 1e-6
SB_W, FOX_W, SGU_W = 256, 512, 256
FOX_HEADS = 8
SGU_CHUNK = 128
IN_W = 2824
ATT_W = 3 * SB_W + 3 * FOX_W
PROJ_W = 3072
CB_QA, CB_KA, CB_VA = 0, 2, 4
CB_QB, CB_KB, CB_VB = 6, 10, 14
CB_UC, CB_VC, CB_FL = 18, 20, 22
ATT_T = 256
PREP_ROWS = 2048
SGU_ROWS = 2048
NORM_ROWS = 512
NORM_BWD_ROWS = 512
CUMSUM_ROWS = 512
VMEM_LIMIT = 56 * 2 ** 20
SKIP_LOG = 110.0

ADAM_LR, ADAM_B1, ADAM_B2, ADAM_EPS, ADAM_WD, ADAM_STEP = 0.001, 0.9, 0.999, 1e-08, 0.01, 10

MESH = pl.DeviceIdType.MESH


def _pcall(body, *, name, out_shape, grid=(), in_specs=None, out_specs=None, scratch_shapes=(),
           semantics=None):
    params = dict(vmem_limit_bytes=VMEM_LIMIT)
    if semantics is not None:
        params["dimension_semantics"] = semantics
    kwargs = {}
    if in_specs is not None:
        kwargs["in_specs"] = in_specs
    if out_specs is not None:
        kwargs["out_specs"] = out_specs
    return pl.pallas_call(body, name=name, out_shape=out_shape, grid=grid,
                          scratch_shapes=list(scratch_shapes),
                          compiler_params=pltpu.CompilerParams(**params), **kwargs)


def _dot(a, b):
    return jnp.dot(a, b, preferred_element_type=F32)


def _dot_nt(a, b):
    return lax.dot_general(a, b, (((1,), (1,)), ((), ())), preferred_element_type=F32)


def _dot_tn(a, b):
    return lax.dot_general(a, b, (((0,), (0,)), ((), ())), preferred_element_type=F32)


def _split2(x):
    hi = x.astype(BF16)
    lo = (x - hi.astype(F32)).astype(BF16)
    return hi, lo


def _ones_dot(x, ones_bf16):
    hi, lo = _split2(x)
    return _dot(hi, ones_bf16) + _dot(lo, ones_bf16)


def _rowwise(fn, fulls, vecs, out_dtypes, n_vec_out, *, name, tr):
    s, n = fulls[0].shape
    tr = min(tr, s)
    assert s % tr == 0, (name, s, tr)
    nf, nv, nfo = len(fulls), len(vecs), len(out_dtypes)

    def body(*refs):
        fi, vi = refs[:nf], refs[nf:nf + nv]
        fo, vo = refs[nf + nv:nf + nv + nfo], refs[nf + nv + nfo:]
        outs_f, outs_v = fn([r[...] for r in fi], [r[...] for r in vi])
        for r, o in zip(fo, outs_f):
            r[...] = o.astype(r.dtype)
        if n_vec_out:
            @pl.when(pl.program_id(0) == 0)
            def _():
                for r in vo:
                    r[...] = jnp.zeros_like(r)
            for r, o in zip(vo, outs_v):
                r[...] += o

    full_spec = pl.BlockSpec((tr, n), lambda i: (i, 0))
    vec_specs = [pl.BlockSpec(v.shape, lambda i: (0, 0)) for v in vecs]
    out_vec_spec = pl.BlockSpec((1, n), lambda i: (0, 0))
    out_shape = [jax.ShapeDtypeStruct((s, n), dt) for dt in out_dtypes]
    out_shape += [jax.ShapeDtypeStruct((1, n), F32)] * n_vec_out
    outs = _pcall(body, name=name, grid=(s // tr,),
                  in_specs=[full_spec] * nf + vec_specs,
                  out_specs=[full_spec] * nfo + [out_vec_spec] * n_vec_out,
                  out_shape=out_shape,
                  semantics=("arbitrary",) if n_vec_out else ("parallel",))(*fulls, *vecs)
    return outs[:nfo], outs[nfo:]


def _colsum(t):
    return jnp.sum(t, axis=0, keepdims=True)


def _rms_mod(x, g, sc, sh):
    r = lax.rsqrt(jnp.mean(x * x, axis=-1, keepdims=True) + EPS)
    return (x * r * g) * (1.0 + sc) + sh


def _norm_mod_fwd(x, g, sc, sh, *, name):
    def fn(f, v):
        return [_rms_mod(f[0], v[0], v[1], v[2])], []
    (h,), _ = _rowwise(fn, [x], [g, sc, sh], [BF16], 0, name=name, tr=NORM_ROWS)
    return h


def _resid_norm_mod_fwd(x, m, gate, g, sc, sh, *, name):
    def fn(f, v):
        xn = f[0] + v[0] * f[1]
        return [xn, _rms_mod(xn, v[1], v[2], v[3])], []
    (xn, h), _ = _rowwise(fn, [x, m], [gate, g, sc, sh], [F32, BF16], 0, name=name, tr=NORM_ROWS)
    return xn, h


def _norm_mod_bwd(x, dh, dres, g, sc, gated, *, name):
    def fn(f, v):
        xv, dhv, dr = f[:3]
        gv, scv = v[:2]
        r = lax.rsqrt(jnp.mean(xv * xv, axis=-1, keepdims=True) + EPS)
        xh = xv * r
        dn = dhv * (1.0 + scv)
        dxh = dn * gv
        dx = dr + r * (dxh - xh * jnp.mean(dxh * xh, axis=-1, keepdims=True))
        sums = [_colsum(dn * xh), _colsum(dhv * (xh * gv)), _colsum(dhv)]
        if gated is None:
            return [dx], sums
        return [dx, dx * v[2]], sums + [_colsum(dx * f[3])]
    if gated is None:
        (dx,), (dg, dsc, dsh) = _rowwise(fn, [x, dh, dres], [g, sc], [F32], 3, name=name,
                                         tr=NORM_BWD_ROWS)
        return dx, dg, dsc, dsh, None, None
    (dx, dm), (dg, dsc, dsh, dgate) = _rowwise(fn, [x, dh, dres, gated[0]], [g, sc, gated[1]],
                                               [F32, BF16], 4, name=name, tr=NORM_BWD_ROWS)
    return dx, dg, dsc, dsh, dm, dgate


def _loss_fwd_bwd(x, m, gate, target, *, name):
    n = x.shape[1]

    def fn(f, v):
        err = f[0] + v[0] * f[1] - f[2]
        dy = err * (1.0 / n)
        return [dy, dy * v[0]], [_colsum(err * err), _colsum(dy * f[1])]
    (dy, dm), (sq, dgate) = _rowwise(fn, [x, m, target], [gate], [F32, BF16], 2, name=name,
                                     tr=NORM_BWD_ROWS)
    return sq, dy, dm, dgate


def _matmul(a, b, *, name, ta=False, tb=False, out_dtype=F32, relu2=None, pre_act=None,
            tm=1024, tn=1024, tk_max=2048):
    m = a.shape[1] if ta else a.shape[0]
    k = a.shape[0] if ta else a.shape[1]
    n = b.shape[0] if tb else b.shape[1]
    assert k == (b.shape[1] if tb else b.shape[0])
    tk = max(dd for dd in range(LANES, min(tk_max, k) + 1, LANES) if k % dd == 0) if k > LANES else k
    tm, tn = min(tm, m), min(tn, n)
    assert m % tm == 0 and n % tn == 0 and k % tk == 0, (name, m, n, k)
    nk = k // tk
    dims = (((0 if ta else 1,), (1 if tb else 0,)), ((), ()))

    plain = relu2 is None and pre_act is None
    in_place = plain and out_dtype == F32
    n_in = 2 + (pre_act is not None)

    def body(*refs):
        a_ref, b_ref = refs[:2]
        o_ref = refs[n_in]
        prod = lax.dot_general(a_ref[...].astype(BF16), b_ref[...].astype(BF16), dims,
                               preferred_element_type=F32)

        def finish(acc):
            if pre_act is not None:
                acc = acc * (2.0 * jnp.maximum(refs[2][...].astype(F32), 0.0))
            o_ref[...] = acc.astype(o_ref.dtype)
            if relu2 is not None:
                r = jnp.maximum(acc, 0.0)
                refs[n_in + 1][...] = (r * r).astype(relu2)

        if nk == 1:
            finish(prod)
            return
        kk = pl.program_id(2)
        acc_ref = o_ref if in_place else refs[-1]

        @pl.when(kk == 0)
        def _():
            acc_ref[...] = prod

        @pl.when(kk > 0)
        def _():
            acc_ref[...] += prod

        if not in_place:
            @pl.when(kk == nk - 1)
            def _():
                finish(acc_ref[...])

    a_spec = (pl.BlockSpec((tk, tm), lambda i, j, kk: (kk, i)) if ta
              else pl.BlockSpec((tm, tk), lambda i, j, kk: (i, kk)))
    b_spec = (pl.BlockSpec((tn, tk), lambda i, j, kk: (j, kk)) if tb
              else pl.BlockSpec((tk, tn), lambda i, j, kk: (kk, j)))
    out_spec = pl.BlockSpec((tm, tn), lambda i, j, kk: (i, j))
    in_specs, args = [a_spec, b_spec], [a, b]
    if pre_act is not None:
        in_specs.append(out_spec)
        args.append(pre_act)
    out_specs, out_shape = out_spec, jax.ShapeDtypeStruct((m, n), out_dtype)
    if relu2 is not None:
        out_specs, out_shape = [out_spec] * 2, [out_shape, jax.ShapeDtypeStruct((m, n), relu2)]
    return _pcall(body, name=name, grid=(m // tm, n // tn, nk),
                  in_specs=in_specs, out_specs=out_specs, out_shape=out_shape,
                  scratch_shapes=[] if nk == 1 or in_place else [pltpu.VMEM((tm, tn), F32)],
                  semantics=("parallel", "parallel", "arbitrary"))(*args)


def _lane_masks():
    lane = lax.broadcasted_iota(jnp.int32, (1, LANES), 1)
    return [lane < HEAD_DIM, lane >= HEAD_DIM]


def _tri_iotas(t):
    r = lax.broadcasted_iota(jnp.int32, (t, t), 0)
    c = lax.broadcasted_iota(jnp.int32, (t, t), 1)
    return r, c


def _rows(j, t):
    return pl.ds(pl.multiple_of(j * t, t), t)


def _neg_softplus(z):
    e = jnp.exp(-jnp.abs(z))
    return -(jnp.maximum(z, 0.0) + jnp.log(1.0 + e)), e


def _sb_fwd(proj, *, name):
    s = proj.shape[0]
    t = min(ATT_T, s)
    scale = HEAD_DIM ** -0.5

    def body(q_ref, k_ref, v_ref, o_ref, ltot_ref, stop_ref):
        i = pl.program_id(1)
        hm = _lane_masks()
        q = q_ref[...] * scale
        qh = [jnp.where(mk, q, 0.0).astype(BF16) for mk in hm]
        r, c = _tri_iotas(t)
        later = (r > c).astype(BF16)
        q2 = jnp.concatenate(qh, axis=0)
        causal2 = jnp.concatenate([c < r, c < r], axis=0)

        def scores(j):
            return _dot_nt(q2, k_ref[_rows(j, t), :].astype(BF16))

        def chunk(j, carry, z, masked):
            e_run, acc = carry
            vb = v_ref[_rows(j, t), :].astype(BF16)
            l, _ = _neg_softplus(z)
            if masked:
                l = jnp.where(causal2, l, 0.0)
            between = _ones_dot(l, later) + e_run
            a = jnp.exp(z + l + between)
            if masked:
                a = jnp.where(causal2, a, 0.0)
            return e_run + jnp.sum(l, axis=1, keepdims=True), acc + _dot(a.astype(BF16), vb)

        init = (jnp.zeros((2 * t, 1), F32), jnp.zeros((2 * t, LANES), F32))
        carry = chunk(i, init, scores(i), True)

        def step(st):
            j, cr, z = st
            z_next = scores(jnp.maximum(j - 1, 0))
            return j - 1, chunk(j, cr, z, False), z_next

        j_stop, (e_tot, acc), _ = lax.while_loop(
            lambda st: (st[0] >= 0) & (jnp.max(st[1][0]) > -SKIP_LOG), step,
            (i - 1, carry, scores(jnp.maximum(i - 1, 0))))
        o_ref[...] = jnp.where(hm[0], acc[:t], acc[t:])
        ltot_ref[...] = jnp.where(hm[0], e_tot[:t], e_tot[t:])
        stop_ref[...] = jnp.full(stop_ref.shape, j_stop.astype(F32), F32)

    blk = lambda cb: pl.BlockSpec((t, LANES), lambda p, i: (i, cb + p))
    full = lambda cb: pl.BlockSpec((s, LANES), lambda p, i: (0, cb + p))
    out_blk = pl.BlockSpec((t, LANES), lambda p, i: (i, p))
    n_pairs = SB_W // LANES
    return _pcall(body, name=name, grid=(n_pairs, s // t),
                  in_specs=[blk(CB_QA), full(CB_KA), full(CB_VA)],
                  out_specs=[out_blk, out_blk,
                             pl.BlockSpec((1, 1, 8, LANES), lambda p, i: (p, i, 0, 0))],
                  out_shape=[jax.ShapeDtypeStruct((s, SB_W), F32)] * 2
                  + [jax.ShapeDtypeStruct((n_pairs, s // t, 8, LANES), F32)],
                  semantics=("parallel", "arbitrary"))(proj, proj, proj)


def _sb_bwd(proj, dmixed, ltot, stop, *, name):
    s = proj.shape[0]
    t = min(ATT_T, s)
    scale = HEAD_DIM ** -0.5

    def body(q_ref, k_ref, v_ref, do_ref, ltot_ref, stop_ref, dq_ref, dk_ref, dv_ref):
        i = pl.program_id(1)

        @pl.when(i == 0)
        def _():
            dk_ref[...] = jnp.zeros_like(dk_ref)
            dv_ref[...] = jnp.zeros_like(dv_ref)

        hm = _lane_masks()
        q = q_ref[...] * scale
        do = do_ref[...]
        qh = [jnp.where(mk, q, 0.0).astype(BF16) for mk in hm]
        doh = [jnp.where(mk, do, 0.0).astype(BF16) for mk in hm]
        r, c = _tri_iotas(t)
        upto = (r <= c).astype(BF16)
        before = (r < c).astype(BF16)
        q2 = jnp.concatenate(qh, axis=0)
        do2 = jnp.concatenate(doh, axis=0)
        causal2 = jnp.concatenate([c < r, c < r], axis=0)

        j_stop = jnp.clip(jnp.max(stop_ref[...]).astype(jnp.int32), -1, i - 1)
        ltv = ltot_ref[...]
        lt = jnp.concatenate([ltv[:, 0:1], ltv[:, HEAD_DIM:HEAD_DIM + 1]], axis=0)

        def products(j):
            return (_dot_nt(q2, k_ref[_rows(j, t), :].astype(BF16)),
                    _dot_nt(do2, v_ref[_rows(j, t), :].astype(BF16)))

        def chunk(j, carry, z, da, masked):
            l_run, g_run, dq = carry
            l, e = _neg_softplus(z)
            beta = jnp.where(z >= 0.0, 1.0, e) / (1.0 + e)
            if masked:
                l = jnp.where(causal2, l, 0.0)
            prefix = _ones_dot(l, upto) + l_run
            a = jnp.exp(z + l + (lt - prefix))
            if masked:
                a = jnp.where(causal2, a, 0.0)
            g = a * da
            g_before = _ones_dot(g, before) + g_run
            dz = g * (1.0 - beta) - beta * g_before
            if masked:
                dz = jnp.where(causal2, dz, 0.0)
            dzb = dz.astype(BF16)
            dk_ref[_rows(j, t), :] += _dot_tn(dzb, q2)
            dv_ref[_rows(j, t), :] += _dot_tn(a.astype(BF16), do2)
            return (l_run + jnp.sum(l, axis=1, keepdims=True),
                    g_run + jnp.sum(g, axis=1, keepdims=True),
                    dq + _dot(dzb, k_ref[_rows(j, t), :].astype(BF16)))

        init = (jnp.zeros((2 * t, 1), F32), jnp.zeros((2 * t, 1), F32),
                jnp.zeros((2 * t, LANES), F32))
        carry = lax.fori_loop(j_stop + 1, i,
                              lambda j, cr: chunk(j, cr, *products(j), False), init)
        dq2 = chunk(i, carry, *products(i), True)[2]
        dq_ref[...] = jnp.where(hm[0], dq2[:t], dq2[t:]) * scale

    blk = lambda cb: pl.BlockSpec((t, LANES), lambda p, i: (i, cb + p))
    full = lambda cb: pl.BlockSpec((s, LANES), lambda p, i: (0, cb + p))
    out_blk = pl.BlockSpec((t, LANES), lambda p, i: (i, p))
    out_full = pl.BlockSpec((s, LANES), lambda p, i: (0, p))
    return _pcall(body, name=name, grid=(SB_W // LANES, s // t),
                  in_specs=[blk(CB_QA), full(CB_KA), full(CB_VA), blk(0), out_blk,
                            pl.BlockSpec((1, 1, 8, LANES), lambda p, i: (p, i, 0, 0))],
                  out_specs=[out_blk, out_full, out_full],
                  out_shape=[jax.ShapeDtypeStruct((s, SB_W), F32)] * 3,
                  semantics=("parallel", "arbitrary"))(proj, proj, proj, dmixed, ltot, stop)


def _group_mean(v, lo):
    s0 = jnp.sum(jnp.where(lo, v, 0.0), axis=1, keepdims=True)
    s1 = jnp.sum(jnp.where(lo, 0.0, v), axis=1, keepdims=True)
    return jnp.where(lo, s0, s1) * (1.0 / HEAD_DIM)


def _fox_prep_fwd(proj, qg, kg, *, name):
    s = proj.shape[0]
    tr = min(PREP_ROWS, s)

    def body(q_ref, k_ref, qg_ref, kg_ref, qn_ref, kn_ref, kmax_ref):
        lo = _lane_masks()[0]
        for x_ref, g_ref, o_ref in ((q_ref, qg_ref, qn_ref), (k_ref, kg_ref, kn_ref)):
            x = x_ref[...]
            o_ref[...] = x * lax.rsqrt(_group_mean(x * x, lo) + EPS) * g_ref[...]

        @pl.when(pl.program_id(1) == 0)
        def _():
            kmax_ref[...] = jnp.zeros_like(kmax_ref)
        kn = kn_ref[...]
        norms = jnp.sqrt(_group_mean(kn * kn, lo) * HEAD_DIM)
        kmax_ref[...] = jnp.maximum(kmax_ref[...], jnp.max(norms, axis=0, keepdims=True))

    blk = lambda cb: pl.BlockSpec((tr, LANES), lambda p, i: (i, cb + p))
    vec = pl.BlockSpec((1, LANES), lambda p, i: (0, 0))
    out_blk = pl.BlockSpec((tr, LANES), lambda p, i: (i, p))
    return _pcall(body, name=name, grid=(FOX_W // LANES, s // tr),
                  in_specs=[blk(CB_QB), blk(CB_KB), vec, vec],
                  out_specs=[out_blk, out_blk, pl.BlockSpec((1, LANES), lambda p, i: (0, p))],
                  out_shape=[jax.ShapeDtypeStruct((s, FOX_W), F32)] * 2
                  + [jax.ShapeDtypeStruct((1, FOX_W), F32)],
                  semantics=("parallel", "arbitrary"))(proj, proj, qg, kg)


def _fox_prep_bwd(proj, dqn, dkn, qg, kg, *, name):
    s = proj.shape[0]
    tr = min(PREP_ROWS, s)

    def body(q_ref, k_ref, dqn_ref, dkn_ref, qg_ref, kg_ref, dq_ref, dk_ref, dqg_ref, dkg_ref):
        @pl.when((pl.program_id(0) == 0) & (pl.program_id(1) == 0))
        def _():
            dqg_ref[...] = jnp.zeros_like(dqg_ref)
            dkg_ref[...] = jnp.zeros_like(dkg_ref)

        lo = _lane_masks()[0]
        for x_ref, dy_ref, g_ref, dx_ref, dg_ref in ((q_ref, dqn_ref, qg_ref, dq_ref, dqg_ref),
                                                     (k_ref, dkn_ref, kg_ref, dk_ref, dkg_ref)):
            x, dy = x_ref[...], dy_ref[...]
            r = lax.rsqrt(_group_mean(x * x, lo) + EPS)
            xh = x * r
            dxh = dy * g_ref[...]
            dx_ref[...] = r * (dxh - xh * _group_mean(dxh * xh, lo))
            dg_ref[...] += _colsum(dy * xh)

    blk = lambda cb: pl.BlockSpec((tr, LANES), lambda p, i: (i, cb + p))
    vec = pl.BlockSpec((1, LANES), lambda p, i: (0, 0))
    out_blk = pl.BlockSpec((tr, LANES), lambda p, i: (i, p))
    return _pcall(body, name=name, grid=(FOX_W // LANES, s // tr),
                  in_specs=[blk(CB_QB), blk(CB_KB), out_blk, out_blk, vec, vec],
                  out_specs=[out_blk, out_blk, vec, vec],
                  out_shape=[jax.ShapeDtypeStruct((s, FOX_W), F32)] * 2
                  + [jax.ShapeDtypeStruct((1, LANES), F32)] * 2,
                  semantics=("arbitrary", "arbitrary"))(proj, proj, dqn, dkn, qg, kg)


def _split3_dot(tri_bf16, x):
    hi = x.astype(BF16)
    r1 = x - hi.astype(F32)
    mid = r1.astype(BF16)
    lo = (r1 - mid.astype(F32)).astype(BF16)
    return _dot(tri_bf16, hi) + _dot(tri_bf16, mid) + _dot(tri_bf16, lo)


def _forget_cumsum_fwd(proj, b_pad, *, name):
    s = proj.shape[0]
    tb = min(CUMSUM_ROWS, s)

    def body(fl_ref, b_ref, cf_ref, run_ref):
        @pl.when(pl.program_id(0) == 0)
        def _():
            run_ref[...] = jnp.zeros_like(run_ref)
        lf, _ = _neg_softplus(-(fl_ref[...] + b_ref[...]))
        r, c = _tri_iotas(tb)
        incl = _split3_dot((c <= r).astype(BF16), lf) + run_ref[...]
        cf_ref[...] = incl
        run_ref[...] = incl[tb - 1:tb, :]

    return _pcall(body, name=name, grid=(s // tb,),
                  in_specs=[pl.BlockSpec((tb, LANES), lambda i: (i, CB_FL)),
                            pl.BlockSpec((1, LANES), lambda i: (0, 0))],
                  out_specs=pl.BlockSpec((tb, LANES), lambda i: (i, 0)),
                  out_shape=jax.ShapeDtypeStruct((s, LANES), F32),
                  scratch_shapes=[pltpu.VMEM((1, LANES), F32)],
                  semantics=("arbitrary",))(proj, b_pad)


def _forget_cumsum_bwd(proj, b_pad, dcf, *, name):
    s = proj.shape[0]
    tb = min(CUMSUM_ROWS, s)
    nb = s // tb

    def body(fl_ref, b_ref, dcf_ref, dfl_ref, db_ref, run_ref):
        @pl.when(pl.program_id(0) == 0)
        def _():
            run_ref[...] = jnp.zeros_like(run_ref)
            db_ref[...] = jnp.zeros_like(db_ref)
        r, c = _tri_iotas(tb)
        dlf = _split3_dot((c >= r).astype(BF16), dcf_ref[...]) + run_ref[...]
        run_ref[...] = dlf[0:1, :]
        xv = fl_ref[...] + b_ref[...]
        e = jnp.exp(-jnp.abs(xv))
        sig_neg = jnp.where(xv >= 0.0, e, 1.0) / (1.0 + e)
        dfl = dlf * sig_neg
        dfl_ref[...] = dfl
        db_ref[...] += _colsum(dfl)

    return _pcall(body, name=name, grid=(nb,),
                  in_specs=[pl.BlockSpec((tb, LANES), lambda i: (nb - 1 - i, CB_FL)),
                            pl.BlockSpec((1, LANES), lambda i: (0, 0)),
                            pl.BlockSpec((tb, LANES), lambda i: (nb - 1 - i, 0))],
                  out_specs=[pl.BlockSpec((tb, LANES), lambda i: (nb - 1 - i, 0)),
                             pl.BlockSpec((1, LANES), lambda i: (0, 0))],
                  out_shape=[jax.ShapeDtypeStruct((s, LANES), F32),
                             jax.ShapeDtypeStruct((1, LANES), F32)],
                  scratch_shapes=[pltpu.VMEM((1, LANES), F32)],
                  semantics=("arbitrary",))(proj, b_pad, dcf)


def _fox_bias_q(cfc, p, h):
    lane = lax.broadcasted_iota(jnp.int32, (1, LANES), 1)
    return jnp.sum(jnp.where(lane == 2 * p + h, cfc, 0.0), axis=1, keepdims=True)


def _fox_score_bound(q, kmax_row, hm):
    out = []
    for h in range(2):
        qnorm = jnp.sqrt(jnp.sum(jnp.where(hm[h], q * q, 0.0), axis=1, keepdims=True))
        out.append(1.02 * qnorm * kmax_row[:, h * HEAD_DIM:h * HEAD_DIM + 1])
    return out


def _fox_live(cfr_ref, j, t, tops):
    jc = jnp.maximum(j, 0)
    worst = []
    for h in range(2):
        cf_min = jnp.min(cfr_ref[0, pl.ds(h, 1), _rows(jc, t)], axis=1, keepdims=True)
        worst.append(jnp.max(tops[h] - cf_min))
    return (j >= 0) & (jnp.maximum(worst[0], worst[1]) > -SKIP_LOG)


def _chip_gather_copies(x_refs, out_refs, send_sems, recv_sems, local_sems):
    ids = (lax.axis_index("x"), lax.axis_index("y"), lax.axis_index("c"))
    chip = 2 * ids[0] + ids[1]
    copies = []
    for n, (x_ref, out_ref) in enumerate(zip(x_refs, out_refs)):
        copies.append(pltpu.make_async_copy(x_ref, out_ref.at[chip], local_sems.at[n]))
        for kk, (flip_x, flip_y) in enumerate(((1, 0), (0, 1), (1, 1))):
            peer = (1 - ids[0] if flip_x else ids[0], 1 - ids[1] if flip_y else ids[1], ids[2])
            copies.append(pltpu.make_async_remote_copy(
                src_ref=x_ref, dst_ref=out_ref.at[chip],
                send_sem=send_sems.at[3 * n + kk], recv_sem=recv_sems.at[3 * n + kk],
                device_id=peer, device_id_type=MESH))
    return copies


def _fox_fwd(proj, qn, kn, cf, cf_rows, kmax, *, name, gathers=()):
    s = proj.shape[0]
    t = min(ATT_T, s)
    scale = HEAD_DIM ** -0.5
    n_pairs, nq, ng = FOX_W // LANES, s // t, len(gathers)

    def body(*refs):
        q_ref, k_ref, v_ref, cfc_ref, cfr_ref, kmax_ref = refs[:6]
        o_ref, lse_ref = refs[6 + ng:8 + ng]
        p, i = pl.program_id(0), pl.program_id(1)
        if ng:
            def copies():
                return _chip_gather_copies(refs[6:6 + ng], refs[8 + ng:8 + 2 * ng], *refs[8 + 2 * ng:])

            @pl.when((p == 0) & (i == 0))
            def _():
                for cp in copies():
                    cp.start()
        hm = _lane_masks()
        q = q_ref[...] * scale
        qh = [jnp.where(mk, q, 0.0).astype(BF16) for mk in hm]
        cfc = cfc_ref[...]
        bq = [_fox_bias_q(cfc, p, h) for h in range(2)]
        qk_top = _fox_score_bound(q, kmax_ref[...], hm)
        r, c = _tri_iotas(t)
        causal = c <= r

        q2 = jnp.concatenate(qh, axis=0)
        causal2 = jnp.concatenate([causal, causal], axis=0)

        def scores(j):
            return _dot_nt(q2, k_ref[_rows(j, t), :].astype(BF16))

        def chunk(j, carry, z2, masked):
            m_run, l_run, acc = carry
            vb = v_ref[_rows(j, t), :].astype(BF16)
            z = jnp.concatenate(
                [z2[h * t:(h + 1) * t] + (bq[h] - cfr_ref[0, pl.ds(h, 1), _rows(j, t)])
                 for h in range(2)], axis=0)
            if masked:
                z = jnp.where(causal2, z, -1e30)
            m_new = jnp.maximum(m_run, jnp.max(z, axis=1, keepdims=True))
            alpha = jnp.exp(m_run - m_new)
            pr = jnp.exp(z - m_new)
            return (m_new, alpha * l_run + jnp.sum(pr, axis=1, keepdims=True),
                    alpha * acc + _dot(pr.astype(BF16), vb))

        init = (jnp.full((2 * t, 1), -1e30, F32), jnp.zeros((2 * t, 1), F32),
                jnp.zeros((2 * t, LANES), F32))
        carry = chunk(i, init, scores(i), True)

        def live(j, cr):
            return _fox_live(cfr_ref, j, t,
                             [qk_top[h] + bq[h] - cr[0][h * t:(h + 1) * t] for h in range(2)])

        def step(st):
            j, _, cr, z2 = st
            z2_next = scores(jnp.maximum(j - 1, 0))
            cr = chunk(j, cr, z2, False)
            return j - 1, live(j - 1, cr), cr, z2_next

        m_fin, l_fin, acc = lax.while_loop(
            lambda st: st[1], step,
            (i - 1, live(i - 1, carry), carry, scores(jnp.maximum(i - 1, 0))))[2]
        o2 = acc / l_fin
        lse2 = m_fin + jnp.log(l_fin)
        o_ref[...] = jnp.where(hm[0], o2[:t], o2[t:])
        lse_ref[...] = jnp.where(hm[0], lse2[:t], lse2[t:])
        if ng:
            @pl.when((p == n_pairs - 1) & (i == nq - 1))
            def _():
                for cp in copies():
                    cp.wait()

    blk = pl.BlockSpec((t, LANES), lambda p, i: (i, p))
    full = pl.BlockSpec((s, LANES), lambda p, i: (0, p))
    any_spec = pl.BlockSpec(memory_space=pl.ANY)
    dma = pltpu.SemaphoreType.DMA
    return _pcall(body, name=name, grid=(n_pairs, nq),
                  in_specs=[blk, full, pl.BlockSpec((s, LANES), lambda p, i: (0, CB_VB + p)),
                            pl.BlockSpec((t, LANES), lambda p, i: (i, 0)),
                            pl.BlockSpec((1, 2, s), lambda p, i: (p, 0, 0)),
                            pl.BlockSpec((1, LANES), lambda p, i: (0, p))] + [any_spec] * ng,
                  out_specs=[blk, blk] + [any_spec] * ng,
                  out_shape=[jax.ShapeDtypeStruct((s, FOX_W), F32)] * 2
                  + [jax.ShapeDtypeStruct((4,) + g.shape, g.dtype) for g in gathers],
                  scratch_shapes=[dma((3 * ng,)), dma((3 * ng,)), dma((ng,))] if ng else [],
                  semantics=("arbitrary", "arbitrary") if ng else ("parallel", "arbitrary"))(
                      qn, kn, proj, cf, cf_rows, kmax, *gathers)


def _fox_bwd(proj, qn, kn, cf, cf_rows, kmax, do, o, lse, *, name):
    s = proj.shape[0]
    t = min(ATT_T, s)
    scale = HEAD_DIM ** -0.5

    def body(q_ref, k_ref, v_ref, cfc_ref, cfr_ref, kmax_ref, do_ref, o_ref, lse_ref,
             dq_ref, dk_ref, dv_ref, dcf_ref, dcfq_ref):
        p, i = pl.program_id(0), pl.program_id(1)

        @pl.when(i == 0)
        def _():
            dk_ref[...] = jnp.zeros_like(dk_ref)
            dv_ref[...] = jnp.zeros_like(dv_ref)
            dcf_ref[...] = jnp.zeros_like(dcf_ref)

        hm = _lane_masks()
        q = q_ref[...] * scale
        do = do_ref[...]
        dov = do * o_ref[...]
        qh = [jnp.where(mk, q, 0.0).astype(BF16) for mk in hm]
        doh = [jnp.where(mk, do, 0.0).astype(BF16) for mk in hm]
        delta = [jnp.sum(jnp.where(mk, dov, 0.0), axis=1, keepdims=True) for mk in hm]
        lsev = lse_ref[...]
        lse = [lsev[:, 0:1], lsev[:, HEAD_DIM:HEAD_DIM + 1]]
        cfc = cfc_ref[...]
        bq = [_fox_bias_q(cfc, p, h) - lse[h] for h in range(2)]
        qk_top = _fox_score_bound(q, kmax_ref[...], hm)
        tops = [qk_top[h] + bq[h] for h in range(2)]
        r, c = _tri_iotas(t)
        j_stop = lax.while_loop(lambda st: st[1],
                                lambda st: (st[0] - 1, _fox_live(cfr_ref, st[0] - 1, t, tops)),
                                (i - 1, _fox_live(cfr_ref, i - 1, t, tops)))[0]
        q2 = jnp.concatenate(qh, axis=0)
        do2 = jnp.concatenate(doh, axis=0)
        delta2 = jnp.concatenate(delta, axis=0)
        causal2 = jnp.concatenate([c <= r, c <= r], axis=0)

        def products(j):
            return (_dot_nt(q2, k_ref[_rows(j, t), :].astype(BF16)),
                    _dot_nt(do2, v_ref[_rows(j, t), :].astype(BF16)))

        def chunk(j, carry, z2, dp, masked):
            dq, row_sum = carry
            z = jnp.concatenate(
                [z2[h * t:(h + 1) * t] + (bq[h] - cfr_ref[0, pl.ds(h, 1), _rows(j, t)])
                 for h in range(2)], axis=0)
            pr = jnp.exp(z)
            if masked:
                pr = jnp.where(causal2, pr, 0.0)
            ds = pr * (dp - delta2)
            dsb = ds.astype(BF16)
            dk_ref[_rows(j, t), :] += _dot_tn(dsb, q2)
            dv_ref[_rows(j, t), :] += _dot_tn(pr.astype(BF16), do2)
            for h in range(2):
                dcf_ref[0, pl.ds(h, 1), _rows(j, t)] -= jnp.sum(ds[h * t:(h + 1) * t], axis=0,
                                                               keepdims=True)
            return (dq + _dot(dsb, k_ref[_rows(j, t), :].astype(BF16)),
                    row_sum + jnp.sum(ds, axis=1, keepdims=True))

        def one(j, cr):
            return chunk(j, cr, *products(j), False)

        init = (jnp.zeros((2 * t, LANES), F32), jnp.zeros((2 * t, 1), F32))
        first, odd = j_stop + 1, (i - j_stop - 1) % 2
        carry = lax.cond(odd == 1, lambda cr: one(first, cr), lambda cr: cr, init)
        carry = lax.fori_loop(0, (i - first) // 2,
                              lambda n, cr: one(first + odd + 2 * n + 1,
                                                one(first + odd + 2 * n, cr)), carry)
        dq2, row_sum = chunk(i, carry, *products(i), True)
        dq_ref[...] = jnp.where(hm[0], dq2[:t], dq2[t:]) * scale
        dcfq_ref[...] = jnp.where(hm[0], row_sum[:t], row_sum[t:])

    blk = pl.BlockSpec((t, LANES), lambda p, i: (i, p))
    full = pl.BlockSpec((s, LANES), lambda p, i: (0, p))
    rows = pl.BlockSpec((1, 2, s), lambda p, i: (p, 0, 0))
    return _pcall(body, name=name, grid=(FOX_W // LANES, s // t),
                  in_specs=[blk, full, pl.BlockSpec((s, LANES), lambda p, i: (0, CB_VB + p)),
                            pl.BlockSpec((t, LANES), lambda p, i: (i, 0)), rows,
                            pl.BlockSpec((1, LANES), lambda p, i: (0, p)),
                            pl.BlockSpec((t, LANES), lambda p, i: (i, SB_W // LANES + p)),
                            blk, blk],
                  out_specs=[blk, full, full, rows, blk],
                  out_shape=[jax.ShapeDtypeStruct((s, FOX_W), F32)] * 3
                  + [jax.ShapeDtypeStruct((FOX_W // LANES, 2, s), F32),
                     jax.ShapeDtypeStruct((s, FOX_W), F32)],
                  semantics=("parallel", "arbitrary"))(qn, kn, proj, cf, cf_rows, kmax, do, o, lse)


_GELU_C0 = math.sqrt(2.0 / math.pi)
_GELU_C1 = 0.044715


def _gelu(x):
    th = jnp.tanh(_GELU_C0 * (x + _GELU_C1 * (x * x * x)))
    return 0.5 * x * (1.0 + th), th


def _gelu_grad(x, th):
    return 0.5 * (1.0 + th) + 0.5 * x * (1.0 - th * th) * (_GELU_C0 * (1.0 + 3.0 * _GELU_C1 * x * x))


def _sgu_mix(wm, vn_c, lo, bcol):
    return jnp.where(lo, _dot(wm[0], vn_c) + bcol[0], _dot(wm[1], vn_c) + bcol[1])


def _sgu_fwd(proj, w, b_cols, gn, *, name):
    s = proj.shape[0]
    tr = min(SGU_ROWS, s)
    ch = SGU_CHUNK

    def body(u_ref, v_ref, w_ref, b_ref, gn_ref, o_ref):
        lo = _lane_masks()[0]
        r, c = _tri_iotas(ch)
        wm = [jnp.where(c <= r, w_ref[h], 0.0).astype(BF16) for h in range(2)]
        bcol = [b_ref[0, :, h:h + 1] for h in range(2)]
        for n in range(tr // ch):
            rows = slice(n * ch, (n + 1) * ch)
            u, _ = _gelu(u_ref[rows, :])
            vg, _ = _gelu(v_ref[rows, :])
            vn = vg * lax.rsqrt(_group_mean(vg * vg, lo) + EPS) * gn_ref[0]
            o_ref[rows, :] = u * _sgu_mix(wm, vn.astype(BF16), lo, bcol)

    blk = lambda cb: pl.BlockSpec((tr, LANES), lambda p, i: (i, cb + p))
    return _pcall(body, name=name, grid=(SGU_W // LANES, s // tr),
                  in_specs=[blk(CB_UC), blk(CB_VC),
                            pl.BlockSpec((2, ch, ch), lambda p, i: (p, 0, 0)),
                            pl.BlockSpec((1, ch, 2), lambda p, i: (p, 0, 0)),
                            pl.BlockSpec((1, 1, LANES), lambda p, i: (p, 0, 0))],
                  out_specs=pl.BlockSpec((tr, LANES), lambda p, i: (i, p)),
                  out_shape=jax.ShapeDtypeStruct((s, SGU_W), F32),
                  semantics=("parallel", "parallel"))(proj, proj, w, b_cols, gn)


def _sgu_bwd(proj, dmixed, w, w_t, b_cols, gn, *, name):
    s = proj.shape[0]
    tr = min(SGU_ROWS, s)
    ch = SGU_CHUNK
    cb_do = (SB_W + FOX_W) // LANES

    def body(u_ref, v_ref, do_ref, w_ref, wt_ref, b_ref, gn_ref,
             du_ref, dv_ref, dw_ref, db_ref, dgn_ref):
        @pl.when(pl.program_id(1) == 0)
        def _():
            dw_ref[...] = jnp.zeros_like(dw_ref)
            db_ref[...] = jnp.zeros_like(db_ref)
            dgn_ref[...] = jnp.zeros_like(dgn_ref)

        hm = _lane_masks()
        lo = hm[0]
        r, c = _tri_iotas(ch)
        wm = [jnp.where(c <= r, w_ref[h], 0.0).astype(BF16) for h in range(2)]
        wtm = [jnp.where(r <= c, wt_ref[h], 0.0).astype(BF16) for h in range(2)]
        bcol = [b_ref[0, :, h:h + 1] for h in range(2)]
        gnv = gn_ref[0]
        for n in range(tr // ch):
            rows = slice(n * ch, (n + 1) * ch)
            uc, vc, do = u_ref[rows, :], v_ref[rows, :], do_ref[rows, :]
            u, thu = _gelu(uc)
            vg, thv = _gelu(vc)
            rinv = lax.rsqrt(_group_mean(vg * vg, lo) + EPS)
            xh = vg * rinv
            vnb = (xh * gnv).astype(BF16)
            mix = _sgu_mix(wm, vnb, lo, bcol)
            du_ref[rows, :] = do * mix * _gelu_grad(uc, thu)
            dm = do * u
            dmb = dm.astype(BF16)
            dvn = jnp.where(lo, _dot(wtm[0], dmb), _dot(wtm[1], dmb))
            for h in range(2):
                dmh = jnp.where(hm[h], dm, 0.0)
                dw_ref[h] += jnp.where(c <= r, _dot_nt(dmh.astype(BF16), vnb), 0.0)
                db_ref[0, :, h:h + 1] += jnp.sum(dmh, axis=1, keepdims=True)
            dgn_ref[0] += _colsum(dvn * xh)
            dxh = dvn * gnv
            dvg = rinv * (dxh - xh * _group_mean(dxh * xh, lo))
            dv_ref[rows, :] = dvg * _gelu_grad(vc, thv)

    blk = lambda cb: pl.BlockSpec((tr, LANES), lambda p, i: (i, cb + p))
    w_spec = pl.BlockSpec((2, ch, ch), lambda p, i: (p, 0, 0))
    b_spec = pl.BlockSpec((1, ch, 2), lambda p, i: (p, 0, 0))
    g_spec = pl.BlockSpec((1, 1, LANES), lambda p, i: (p, 0, 0))
    out_blk = pl.BlockSpec((tr, LANES), lambda p, i: (i, p))
    return _pcall(body, name=name, grid=(SGU_W // LANES, s // tr),
                  in_specs=[blk(CB_UC), blk(CB_VC), blk(cb_do), w_spec, w_spec, b_spec, g_spec],
                  out_specs=[out_blk, out_blk, w_spec, b_spec, g_spec],
                  out_shape=[jax.ShapeDtypeStruct((s, SGU_W), F32)] * 2
                  + [jax.ShapeDtypeStruct(w.shape, F32), jax.ShapeDtypeStruct(b_cols.shape, F32),
                     jax.ShapeDtypeStruct(gn.shape, F32)],
                  semantics=("parallel", "arbitrary"))(proj, proj, dmixed, w, w_t, b_cols, gn)


def _pad_lanes(v):
    return jnp.zeros((1, LANES), F32).at[0, :v.shape[0]].set(v)


def _small_views(sm):
    return dict(
        n1=sm["norm1_g"][None, :], n2=sm["norm2_g"][None, :],
        b_pad=_pad_lanes(sm["b_forget"]),
        qg=jnp.tile(sm["q_norm_g"], 2)[None, :], kg=jnp.tile(sm["k_norm_g"], 2)[None, :],
        gn=sm["sgu_norm_g"].reshape(2, 1, LANES),
        w=sm["sgu_w"], w_t=jnp.swapaxes(sm["sgu_w"], 1, 2),
        b_cols=sm["sgu_b"].reshape(2, 2, SGU_CHUNK).transpose(0, 2, 1))


def _cf_rows(cf):
    return cf[:, :FOX_HEADS].T.reshape(FOX_W // LANES, 2, cf.shape[0])


def _layer_fwd(x_in, prev, mod, wts, sm, l, gathers=(), late_weights=None):
    sh1, sc1, g1, sh2, sc2, g2 = mod
    v = _small_views(sm)
    if prev is None:
        x0 = x_in
        h1 = _norm_mod_fwd(x0, v["n1"], sc1, sh1, name=f"l{l}_norm1")
    else:
        x0, h1 = _resid_norm_mod_fwd(x_in, prev[0], prev[1], v["n1"], sc1, sh1, name=f"l{l}_norm1")
    proj = _matmul(h1, wts["w_in"], name=f"l{l}_proj")
    o_sb, sb_ltot, sb_stop = _sb_fwd(proj, name=f"l{l}_sb_fwd")
    qn, kn, kmax = _fox_prep_fwd(proj, v["qg"], v["kg"], name=f"l{l}_fox_prep")
    cf = _forget_cumsum_fwd(proj, v["b_pad"], name=f"l{l}_cumf")
    cfr = _cf_rows(cf)
    o_fox, lse, *gathered = _fox_fwd(proj, qn, kn, cf, cfr, kmax, name=f"l{l}_fox_fwd",
                                     gathers=gathers)
    if gathers:
        late_weights(gathered)
    o_sgu = _sgu_fwd(proj, v["w"], v["b_cols"], v["gn"], name=f"l{l}_sgu_fwd")
    mixed = jnp.concatenate([o_sb, o_fox, o_sgu], axis=1).astype(BF16)
    mo = _matmul(mixed, wts["w_out"], name=f"l{l}_wout")
    x1, h2 = _resid_norm_mod_fwd(x0, mo, g1, v["n2"], sc2, sh2, name=f"l{l}_norm2")
    a, rr = _matmul(h2, wts["w1"], name=f"l{l}_mlp1", out_dtype=BF16, relu2=BF16)
    m2 = _matmul(rr, wts["w2"], name=f"l{l}_mlp2")
    saved = dict(x0=x0, h1=h1, proj=proj, sb_ltot=sb_ltot, sb_stop=sb_stop, qn=qn, kn=kn, kmax=kmax, cf=cf, cfr=cfr, o_fox=o_fox,
                 lse=lse, mixed=mixed, mo=mo, x1=x1, h2=h2, a=a, rr=rr, m2=m2)
    return saved


def _layer_bwd(dx2, dm2, dg2, sv, mod, wts, sm, l, below):
    sh1, sc1, g1, sh2, sc2, g2 = mod
    v = _small_views(sm)
    dw2 = _matmul(sv["rr"], dm2, ta=True, name=f"l{l}_dw2")
    da = _matmul(dm2, wts["w2"], tb=True, name=f"l{l}_da", out_dtype=BF16, pre_act=sv["a"])
    dw1 = _matmul(sv["h2"], da, ta=True, name=f"l{l}_dw1")
    dh2 = _matmul(da, wts["w1"], tb=True, name=f"l{l}_dh2")
    dx1, dn2, dsc2, dsh2, dmo, dg1 = _norm_mod_bwd(sv["x1"], dh2, dx2, v["n2"], sc2,
                                                    (sv["mo"], g1), name=f"l{l}_norm2_bwd")
    dwo = _matmul(sv["mixed"], dmo, ta=True, name=f"l{l}_dwout")
    dmixed = _matmul(dmo, wts["w_out"], tb=True, name=f"l{l}_dmixed")
    proj = sv["proj"]
    dqa, dka, dva = _sb_bwd(proj, dmixed, sv["sb_ltot"], sv["sb_stop"], name=f"l{l}_sb_bwd")
    dqn, dkn, dvb, dcfr, dcfq = _fox_bwd(proj, sv["qn"], sv["kn"], sv["cf"], sv["cfr"], sv["kmax"], dmixed,
                                   sv["o_fox"], sv["lse"], name=f"l{l}_fox_bwd")
    dqb, dkb, dqg, dkg = _fox_prep_bwd(proj, dqn, dkn, v["qg"], v["kg"], name=f"l{l}_fox_prep_bwd")
    s = proj.shape[0]
    dcf_heads = dcfr.reshape(FOX_HEADS, s).T + dcfq.reshape(s, FOX_HEADS, HEAD_DIM)[:, :, 0]
    dcf = jnp.zeros((s, LANES), F32).at[:, :FOX_HEADS].set(dcf_heads)
    dfl, dbf = _forget_cumsum_bwd(proj, v["b_pad"], dcf, name=f"l{l}_cumf_bwd")
    duc, dvc, dsw, dsb_cols, dgn = _sgu_bwd(proj, dmixed, v["w"], v["w_t"], v["b_cols"], v["gn"],
                                            name=f"l{l}_sgu_bwd")
    dproj = jnp.concatenate([dqa, dka, dva, dqb, dkb, dvb, duc, dvc, dfl,
                             jnp.zeros((s, LANES), F32)], axis=1).astype(BF16)
    dwin = _matmul(sv["h1"], dproj, ta=True, name=f"l{l}_dwin")
    dh1 = _matmul(dproj, wts["w_in"], tb=True, name=f"l{l}_dh1")
    dx0, dn1, dsc1, dsh1, dm_below, dg_below = _norm_mod_bwd(sv["x0"], dh1, dx1, v["n1"], sc1, below,
                                                             name=f"l{l}_norm1_bwd")
    big = dict(w_in=dwin, w_out=dwo, w1=dw1, w2=dw2)
    small = dict(norm1_g=dn1[0], norm2_g=dn2[0], b_forget=dbf[0, :FOX_HEADS],
                 q_norm_g=dqg[0, :HEAD_DIM] + dqg[0, HEAD_DIM:],
                 k_norm_g=dkg[0, :HEAD_DIM] + dkg[0, HEAD_DIM:],
                 sgu_norm_g=dgn.reshape(4, HEAD_DIM), sgu_w=dsw,
                 sgu_b=dsb_cols.transpose(0, 2, 1).reshape(4, SGU_CHUNK))
    dmod = jnp.concatenate([dsh1, dsc1, dg1, dsh2, dsc2, dg2], axis=1)
    return dx0, dm_below, dg_below, big, small, dmod


def _w_in_to_internal(w):
    pad = jnp.zeros((w.shape[0], PROJ_W - IN_W), w.dtype)
    return jnp.concatenate([w[:, :ATT_W], w[:, ATT_W + FOX_HEADS:], w[:, ATT_W:ATT_W + FOX_HEADS],
                            pad], axis=1)


def _w_in_from_internal(g):
    n_gate = SGU_W * 2
    return jnp.concatenate([g[:, :ATT_W], g[:, ATT_W + n_gate:ATT_W + n_gate + FOX_HEADS],
                            g[:, ATT_W:ATT_W + n_gate]], axis=1)


def _exchange(x, masks, slot_shift, slot_bits, scatter, *, name):
    n_slots = 2 ** slot_bits
    blk_shape = x.shape[1:] if scatter else x.shape
    n_peers = len(masks)

    def body(x_ref, out_ref, send_sems, recv_sems, local_sem):
        ids = (lax.axis_index("x"), lax.axis_index("y"), lax.axis_index("c"))
        me = 4 * ids[0] + 2 * ids[1] + ids[2]
        my_slot = (me >> slot_shift) & (n_slots - 1)

        def peer(mask):
            return tuple(1 - v if (mask >> b) & 1 else v for v, b in zip(ids, (2, 1, 0)))

        def src_for(slot):
            return x_ref.at[slot] if scatter else x_ref

        copies = [pltpu.make_async_copy(src_for(my_slot), out_ref.at[my_slot], local_sem)]
        for kk, mask in enumerate(masks):
            peer_slot = ((me ^ mask) >> slot_shift) & (n_slots - 1)
            copies.append(pltpu.make_async_remote_copy(
                src_ref=src_for(peer_slot), dst_ref=out_ref.at[my_slot],
                send_sem=send_sems.at[kk], recv_sem=recv_sems.at[kk],
                device_id=peer(mask), device_id_type=MESH))
        for cp in copies:
            cp.start()
        for cp in copies:
            cp.wait()

    any_spec = pl.BlockSpec(memory_space=pl.ANY)
    return _pcall(body, name=name, in_specs=[any_spec], out_specs=any_spec,
                  out_shape=jax.ShapeDtypeStruct((n_slots,) + tuple(blk_shape), x.dtype),
                  scratch_shapes=[pltpu.SemaphoreType.DMA((n_peers,)),
                                  pltpu.SemaphoreType.DMA((n_peers,)),
                                  pltpu.SemaphoreType.DMA(())])(x)


CORE_PIECE_BYTES = 12 * 2 ** 20
CORE_DMA_CHUNKS = 4


def _core_swap_piece(x, *, name):
    rows, cols = x.shape
    n_ch = CORE_DMA_CHUNKS if rows % (16 * CORE_DMA_CHUNKS) == 0 else 1
    rc = rows // n_ch

    def body(x_ref, out_ref, send_sems, recv_sems):
        sibling = (lax.axis_index("x"), lax.axis_index("y"), 1 - lax.axis_index("c"))
        copies = [pltpu.make_async_remote_copy(
            src_ref=x_ref.at[pl.ds(ch * rc, rc)], dst_ref=out_ref.at[pl.ds(ch * rc, rc)],
            send_sem=send_sems.at[ch], recv_sem=recv_sems.at[ch],
            device_id=sibling, device_id_type=MESH) for ch in range(n_ch)]
        for cp in copies:
            cp.start()
        for cp in copies:
            cp.wait()

    vmem = pl.BlockSpec(memory_space=pltpu.VMEM)
    return _pcall(body, name=name, in_specs=[vmem], out_specs=vmem,
                  out_shape=jax.ShapeDtypeStruct(x.shape, x.dtype),
                  scratch_shapes=[pltpu.SemaphoreType.DMA((n_ch,)),
                                  pltpu.SemaphoreType.DMA((n_ch,))])(x)


def _core_swap(x, *, name):
    rows, cols = x.shape
    n = 1
    while (rows % n or (rows // n) % 16 or
           (rows // n) * (-(-cols // LANES) * LANES) * x.dtype.itemsize > CORE_PIECE_BYTES):
        n += 1
    pr = rows // n
    pieces = [_core_swap_piece(x[kk * pr:(kk + 1) * pr], name=f"{name}_{kk}") for kk in range(n)]
    return pieces[0] if n == 1 else jnp.concatenate(pieces, axis=0)


def _by_core(core, mine, theirs, axis):
    return jnp.where(core == 0, jnp.concatenate([mine, theirs], axis=axis),
                     jnp.concatenate([theirs, mine], axis=axis))


def _gather_chips(x, *, name):
    return _exchange(x, (2, 4, 6), 1, 2, False, name=name)


def _gather_all(x, *, name):
    return _exchange(x, (1, 2, 3, 4, 5, 6, 7), 0, 3, False, name=name)


def _scatter_chips(x4, *, name):
    return _exchange(x4, (2, 4, 6), 1, 2, True, name=name)


def _sum_slots(parts, *, name, out_dtype=F32, tr=256):
    n, rows, cols = parts.shape
    tr = min(tr, rows)
    assert rows % tr == 0, (name, rows, tr)

    def body(p_ref, o_ref):
        acc = p_ref[0].astype(F32)
        for kk in range(1, n):
            acc = acc + p_ref[kk].astype(F32)
        o_ref[...] = acc.astype(o_ref.dtype)

    return _pcall(body, name=name, grid=(rows // tr,),
                  in_specs=[pl.BlockSpec((n, tr, cols), lambda i: (0, i, 0))],
                  out_specs=pl.BlockSpec((tr, cols), lambda i: (i, 0)),
                  out_shape=jax.ShapeDtypeStruct((rows, cols), out_dtype),
                  semantics=("parallel",))(parts)


def _add2(a, b, *, name, out_dtype, tr=512):
    def fn(f, v):
        return [f[0] + f[1]], []
    (out,), _ = _rowwise(fn, [a, b], [], [out_dtype], 0, name=name, tr=tr)
    return out


def _adamw(w, m, v, parts, *, name, tr=256):
    n, rows, cols = parts.shape
    tr = min(tr, rows)
    assert rows % tr == 0, (name, rows, tr)
    c1 = 1.0 - ADAM_B1 ** ADAM_STEP
    c2 = 1.0 - ADAM_B2 ** ADAM_STEP

    def body(w_ref, m_ref, v_ref, p_ref, g_ref, d_ref, nm_ref, nv_ref):
        g = p_ref[0]
        for kk in range(1, n):
            g = g + p_ref[kk]
        nm = ADAM_B1 * m_ref[...] + (1.0 - ADAM_B1) * g
        nv = ADAM_B2 * v_ref[...] + (1.0 - ADAM_B2) * (g * g)
        g_ref[...] = g
        nm_ref[...] = nm
        nv_ref[...] = nv
        d_ref[...] = -ADAM_LR * ((nm / c1) / (jnp.sqrt(nv / c2) + ADAM_EPS) + ADAM_WD * w_ref[...])

    spec = pl.BlockSpec((tr, cols), lambda i: (i, 0))
    return _pcall(body, name=name, grid=(rows // tr,),
                  in_specs=[spec, spec, spec, pl.BlockSpec((n, tr, cols), lambda i: (0, i, 0))],
                  out_specs=[spec] * 4,
                  out_shape=[jax.ShapeDtypeStruct((rows, cols), F32)] * 4,
                  semantics=("parallel",))(w, m, v, parts)


def _silu(c):
    return c / (1.0 + jnp.exp(-c))


def _ada_fwd(c_all, ada_w, ada_b_sh, *, name):
    nl, d, wsh = ada_w.shape

    def body(c_ref, w_ref, b_ref, o_ref):
        cond = _silu(c_ref[...]).astype(BF16)
        o_ref[0] = _dot(cond, w_ref[0].astype(BF16)) + b_ref[0]

    return _pcall(body, name=name, grid=(nl,),
                  in_specs=[pl.BlockSpec(c_all.shape, lambda l: (0, 0)),
                            pl.BlockSpec((1, d, wsh), lambda l: (l, 0, 0)),
                            pl.BlockSpec((1, 1, wsh), lambda l: (l, 0, 0))],
                  out_specs=pl.BlockSpec((1, c_all.shape[0], wsh), lambda l: (l, 0, 0)),
                  out_shape=jax.ShapeDtypeStruct((nl, c_all.shape[0], wsh), F32),
                  semantics=("parallel",))(c_all, ada_w, ada_b_sh)


def _ada_bwd(c_all, dmod_sh, *, name):
    nl, nb, wsh = dmod_sh.shape
    d = c_all.shape[1]

    def body(c_ref, dm_ref, o_ref):
        cond = _silu(c_ref[...]).astype(BF16)
        o_ref[0] = _dot_tn(cond, dm_ref[0].astype(BF16))

    return _pcall(body, name=name, grid=(nl,),
                  in_specs=[pl.BlockSpec(c_all.shape, lambda l: (0, 0)),
                            pl.BlockSpec((1, nb, wsh), lambda l: (l, 0, 0))],
                  out_specs=pl.BlockSpec((1, d, wsh), lambda l: (l, 0, 0)),
                  out_shape=jax.ShapeDtypeStruct((nl, d, wsh), F32),
                  semantics=("parallel",))(c_all, dmod_sh)


SMALL_NAMES = ("norm1_g", "norm2_g", "b_forget", "q_norm_g", "k_norm_g", "sgu_norm_g", "sgu_w",
               "sgu_b")
WEIGHT_NAMES = ("ada_w", "ada_b", "norm1_g", "norm2_g", "w_in", "b_forget", "q_norm_g", "k_norm_g",
                "sgu_norm_g", "sgu_w", "sgu_b", "w_out", "mlp_w1", "mlp_w2")


SMALL_TILE_ROWS = 256


def _pack_small(tree):
    flat = jnp.concatenate([tree[n].reshape(-1) for n in SMALL_NAMES])
    n = flat.shape[0]
    rows = -(-n // (SMALL_TILE_ROWS * LANES)) * SMALL_TILE_ROWS
    return jnp.zeros((rows * LANES,), F32).at[:n].set(flat).reshape(rows, LANES)


def _unpack_small(packed, like):
    flat = packed.reshape(-1)
    out, off = {}, 0
    for n in SMALL_NAMES:
        size = like[n].size
        out[n] = flat[off:off + size].reshape(like[n].shape)
        off += size
    return out


def kernel(x, c, ada_w, ada_b, norm1_g, norm2_g, w_in, b_forget, q_norm_g, k_norm_g, sgu_norm_g, sgu_w, sgu_b, w_out, mlp_w1, mlp_w2, loss_target, m_ada_w, m_ada_b, m_norm1_g, m_norm2_g, m_w_in, m_b_forget, m_q_norm_g, m_k_norm_g, m_sgu_norm_g, m_sgu_w, m_sgu_b, m_w_out, m_mlp_w1, m_mlp_w2, v_ada_w, v_ada_b, v_norm1_g, v_norm2_g, v_w_in, v_b_forget, v_q_norm_g, v_k_norm_g, v_sgu_norm_g, v_sgu_w, v_sgu_b, v_w_out, v_mlp_w1, v_mlp_w2):
    w = dict(ada_w=ada_w, ada_b=ada_b, norm1_g=norm1_g, norm2_g=norm2_g, w_in=w_in,
             b_forget=b_forget, q_norm_g=q_norm_g, k_norm_g=k_norm_g, sgu_norm_g=sgu_norm_g,
             sgu_w=sgu_w, sgu_b=sgu_b, w_out=w_out, mlp_w1=mlp_w1, mlp_w2=mlp_w2)
    mom = dict(ada_w=m_ada_w, ada_b=m_ada_b, norm1_g=m_norm1_g, norm2_g=m_norm2_g, w_in=m_w_in,
               b_forget=m_b_forget, q_norm_g=m_q_norm_g, k_norm_g=m_k_norm_g,
               sgu_norm_g=m_sgu_norm_g, sgu_w=m_sgu_w, sgu_b=m_sgu_b, w_out=m_w_out,
               mlp_w1=m_mlp_w1, mlp_w2=m_mlp_w2)
    var = dict(ada_w=v_ada_w, ada_b=v_ada_b, norm1_g=v_norm1_g, norm2_g=v_norm2_g, w_in=v_w_in,
               b_forget=v_b_forget, q_norm_g=v_q_norm_g, k_norm_g=v_k_norm_g,
               sgu_norm_g=v_sgu_norm_g, sgu_w=v_sgu_w, sgu_b=v_sgu_b, w_out=v_w_out,
               mlp_w1=v_mlp_w1, mlp_w2=v_mlp_w2)
    depth, d = norm1_g.shape
    chip = 2 * lax.axis_index("x") + lax.axis_index("y")
    me = 2 * chip + lax.axis_index("c")
    n_chips = 4
    ada_sh = ada_w.shape[2]

    core = lax.axis_index("c")
    half_l = depth // 2

    def my_part(w_sh):
        _, r, cols = w_sh.shape
        mine = lax.dynamic_slice_in_dim(w_sh, core * half_l, half_l, axis=0).astype(BF16)
        return mine.reshape(half_l * r, cols)

    def share(got, w_sh, name):
        _, r, cols = w_sh.shape
        theirs = _core_swap(got.reshape(n_chips * half_l * r, cols), name=f"share_{name}")
        return _by_core(core, got.reshape(n_chips, half_l, r, cols),
                        theirs.reshape(n_chips, half_l, r, cols), 1)

    g_in = share(_gather_chips(my_part(w_in), name="gather_w_in"), w_in, "w_in")
    layer_w = [dict(w_in=_w_in_to_internal(
        jnp.concatenate([g_in[k, l] for k in range(n_chips)], axis=1))) for l in range(depth)]
    later = (("w_out", w_out), ("w1", mlp_w1), ("w2", mlp_w2))

    def late_weights(gathered):
        g_out, g_w1, g_w2 = [share(got, w_sh, name) for got, (name, w_sh) in zip(gathered, later)]
        for l in range(depth):
            layer_w[l].update(
                w_out=g_out[:, l].reshape(d, d),
                w1=jnp.concatenate([g_w1[k, l] for k in range(n_chips)], axis=1),
                w2=g_w2[:, l].reshape(D_FF, d))

    c_all = _gather_all(jnp.zeros((8, d), F32).at[0].set(c[0]), name="gather_c")[:, 0]
    c_pad = jnp.concatenate([c_all, jnp.zeros_like(c_all)], axis=0)
    ada_b_sh = lax.dynamic_slice_in_dim(ada_b, chip * ada_sh, ada_sh, axis=1)[:, None, :]
    mod_sh = _ada_fwd(c_pad, ada_w, ada_b_sh, name="ada_fwd")
    mod_all = _gather_chips(mod_sh, name="gather_mod")
    mod_me = lax.dynamic_index_in_dim(mod_all, me, axis=2, keepdims=False)
    mod_me = mod_me.transpose(1, 0, 2).reshape(depth, 6, 1, d)

    saved = []
    xs, prev = x[0], None
    for l in range(depth):
        mod = [mod_me[l, kk] for kk in range(6)]
        sm = {n: w[n][l] for n in SMALL_NAMES}
        first = dict(gathers=[my_part(w_sh) for _, w_sh in later], late_weights=late_weights)
        sv = _layer_fwd(xs, prev, mod, layer_w[l], sm, l, **(first if l == 0 else {}))
        saved.append(sv)
        xs, prev = sv["x1"], (sv["m2"], mod[5])

    sq, dxs, dm2, dg2 = _loss_fwd_bwd(xs, prev[0], prev[1], loss_target[0], name="loss")
    loss = lax.psum(0.5 * jnp.sum(sq) / d, ("x", "y", "c"))

    big = {n: [] for n in ("w_in", "w_out", "w1", "w2")}
    small = {n: [] for n in SMALL_NAMES}
    dmods = []
    for l in reversed(range(depth)):
        mod = [mod_me[l, kk] for kk in range(6)]
        sm = {n: w[n][l] for n in SMALL_NAMES}
        below = (saved[l - 1]["m2"], mod_me[l - 1, 5]) if l else None
        dxs, dm2, dg2, bg, smg, dmod = _layer_bwd(dxs, dm2, dg2, saved[l], mod, layer_w[l], sm, l,
                                                  below)
        for n in big:
            big[n].insert(0, bg[n])
        for n in SMALL_NAMES:
            small[n].insert(0, smg[n])
        dmods.insert(0, dmod)
    grad_x = dxs[None]

    out_g, out_d, out_m, out_v = {}, {}, {}, {}

    def run_adamw(name, parts2d, shape):
        rows, cols = parts2d.shape[1:]
        g, dl, nm, nv = _adamw(w[name].reshape(rows, cols), mom[name].reshape(rows, cols),
                               var[name].reshape(rows, cols), parts2d, name=f"adamw_{name}")
        out_g[name], out_d[name] = g.reshape(shape), dl.reshape(shape)
        out_m[name], out_v[name] = nm.reshape(shape), nv.reshape(shape)

    def shards_of(name, l):
        if name == "w_in":
            g = _w_in_from_internal(big["w_in"][l])
            return jnp.stack(jnp.split(g, n_chips, axis=1))
        if name == "mlp_w1":
            return jnp.stack(jnp.split(big["w1"][l], n_chips, axis=1))
        if name == "w_out":
            return big["w_out"][l].reshape(n_chips, d // n_chips, d)
        return big["w2"][l].reshape(n_chips, D_FF // n_chips, d)

    for name in ("w_in", "w_out", "mlp_w1", "mlp_w2"):
        per_chip = jnp.stack([shards_of(name, l) for l in range(depth)], axis=1)
        r, cols = per_chip.shape[2:]
        half_rows = half_l * r
        keep = lax.dynamic_slice_in_dim(per_chip, core * half_l, half_l, axis=1)
        send = lax.dynamic_slice_in_dim(per_chip, (1 - core) * half_l, half_l, axis=1).astype(BF16)
        theirs = _core_swap(send.reshape(n_chips * half_rows, cols), name=f"pair_{name}")
        chip_sum = _add2(keep.reshape(n_chips * half_rows, cols), theirs, out_dtype=BF16,
                         name=f"pairsum_{name}")
        got = _scatter_chips(chip_sum.reshape(n_chips, half_rows, cols), name=f"scatter_{name}")
        half = _sum_slots(got, name=f"sum_{name}")
        both = _by_core(core, half, _core_swap(half, name=f"swap_{name}"), 0)
        run_adamw(name, both[None], w[name].shape)

    small_tree = {n: jnp.stack(small[n]) for n in SMALL_NAMES}
    gathered = _gather_all(_pack_small(small_tree), name="gather_small")
    gs, ds_, ms, vs = _adamw(_pack_small({n: w[n] for n in SMALL_NAMES}),
                             _pack_small({n: mom[n] for n in SMALL_NAMES}),
                             _pack_small({n: var[n] for n in SMALL_NAMES}), gathered,
                             name="adamw_small")
    like = {n: w[n] for n in SMALL_NAMES}
    for tree, packed in ((out_g, gs), (out_d, ds_), (out_m, ms), (out_v, vs)):
        tree.update(_unpack_small(packed, like))

    dmod_mine = jnp.concatenate(dmods, axis=0)
    dmod_all = _gather_all(jnp.zeros((depth, 8, 6 * d), F32).at[:, 0].set(dmod_mine),
                           name="gather_dmod")[:, :, 0]
    dmod_lb = dmod_all.transpose(1, 0, 2)
    dmod_sh = lax.dynamic_slice_in_dim(dmod_lb, chip * ada_sh, ada_sh, axis=2)
    dmod_sh = jnp.concatenate([dmod_sh, jnp.zeros_like(dmod_sh)], axis=1)
    g_ada_w = _ada_bwd(c_pad, dmod_sh, name="ada_bwd")
    run_adamw("ada_w", g_ada_w.reshape(1, depth * d, ada_sh), ada_w.shape)
    parts_b = dmod_all.reshape(8, depth * 6 * d // LANES, LANES)
    run_adamw("ada_b", parts_b, ada_b.shape)

    outs = [loss, grad_x]
    for tree in (out_g, out_d, out_m, out_v):
        outs += [tree[n] for n in WEIGHT_NAMES]
    return tuple(outs)
```

```python
import functools
import math

import jax
import jax.numpy as jnp
from jax import lax
from jax.experimental import pallas as pl
from jax.experimental.pallas import tpu as pltpu

F32 = jnp.float32
BF16 = jnp.bfloat16

D_MODEL = 1024
DEPTH = 4
HEAD_DIM = 64
LANES = 128
D_FF = 4 * D_MODEL
EPS = 1e-6
SB_W, FOX_W, SGU_W = 256, 512, 256
FOX_HEADS = 8
SGU_CHUNK = 128
IN_W = 2824
ATT_W = 3 * SB_W + 3 * FOX_W
PROJ_W = 3072
CB_QA, CB_KA, CB_VA = 0, 2, 4
CB_QB, CB_KB, CB_VB = 6, 10, 14
CB_UC, CB_VC, CB_FL = 18, 20, 22
ATT_T = 256
PREP_ROWS = 2048
SGU_ROWS = 2048
NORM_ROWS = 512
NORM_BWD_ROWS = 512
CUMSUM_ROWS = 512
VMEM_LIMIT = 56 * 2 ** 20
SKIP_LOG = 110.0

ADAM_LR, ADAM_B1, ADAM_B2, ADAM_EPS, ADAM_WD, ADAM_STEP = 0.001, 0.9, 0.999, 1e-08, 0.01, 10

MESH = pl.DeviceIdType.MESH


def _pcall(body, *, name, out_shape, grid=(), in_specs=None, out_specs=None, scratch_shapes=(),
           semantics=None):
    params = dict(vmem_limit_bytes=VMEM_LIMIT)
    if semantics is not None:
        params["dimension_semantics"] = semantics
    kwargs = {}
    if in_specs is not None:
        kwargs["in_specs"] = in_specs
    if out_specs is not None:
        kwargs["out_specs"] = out_specs
    return pl.pallas_call(body, name=name, out_shape=out_shape, grid=grid,
                          scratch_shapes=list(scratch_shapes),
                          compiler_params=pltpu.CompilerParams(**params), **kwargs)


def _dot(a, b):
    return jnp.dot(a, b, preferred_element_type=F32)


def _dot_nt(a, b):
    return lax.dot_general(a, b, (((1,), (1,)), ((), ())), preferred_element_type=F32)


def _dot_tn(a, b):
    return lax.dot_general(a, b, (((0,), (0,)), ((), ())), preferred_element_type=F32)


def _split2(x):
    hi = x.astype(BF16)
    lo = (x - hi.astype(F32)).astype(BF16)
    return hi, lo


def _ones_dot(x, ones_bf16):
    hi, lo = _split2(x)
    return _dot(hi, ones_bf16) + _dot(lo, ones_bf16)


def _rowwise(fn, fulls, vecs, out_dtypes, n_vec_out, *, name, tr):
    s, n = fulls[0].shape
    tr = min(tr, s)
    assert s % tr == 0, (name, s, tr)
    nf, nv, nfo = len(fulls), len(vecs), len(out_dtypes)

    def body(*refs):
        fi, vi = refs[:nf], refs[nf:nf + nv]
        fo, vo = refs[nf + nv:nf + nv + nfo], refs[nf + nv + nfo:]
        outs_f, outs_v = fn([r[...] for r in fi], [r[...] for r in vi])
        for r, o in zip(fo, outs_f):
            r[...] = o.astype(r.dtype)
        if n_vec_out:
            @pl.when(pl.program_id(0) == 0)
            def _():
                for r in vo:
                    r[...] = jnp.zeros_like(r)
            for r, o in zip(vo, outs_v):
                r[...] += o

    full_spec = pl.BlockSpec((tr, n), lambda i: (i, 0))
    vec_specs = [pl.BlockSpec(v.shape, lambda i: (0, 0)) for v in vecs]
    out_vec_spec = pl.BlockSpec((1, n), lambda i: (0, 0))
    out_shape = [jax.ShapeDtypeStruct((s, n), dt) for dt in out_dtypes]
    out_shape += [jax.ShapeDtypeStruct((1, n), F32)] * n_vec_out
    outs = _pcall(body, name=name, grid=(s // tr,),
                  in_specs=[full_spec] * nf + vec_specs,
                  out_specs=[full_spec] * nfo + [out_vec_spec] * n_vec_out,
                  out_shape=out_shape,
                  semantics=("arbitrary",) if n_vec_out else ("parallel",))(*fulls, *vecs)
    return outs[:nfo], outs[nfo:]


def _colsum(t):
    return jnp.sum(t, axis=0, keepdims=True)


def _rms_mod(x, g, sc, sh):
    r = lax.rsqrt(jnp.mean(x * x, axis=-1, keepdims=True) + EPS)
    return (x * r * g) * (1.0 + sc) + sh


def _norm_mod_fwd(x, g, sc, sh, *, name):
    def fn(f, v):
        return [_rms_mod(f[0], v[0], v[1], v[2])], []
    (h,), _ = _rowwise(fn, [x], [g, sc, sh], [BF16], 0, name=name, tr=NORM_ROWS)
    return h


def _resid_norm_mod_fwd(x, m, gate, g, sc, sh, *, name):
    def fn(f, v):
        xn = f[0] + v[0] * f[1]
        return [xn, _rms_mod(xn, v[1], v[2], v[3])], []
    (xn, h), _ = _rowwise(fn, [x, m], [gate, g, sc, sh], [F32, BF16], 0, name=name, tr=NORM_ROWS)
    return xn, h


def _norm_mod_bwd(x, dh, dres, g, sc, gated, *, name):
    def fn(f, v):
        xv, dhv, dr = f[:3]
        gv, scv = v[:2]
        r = lax.rsqrt(jnp.mean(xv * xv, axis=-1, keepdims=True) + EPS)
        xh = xv * r
        dn = dhv * (1.0 + scv)
        dxh = dn * gv
        dx = dr + r * (dxh - xh * jnp.mean(dxh * xh, axis=-1, keepdims=True))
        sums = [_colsum(dn * xh), _colsum(dhv * (xh * gv)), _colsum(dhv)]
        if gated is None:
            return [dx], sums
        return [dx, dx * v[2]], sums + [_colsum(dx * f[3])]
    if gated is None:
        (dx,), (dg, dsc, dsh) = _rowwise(fn, [x, dh, dres], [g, sc], [F32], 3, name=name,
                                         tr=NORM_BWD_ROWS)
        return dx, dg, dsc, dsh, None, None
    (dx, dm), (dg, dsc, dsh, dgate) = _rowwise(fn, [x, dh, dres, gated[0]], [g, sc, gated[1]],
                                               [F32, BF16], 4, name=name, tr=NORM_BWD_ROWS)
    return dx, dg, dsc, dsh, dm, dgate


def _loss_fwd_bwd(x, m, gate, target, *, name):
    n = x.shape[1]

    def fn(f, v):
        err = f[0] + v[0] * f[1] - f[2]
        dy = err * (1.0 / n)
        return [dy, dy * v[0]], [_colsum(err * err), _colsum(dy * f[1])]
    (dy, dm), (sq, dgate) = _rowwise(fn, [x, m, target], [gate], [F32, BF16], 2, name=name,
                                     tr=NORM_BWD_ROWS)
    return sq, dy, dm, dgate


def _matmul(a, b, *, name, ta=False, tb=False, out_dtype=F32, relu2=None, pre_act=None, copy=None,
            tm=1024, tn=1024, tk_max=2048):
    m = a.shape[1] if ta else a.shape[0]
    k = a.shape[0] if ta else a.shape[1]
    n = b.shape[0] if tb else b.shape[1]
    assert k == (b.shape[1] if tb else b.shape[0])
    tk = max(dd for dd in range(LANES, min(tk_max, k) + 1, LANES) if k % dd == 0) if k > LANES else k
    tm, tn = min(tm, m), min(tn, n)
    assert m % tm == 0 and n % tn == 0 and k % tk == 0, (name, m, n, k)
    nk = k // tk
    dims = (((0 if ta else 1,), (1 if tb else 0,)), ((), ()))

    plain = relu2 is None and pre_act is None and copy is None
    second = relu2 if relu2 is not None else copy
    in_place = plain and out_dtype == F32
    n_in = 2 + (pre_act is not None)

    def body(*refs):
        a_ref, b_ref = refs[:2]
        o_ref = refs[n_in]
        prod = lax.dot_general(a_ref[...].astype(BF16), b_ref[...].astype(BF16), dims,
                               preferred_element_type=F32)

        def finish(acc):
            if pre_act is not None:
                acc = acc * (2.0 * jnp.maximum(refs[2][...].astype(F32), 0.0))
            o_ref[...] = acc.astype(o_ref.dtype)
            if relu2 is not None:
                r = jnp.maximum(acc, 0.0)
                refs[n_in + 1][...] = (r * r).astype(relu2)
            if copy is not None:
                refs[n_in + 1][...] = acc.astype(copy)

        if nk == 1:
            finish(prod)
            return
        kk = pl.program_id(2)
        acc_ref = o_ref if in_place else refs[-1]

        @pl.when(kk == 0)
        def _():
            acc_ref[...] = prod

        @pl.when(kk > 0)
        def _():
            acc_ref[...] += prod

        if not in_place:
            @pl.when(kk == nk - 1)
            def _():
                finish(acc_ref[...])

    a_spec = (pl.BlockSpec((tk, tm), lambda i, j, kk: (kk, i)) if ta
              else pl.BlockSpec((tm, tk), lambda i, j, kk: (i, kk)))
    b_spec = (pl.BlockSpec((tn, tk), lambda i, j, kk: (j, kk)) if tb
              else pl.BlockSpec((tk, tn), lambda i, j, kk: (kk, j)))
    out_spec = pl.BlockSpec((tm, tn), lambda i, j, kk: (i, j))
    in_specs, args = [a_spec, b_spec], [a, b]
    if pre_act is not None:
        in_specs.append(out_spec)
        args.append(pre_act)
    out_specs, out_shape = out_spec, jax.ShapeDtypeStruct((m, n), out_dtype)
    if second is not None:
        out_specs, out_shape = [out_spec] * 2, [out_shape, jax.ShapeDtypeStruct((m, n), second)]
    return _pcall(body, name=name, grid=(m // tm, n // tn, nk),
                  in_specs=in_specs, out_specs=out_specs, out_shape=out_shape,
                  scratch_shapes=[] if nk == 1 or in_place else [pltpu.VMEM((tm, tn), F32)],
                  semantics=("parallel", "parallel", "arbitrary"))(*args)


def _lane_masks():
    lane = lax.broadcasted_iota(jnp.int32, (1, LANES), 1)
    return [lane < HEAD_DIM, lane >= HEAD_DIM]


def _tri_iotas(t):
    r = lax.broadcasted_iota(jnp.int32, (t, t), 0)
    c = lax.broadcasted_iota(jnp.int32, (t, t), 1)
    return r, c


def _rows(j, t):
    return pl.ds(pl.multiple_of(j * t, t), t)


def _neg_softplus(z):
    e = jnp.exp(-jnp.abs(z))
    return -(jnp.maximum(z, 0.0) + jnp.log(1.0 + e)), e


def _sb_fwd(proj, *, name):
    s = proj.shape[0]
    t = min(ATT_T, s)
    scale = HEAD_DIM ** -0.5

    def body(q_ref, k_ref, v_ref, o_ref, ltot_ref, stop_ref):
        i = pl.program_id(1)
        hm = _lane_masks()
        q = q_ref[...].astype(F32) * scale
        qh = [jnp.where(mk, q, 0.0).astype(BF16) for mk in hm]
        r, c = _tri_iotas(t)
        later = (r > c).astype(BF16)
        q2 = jnp.concatenate(qh, axis=0)
        causal2 = jnp.concatenate([c < r, c < r], axis=0)

        def scores(j):
            return _dot_nt(q2, k_ref[_rows(j, t), :].astype(BF16))

        def chunk(j, carry, z, masked):
            e_run, acc = carry
            vb = v_ref[_rows(j, t), :].astype(BF16)
            l, _ = _neg_softplus(z)
            if masked:
                l = jnp.where(causal2, l, 0.0)
            between = _ones_dot(l, later) + e_run
            a = jnp.exp(z + l + between)
            if masked:
                a = jnp.where(causal2, a, 0.0)
            return e_run + jnp.sum(l, axis=1, keepdims=True), acc + _dot(a.astype(BF16), vb)

        init = (jnp.zeros((2 * t, 1), F32), jnp.zeros((2 * t, LANES), F32))
        carry = chunk(i, init, scores(i), True)

        def step(st):
            j, cr, z = st
            z_next = scores(jnp.maximum(j - 1, 0))
            return j - 1, chunk(j, cr, z, False), z_next

        j_stop, (e_tot, acc), _ = lax.while_loop(
            lambda st: (st[0] >= 0) & (jnp.max(st[1][0]) > -SKIP_LOG), step,
            (i - 1, carry, scores(jnp.maximum(i - 1, 0))))
        o_ref[...] = jnp.where(hm[0], acc[:t], acc[t:])
        ltot_ref[...] = jnp.where(hm[0], e_tot[:t], e_tot[t:])
        stop_ref[...] = jnp.full(stop_ref.shape, j_stop.astype(F32), F32)

    blk = lambda cb: pl.BlockSpec((t, LANES), lambda p, i: (i, cb + p))
    full = lambda cb: pl.BlockSpec((s, LANES), lambda p, i: (0, cb + p))
    out_blk = pl.BlockSpec((t, LANES), lambda p, i: (i, p))
    n_pairs = SB_W // LANES
    return _pcall(body, name=name, grid=(n_pairs, s // t),
                  in_specs=[blk(CB_QA), full(CB_KA), full(CB_VA)],
                  out_specs=[out_blk, out_blk,
                             pl.BlockSpec((1, 1, 8, LANES), lambda p, i: (p, i, 0, 0))],
                  out_shape=[jax.ShapeDtypeStruct((s, SB_W), F32)] * 2
                  + [jax.ShapeDtypeStruct((n_pairs, s // t, 8, LANES), F32)],
                  semantics=("parallel", "arbitrary"))(proj, proj, proj)


def _sb_bwd(proj, dmixed, ltot, stop, *, name):
    s = proj.shape[0]
    t = min(ATT_T, s)
    scale = HEAD_DIM ** -0.5

    def body(q_ref, k_ref, v_ref, do_ref, ltot_ref, stop_ref, dq_ref, dk_ref, dv_ref):
        i = pl.program_id(1)

        @pl.when(i == 0)
        def _():
            dk_ref[...] = jnp.zeros_like(dk_ref)
            dv_ref[...] = jnp.zeros_like(dv_ref)

        hm = _lane_masks()
        q = q_ref[...].astype(F32) * scale
        do = do_ref[...]
        qh = [jnp.where(mk, q, 0.0).astype(BF16) for mk in hm]
        doh = [jnp.where(mk, do, 0.0).astype(BF16) for mk in hm]
        r, c = _tri_iotas(t)
        upto = (r <= c).astype(BF16)
        before = (r < c).astype(BF16)
        q2 = jnp.concatenate(qh, axis=0)
        do2 = jnp.concatenate(doh, axis=0)
        causal2 = jnp.concatenate([c < r, c < r], axis=0)

        j_stop = jnp.clip(jnp.max(stop_ref[...]).astype(jnp.int32), -1, i - 1)
        ltv = ltot_ref[...]
        lt = jnp.concatenate([ltv[:, 0:1], ltv[:, HEAD_DIM:HEAD_DIM + 1]], axis=0)

        def products(j):
            return (_dot_nt(q2, k_ref[_rows(j, t), :].astype(BF16)),
                    _dot_nt(do2, v_ref[_rows(j, t), :].astype(BF16)))

        def chunk(j, carry, z, da, masked):
            l_run, g_run, dq = carry
            l, e = _neg_softplus(z)
            beta = jnp.where(z >= 0.0, 1.0, e) / (1.0 + e)
            if masked:
                l = jnp.where(causal2, l, 0.0)
            prefix = _ones_dot(l, upto) + l_run
            a = jnp.exp(z + l + (lt - prefix))
            if masked:
                a = jnp.where(causal2, a, 0.0)
            g = a * da
            g_before = _ones_dot(g, before) + g_run
            dz = g * (1.0 - beta) - beta * g_before
            if masked:
                dz = jnp.where(causal2, dz, 0.0)
            dzb = dz.astype(BF16)
            dk_ref[_rows(j, t), :] += _dot_tn(dzb, q2)
            dv_ref[_rows(j, t), :] += _dot_tn(a.astype(BF16), do2)
            return (l_run + jnp.sum(l, axis=1, keepdims=True),
                    g_run + jnp.sum(g, axis=1, keepdims=True),
                    dq + _dot(dzb, k_ref[_rows(j, t), :].astype(BF16)))

        init = (jnp.zeros((2 * t, 1), F32), jnp.zeros((2 * t, 1), F32),
                jnp.zeros((2 * t, LANES), F32))
        carry = lax.fori_loop(j_stop + 1, i,
                              lambda j, cr: chunk(j, cr, *products(j), False), init)
        dq2 = chunk(i, carry, *products(i), True)[2]
        dq_ref[...] = jnp.where(hm[0], dq2[:t], dq2[t:]) * scale

    blk = lambda cb: pl.BlockSpec((t, LANES), lambda p, i: (i, cb + p))
    full = lambda cb: pl.BlockSpec((s, LANES), lambda p, i: (0, cb + p))
    out_blk = pl.BlockSpec((t, LANES), lambda p, i: (i, p))
    out_full = pl.BlockSpec((s, LANES), lambda p, i: (0, p))
    return _pcall(body, name=name, grid=(SB_W // LANES, s // t),
                  in_specs=[blk(CB_QA), full(CB_KA), full(CB_VA), blk(0), out_blk,
                            pl.BlockSpec((1, 1, 8, LANES), lambda p, i: (p, i, 0, 0))],
                  out_specs=[out_blk, out_full, out_full],
                  out_shape=[jax.ShapeDtypeStruct((s, SB_W), F32)] * 3,
                  semantics=("parallel", "arbitrary"))(proj, proj, proj, dmixed, ltot, stop)


def _group_mean(v, lo):
    s0 = jnp.sum(jnp.where(lo, v, 0.0), axis=1, keepdims=True)
    s1 = jnp.sum(jnp.where(lo, 0.0, v), axis=1, keepdims=True)
    return jnp.where(lo, s0, s1) * (1.0 / HEAD_DIM)


def _fox_prep_fwd(proj, qg, kg, *, name):
    s = proj.shape[0]
    tr = min(PREP_ROWS, s)

    def body(q_ref, k_ref, qg_ref, kg_ref, qn_ref, kn_ref, kmax_ref):
        lo = _lane_masks()[0]
        normed = []
        for x_ref, g_ref, o_ref in ((q_ref, qg_ref, qn_ref), (k_ref, kg_ref, kn_ref)):
            x = x_ref[...]
            normed.append(x * lax.rsqrt(_group_mean(x * x, lo) + EPS) * g_ref[...])
            o_ref[...] = normed[-1].astype(o_ref.dtype)

        @pl.when(pl.program_id(1) == 0)
        def _():
            kmax_ref[...] = jnp.zeros_like(kmax_ref)
        kn = normed[1]
        norms = jnp.sqrt(_group_mean(kn * kn, lo) * HEAD_DIM)
        kmax_ref[...] = jnp.maximum(kmax_ref[...], jnp.max(norms, axis=0, keepdims=True))

    blk = lambda cb: pl.BlockSpec((tr, LANES), lambda p, i: (i, cb + p))
    vec = pl.BlockSpec((1, LANES), lambda p, i: (0, 0))
    out_blk = pl.BlockSpec((tr, LANES), lambda p, i: (i, p))
    return _pcall(body, name=name, grid=(FOX_W // LANES, s // tr),
                  in_specs=[blk(CB_QB), blk(CB_KB), vec, vec],
                  out_specs=[out_blk, out_blk, pl.BlockSpec((1, LANES), lambda p, i: (0, p))],
                  out_shape=[jax.ShapeDtypeStruct((s, FOX_W), BF16)] * 2
                  + [jax.ShapeDtypeStruct((1, FOX_W), F32)],
                  semantics=("parallel", "arbitrary"))(proj, proj, qg, kg)


def _fox_prep_bwd(proj, dqn, dkn, qg, kg, *, name):
    s = proj.shape[0]
    tr = min(PREP_ROWS, s)

    def body(q_ref, k_ref, dqn_ref, dkn_ref, qg_ref, kg_ref, dq_ref, dk_ref, dqg_ref, dkg_ref):
        @pl.when((pl.program_id(0) == 0) & (pl.program_id(1) == 0))
        def _():
            dqg_ref[...] = jnp.zeros_like(dqg_ref)
            dkg_ref[...] = jnp.zeros_like(dkg_ref)

        lo = _lane_masks()[0]
        for x_ref, dy_ref, g_ref, dx_ref, dg_ref in ((q_ref, dqn_ref, qg_ref, dq_ref, dqg_ref),
                                                     (k_ref, dkn_ref, kg_ref, dk_ref, dkg_ref)):
            x, dy = x_ref[...], dy_ref[...]
            r = lax.rsqrt(_group_mean(x * x, lo) + EPS)
            xh = x * r
            dxh = dy * g_ref[...]
            dx_ref[...] = r * (dxh - xh * _group_mean(dxh * xh, lo))
            dg_ref[...] += _colsum(dy * xh)

    blk = lambda cb: pl.BlockSpec((tr, LANES), lambda p, i: (i, cb + p))
    vec = pl.BlockSpec((1, LANES), lambda p, i: (0, 0))
    out_blk = pl.BlockSpec((tr, LANES), lambda p, i: (i, p))
    return _pcall(body, name=name, grid=(FOX_W // LANES, s // tr),
                  in_specs=[blk(CB_QB), blk(CB_KB), out_blk, out_blk, vec, vec],
                  out_specs=[out_blk, out_blk, vec, vec],
                  out_shape=[jax.ShapeDtypeStruct((s, FOX_W), F32)] * 2
                  + [jax.ShapeDtypeStruct((1, LANES), F32)] * 2,
                  semantics=("arbitrary", "arbitrary"))(proj, proj, dqn, dkn, qg, kg)


def _split3_dot(tri_bf16, x):
    hi = x.astype(BF16)
    r1 = x - hi.astype(F32)
    mid = r1.astype(BF16)
    lo = (r1 - mid.astype(F32)).astype(BF16)
    return _dot(tri_bf16, hi) + _dot(tri_bf16, mid) + _dot(tri_bf16, lo)


def _forget_cumsum_fwd(proj, b_pad, *, name):
    s = proj.shape[0]
    tb = min(CUMSUM_ROWS, s)

    def body(fl_ref, b_ref, cf_ref, run_ref):
        @pl.when(pl.program_id(0) == 0)
        def _():
            run_ref[...] = jnp.zeros_like(run_ref)
        lf, _ = _neg_softplus(-(fl_ref[...] + b_ref[...]))
        r, c = _tri_iotas(tb)
        incl = _split3_dot((c <= r).astype(BF16), lf) + run_ref[...]
        cf_ref[...] = incl
        run_ref[...] = incl[tb - 1:tb, :]

    return _pcall(body, name=name, grid=(s // tb,),
                  in_specs=[pl.BlockSpec((tb, LANES), lambda i: (i, CB_FL)),
                            pl.BlockSpec((1, LANES), lambda i: (0, 0))],
                  out_specs=pl.BlockSpec((tb, LANES), lambda i: (i, 0)),
                  out_shape=jax.ShapeDtypeStruct((s, LANES), F32),
                  scratch_shapes=[pltpu.VMEM((1, LANES), F32)],
                  semantics=("arbitrary",))(proj, b_pad)


def _forget_cumsum_bwd(proj, b_pad, dcf, *, name):
    s = proj.shape[0]
    tb = min(CUMSUM_ROWS, s)
    nb = s // tb

    def body(fl_ref, b_ref, dcf_ref, dfl_ref, db_ref, run_ref):
        @pl.when(pl.program_id(0) == 0)
        def _():
            run_ref[...] = jnp.zeros_like(run_ref)
            db_ref[...] = jnp.zeros_like(db_ref)
        r, c = _tri_iotas(tb)
        dlf = _split3_dot((c >= r).astype(BF16), dcf_ref[...]) + run_ref[...]
        run_ref[...] = dlf[0:1, :]
        xv = fl_ref[...] + b_ref[...]
        e = jnp.exp(-jnp.abs(xv))
        sig_neg = jnp.where(xv >= 0.0, e, 1.0) / (1.0 + e)
        dfl = dlf * sig_neg
        dfl_ref[...] = dfl
        db_ref[...] += _colsum(dfl)

    return _pcall(body, name=name, grid=(nb,),
                  in_specs=[pl.BlockSpec((tb, LANES), lambda i: (nb - 1 - i, CB_FL)),
                            pl.BlockSpec((1, LANES), lambda i: (0, 0)),
                            pl.BlockSpec((tb, LANES), lambda i: (nb - 1 - i, 0))],
                  out_specs=[pl.BlockSpec((tb, LANES), lambda i: (nb - 1 - i, 0)),
                             pl.BlockSpec((1, LANES), lambda i: (0, 0))],
                  out_shape=[jax.ShapeDtypeStruct((s, LANES), F32),
                             jax.ShapeDtypeStruct((1, LANES), F32)],
                  scratch_shapes=[pltpu.VMEM((1, LANES), F32)],
                  semantics=("arbitrary",))(proj, b_pad, dcf)


def _fox_bias_q(cfc, p, h):
    lane = lax.broadcasted_iota(jnp.int32, (1, LANES), 1)
    return jnp.sum(jnp.where(lane == 2 * p + h, cfc, 0.0), axis=1, keepdims=True)


def _fox_score_bound(q, kmax_row, hm):
    out = []
    for h in range(2):
        qnorm = jnp.sqrt(jnp.sum(jnp.where(hm[h], q * q, 0.0), axis=1, keepdims=True))
        out.append(1.02 * qnorm * kmax_row[:, h * HEAD_DIM:h * HEAD_DIM + 1])
    return out


def _fox_live(cfr_ref, j, t, tops):
    jc = jnp.maximum(j, 0)
    worst = []
    for h in range(2):
        cf_min = jnp.min(cfr_ref[0, pl.ds(h, 1), _rows(jc, t)], axis=1, keepdims=True)
        worst.append(jnp.max(tops[h] - cf_min))
    return (j >= 0) & (jnp.maximum(worst[0], worst[1]) > -SKIP_LOG)


def _chip_gather_copies(x_refs, out_refs, send_sems, recv_sems, local_sems):
    ids = (lax.axis_index("x"), lax.axis_index("y"), lax.axis_index("c"))
    chip = 2 * ids[0] + ids[1]
    copies = []
    for n, (x_ref, out_ref) in enumerate(zip(x_refs, out_refs)):
        copies.append(pltpu.make_async_copy(x_ref, out_ref.at[chip], local_sems.at[n]))
        for kk, (flip_x, flip_y) in enumerate(((1, 0), (0, 1), (1, 1))):
            peer = (1 - ids[0] if flip_x else ids[0], 1 - ids[1] if flip_y else ids[1], ids[2])
            copies.append(pltpu.make_async_remote_copy(
                src_ref=x_ref, dst_ref=out_ref.at[chip],
                send_sem=send_sems.at[3 * n + kk], recv_sem=recv_sems.at[3 * n + kk],
                device_id=peer, device_id_type=MESH))
    return copies


def _fox_fwd(proj, qn, kn, cf, cf_rows, kmax, *, name, gathers=()):
    s = proj.shape[0]
    t = min(ATT_T, s)
    scale = HEAD_DIM ** -0.5
    n_pairs, nq, ng = FOX_W // LANES, s // t, len(gathers)

    def body(*refs):
        q_ref, k_ref, v_ref, cfc_ref, cfr_ref, kmax_ref = refs[:6]
        o_ref, lse_ref = refs[6 + ng:8 + ng]
        p, i = pl.program_id(0), pl.program_id(1)
        if ng:
            def copies():
                return _chip_gather_copies(refs[6:6 + ng], refs[8 + ng:8 + 2 * ng], *refs[8 + 2 * ng:])

            @pl.when((p == 0) & (i == 0))
            def _():
                for cp in copies():
                    cp.start()
        hm = _lane_masks()
        q = q_ref[...].astype(F32) * scale
        qh = [jnp.where(mk, q, 0.0).astype(BF16) for mk in hm]
        cfc = cfc_ref[...]
        bq = [_fox_bias_q(cfc, p, h) for h in range(2)]
        qk_top = _fox_score_bound(q, kmax_ref[...], hm)
        r, c = _tri_iotas(t)
        causal = c <= r

        q2 = jnp.concatenate(qh, axis=0)
        causal2 = jnp.concatenate([causal, causal], axis=0)

        def scores(j):
            return _dot_nt(q2, k_ref[_rows(j, t), :].astype(BF16))

        def chunk(j, carry, z2, masked):
            m_run, l_run, acc = carry
            vb = v_ref[_rows(j, t), :].astype(BF16)
            z = jnp.concatenate(
                [z2[h * t:(h + 1) * t] + (bq[h] - cfr_ref[0, pl.ds(h, 1), _rows(j, t)])
                 for h in range(2)], axis=0)
            if masked:
                z = jnp.where(causal2, z, -1e30)
            m_new = jnp.maximum(m_run, jnp.max(z, axis=1, keepdims=True))
            alpha = jnp.exp(m_run - m_new)
            pr = jnp.exp(z - m_new)
            return (m_new, alpha * l_run + jnp.sum(pr, axis=1, keepdims=True),
                    alpha * acc + _dot(pr.astype(BF16), vb))

        init = (jnp.full((2 * t, 1), -1e30, F32), jnp.zeros((2 * t, 1), F32),
                jnp.zeros((2 * t, LANES), F32))
        carry = chunk(i, init, scores(i), True)

        def live(j, cr):
            return _fox_live(cfr_ref, j, t,
                             [qk_top[h] + bq[h] - cr[0][h * t:(h + 1) * t] for h in range(2)])

        def step(st):
            j, _, cr, z2 = st
            z2_next = scores(jnp.maximum(j - 1, 0))
            cr = chunk(j, cr, z2, False)
            return j - 1, live(j - 1, cr), cr, z2_next

        m_fin, l_fin, acc = lax.while_loop(
            lambda st: st[1], step,
            (i - 1, live(i - 1, carry), carry, scores(jnp.maximum(i - 1, 0))))[2]
        o2 = acc / l_fin
        lse2 = m_fin + jnp.log(l_fin)
        o_ref[...] = jnp.where(hm[0], o2[:t], o2[t:])
        lse_ref[...] = jnp.where(hm[0], lse2[:t], lse2[t:])
        if ng:
            @pl.when((p == n_pairs - 1) & (i == nq - 1))
            def _():
                for cp in copies():
                    cp.wait()

    blk = pl.BlockSpec((t, LANES), lambda p, i: (i, p))
    full = pl.BlockSpec((s, LANES), lambda p, i: (0, p))
    any_spec = pl.BlockSpec(memory_space=pl.ANY)
    dma = pltpu.SemaphoreType.DMA
    return _pcall(body, name=name, grid=(n_pairs, nq),
                  in_specs=[blk, full, pl.BlockSpec((s, LANES), lambda p, i: (0, CB_VB + p)),
                            pl.BlockSpec((t, LANES), lambda p, i: (i, 0)),
                            pl.BlockSpec((1, 2, s), lambda p, i: (p, 0, 0)),
                            pl.BlockSpec((1, LANES), lambda p, i: (0, p))] + [any_spec] * ng,
                  out_specs=[blk, blk] + [any_spec] * ng,
                  out_shape=[jax.ShapeDtypeStruct((s, FOX_W), F32)] * 2
                  + [jax.ShapeDtypeStruct((4,) + g.shape, g.dtype) for g in gathers],
                  scratch_shapes=[dma((3 * ng,)), dma((3 * ng,)), dma((ng,))] if ng else [],
                  semantics=("arbitrary", "arbitrary") if ng else ("parallel", "arbitrary"))(
                      qn, kn, proj, cf, cf_rows, kmax, *gathers)


def _fox_bwd(proj, qn, kn, cf, cf_rows, kmax, do, o, lse, *, name):
    s = proj.shape[0]
    t = min(ATT_T, s)
    scale = HEAD_DIM ** -0.5

    def body(q_ref, k_ref, v_ref, cfc_ref, cfr_ref, kmax_ref, do_ref, o_ref, lse_ref,
             dq_ref, dk_ref, dv_ref, dcf_ref, dcfq_ref):
        p, i = pl.program_id(0), pl.program_id(1)

        @pl.when(i == 0)
        def _():
            dk_ref[...] = jnp.zeros_like(dk_ref)
            dv_ref[...] = jnp.zeros_like(dv_ref)
            dcf_ref[...] = jnp.zeros_like(dcf_ref)

        hm = _lane_masks()
        q = q_ref[...].astype(F32) * scale
        do = do_ref[...]
        dov = do * o_ref[...]
        qh = [jnp.where(mk, q, 0.0).astype(BF16) for mk in hm]
        doh = [jnp.where(mk, do, 0.0).astype(BF16) for mk in hm]
        delta = [jnp.sum(jnp.where(mk, dov, 0.0), axis=1, keepdims=True) for mk in hm]
        lsev = lse_ref[...]
        lse = [lsev[:, 0:1], lsev[:, HEAD_DIM:HEAD_DIM + 1]]
        cfc = cfc_ref[...]
        bq = [_fox_bias_q(cfc, p, h) - lse[h] for h in range(2)]
        qk_top = _fox_score_bound(q, kmax_ref[...], hm)
        tops = [qk_top[h] + bq[h] for h in range(2)]
        r, c = _tri_iotas(t)
        j_stop = lax.while_loop(lambda st: st[1],
                                lambda st: (st[0] - 1, _fox_live(cfr_ref, st[0] - 1, t, tops)),
                                (i - 1, _fox_live(cfr_ref, i - 1, t, tops)))[0]
        q2 = jnp.concatenate(qh, axis=0)
        do2 = jnp.concatenate(doh, axis=0)
        delta2 = jnp.concatenate(delta, axis=0)
        causal2 = jnp.concatenate([c <= r, c <= r], axis=0)

        def products(j):
            return (_dot_nt(q2, k_ref[_rows(j, t), :].astype(BF16)),
                    _dot_nt(do2, v_ref[_rows(j, t), :].astype(BF16)))

        def chunk(j, carry, z2, dp, masked):
            dq, row_sum = carry
            z = jnp.concatenate(
                [z2[h * t:(h + 1) * t] + (bq[h] - cfr_ref[0, pl.ds(h, 1), _rows(j, t)])
                 for h in range(2)], axis=0)
            pr = jnp.exp(z)
            if masked:
                pr = jnp.where(causal2, pr, 0.0)
            ds = pr * (dp - delta2)
            dsb = ds.astype(BF16)
            dk_ref[_rows(j, t), :] += _dot_tn(dsb, q2)
            dv_ref[_rows(j, t), :] += _dot_tn(pr.astype(BF16), do2)
            for h in range(2):
                dcf_ref[0, pl.ds(h, 1), _rows(j, t)] -= jnp.sum(ds[h * t:(h + 1) * t], axis=0,
                                                               keepdims=True)
            return (dq + _dot(dsb, k_ref[_rows(j, t), :].astype(BF16)),
                    row_sum + jnp.sum(ds, axis=1, keepdims=True))

        def one(j, cr):
            return chunk(j, cr, *products(j), False)

        init = (jnp.zeros((2 * t, LANES), F32), jnp.zeros((2 * t, 1), F32))
        first, odd = j_stop + 1, (i - j_stop - 1) % 2
        carry = lax.cond(odd == 1, lambda cr: one(first, cr), lambda cr: cr, init)
        carry = lax.fori_loop(0, (i - first) // 2,
                              lambda n, cr: one(first + odd + 2 * n + 1,
                                                one(first + odd + 2 * n, cr)), carry)
        dq2, row_sum = chunk(i, carry, *products(i), True)
        dq_ref[...] = jnp.where(hm[0], dq2[:t], dq2[t:]) * scale
        dcfq_ref[...] = jnp.where(hm[0], row_sum[:t], row_sum[t:])

    blk = pl.BlockSpec((t, LANES), lambda p, i: (i, p))
    full = pl.BlockSpec((s, LANES), lambda p, i: (0, p))
    rows = pl.BlockSpec((1, 2, s), lambda p, i: (p, 0, 0))
    return _pcall(body, name=name, grid=(FOX_W // LANES, s // t),
                  in_specs=[blk, full, pl.BlockSpec((s, LANES), lambda p, i: (0, CB_VB + p)),
                            pl.BlockSpec((t, LANES), lambda p, i: (i, 0)), rows,
                            pl.BlockSpec((1, LANES), lambda p, i: (0, p)),
                            pl.BlockSpec((t, LANES), lambda p, i: (i, SB_W // LANES + p)),
                            blk, blk],
                  out_specs=[blk, full, full, rows, blk],
                  out_shape=[jax.ShapeDtypeStruct((s, FOX_W), F32)] * 3
                  + [jax.ShapeDtypeStruct((FOX_W // LANES, 2, s), F32),
                     jax.ShapeDtypeStruct((s, FOX_W), F32)],
                  semantics=("parallel", "arbitrary"))(qn, kn, proj, cf, cf_rows, kmax, do, o, lse)


_GELU_C0 = math.sqrt(2.0 / math.pi)
_GELU_C1 = 0.044715


def _gelu(x):
    th = jnp.tanh(_GELU_C0 * (x + _GELU_C1 * (x * x * x)))
    return 0.5 * x * (1.0 + th), th


def _gelu_grad(x, th):
    return 0.5 * (1.0 + th) + 0.5 * x * (1.0 - th * th) * (_GELU_C0 * (1.0 + 3.0 * _GELU_C1 * x * x))


def _sgu_mix(wm, vn_c, lo, bcol):
    return jnp.where(lo, _dot(wm[0], vn_c) + bcol[0], _dot(wm[1], vn_c) + bcol[1])


def _sgu_fwd(proj, w, b_cols, gn, *, name):
    s = proj.shape[0]
    tr = min(SGU_ROWS, s)
    ch = SGU_CHUNK

    def body(u_ref, v_ref, w_ref, b_ref, gn_ref, o_ref):
        lo = _lane_masks()[0]
        r, c = _tri_iotas(ch)
        wm = [jnp.where(c <= r, w_ref[h], 0.0).astype(BF16) for h in range(2)]
        bcol = [b_ref[0, :, h:h + 1] for h in range(2)]
        for n in range(tr // ch):
            rows = slice(n * ch, (n + 1) * ch)
            u, _ = _gelu(u_ref[rows, :])
            vg, _ = _gelu(v_ref[rows, :])
            vn = vg * lax.rsqrt(_group_mean(vg * vg, lo) + EPS) * gn_ref[0]
            o_ref[rows, :] = u * _sgu_mix(wm, vn.astype(BF16), lo, bcol)

    blk = lambda cb: pl.BlockSpec((tr, LANES), lambda p, i: (i, cb + p))
    return _pcall(body, name=name, grid=(SGU_W // LANES, s // tr),
                  in_specs=[blk(CB_UC), blk(CB_VC),
                            pl.BlockSpec((2, ch, ch), lambda p, i: (p, 0, 0)),
                            pl.BlockSpec((1, ch, 2), lambda p, i: (p, 0, 0)),
                            pl.BlockSpec((1, 1, LANES), lambda p, i: (p, 0, 0))],
                  out_specs=pl.BlockSpec((tr, LANES), lambda p, i: (i, p)),
                  out_shape=jax.ShapeDtypeStruct((s, SGU_W), F32),
                  semantics=("parallel", "parallel"))(proj, proj, w, b_cols, gn)


def _sgu_bwd(proj, dmixed, w, w_t, b_cols, gn, *, name):
    s = proj.shape[0]
    tr = min(SGU_ROWS, s)
    ch = SGU_CHUNK
    cb_do = (SB_W + FOX_W) // LANES

    def body(u_ref, v_ref, do_ref, w_ref, wt_ref, b_ref, gn_ref,
             du_ref, dv_ref, dw_ref, db_ref, dgn_ref):
        @pl.when(pl.program_id(1) == 0)
        def _():
            dw_ref[...] = jnp.zeros_like(dw_ref)
            db_ref[...] = jnp.zeros_like(db_ref)
            dgn_ref[...] = jnp.zeros_like(dgn_ref)

        hm = _lane_masks()
        lo = hm[0]
        r, c = _tri_iotas(ch)
        wm = [jnp.where(c <= r, w_ref[h], 0.0).astype(BF16) for h in range(2)]
        wtm = [jnp.where(r <= c, wt_ref[h], 0.0).astype(BF16) for h in range(2)]
        bcol = [b_ref[0, :, h:h + 1] for h in range(2)]
        gnv = gn_ref[0]
        for n in range(tr // ch):
            rows = slice(n * ch, (n + 1) * ch)
            uc, vc, do = u_ref[rows, :], v_ref[rows, :], do_ref[rows, :]
            u, thu = _gelu(uc)
            vg, thv = _gelu(vc)
            rinv = lax.rsqrt(_group_mean(vg * vg, lo) + EPS)
            xh = vg * rinv
            vnb = (xh * gnv).astype(BF16)
            mix = _sgu_mix(wm, vnb, lo, bcol)
            du_ref[rows, :] = do * mix * _gelu_grad(uc, thu)
            dm = do * u
            dmb = dm.astype(BF16)
            dvn = jnp.where(lo, _dot(wtm[0], dmb), _dot(wtm[1], dmb))
            for h in range(2):
                dmh = jnp.where(hm[h], dm, 0.0)
                dw_ref[h] += jnp.where(c <= r, _dot_nt(dmh.astype(BF16), vnb), 0.0)
                db_ref[0, :, h:h + 1] += jnp.sum(dmh, axis=1, keepdims=True)
            dgn_ref[0] += _colsum(dvn * xh)
            dxh = dvn * gnv
            dvg = rinv * (dxh - xh * _group_mean(dxh * xh, lo))
            dv_ref[rows, :] = dvg * _gelu_grad(vc, thv)

    blk = lambda cb: pl.BlockSpec((tr, LANES), lambda p, i: (i, cb + p))
    w_spec = pl.BlockSpec((2, ch, ch), lambda p, i: (p, 0, 0))
    b_spec = pl.BlockSpec((1, ch, 2), lambda p, i: (p, 0, 0))
    g_spec = pl.BlockSpec((1, 1, LANES), lambda p, i: (p, 0, 0))
    out_blk = pl.BlockSpec((tr, LANES), lambda p, i: (i, p))
    return _pcall(body, name=name, grid=(SGU_W // LANES, s // tr),
                  in_specs=[blk(CB_UC), blk(CB_VC), blk(cb_do), w_spec, w_spec, b_spec, g_spec],
                  out_specs=[out_blk, out_blk, w_spec, b_spec, g_spec],
                  out_shape=[jax.ShapeDtypeStruct((s, SGU_W), F32)] * 2
                  + [jax.ShapeDtypeStruct(w.shape, F32), jax.ShapeDtypeStruct(b_cols.shape, F32),
                     jax.ShapeDtypeStruct(gn.shape, F32)],
                  semantics=("parallel", "arbitrary"))(proj, proj, dmixed, w, w_t, b_cols, gn)


def _pad_lanes(v):
    return jnp.zeros((1, LANES), F32).at[0, :v.shape[0]].set(v)


def _small_views(sm):
    return dict(
        n1=sm["norm1_g"][None, :], n2=sm["norm2_g"][None, :],
        b_pad=_pad_lanes(sm["b_forget"]),
        qg=jnp.tile(sm["q_norm_g"], 2)[None, :], kg=jnp.tile(sm["k_norm_g"], 2)[None, :],
        gn=sm["sgu_norm_g"].reshape(2, 1, LANES),
        w=sm["sgu_w"], w_t=jnp.swapaxes(sm["sgu_w"], 1, 2),
        b_cols=sm["sgu_b"].reshape(2, 2, SGU_CHUNK).transpose(0, 2, 1))


def _cf_rows(cf):
    return cf[:, :FOX_HEADS].T.reshape(FOX_W // LANES, 2, cf.shape[0])


def _layer_fwd(x_in, prev, mod, wts, sm, l, gathers=(), late_weights=None):
    sh1, sc1, g1, sh2, sc2, g2 = mod
    v = _small_views(sm)
    if prev is None:
        x0 = x_in
        h1 = _norm_mod_fwd(x0, v["n1"], sc1, sh1, name=f"l{l}_norm1")
    else:
        x0, h1 = _resid_norm_mod_fwd(x_in, prev[0], prev[1], v["n1"], sc1, sh1, name=f"l{l}_norm1")
    proj, proj_bf = _matmul(h1, wts["w_in"], name=f"l{l}_proj", copy=BF16)
    o_sb, sb_ltot, sb_stop = _sb_fwd(proj_bf, name=f"l{l}_sb_fwd")
    qn, kn, kmax = _fox_prep_fwd(proj, v["qg"], v["kg"], name=f"l{l}_fox_prep")
    cf = _forget_cumsum_fwd(proj, v["b_pad"], name=f"l{l}_cumf")
    cfr = _cf_rows(cf)
    o_fox, lse, *gathered = _fox_fwd(proj_bf, qn, kn, cf, cfr, kmax, name=f"l{l}_fox_fwd",
                                     gathers=gathers)
    if gathers:
        late_weights(gathered)
    o_sgu = _sgu_fwd(proj, v["w"], v["b_cols"], v["gn"], name=f"l{l}_sgu_fwd")
    mixed = jnp.concatenate([o_sb, o_fox, o_sgu], axis=1).astype(BF16)
    mo = _matmul(mixed, wts["w_out"], name=f"l{l}_wout")
    x1, h2 = _resid_norm_mod_fwd(x0, mo, g1, v["n2"], sc2, sh2, name=f"l{l}_norm2")
    a, rr = _matmul(h2, wts["w1"], name=f"l{l}_mlp1", out_dtype=BF16, relu2=BF16)
    m2 = _matmul(rr, wts["w2"], name=f"l{l}_mlp2")
    saved = dict(x0=x0, h1=h1, proj=proj, proj_bf=proj_bf, sb_ltot=sb_ltot, sb_stop=sb_stop, qn=qn, kn=kn, kmax=kmax, cf=cf, cfr=cfr, o_fox=o_fox,
                 lse=lse, mixed=mixed, mo=mo, x1=x1, h2=h2, a=a, rr=rr, m2=m2)
    return saved


def _layer_bwd(dx2, dm2, dg2, sv, mod, wts, sm, l, below):
    sh1, sc1, g1, sh2, sc2, g2 = mod
    v = _small_views(sm)
    dw2 = _matmul(sv["rr"], dm2, ta=True, name=f"l{l}_dw2")
    da = _matmul(dm2, wts["w2"], tb=True, name=f"l{l}_da", out_dtype=BF16, pre_act=sv["a"])
    dw1 = _matmul(sv["h2"], da, ta=True, name=f"l{l}_dw1")
    dh2 = _matmul(da, wts["w1"], tb=True, name=f"l{l}_dh2")
    dx1, dn2, dsc2, dsh2, dmo, dg1 = _norm_mod_bwd(sv["x1"], dh2, dx2, v["n2"], sc2,
                                                    (sv["mo"], g1), name=f"l{l}_norm2_bwd")
    dwo = _matmul(sv["mixed"], dmo, ta=True, name=f"l{l}_dwout")
    dmixed = _matmul(dmo, wts["w_out"], tb=True, name=f"l{l}_dmixed")
    proj = sv["proj"]
    dqa, dka, dva = _sb_bwd(sv["proj_bf"], dmixed, sv["sb_ltot"], sv["sb_stop"], name=f"l{l}_sb_bwd")
    dqn, dkn, dvb, dcfr, dcfq = _fox_bwd(sv["proj_bf"], sv["qn"], sv["kn"], sv["cf"], sv["cfr"], sv["kmax"], dmixed,
                                   sv["o_fox"], sv["lse"], name=f"l{l}_fox_bwd")
    dqb, dkb, dqg, dkg = _fox_prep_bwd(proj, dqn, dkn, v["qg"], v["kg"], name=f"l{l}_fox_prep_bwd")
    s = proj.shape[0]
    dcf_heads = dcfr.reshape(FOX_HEADS, s).T + dcfq.reshape(s, FOX_HEADS, HEAD_DIM)[:, :, 0]
    dcf = jnp.zeros((s, LANES), F32).at[:, :FOX_HEADS].set(dcf_heads)
    dfl, dbf = _forget_cumsum_bwd(proj, v["b_pad"], dcf, name=f"l{l}_cumf_bwd")
    duc, dvc, dsw, dsb_cols, dgn = _sgu_bwd(proj, dmixed, v["w"], v["w_t"], v["b_cols"], v["gn"],
                                            name=f"l{l}_sgu_bwd")
    dproj = jnp.concatenate([dqa, dka, dva, dqb, dkb, dvb, duc, dvc, dfl,
                             jnp.zeros((s, LANES), F32)], axis=1).astype(BF16)
    dwin = _matmul(sv["h1"], dproj, ta=True, name=f"l{l}_dwin")
    dh1 = _matmul(dproj, wts["w_in"], tb=True, name=f"l{l}_dh1")
    dx0, dn1, dsc1, dsh1, dm_below, dg_below = _norm_mod_bwd(sv["x0"], dh1, dx1, v["n1"], sc1, below,
                                                             name=f"l{l}_norm1_bwd")
    big = dict(w_in=dwin, w_out=dwo, w1=dw1, w2=dw2)
    small = dict(norm1_g=dn1[0], norm2_g=dn2[0], b_forget=dbf[0, :FOX_HEADS],
                 q_norm_g=dqg[0, :HEAD_DIM] + dqg[0, HEAD_DIM:],
                 k_norm_g=dkg[0, :HEAD_DIM] + dkg[0, HEAD_DIM:],
                 sgu_norm_g=dgn.reshape(4, HEAD_DIM), sgu_w=dsw,
                 sgu_b=dsb_cols.transpose(0, 2, 1).reshape(4, SGU_CHUNK))
    dmod = jnp.concatenate([dsh1, dsc1, dg1, dsh2, dsc2, dg2], axis=1)
    return dx0, dm_below, dg_below, big, small, dmod


def _w_in_to_internal(w):
    pad = jnp.zeros((w.shape[0], PROJ_W - IN_W), w.dtype)
    return jnp.concatenate([w[:, :ATT_W], w[:, ATT_W + FOX_HEADS:], w[:, ATT_W:ATT_W + FOX_HEADS],
                            pad], axis=1)


def _w_in_from_internal(g):
    n_gate = SGU_W * 2
    return jnp.concatenate([g[:, :ATT_W], g[:, ATT_W + n_gate:ATT_W + n_gate + FOX_HEADS],
                            g[:, ATT_W:ATT_W + n_gate]], axis=1)


def _exchange(x, masks, slot_shift, slot_bits, scatter, *, name):
    n_slots = 2 ** slot_bits
    blk_shape = x.shape[1:] if scatter else x.shape
    n_peers = len(masks)

    def body(x_ref, out_ref, send_sems, recv_sems, local_sem):
        ids = (lax.axis_index("x"), lax.axis_index("y"), lax.axis_index("c"))
        me = 4 * ids[0] + 2 * ids[1] + ids[2]
        my_slot = (me >> slot_shift) & (n_slots - 1)

        def peer(mask):
            return tuple(1 - v if (mask >> b) & 1 else v for v, b in zip(ids, (2, 1, 0)))

        def src_for(slot):
            return x_ref.at[slot] if scatter else x_ref

        copies = [pltpu.make_async_copy(src_for(my_slot), out_ref.at[my_slot], local_sem)]
        for kk, mask in enumerate(masks):
            peer_slot = ((me ^ mask) >> slot_shift) & (n_slots - 1)
            copies.append(pltpu.make_async_remote_copy(
                src_ref=src_for(peer_slot), dst_ref=out_ref.at[my_slot],
                send_sem=send_sems.at[kk], recv_sem=recv_sems.at[kk],
                device_id=peer(mask), device_id_type=MESH))
        for cp in copies:
            cp.start()
        for cp in copies:
            cp.wait()

    any_spec = pl.BlockSpec(memory_space=pl.ANY)
    return _pcall(body, name=name, in_specs=[any_spec], out_specs=any_spec,
                  out_shape=jax.ShapeDtypeStruct((n_slots,) + tuple(blk_shape), x.dtype),
                  scratch_shapes=[pltpu.SemaphoreType.DMA((n_peers,)),
                                  pltpu.SemaphoreType.DMA((n_peers,)),
                                  pltpu.SemaphoreType.DMA(())])(x)


CORE_PIECE_BYTES = 12 * 2 ** 20
CORE_DMA_CHUNKS = 4


def _core_swap_piece(x, *, name):
    rows, cols = x.shape
    n_ch = CORE_DMA_CHUNKS if rows % (16 * CORE_DMA_CHUNKS) == 0 else 1
    rc = rows // n_ch

    def body(x_ref, out_ref, send_sems, recv_sems):
        sibling = (lax.axis_index("x"), lax.axis_index("y"), 1 - lax.axis_index("c"))
        copies = [pltpu.make_async_remote_copy(
            src_ref=x_ref.at[pl.ds(ch * rc, rc)], dst_ref=out_ref.at[pl.ds(ch * rc, rc)],
            send_sem=send_sems.at[ch], recv_sem=recv_sems.at[ch],
            device_id=sibling, device_id_type=MESH) for ch in range(n_ch)]
        for cp in copies:
            cp.start()
        for cp in copies:
            cp.wait()

    vmem = pl.BlockSpec(memory_space=pltpu.VMEM)
    return _pcall(body, name=name, in_specs=[vmem], out_specs=vmem,
                  out_shape=jax.ShapeDtypeStruct(x.shape, x.dtype),
                  scratch_shapes=[pltpu.SemaphoreType.DMA((n_ch,)),
                                  pltpu.SemaphoreType.DMA((n_ch,))])(x)


def _core_swap(x, *, name):
    rows, cols = x.shape
    n = 1
    while (rows % n or (rows // n) % 16 or
           (rows // n) * (-(-cols // LANES) * LANES) * x.dtype.itemsize > CORE_PIECE_BYTES):
        n += 1
    pr = rows // n
    pieces = [_core_swap_piece(x[kk * pr:(kk + 1) * pr], name=f"{name}_{kk}") for kk in range(n)]
    return pieces[0] if n == 1 else jnp.concatenate(pieces, axis=0)


def _by_core(core, mine, theirs, axis):
    return jnp.where(core == 0, jnp.concatenate([mine, theirs], axis=axis),
                     jnp.concatenate([theirs, mine], axis=axis))


def _gather_chips(x, *, name):
    return _exchange(x, (2, 4, 6), 1, 2, False, name=name)


def _gather_all(x, *, name):
    return _exchange(x, (1, 2, 3, 4, 5, 6, 7), 0, 3, False, name=name)


def _scatter_chips(x4, *, name):
    return _exchange(x4, (2, 4, 6), 1, 2, True, name=name)


def _sum_slots(parts, *, name, out_dtype=F32, tr=256):
    n, rows, cols = parts.shape
    tr = min(tr, rows)
    assert rows % tr == 0, (name, rows, tr)

    def body(p_ref, o_ref):
        acc = p_ref[0].astype(F32)
        for kk in range(1, n):
            acc = acc + p_ref[kk].astype(F32)
        o_ref[...] = acc.astype(o_ref.dtype)

    return _pcall(body, name=name, grid=(rows // tr,),
                  in_specs=[pl.BlockSpec((n, tr, cols), lambda i: (0, i, 0))],
                  out_specs=pl.BlockSpec((tr, cols), lambda i: (i, 0)),
                  out_shape=jax.ShapeDtypeStruct((rows, cols), out_dtype),
                  semantics=("parallel",))(parts)


def _add2(a, b, *, name, out_dtype, tr=512):
    def fn(f, v):
        return [f[0] + f[1]], []
    (out,), _ = _rowwise(fn, [a, b], [], [out_dtype], 0, name=name, tr=tr)
    return out


def _adamw(w, m, v, parts, *, name, tr=256):
    n, rows, cols = parts.shape
    tr = min(tr, rows)
    assert rows % tr == 0, (name, rows, tr)
    c1 = 1.0 - ADAM_B1 ** ADAM_STEP
    c2 = 1.0 - ADAM_B2 ** ADAM_STEP

    def body(w_ref, m_ref, v_ref, p_ref, g_ref, d_ref, nm_ref, nv_ref):
        g = p_ref[0]
        for kk in range(1, n):
            g = g + p_ref[kk]
        nm = ADAM_B1 * m_ref[...] + (1.0 - ADAM_B1) * g
        nv = ADAM_B2 * v_ref[...] + (1.0 - ADAM_B2) * (g * g)
        g_ref[...] = g
        nm_ref[...] = nm
        nv_ref[...] = nv
        d_ref[...] = -ADAM_LR * ((nm / c1) / (jnp.sqrt(nv / c2) + ADAM_EPS) + ADAM_WD * w_ref[...])

    spec = pl.BlockSpec((tr, cols), lambda i: (i, 0))
    return _pcall(body, name=name, grid=(rows // tr,),
                  in_specs=[spec, spec, spec, pl.BlockSpec((n, tr, cols), lambda i: (0, i, 0))],
                  out_specs=[spec] * 4,
                  out_shape=[jax.ShapeDtypeStruct((rows, cols), F32)] * 4,
                  semantics=("parallel",))(w, m, v, parts)


def _silu(c):
    return c / (1.0 + jnp.exp(-c))


def _ada_fwd(c_all, ada_w, ada_b_sh, *, name):
    nl, d, wsh = ada_w.shape

    def body(c_ref, w_ref, b_ref, o_ref):
        cond = _silu(c_ref[...]).astype(BF16)
        o_ref[0] = _dot(cond, w_ref[0].astype(BF16)) + b_ref[0]

    return _pcall(body, name=name, grid=(nl,),
                  in_specs=[pl.BlockSpec(c_all.shape, lambda l: (0, 0)),
                            pl.BlockSpec((1, d, wsh), lambda l: (l, 0, 0)),
                            pl.BlockSpec((1, 1, wsh), lambda l: (l, 0, 0))],
                  out_specs=pl.BlockSpec((1, c_all.shape[0], wsh), lambda l: (l, 0, 0)),
                  out_shape=jax.ShapeDtypeStruct((nl, c_all.shape[0], wsh), F32),
                  semantics=("parallel",))(c_all, ada_w, ada_b_sh)


def _ada_bwd(c_all, dmod_sh, *, name):
    nl, nb, wsh = dmod_sh.shape
    d = c_all.shape[1]

    def body(c_ref, dm_ref, o_ref):
        cond = _silu(c_ref[...]).astype(BF16)
        o_ref[0] = _dot_tn(cond, dm_ref[0].astype(BF16))

    return _pcall(body, name=name, grid=(nl,),
                  in_specs=[pl.BlockSpec(c_all.shape, lambda l: (0, 0)),
                            pl.BlockSpec((1, nb, wsh), lambda l: (l, 0, 0))],
                  out_specs=pl.BlockSpec((1, d, wsh), lambda l: (l, 0, 0)),
                  out_shape=jax.ShapeDtypeStruct((nl, d, wsh), F32),
                  semantics=("parallel",))(c_all, dmod_sh)


SMALL_NAMES = ("norm1_g", "norm2_g", "b_forget", "q_norm_g", "k_norm_g", "sgu_norm_g", "sgu_w",
               "sgu_b")
WEIGHT_NAMES = ("ada_w", "ada_b", "norm1_g", "norm2_g", "w_in", "b_forget", "q_norm_g", "k_norm_g",
                "sgu_norm_g", "sgu_w", "sgu_b", "w_out", "mlp_w1", "mlp_w2")


SMALL_TILE_ROWS = 256


def _pack_small(tree):
    flat = jnp.concatenate([tree[n].reshape(-1) for n in SMALL_NAMES])
    n = flat.shape[0]
    rows = -(-n // (SMALL_TILE_ROWS * LANES)) * SMALL_TILE_ROWS
    return jnp.zeros((rows * LANES,), F32).at[:n].set(flat).reshape(rows, LANES)


def _unpack_small(packed, like):
    flat = packed.reshape(-1)
    out, off = {}, 0
    for n in SMALL_NAMES:
        size = like[n].size
        out[n] = flat[off:off + size].reshape(like[n].shape)
        off += size
    return out


def kernel(x, c, ada_w, ada_b, norm1_g, norm2_g, w_in, b_forget, q_norm_g, k_norm_g, sgu_norm_g, sgu_w, sgu_b, w_out, mlp_w1, mlp_w2, loss_target, m_ada_w, m_ada_b, m_norm1_g, m_norm2_g, m_w_in, m_b_forget, m_q_norm_g, m_k_norm_g, m_sgu_norm_g, m_sgu_w, m_sgu_b, m_w_out, m_mlp_w1, m_mlp_w2, v_ada_w, v_ada_b, v_norm1_g, v_norm2_g, v_w_in, v_b_forget, v_q_norm_g, v_k_norm_g, v_sgu_norm_g, v_sgu_w, v_sgu_b, v_w_out, v_mlp_w1, v_mlp_w2):
    w = dict(ada_w=ada_w, ada_b=ada_b, norm1_g=norm1_g, norm2_g=norm2_g, w_in=w_in,
             b_forget=b_forget, q_norm_g=q_norm_g, k_norm_g=k_norm_g, sgu_norm_g=sgu_norm_g,
             sgu_w=sgu_w, sgu_b=sgu_b, w_out=w_out, mlp_w1=mlp_w1, mlp_w2=mlp_w2)
    mom = dict(ada_w=m_ada_w, ada_b=m_ada_b, norm1_g=m_norm1_g, norm2_g=m_norm2_g, w_in=m_w_in,
               b_forget=m_b_forget, q_norm_g=m_q_norm_g, k_norm_g=m_k_norm_g,
               sgu_norm_g=m_sgu_norm_g, sgu_w=m_sgu_w, sgu_b=m_sgu_b, w_out=m_w_out,
               mlp_w1=m_mlp_w1, mlp_w2=m_mlp_w2)
    var = dict(ada_w=v_ada_w, ada_b=v_ada_b, norm1_g=v_norm1_g, norm2_g=v_norm2_g, w_in=v_w_in,
               b_forget=v_b_forget, q_norm_g=v_q_norm_g, k_norm_g=v_k_norm_g,
               sgu_norm_g=v_sgu_norm_g, sgu_w=v_sgu_w, sgu_b=v_sgu_b, w_out=v_w_out,
               mlp_w1=v_mlp_w1, mlp_w2=v_mlp_w2)
    depth, d = norm1_g.shape
    chip = 2 * lax.axis_index("x") + lax.axis_index("y")
    me = 2 * chip + lax.axis_index("c")
    n_chips = 4
    ada_sh = ada_w.shape[2]

    core = lax.axis_index("c")
    half_l = depth // 2

    def my_part(w_sh):
        _, r, cols = w_sh.shape
        mine = lax.dynamic_slice_in_dim(w_sh, core * half_l, half_l, axis=0).astype(BF16)
        return mine.reshape(half_l * r, cols)

    def share(got, w_sh, name):
        _, r, cols = w_sh.shape
        theirs = _core_swap(got.reshape(n_chips * half_l * r, cols), name=f"share_{name}")
        return _by_core(core, got.reshape(n_chips, half_l, r, cols),
                        theirs.reshape(n_chips, half_l, r, cols), 1)

    g_in = share(_gather_chips(my_part(w_in), name="gather_w_in"), w_in, "w_in")
    layer_w = [dict(w_in=_w_in_to_internal(
        jnp.concatenate([g_in[k, l] for k in range(n_chips)], axis=1))) for l in range(depth)]
    later = (("w_out", w_out), ("w1", mlp_w1), ("w2", mlp_w2))

    def late_weights(gathered):
        g_out, g_w1, g_w2 = [share(got, w_sh, name) for got, (name, w_sh) in zip(gathered, later)]
        for l in range(depth):
            layer_w[l].update(
                w_out=g_out[:, l].reshape(d, d),
                w1=jnp.concatenate([g_w1[k, l] for k in range(n_chips)], axis=1),
                w2=g_w2[:, l].reshape(D_FF, d))

    c_all = _gather_all(jnp.zeros((8, d), F32).at[0].set(c[0]), name="gather_c")[:, 0]
    c_pad = jnp.concatenate([c_all, jnp.zeros_like(c_all)], axis=0)
    ada_b_sh = lax.dynamic_slice_in_dim(ada_b, chip * ada_sh, ada_sh, axis=1)[:, None, :]
    mod_sh = _ada_fwd(c_pad, ada_w, ada_b_sh, name="ada_fwd")
    mod_all = _gather_chips(mod_sh, name="gather_mod")
    mod_me = lax.dynamic_index_in_dim(mod_all, me, axis=2, keepdims=False)
    mod_me = mod_me.transpose(1, 0, 2).reshape(depth, 6, 1, d)

    saved = []
    xs, prev = x[0], None
    for l in range(depth):
        mod = [mod_me[l, kk] for kk in range(6)]
        sm = {n: w[n][l] for n in SMALL_NAMES}
        first = dict(gathers=[my_part(w_sh) for _, w_sh in later], late_weights=late_weights)
        sv = _layer_fwd(xs, prev, mod, layer_w[l], sm, l, **(first if l == 0 else {}))
        saved.append(sv)
        xs, prev = sv["x1"], (sv["m2"], mod[5])

    sq, dxs, dm2, dg2 = _loss_fwd_bwd(xs, prev[0], prev[1], loss_target[0], name="loss")
    loss = lax.psum(0.5 * jnp.sum(sq) / d, ("x", "y", "c"))

    big = {n: [] for n in ("w_in", "w_out", "w1", "w2")}
    small = {n: [] for n in SMALL_NAMES}
    dmods = []
    for l in reversed(range(depth)):
        mod = [mod_me[l, kk] for kk in range(6)]
        sm = {n: w[n][l] for n in SMALL_NAMES}
        below = (saved[l - 1]["m2"], mod_me[l - 1, 5]) if l else None
        dxs, dm2, dg2, bg, smg, dmod = _layer_bwd(dxs, dm2, dg2, saved[l], mod, layer_w[l], sm, l,
                                                  below)
        for n in big:
            big[n].insert(0, bg[n])
        for n in SMALL_NAMES:
            small[n].insert(0, smg[n])
        dmods.insert(0, dmod)
    grad_x = dxs[None]

    out_g, out_d, out_m, out_v = {}, {}, {}, {}

    def run_adamw(name, parts2d, shape):
        rows, cols = parts2d.shape[1:]
        g, dl, nm, nv = _adamw(w[name].reshape(rows, cols), mom[name].reshape(rows, cols),
                               var[name].reshape(rows, cols), parts2d, name=f"adamw_{name}")
        out_g[name], out_d[name] = g.reshape(shape), dl.reshape(shape)
        out_m[name], out_v[name] = nm.reshape(shape), nv.reshape(shape)

    def shards_of(name, l):
        if name == "w_in":
            g = _w_in_from_internal(big["w_in"][l])
            return jnp.stack(jnp.split(g, n_chips, axis=1))
        if name == "mlp_w1":
            return jnp.stack(jnp.split(big["w1"][l], n_chips, axis=1))
        if name == "w_out":
            return big["w_out"][l].reshape(n_chips, d // n_chips, d)
        return big["w2"][l].reshape(n_chips, D_FF // n_chips, d)

    for name in ("w_in", "w_out", "mlp_w1", "mlp_w2"):
        per_chip = jnp.stack([shards_of(name, l) for l in range(depth)], axis=1)
        r, cols = per_chip.shape[2:]
        half_rows = half_l * r
        keep = lax.dynamic_slice_in_dim(per_chip, core * half_l, half_l, axis=1)
        send = lax.dynamic_slice_in_dim(per_chip, (1 - core) * half_l, half_l, axis=1).astype(BF16)
        theirs = _core_swap(send.reshape(n_chips * half_rows, cols), name=f"pair_{name}")
        chip_sum = _add2(keep.reshape(n_chips * half_rows, cols), theirs, out_dtype=BF16,
                         name=f"pairsum_{name}")
        got = _scatter_chips(chip_sum.reshape(n_chips, half_rows, cols), name=f"scatter_{name}")
        half = _sum_slots(got, name=f"sum_{name}")
        both = _by_core(core, half, _core_swap(half, name=f"swap_{name}"), 0)
        run_adamw(name, both[None], w[name].shape)

    small_tree = {n: jnp.stack(small[n]) for n in SMALL_NAMES}
    gathered = _gather_all(_pack_small(small_tree), name="gather_small")
    gs, ds_, ms, vs = _adamw(_pack_small({n: w[n] for n in SMALL_NAMES}),
                             _pack_small({n: mom[n] for n in SMALL_NAMES}),
                             _pack_small({n: var[n] for n in SMALL_NAMES}), gathered,
                             name="adamw_small")
    like = {n: w[n] for n in SMALL_NAMES}
    for tree, packed in ((out_g, gs), (out_d, ds_), (out_m, ms), (out_v, vs)):
        tree.update(_unpack_small(packed, like))

    dmod_mine = jnp.concatenate(dmods, axis=0)
    dmod_all = _gather_all(jnp.zeros((depth, 8, 6 * d), F32).at[:, 0].set(dmod_mine),
                           name="gather_dmod")[:, :, 0]
    dmod_lb = dmod_all.transpose(1, 0, 2)
    dmod_sh = lax.dynamic_slice_in_dim(dmod_lb, chip * ada_sh, ada_sh, axis=2)
    dmod_sh = jnp.concatenate([dmod_sh, jnp.zeros_like(dmod_sh)], axis=1)
    g_ada_w = _ada_bwd(c_pad, dmod_sh, name="ada_bwd")
    run_adamw("ada_w", g_ada_w.reshape(1, depth * d, ada_sh), ada_w.shape)
    parts_b = dmod_all.reshape(8, depth * 6 * d // LANES, LANES)
    run_adamw("ada_b", parts_b, ada_b.shape)

    outs = [loss, grad_x]
    for tree in (out_g, out_d, out_m, out_v):
        outs += [tree[n] for n in WEIGHT_NAMES]
    return tuple(outs)
```
